```python
import math
import jax, jax.numpy as jnp
from jax import lax
import numpy as np

D_MODEL = 2048
BATCH = 8
SEQ = 8192
DEPTH = 2

GDN_HEADS = 8
GDN_DK = 128
GDN_DV = 128
RET_HEADS = 8
RET_DK = 128
RET_DV = 128
CONV_WIDTH = 4
LA_CHUNK = 64
ROPE_BASE = 10000.0
GDN_QK = GDN_HEADS * GDN_DK
GDN_V = GDN_HEADS * GDN_DV
RET_QK = RET_HEADS * RET_DK
RET_V = RET_HEADS * RET_DV
MIX_IN = 2 * GDN_QK + 2 * GDN_V + 2 * GDN_HEADS + 2 * RET_QK + 2 * RET_V
MIX_OUT = GDN_V + RET_V
SG_CHUNK = 128
SG_GROUPS = 8
SG_WIDTH = 2 * D_MODEL
SG_GROUP_DIM = SG_WIDTH // SG_GROUPS
FFN_HIDDEN = 4 * D_MODEL
EPS = 1e-6
N_EVEN = (DEPTH + 1) // 2
N_ODD = DEPTH // 2

kernel_name = "hybrid_gdn_retention_gmlp_block"


def rmsnorm(x, w):
    xf = x.astype(jnp.float32)
    y = xf * lax.rsqrt(jnp.mean(jnp.square(xf), axis=-1, keepdims=True) + EPS)
    return (y * w.astype(jnp.float32)).astype(x.dtype)


def head_rmsnorm(x):
    return x * lax.rsqrt(jnp.mean(jnp.square(x), axis=-1, keepdims=True) + EPS)


def layernorm(x, w, b):
    xf = x.astype(jnp.float32)
    mu = jnp.mean(xf, axis=-1, keepdims=True)
    xc = xf - mu
    var = jnp.mean(jnp.square(xc), axis=-1, keepdims=True)
    y = xc * lax.rsqrt(var + EPS) * w.astype(jnp.float32) + b.astype(jnp.float32)
    return y.astype(x.dtype)


def l2norm(x):
    return x * lax.rsqrt(jnp.sum(jnp.square(x), axis=-1, keepdims=True) + 1e-6)


def causal_conv(x, w):
    k_w = w.shape[-1]
    kern = jnp.transpose(w)[:, None, :].astype(x.dtype)
    return lax.conv_general_dilated(x, kern, window_strides=(1,), padding=[(k_w - 1, 0)],
                                    dimension_numbers=('NWC', 'WIO', 'NWC'),
                                    feature_group_count=x.shape[-1])


def rotary(x, pos):
    half = x.shape[-1] // 2
    inv_freq = 1.0 / (ROPE_BASE ** jnp.linspace(0.0, 1.0, half, dtype=jnp.float32))
    ang = pos[:, None] * inv_freq[None, :]
    cos = jnp.cos(ang)[None, :, None, :]
    sin = jnp.sin(ang)[None, :, None, :]
    x1, x2 = x[..., :half], x[..., half:]
    return jnp.concatenate([x1 * cos - x2 * sin, x2 * cos + x1 * sin], axis=-1)


def to_chunks(t, c):
    b_, l_ = t.shape[:2]
    t = t.reshape(b_, l_ // c, c, *t.shape[2:])
    return jnp.swapaxes(t, 2, 3)


def from_chunks(t):
    t = jnp.swapaxes(t, 2, 3)
    return t.reshape(t.shape[0], -1, *t.shape[3:])


def gated_delta_rule(q, k, v, beta, g):
    c = LA_CHUNK
    dk = q.shape[-1]
    dv = v.shape[-1]
    q, k, v, beta, g = (to_chunks(t, c) for t in (q * dk ** -0.5, k, v, beta, g))
    g = jnp.cumsum(g, axis=-1)
    causal = jnp.tril(jnp.ones((c, c), dtype=bool))
    strict = jnp.tril(jnp.ones((c, c), dtype=bool), k=-1)
    decay = jnp.exp(jnp.where(causal, g[..., :, None] - g[..., None, :], -jnp.inf))
    kb = k * beta[..., None]
    a = jnp.where(strict, jnp.einsum('bnhtk,bnhsk->bnhts', kb, k) * decay, 0.0)
    eye = jnp.eye(c, dtype=q.dtype)
    t_inv = lax.linalg.triangular_solve(a + eye, jnp.broadcast_to(eye, a.shape), left_side=True,
                                        lower=True, unit_diagonal=True)
    u = jnp.einsum('bnhts,bnhsv->bnhtv', t_inv, v * beta[..., None])
    w = jnp.einsum('bnhts,bnhsk->bnhtk', t_inv, kb * jnp.exp(g)[..., None])
    qk = jnp.where(causal, jnp.einsum('bnhtk,bnhsk->bnhts', q, k) * decay, 0.0)
    q_dec = q * jnp.exp(g)[..., None]
    k_tail = k * jnp.exp(g[..., -1:] - g)[..., None]
    chunk_decay = jnp.exp(g[..., -1])

    def step(state, xs):
        q_c, k_c, u_c, w_c, qk_c, d_c = xs
        v_new = u_c - jnp.einsum('bhtk,bhkv->bhtv', w_c, state)
        o = jnp.einsum('bhtk,bhkv->bhtv', q_c, state) + jnp.einsum('bhts,bhsv->bhtv', qk_c, v_new)
        state = state * d_c[..., None, None] + jnp.einsum('bhtk,bhtv->bhkv', k_c, v_new)
        return state, o

    b_, _, h_ = g.shape[:3]
    state0 = jnp.zeros((b_, h_, dk, dv), q.dtype)
    xs = tuple(jnp.moveaxis(t, 1, 0) for t in (q_dec, k_tail, u, w, qk, chunk_decay))
    _, o = lax.scan(step, state0, xs)
    return from_chunks(jnp.moveaxis(o, 0, 1))


def retention(q, k, v, log_gamma):
    c = LA_CHUNK
    dk = q.shape[-1]
    dv = v.shape[-1]
    q, k, v = (to_chunks(t, c) for t in (q, k, v))
    pos = jnp.arange(c, dtype=jnp.float32)
    causal = jnp.tril(jnp.ones((c, c), dtype=bool))
    lg = log_gamma[:, None]
    d_mat = jnp.exp(jnp.where(causal, (pos[:, None] - pos[None, :]) * log_gamma[:, None, None], -jnp.inf))
    inner = jnp.einsum('bnhts,bnhsv->bnhtv', jnp.einsum('bnhtk,bnhsk->bnhts', q, k) * d_mat, v)
    q_dec = q * jnp.exp((pos + 1.0) * lg)[..., None]
    k_dec = k * jnp.exp((c - 1.0 - pos) * lg)[..., None]
    chunk_decay = jnp.exp(c * log_gamma)[:, None, None]

    def step(state, xs):
        q_c, k_c, v_c = xs
        o = jnp.einsum('bhtk,bhkv->bhtv', q_c, state)
        state = state * chunk_decay + jnp.einsum('bhtk,bhtv->bhkv', k_c, v_c)
        return state, o

    state0 = jnp.zeros((q.shape[0], q.shape[2], dk, dv), q.dtype)
    xs = tuple(jnp.moveaxis(t, 1, 0) for t in (q_dec, k_dec, v))
    _, cross = lax.scan(step, state0, xs)
    return from_chunks(inner + jnp.moveaxis(cross, 0, 1))


def linear_attention_mixer(h, w_in, conv_w, a_log, dt_bias, out_norm_w, w_out):
    f32 = jnp.float32
    b_, l_, _ = h.shape
    proj = h @ w_in
    sizes = [GDN_QK, GDN_QK, GDN_V, GDN_V, GDN_HEADS, GDN_HEADS, RET_QK, RET_QK, RET_V, RET_V]
    cuts = [int(s) for s in np.cumsum(sizes)[:-1]]
    gq, gk, gv, gz, gb, ga, rq, rk, rv, rg = jnp.split(proj, cuts, axis=-1)

    qkv = jax.nn.silu(causal_conv(jnp.concatenate([gq, gk, gv], axis=-1), conv_w))
    gq, gk, gv = jnp.split(qkv, [GDN_QK, 2 * GDN_QK], axis=-1)
    q = l2norm(gq.astype(f32).reshape(b_, l_, GDN_HEADS, GDN_DK))
    k = l2norm(gk.astype(f32).reshape(b_, l_, GDN_HEADS, GDN_DK))
    v = gv.astype(f32).reshape(b_, l_, GDN_HEADS, GDN_DV)
    beta = jax.nn.sigmoid(gb.astype(f32))
    g = -jnp.exp(a_log.astype(f32)) * jax.nn.softplus(ga.astype(f32) + dt_bias.astype(f32))
    o_a = gated_delta_rule(q, k, v, beta, g)
    z = gz.astype(f32).reshape(b_, l_, GDN_HEADS, GDN_DV)
    o_a = head_rmsnorm(o_a) * out_norm_w.astype(f32) * jax.nn.silu(z)
    o_a = o_a.reshape(b_, l_, GDN_V)

    pos = jnp.arange(l_, dtype=f32)
    log_gamma = jnp.log1p(-jnp.power(2.0, -5.0 - jnp.arange(RET_HEADS, dtype=f32)))
    q = rotary(rq.astype(f32).reshape(b_, l_, RET_HEADS, RET_DK), pos)
    k = rotary(rk.astype(f32).reshape(b_, l_, RET_HEADS, RET_DK), pos) * RET_DK ** -0.5
    v = rv.astype(f32).reshape(b_, l_, RET_HEADS, RET_DV)
    o_b = head_rmsnorm(retention(q, k, v, log_gamma)).reshape(b_, l_, RET_V)
    o_b = jax.nn.silu(rg.astype(f32)) * o_b

    o = jnp.concatenate([o_a, o_b], axis=-1).astype(h.dtype)
    return o @ w_out


def spatial_gating_mixer(h, w_in, ln_w, ln_b, w_s, b_s, w_out):
    b_, l_, _ = h.shape
    proj = jax.nn.gelu(h @ w_in)
    u, v = jnp.split(proj, 2, axis=-1)
    v = layernorm(v, ln_w, ln_b)
    vc = v.reshape(b_, l_ // SG_CHUNK, SG_CHUNK, SG_GROUPS, SG_GROUP_DIM)
    causal = jnp.tril(jnp.ones((SG_CHUNK, SG_CHUNK), dtype=bool))
    ws = jnp.where(causal, w_s, 0.0)
    s = jnp.einsum('gts,bnsgd->bntgd', ws, vc) + jnp.swapaxes(b_s, 0, 1)[:, :, None]
    s = s.reshape(b_, l_, SG_WIDTH)
    return (u * s) @ w_out


def squared_relu_mlp(h, w_up, w_down):
    return jnp.square(jax.nn.relu(h @ w_up)) @ w_down


def _fwd_setup_inputs(seed: int = 0) -> dict:
    key = jax.random.key(seed)
    ks = jax.random.split(key, 16)
    f32 = jnp.float32

    def dense(k, shape, fan_in):
        return jax.random.normal(k, shape, f32) * fan_in ** -0.5

    x = jax.random.normal(ks[0], (BATCH, SEQ, D_MODEL), f32)
    norm_w = 1.0 + 0.1 * jax.random.normal(ks[1], (DEPTH, 4, D_MODEL), f32)
    la_w_in = dense(ks[2], (N_EVEN, D_MODEL, MIX_IN), D_MODEL)
    la_conv_w = dense(ks[3], (N_EVEN, 2 * GDN_QK + GDN_V, CONV_WIDTH), CONV_WIDTH)
    la_a_log = jnp.log(jax.random.uniform(ks[4], (N_EVEN, GDN_HEADS), f32, 1.0, 16.0))
    dt = jnp.exp(jax.random.uniform(ks[5], (N_EVEN, GDN_HEADS), f32) * (math.log(0.1) - math.log(0.001))
                 + math.log(0.001))
    la_dt_bias = dt + jnp.log(-jnp.expm1(-dt))
    la_out_norm_w = 1.0 + 0.1 * jax.random.normal(ks[6], (N_EVEN, GDN_DV), f32)
    la_w_out = dense(ks[7], (N_EVEN, MIX_OUT, D_MODEL), MIX_OUT)
    sg_w_in = dense(ks[8], (N_ODD, D_MODEL, 2 * SG_WIDTH), D_MODEL)
    sg_ln_w = 1.0 + 0.1 * jax.random.normal(ks[9], (N_ODD, SG_WIDTH), f32)
    sg_ln_b = 0.02 * jax.random.normal(ks[10], (N_ODD, SG_WIDTH), f32)
    sg_w_s = dense(ks[11], (N_ODD, SG_GROUPS, SG_CHUNK, SG_CHUNK), SG_CHUNK)
    sg_b_s = 1.0 + 0.1 * jax.random.normal(ks[12], (N_ODD, SG_GROUPS, SG_CHUNK), f32)
    sg_w_out = dense(ks[13], (N_ODD, SG_WIDTH, D_MODEL), SG_WIDTH)
    ffn_w_up = dense(ks[14], (DEPTH, D_MODEL, FFN_HIDDEN), D_MODEL)
    ffn_w_down = dense(ks[15], (DEPTH, FFN_HIDDEN, D_MODEL), FFN_HIDDEN)
    return {"x": x, "norm_w": norm_w, "la_w_in": la_w_in, "la_conv_w": la_conv_w,
            "la_a_log": la_a_log, "la_dt_bias": la_dt_bias, "la_out_norm_w": la_out_norm_w,
            "la_w_out": la_w_out, "sg_w_in": sg_w_in, "sg_ln_w": sg_ln_w, "sg_ln_b": sg_ln_b,
            "sg_w_s": sg_w_s, "sg_b_s": sg_b_s, "sg_w_out": sg_w_out,
            "ffn_w_up": ffn_w_up, "ffn_w_down": ffn_w_down}


def _fwd_reference(x, norm_w, la_w_in, la_conv_w, la_a_log, la_dt_bias, la_out_norm_w, la_w_out,
              sg_w_in, sg_ln_w, sg_ln_b, sg_w_s, sg_b_s, sg_w_out, ffn_w_up, ffn_w_down):
    h = x
    for layer in range(DEPTH):
        i = layer // 2
        y = rmsnorm(h, norm_w[layer, 0])
        if layer % 2 == 0:
            y = linear_attention_mixer(y, la_w_in[i], la_conv_w[i], la_a_log[i], la_dt_bias[i],
                                       la_out_norm_w[i], la_w_out[i])
        else:
            y = spatial_gating_mixer(y, sg_w_in[i], sg_ln_w[i], sg_ln_b[i], sg_w_s[i], sg_b_s[i],
                                     sg_w_out[i])
        h = h + rmsnorm(y, norm_w[layer, 1])
        y = squared_relu_mlp(rmsnorm(h, norm_w[layer, 2]), ffn_w_up[layer], ffn_w_down[layer])
        h = h + rmsnorm(y, norm_w[layer, 3])
    return h


import jax as _jax
import jax.numpy as _jnp

TWIN_FORMAT = 'train_step'
FWD_PARAMS = ['x', 'norm_w', 'la_w_in', 'la_conv_w', 'la_a_log', 'la_dt_bias', 'la_out_norm_w', 'la_w_out', 'sg_w_in', 'sg_ln_w', 'sg_ln_b', 'sg_w_s', 'sg_b_s', 'sg_w_out', 'ffn_w_up', 'ffn_w_down']
TWIN_WEIGHTS = ['norm_w', 'la_w_in', 'la_conv_w', 'la_a_log', 'la_dt_bias', 'la_out_norm_w', 'la_w_out', 'sg_w_in', 'sg_ln_w', 'sg_ln_b', 'sg_w_s', 'sg_b_s', 'sg_w_out', 'ffn_w_up', 'ffn_w_down']
TWIN_DIFF_INPUT = 'x'
TWIN_INPUTS = ['x', 'norm_w', 'la_w_in', 'la_conv_w', 'la_a_log', 'la_dt_bias', 'la_out_norm_w', 'la_w_out', 'sg_w_in', 'sg_ln_w', 'sg_ln_b', 'sg_w_s', 'sg_b_s', 'sg_w_out', 'ffn_w_up', 'ffn_w_down', 'loss_target', 'm_norm_w', 'm_la_w_in', 'm_la_conv_w', 'm_la_a_log', 'm_la_dt_bias', 'm_la_out_norm_w', 'm_la_w_out', 'm_sg_w_in', 'm_sg_ln_w', 'm_sg_ln_b', 'm_sg_w_s', 'm_sg_b_s', 'm_sg_w_out', 'm_ffn_w_up', 'm_ffn_w_down', 'v_norm_w', 'v_la_w_in', 'v_la_conv_w', 'v_la_a_log', 'v_la_dt_bias', 'v_la_out_norm_w', 'v_la_w_out', 'v_sg_w_in', 'v_sg_ln_w', 'v_sg_ln_b', 'v_sg_w_s', 'v_sg_b_s', 'v_sg_w_out', 'v_ffn_w_up', 'v_ffn_w_down']
TWIN_OUTPUTS = ['loss', 'grad_x', 'grad_norm_w', 'grad_la_w_in', 'grad_la_conv_w', 'grad_la_a_log', 'grad_la_dt_bias', 'grad_la_out_norm_w', 'grad_la_w_out', 'grad_sg_w_in', 'grad_sg_ln_w', 'grad_sg_ln_b', 'grad_sg_w_s', 'grad_sg_b_s', 'grad_sg_w_out', 'grad_ffn_w_up', 'grad_ffn_w_down', 'delta_norm_w', 'delta_la_w_in', 'delta_la_conv_w', 'delta_la_a_log', 'delta_la_dt_bias', 'delta_la_out_norm_w', 'delta_la_w_out', 'delta_sg_w_in', 'delta_sg_ln_w', 'delta_sg_ln_b', 'delta_sg_w_s', 'delta_sg_b_s', 'delta_sg_w_out', 'delta_ffn_w_up', 'delta_ffn_w_down', 'new_m_norm_w', 'new_m_la_w_in', 'new_m_la_conv_w', 'new_m_la_a_log', 'new_m_la_dt_bias', 'new_m_la_out_norm_w', 'new_m_la_w_out', 'new_m_sg_w_in', 'new_m_sg_ln_w', 'new_m_sg_ln_b', 'new_m_sg_w_s', 'new_m_sg_b_s', 'new_m_sg_w_out', 'new_m_ffn_w_up', 'new_m_ffn_w_down', 'new_v_norm_w', 'new_v_la_w_in', 'new_v_la_conv_w', 'new_v_la_a_log', 'new_v_la_dt_bias', 'new_v_la_out_norm_w', 'new_v_la_w_out', 'new_v_sg_w_in', 'new_v_sg_ln_w', 'new_v_sg_ln_b', 'new_v_sg_w_s', 'new_v_sg_b_s', 'new_v_sg_w_out', 'new_v_ffn_w_up', 'new_v_ffn_w_down']
TWIN_LEAF_KINDS = {'loss': 'loss', 'grad_x': 'grad_x', 'grad_norm_w': 'grad_w', 'grad_la_w_in': 'grad_w', 'grad_la_conv_w': 'grad_w', 'grad_la_a_log': 'grad_w', 'grad_la_dt_bias': 'grad_w', 'grad_la_out_norm_w': 'grad_w', 'grad_la_w_out': 'grad_w', 'grad_sg_w_in': 'grad_w', 'grad_sg_ln_w': 'grad_w', 'grad_sg_ln_b': 'grad_w', 'grad_sg_w_s': 'grad_w', 'grad_sg_b_s': 'grad_w', 'grad_sg_w_out': 'grad_w', 'grad_ffn_w_up': 'grad_w', 'grad_ffn_w_down': 'grad_w', 'delta_norm_w': 'delta_w', 'delta_la_w_in': 'delta_w', 'delta_la_conv_w': 'delta_w', 'delta_la_a_log': 'delta_w', 'delta_la_dt_bias': 'delta_w', 'delta_la_out_norm_w': 'delta_w', 'delta_la_w_out': 'delta_w', 'delta_sg_w_in': 'delta_w', 'delta_sg_ln_w': 'delta_w', 'delta_sg_ln_b': 'delta_w', 'delta_sg_w_s': 'delta_w', 'delta_sg_b_s': 'delta_w', 'delta_sg_w_out': 'delta_w', 'delta_ffn_w_up': 'delta_w', 'delta_ffn_w_down': 'delta_w', 'new_m_norm_w': 'new_m', 'new_m_la_w_in': 'new_m', 'new_m_la_conv_w': 'new_m', 'new_m_la_a_log': 'new_m', 'new_m_la_dt_bias': 'new_m', 'new_m_la_out_norm_w': 'new_m', 'new_m_la_w_out': 'new_m', 'new_m_sg_w_in': 'new_m', 'new_m_sg_ln_w': 'new_m', 'new_m_sg_ln_b': 'new_m', 'new_m_sg_w_s': 'new_m', 'new_m_sg_b_s': 'new_m', 'new_m_sg_w_out': 'new_m', 'new_m_ffn_w_up': 'new_m', 'new_m_ffn_w_down': 'new_m', 'new_v_norm_w': 'new_v', 'new_v_la_w_in': 'new_v', 'new_v_la_conv_w': 'new_v', 'new_v_la_a_log': 'new_v', 'new_v_la_dt_bias': 'new_v', 'new_v_la_out_norm_w': 'new_v', 'new_v_la_w_out': 'new_v', 'new_v_sg_w_in': 'new_v', 'new_v_sg_ln_w': 'new_v', 'new_v_sg_ln_b': 'new_v', 'new_v_sg_w_s': 'new_v', 'new_v_sg_b_s': 'new_v', 'new_v_sg_w_out': 'new_v', 'new_v_ffn_w_up': 'new_v', 'new_v_ffn_w_down': 'new_v'}


def _forward(args):
    return _fwd_reference(*[args[k] for k in FWD_PARAMS])


def _output_shape():
    def fwd():
        inp = _fwd_setup_inputs(0)
        return _fwd_reference(*[inp[k] for k in FWD_PARAMS])
    out = _jax.eval_shape(fwd)
    return out.shape, out.dtype

N_MICROBATCH = 1
ADAM_LR = 0.001
ADAM_B1 = 0.9
ADAM_B2 = 0.999
ADAM_EPS = 1e-08
ADAM_WD = 0.01
ADAM_STEP = 10
PER_EXAMPLE_BATCH_AXIS = {'x': 0, 'loss_target': 0}
SHARED_INPUTS = []
_WEIGHT_DTYPES = {'norm_w': _jnp.float32, 'la_w_in': _jnp.float32, 'la_conv_w': _jnp.float32, 'la_a_log': _jnp.float32, 'la_dt_bias': _jnp.float32, 'la_out_norm_w': _jnp.float32, 'la_w_out': _jnp.float32, 'sg_w_in': _jnp.float32, 'sg_ln_w': _jnp.float32, 'sg_ln_b': _jnp.float32, 'sg_w_s': _jnp.float32, 'sg_b_s': _jnp.float32, 'sg_w_out': _jnp.float32, 'ffn_w_up': _jnp.float32, 'ffn_w_down': _jnp.float32}
MOMENT_SCALE = {'norm_w': 2.392707e+01, 'la_w_in': 5.025435e-01, 'la_conv_w': 1.975762e+00, 'la_a_log': 1.364039e+01, 'la_dt_bias': 1.289026e+01, 'la_out_norm_w': 1.715275e+01, 'la_w_out': 3.631384e+00, 'sg_w_in': 1.650965e+00, 'sg_ln_w': 1.467883e-01, 'sg_ln_b': 2.577248e-01, 'sg_w_s': 2.612931e-01, 'sg_b_s': 5.063824e-01, 'sg_w_out': 8.422098e+00, 'ffn_w_up': 1.368467e+00, 'ffn_w_down': 8.916791e+00}


def _to_microbatches(a, axis):
    t = _jnp.moveaxis(a, axis, 0)
    t = t.reshape((N_MICROBATCH, t.shape[0] // N_MICROBATCH) + t.shape[1:])
    return _jnp.moveaxis(t, 1, axis + 1)


def setup_inputs(seed: int = 0) -> dict:
    inp = _fwd_setup_inputs(seed)
    key = _jax.random.fold_in(_jax.random.key(seed), 7919)
    shape, _ = _output_shape()
    out = dict(inp)
    out["loss_target"] = _jax.random.normal(_jax.random.fold_in(key, 0), shape, _jnp.float32)
    for i, name in enumerate(TWIN_WEIGHTS):
        w = inp[name].astype(_jnp.float32)
        if MOMENT_SCALE is None:
            s = _jnp.sqrt(_jnp.mean(_jnp.square(w)) + 1e-30)
        else:
            s = MOMENT_SCALE[name]
        km, kv = _jax.random.split(_jax.random.fold_in(key, i + 1))
        out[name] = w
        out["m_" + name] = s * _jax.random.normal(km, w.shape, _jnp.float32)
        out["v_" + name] = (s * s) * _jax.random.uniform(kv, w.shape, _jnp.float32, 0.5, 1.5)
    if N_MICROBATCH > 1:
        for name, axis in PER_EXAMPLE_BATCH_AXIS.items():
            out[name] = _to_microbatches(out[name], axis)
    return {'x': out['x'], 'norm_w': out['norm_w'], 'la_w_in': out['la_w_in'], 'la_conv_w': out['la_conv_w'], 'la_a_log': out['la_a_log'], 'la_dt_bias': out['la_dt_bias'], 'la_out_norm_w': out['la_out_norm_w'], 'la_w_out': out['la_w_out'], 'sg_w_in': out['sg_w_in'], 'sg_ln_w': out['sg_ln_w'], 'sg_ln_b': out['sg_ln_b'], 'sg_w_s': out['sg_w_s'], 'sg_b_s': out['sg_b_s'], 'sg_w_out': out['sg_w_out'], 'ffn_w_up': out['ffn_w_up'], 'ffn_w_down': out['ffn_w_down'], 'loss_target': out['loss_target'], 'm_norm_w': out['m_norm_w'], 'm_la_w_in': out['m_la_w_in'], 'm_la_conv_w': out['m_la_conv_w'], 'm_la_a_log': out['m_la_a_log'], 'm_la_dt_bias': out['m_la_dt_bias'], 'm_la_out_norm_w': out['m_la_out_norm_w'], 'm_la_w_out': out['m_la_w_out'], 'm_sg_w_in': out['m_sg_w_in'], 'm_sg_ln_w': out['m_sg_ln_w'], 'm_sg_ln_b': out['m_sg_ln_b'], 'm_sg_w_s': out['m_sg_w_s'], 'm_sg_b_s': out['m_sg_b_s'], 'm_sg_w_out': out['m_sg_w_out'], 'm_ffn_w_up': out['m_ffn_w_up'], 'm_ffn_w_down': out['m_ffn_w_down'], 'v_norm_w': out['v_norm_w'], 'v_la_w_in': out['v_la_w_in'], 'v_la_conv_w': out['v_la_conv_w'], 'v_la_a_log': out['v_la_a_log'], 'v_la_dt_bias': out['v_la_dt_bias'], 'v_la_out_norm_w': out['v_la_out_norm_w'], 'v_la_w_out': out['v_la_w_out'], 'v_sg_w_in': out['v_sg_w_in'], 'v_sg_ln_w': out['v_sg_ln_w'], 'v_sg_ln_b': out['v_sg_ln_b'], 'v_sg_w_s': out['v_sg_w_s'], 'v_sg_b_s': out['v_sg_b_s'], 'v_sg_w_out': out['v_sg_w_out'], 'v_ffn_w_up': out['v_ffn_w_up'], 'v_ffn_w_down': out['v_ffn_w_down']}


def _loss(weights, diff, rest, loss_target):
    with _jax.named_scope("forward"):
        args = {**rest, TWIN_DIFF_INPUT: diff, **{k: w.astype(_WEIGHT_DTYPES[k]) for k, w in weights.items()}}
        y = _forward(args)
    with _jax.named_scope("loss_head"):
        err = _jnp.square(y.astype(_jnp.float32) - loss_target)
        return 0.5 * _jnp.sum(_jnp.mean(err, axis=-1)) if err.ndim else 0.5 * err


def _adamw(w, g, m, v):
    m = ADAM_B1 * m + (1.0 - ADAM_B1) * g
    v = ADAM_B2 * v + (1.0 - ADAM_B2) * _jnp.square(g)
    m_hat = m / (1.0 - ADAM_B1 ** ADAM_STEP)
    v_hat = v / (1.0 - ADAM_B2 ** ADAM_STEP)
    delta = -ADAM_LR * (m_hat / (_jnp.sqrt(v_hat) + ADAM_EPS) + ADAM_WD * w)
    return delta, m, v


def reference(x, norm_w, la_w_in, la_conv_w, la_a_log, la_dt_bias, la_out_norm_w, la_w_out, sg_w_in, sg_ln_w, sg_ln_b, sg_w_s, sg_b_s, sg_w_out, ffn_w_up, ffn_w_down, loss_target, m_norm_w, m_la_w_in, m_la_conv_w, m_la_a_log, m_la_dt_bias, m_la_out_norm_w, m_la_w_out, m_sg_w_in, m_sg_ln_w, m_sg_ln_b, m_sg_w_s, m_sg_b_s, m_sg_w_out, m_ffn_w_up, m_ffn_w_down, v_norm_w, v_la_w_in, v_la_conv_w, v_la_a_log, v_la_dt_bias, v_la_out_norm_w, v_la_w_out, v_sg_w_in, v_sg_ln_w, v_sg_ln_b, v_sg_w_s, v_sg_b_s, v_sg_w_out, v_ffn_w_up, v_ffn_w_down):
    given = dict(x=x, norm_w=norm_w, la_w_in=la_w_in, la_conv_w=la_conv_w, la_a_log=la_a_log, la_dt_bias=la_dt_bias, la_out_norm_w=la_out_norm_w, la_w_out=la_w_out, sg_w_in=sg_w_in, sg_ln_w=sg_ln_w, sg_ln_b=sg_ln_b, sg_w_s=sg_w_s, sg_b_s=sg_b_s, sg_w_out=sg_w_out, ffn_w_up=ffn_w_up, ffn_w_down=ffn_w_down, loss_target=loss_target, m_norm_w=m_norm_w, m_la_w_in=m_la_w_in, m_la_conv_w=m_la_conv_w, m_la_a_log=m_la_a_log, m_la_dt_bias=m_la_dt_bias, m_la_out_norm_w=m_la_out_norm_w, m_la_w_out=m_la_w_out, m_sg_w_in=m_sg_w_in, m_sg_ln_w=m_sg_ln_w, m_sg_ln_b=m_sg_ln_b, m_sg_w_s=m_sg_w_s, m_sg_b_s=m_sg_b_s, m_sg_w_out=m_sg_w_out, m_ffn_w_up=m_ffn_w_up, m_ffn_w_down=m_ffn_w_down, v_norm_w=v_norm_w, v_la_w_in=v_la_w_in, v_la_conv_w=v_la_conv_w, v_la_a_log=v_la_a_log, v_la_dt_bias=v_la_dt_bias, v_la_out_norm_w=v_la_out_norm_w, v_la_w_out=v_la_w_out, v_sg_w_in=v_sg_w_in, v_sg_ln_w=v_sg_ln_w, v_sg_ln_b=v_sg_ln_b, v_sg_w_s=v_sg_w_s, v_sg_b_s=v_sg_b_s, v_sg_w_out=v_sg_w_out, v_ffn_w_up=v_ffn_w_up, v_ffn_w_down=v_ffn_w_down)
    weights = {n: given[n] for n in TWIN_WEIGHTS}
    shared = {n: given[n] for n in SHARED_INPUTS}
    per_example = {n: given[n] for n in ['x']}
    grad_fn = _jax.value_and_grad(_loss, argnums=(0, 1))

    def one_microbatch(ex, loss_target):
        ex = dict(ex)
        diff = ex.pop(TWIN_DIFF_INPUT)
        return grad_fn(weights, diff, {**shared, **ex}, loss_target)

    if N_MICROBATCH == 1:
        loss, (grad_w, grad_x) = one_microbatch(per_example, given["loss_target"])
    else:
        def body(carry, xs):
            loss_sum, grad_sum = carry
            l_k, (gw_k, gx_k) = one_microbatch(xs[0], xs[1])
            with _jax.named_scope("update"):
                return (loss_sum + l_k, _jax.tree.map(_jnp.add, grad_sum, gw_k)), gx_k

        init = (_jnp.zeros((), _jnp.float32), _jax.tree.map(_jnp.zeros_like, weights))
        (loss, grad_w), grad_x = _jax.lax.scan(body, init, (per_example, given["loss_target"]))
    with _jax.named_scope("update"):
        delta_w, new_m, new_v = {}, {}, {}
        for n in TWIN_WEIGHTS:
            delta_w[n], new_m[n], new_v[n] = _adamw(weights[n], grad_w[n], given["m_" + n], given["v_" + n])
    return (loss, grad_x, *[grad_w[n] for n in TWIN_WEIGHTS], *[delta_w[n] for n in TWIN_WEIGHTS],
            *[new_m[n] for n in TWIN_WEIGHTS], *[new_v[n] for n in TWIN_WEIGHTS])
```

```python
import functools
import math

import numpy as np
import jax
import jax.numpy as jnp
from jax import lax
from jax.experimental import pallas as pl
from jax.experimental.pallas import tpu as pltpu

F32 = jnp.float32
BF = jnp.bfloat16
HI = lax.Precision.HIGHEST

V7X_VMEM_BYTES = 64 * 1024 * 1024
VMEM_LIMIT = (V7X_VMEM_BYTES * 3) // 4
LANES = 128
SUBLANES = 8
HEAD_DIM = 128
LA_CHUNK = 64
SG_CHUNK = 128
CONV_WIDTH = 4
ROPE_BASE = 10000.0
EPS = 1e-6
L2_EPS = 1e-6
N_CHIPS = 4
N_DEV = 8

ADAM_LR = 0.001
ADAM_B1 = 0.9
ADAM_B2 = 0.999
ADAM_EPS = 1e-08
ADAM_WD = 0.01
ADAM_STEP = 10

MESH = pl.DeviceIdType.MESH
ANY = pl.BlockSpec(memory_space=pl.ANY)

NN = (((1,), (0,)), ((), ()))
NT = (((1,), (1,)), ((), ()))
TN = (((0,), (0,)), ((), ()))


def _pcall(body, **kw):
    return pl.pallas_call(body, **kw)


def _cp(n_axes):
    return pltpu.CompilerParams(dimension_semantics=("arbitrary",) * n_axes, vmem_limit_bytes=VMEM_LIMIT)


def _tile(n, pref, unit=LANES):
    if n <= pref:
        return n
    t = (pref // unit) * unit
    while t >= unit:
        if n % t == 0:
            return t
        t -= unit
    return n


def _dot(a, b, dims=NN):
    return lax.dot_general(a.astype(BF), b.astype(BF), dims, preferred_element_type=F32)


def _dot_hi(a, b, dims=NN):
    return lax.dot_general(a.astype(F32), b.astype(F32), dims, precision=HI, preferred_element_type=F32)


def _sigmoid(x):
    return 1.0 / (1.0 + jnp.exp(-x))


def _silu(x):
    return x * _sigmoid(x)


def _dsilu(x):
    s = _sigmoid(x)
    return s * (1.0 + x * (1.0 - s))


GELU_C = math.sqrt(2.0 / math.pi)
GELU_A = 0.044715


def _gelu(x):
    return 0.5 * x * (1.0 + jnp.tanh(GELU_C * (x + GELU_A * x * x * x)))


def _dgelu(x):
    t = jnp.tanh(GELU_C * (x + GELU_A * x * x * x))
    return 0.5 * (1.0 + t) + 0.5 * x * (1.0 - t * t) * GELU_C * (1.0 + 3.0 * GELU_A * x * x)


def _matmul(a, b, *, dims, grid, a_spec, b_spec, out_shape, out_spec, acc_shape, name,
            epilogue=None, extras=(), extra_specs=()):
    nk = grid[2]
    outs = tuple(out_shape) if isinstance(out_shape, (tuple, list)) else (out_shape,)
    out_specs = tuple(out_spec) if isinstance(out_spec, (tuple, list)) else (out_spec,)
    n_ex, n_out = len(extras), len(outs)

    def body(*refs):
        a_ref, b_ref = refs[0], refs[1]
        ex = refs[2:2 + n_ex]
        o = refs[2 + n_ex:2 + n_ex + n_out]
        k = pl.program_id(2)
        part = lax.dot_general(a_ref[...].astype(BF), b_ref[...].astype(BF), dims, preferred_element_type=F32)

        def finish(val):
            res = epilogue(val, *[e[...] for e in ex]) if epilogue is not None else (val,)
            for r, oref in zip(res, o):
                oref[...] = r.astype(oref.dtype)

        if nk == 1:
            finish(part)
        else:
            acc = refs[2 + n_ex + n_out]

            @pl.when(k == 0)
            def _():
                acc[...] = part

            @pl.when(k > 0)
            def _():
                acc[...] += part

            @pl.when(k == nk - 1)
            def _():
                finish(acc[...])

    res = _pcall(
        body, name=name, grid=grid,
        in_specs=[a_spec, b_spec, *extra_specs],
        out_specs=out_specs if len(outs) > 1 else out_specs[0],
        out_shape=outs if len(outs) > 1 else outs[0],
        scratch_shapes=[pltpu.VMEM(acc_shape, F32)] if nk > 1 else [],
        compiler_params=_cp(3),
    )(a, b, *extras)
    return res


def mm_nn(a, w, *, name, out_dtypes=(F32,), epilogue=None, extras=(), tm=1024, tn=1024, tk=512):
    M, K = a.shape
    if w.ndim == 3:
        S, _, Ns = w.shape
        N = S * Ns
    else:
        S, Ns = 1, w.shape[1]
        N = Ns
    tm, tn, tk = _tile(M, tm), _tile(Ns, tn), _tile(K, tk)
    npb = Ns // tn
    grid = (M // tm, N // tn, K // tk)
    a_spec = pl.BlockSpec((tm, tk), lambda i, j, k: (i, k))
    if w.ndim == 3:
        b_spec = pl.BlockSpec((None, tk, tn), lambda i, j, k: (j // npb, k, j % npb))
    else:
        b_spec = pl.BlockSpec((tk, tn), lambda i, j, k: (k, j))
    o_spec = pl.BlockSpec((tm, tn), lambda i, j, k: (i, j))
    outs = tuple(jax.ShapeDtypeStruct((M, N), d) for d in out_dtypes)
    res = _matmul(a, w, dims=NN, grid=grid, a_spec=a_spec, b_spec=b_spec,
                  out_shape=outs, out_spec=(o_spec,) * len(outs), acc_shape=(tm, tn), name=name,
                  epilogue=epilogue, extras=extras, extra_specs=(o_spec,) * len(extras))
    return res


def mm_nt(a, w, *, name, out_dtypes=(F32,), epilogue=None, extras=(), tm=1024, tn=1024, tk=512):
    M, Kc = a.shape
    if w.ndim == 3:
        S, Nout, Ks = w.shape
    else:
        S, (Nout, Ks) = 1, w.shape
    assert S * Ks == Kc
    tm, tn, tk = _tile(M, tm), _tile(Nout, tn), _tile(Ks, tk)
    kpb = Ks // tk
    grid = (M // tm, Nout // tn, Kc // tk)
    a_spec = pl.BlockSpec((tm, tk), lambda i, j, k: (i, k))
    if w.ndim == 3:
        b_spec = pl.BlockSpec((None, tn, tk), lambda i, j, k: (k // kpb, j, k % kpb))
    else:
        b_spec = pl.BlockSpec((tn, tk), lambda i, j, k: (j, k))
    o_spec = pl.BlockSpec((tm, tn), lambda i, j, k: (i, j))
    outs = tuple(jax.ShapeDtypeStruct((M, Nout), d) for d in out_dtypes)
    return _matmul(a, w, dims=NT, grid=grid, a_spec=a_spec, b_spec=b_spec,
                   out_shape=outs, out_spec=(o_spec,) * len(outs), acc_shape=(tm, tn), name=name,
                   epilogue=epilogue, extras=extras, extra_specs=(o_spec,) * len(extras))


def mm_tn(x, dy, *, name, shards=1, tm=1024, tn=1024, tk=512):
    T, Kin = x.shape
    N = dy.shape[1]
    Ns = N // shards
    tm, tn, tk = _tile(Kin, tm), _tile(Ns, tn), _tile(T, tk)
    npb = Ns // tn
    grid = (Kin // tm, N // tn, T // tk)
    a_spec = pl.BlockSpec((tk, tm), lambda i, j, k: (k, i))
    b_spec = pl.BlockSpec((tk, tn), lambda i, j, k: (k, j))
    if shards > 1:
        o_spec = pl.BlockSpec((None, tm, tn), lambda i, j, k: (j // npb, i, j % npb))
        out = jax.ShapeDtypeStruct((shards, Kin, Ns), F32)
    else:
        o_spec = pl.BlockSpec((tm, tn), lambda i, j, k: (i, j))
        out = jax.ShapeDtypeStruct((Kin, N), F32)
    return _matmul(x, dy, dims=TN, grid=grid, a_spec=a_spec, b_spec=b_spec,
                   out_shape=out, out_spec=o_spec, acc_shape=(tm, tn), name=name)


ROW_TILE = 256


def _rows(tr, d):
    return pl.BlockSpec((tr, d), lambda i: (i, 0))


def _fixed(shape):
    nd = len(shape)
    return pl.BlockSpec(shape, lambda *_: (0,) * nd)


def _rms(xv, w):
    r = lax.rsqrt(jnp.mean(xv * xv, axis=-1, keepdims=True) + EPS)
    return xv * r * w


def rms_fwd(x, w, *, name):
    T, D = x.shape
    tr = _tile(T, ROW_TILE, SUBLANES)

    def body(x_ref, w_ref, y_ref):
        y_ref[...] = _rms(x_ref[...], w_ref[...]).astype(y_ref.dtype)

    return _pcall(body, name=name, grid=(T // tr,), in_specs=[_rows(tr, D), _fixed((1, D))],
                  out_specs=_rows(tr, D), out_shape=jax.ShapeDtypeStruct((T, D), BF), compiler_params=_cp(1))(x, w)


def res_norm(h, m, wa, wb, *, name):
    T, D = h.shape
    tr = _tile(T, ROW_TILE, SUBLANES)
    second = wb is not None

    def body(*refs):
        if second:
            h_ref, m_ref, wa_ref, wb_ref, ho_ref, y_ref = refs
        else:
            h_ref, m_ref, wa_ref, ho_ref = refs
        ho = h_ref[...] + _rms(m_ref[...], wa_ref[...])
        ho_ref[...] = ho
        if second:
            y_ref[...] = _rms(ho, wb_ref[...]).astype(y_ref.dtype)

    ins = [h, m, wa] + ([wb] if second else [])
    in_specs = [_rows(tr, D), _rows(tr, D), _fixed((1, D))] + ([_fixed((1, D))] if second else [])
    out_shape = [jax.ShapeDtypeStruct((T, D), F32)] + ([jax.ShapeDtypeStruct((T, D), BF)] if second else [])
    out_specs = [_rows(tr, D)] * len(out_shape)
    res = _pcall(body, name=name, grid=(T // tr,), in_specs=in_specs, out_specs=out_specs,
                 out_shape=out_shape, compiler_params=_cp(1))(*ins)
    return tuple(res) if second else (res[0], None)


def rms_bwd(x, w, dy, dres, *, name, out_dtype):
    T, D = x.shape
    tr = _tile(T, ROW_TILE, SUBLANES)
    has_res = dres is not None

    def body(*refs):
        if has_res:
            x_ref, w_ref, dy_ref, dr_ref, dx_ref, dw_ref = refs
        else:
            x_ref, w_ref, dy_ref, dx_ref, dw_ref = refs
        i = pl.program_id(0)
        xv = x_ref[...]
        r = lax.rsqrt(jnp.mean(xv * xv, axis=-1, keepdims=True) + EPS)
        xh = xv * r
        dyv = dy_ref[...].astype(F32)
        dyw = dyv * w_ref[...]
        dx = r * (dyw - xh * jnp.mean(dyw * xh, axis=-1, keepdims=True))
        if has_res:
            dx = dx + dr_ref[...].astype(F32)
        dx_ref[...] = dx.astype(dx_ref.dtype)
        part = jnp.sum(dyv * xh, axis=0, keepdims=True)

        @pl.when(i == 0)
        def _():
            dw_ref[...] = part

        @pl.when(i > 0)
        def _():
            dw_ref[...] += part

    ins = [x, w, dy] + ([dres] if has_res else [])
    in_specs = [_rows(tr, D), _fixed((1, D)), _rows(tr, D)] + ([_rows(tr, D)] if has_res else [])
    return _pcall(body, name=name, grid=(T // tr,), in_specs=in_specs,
                  out_specs=[_rows(tr, D), _fixed((1, D))],
                  out_shape=[jax.ShapeDtypeStruct((T, D), out_dtype), jax.ShapeDtypeStruct((1, D), F32)],
                  compiler_params=_cp(1))(*ins)


def loss_head(h, tgt, *, name):
    T, D = h.shape
    tr = _tile(T, ROW_TILE, SUBLANES)

    def body(h_ref, t_ref, dh_ref, l_ref):
        i = pl.program_id(0)
        e = h_ref[...] - t_ref[...]
        dh_ref[...] = e * (1.0 / D)
        part = 0.5 * jnp.sum(jnp.mean(e * e, axis=-1, keepdims=True), axis=0, keepdims=True)
        part = jnp.broadcast_to(part, (1, LANES))

        @pl.when(i == 0)
        def _():
            l_ref[...] = part

        @pl.when(i > 0)
        def _():
            l_ref[...] += part

    return _pcall(body, name=name, grid=(T // tr,), in_specs=[_rows(tr, D), _rows(tr, D)],
                  out_specs=[_rows(tr, D), _fixed((1, LANES))],
                  out_shape=[jax.ShapeDtypeStruct((T, D), F32), jax.ShapeDtypeStruct((1, LANES), F32)],
                  compiler_params=_cp(1))(h, tgt)


def adamw(w, g, m, v, *, name):
    R, C = w.shape
    tr = _tile(R, max(SUBLANES, (1 << 18) // C), SUBLANES)
    c1 = 1.0 - ADAM_B1 ** ADAM_STEP
    c2 = 1.0 - ADAM_B2 ** ADAM_STEP

    def body(w_ref, g_ref, m_ref, v_ref, d_ref, mo_ref, vo_ref):
        gv = g_ref[...]
        m2 = ADAM_B1 * m_ref[...] + (1.0 - ADAM_B1) * gv
        v2 = ADAM_B2 * v_ref[...] + (1.0 - ADAM_B2) * (gv * gv)
        d_ref[...] = -ADAM_LR * ((m2 / c1) / (jnp.sqrt(v2 / c2) + ADAM_EPS) + ADAM_WD * w_ref[...])
        mo_ref[...] = m2
        vo_ref[...] = v2

    spec = _rows(tr, C)
    sds = jax.ShapeDtypeStruct((R, C), F32)
    return _pcall(body, name=name, grid=(R // tr,), in_specs=[spec] * 4, out_specs=[spec] * 3,
                  out_shape=[sds] * 3, compiler_params=_cp(1))(w, g, m, v)


HALO = SUBLANES


def _conv_down(xx, w_ref):
    acc = xx * w_ref[pl.ds(CONV_WIDTH - 1, 1), :]
    for d in range(1, CONV_WIDTH):
        acc = acc + pltpu.roll(xx, d, 0) * w_ref[pl.ds(CONV_WIDTH - 1 - d, 1), :]
    return acc


def _conv_tile(x_ref, halo_ref, w_ref, first):
    xs = x_ref[...]
    hal = jnp.where(first, 0.0, halo_ref[...])
    cat = jnp.concatenate([hal, xs[0:HALO]], axis=0)
    return jnp.concatenate([_conv_down(cat, w_ref)[HALO:2 * HALO], _conv_down(xs, w_ref)[HALO:]], axis=0)


def _shift_down_tile(x_ref, halo_ref, first, d):
    xs = x_ref[...]
    if d == 0:
        return xs
    hal = jnp.where(first, 0.0, halo_ref[...])
    cat = jnp.concatenate([hal, xs[0:HALO]], axis=0)
    return jnp.concatenate([pltpu.roll(cat, d, 0)[HALO:2 * HALO], pltpu.roll(xs, d, 0)[HALO:]], axis=0)


def _l2n(s):
    return s * lax.rsqrt(jnp.sum(s * s, axis=-1, keepdims=True) + L2_EPS)


def prep_fwd(pm, off, wc8, woff, nblk, l2, *, name):
    T = pm.shape[0]
    tr = _tile(T, ROW_TILE, SUBLANES)
    hb = tr // HALO

    def body(x_ref, halo_ref, w_ref, o_ref):
        i = pl.program_id(0)
        s = _silu(_conv_tile(x_ref, halo_ref, w_ref, i == 0))
        o_ref[...] = _l2n(s) if l2 else s

    return _pcall(
        body, name=name, grid=(T // tr, nblk),
        in_specs=[pl.BlockSpec((tr, LANES), lambda i, c: (i, off + c)),
                  pl.BlockSpec((HALO, LANES), lambda i, c: (jnp.maximum(i * hb - 1, 0), off + c)),
                  pl.BlockSpec((SUBLANES, LANES), lambda i, c: (0, woff + c))],
        out_specs=pl.BlockSpec((tr, LANES), lambda i, c: (i, c)),
        out_shape=jax.ShapeDtypeStruct((T, nblk * LANES), F32), compiler_params=_cp(2))(pm, pm, wc8)


def prep_bwd_act(pm, off, wc8, woff, nblk, l2, dout, *, name):
    T = pm.shape[0]
    tr = _tile(T, ROW_TILE, SUBLANES)
    hb = tr // HALO

    def body(x_ref, halo_ref, w_ref, do_ref, dc_ref, dw_ref):
        i = pl.program_id(1)
        first = i == 0
        y = _conv_tile(x_ref, halo_ref, w_ref, first)
        s = _silu(y)
        do = do_ref[...]
        if l2:
            r = lax.rsqrt(jnp.sum(s * s, axis=-1, keepdims=True) + L2_EPS)
            nrm = s * r
            ds = r * (do - nrm * jnp.sum(do * nrm, axis=-1, keepdims=True))
        else:
            ds = do
        dc = ds * _dsilu(y)
        dc_ref[...] = dc

        @pl.when(first)
        def _():
            dw_ref[...] = jnp.zeros_like(dw_ref)

        for j in range(CONV_WIDTH):
            xsh = _shift_down_tile(x_ref, halo_ref, first, CONV_WIDTH - 1 - j)
            dw_ref[pl.ds(j, 1), :] += jnp.sum(dc * xsh, axis=0, keepdims=True)

    return _pcall(
        body, name=name, grid=(nblk, T // tr),
        in_specs=[pl.BlockSpec((tr, LANES), lambda c, i: (i, off + c)),
                  pl.BlockSpec((HALO, LANES), lambda c, i: (jnp.maximum(i * hb - 1, 0), off + c)),
                  pl.BlockSpec((SUBLANES, LANES), lambda c, i: (0, woff + c)),
                  pl.BlockSpec((tr, LANES), lambda c, i: (i, c))],
        out_specs=[pl.BlockSpec((tr, LANES), lambda c, i: (i, c)),
                   pl.BlockSpec((SUBLANES, LANES), lambda c, i: (0, c))],
        out_shape=[jax.ShapeDtypeStruct((T, nblk * LANES), F32),
                   jax.ShapeDtypeStruct((SUBLANES, nblk * LANES), F32)],
        compiler_params=_cp(2))(pm, pm, wc8, dout)


def prep_bwd_conv(dc, wc8, woff, nblk, *, name):
    T = dc.shape[0]
    tr = _tile(T, ROW_TILE, SUBLANES)
    hb = tr // HALO
    nt = T // tr
    last_halo = T // HALO - 1

    def up(xx, w_ref):
        rows = xx.shape[0]
        acc = xx * w_ref[pl.ds(CONV_WIDTH - 1, 1), :]
        for d in range(1, CONV_WIDTH):
            acc = acc + pltpu.roll(xx, rows - d, 0) * w_ref[pl.ds(CONV_WIDTH - 1 - d, 1), :]
        return acc

    def body(x_ref, halo_ref, w_ref, o_ref):
        i = pl.program_id(0)
        xs = x_ref[...]
        hal = jnp.where(i == nt - 1, 0.0, halo_ref[...])
        cat = jnp.concatenate([xs[tr - HALO:tr], hal], axis=0)
        out = jnp.concatenate([up(xs, w_ref)[:tr - HALO], up(cat, w_ref)[0:HALO]], axis=0)
        o_ref[...] = out.astype(o_ref.dtype)

    return _pcall(
        body, name=name, grid=(nt, nblk),
        in_specs=[pl.BlockSpec((tr, LANES), lambda i, c: (i, c)),
                  pl.BlockSpec((HALO, LANES), lambda i, c: (jnp.minimum((i + 1) * hb, last_halo), c)),
                  pl.BlockSpec((SUBLANES, LANES), lambda i, c: (0, woff + c))],
        out_specs=pl.BlockSpec((tr, LANES), lambda i, c: (i, c)),
        out_shape=jax.ShapeDtypeStruct((T, nblk * LANES), BF), compiler_params=_cp(2))(dc, dc, wc8)


def _softplus(x):
    return jnp.maximum(x, 0.0) + jnp.log(1.0 + jnp.exp(-jnp.abs(x)))


def _tril_ones(c):
    t = lax.broadcasted_iota(jnp.int32, (c, c), 0)
    s = lax.broadcasted_iota(jnp.int32, (c, c), 1)
    return (t >= s).astype(F32)


def _triu_ones(c):
    t = lax.broadcasted_iota(jnp.int32, (c, c), 0)
    s = lax.broadcasted_iota(jnp.int32, (c, c), 1)
    return (t <= s).astype(F32)


def gates_fwd(pg, arow, dtrow, H, *, name):
    T = pg.shape[0]
    C = LA_CHUNK
    N = T // C

    def body(x_ref, a_ref, dt_ref, bg_ref, gr_ref):
        x = x_ref[...]
        lane = lax.broadcasted_iota(jnp.int32, (C, LANES), 1)
        g = -jnp.exp(a_ref[...]) * _softplus(x + dt_ref[...])
        g = jnp.where((lane >= H) & (lane < 2 * H), g, 0.0)
        lm = _tril_ones(C)
        gc = _dot_hi(lm, g)
        bg_ref[...] = jnp.where(lane < H, _sigmoid(x), gc)
        gr_ref[...] = _dot_hi(g, _triu_ones(C), TN)

    return _pcall(
        body, name=name, grid=(N,),
        in_specs=[pl.BlockSpec((C, LANES), lambda n: (n, 0)), _fixed((1, LANES)), _fixed((1, LANES))],
        out_specs=[pl.BlockSpec((C, LANES), lambda n: (n, 0)), pl.BlockSpec((None, LANES, C), lambda n: (n, 0, 0))],
        out_shape=[jax.ShapeDtypeStruct((T, LANES), F32), jax.ShapeDtypeStruct((N, LANES, C), F32)],
        compiler_params=_cp(1))(pg, arow, dtrow)


def gates_bwd(pg, arow, dtrow, dbg, H, *, name):
    T = pg.shape[0]
    C = LA_CHUNK
    N = T // C

    def body(x_ref, a_ref, dt_ref, d_ref, dx_ref, da_ref, ddt_ref):
        n = pl.program_id(0)
        x = x_ref[...]
        d = d_ref[...]
        lane = lax.broadcasted_iota(jnp.int32, (C, LANES), 1)
        in_g = (lane >= H) & (lane < 2 * H)
        e = jnp.exp(a_ref[...])
        xs = x + dt_ref[...]
        g = -e * _softplus(xs)
        dg = _dot_hi(_tril_ones(C), jnp.where(in_g, d, 0.0), TN)
        dxs = jnp.where(in_g, dg * (-e) * _sigmoid(xs), 0.0)
        beta = _sigmoid(x)
        dx_ref[...] = jnp.where(lane < H, d * beta * (1.0 - beta), dxs).astype(dx_ref.dtype)
        pa = jnp.sum(jnp.where(in_g, dg * g, 0.0), axis=0, keepdims=True)
        pd = jnp.sum(dxs, axis=0, keepdims=True)

        @pl.when(n == 0)
        def _():
            da_ref[...] = pa
            ddt_ref[...] = pd

        @pl.when(n > 0)
        def _():
            da_ref[...] += pa
            ddt_ref[...] += pd

    return _pcall(
        body, name=name, grid=(N,),
        in_specs=[pl.BlockSpec((C, LANES), lambda n: (n, 0)), _fixed((1, LANES)), _fixed((1, LANES)),
                  pl.BlockSpec((C, LANES), lambda n: (n, 0))],
        out_specs=[pl.BlockSpec((C, LANES), lambda n: (n, 0)), _fixed((1, LANES)), _fixed((1, LANES))],
        out_shape=[jax.ShapeDtypeStruct((T, LANES), BF), jax.ShapeDtypeStruct((1, LANES), F32),
                   jax.ShapeDtypeStruct((1, LANES), F32)],
        compiler_params=_cp(1))(pg, arow, dtrow, dbg)


QK_SCALE = HEAD_DIM ** -0.5


def _head_rstd(o):
    return lax.rsqrt(jnp.mean(o * o, axis=-1, keepdims=True) + EPS)


def _gdn_gates(bg_ref, gr_ref, h, H):
    C = LA_CHUNK
    bgv = bg_ref[...]
    lane = lax.broadcasted_iota(jnp.int32, (C, LANES), 1)
    beta = jnp.sum(jnp.where(lane == h, bgv, 0.0), axis=1, keepdims=True)
    gc = jnp.sum(jnp.where(lane == H + h, bgv, 0.0), axis=1, keepdims=True)
    grow = gr_ref[pl.ds(H + h, 1), :]
    ri = lax.broadcasted_iota(jnp.int32, (C, 1), 0)
    gl = jnp.sum(jnp.where(ri == C - 1, gc, 0.0), axis=0, keepdims=True)
    return beta, gc, grow, gl


def _chunk_masks():
    C = LA_CHUNK
    ti = lax.broadcasted_iota(jnp.int32, (C, C), 0)
    si = lax.broadcasted_iota(jnp.int32, (C, C), 1)
    return ti >= si, ti > si, ti == si


def _decay(gc, grow, causal):
    return jnp.where(causal, jnp.exp(jnp.where(causal, gc - grow, 0.0)), 0.0)


def _unit_lower_inverse(a, eye):
    x = -a
    p = jnp.where(eye, 1.0, 0.0) + x
    for _ in range(5):
        x = _dot_hi(x, x)
        p = p + _dot_hi(p, x)
    return p


def gdn_fwd(q, k, v, pm, zoff, bg, gcrow, wn, H, *, name):
    T = q.shape[0]
    C = LA_CHUNK
    N = T // C
    hd = HEAD_DIM

    def body(q_ref, k_ref, v_ref, z_ref, bg_ref, gr_ref, wn_ref, og_ref, or_ref, sall_ref, tall_ref, S):
        n = pl.program_id(0)
        h = pl.program_id(1)

        @pl.when(n == 0)
        def _():
            S[h] = jnp.zeros((hd, hd), F32)

        beta, gc, grow, gl = _gdn_gates(bg_ref, gr_ref, h, H)
        causal, strict, eye = _chunk_masks()
        dm = _decay(gc, grow, causal)
        qs = q_ref[...] * QK_SCALE
        kk = k_ref[...]
        vv = v_ref[...]
        eg = jnp.exp(gc)
        kb = kk * beta
        a = jnp.where(strict, _dot(kb, kk, NT) * dm, 0.0)
        tm = _unit_lower_inverse(a, eye)
        u = _dot(tm, vv * beta)
        w = _dot(tm, kb * eg)
        s0 = S[h]
        vnew = u - _dot(w, s0)
        qk = jnp.where(causal, _dot(qs, kk, NT) * dm, 0.0)
        o = _dot(qs * eg, s0) + _dot(qk, vnew)
        S[h] = s0 * jnp.exp(gl) + _dot(kk * jnp.exp(gl - gc), vnew, TN)
        sall_ref[...] = s0
        tall_ref[...] = tm
        or_ref[...] = o
        og_ref[...] = (o * _head_rstd(o) * wn_ref[...] * _silu(z_ref[...])).astype(og_ref.dtype)

    blk = lambda off: pl.BlockSpec((C, hd), lambda n, h: (n, off + h))
    return _pcall(
        body, name=name, grid=(N, H),
        in_specs=[blk(0), blk(0), blk(0), blk(zoff),
                  pl.BlockSpec((C, LANES), lambda n, h: (n, 0)),
                  pl.BlockSpec((None, LANES, C), lambda n, h: (n, 0, 0)),
                  _fixed((1, hd))],
        out_specs=[blk(0), blk(0),
                   pl.BlockSpec((None, None, hd, hd), lambda n, h: (n, h, 0, 0)),
                   pl.BlockSpec((None, None, C, C), lambda n, h: (n, h, 0, 0))],
        out_shape=[jax.ShapeDtypeStruct((T, H * hd), BF), jax.ShapeDtypeStruct((T, H * hd), F32),
                   jax.ShapeDtypeStruct((N, H, hd, hd), F32), jax.ShapeDtypeStruct((N, H, C, C), F32)],
        scratch_shapes=[pltpu.VMEM((H, hd, hd), F32)],
        compiler_params=_cp(2))(q, k, v, pm, bg, gcrow, wn)


def gdn_bwd(q, k, v, pm, zoff, bg, gcrow, wn, oraw, sall, tall, dog, H, *, name):
    T = q.shape[0]
    C = LA_CHUNK
    N = T // C
    hd = HEAD_DIM

    def body(q_ref, k_ref, v_ref, z_ref, bg_ref, gr_ref, wn_ref, or_ref, sall_ref, tall_ref, dog_ref,
             dq_ref, dk_ref, dv_ref, dz_ref, dbg_ref, dwn_ref, dS):
        n = pl.program_id(0)
        h = pl.program_id(1)

        @pl.when(n == 0)
        def _():
            dS[h] = jnp.zeros((hd, hd), F32)

        @pl.when((n == 0) & (h == 0))
        def _():
            dwn_ref[...] = jnp.zeros_like(dwn_ref)

        @pl.when(h == 0)
        def _():
            dbg_ref[...] = jnp.zeros_like(dbg_ref)

        beta, gc, grow, gl = _gdn_gates(bg_ref, gr_ref, h, H)
        causal, strict, eye = _chunk_masks()
        dm = _decay(gc, grow, causal)
        qs = q_ref[...] * QK_SCALE
        kk = k_ref[...]
        vv = v_ref[...]
        zz = z_ref[...]
        o = or_ref[...]
        dog = dog_ref[...]
        wn_v = wn_ref[...]
        s0 = sall_ref[...]
        tm = tall_ref[...]

        rstd = _head_rstd(o)
        on = o * rstd
        sz = _silu(zz)
        don = dog * wn_v * sz
        dwn_ref[...] += jnp.sum(dog * on * sz, axis=0, keepdims=True)
        dz_ref[...] = (dog * on * wn_v * _dsilu(zz)).astype(dz_ref.dtype)
        do = rstd * (don - on * jnp.mean(don * on, axis=-1, keepdims=True))

        eg = jnp.exp(gc)
        kb = kk * beta
        vb = vv * beta
        kbg = kb * eg
        a = jnp.where(strict, _dot(kb, kk, NT) * dm, 0.0)
        u = _dot(tm, vb)
        w = _dot(tm, kbg)
        vnew = u - _dot(w, s0)
        qk = jnp.where(causal, _dot(qs, kk, NT) * dm, 0.0)
        qdec = qs * eg
        etail = jnp.exp(gl - gc)
        ktail = kk * etail
        egl = jnp.exp(gl)

        ds1 = dS[h]
        dvnew = _dot(qk, do, TN) + _dot(ktail, ds1)
        dqdec = _dot(do, s0, NT)
        dqk = jnp.where(causal, _dot(do, vnew, NT), 0.0)
        dktail = _dot(vnew, ds1, NT)
        dcd = jnp.sum(jnp.sum(s0 * ds1, axis=1, keepdims=True), axis=0, keepdims=True)
        dS[h] = egl * ds1 + _dot(qdec, do, TN) - _dot(w, dvnew, TN)
        dw = -_dot(dvnew, s0, NT)
        dvb = _dot(tm, dvnew, TN)
        dkbg = _dot(tm, dw, TN)
        dtm = _dot(dvnew, vb, NT) + _dot(dw, kbg, NT)
        da = jnp.where(strict, -_dot_hi(_dot_hi(tm, dtm, TN), tm, NT), 0.0)
        dkk = da * dm
        dkb = _dot(dkk, kk) + dkbg * eg
        dk = _dot(dkk, kb, TN)
        dqkr = dqk * dm
        dqs = _dot(dqkr, kk) + dqdec * eg
        dk = dk + _dot(dqkr, qs, TN) + dktail * etail + dkb * beta
        g = da * a + dqk * qk
        colsum = jnp.max(_dot_hi(g, jnp.ones((C, LANES), F32), TN), axis=1, keepdims=True)
        rk = jnp.sum(dktail * ktail, axis=1, keepdims=True)
        dgc = (jnp.sum(g, axis=1, keepdims=True) - colsum
               + jnp.sum(dqdec * qdec, axis=1, keepdims=True) - rk
               + jnp.sum(dkbg * kbg, axis=1, keepdims=True))
        dgl = jnp.sum(rk, axis=0, keepdims=True) + dcd * egl
        ri = lax.broadcasted_iota(jnp.int32, (C, 1), 0)
        dgc = dgc + jnp.where(ri == C - 1, dgl, 0.0)
        dbeta = jnp.sum(dkb * kk, axis=1, keepdims=True) + jnp.sum(dvb * vv, axis=1, keepdims=True)

        dq_ref[...] = dqs * QK_SCALE
        dk_ref[...] = dk
        dv_ref[...] = dvb * beta
        lane = lax.broadcasted_iota(jnp.int32, (C, LANES), 1)
        dbg_ref[...] += jnp.where(lane == h, dbeta, 0.0) + jnp.where(lane == H + h, dgc, 0.0)

    blk = lambda off: pl.BlockSpec((C, hd), lambda n, h: (N - 1 - n, off + h))
    st = lambda r: pl.BlockSpec((None, None, r, r), lambda n, h: (N - 1 - n, h, 0, 0))
    return _pcall(
        body, name=name, grid=(N, H),
        in_specs=[blk(0), blk(0), blk(0), blk(zoff),
                  pl.BlockSpec((C, LANES), lambda n, h: (N - 1 - n, 0)),
                  pl.BlockSpec((None, LANES, C), lambda n, h: (N - 1 - n, 0, 0)),
                  _fixed((1, hd)), blk(0), st(hd), st(C), blk(0)],
        out_specs=[blk(0), blk(0), blk(0), blk(0),
                   pl.BlockSpec((C, LANES), lambda n, h: (N - 1 - n, 0)), _fixed((1, hd))],
        out_shape=[jax.ShapeDtypeStruct((T, H * hd), F32)] * 3
        + [jax.ShapeDtypeStruct((T, H * hd), BF), jax.ShapeDtypeStruct((T, LANES), F32),
           jax.ShapeDtypeStruct((1, hd), F32)],
        scratch_shapes=[pltpu.VMEM((H, hd, hd), F32)],
        compiler_params=_cp(2))(q, k, v, pm, bg, gcrow, wn, oraw, sall, tall, dog)


def _rot(x, cs, sn):
    return x * cs + pltpu.roll(x, HEAD_DIM // 2, 1) * sn


def _rot_t(dy, cs, sn):
    return dy * cs + pltpu.roll(dy * sn, HEAD_DIM // 2, 1)


def ret_fwd(pm, qoff, koff, voff, goff, cs, sn, dmat, avec, bvec, gam, H, *, name):
    T = pm.shape[0]
    C = LA_CHUNK
    N = T // C
    hd = HEAD_DIM

    def body(q_ref, k_ref, v_ref, g_ref, cs_ref, sn_ref, dm_ref, a_ref, b_ref, gam_ref,
             og_ref, or_ref, sall_ref, S):
        n = pl.program_id(0)
        h = pl.program_id(1)

        @pl.when(n == 0)
        def _():
            S[h] = jnp.zeros((hd, hd), F32)

        c, s = cs_ref[...], sn_ref[...]
        qq = _rot(q_ref[...], c, s)
        kk = _rot(k_ref[...], c, s) * QK_SCALE
        vv = v_ref[...]
        s0 = S[h]
        p = _dot(qq, kk, NT) * dm_ref[...]
        o = _dot(p, vv) + _dot(qq * a_ref[...], s0)
        S[h] = s0 * gam_ref[...] + _dot(kk * b_ref[...], vv, TN)
        sall_ref[...] = s0
        or_ref[...] = o
        og_ref[...] = (_silu(g_ref[...]) * o * _head_rstd(o)).astype(og_ref.dtype)

    blk = lambda off: pl.BlockSpec((C, hd), lambda n, h: (n, off + h))
    tab = pl.BlockSpec((C, hd), lambda n, h: (n, 0))
    per_h = lambda r, cdim: pl.BlockSpec((None, r, cdim), lambda n, h: (h, 0, 0))
    return _pcall(
        body, name=name, grid=(N, H),
        in_specs=[blk(qoff), blk(koff), blk(voff), blk(goff), tab, tab,
                  per_h(C, C), per_h(C, hd), per_h(C, hd), per_h(1, hd)],
        out_specs=[blk(0), blk(0), pl.BlockSpec((None, None, hd, hd), lambda n, h: (n, h, 0, 0))],
        out_shape=[jax.ShapeDtypeStruct((T, H * hd), BF), jax.ShapeDtypeStruct((T, H * hd), F32),
                   jax.ShapeDtypeStruct((N, H, hd, hd), F32)],
        scratch_shapes=[pltpu.VMEM((H, hd, hd), F32)],
        compiler_params=_cp(2))(pm, pm, pm, pm, cs, sn, dmat, avec, bvec, gam)


def ret_bwd(pm, qoff, koff, voff, goff, cs, sn, dmat, avec, bvec, gam, oraw, sall, dog, dogoff, H, *, name):
    T = pm.shape[0]
    C = LA_CHUNK
    N = T // C
    hd = HEAD_DIM

    def body(q_ref, k_ref, v_ref, g_ref, cs_ref, sn_ref, dm_ref, a_ref, b_ref, gam_ref, or_ref, sall_ref,
             dog_ref, dq_ref, dk_ref, dv_ref, dg_ref, dS):
        n = pl.program_id(0)
        h = pl.program_id(1)

        @pl.when(n == 0)
        def _():
            dS[h] = jnp.zeros((hd, hd), F32)

        c, s = cs_ref[...], sn_ref[...]
        qq = _rot(q_ref[...], c, s)
        kk = _rot(k_ref[...], c, s) * QK_SCALE
        vv = v_ref[...]
        gg = g_ref[...]
        o = or_ref[...]
        dog = dog_ref[...]
        dm = dm_ref[...]
        av, bv = a_ref[...], b_ref[...]
        s0 = sall_ref[...]
        ds1 = dS[h]

        rstd = _head_rstd(o)
        on = o * rstd
        don = dog * _silu(gg)
        dg_ref[...] = (dog * on * _dsilu(gg)).astype(dg_ref.dtype)
        do = rstd * (don - on * jnp.mean(don * on, axis=-1, keepdims=True))

        p = _dot(qq, kk, NT) * dm
        dp = _dot(do, vv, NT) * dm
        dv_ref[...] = (_dot(p, do, TN) + _dot(kk * bv, ds1)).astype(dv_ref.dtype)
        dqq = _dot(dp, kk) + _dot(do, s0, NT) * av
        dkk = (_dot(dp, qq, TN) + _dot(vv, ds1, NT) * bv) * QK_SCALE
        dS[h] = ds1 * gam_ref[...] + _dot(qq * av, do, TN)
        dq_ref[...] = _rot_t(dqq, c, s).astype(dq_ref.dtype)
        dk_ref[...] = _rot_t(dkk, c, s).astype(dk_ref.dtype)

    blk = lambda off: pl.BlockSpec((C, hd), lambda n, h: (N - 1 - n, off + h))
    tab = pl.BlockSpec((C, hd), lambda n, h: (N - 1 - n, 0))
    per_h = lambda r, cdim: pl.BlockSpec((None, r, cdim), lambda n, h: (h, 0, 0))
    return _pcall(
        body, name=name, grid=(N, H),
        in_specs=[blk(qoff), blk(koff), blk(voff), blk(goff), tab, tab,
                  per_h(C, C), per_h(C, hd), per_h(C, hd), per_h(1, hd), blk(0),
                  pl.BlockSpec((None, None, hd, hd), lambda n, h: (N - 1 - n, h, 0, 0)), blk(dogoff)],
        out_specs=[blk(0)] * 4,
        out_shape=[jax.ShapeDtypeStruct((T, H * hd), BF)] * 4,
        scratch_shapes=[pltpu.VMEM((H, hd, hd), F32)],
        compiler_params=_cp(2))(pm, pm, pm, pm, cs, sn, dmat, avec, bvec, gam, oraw, sall, dog)


LN_ROWS = 128


def ln_fwd(pre, lw, lb, *, name):
    T, W2 = pre.shape
    W = W2 // 2
    tr = _tile(T, LN_ROWS, SUBLANES)

    def body(p_ref, w_ref, b_ref, o_ref):
        v = _gelu(p_ref[...])
        xc = v - jnp.mean(v, axis=-1, keepdims=True)
        r = lax.rsqrt(jnp.mean(xc * xc, axis=-1, keepdims=True) + EPS)
        o_ref[...] = xc * r * w_ref[...] + b_ref[...]

    return _pcall(body, name=name, grid=(T // tr,),
                  in_specs=[pl.BlockSpec((tr, W), lambda i: (i, 1)), _fixed((1, W)), _fixed((1, W))],
                  out_specs=_rows(tr, W), out_shape=jax.ShapeDtypeStruct((T, W), F32),
                  compiler_params=_cp(1))(pre, lw, lb)


def ln_bwd(pre, lw, dvn, *, name):
    T, W2 = pre.shape
    W = W2 // 2
    tr = _tile(T, LN_ROWS, SUBLANES)

    def body(p_ref, w_ref, d_ref, dp_ref, dw_ref, db_ref):
        i = pl.program_id(0)
        pv = p_ref[...]
        v = _gelu(pv)
        xc = v - jnp.mean(v, axis=-1, keepdims=True)
        r = lax.rsqrt(jnp.mean(xc * xc, axis=-1, keepdims=True) + EPS)
        xh = xc * r
        d = d_ref[...]
        dxh = d * w_ref[...]
        dv = r * (dxh - jnp.mean(dxh, axis=-1, keepdims=True) - xh * jnp.mean(dxh * xh, axis=-1, keepdims=True))
        dp_ref[...] = (dv * _dgelu(pv)).astype(dp_ref.dtype)
        pw = jnp.sum(d * xh, axis=0, keepdims=True)
        pb = jnp.sum(d, axis=0, keepdims=True)

        @pl.when(i == 0)
        def _():
            dw_ref[...] = pw
            db_ref[...] = pb

        @pl.when(i > 0)
        def _():
            dw_ref[...] += pw
            db_ref[...] += pb

    return _pcall(body, name=name, grid=(T // tr,),
                  in_specs=[pl.BlockSpec((tr, W), lambda i: (i, 1)), _fixed((1, W)), _rows(tr, W)],
                  out_specs=[_rows(tr, W), _fixed((1, W)), _fixed((1, W))],
                  out_shape=[jax.ShapeDtypeStruct((T, W), BF), jax.ShapeDtypeStruct((1, W), F32),
                             jax.ShapeDtypeStruct((1, W), F32)],
                  compiler_params=_cp(1))(pre, lw, dvn)


def _tril_mask(c):
    t = lax.broadcasted_iota(jnp.int32, (c, c), 0)
    s = lax.broadcasted_iota(jnp.int32, (c, c), 1)
    return t >= s


def sg_fwd(pre, vn, ws, bs3, *, name):
    T, W = vn.shape
    G = ws.shape[0]
    gd = W // G
    C = SG_CHUNK

    def body(p_ref, v_ref, w_ref, b_ref, o_ref):
        wm = jnp.where(_tril_mask(C), w_ref[...], 0.0)
        s = _dot(wm, v_ref[...]) + b_ref[...]
        o_ref[...] = (_gelu(p_ref[...]) * s).astype(o_ref.dtype)

    blk = pl.BlockSpec((C, gd), lambda n, g: (n, g))
    return _pcall(body, name=name, grid=(T // C, G),
                  in_specs=[blk, blk, pl.BlockSpec((None, C, C), lambda n, g: (g, 0, 0)),
                            pl.BlockSpec((None, C, 1), lambda n, g: (g, 0, 0))],
                  out_specs=blk, out_shape=jax.ShapeDtypeStruct((T, W), BF),
                  compiler_params=_cp(2))(pre, vn, ws, bs3)


def sg_bwd(pre, vn, ws, bs3, dus, *, name):
    T, W = vn.shape
    G = ws.shape[0]
    gd = W // G
    C = SG_CHUNK

    def body(p_ref, v_ref, w_ref, b_ref, d_ref, dp_ref, dv_ref, dw_ref, db_ref):
        n = pl.program_id(1)
        mask = _tril_mask(C)
        wm = jnp.where(mask, w_ref[...], 0.0)
        pv = p_ref[...]
        vv = v_ref[...]
        d = d_ref[...]
        s = _dot(wm, vv) + b_ref[...]
        ds = d * _gelu(pv)
        dp_ref[...] = (d * s * _dgelu(pv)).astype(dp_ref.dtype)
        dv_ref[...] = _dot(wm, ds, TN)
        pw = jnp.where(mask, _dot(ds, vv, NT), 0.0)
        pb = jnp.sum(ds, axis=1, keepdims=True)

        @pl.when(n == 0)
        def _():
            dw_ref[...] = pw
            db_ref[...] = pb

        @pl.when(n > 0)
        def _():
            dw_ref[...] += pw
            db_ref[...] += pb

    blk = pl.BlockSpec((C, gd), lambda g, n: (n, g))
    wsp = pl.BlockSpec((None, C, C), lambda g, n: (g, 0, 0))
    bsp = pl.BlockSpec((None, C, 1), lambda g, n: (g, 0, 0))
    return _pcall(body, name=name, grid=(G, T // C),
                  in_specs=[blk, blk, wsp, bsp, blk],
                  out_specs=[blk, blk, wsp, bsp],
                  out_shape=[jax.ShapeDtypeStruct((T, W), BF), jax.ShapeDtypeStruct((T, W), F32),
                             jax.ShapeDtypeStruct((G, C, C), F32), jax.ShapeDtypeStruct((G, C, 1), F32)],
                  compiler_params=_cp(2))(pre, vn, ws, bs3, dus)


CHIP_RELATIONS = ((1, 0), (0, 1), (1, 1))


def _place():
    return lax.axis_index("x"), lax.axis_index("y"), lax.axis_index("c")


def _peer_chip(x, y, r):
    fx, fy = CHIP_RELATIONS[r]
    return (1 - x if fx else x), (1 - y if fy else y)


def all_gather_halves(arrs, *, name):
    n = len(arrs)
    per = 2 * len(CHIP_RELATIONS)

    def body(*refs):
        ins, outs = refs[:n], refs[n:2 * n]
        send, recv, loc = refs[2 * n:2 * n + 3]
        x, y, c = _place()
        j = 2 * x + y
        sib = (x, y, 1 - c)
        started = []
        for a in range(n):
            lc = pltpu.make_async_copy(ins[a], outs[a].at[j], loc.at[a])
            lc.start()
            started.append(lc)
        sends = []
        for a in range(n):
            for r in range(3):
                px, py = _peer_chip(x, y, r)
                cp = pltpu.make_async_remote_copy(
                    src_ref=ins[a].at[c], dst_ref=outs[a].at[j, c], send_sem=send.at[a * per + r],
                    recv_sem=recv.at[a * per + r], device_id=(px, py, c), device_id_type=MESH)
                cp.start()
                sends.append(cp)
        for a in range(n):
            for r in range(3):
                px, py = _peer_chip(x, y, r)
                kk = 2 * px + py
                landed = outs[a].at[kk, c]
                pltpu.make_async_remote_copy(
                    src_ref=landed, dst_ref=landed, send_sem=send.at[a * per + r],
                    recv_sem=recv.at[a * per + r], device_id=(px, py, c), device_id_type=MESH).wait_recv()
                fw = pltpu.make_async_remote_copy(
                    src_ref=landed, dst_ref=landed, send_sem=send.at[a * per + 3 + r],
                    recv_sem=recv.at[a * per + 3 + r], device_id=sib, device_id_type=MESH)
                fw.start()
                sends.append(fw)
        for a in range(n):
            for r in range(3):
                px, py = _peer_chip(x, y, r)
                kk = 2 * px + py
                other = outs[a].at[kk, 1 - c]
                pltpu.make_async_remote_copy(
                    src_ref=other, dst_ref=other, send_sem=send.at[a * per + 3 + r],
                    recv_sem=recv.at[a * per + 3 + r], device_id=sib, device_id_type=MESH).wait_recv()
        for cp in sends:
            cp.wait_send()
        for lc in started:
            lc.wait()

    out_shape = [jax.ShapeDtypeStruct((N_CHIPS,) + a.shape, a.dtype) for a in arrs]
    res = _pcall(body, name=name, in_specs=[ANY] * n, out_specs=[ANY] * n, out_shape=out_shape,
                 scratch_shapes=[pltpu.SemaphoreType.DMA((n * per,)), pltpu.SemaphoreType.DMA((n * per,)),
                                 pltpu.SemaphoreType.DMA((n,))])(*arrs)
    return list(res)


def pair_exchange(gs, *, name):
    n = len(gs)

    def body(*refs):
        ins, outs = refs[:n], refs[n:2 * n]
        send, recv = refs[2 * n:2 * n + 2]
        x, y, c = _place()
        cps = []
        for a in range(n):
            cp = pltpu.make_async_remote_copy(
                src_ref=ins[a].at[:, pl.ds(1 - c, 1)], dst_ref=outs[a], send_sem=send.at[a], recv_sem=recv.at[a],
                device_id=(x, y, 1 - c), device_id_type=MESH)
            cp.start()
            cps.append(cp)
        for cp in cps:
            cp.wait()

    out_shape = [jax.ShapeDtypeStruct((g.shape[0], 1) + g.shape[2:], g.dtype) for g in gs]
    res = _pcall(body, name=name, in_specs=[ANY] * n, out_specs=[ANY] * n, out_shape=out_shape,
                 scratch_shapes=[pltpu.SemaphoreType.DMA((n,)), pltpu.SemaphoreType.DMA((n,))])(*gs)
    return list(res)


def chip_exchange(ps, *, name):
    n = len(ps)

    def body(*refs):
        ins, outs = refs[:n], refs[n:2 * n]
        send, recv = refs[2 * n:2 * n + 2]
        x, y, c = _place()
        cps = []
        for a in range(n):
            for r in range(3):
                px, py = _peer_chip(x, y, r)
                cp = pltpu.make_async_remote_copy(
                    src_ref=ins[a].at[2 * px + py], dst_ref=outs[a].at[r], send_sem=send.at[3 * a + r],
                    recv_sem=recv.at[3 * a + r], device_id=(px, py, c), device_id_type=MESH)
                cp.start()
                cps.append(cp)
        for cp in cps:
            cp.wait()

    out_shape = [jax.ShapeDtypeStruct((3,) + p.shape[1:], p.dtype) for p in ps]
    res = _pcall(body, name=name, in_specs=[ANY] * n, out_specs=[ANY] * n, out_shape=out_shape,
                 scratch_shapes=[pltpu.SemaphoreType.DMA((3 * n,)), pltpu.SemaphoreType.DMA((3 * n,))])(*ps)
    return list(res)


def pair_share(fs, *, name):
    n = len(fs)

    def body(*refs):
        ins, outs = refs[:n], refs[n:2 * n]
        send, recv, loc = refs[2 * n:2 * n + 3]
        x, y, c = _place()
        cps, lcs = [], []
        for a in range(n):
            lc = pltpu.make_async_copy(ins[a], outs[a].at[c], loc.at[a])
            lc.start()
            lcs.append(lc)
            cp = pltpu.make_async_remote_copy(
                src_ref=ins[a], dst_ref=outs[a].at[c], send_sem=send.at[a], recv_sem=recv.at[a],
                device_id=(x, y, 1 - c), device_id_type=MESH)
            cp.start()
            cps.append(cp)
        for a in range(n):
            cps[a].wait_send()
            other = outs[a].at[1 - c]
            pltpu.make_async_remote_copy(
                src_ref=other, dst_ref=other, send_sem=send.at[a], recv_sem=recv.at[a],
                device_id=(x, y, 1 - c), device_id_type=MESH).wait_recv()
            lcs[a].wait()

    out_shape = [jax.ShapeDtypeStruct((2,) + f.shape, f.dtype) for f in fs]
    res = _pcall(body, name=name, in_specs=[ANY] * n, out_specs=[ANY] * n, out_shape=out_shape,
                 scratch_shapes=[pltpu.SemaphoreType.DMA((n,)), pltpu.SemaphoreType.DMA((n,)),
                                 pltpu.SemaphoreType.DMA((n,))])(*fs)
    return list(res)


def all_reduce_small(v, *, name):
    R = v.shape[0]

    def body(v_ref, sum_ref, gat_ref, send, recv):
        x, y, c = _place()
        me = 4 * x + 2 * y + c
        gat_ref[me] = v_ref[...]
        cps = []
        peers = []
        for r in range(1, N_DEV):
            fx, fy, fc = (r >> 2) & 1, (r >> 1) & 1, r & 1
            px, py, pc = (1 - x if fx else x), (1 - y if fy else y), (1 - c if fc else c)
            peers.append((px, py, pc))
            cp = pltpu.make_async_remote_copy(
                src_ref=v_ref, dst_ref=gat_ref.at[me], send_sem=send.at[r - 1], recv_sem=recv.at[r - 1],
                device_id=(px, py, pc), device_id_type=MESH)
            cp.start()
            cps.append(cp)
        for r in range(1, N_DEV):
            px, py, pc = peers[r - 1]
            slot = gat_ref.at[4 * px + 2 * py + pc]
            pltpu.make_async_remote_copy(
                src_ref=v_ref, dst_ref=slot, send_sem=send.at[r - 1], recv_sem=recv.at[r - 1],
                device_id=(px, py, pc), device_id_type=MESH).wait_recv()
        for cp in cps:
            cp.wait_send()
        acc = gat_ref[0]
        for s in range(1, N_DEV):
            acc = acc + gat_ref[s]
        sum_ref[...] = acc

    vm = pl.BlockSpec(memory_space=pltpu.VMEM)
    res = _pcall(body, name=name, in_specs=[vm], out_specs=[vm, vm],
                 out_shape=[jax.ShapeDtypeStruct((R, LANES), F32), jax.ShapeDtypeStruct((N_DEV, R, LANES), F32)],
                 scratch_shapes=[pltpu.SemaphoreType.DMA((N_DEV - 1,)), pltpu.SemaphoreType.DMA((N_DEV - 1,))],
                 compiler_params=pltpu.CompilerParams(vmem_limit_bytes=VMEM_LIMIT))(v)
    return res[0]


def pair_sum(g, r1, c_idx, *, name):
    nb, _, hr, C = g.shape
    tr = _tile(hr, max(SUBLANES, (1 << 18) // C), SUBLANES)

    def body(c_ref, g_ref, r_ref, o_ref):
        o_ref[...] = g_ref[...] + r_ref[...]

    gs = pltpu.PrefetchScalarGridSpec(
        num_scalar_prefetch=1, grid=(nb, hr // tr),
        in_specs=[pl.BlockSpec((None, None, tr, C), lambda b, i, cr: (b, cr[0], i, 0)),
                  pl.BlockSpec((None, None, tr, C), lambda b, i, cr: (b, 0, i, 0))],
        out_specs=pl.BlockSpec((None, tr, C), lambda b, i, cr: (b, i, 0)))
    return _pcall(body, name=name, grid_spec=gs, out_shape=jax.ShapeDtypeStruct((nb, hr, C), F32),
                  compiler_params=_cp(2))(c_idx, g, r1)


def chip_sum(p, r2, j_idx, *, name):
    _, hr, C = p.shape
    tr = _tile(hr, max(SUBLANES, (1 << 18) // C), SUBLANES)

    def body(j_ref, p_ref, a_ref, b_ref, c_ref, o_ref):
        o_ref[...] = ((p_ref[...] + a_ref[...]) + b_ref[...]) + c_ref[...]

    rel = lambda r: pl.BlockSpec((None, tr, C), lambda i, jr: (r, i, 0))
    gs = pltpu.PrefetchScalarGridSpec(
        num_scalar_prefetch=1, grid=(hr // tr,),
        in_specs=[pl.BlockSpec((None, tr, C), lambda i, jr: (jr[0], i, 0)), rel(0), rel(1), rel(2)],
        out_specs=pl.BlockSpec((tr, C), lambda i, jr: (i, 0)))
    return _pcall(body, name=name, grid_spec=gs, out_shape=jax.ShapeDtypeStruct((hr, C), F32),
                  compiler_params=_cp(1))(j_idx, p, r2, r2, r2)


def _pack_rows(arrs):
    parts = []
    for a in arrs:
        flat = a.reshape(-1).astype(F32)
        tile = SUBLANES * LANES
        pad = (-flat.shape[0]) % tile
        parts.append(jnp.pad(flat, (0, pad)).reshape(-1, LANES))
    return jnp.concatenate(parts, axis=0)


def _unpack_rows(buf, shapes):
    out, row = [], 0
    for shp in shapes:
        size = int(np.prod(shp))
        rows = -(-size // (SUBLANES * LANES)) * SUBLANES
        out.append(buf[row:row + rows].reshape(-1)[:size].reshape(shp))
        row += rows
    return out


def _halves(a2d):
    r, c = a2d.shape
    return a2d.reshape(2, r // 2, c)


def _rotary_tables(T):
    half = HEAD_DIM // 2
    pos = jnp.arange(T, dtype=F32)
    inv_freq = 1.0 / (ROPE_BASE ** jnp.linspace(0.0, 1.0, half, dtype=F32))
    ang = pos[:, None] * inv_freq[None, :]
    cos, sin = jnp.cos(ang), jnp.sin(ang)
    return jnp.concatenate([cos, cos], axis=1), jnp.concatenate([-sin, sin], axis=1)


def _retention_tables(H):
    C = LA_CHUNK
    lg = jnp.log1p(-jnp.power(2.0, -5.0 - jnp.arange(H, dtype=F32)))
    pos = jnp.arange(C, dtype=F32)
    causal = jnp.tril(jnp.ones((C, C), dtype=bool))
    dmat = jnp.exp(jnp.where(causal, (pos[:, None] - pos[None, :]) * lg[:, None, None], -jnp.inf))
    bc = lambda t: jnp.broadcast_to(t[..., None], t.shape + (HEAD_DIM,))
    avec = bc(jnp.exp((pos + 1.0)[None, :] * lg[:, None]))
    bvec = bc(jnp.exp((C - 1.0 - pos)[None, :] * lg[:, None]))
    gam = bc(jnp.exp(C * lg)[:, None])
    return dmat, avec, bvec, gam


def _relu2(acc):
    return acc, jnp.square(jnp.maximum(acc, 0.0))


def _drelu2(acc, up):
    return (acc * (2.0 * jnp.maximum(up, 0.0)),)


def _add(acc, e):
    return (acc + e,)


def _ffn_fwd(y, w_up3, w_dn, tag):
    up, act = mm_nn(y, w_up3, name=f"ffn_up_{tag}", out_dtypes=(F32, BF), epilogue=_relu2)
    dn = mm_nn(act, w_dn, name=f"ffn_down_{tag}")
    return up, act, dn


def _ffn_bwd(y, up, act, ddn, w_up3, w_dn, tag):
    g_dn = mm_tn(act, ddn, name=f"ffn_dwdown_{tag}")
    dup = mm_nt(ddn, w_dn, name=f"ffn_dup_{tag}", out_dtypes=(BF,), epilogue=_drelu2, extras=(up,))
    g_up = mm_tn(y, dup, name=f"ffn_dwup_{tag}", shards=N_CHIPS)
    dy = mm_nt(dup, w_up3, name=f"ffn_dy_{tag}")
    return dy, g_up, g_dn


def _train_local(x2, tgt, W):
    T, D = x2.shape
    H = W["a_log"].shape[0]
    nw = W["norm_w"]
    row = lambda v: v.reshape(1, -1).astype(F32)

    y0 = rms_fwd(x2, row(nw[0, 0]), name="norm00")
    pm = mm_nn(y0, W["la_in_main"], name="la_in_main")
    pg = mm_nn(y0, W["la_in_gate"], name="la_in_gate")
    wc8 = jnp.pad(jnp.transpose(W["conv_w"]), ((0, SUBLANES - CONV_WIDTH), (0, 0)))
    lanes_pad = (H, LANES - 2 * H)
    arow = jnp.pad(W["a_log"], lanes_pad).reshape(1, LANES)
    dtrow = jnp.pad(W["dt_bias"], lanes_pad).reshape(1, LANES)
    bg, gcrow = gates_fwd(pg, arow, dtrow, H, name="gates_fwd")
    q = prep_fwd(pm, 0, wc8, 0, H, True, name="prep_q")
    k = prep_fwd(pm, H, wc8, H, H, True, name="prep_k")
    v = prep_fwd(pm, 2 * H, wc8, 2 * H, H, False, name="prep_v")
    wn = row(W["out_norm_w"])
    og_a, or_a, sall_a, tall = gdn_fwd(q, k, v, pm, 3 * H, bg, gcrow, wn, H, name="gdn_fwd")
    cs, sn = _rotary_tables(T)
    dmat, avec, bvec, gam = _retention_tables(H)
    og_b, or_b, sall_b = ret_fwd(pm, 4 * H, 5 * H, 6 * H, 7 * H, cs, sn, dmat, avec, bvec, gam, H, name="ret_fwd")
    ocat = jnp.concatenate([og_a, og_b], axis=1)
    mix = mm_nn(ocat, W["la_out"], name="la_out")
    h1, y2 = res_norm(x2, mix, row(nw[0, 1]), row(nw[0, 2]), name="resnorm_0a")
    up, act, dn = _ffn_fwd(y2, W["ffn_up"][0], W["ffn_down"][0], "0")
    h2, y0b = res_norm(h1, dn, row(nw[0, 3]), row(nw[1, 0]), name="resnorm_0b")

    pre = mm_nn(y0b, W["sg_in"], name="sg_in")
    lw, lb = row(W["ln_w"]), row(W["ln_b"])
    vn = ln_fwd(pre, lw, lb, name="sg_ln")
    ws = W["w_s"]
    bs3 = W["b_s"][:, :, None]
    us = sg_fwd(pre, vn, ws, bs3, name="sg_gate")
    mix1 = mm_nn(us, W["sg_out"], name="sg_out")
    h3, y2b = res_norm(h2, mix1, row(nw[1, 1]), row(nw[1, 2]), name="resnorm_1a")
    up1, act1, dn1 = _ffn_fwd(y2b, W["ffn_up"][1], W["ffn_down"][1], "1")
    h4, _ = res_norm(h3, dn1, row(nw[1, 3]), None, name="resnorm_1b")
    dh4, lrow = loss_head(h4, tgt, name="loss_head")
    loss = lrow[0, 0]

    G = {}
    dnw = [[None] * 4 for _ in range(2)]
    ddn1, dnw[1][3] = rms_bwd(dn1, row(nw[1, 3]), dh4, None, name="dnorm13", out_dtype=BF)
    dy2b, G["ffn_up1"], G["ffn_down1"] = _ffn_bwd(y2b, up1, act1, ddn1, W["ffn_up"][1], W["ffn_down"][1], "1")
    dh3, dnw[1][2] = rms_bwd(h3, row(nw[1, 2]), dy2b, dh4, name="dnorm12", out_dtype=F32)
    dmix1, dnw[1][1] = rms_bwd(mix1, row(nw[1, 1]), dh3, None, name="dnorm11", out_dtype=BF)
    G["sg_out"] = mm_tn(us, dmix1, name="sg_dwout")
    dus = mm_nt(dmix1, W["sg_out"], name="sg_dus")
    dpre_u, dvn, G["w_s"], dbs3 = sg_bwd(pre, vn, ws, bs3, dus, name="sg_gate_bwd")
    G["b_s"] = dbs3[:, :, 0]
    dpre_v, dlw, dlb = ln_bwd(pre, lw, dvn, name="sg_ln_bwd")
    G["ln_w"], G["ln_b"] = dlw[0], dlb[0]
    dpre = jnp.concatenate([dpre_u, dpre_v], axis=1)
    G["sg_in"] = mm_tn(y0b, dpre, name="sg_dwin", shards=N_CHIPS)
    dy0b = mm_nt(dpre, W["sg_in"], name="sg_dy")
    dh2, dnw[1][0] = rms_bwd(h2, row(nw[1, 0]), dy0b, dh3, name="dnorm10", out_dtype=F32)

    ddn, dnw[0][3] = rms_bwd(dn, row(nw[0, 3]), dh2, None, name="dnorm03", out_dtype=BF)
    dy2, G["ffn_up0"], G["ffn_down0"] = _ffn_bwd(y2, up, act, ddn, W["ffn_up"][0], W["ffn_down"][0], "0")
    dh1, dnw[0][2] = rms_bwd(h1, row(nw[0, 2]), dy2, dh2, name="dnorm02", out_dtype=F32)
    dmix, dnw[0][1] = rms_bwd(mix, row(nw[0, 1]), dh1, None, name="dnorm01", out_dtype=BF)
    G["la_out"] = mm_tn(ocat, dmix, name="la_dwout")
    docat = mm_nt(dmix, W["la_out"], name="la_docat")
    dq, dk, dv, dz, dbg, dwn = gdn_bwd(q, k, v, pm, 3 * H, bg, gcrow, wn, or_a, sall_a, tall, docat, H,
                                       name="gdn_bwd")
    drq, drk, drv, drg = ret_bwd(pm, 4 * H, 5 * H, 6 * H, 7 * H, cs, sn, dmat, avec, bvec, gam, or_b, sall_b,
                                 docat, H, H, name="ret_bwd")
    dpg, da, ddt = gates_bwd(pg, arow, dtrow, dbg, H, name="gates_bwd")
    dcq, dwq = prep_bwd_act(pm, 0, wc8, 0, H, True, dq, name="prep_dq")
    dck, dwk = prep_bwd_act(pm, H, wc8, H, H, True, dk, name="prep_dk")
    dcv, dwv = prep_bwd_act(pm, 2 * H, wc8, 2 * H, H, False, dv, name="prep_dv")
    dxq = prep_bwd_conv(dcq, wc8, 0, H, name="conv_dq")
    dxk = prep_bwd_conv(dck, wc8, H, H, name="conv_dk")
    dxv = prep_bwd_conv(dcv, wc8, 2 * H, H, name="conv_dv")
    dpm = jnp.concatenate([dxq, dxk, dxv, dz, drq, drk, drv, drg], axis=1)
    G["la_in_main"] = mm_tn(y0, dpm, name="la_dwin_main")
    G["la_in_gate"] = mm_tn(y0, dpg, name="la_dwin_gate")
    dy0 = mm_nt(dpm, W["la_in_main"], name="la_dy_main")
    dy0 = mm_nt(dpg, W["la_in_gate"], name="la_dy_gate", epilogue=_add, extras=(dy0,))
    dx, dnw[0][0] = rms_bwd(x2, row(nw[0, 0]), dy0, dh1, name="dnorm00", out_dtype=F32)

    G["norm_w"] = jnp.stack([jnp.concatenate(r, axis=0) for r in dnw], axis=0)
    G["conv_w"] = jnp.transpose(jnp.concatenate([dwq, dwk, dwv], axis=1)[:CONV_WIDTH])
    G["a_log"] = da[0, H:2 * H]
    G["dt_bias"] = ddt[0, H:2 * H]
    G["out_norm_w"] = dwn[0]
    return loss, dx, G


def _as2d(a):
    n = int(np.prod(a.shape))
    if a.shape[-1] < LANES and n % LANES == 0:
        return a.reshape(-1, LANES)
    return a.reshape(-1, a.shape[-1])


def _adamw_any(w, g, m, v, name):
    shp = w.shape
    d, m2, v2 = adamw(_as2d(w), _as2d(g.reshape(shp)), _as2d(m), _as2d(v), name=name)
    return g.reshape(shp), d.reshape(shp), m2.reshape(shp), v2.reshape(shp)


def kernel(x, norm_w, la_w_in, la_conv_w, la_a_log, la_dt_bias, la_out_norm_w, la_w_out, sg_w_in, sg_ln_w, sg_ln_b, sg_w_s, sg_b_s, sg_w_out, ffn_w_up, ffn_w_down, loss_target, m_norm_w, m_la_w_in, m_la_conv_w, m_la_a_log, m_la_dt_bias, m_la_out_norm_w, m_la_w_out, m_sg_w_in, m_sg_ln_w, m_sg_ln_b, m_sg_w_s, m_sg_b_s, m_sg_w_out, m_ffn_w_up, m_ffn_w_down, v_norm_w, v_la_w_in, v_la_conv_w, v_la_a_log, v_la_dt_bias, v_la_out_norm_w, v_la_w_out, v_sg_w_in, v_sg_ln_w, v_sg_ln_b, v_sg_w_s, v_sg_b_s, v_sg_w_out, v_ffn_w_up, v_ffn_w_down):
    weights = dict(norm_w=norm_w, la_w_in=la_w_in, la_conv_w=la_conv_w, la_a_log=la_a_log, la_dt_bias=la_dt_bias,
                   la_out_norm_w=la_out_norm_w, la_w_out=la_w_out, sg_w_in=sg_w_in, sg_ln_w=sg_ln_w,
                   sg_ln_b=sg_ln_b, sg_w_s=sg_w_s, sg_b_s=sg_b_s, sg_w_out=sg_w_out, ffn_w_up=ffn_w_up,
                   ffn_w_down=ffn_w_down)
    mom_m = dict(norm_w=m_norm_w, la_w_in=m_la_w_in, la_conv_w=m_la_conv_w, la_a_log=m_la_a_log,
                 la_dt_bias=m_la_dt_bias, la_out_norm_w=m_la_out_norm_w, la_w_out=m_la_w_out, sg_w_in=m_sg_w_in,
                 sg_ln_w=m_sg_ln_w, sg_ln_b=m_sg_ln_b, sg_w_s=m_sg_w_s, sg_b_s=m_sg_b_s, sg_w_out=m_sg_w_out,
                 ffn_w_up=m_ffn_w_up, ffn_w_down=m_ffn_w_down)
    mom_v = dict(norm_w=v_norm_w, la_w_in=v_la_w_in, la_conv_w=v_la_conv_w, la_a_log=v_la_a_log,
                 la_dt_bias=v_la_dt_bias, la_out_norm_w=v_la_out_norm_w, la_w_out=v_la_w_out, sg_w_in=v_sg_w_in,
                 sg_ln_w=v_sg_ln_w, sg_ln_b=v_sg_ln_b, sg_w_s=v_sg_w_s, sg_b_s=v_sg_b_s, sg_w_out=v_sg_w_out,
                 ffn_w_up=v_ffn_w_up, ffn_w_down=v_ffn_w_down)
    order = list(weights)

    T, D = x.shape[1], x.shape[2]
    H = la_a_log.shape[1]
    HD = H * HEAD_DIM
    xi, yi, ci = _place()
    chip = 2 * xi + yi
    c_idx = jnp.reshape(ci, (1,)).astype(jnp.int32)
    j_idx = jnp.reshape(chip, (1,)).astype(jnp.int32)

    big = [la_w_in[0], la_w_out[0], sg_w_in[0], sg_w_out[0], ffn_w_up[0], ffn_w_up[1], ffn_w_down[0], ffn_w_down[1]]
    small_shapes = [norm_w.shape, la_conv_w[0].shape, sg_ln_w[0].shape, sg_ln_b[0].shape]
    small = _pack_rows([norm_w, la_conv_w[0], sg_ln_w[0], sg_ln_b[0]])
    small = jnp.pad(small, ((0, (-small.shape[0]) % (2 * SUBLANES)), (0, 0)))
    gathered = all_gather_halves([_halves(w.astype(BF)) for w in big] + [_halves(small)], name="gather_weights")
    whole = [g.reshape(N_CHIPS, g.shape[1] * g.shape[2], g.shape[3]) for g in gathered]
    la_in_g, la_out_g, sg_in_g, sg_out_g, up0_g, up1_g, dn0_g, dn1_g, small_g = whole
    pieces = [_unpack_rows(small_g[kk], small_shapes) for kk in range(N_CHIPS)]
    la_nat = jnp.transpose(la_in_g, (1, 0, 2)).reshape(D, -1)
    rows_of = lambda g: g.reshape(-1, g.shape[-1])
    W = dict(
        norm_w=jnp.concatenate([p[0] for p in pieces], axis=-1),
        conv_w=jnp.concatenate([p[1] for p in pieces], axis=0),
        ln_w=jnp.concatenate([p[2] for p in pieces], axis=0),
        ln_b=jnp.concatenate([p[3] for p in pieces], axis=0),
        a_log=la_a_log[0], dt_bias=la_dt_bias[0], out_norm_w=la_out_norm_w[0], w_s=sg_w_s[0], b_s=sg_b_s[0],
        la_in_main=jnp.concatenate([la_nat[:, :4 * HD], la_nat[:, 4 * HD + 2 * H:]], axis=1),
        la_in_gate=jnp.pad(la_nat[:, 4 * HD:4 * HD + 2 * H], ((0, 0), (0, LANES - 2 * H))),
        la_out=rows_of(la_out_g), sg_in=sg_in_g, sg_out=rows_of(sg_out_g),
        ffn_up=[up0_g, up1_g], ffn_down=[rows_of(dn0_g), rows_of(dn1_g)],
    )

    loss_local, dx, G = _train_local(x[0], loss_target[0], W)
    loss = lax.psum(loss_local, ("x", "y", "c"))

    g_la_nat = jnp.concatenate([G["la_in_main"][:, :4 * HD], G["la_in_gate"][:, :2 * H], G["la_in_main"][:, 4 * HD:]],
                               axis=1)
    g_la_in = jnp.transpose(g_la_nat.reshape(D, N_CHIPS, -1), (1, 0, 2))
    by_chip = lambda g: g.reshape(N_CHIPS, -1, g.shape[-1])
    g_big = [g_la_in, by_chip(G["la_out"]), G["sg_in"], by_chip(G["sg_out"]), G["ffn_up0"], G["ffn_up1"],
             by_chip(G["ffn_down0"]), by_chip(G["ffn_down1"])]
    g_big = [g.reshape(N_CHIPS, 2, g.shape[1] // 2, g.shape[2]) for g in g_big]
    sib = pair_exchange(g_big, name="grads_pair_exchange")
    part = [pair_sum(g, r, c_idx, name=f"grads_pair_sum_{a}") for a, (g, r) in enumerate(zip(g_big, sib))]
    others = chip_exchange(part, name="grads_chip_exchange")
    done = [chip_sum(p, r, j_idx, name=f"grads_chip_sum_{a}") for a, (p, r) in enumerate(zip(part, others))]
    shards = pair_share(done, name="grads_pair_share")
    shards = [s.reshape(-1, s.shape[-1]) for s in shards]

    small_names = ["norm_w", "conv_w", "ln_w", "ln_b", "a_log", "dt_bias", "out_norm_w", "w_s", "b_s"]
    small_full = [G[nm] for nm in small_names]
    summed = _unpack_rows(all_reduce_small(_pack_rows(small_full), name="grads_all_reduce_small"),
                          [g.shape for g in small_full])
    sm = dict(zip(small_names, summed))
    own = lambda full, axis: lax.dynamic_slice_in_dim(full, chip * (full.shape[axis] // N_CHIPS),
                                                      full.shape[axis] // N_CHIPS, axis)
    grads = dict(
        norm_w=own(sm["norm_w"], 2), la_w_in=shards[0], la_conv_w=own(sm["conv_w"], 0), la_a_log=sm["a_log"],
        la_dt_bias=sm["dt_bias"], la_out_norm_w=sm["out_norm_w"], la_w_out=shards[1], sg_w_in=shards[2],
        sg_ln_w=own(sm["ln_w"], 0), sg_ln_b=own(sm["ln_b"], 0), sg_w_s=sm["w_s"], sg_b_s=sm["b_s"],
        sg_w_out=shards[3], ffn_w_up=jnp.stack([shards[4], shards[5]]), ffn_w_down=jnp.stack([shards[6], shards[7]]),
    )

    res = {nm: _adamw_any(weights[nm], grads[nm], mom_m[nm], mom_v[nm], f"adamw_{nm}") for nm in order}
    return (loss, dx.reshape(x.shape), *[res[nm][0] for nm in order], *[res[nm][1] for nm in order],
            *[res[nm][2] for nm in order], *[res[nm][3] for nm in order])
```

```python
import functools
import math

import numpy as np
import jax
import jax.numpy as jnp
from jax import lax
from jax.experimental import pallas as pl
from jax.experimental.pallas import tpu as pltpu

F32 = jnp.float32
BF = jnp.bfloat16
HI = lax.Precision.HIGHEST

V7X_VMEM_BYTES = 64 * 1024 * 1024
VMEM_LIMIT = (V7X_VMEM_BYTES * 3) // 4
LANES = 128
SUBLANES = 8
HEAD_DIM = 128
LA_CHUNK = 64
SG_CHUNK = 128
CONV_WIDTH = 4
ROPE_BASE = 10000.0
EPS = 1e-6
L2_EPS = 1e-6
N_CHIPS = 4
N_DEV = 8

ADAM_LR = 0.001
ADAM_B1 = 0.9
ADAM_B2 = 0.999
ADAM_EPS = 1e-08
ADAM_WD = 0.01
ADAM_STEP = 10

MESH = pl.DeviceIdType.MESH
ANY = pl.BlockSpec(memory_space=pl.ANY)

NN = (((1,), (0,)), ((), ()))
NT = (((1,), (1,)), ((), ()))
TN = (((0,), (0,)), ((), ()))


def _pcall(body, **kw):
    return pl.pallas_call(body, **kw)


def _cp(n_axes):
    return pltpu.CompilerParams(dimension_semantics=("arbitrary",) * n_axes, vmem_limit_bytes=VMEM_LIMIT)


def _tile(n, pref, unit=LANES):
    if n <= pref:
        return n
    t = (pref // unit) * unit
    while t >= unit:
        if n % t == 0:
            return t
        t -= unit
    return n


def _dot(a, b, dims=NN):
    return lax.dot_general(a.astype(BF), b.astype(BF), dims, preferred_element_type=F32)


def _dot_hi(a, b, dims=NN):
    return lax.dot_general(a.astype(F32), b.astype(F32), dims, precision=HI, preferred_element_type=F32)


def _sigmoid(x):
    return 1.0 / (1.0 + jnp.exp(-x))


def _silu(x):
    return x * _sigmoid(x)


def _dsilu(x):
    s = _sigmoid(x)
    return s * (1.0 + x * (1.0 - s))


GELU_C = math.sqrt(2.0 / math.pi)
GELU_A = 0.044715


def _gelu(x):
    return 0.5 * x * (1.0 + jnp.tanh(GELU_C * (x + GELU_A * x * x * x)))


def _dgelu(x):
    t = jnp.tanh(GELU_C * (x + GELU_A * x * x * x))
    return 0.5 * (1.0 + t) + 0.5 * x * (1.0 - t * t) * GELU_C * (1.0 + 3.0 * GELU_A * x * x)


def _matmul(a, b, *, dims, grid, a_spec, b_spec, out_shape, out_spec, acc_shape, name,
            epilogue=None, extras=(), extra_specs=()):
    nk = grid[2]
    outs = tuple(out_shape) if isinstance(out_shape, (tuple, list)) else (out_shape,)
    out_specs = tuple(out_spec) if isinstance(out_spec, (tuple, list)) else (out_spec,)
    n_ex, n_out = len(extras), len(outs)

    def body(*refs):
        a_ref, b_ref = refs[0], refs[1]
        ex = refs[2:2 + n_ex]
        o = refs[2 + n_ex:2 + n_ex + n_out]
        k = pl.program_id(2)
        part = lax.dot_general(a_ref[...].astype(BF), b_ref[...].astype(BF), dims, preferred_element_type=F32)

        def finish(val):
            res = epilogue(val, *[e[...] for e in ex]) if epilogue is not None else (val,)
            for r, oref in zip(res, o):
                oref[...] = r.astype(oref.dtype)

        if nk == 1:
            finish(part)
        else:
            acc = refs[2 + n_ex + n_out]

            @pl.when(k == 0)
            def _():
                acc[...] = part

            @pl.when(k > 0)
            def _():
                acc[...] += part

            @pl.when(k == nk - 1)
            def _():
                finish(acc[...])

    res = _pcall(
        body, name=name, grid=grid,
        in_specs=[a_spec, b_spec, *extra_specs],
        out_specs=out_specs if len(outs) > 1 else out_specs[0],
        out_shape=outs if len(outs) > 1 else outs[0],
        scratch_shapes=[pltpu.VMEM(acc_shape, F32)] if nk > 1 else [],
        compiler_params=_cp(3),
    )(a, b, *extras)
    return res


def mm_nn(a, w, *, name, out_dtypes=(F32,), epilogue=None, extras=(), tm=1024, tn=1024, tk=2048):
    M, K = a.shape
    if w.ndim == 3:
        S, _, Ns = w.shape
        N = S * Ns
    else:
        S, Ns = 1, w.shape[1]
        N = Ns
    tm, tn, tk = _tile(M, tm), _tile(Ns, tn), _tile(K, tk)
    npb = Ns // tn
    grid = (M // tm, N // tn, K // tk)
    a_spec = pl.BlockSpec((tm, tk), lambda i, j, k: (i, k))
    if w.ndim == 3:
        b_spec = pl.BlockSpec((None, tk, tn), lambda i, j, k: (j // npb, k, j % npb))
    else:
        b_spec = pl.BlockSpec((tk, tn), lambda i, j, k: (k, j))
    o_spec = pl.BlockSpec((tm, tn), lambda i, j, k: (i, j))
    outs = tuple(jax.ShapeDtypeStruct((M, N), d) for d in out_dtypes)
    res = _matmul(a, w, dims=NN, grid=grid, a_spec=a_spec, b_spec=b_spec,
                  out_shape=outs, out_spec=(o_spec,) * len(outs), acc_shape=(tm, tn), name=name,
                  epilogue=epilogue, extras=extras, extra_specs=(o_spec,) * len(extras))
    return res


def mm_nt(a, w, *, name, out_dtypes=(F32,), epilogue=None, extras=(), tm=1024, tn=1024, tk=2048):
    M, Kc = a.shape
    if w.ndim == 3:
        S, Nout, Ks = w.shape
    else:
        S, (Nout, Ks) = 1, w.shape
    assert S * Ks == Kc
    tm, tn, tk = _tile(M, tm), _tile(Nout, tn), _tile(Ks, tk)
    kpb = Ks // tk
    grid = (M // tm, Nout // tn, Kc // tk)
    a_spec = pl.BlockSpec((tm, tk), lambda i, j, k: (i, k))
    if w.ndim == 3:
        b_spec = pl.BlockSpec((None, tn, tk), lambda i, j, k: (k // kpb, j, k % kpb))
    else:
        b_spec = pl.BlockSpec((tn, tk), lambda i, j, k: (j, k))
    o_spec = pl.BlockSpec((tm, tn), lambda i, j, k: (i, j))
    outs = tuple(jax.ShapeDtypeStruct((M, Nout), d) for d in out_dtypes)
    return _matmul(a, w, dims=NT, grid=grid, a_spec=a_spec, b_spec=b_spec,
                   out_shape=outs, out_spec=(o_spec,) * len(outs), acc_shape=(tm, tn), name=name,
                   epilogue=epilogue, extras=extras, extra_specs=(o_spec,) * len(extras))


def mm_tn(x, dy, *, name, shards=1, tm=1024, tn=1024, tk=2048):
    T, Kin = x.shape
    N = dy.shape[1]
    Ns = N // shards
    tm, tn, tk = _tile(Kin, tm), _tile(Ns, tn), _tile(T, tk)
    npb = Ns // tn
    grid = (Kin // tm, N // tn, T // tk)
    a_spec = pl.BlockSpec((tk, tm), lambda i, j, k: (k, i))
    b_spec = pl.BlockSpec((tk, tn), lambda i, j, k: (k, j))
    if shards > 1:
        o_spec = pl.BlockSpec((None, tm, tn), lambda i, j, k: (j // npb, i, j % npb))
        out = jax.ShapeDtypeStruct((shards, Kin, Ns), F32)
    else:
        o_spec = pl.BlockSpec((tm, tn), lambda i, j, k: (i, j))
        out = jax.ShapeDtypeStruct((Kin, N), F32)
    return _matmul(x, dy, dims=TN, grid=grid, a_spec=a_spec, b_spec=b_spec,
                   out_shape=out, out_spec=o_spec, acc_shape=(tm, tn), name=name)


ROW_TILE = 256


def _rows(tr, d):
    return pl.BlockSpec((tr, d), lambda i: (i, 0))


def _fixed(shape):
    nd = len(shape)
    return pl.BlockSpec(shape, lambda *_: (0,) * nd)


def _rms(xv, w):
    r = lax.rsqrt(jnp.mean(xv * xv, axis=-1, keepdims=True) + EPS)
    return xv * r * w


def rms_fwd(x, w, *, name):
    T, D = x.shape
    tr = _tile(T, ROW_TILE, SUBLANES)

    def body(x_ref, w_ref, y_ref):
        y_ref[...] = _rms(x_ref[...], w_ref[...]).astype(y_ref.dtype)

    return _pcall(body, name=name, grid=(T // tr,), in_specs=[_rows(tr, D), _fixed((1, D))],
                  out_specs=_rows(tr, D), out_shape=jax.ShapeDtypeStruct((T, D), BF), compiler_params=_cp(1))(x, w)


def res_norm(h, m, wa, wb, *, name):
    T, D = h.shape
    tr = _tile(T, ROW_TILE, SUBLANES)
    second = wb is not None

    def body(*refs):
        if second:
            h_ref, m_ref, wa_ref, wb_ref, ho_ref, y_ref = refs
        else:
            h_ref, m_ref, wa_ref, ho_ref = refs
        ho = h_ref[...] + _rms(m_ref[...], wa_ref[...])
        ho_ref[...] = ho
        if second:
            y_ref[...] = _rms(ho, wb_ref[...]).astype(y_ref.dtype)

    ins = [h, m, wa] + ([wb] if second else [])
    in_specs = [_rows(tr, D), _rows(tr, D), _fixed((1, D))] + ([_fixed((1, D))] if second else [])
    out_shape = [jax.ShapeDtypeStruct((T, D), F32)] + ([jax.ShapeDtypeStruct((T, D), BF)] if second else [])
    out_specs = [_rows(tr, D)] * len(out_shape)
    res = _pcall(body, name=name, grid=(T // tr,), in_specs=in_specs, out_specs=out_specs,
                 out_shape=out_shape, compiler_params=_cp(1))(*ins)
    return tuple(res) if second else (res[0], None)


def rms_bwd(x, w, dy, dres, *, name, out_dtype):
    T, D = x.shape
    tr = _tile(T, ROW_TILE, SUBLANES)
    has_res = dres is not None

    def body(*refs):
        if has_res:
            x_ref, w_ref, dy_ref, dr_ref, dx_ref, dw_ref = refs
        else:
            x_ref, w_ref, dy_ref, dx_ref, dw_ref = refs
        i = pl.program_id(0)
        xv = x_ref[...]
        r = lax.rsqrt(jnp.mean(xv * xv, axis=-1, keepdims=True) + EPS)
        xh = xv * r
        dyv = dy_ref[...].astype(F32)
        dyw = dyv * w_ref[...]
        dx = r * (dyw - xh * jnp.mean(dyw * xh, axis=-1, keepdims=True))
        if has_res:
            dx = dx + dr_ref[...].astype(F32)
        dx_ref[...] = dx.astype(dx_ref.dtype)
        part = jnp.sum(dyv * xh, axis=0, keepdims=True)

        @pl.when(i == 0)
        def _():
            dw_ref[...] = part

        @pl.when(i > 0)
        def _():
            dw_ref[...] += part

    ins = [x, w, dy] + ([dres] if has_res else [])
    in_specs = [_rows(tr, D), _fixed((1, D)), _rows(tr, D)] + ([_rows(tr, D)] if has_res else [])
    return _pcall(body, name=name, grid=(T // tr,), in_specs=in_specs,
                  out_specs=[_rows(tr, D), _fixed((1, D))],
                  out_shape=[jax.ShapeDtypeStruct((T, D), out_dtype), jax.ShapeDtypeStruct((1, D), F32)],
                  compiler_params=_cp(1))(*ins)


def loss_head(h, tgt, *, name):
    T, D = h.shape
    tr = _tile(T, ROW_TILE, SUBLANES)

    def body(h_ref, t_ref, dh_ref, l_ref):
        i = pl.program_id(0)
        e = h_ref[...] - t_ref[...]
        dh_ref[...] = e * (1.0 / D)
        part = 0.5 * jnp.sum(jnp.mean(e * e, axis=-1, keepdims=True), axis=0, keepdims=True)
        part = jnp.broadcast_to(part, (1, LANES))

        @pl.when(i == 0)
        def _():
            l_ref[...] = part

        @pl.when(i > 0)
        def _():
            l_ref[...] += part

    return _pcall(body, name=name, grid=(T // tr,), in_specs=[_rows(tr, D), _rows(tr, D)],
                  out_specs=[_rows(tr, D), _fixed((1, LANES))],
                  out_shape=[jax.ShapeDtypeStruct((T, D), F32), jax.ShapeDtypeStruct((1, LANES), F32)],
                  compiler_params=_cp(1))(h, tgt)


def adamw(w, g, m, v, *, name):
    R, C = w.shape
    tr = _tile(R, max(SUBLANES, (1 << 18) // C), SUBLANES)
    c1 = 1.0 - ADAM_B1 ** ADAM_STEP
    c2 = 1.0 - ADAM_B2 ** ADAM_STEP

    def body(w_ref, g_ref, m_ref, v_ref, d_ref, mo_ref, vo_ref):
        gv = g_ref[...]
        m2 = ADAM_B1 * m_ref[...] + (1.0 - ADAM_B1) * gv
        v2 = ADAM_B2 * v_ref[...] + (1.0 - ADAM_B2) * (gv * gv)
        d_ref[...] = -ADAM_LR * ((m2 / c1) / (jnp.sqrt(v2 / c2) + ADAM_EPS) + ADAM_WD * w_ref[...])
        mo_ref[...] = m2
        vo_ref[...] = v2

    spec = _rows(tr, C)
    sds = jax.ShapeDtypeStruct((R, C), F32)
    return _pcall(body, name=name, grid=(R // tr,), in_specs=[spec] * 4, out_specs=[spec] * 3,
                  out_shape=[sds] * 3, compiler_params=_cp(1))(w, g, m, v)


HALO = SUBLANES


def _conv_down(xx, w_ref):
    acc = xx * w_ref[pl.ds(CONV_WIDTH - 1, 1), :]
    for d in range(1, CONV_WIDTH):
        acc = acc + pltpu.roll(xx, d, 0) * w_ref[pl.ds(CONV_WIDTH - 1 - d, 1), :]
    return acc


def _conv_tile(x_ref, halo_ref, w_ref, first):
    xs = x_ref[...]
    hal = jnp.where(first, 0.0, halo_ref[...])
    cat = jnp.concatenate([hal, xs[0:HALO]], axis=0)
    return jnp.concatenate([_conv_down(cat, w_ref)[HALO:2 * HALO], _conv_down(xs, w_ref)[HALO:]], axis=0)


def _shift_down_tile(x_ref, halo_ref, first, d):
    xs = x_ref[...]
    if d == 0:
        return xs
    hal = jnp.where(first, 0.0, halo_ref[...])
    cat = jnp.concatenate([hal, xs[0:HALO]], axis=0)
    return jnp.concatenate([pltpu.roll(cat, d, 0)[HALO:2 * HALO], pltpu.roll(xs, d, 0)[HALO:]], axis=0)


def _l2n(s):
    return s * lax.rsqrt(jnp.sum(s * s, axis=-1, keepdims=True) + L2_EPS)


PREP_ROWS = 512


def _l2n_groups(s, nb):
    return jnp.concatenate([_l2n(s[:, g * LANES:(g + 1) * LANES]) for g in range(nb)], axis=1)


def prep_fwd(pm, off, wc8, woff, nblk, l2, *, name):
    T = pm.shape[0]
    tr = _tile(T, PREP_ROWS, SUBLANES)
    hb = tr // HALO
    wb = _heads_per_step(nblk)
    wl = wb * LANES

    def body(x_ref, halo_ref, w_ref, o_ref):
        i = pl.program_id(0)
        s = _silu(_conv_tile(x_ref, halo_ref, w_ref, i == 0))
        o_ref[...] = _l2n_groups(s, wb) if l2 else s

    return _pcall(
        body, name=name, grid=(T // tr, nblk // wb),
        in_specs=[pl.BlockSpec((tr, wl), lambda i, c: (i, off // wb + c)),
                  pl.BlockSpec((HALO, wl), lambda i, c: (jnp.maximum(i * hb - 1, 0), off // wb + c)),
                  pl.BlockSpec((SUBLANES, wl), lambda i, c: (0, woff // wb + c))],
        out_specs=pl.BlockSpec((tr, wl), lambda i, c: (i, c)),
        out_shape=jax.ShapeDtypeStruct((T, nblk * LANES), F32), compiler_params=_cp(2))(pm, pm, wc8)


def prep_bwd_act(pm, off, wc8, woff, nblk, l2, dout, *, name):
    T = pm.shape[0]
    tr = _tile(T, PREP_ROWS, SUBLANES)
    hb = tr // HALO
    wb = _heads_per_step(nblk)
    wl = wb * LANES

    def l2_bwd(s, do):
        r = lax.rsqrt(jnp.sum(s * s, axis=-1, keepdims=True) + L2_EPS)
        nrm = s * r
        return r * (do - nrm * jnp.sum(do * nrm, axis=-1, keepdims=True))

    def body(x_ref, halo_ref, w_ref, do_ref, dc_ref, dw_ref):
        i = pl.program_id(1)
        first = i == 0
        y = _conv_tile(x_ref, halo_ref, w_ref, first)
        s = _silu(y)
        do = do_ref[...]
        if l2:
            ds = jnp.concatenate([l2_bwd(s[:, g * LANES:(g + 1) * LANES], do[:, g * LANES:(g + 1) * LANES])
                                  for g in range(wb)], axis=1)
        else:
            ds = do
        dc = ds * _dsilu(y)
        dc_ref[...] = dc

        @pl.when(first)
        def _():
            dw_ref[...] = jnp.zeros_like(dw_ref)

        for j in range(CONV_WIDTH):
            xsh = _shift_down_tile(x_ref, halo_ref, first, CONV_WIDTH - 1 - j)
            dw_ref[pl.ds(j, 1), :] += jnp.sum(dc * xsh, axis=0, keepdims=True)

    return _pcall(
        body, name=name, grid=(nblk // wb, T // tr),
        in_specs=[pl.BlockSpec((tr, wl), lambda c, i: (i, off // wb + c)),
                  pl.BlockSpec((HALO, wl), lambda c, i: (jnp.maximum(i * hb - 1, 0), off // wb + c)),
                  pl.BlockSpec((SUBLANES, wl), lambda c, i: (0, woff // wb + c)),
                  pl.BlockSpec((tr, wl), lambda c, i: (i, c))],
        out_specs=[pl.BlockSpec((tr, wl), lambda c, i: (i, c)),
                   pl.BlockSpec((SUBLANES, wl), lambda c, i: (0, c))],
        out_shape=[jax.ShapeDtypeStruct((T, nblk * LANES), F32),
                   jax.ShapeDtypeStruct((SUBLANES, nblk * LANES), F32)],
        compiler_params=_cp(2))(pm, pm, wc8, dout)


def prep_bwd_conv(dc, wc8, woff, nblk, *, name):
    T = dc.shape[0]
    tr = _tile(T, PREP_ROWS, SUBLANES)
    hb = tr // HALO
    nt = T // tr
    last_halo = T // HALO - 1
    wb = _heads_per_step(nblk)
    wl = wb * LANES

    def up(xx, w_ref):
        rows = xx.shape[0]
        acc = xx * w_ref[pl.ds(CONV_WIDTH - 1, 1), :]
        for d in range(1, CONV_WIDTH):
            acc = acc + pltpu.roll(xx, rows - d, 0) * w_ref[pl.ds(CONV_WIDTH - 1 - d, 1), :]
        return acc

    def body(x_ref, halo_ref, w_ref, o_ref):
        i = pl.program_id(0)
        xs = x_ref[...]
        hal = jnp.where(i == nt - 1, 0.0, halo_ref[...])
        cat = jnp.concatenate([xs[tr - HALO:tr], hal], axis=0)
        out = jnp.concatenate([up(xs, w_ref)[:tr - HALO], up(cat, w_ref)[0:HALO]], axis=0)
        o_ref[...] = out.astype(o_ref.dtype)

    return _pcall(
        body, name=name, grid=(nt, nblk // wb),
        in_specs=[pl.BlockSpec((tr, wl), lambda i, c: (i, c)),
                  pl.BlockSpec((HALO, wl), lambda i, c: (jnp.minimum((i + 1) * hb, last_halo), c)),
                  pl.BlockSpec((SUBLANES, wl), lambda i, c: (0, woff // wb + c))],
        out_specs=pl.BlockSpec((tr, wl), lambda i, c: (i, c)),
        out_shape=jax.ShapeDtypeStruct((T, nblk * LANES), BF), compiler_params=_cp(2))(dc, dc, wc8)


def _softplus(x):
    return jnp.maximum(x, 0.0) + jnp.log(1.0 + jnp.exp(-jnp.abs(x)))


def _tril_ones(c):
    t = lax.broadcasted_iota(jnp.int32, (c, c), 0)
    s = lax.broadcasted_iota(jnp.int32, (c, c), 1)
    return (t >= s).astype(F32)


def _triu_ones(c):
    t = lax.broadcasted_iota(jnp.int32, (c, c), 0)
    s = lax.broadcasted_iota(jnp.int32, (c, c), 1)
    return (t <= s).astype(F32)


def gates_fwd(pg, arow, dtrow, H, *, name):
    T = pg.shape[0]
    C = LA_CHUNK
    N = T // C

    def body(x_ref, a_ref, dt_ref, bg_ref, gr_ref):
        x = x_ref[...]
        lane = lax.broadcasted_iota(jnp.int32, (C, LANES), 1)
        g = -jnp.exp(a_ref[...]) * _softplus(x + dt_ref[...])
        g = jnp.where((lane >= H) & (lane < 2 * H), g, 0.0)
        lm = _tril_ones(C)
        gc = _dot_hi(lm, g)
        bg_ref[...] = jnp.where(lane < H, _sigmoid(x), gc)
        gr_ref[...] = _dot_hi(g, _triu_ones(C), TN)

    return _pcall(
        body, name=name, grid=(N,),
        in_specs=[pl.BlockSpec((C, LANES), lambda n: (n, 0)), _fixed((1, LANES)), _fixed((1, LANES))],
        out_specs=[pl.BlockSpec((C, LANES), lambda n: (n, 0)), pl.BlockSpec((None, LANES, C), lambda n: (n, 0, 0))],
        out_shape=[jax.ShapeDtypeStruct((T, LANES), F32), jax.ShapeDtypeStruct((N, LANES, C), F32)],
        compiler_params=_cp(1))(pg, arow, dtrow)


def gates_bwd(pg, arow, dtrow, dbg, H, *, name):
    T = pg.shape[0]
    C = LA_CHUNK
    N = T // C

    def body(x_ref, a_ref, dt_ref, d_ref, dx_ref, da_ref, ddt_ref):
        n = pl.program_id(0)
        x = x_ref[...]
        d = d_ref[...]
        lane = lax.broadcasted_iota(jnp.int32, (C, LANES), 1)
        in_g = (lane >= H) & (lane < 2 * H)
        e = jnp.exp(a_ref[...])
        xs = x + dt_ref[...]
        g = -e * _softplus(xs)
        dg = _dot_hi(_tril_ones(C), jnp.where(in_g, d, 0.0), TN)
        dxs = jnp.where(in_g, dg * (-e) * _sigmoid(xs), 0.0)
        beta = _sigmoid(x)
        dx_ref[...] = jnp.where(lane < H, d * beta * (1.0 - beta), dxs).astype(dx_ref.dtype)
        pa = jnp.sum(jnp.where(in_g, dg * g, 0.0), axis=0, keepdims=True)
        pd = jnp.sum(dxs, axis=0, keepdims=True)

        @pl.when(n == 0)
        def _():
            da_ref[...] = pa
            ddt_ref[...] = pd

        @pl.when(n > 0)
        def _():
            da_ref[...] += pa
            ddt_ref[...] += pd

    return _pcall(
        body, name=name, grid=(N,),
        in_specs=[pl.BlockSpec((C, LANES), lambda n: (n, 0)), _fixed((1, LANES)), _fixed((1, LANES)),
                  pl.BlockSpec((C, LANES), lambda n: (n, 0))],
        out_specs=[pl.BlockSpec((C, LANES), lambda n: (n, 0)), _fixed((1, LANES)), _fixed((1, LANES))],
        out_shape=[jax.ShapeDtypeStruct((T, LANES), BF), jax.ShapeDtypeStruct((1, LANES), F32),
                   jax.ShapeDtypeStruct((1, LANES), F32)],
        compiler_params=_cp(1))(pg, arow, dtrow, dbg)


QK_SCALE = HEAD_DIM ** -0.5


HEADS_PER_STEP = 4


def _heads_per_step(H):
    hb = HEADS_PER_STEP
    while H % hb:
        hb //= 2
    return hb


def _head_rstd(o):
    return lax.rsqrt(jnp.mean(o * o, axis=-1, keepdims=True) + EPS)


def _gdn_gates(bg_ref, gr_ref, h, H):
    C = LA_CHUNK
    bgv = bg_ref[...]
    lane = lax.broadcasted_iota(jnp.int32, (C, LANES), 1)
    beta = jnp.sum(jnp.where(lane == h, bgv, 0.0), axis=1, keepdims=True)
    gc = jnp.sum(jnp.where(lane == H + h, bgv, 0.0), axis=1, keepdims=True)
    grow = gr_ref[pl.ds(H + h, 1), :]
    ri = lax.broadcasted_iota(jnp.int32, (C, 1), 0)
    gl = jnp.sum(jnp.where(ri == C - 1, gc, 0.0), axis=0, keepdims=True)
    return beta, gc, grow, gl


def _chunk_masks():
    C = LA_CHUNK
    ti = lax.broadcasted_iota(jnp.int32, (C, C), 0)
    si = lax.broadcasted_iota(jnp.int32, (C, C), 1)
    return ti >= si, ti > si, ti == si


def _decay(gc, grow, causal):
    return jnp.where(causal, jnp.exp(jnp.where(causal, gc - grow, 0.0)), 0.0)


def _unit_lower_inverse(a, eye):
    x = -a
    p = jnp.where(eye, 1.0, 0.0) + x
    for _ in range(5):
        x = _dot_hi(x, x)
        p = p + _dot_hi(p, x)
    return p


def gdn_fwd(q, k, v, pm, zoff, bg, gcrow, wn, H, *, name):
    T = q.shape[0]
    C = LA_CHUNK
    N = T // C
    hd = HEAD_DIM

    HB = _heads_per_step(H)

    def body(q_ref, k_ref, v_ref, z_ref, bg_ref, gr_ref, wn_ref, og_ref, or_ref, sall_ref, tall_ref, S):
        n = pl.program_id(0)
        hg = pl.program_id(1)
        causal, strict, eye = _chunk_masks()
        for i in range(HB):
            h = hg * HB + i
            sl = slice(i * hd, (i + 1) * hd)

            @pl.when(n == 0)
            def _():
                S[h] = jnp.zeros((hd, hd), F32)

            beta, gc, grow, gl = _gdn_gates(bg_ref, gr_ref, h, H)
            dm = _decay(gc, grow, causal)
            qs = q_ref[:, sl] * QK_SCALE
            kk = k_ref[:, sl]
            vv = v_ref[:, sl]
            eg = jnp.exp(gc)
            kb = kk * beta
            a = jnp.where(strict, _dot(kb, kk, NT) * dm, 0.0)
            tm = _unit_lower_inverse(a, eye)
            u = _dot(tm, vv * beta)
            w = _dot(tm, kb * eg)
            s0 = S[h]
            vnew = u - _dot(w, s0)
            qk = jnp.where(causal, _dot(qs, kk, NT) * dm, 0.0)
            o = _dot(qs * eg, s0) + _dot(qk, vnew)
            S[h] = s0 * jnp.exp(gl) + _dot(kk * jnp.exp(gl - gc), vnew, TN)
            sall_ref[i] = s0
            tall_ref[i] = tm
            or_ref[:, sl] = o
            og_ref[:, sl] = (o * _head_rstd(o) * wn_ref[...] * _silu(z_ref[:, sl])).astype(og_ref.dtype)

    blk = lambda off: pl.BlockSpec((C, HB * hd), lambda n, h: (n, off // HB + h))
    return _pcall(
        body, name=name, grid=(N, H // HB),
        in_specs=[blk(0), blk(0), blk(0), blk(zoff),
                  pl.BlockSpec((C, LANES), lambda n, h: (n, 0)),
                  pl.BlockSpec((None, LANES, C), lambda n, h: (n, 0, 0)),
                  _fixed((1, hd))],
        out_specs=[blk(0), blk(0),
                   pl.BlockSpec((None, HB, hd, hd), lambda n, h: (n, h, 0, 0)),
                   pl.BlockSpec((None, HB, C, C), lambda n, h: (n, h, 0, 0))],
        out_shape=[jax.ShapeDtypeStruct((T, H * hd), BF), jax.ShapeDtypeStruct((T, H * hd), F32),
                   jax.ShapeDtypeStruct((N, H, hd, hd), F32), jax.ShapeDtypeStruct((N, H, C, C), F32)],
        scratch_shapes=[pltpu.VMEM((H, hd, hd), F32)],
        compiler_params=_cp(2))(q, k, v, pm, bg, gcrow, wn)


def gdn_bwd(q, k, v, pm, zoff, bg, gcrow, wn, oraw, sall, tall, dog, H, *, name):
    T = q.shape[0]
    C = LA_CHUNK
    N = T // C
    hd = HEAD_DIM

    HB = _heads_per_step(H)

    def body(*refs):
        dbg_ref, dwn_ref, dS = refs[15], refs[16], refs[17]
        n = pl.program_id(0)
        hg = pl.program_id(1)

        @pl.when((n == 0) & (hg == 0))
        def _():
            dwn_ref[...] = jnp.zeros_like(dwn_ref)

        @pl.when(hg == 0)
        def _():
            dbg_ref[...] = jnp.zeros_like(dbg_ref)

        for i in range(HB):
            h = hg * HB + i

            @pl.when(n == 0)
            def _():
                dS[h] = jnp.zeros((hd, hd), F32)

            head(i, h, *refs)

    def head(i, h, q_ref, k_ref, v_ref, z_ref, bg_ref, gr_ref, wn_ref, or_ref, sall_ref, tall_ref, dog_ref,
             dq_ref, dk_ref, dv_ref, dz_ref, dbg_ref, dwn_ref, dS):
        sl = slice(i * hd, (i + 1) * hd)
        beta, gc, grow, gl = _gdn_gates(bg_ref, gr_ref, h, H)
        causal, strict, eye = _chunk_masks()
        dm = _decay(gc, grow, causal)
        qs = q_ref[:, sl] * QK_SCALE
        kk = k_ref[:, sl]
        vv = v_ref[:, sl]
        zz = z_ref[:, sl]
        o = or_ref[:, sl]
        dog = dog_ref[:, sl]
        wn_v = wn_ref[...]
        s0 = sall_ref[i]
        tm = tall_ref[i]

        rstd = _head_rstd(o)
        on = o * rstd
        sz = _silu(zz)
        don = dog * wn_v * sz
        dwn_ref[...] += jnp.sum(dog * on * sz, axis=0, keepdims=True)
        dz_ref[:, sl] = (dog * on * wn_v * _dsilu(zz)).astype(dz_ref.dtype)
        do = rstd * (don - on * jnp.mean(don * on, axis=-1, keepdims=True))

        eg = jnp.exp(gc)
        kb = kk * beta
        vb = vv * beta
        kbg = kb * eg
        a = jnp.where(strict, _dot(kb, kk, NT) * dm, 0.0)
        u = _dot(tm, vb)
        w = _dot(tm, kbg)
        vnew = u - _dot(w, s0)
        qk = jnp.where(causal, _dot(qs, kk, NT) * dm, 0.0)
        qdec = qs * eg
        etail = jnp.exp(gl - gc)
        ktail = kk * etail
        egl = jnp.exp(gl)

        ds1 = dS[h]
        dvnew = _dot(qk, do, TN) + _dot(ktail, ds1)
        dqdec = _dot(do, s0, NT)
        dqk = jnp.where(causal, _dot(do, vnew, NT), 0.0)
        dktail = _dot(vnew, ds1, NT)
        dcd = jnp.sum(jnp.sum(s0 * ds1, axis=1, keepdims=True), axis=0, keepdims=True)
        dS[h] = egl * ds1 + _dot(qdec, do, TN) - _dot(w, dvnew, TN)
        dw = -_dot(dvnew, s0, NT)
        dvb = _dot(tm, dvnew, TN)
        dkbg = _dot(tm, dw, TN)
        dtm = _dot(dvnew, vb, NT) + _dot(dw, kbg, NT)
        da = jnp.where(strict, -_dot_hi(_dot_hi(tm, dtm, TN), tm, NT), 0.0)
        dkk = da * dm
        dkb = _dot(dkk, kk) + dkbg * eg
        dk = _dot(dkk, kb, TN)
        dqkr = dqk * dm
        dqs = _dot(dqkr, kk) + dqdec * eg
        dk = dk + _dot(dqkr, qs, TN) + dktail * etail + dkb * beta
        g = da * a + dqk * qk
        colsum = jnp.max(_dot_hi(g, jnp.ones((C, LANES), F32), TN), axis=1, keepdims=True)
        rk = jnp.sum(dktail * ktail, axis=1, keepdims=True)
        dgc = (jnp.sum(g, axis=1, keepdims=True) - colsum
               + jnp.sum(dqdec * qdec, axis=1, keepdims=True) - rk
               + jnp.sum(dkbg * kbg, axis=1, keepdims=True))
        dgl = jnp.sum(rk, axis=0, keepdims=True) + dcd * egl
        ri = lax.broadcasted_iota(jnp.int32, (C, 1), 0)
        dgc = dgc + jnp.where(ri == C - 1, dgl, 0.0)
        dbeta = jnp.sum(dkb * kk, axis=1, keepdims=True) + jnp.sum(dvb * vv, axis=1, keepdims=True)

        dq_ref[:, sl] = dqs * QK_SCALE
        dk_ref[:, sl] = dk
        dv_ref[:, sl] = dvb * beta
        lane = lax.broadcasted_iota(jnp.int32, (C, LANES), 1)
        dbg_ref[...] += jnp.where(lane == h, dbeta, 0.0) + jnp.where(lane == H + h, dgc, 0.0)

    blk = lambda off: pl.BlockSpec((C, HB * hd), lambda n, h: (N - 1 - n, off // HB + h))
    st = lambda r: pl.BlockSpec((None, HB, r, r), lambda n, h: (N - 1 - n, h, 0, 0))
    return _pcall(
        body, name=name, grid=(N, H // HB),
        in_specs=[blk(0), blk(0), blk(0), blk(zoff),
                  pl.BlockSpec((C, LANES), lambda n, h: (N - 1 - n, 0)),
                  pl.BlockSpec((None, LANES, C), lambda n, h: (N - 1 - n, 0, 0)),
                  _fixed((1, hd)), blk(0), st(hd), st(C), blk(0)],
        out_specs=[blk(0), blk(0), blk(0), blk(0),
                   pl.BlockSpec((C, LANES), lambda n, h: (N - 1 - n, 0)), _fixed((1, hd))],
        out_shape=[jax.ShapeDtypeStruct((T, H * hd), F32)] * 3
        + [jax.ShapeDtypeStruct((T, H * hd), BF), jax.ShapeDtypeStruct((T, LANES), F32),
           jax.ShapeDtypeStruct((1, hd), F32)],
        scratch_shapes=[pltpu.VMEM((H, hd, hd), F32)],
        compiler_params=_cp(2))(q, k, v, pm, bg, gcrow, wn, oraw, sall, tall, dog)


def _rot(x, cs, sn):
    return x * cs + pltpu.roll(x, HEAD_DIM // 2, 1) * sn


def _rot_t(dy, cs, sn):
    return dy * cs + pltpu.roll(dy * sn, HEAD_DIM // 2, 1)


def ret_fwd(pm, qoff, koff, voff, goff, cs, sn, dmat, avec, bvec, gam, H, *, name):
    T = pm.shape[0]
    C = LA_CHUNK
    N = T // C
    hd = HEAD_DIM

    HB = _heads_per_step(H)

    def body(q_ref, k_ref, v_ref, g_ref, cs_ref, sn_ref, dm_ref, a_ref, b_ref, gam_ref,
             og_ref, or_ref, sall_ref, S):
        n = pl.program_id(0)
        hg = pl.program_id(1)
        c, s = cs_ref[...], sn_ref[...]
        for i in range(HB):
            h = hg * HB + i
            sl = slice(i * hd, (i + 1) * hd)

            @pl.when(n == 0)
            def _():
                S[h] = jnp.zeros((hd, hd), F32)

            qq = _rot(q_ref[:, sl], c, s)
            kk = _rot(k_ref[:, sl], c, s) * QK_SCALE
            vv = v_ref[:, sl]
            s0 = S[h]
            p = _dot(qq, kk, NT) * dm_ref[i]
            o = _dot(p, vv) + _dot(qq * a_ref[i], s0)
            S[h] = s0 * gam_ref[i] + _dot(kk * b_ref[i], vv, TN)
            sall_ref[i] = s0
            or_ref[:, sl] = o
            og_ref[:, sl] = (_silu(g_ref[:, sl]) * o * _head_rstd(o)).astype(og_ref.dtype)

    blk = lambda off: pl.BlockSpec((C, HB * hd), lambda n, h: (n, off // HB + h))
    tab = pl.BlockSpec((C, hd), lambda n, h: (n, 0))
    per_h = lambda r, cdim: pl.BlockSpec((HB, r, cdim), lambda n, h: (h, 0, 0))
    return _pcall(
        body, name=name, grid=(N, H // HB),
        in_specs=[blk(qoff), blk(koff), blk(voff), blk(goff), tab, tab,
                  per_h(C, C), per_h(C, hd), per_h(C, hd), per_h(1, hd)],
        out_specs=[blk(0), blk(0), pl.BlockSpec((None, HB, hd, hd), lambda n, h: (n, h, 0, 0))],
        out_shape=[jax.ShapeDtypeStruct((T, H * hd), BF), jax.ShapeDtypeStruct((T, H * hd), F32),
                   jax.ShapeDtypeStruct((N, H, hd, hd), F32)],
        scratch_shapes=[pltpu.VMEM((H, hd, hd), F32)],
        compiler_params=_cp(2))(pm, pm, pm, pm, cs, sn, dmat, avec, bvec, gam)


def ret_bwd(pm, qoff, koff, voff, goff, cs, sn, dmat, avec, bvec, gam, oraw, sall, dog, dogoff, H, *, name):
    T = pm.shape[0]
    C = LA_CHUNK
    N = T // C
    hd = HEAD_DIM

    HB = _heads_per_step(H)

    def body(q_ref, k_ref, v_ref, g_ref, cs_ref, sn_ref, dm_ref, a_ref, b_ref, gam_ref, or_ref, sall_ref,
             dog_ref, dq_ref, dk_ref, dv_ref, dg_ref, dS):
        n = pl.program_id(0)
        hg = pl.program_id(1)
        c, s = cs_ref[...], sn_ref[...]
        for i in range(HB):
            h = hg * HB + i
            sl = slice(i * hd, (i + 1) * hd)

            @pl.when(n == 0)
            def _():
                dS[h] = jnp.zeros((hd, hd), F32)

            qq = _rot(q_ref[:, sl], c, s)
            kk = _rot(k_ref[:, sl], c, s) * QK_SCALE
            vv = v_ref[:, sl]
            gg = g_ref[:, sl]
            o = or_ref[:, sl]
            dog = dog_ref[:, sl]
            dm = dm_ref[i]
            av, bv = a_ref[i], b_ref[i]
            s0 = sall_ref[i]
            ds1 = dS[h]

            rstd = _head_rstd(o)
            on = o * rstd
            don = dog * _silu(gg)
            dg_ref[:, sl] = (dog * on * _dsilu(gg)).astype(dg_ref.dtype)
            do = rstd * (don - on * jnp.mean(don * on, axis=-1, keepdims=True))

            p = _dot(qq, kk, NT) * dm
            dp = _dot(do, vv, NT) * dm
            dv_ref[:, sl] = (_dot(p, do, TN) + _dot(kk * bv, ds1)).astype(dv_ref.dtype)
            dqq = _dot(dp, kk) + _dot(do, s0, NT) * av
            dkk = (_dot(dp, qq, TN) + _dot(vv, ds1, NT) * bv) * QK_SCALE
            dS[h] = ds1 * gam_ref[i] + _dot(qq * av, do, TN)
            dq_ref[:, sl] = _rot_t(dqq, c, s).astype(dq_ref.dtype)
            dk_ref[:, sl] = _rot_t(dkk, c, s).astype(dk_ref.dtype)

    blk = lambda off: pl.BlockSpec((C, HB * hd), lambda n, h: (N - 1 - n, off // HB + h))
    tab = pl.BlockSpec((C, hd), lambda n, h: (N - 1 - n, 0))
    per_h = lambda r, cdim: pl.BlockSpec((HB, r, cdim), lambda n, h: (h, 0, 0))
    return _pcall(
        body, name=name, grid=(N, H // HB),
        in_specs=[blk(qoff), blk(koff), blk(voff), blk(goff), tab, tab,
                  per_h(C, C), per_h(C, hd), per_h(C, hd), per_h(1, hd), blk(0),
                  pl.BlockSpec((None, HB, hd, hd), lambda n, h: (N - 1 - n, h, 0, 0)), blk(dogoff)],
        out_specs=[blk(0)] * 4,
        out_shape=[jax.ShapeDtypeStruct((T, H * hd), BF)] * 4,
        scratch_shapes=[pltpu.VMEM((H, hd, hd), F32)],
        compiler_params=_cp(2))(pm, pm, pm, pm, cs, sn, dmat, avec, bvec, gam, oraw, sall, dog)


LN_ROWS = 128


def ln_fwd(pre, lw, lb, *, name):
    T, W2 = pre.shape
    W = W2 // 2
    tr = _tile(T, LN_ROWS, SUBLANES)

    def body(p_ref, w_ref, b_ref, o_ref):
        v = _gelu(p_ref[...])
        xc = v - jnp.mean(v, axis=-1, keepdims=True)
        r = lax.rsqrt(jnp.mean(xc * xc, axis=-1, keepdims=True) + EPS)
        o_ref[...] = xc * r * w_ref[...] + b_ref[...]

    return _pcall(body, name=name, grid=(T // tr,),
                  in_specs=[pl.BlockSpec((tr, W), lambda i: (i, 1)), _fixed((1, W)), _fixed((1, W))],
                  out_specs=_rows(tr, W), out_shape=jax.ShapeDtypeStruct((T, W), F32),
                  compiler_params=_cp(1))(pre, lw, lb)


def ln_bwd(pre, lw, dvn, *, name):
    T, W2 = pre.shape
    W = W2 // 2
    tr = _tile(T, LN_ROWS, SUBLANES)

    def body(p_ref, w_ref, d_ref, dp_ref, dw_ref, db_ref):
        i = pl.program_id(0)
        pv = p_ref[...]
        v = _gelu(pv)
        xc = v - jnp.mean(v, axis=-1, keepdims=True)
        r = lax.rsqrt(jnp.mean(xc * xc, axis=-1, keepdims=True) + EPS)
        xh = xc * r
        d = d_ref[...]
        dxh = d * w_ref[...]
        dv = r * (dxh - jnp.mean(dxh, axis=-1, keepdims=True) - xh * jnp.mean(dxh * xh, axis=-1, keepdims=True))
        dp_ref[...] = (dv * _dgelu(pv)).astype(dp_ref.dtype)
        pw = jnp.sum(d * xh, axis=0, keepdims=True)
        pb = jnp.sum(d, axis=0, keepdims=True)

        @pl.when(i == 0)
        def _():
            dw_ref[...] = pw
            db_ref[...] = pb

        @pl.when(i > 0)
        def _():
            dw_ref[...] += pw
            db_ref[...] += pb

    return _pcall(body, name=name, grid=(T // tr,),
                  in_specs=[pl.BlockSpec((tr, W), lambda i: (i, 1)), _fixed((1, W)), _rows(tr, W)],
                  out_specs=[_rows(tr, W), _fixed((1, W)), _fixed((1, W))],
                  out_shape=[jax.ShapeDtypeStruct((T, W), BF), jax.ShapeDtypeStruct((1, W), F32),
                             jax.ShapeDtypeStruct((1, W), F32)],
                  compiler_params=_cp(1))(pre, lw, dvn)


def _tril_mask(c):
    t = lax.broadcasted_iota(jnp.int32, (c, c), 0)
    s = lax.broadcasted_iota(jnp.int32, (c, c), 1)
    return t >= s


def sg_fwd(pre, vn, ws, bs3, *, name):
    T, W = vn.shape
    G = ws.shape[0]
    gd = W // G
    C = SG_CHUNK

    def body(p_ref, v_ref, w_ref, b_ref, o_ref):
        mask = _tril_mask(C)
        for g in range(G):
            sl = slice(g * gd, (g + 1) * gd)
            wm = jnp.where(mask, w_ref[g], 0.0)
            s = _dot(wm, v_ref[:, sl]) + b_ref[g]
            o_ref[:, sl] = (_gelu(p_ref[:, sl]) * s).astype(o_ref.dtype)

    blk = pl.BlockSpec((C, W), lambda n: (n, 0))
    return _pcall(body, name=name, grid=(T // C,),
                  in_specs=[blk, blk, _fixed((G, C, C)), _fixed((G, C, 1))],
                  out_specs=blk, out_shape=jax.ShapeDtypeStruct((T, W), BF),
                  compiler_params=_cp(1))(pre, vn, ws, bs3)


def sg_bwd(pre, vn, ws, bs3, dus, *, name):
    T, W = vn.shape
    G = ws.shape[0]
    gd = W // G
    C = SG_CHUNK

    def body(p_ref, v_ref, w_ref, b_ref, d_ref, dp_ref, dv_ref, dw_ref, db_ref):
        n = pl.program_id(0)
        mask = _tril_mask(C)

        @pl.when(n == 0)
        def _():
            dw_ref[...] = jnp.zeros_like(dw_ref)
            db_ref[...] = jnp.zeros_like(db_ref)

        for g in range(G):
            sl = slice(g * gd, (g + 1) * gd)
            wm = jnp.where(mask, w_ref[g], 0.0)
            pv = p_ref[:, sl]
            vv = v_ref[:, sl]
            d = d_ref[:, sl]
            s = _dot(wm, vv) + b_ref[g]
            ds = d * _gelu(pv)
            dp_ref[:, sl] = (d * s * _dgelu(pv)).astype(dp_ref.dtype)
            dv_ref[:, sl] = _dot(wm, ds, TN)
            dw_ref[g] += jnp.where(mask, _dot(ds, vv, NT), 0.0)
            db_ref[g] += jnp.sum(ds, axis=1, keepdims=True)

    blk = pl.BlockSpec((C, W), lambda n: (n, 0))
    return _pcall(body, name=name, grid=(T // C,),
                  in_specs=[blk, blk, _fixed((G, C, C)), _fixed((G, C, 1)), blk],
                  out_specs=[blk, blk, _fixed((G, C, C)), _fixed((G, C, 1))],
                  out_shape=[jax.ShapeDtypeStruct((T, W), BF), jax.ShapeDtypeStruct((T, W), F32),
                             jax.ShapeDtypeStruct((G, C, C), F32), jax.ShapeDtypeStruct((G, C, 1), F32)],
                  compiler_params=_cp(1))(pre, vn, ws, bs3, dus)


CHIP_RELATIONS = ((1, 0), (0, 1), (1, 1))


def _place():
    return lax.axis_index("x"), lax.axis_index("y"), lax.axis_index("c")


def _peer_chip(x, y, r):
    fx, fy = CHIP_RELATIONS[r]
    return (1 - x if fx else x), (1 - y if fy else y)


def all_gather_halves(arrs, *, name):
    n = len(arrs)
    per = 2 * len(CHIP_RELATIONS)

    def body(*refs):
        ins, outs = refs[:n], refs[n:2 * n]
        send, recv = refs[2 * n:2 * n + 2]
        x, y, c = _place()
        j = 2 * x + y
        sib = (x, y, 1 - c)
        sends = []
        for a in range(n):
            for r in range(3):
                px, py = _peer_chip(x, y, r)
                cp = pltpu.make_async_remote_copy(
                    src_ref=ins[a].at[c], dst_ref=outs[a].at[j, c], send_sem=send.at[a * per + r],
                    recv_sem=recv.at[a * per + r], device_id=(px, py, c), device_id_type=MESH)
                cp.start()
                sends.append(cp)
        for a in range(n):
            for r in range(3):
                px, py = _peer_chip(x, y, r)
                kk = 2 * px + py
                landed = outs[a].at[kk, c]
                pltpu.make_async_remote_copy(
                    src_ref=landed, dst_ref=landed, send_sem=send.at[a * per + r],
                    recv_sem=recv.at[a * per + r], device_id=(px, py, c), device_id_type=MESH).wait_recv()
                fw = pltpu.make_async_remote_copy(
                    src_ref=landed, dst_ref=landed, send_sem=send.at[a * per + 3 + r],
                    recv_sem=recv.at[a * per + 3 + r], device_id=sib, device_id_type=MESH)
                fw.start()
                sends.append(fw)
        for a in range(n):
            for r in range(3):
                px, py = _peer_chip(x, y, r)
                kk = 2 * px + py
                other = outs[a].at[kk, 1 - c]
                pltpu.make_async_remote_copy(
                    src_ref=other, dst_ref=other, send_sem=send.at[a * per + 3 + r],
                    recv_sem=recv.at[a * per + 3 + r], device_id=sib, device_id_type=MESH).wait_recv()
        for cp in sends:
            cp.wait_send()

    out_shape = [jax.ShapeDtypeStruct((N_CHIPS,) + a.shape, a.dtype) for a in arrs]
    res = _pcall(body, name=name, in_specs=[ANY] * n, out_specs=[ANY] * n, out_shape=out_shape,
                 scratch_shapes=[pltpu.SemaphoreType.DMA((n * per,)), pltpu.SemaphoreType.DMA((n * per,))])(*arrs)
    return list(res)


def pair_exchange(gs, *, name):
    n = len(gs)

    def body(*refs):
        ins, outs = refs[:n], refs[n:2 * n]
        send, recv = refs[2 * n:2 * n + 2]
        x, y, c = _place()
        cps = []
        for a in range(n):
            cp = pltpu.make_async_remote_copy(
                src_ref=ins[a].at[:, pl.ds(1 - c, 1)], dst_ref=outs[a], send_sem=send.at[a], recv_sem=recv.at[a],
                device_id=(x, y, 1 - c), device_id_type=MESH)
            cp.start()
            cps.append(cp)
        for cp in cps:
            cp.wait()

    out_shape = [jax.ShapeDtypeStruct((g.shape[0], 1) + g.shape[2:], g.dtype) for g in gs]
    res = _pcall(body, name=name, in_specs=[ANY] * n, out_specs=[ANY] * n, out_shape=out_shape,
                 scratch_shapes=[pltpu.SemaphoreType.DMA((n,)), pltpu.SemaphoreType.DMA((n,))])(*gs)
    return list(res)


def chip_exchange(ps, *, name):
    n = len(ps)

    def body(*refs):
        ins, outs = refs[:n], refs[n:2 * n]
        send, recv = refs[2 * n:2 * n + 2]
        x, y, c = _place()
        cps = []
        for a in range(n):
            for r in range(3):
                px, py = _peer_chip(x, y, r)
                cp = pltpu.make_async_remote_copy(
                    src_ref=ins[a].at[2 * px + py], dst_ref=outs[a].at[r], send_sem=send.at[3 * a + r],
                    recv_sem=recv.at[3 * a + r], device_id=(px, py, c), device_id_type=MESH)
                cp.start()
                cps.append(cp)
        for cp in cps:
            cp.wait()

    out_shape = [jax.ShapeDtypeStruct((3,) + p.shape[1:], p.dtype) for p in ps]
    res = _pcall(body, name=name, in_specs=[ANY] * n, out_specs=[ANY] * n, out_shape=out_shape,
                 scratch_shapes=[pltpu.SemaphoreType.DMA((3 * n,)), pltpu.SemaphoreType.DMA((3 * n,))])(*ps)
    return list(res)


def pair_share(fs, *, name):
    n = len(fs)

    def body(*refs):
        ins, outs = refs[:n], refs[n:2 * n]
        send, recv = refs[2 * n:2 * n + 2]
        x, y, c = _place()
        cps = []
        for a in range(n):
            cp = pltpu.make_async_remote_copy(
                src_ref=ins[a], dst_ref=outs[a], send_sem=send.at[a], recv_sem=recv.at[a],
                device_id=(x, y, 1 - c), device_id_type=MESH)
            cp.start()
            cps.append(cp)
        for cp in cps:
            cp.wait()

    out_shape = [jax.ShapeDtypeStruct(f.shape, f.dtype) for f in fs]
    res = _pcall(body, name=name, in_specs=[ANY] * n, out_specs=[ANY] * n, out_shape=out_shape,
                 scratch_shapes=[pltpu.SemaphoreType.DMA((n,)), pltpu.SemaphoreType.DMA((n,))])(*fs)
    return list(res)


def all_reduce_small(v, *, name):
    R = v.shape[0]

    def body(v_ref, sum_ref, gat_ref, send, recv):
        x, y, c = _place()
        me = 4 * x + 2 * y + c
        gat_ref[me] = v_ref[...]
        cps = []
        peers = []
        for r in range(1, N_DEV):
            fx, fy, fc = (r >> 2) & 1, (r >> 1) & 1, r & 1
            px, py, pc = (1 - x if fx else x), (1 - y if fy else y), (1 - c if fc else c)
            peers.append((px, py, pc))
            cp = pltpu.make_async_remote_copy(
                src_ref=v_ref, dst_ref=gat_ref.at[me], send_sem=send.at[r - 1], recv_sem=recv.at[r - 1],
                device_id=(px, py, pc), device_id_type=MESH)
            cp.start()
            cps.append(cp)
        for r in range(1, N_DEV):
            px, py, pc = peers[r - 1]
            slot = gat_ref.at[4 * px + 2 * py + pc]
            pltpu.make_async_remote_copy(
                src_ref=v_ref, dst_ref=slot, send_sem=send.at[r - 1], recv_sem=recv.at[r - 1],
                device_id=(px, py, pc), device_id_type=MESH).wait_recv()
        for cp in cps:
            cp.wait_send()
        acc = gat_ref[0]
        for s in range(1, N_DEV):
            acc = acc + gat_ref[s]
        sum_ref[...] = acc

    vm = pl.BlockSpec(memory_space=pltpu.VMEM)
    res = _pcall(body, name=name, in_specs=[vm], out_specs=[vm, vm],
                 out_shape=[jax.ShapeDtypeStruct((R, LANES), F32), jax.ShapeDtypeStruct((N_DEV, R, LANES), F32)],
                 scratch_shapes=[pltpu.SemaphoreType.DMA((N_DEV - 1,)), pltpu.SemaphoreType.DMA((N_DEV - 1,))],
                 compiler_params=pltpu.CompilerParams(vmem_limit_bytes=VMEM_LIMIT))(v)
    return res[0]


def pair_sum(g, r1, c_idx, *, name):
    nb, _, hr, C = g.shape
    tr = _tile(hr, max(SUBLANES, (1 << 18) // C), SUBLANES)

    def body(c_ref, g_ref, r_ref, o_ref):
        o_ref[...] = g_ref[...] + r_ref[...]

    gs = pltpu.PrefetchScalarGridSpec(
        num_scalar_prefetch=1, grid=(nb, hr // tr),
        in_specs=[pl.BlockSpec((None, None, tr, C), lambda b, i, cr: (b, cr[0], i, 0)),
                  pl.BlockSpec((None, None, tr, C), lambda b, i, cr: (b, 0, i, 0))],
        out_specs=pl.BlockSpec((None, tr, C), lambda b, i, cr: (b, i, 0)))
    return _pcall(body, name=name, grid_spec=gs, out_shape=jax.ShapeDtypeStruct((nb, hr, C), F32),
                  compiler_params=_cp(2))(c_idx, g, r1)


def chip_sum(p, r2, j_idx, *, name):
    _, hr, C = p.shape
    tr = _tile(hr, max(SUBLANES, (1 << 18) // C), SUBLANES)

    def body(j_ref, p_ref, a_ref, b_ref, c_ref, o_ref):
        o_ref[...] = ((p_ref[...] + a_ref[...]) + b_ref[...]) + c_ref[...]

    rel = lambda r: pl.BlockSpec((None, tr, C), lambda i, jr: (r, i, 0))
    gs = pltpu.PrefetchScalarGridSpec(
        num_scalar_prefetch=1, grid=(hr // tr,),
        in_specs=[pl.BlockSpec((None, tr, C), lambda i, jr: (jr[0], i, 0)), rel(0), rel(1), rel(2)],
        out_specs=pl.BlockSpec((tr, C), lambda i, jr: (i, 0)))
    return _pcall(body, name=name, grid_spec=gs, out_shape=jax.ShapeDtypeStruct((hr, C), F32),
                  compiler_params=_cp(1))(j_idx, p, r2, r2, r2)


def adamw_halves(w, g_mine, g_other, m, v, c_idx, *, name):
    R, C = w.shape
    hr = R // 2
    tr = _tile(hr, max(SUBLANES, (1 << 18) // C), SUBLANES)
    nbh = hr // tr
    c1 = 1.0 - ADAM_B1 ** ADAM_STEP
    c2 = 1.0 - ADAM_B2 ** ADAM_STEP

    def body(c_ref, w_ref, gm_ref, go_ref, m_ref, v_ref, g_ref, d_ref, mo_ref, vo_ref):
        i = pl.program_id(0)
        gv = jnp.where(i // nbh == c_ref[0], gm_ref[...], go_ref[...])
        m2 = ADAM_B1 * m_ref[...] + (1.0 - ADAM_B1) * gv
        v2 = ADAM_B2 * v_ref[...] + (1.0 - ADAM_B2) * (gv * gv)
        d_ref[...] = -ADAM_LR * ((m2 / c1) / (jnp.sqrt(v2 / c2) + ADAM_EPS) + ADAM_WD * w_ref[...])
        g_ref[...] = gv
        mo_ref[...] = m2
        vo_ref[...] = v2

    full = pl.BlockSpec((tr, C), lambda i, cr: (i, 0))
    half = pl.BlockSpec((tr, C), lambda i, cr: (i % nbh, 0))
    gs = pltpu.PrefetchScalarGridSpec(
        num_scalar_prefetch=1, grid=(R // tr,),
        in_specs=[full, half, half, full, full], out_specs=[full] * 4)
    sds = jax.ShapeDtypeStruct((R, C), F32)
    return _pcall(body, name=name, grid_spec=gs, out_shape=[sds] * 4,
                  compiler_params=_cp(1))(c_idx, w, g_mine, g_other, m, v)


def _pack_rows(arrs):
    parts = []
    for a in arrs:
        flat = a.reshape(-1).astype(F32)
        tile = SUBLANES * LANES
        pad = (-flat.shape[0]) % tile
        parts.append(jnp.pad(flat, (0, pad)).reshape(-1, LANES))
    return jnp.concatenate(parts, axis=0)


def _unpack_rows(buf, shapes):
    out, row = [], 0
    for shp in shapes:
        size = int(np.prod(shp))
        rows = -(-size // (SUBLANES * LANES)) * SUBLANES
        out.append(buf[row:row + rows].reshape(-1)[:size].reshape(shp))
        row += rows
    return out


def _halves(a2d):
    r, c = a2d.shape
    return a2d.reshape(2, r // 2, c)


def _rotary_tables(T):
    half = HEAD_DIM // 2
    pos = jnp.arange(T, dtype=F32)
    inv_freq = 1.0 / (ROPE_BASE ** jnp.linspace(0.0, 1.0, half, dtype=F32))
    ang = pos[:, None] * inv_freq[None, :]
    cos, sin = jnp.cos(ang), jnp.sin(ang)
    return jnp.concatenate([cos, cos], axis=1), jnp.concatenate([-sin, sin], axis=1)


def _retention_tables(H):
    C = LA_CHUNK
    lg = jnp.log1p(-jnp.power(2.0, -5.0 - jnp.arange(H, dtype=F32)))
    pos = jnp.arange(C, dtype=F32)
    causal = jnp.tril(jnp.ones((C, C), dtype=bool))
    dmat = jnp.exp(jnp.where(causal, (pos[:, None] - pos[None, :]) * lg[:, None, None], -jnp.inf))
    bc = lambda t: jnp.broadcast_to(t[..., None], t.shape + (HEAD_DIM,))
    avec = bc(jnp.exp((pos + 1.0)[None, :] * lg[:, None]))
    bvec = bc(jnp.exp((C - 1.0 - pos)[None, :] * lg[:, None]))
    gam = bc(jnp.exp(C * lg)[:, None])
    return dmat, avec, bvec, gam


def _relu2(acc):
    return acc, jnp.square(jnp.maximum(acc, 0.0))


def _drelu2(acc, up):
    return (acc * (2.0 * jnp.maximum(up, 0.0)),)


def _add(acc, e):
    return (acc + e,)


def _ffn_fwd(y, w_up3, w_dn, tag):
    up, act = mm_nn(y, w_up3, name=f"ffn_up_{tag}", out_dtypes=(F32, BF), epilogue=_relu2)
    dn = mm_nn(act, w_dn, name=f"ffn_down_{tag}")
    return up, act, dn


def _ffn_bwd(y, up, act, ddn, w_up3, w_dn, tag):
    g_dn = mm_tn(act, ddn, name=f"ffn_dwdown_{tag}")
    dup = mm_nt(ddn, w_dn, name=f"ffn_dup_{tag}", out_dtypes=(BF,), epilogue=_drelu2, extras=(up,))
    g_up = mm_tn(y, dup, name=f"ffn_dwup_{tag}", shards=N_CHIPS)
    dy = mm_nt(dup, w_up3, name=f"ffn_dy_{tag}")
    return dy, g_up, g_dn


def _train_local(x2, tgt, W):
    T, D = x2.shape
    H = W["a_log"].shape[0]
    nw = W["norm_w"]
    row = lambda v: v.reshape(1, -1).astype(F32)

    y0 = rms_fwd(x2, row(nw[0, 0]), name="norm00")
    pm = mm_nn(y0, W["la_in_main"], name="la_in_main")
    pg = mm_nn(y0, W["la_in_gate"], name="la_in_gate")
    wc8 = jnp.pad(jnp.transpose(W["conv_w"]), ((0, SUBLANES - CONV_WIDTH), (0, 0)))
    lanes_pad = (H, LANES - 2 * H)
    arow = jnp.pad(W["a_log"], lanes_pad).reshape(1, LANES)
    dtrow = jnp.pad(W["dt_bias"], lanes_pad).reshape(1, LANES)
    bg, gcrow = gates_fwd(pg, arow, dtrow, H, name="gates_fwd")
    q = prep_fwd(pm, 0, wc8, 0, H, True, name="prep_q")
    k = prep_fwd(pm, H, wc8, H, H, True, name="prep_k")
    v = prep_fwd(pm, 2 * H, wc8, 2 * H, H, False, name="prep_v")
    wn = row(W["out_norm_w"])
    og_a, or_a, sall_a, tall = gdn_fwd(q, k, v, pm, 3 * H, bg, gcrow, wn, H, name="gdn_fwd")
    cs, sn = _rotary_tables(T)
    dmat, avec, bvec, gam = _retention_tables(H)
    og_b, or_b, sall_b = ret_fwd(pm, 4 * H, 5 * H, 6 * H, 7 * H, cs, sn, dmat, avec, bvec, gam, H, name="ret_fwd")
    ocat = jnp.concatenate([og_a, og_b], axis=1)
    mix = mm_nn(ocat, W["la_out"], name="la_out")
    h1, y2 = res_norm(x2, mix, row(nw[0, 1]), row(nw[0, 2]), name="resnorm_0a")
    up, act, dn = _ffn_fwd(y2, W["ffn_up"][0], W["ffn_down"][0], "0")
    h2, y0b = res_norm(h1, dn, row(nw[0, 3]), row(nw[1, 0]), name="resnorm_0b")

    pre = mm_nn(y0b, W["sg_in"], name="sg_in")
    lw, lb = row(W["ln_w"]), row(W["ln_b"])
    vn = ln_fwd(pre, lw, lb, name="sg_ln")
    ws = W["w_s"]
    bs3 = W["b_s"][:, :, None]
    us = sg_fwd(pre, vn, ws, bs3, name="sg_gate")
    mix1 = mm_nn(us, W["sg_out"], name="sg_out")
    h3, y2b = res_norm(h2, mix1, row(nw[1, 1]), row(nw[1, 2]), name="resnorm_1a")
    up1, act1, dn1 = _ffn_fwd(y2b, W["ffn_up"][1], W["ffn_down"][1], "1")
    h4, _ = res_norm(h3, dn1, row(nw[1, 3]), None, name="resnorm_1b")
    dh4, lrow = loss_head(h4, tgt, name="loss_head")
    loss = lrow[0, 0]

    G = {}
    dnw = [[None] * 4 for _ in range(2)]
    ddn1, dnw[1][3] = rms_bwd(dn1, row(nw[1, 3]), dh4, None, name="dnorm13", out_dtype=BF)
    dy2b, G["ffn_up1"], G["ffn_down1"] = _ffn_bwd(y2b, up1, act1, ddn1, W["ffn_up"][1], W["ffn_down"][1], "1")
    dh3, dnw[1][2] = rms_bwd(h3, row(nw[1, 2]), dy2b, dh4, name="dnorm12", out_dtype=F32)
    dmix1, dnw[1][1] = rms_bwd(mix1, row(nw[1, 1]), dh3, None, name="dnorm11", out_dtype=BF)
    G["sg_out"] = mm_tn(us, dmix1, name="sg_dwout")
    dus = mm_nt(dmix1, W["sg_out"], name="sg_dus")
    dpre_u, dvn, G["w_s"], dbs3 = sg_bwd(pre, vn, ws, bs3, dus, name="sg_gate_bwd")
    G["b_s"] = dbs3[:, :, 0]
    dpre_v, dlw, dlb = ln_bwd(pre, lw, dvn, name="sg_ln_bwd")
    G["ln_w"], G["ln_b"] = dlw[0], dlb[0]
    dpre = jnp.concatenate([dpre_u, dpre_v], axis=1)
    G["sg_in"] = mm_tn(y0b, dpre, name="sg_dwin", shards=N_CHIPS)
    dy0b = mm_nt(dpre, W["sg_in"], name="sg_dy")
    dh2, dnw[1][0] = rms_bwd(h2, row(nw[1, 0]), dy0b, dh3, name="dnorm10", out_dtype=F32)

    ddn, dnw[0][3] = rms_bwd(dn, row(nw[0, 3]), dh2, None, name="dnorm03", out_dtype=BF)
    dy2, G["ffn_up0"], G["ffn_down0"] = _ffn_bwd(y2, up, act, ddn, W["ffn_up"][0], W["ffn_down"][0], "0")
    dh1, dnw[0][2] = rms_bwd(h1, row(nw[0, 2]), dy2, dh2, name="dnorm02", out_dtype=F32)
    dmix, dnw[0][1] = rms_bwd(mix, row(nw[0, 1]), dh1, None, name="dnorm01", out_dtype=BF)
    G["la_out"] = mm_tn(ocat, dmix, name="la_dwout")
    docat = mm_nt(dmix, W["la_out"], name="la_docat")
    dq, dk, dv, dz, dbg, dwn = gdn_bwd(q, k, v, pm, 3 * H, bg, gcrow, wn, or_a, sall_a, tall, docat, H,
                                       name="gdn_bwd")
    drq, drk, drv, drg = ret_bwd(pm, 4 * H, 5 * H, 6 * H, 7 * H, cs, sn, dmat, avec, bvec, gam, or_b, sall_b,
                                 docat, H, H, name="ret_bwd")
    dpg, da, ddt = gates_bwd(pg, arow, dtrow, dbg, H, name="gates_bwd")
    dcq, dwq = prep_bwd_act(pm, 0, wc8, 0, H, True, dq, name="prep_dq")
    dck, dwk = prep_bwd_act(pm, H, wc8, H, H, True, dk, name="prep_dk")
    dcv, dwv = prep_bwd_act(pm, 2 * H, wc8, 2 * H, H, False, dv, name="prep_dv")
    dxq = prep_bwd_conv(dcq, wc8, 0, H, name="conv_dq")
    dxk = prep_bwd_conv(dck, wc8, H, H, name="conv_dk")
    dxv = prep_bwd_conv(dcv, wc8, 2 * H, H, name="conv_dv")
    dpm = jnp.concatenate([dxq, dxk, dxv, dz, drq, drk, drv, drg], axis=1)
    G["la_in_main"] = mm_tn(y0, dpm, name="la_dwin_main")
    G["la_in_gate"] = mm_tn(y0, dpg, name="la_dwin_gate")
    dy0 = mm_nt(dpm, W["la_in_main"], name="la_dy_main")
    dy0 = mm_nt(dpg, W["la_in_gate"], name="la_dy_gate", epilogue=_add, extras=(dy0,))
    dx, dnw[0][0] = rms_bwd(x2, row(nw[0, 0]), dy0, dh1, name="dnorm00", out_dtype=F32)

    G["norm_w"] = jnp.stack([jnp.concatenate(r, axis=0) for r in dnw], axis=0)
    G["conv_w"] = jnp.transpose(jnp.concatenate([dwq, dwk, dwv], axis=1)[:CONV_WIDTH])
    G["a_log"] = da[0, H:2 * H]
    G["dt_bias"] = ddt[0, H:2 * H]
    G["out_norm_w"] = dwn[0]
    return loss, dx, G


def _as2d(a):
    n = int(np.prod(a.shape))
    if a.shape[-1] < LANES and n % LANES == 0:
        return a.reshape(-1, LANES)
    return a.reshape(-1, a.shape[-1])


def _adamw_any(w, g, m, v, name):
    shp = w.shape
    d, m2, v2 = adamw(_as2d(w), _as2d(g.reshape(shp)), _as2d(m), _as2d(v), name=name)
    return g.reshape(shp), d.reshape(shp), m2.reshape(shp), v2.reshape(shp)


def kernel(x, norm_w, la_w_in, la_conv_w, la_a_log, la_dt_bias, la_out_norm_w, la_w_out, sg_w_in, sg_ln_w, sg_ln_b, sg_w_s, sg_b_s, sg_w_out, ffn_w_up, ffn_w_down, loss_target, m_norm_w, m_la_w_in, m_la_conv_w, m_la_a_log, m_la_dt_bias, m_la_out_norm_w, m_la_w_out, m_sg_w_in, m_sg_ln_w, m_sg_ln_b, m_sg_w_s, m_sg_b_s, m_sg_w_out, m_ffn_w_up, m_ffn_w_down, v_norm_w, v_la_w_in, v_la_conv_w, v_la_a_log, v_la_dt_bias, v_la_out_norm_w, v_la_w_out, v_sg_w_in, v_sg_ln_w, v_sg_ln_b, v_sg_w_s, v_sg_b_s, v_sg_w_out, v_ffn_w_up, v_ffn_w_down):
    weights = dict(norm_w=norm_w, la_w_in=la_w_in, la_conv_w=la_conv_w, la_a_log=la_a_log, la_dt_bias=la_dt_bias,
                   la_out_norm_w=la_out_norm_w, la_w_out=la_w_out, sg_w_in=sg_w_in, sg_ln_w=sg_ln_w,
                   sg_ln_b=sg_ln_b, sg_w_s=sg_w_s, sg_b_s=sg_b_s, sg_w_out=sg_w_out, ffn_w_up=ffn_w_up,
                   ffn_w_down=ffn_w_down)
    mom_m = dict(norm_w=m_norm_w, la_w_in=m_la_w_in, la_conv_w=m_la_conv_w, la_a_log=m_la_a_log,
                 la_dt_bias=m_la_dt_bias, la_out_norm_w=m_la_out_norm_w, la_w_out=m_la_w_out, sg_w_in=m_sg_w_in,
                 sg_ln_w=m_sg_ln_w, sg_ln_b=m_sg_ln_b, sg_w_s=m_sg_w_s, sg_b_s=m_sg_b_s, sg_w_out=m_sg_w_out,
                 ffn_w_up=m_ffn_w_up, ffn_w_down=m_ffn_w_down)
    mom_v = dict(norm_w=v_norm_w, la_w_in=v_la_w_in, la_conv_w=v_la_conv_w, la_a_log=v_la_a_log,
                 la_dt_bias=v_la_dt_bias, la_out_norm_w=v_la_out_norm_w, la_w_out=v_la_w_out, sg_w_in=v_sg_w_in,
                 sg_ln_w=v_sg_ln_w, sg_ln_b=v_sg_ln_b, sg_w_s=v_sg_w_s, sg_b_s=v_sg_b_s, sg_w_out=v_sg_w_out,
                 ffn_w_up=v_ffn_w_up, ffn_w_down=v_ffn_w_down)
    order = list(weights)

    T, D = x.shape[1], x.shape[2]
    H = la_a_log.shape[1]
    HD = H * HEAD_DIM
    xi, yi, ci = _place()
    chip = 2 * xi + yi
    c_idx = jnp.reshape(ci, (1,)).astype(jnp.int32)
    j_idx = jnp.reshape(chip, (1,)).astype(jnp.int32)

    big = [la_w_in[0], la_w_out[0], sg_w_in[0], sg_w_out[0], ffn_w_up[0], ffn_w_up[1], ffn_w_down[0], ffn_w_down[1]]
    small_shapes = [norm_w.shape, la_conv_w[0].shape, sg_ln_w[0].shape, sg_ln_b[0].shape]
    small = _pack_rows([norm_w, la_conv_w[0], sg_ln_w[0], sg_ln_b[0]])
    small = jnp.pad(small, ((0, (-small.shape[0]) % (2 * SUBLANES)), (0, 0)))
    mine = [_halves(w.astype(BF)) for w in big] + [_halves(small)]
    gathered = all_gather_halves(mine, name="gather_weights")
    gathered = [lax.dynamic_update_slice(g, m[None], (chip, 0, 0, 0)) for g, m in zip(gathered, mine)]
    whole =[g.reshape(N_CHIPS, g.shape[1] * g.shape[2], g.shape[3]) for g in gathered]
    la_in_g, la_out_g, sg_in_g, sg_out_g, up0_g, up1_g, dn0_g, dn1_g, small_g = whole
    pieces = [_unpack_rows(small_g[kk], small_shapes) for kk in range(N_CHIPS)]
    la_nat = jnp.transpose(la_in_g, (1, 0, 2)).reshape(D, -1)
    rows_of = lambda g: g.reshape(-1, g.shape[-1])
    W = dict(
        norm_w=jnp.concatenate([p[0] for p in pieces], axis=-1),
        conv_w=jnp.concatenate([p[1] for p in pieces], axis=0),
        ln_w=jnp.concatenate([p[2] for p in pieces], axis=0),
        ln_b=jnp.concatenate([p[3] for p in pieces], axis=0),
        a_log=la_a_log[0], dt_bias=la_dt_bias[0], out_norm_w=la_out_norm_w[0], w_s=sg_w_s[0], b_s=sg_b_s[0],
        la_in_main=jnp.concatenate([la_nat[:, :4 * HD], la_nat[:, 4 * HD + 2 * H:]], axis=1),
        la_in_gate=jnp.pad(la_nat[:, 4 * HD:4 * HD + 2 * H], ((0, 0), (0, LANES - 2 * H))),
        la_out=rows_of(la_out_g), sg_in=sg_in_g, sg_out=rows_of(sg_out_g),
        ffn_up=[up0_g, up1_g], ffn_down=[rows_of(dn0_g), rows_of(dn1_g)],
    )

    loss_local, dx, G = _train_local(x[0], loss_target[0], W)
    loss = lax.psum(loss_local, ("x", "y", "c"))

    g_la_nat = jnp.concatenate([G["la_in_main"][:, :4 * HD], G["la_in_gate"][:, :2 * H], G["la_in_main"][:, 4 * HD:]],
                               axis=1)
    g_la_in = jnp.transpose(g_la_nat.reshape(D, N_CHIPS, -1), (1, 0, 2))
    by_chip = lambda g: g.reshape(N_CHIPS, -1, g.shape[-1])
    g_big = [g_la_in, by_chip(G["la_out"]), G["sg_in"], by_chip(G["sg_out"]), G["ffn_up0"], G["ffn_up1"],
             by_chip(G["ffn_down0"]), by_chip(G["ffn_down1"])]
    g_big = [g.reshape(N_CHIPS, 2, g.shape[1] // 2, g.shape[2]) for g in g_big]
    sib = pair_exchange(g_big, name="grads_pair_exchange")
    part = [pair_sum(g, r, c_idx, name=f"grads_pair_sum_{a}") for a, (g, r) in enumerate(zip(g_big, sib))]
    others = chip_exchange(part, name="grads_chip_exchange")
    done = [chip_sum(p, r, j_idx, name=f"grads_chip_sum_{a}") for a, (p, r) in enumerate(zip(part, others))]
    from_sib = pair_share(done, name="grads_pair_share")

    def big_update(a, w, m, v, tag):
        shp = w.shape
        r4 = adamw_halves(w.reshape(-1, shp[-1]), done[a], from_sib[a], m.reshape(-1, shp[-1]),
                          v.reshape(-1, shp[-1]), c_idx, name=f"adamw_{tag}")
        return [t.reshape(shp) for t in r4]

    big_res = dict(
        la_w_in=big_update(0, la_w_in, m_la_w_in, v_la_w_in, "la_w_in"),
        la_w_out=big_update(1, la_w_out, m_la_w_out, v_la_w_out, "la_w_out"),
        sg_w_in=big_update(2, sg_w_in, m_sg_w_in, v_sg_w_in, "sg_w_in"),
        sg_w_out=big_update(3, sg_w_out, m_sg_w_out, v_sg_w_out, "sg_w_out"),
    )
    for nm, a0, w, m, v in (("ffn_w_up", 4, ffn_w_up, m_ffn_w_up, v_ffn_w_up),
                            ("ffn_w_down", 6, ffn_w_down, m_ffn_w_down, v_ffn_w_down)):
        per_layer = [big_update(a0 + l, w[l], m[l], v[l], f"{nm}_{l}") for l in range(2)]
        big_res[nm] = [jnp.stack([per_layer[0][t], per_layer[1][t]]) for t in range(4)]

    small_names = ["norm_w", "conv_w", "ln_w", "ln_b", "a_log", "dt_bias", "out_norm_w", "w_s", "b_s"]
    small_full = [G[nm] for nm in small_names]
    summed = _unpack_rows(all_reduce_small(_pack_rows(small_full), name="grads_all_reduce_small"),
                          [g.shape for g in small_full])
    sm = dict(zip(small_names, summed))
    own = lambda full, axis: lax.dynamic_slice_in_dim(full, chip * (full.shape[axis] // N_CHIPS),
                                                      full.shape[axis] // N_CHIPS, axis)
    grads = dict(
        norm_w=own(sm["norm_w"], 2), la_conv_w=own(sm["conv_w"], 0), la_a_log=sm["a_log"],
        la_dt_bias=sm["dt_bias"], la_out_norm_w=sm["out_norm_w"],
        sg_ln_w=own(sm["ln_w"], 0), sg_ln_b=own(sm["ln_b"], 0), sg_w_s=sm["w_s"], sg_b_s=sm["b_s"],
    )

    res = {nm: big_res[nm] if nm in big_res else
           _adamw_any(weights[nm], grads[nm], mom_m[nm], mom_v[nm], f"adamw_{nm}") for nm in order}
    return (loss, dx.reshape(x.shape), *[res[nm][0] for nm in order], *[res[nm][1] for nm in order],
            *[res[nm][2] for nm in order], *[res[nm][3] for nm in order])
```

```python
import functools
import math

import numpy as np
import jax
import jax.numpy as jnp
from jax import lax
from jax.experimental import pallas as pl
from jax.experimental.pallas import tpu as pltpu

F32 = jnp.float32
BF = jnp.bfloat16
HI = lax.Precision.HIGHEST

V7X_VMEM_BYTES = 64 * 1024 * 1024
VMEM_LIMIT = (V7X_VMEM_BYTES * 3) // 4
LANES = 128
SUBLANES = 8
BF16_ROWS = 16
HEAD_DIM = 128
LA_CHUNK = 64
SG_CHUNK = 128
CONV_WIDTH = 4
ROPE_BASE = 10000.0
EPS = 1e-6
L2_EPS = 1e-6
N_CHIPS = 4
N_DEV = 8

ADAM_LR = 0.001
ADAM_B1 = 0.9
ADAM_B2 = 0.999
ADAM_EPS = 1e-08
ADAM_WD = 0.01
ADAM_STEP = 10

MESH = pl.DeviceIdType.MESH
ANY = pl.BlockSpec(memory_space=pl.ANY)

NN = (((1,), (0,)), ((), ()))
NT = (((1,), (1,)), ((), ()))
TN = (((0,), (0,)), ((), ()))


def _pcall(body, **kw):
    return pl.pallas_call(body, **kw)


def _cp(n_axes):
    return pltpu.CompilerParams(dimension_semantics=("arbitrary",) * n_axes, vmem_limit_bytes=VMEM_LIMIT)


def _tile(n, pref, unit=LANES):
    if n <= pref:
        return n
    t = (pref // unit) * unit
    while t >= unit:
        if n % t == 0:
            return t
        t -= unit
    return n


def _dot(a, b, dims=NN):
    return lax.dot_general(a.astype(BF), b.astype(BF), dims, preferred_element_type=F32)


def _dot_hi(a, b, dims=NN):
    return lax.dot_general(a.astype(F32), b.astype(F32), dims, precision=HI, preferred_element_type=F32)


def _sigmoid(x):
    return 1.0 / (1.0 + jnp.exp(-x))


def _silu(x):
    return x * _sigmoid(x)


def _dsilu(x):
    s = _sigmoid(x)
    return s * (1.0 + x * (1.0 - s))


GELU_C = math.sqrt(2.0 / math.pi)
GELU_A = 0.044715


def _gelu(x):
    return 0.5 * x * (1.0 + jnp.tanh(GELU_C * (x + GELU_A * x * x * x)))


def _dgelu(x):
    t = jnp.tanh(GELU_C * (x + GELU_A * x * x * x))
    return 0.5 * (1.0 + t) + 0.5 * x * (1.0 - t * t) * GELU_C * (1.0 + 3.0 * GELU_A * x * x)


def _matmul(a, b, *, dims, grid, a_spec, b_spec, out_shape, out_spec, acc_shape, name,
            epilogue=None, extras=(), extra_specs=()):
    nk = grid[2]
    outs = tuple(out_shape) if isinstance(out_shape, (tuple, list)) else (out_shape,)
    out_specs = tuple(out_spec) if isinstance(out_spec, (tuple, list)) else (out_spec,)
    n_ex, n_out = len(extras), len(outs)

    def body(*refs):
        a_ref, b_ref = refs[0], refs[1]
        ex = refs[2:2 + n_ex]
        o = refs[2 + n_ex:2 + n_ex + n_out]
        k = pl.program_id(2)
        part = lax.dot_general(a_ref[...].astype(BF), b_ref[...].astype(BF), dims, preferred_element_type=F32)

        def finish(val):
            res = epilogue(val, *[e[...] for e in ex]) if epilogue is not None else (val,)
            for r, oref in zip(res, o):
                oref[...] = r.astype(oref.dtype)

        if nk == 1:
            finish(part)
        else:
            acc = refs[2 + n_ex + n_out]

            @pl.when(k == 0)
            def _():
                acc[...] = part

            @pl.when(k > 0)
            def _():
                acc[...] += part

            @pl.when(k == nk - 1)
            def _():
                finish(acc[...])

    res = _pcall(
        body, name=name, grid=grid,
        in_specs=[a_spec, b_spec, *extra_specs],
        out_specs=out_specs if len(outs) > 1 else out_specs[0],
        out_shape=outs if len(outs) > 1 else outs[0],
        scratch_shapes=[pltpu.VMEM(acc_shape, F32)] if nk > 1 else [],
        compiler_params=_cp(3),
    )(a, b, *extras)
    return res


def mm_nn(a, w, *, name, out_dtypes=(F32,), epilogue=None, extras=(), tm=1024, tn=1024, tk=2048):
    M, K = a.shape
    if w.ndim == 3:
        S, _, Ns = w.shape
        N = S * Ns
    else:
        S, Ns = 1, w.shape[1]
        N = Ns
    tm, tn, tk = _tile(M, tm), _tile(Ns, tn), _tile(K, tk)
    npb = Ns // tn
    grid = (M // tm, N // tn, K // tk)
    a_spec = pl.BlockSpec((tm, tk), lambda i, j, k: (i, k))
    if w.ndim == 3:
        b_spec = pl.BlockSpec((None, tk, tn), lambda i, j, k: (j // npb, k, j % npb))
    else:
        b_spec = pl.BlockSpec((tk, tn), lambda i, j, k: (k, j))
    o_spec = pl.BlockSpec((tm, tn), lambda i, j, k: (i, j))
    outs = tuple(jax.ShapeDtypeStruct((M, N), d) for d in out_dtypes)
    res = _matmul(a, w, dims=NN, grid=grid, a_spec=a_spec, b_spec=b_spec,
                  out_shape=outs, out_spec=(o_spec,) * len(outs), acc_shape=(tm, tn), name=name,
                  epilogue=epilogue, extras=extras, extra_specs=(o_spec,) * len(extras))
    return res


def mm_nt(a, w, *, name, out_dtypes=(F32,), epilogue=None, extras=(), tm=1024, tn=1024, tk=2048):
    M, Kc = a.shape
    if w.ndim == 3:
        S, Nout, Ks = w.shape
    else:
        S, (Nout, Ks) = 1, w.shape
    assert S * Ks == Kc
    tm, tn, tk = _tile(M, tm), _tile(Nout, tn), _tile(Ks, tk)
    kpb = Ks // tk
    grid = (M // tm, Nout // tn, Kc // tk)
    a_spec = pl.BlockSpec((tm, tk), lambda i, j, k: (i, k))
    if w.ndim == 3:
        b_spec = pl.BlockSpec((None, tn, tk), lambda i, j, k: (k // kpb, j, k % kpb))
    else:
        b_spec = pl.BlockSpec((tn, tk), lambda i, j, k: (j, k))
    o_spec = pl.BlockSpec((tm, tn), lambda i, j, k: (i, j))
    outs = tuple(jax.ShapeDtypeStruct((M, Nout), d) for d in out_dtypes)
    return _matmul(a, w, dims=NT, grid=grid, a_spec=a_spec, b_spec=b_spec,
                   out_shape=outs, out_spec=(o_spec,) * len(outs), acc_shape=(tm, tn), name=name,
                   epilogue=epilogue, extras=extras, extra_specs=(o_spec,) * len(extras))


def mm_tn(x, dy, *, name, shards=1, tm=1024, tn=1024, tk=2048):
    T, Kin = x.shape
    N = dy.shape[1]
    Ns = N // shards
    tm, tn, tk = _tile(Kin, tm), _tile(Ns, tn), _tile(T, tk)
    npb = Ns // tn
    grid = (Kin // tm, N // tn, T // tk)
    a_spec = pl.BlockSpec((tk, tm), lambda i, j, k: (k, i))
    b_spec = pl.BlockSpec((tk, tn), lambda i, j, k: (k, j))
    if shards > 1:
        o_spec = pl.BlockSpec((None, tm, tn), lambda i, j, k: (j // npb, i, j % npb))
        out = jax.ShapeDtypeStruct((shards, Kin, Ns), F32)
    else:
        o_spec = pl.BlockSpec((tm, tn), lambda i, j, k: (i, j))
        out = jax.ShapeDtypeStruct((Kin, N), F32)
    return _matmul(x, dy, dims=TN, grid=grid, a_spec=a_spec, b_spec=b_spec,
                   out_shape=out, out_spec=o_spec, acc_shape=(tm, tn), name=name)


ROW_TILE = 256


def _rows(tr, d):
    return pl.BlockSpec((tr, d), lambda i: (i, 0))


def _fixed(shape):
    nd = len(shape)
    return pl.BlockSpec(shape, lambda *_: (0,) * nd)


def _rms(xv, w):
    r = lax.rsqrt(jnp.mean(xv * xv, axis=-1, keepdims=True) + EPS)
    return xv * r * w


def rms_fwd(x, w, *, name):
    T, D = x.shape
    tr = _tile(T, ROW_TILE, SUBLANES)

    def body(x_ref, w_ref, y_ref):
        y_ref[...] = _rms(x_ref[...], w_ref[...]).astype(y_ref.dtype)

    return _pcall(body, name=name, grid=(T // tr,), in_specs=[_rows(tr, D), _fixed((1, D))],
                  out_specs=_rows(tr, D), out_shape=jax.ShapeDtypeStruct((T, D), BF), compiler_params=_cp(1))(x, w)


def res_norm(h, m, wa, wb, *, name):
    T, D = h.shape
    tr = _tile(T, ROW_TILE, SUBLANES)
    second = wb is not None

    def body(*refs):
        if second:
            h_ref, m_ref, wa_ref, wb_ref, ho_ref, y_ref = refs
        else:
            h_ref, m_ref, wa_ref, ho_ref = refs
        ho = h_ref[...] + _rms(m_ref[...], wa_ref[...])
        ho_ref[...] = ho
        if second:
            y_ref[...] = _rms(ho, wb_ref[...]).astype(y_ref.dtype)

    ins = [h, m, wa] + ([wb] if second else [])
    in_specs = [_rows(tr, D), _rows(tr, D), _fixed((1, D))] + ([_fixed((1, D))] if second else [])
    out_shape = [jax.ShapeDtypeStruct((T, D), F32)] + ([jax.ShapeDtypeStruct((T, D), BF)] if second else [])
    out_specs = [_rows(tr, D)] * len(out_shape)
    res = _pcall(body, name=name, grid=(T // tr,), in_specs=in_specs, out_specs=out_specs,
                 out_shape=out_shape, compiler_params=_cp(1))(*ins)
    return tuple(res) if second else (res[0], None)


def rms_bwd(x, w, dy, dres, *, name, out_dtype):
    T, D = x.shape
    tr = _tile(T, ROW_TILE, SUBLANES)
    has_res = dres is not None

    def body(*refs):
        if has_res:
            x_ref, w_ref, dy_ref, dr_ref, dx_ref, dw_ref = refs
        else:
            x_ref, w_ref, dy_ref, dx_ref, dw_ref = refs
        i = pl.program_id(0)
        xv = x_ref[...]
        r = lax.rsqrt(jnp.mean(xv * xv, axis=-1, keepdims=True) + EPS)
        xh = xv * r
        dyv = dy_ref[...].astype(F32)
        dyw = dyv * w_ref[...]
        dx = r * (dyw - xh * jnp.mean(dyw * xh, axis=-1, keepdims=True))
        if has_res:
            dx = dx + dr_ref[...].astype(F32)
        dx_ref[...] = dx.astype(dx_ref.dtype)
        part = jnp.sum(dyv * xh, axis=0, keepdims=True)

        @pl.when(i == 0)
        def _():
            dw_ref[...] = part

        @pl.when(i > 0)
        def _():
            dw_ref[...] += part

    ins = [x, w, dy] + ([dres] if has_res else [])
    in_specs = [_rows(tr, D), _fixed((1, D)), _rows(tr, D)] + ([_rows(tr, D)] if has_res else [])
    return _pcall(body, name=name, grid=(T // tr,), in_specs=in_specs,
                  out_specs=[_rows(tr, D), _fixed((1, D))],
                  out_shape=[jax.ShapeDtypeStruct((T, D), out_dtype), jax.ShapeDtypeStruct((1, D), F32)],
                  compiler_params=_cp(1))(*ins)


def loss_head(h, tgt, *, name):
    T, D = h.shape
    tr = _tile(T, ROW_TILE, SUBLANES)

    def body(h_ref, t_ref, dh_ref, l_ref):
        i = pl.program_id(0)
        e = h_ref[...] - t_ref[...]
        dh_ref[...] = e * (1.0 / D)
        part = 0.5 * jnp.sum(jnp.mean(e * e, axis=-1, keepdims=True), axis=0, keepdims=True)
        part = jnp.broadcast_to(part, (1, LANES))

        @pl.when(i == 0)
        def _():
            l_ref[...] = part

        @pl.when(i > 0)
        def _():
            l_ref[...] += part

    return _pcall(body, name=name, grid=(T // tr,), in_specs=[_rows(tr, D), _rows(tr, D)],
                  out_specs=[_rows(tr, D), _fixed((1, LANES))],
                  out_shape=[jax.ShapeDtypeStruct((T, D), F32), jax.ShapeDtypeStruct((1, LANES), F32)],
                  compiler_params=_cp(1))(h, tgt)


def adamw(w, g, m, v, *, name):
    R, C = w.shape
    tr = _tile(R, max(SUBLANES, (1 << 18) // C), SUBLANES)
    c1 = 1.0 - ADAM_B1 ** ADAM_STEP
    c2 = 1.0 - ADAM_B2 ** ADAM_STEP

    def body(w_ref, g_ref, m_ref, v_ref, d_ref, mo_ref, vo_ref):
        gv = g_ref[...]
        m2 = ADAM_B1 * m_ref[...] + (1.0 - ADAM_B1) * gv
        v2 = ADAM_B2 * v_ref[...] + (1.0 - ADAM_B2) * (gv * gv)
        d_ref[...] = -ADAM_LR * ((m2 / c1) / (jnp.sqrt(v2 / c2) + ADAM_EPS) + ADAM_WD * w_ref[...])
        mo_ref[...] = m2
        vo_ref[...] = v2

    spec = _rows(tr, C)
    sds = jax.ShapeDtypeStruct((R, C), F32)
    return _pcall(body, name=name, grid=(R // tr,), in_specs=[spec] * 4, out_specs=[spec] * 3,
                  out_shape=[sds] * 3, compiler_params=_cp(1))(w, g, m, v)


HALO = SUBLANES


def _conv_down(xx, w_ref):
    acc = xx * w_ref[pl.ds(CONV_WIDTH - 1, 1), :]
    for d in range(1, CONV_WIDTH):
        acc = acc + pltpu.roll(xx, d, 0) * w_ref[pl.ds(CONV_WIDTH - 1 - d, 1), :]
    return acc


def _conv_tile(x_ref, halo_ref, w_ref, first):
    xs = x_ref[...]
    hal = jnp.where(first, 0.0, halo_ref[...])
    cat = jnp.concatenate([hal, xs[0:HALO]], axis=0)
    return jnp.concatenate([_conv_down(cat, w_ref)[HALO:2 * HALO], _conv_down(xs, w_ref)[HALO:]], axis=0)


def _shift_down_tile(x_ref, halo_ref, first, d):
    xs = x_ref[...]
    if d == 0:
        return xs
    hal = jnp.where(first, 0.0, halo_ref[...])
    cat = jnp.concatenate([hal, xs[0:HALO]], axis=0)
    return jnp.concatenate([pltpu.roll(cat, d, 0)[HALO:2 * HALO], pltpu.roll(xs, d, 0)[HALO:]], axis=0)


def _l2n(s):
    return s * lax.rsqrt(jnp.sum(s * s, axis=-1, keepdims=True) + L2_EPS)


PREP_ROWS = 512


def _l2n_groups(s, nb):
    return jnp.concatenate([_l2n(s[:, g * LANES:(g + 1) * LANES]) for g in range(nb)], axis=1)


def prep_fwd(pm, off, wc8, woff, nblk, l2, *, name):
    T = pm.shape[0]
    tr = _tile(T, PREP_ROWS, SUBLANES)
    hb = tr // HALO
    wb = _heads_per_step(nblk)
    wl = wb * LANES

    def body(x_ref, halo_ref, w_ref, o_ref):
        i = pl.program_id(0)
        s = _silu(_conv_tile(x_ref, halo_ref, w_ref, i == 0))
        o_ref[...] = _l2n_groups(s, wb) if l2 else s

    return _pcall(
        body, name=name, grid=(T // tr, nblk // wb),
        in_specs=[pl.BlockSpec((tr, wl), lambda i, c: (i, off // wb + c)),
                  pl.BlockSpec((HALO, wl), lambda i, c: (jnp.maximum(i * hb - 1, 0), off // wb + c)),
                  pl.BlockSpec((SUBLANES, wl), lambda i, c: (0, woff // wb + c))],
        out_specs=pl.BlockSpec((tr, wl), lambda i, c: (i, c)),
        out_shape=jax.ShapeDtypeStruct((T, nblk * LANES), F32), compiler_params=_cp(2))(pm, pm, wc8)


def prep_bwd_act(pm, off, wc8, woff, nblk, l2, dout, *, name):
    T = pm.shape[0]
    tr = _tile(T, PREP_ROWS, SUBLANES)
    hb = tr // HALO
    wb = _heads_per_step(nblk)
    wl = wb * LANES

    def l2_bwd(s, do):
        r = lax.rsqrt(jnp.sum(s * s, axis=-1, keepdims=True) + L2_EPS)
        nrm = s * r
        return r * (do - nrm * jnp.sum(do * nrm, axis=-1, keepdims=True))

    def body(x_ref, halo_ref, w_ref, do_ref, dc_ref, dw_ref):
        i = pl.program_id(1)
        first = i == 0
        y = _conv_tile(x_ref, halo_ref, w_ref, first)
        s = _silu(y)
        do = do_ref[...]
        if l2:
            ds = jnp.concatenate([l2_bwd(s[:, g * LANES:(g + 1) * LANES], do[:, g * LANES:(g + 1) * LANES])
                                  for g in range(wb)], axis=1)
        else:
            ds = do
        dc = ds * _dsilu(y)
        dc_ref[...] = dc

        @pl.when(first)
        def _():
            dw_ref[...] = jnp.zeros_like(dw_ref)

        for j in range(CONV_WIDTH):
            xsh = _shift_down_tile(x_ref, halo_ref, first, CONV_WIDTH - 1 - j)
            dw_ref[pl.ds(j, 1), :] += jnp.sum(dc * xsh, axis=0, keepdims=True)

    return _pcall(
        body, name=name, grid=(nblk // wb, T // tr),
        in_specs=[pl.BlockSpec((tr, wl), lambda c, i: (i, off // wb + c)),
                  pl.BlockSpec((HALO, wl), lambda c, i: (jnp.maximum(i * hb - 1, 0), off // wb + c)),
                  pl.BlockSpec((SUBLANES, wl), lambda c, i: (0, woff // wb + c)),
                  pl.BlockSpec((tr, wl), lambda c, i: (i, c))],
        out_specs=[pl.BlockSpec((tr, wl), lambda c, i: (i, c)),
                   pl.BlockSpec((SUBLANES, wl), lambda c, i: (0, c))],
        out_shape=[jax.ShapeDtypeStruct((T, nblk * LANES), F32),
                   jax.ShapeDtypeStruct((SUBLANES, nblk * LANES), F32)],
        compiler_params=_cp(2))(pm, pm, wc8, dout)


def prep_bwd_conv(dc, wc8, woff, nblk, *, name):
    T = dc.shape[0]
    tr = _tile(T, PREP_ROWS, SUBLANES)
    hb = tr // HALO
    nt = T // tr
    last_halo = T // HALO - 1
    wb = _heads_per_step(nblk)
    wl = wb * LANES

    def up(xx, w_ref):
        rows = xx.shape[0]
        acc = xx * w_ref[pl.ds(CONV_WIDTH - 1, 1), :]
        for d in range(1, CONV_WIDTH):
            acc = acc + pltpu.roll(xx, rows - d, 0) * w_ref[pl.ds(CONV_WIDTH - 1 - d, 1), :]
        return acc

    def body(x_ref, halo_ref, w_ref, o_ref):
        i = pl.program_id(0)
        xs = x_ref[...]
        hal = jnp.where(i == nt - 1, 0.0, halo_ref[...])
        cat = jnp.concatenate([xs[tr - HALO:tr], hal], axis=0)
        out = jnp.concatenate([up(xs, w_ref)[:tr - HALO], up(cat, w_ref)[0:HALO]], axis=0)
        o_ref[...] = out.astype(o_ref.dtype)

    return _pcall(
        body, name=name, grid=(nt, nblk // wb),
        in_specs=[pl.BlockSpec((tr, wl), lambda i, c: (i, c)),
                  pl.BlockSpec((HALO, wl), lambda i, c: (jnp.minimum((i + 1) * hb, last_halo), c)),
                  pl.BlockSpec((SUBLANES, wl), lambda i, c: (0, woff // wb + c))],
        out_specs=pl.BlockSpec((tr, wl), lambda i, c: (i, c)),
        out_shape=jax.ShapeDtypeStruct((T, nblk * LANES), BF), compiler_params=_cp(2))(dc, dc, wc8)


def _softplus(x):
    return jnp.maximum(x, 0.0) + jnp.log(1.0 + jnp.exp(-jnp.abs(x)))


def _tril_ones(c):
    t = lax.broadcasted_iota(jnp.int32, (c, c), 0)
    s = lax.broadcasted_iota(jnp.int32, (c, c), 1)
    return (t >= s).astype(F32)


def _triu_ones(c):
    t = lax.broadcasted_iota(jnp.int32, (c, c), 0)
    s = lax.broadcasted_iota(jnp.int32, (c, c), 1)
    return (t <= s).astype(F32)


def gates_fwd(pg, arow, dtrow, H, *, name):
    T = pg.shape[0]
    C = LA_CHUNK
    N = T // C

    def body(x_ref, a_ref, dt_ref, bg_ref, gr_ref):
        x = x_ref[...]
        lane = lax.broadcasted_iota(jnp.int32, (C, LANES), 1)
        g = -jnp.exp(a_ref[...]) * _softplus(x + dt_ref[...])
        g = jnp.where((lane >= H) & (lane < 2 * H), g, 0.0)
        lm = _tril_ones(C)
        gc = _dot_hi(lm, g)
        bg_ref[...] = jnp.where(lane < H, _sigmoid(x), gc)
        gr_ref[...] = _dot_hi(g, _triu_ones(C), TN)

    return _pcall(
        body, name=name, grid=(N,),
        in_specs=[pl.BlockSpec((C, LANES), lambda n: (n, 0)), _fixed((1, LANES)), _fixed((1, LANES))],
        out_specs=[pl.BlockSpec((C, LANES), lambda n: (n, 0)), pl.BlockSpec((None, LANES, C), lambda n: (n, 0, 0))],
        out_shape=[jax.ShapeDtypeStruct((T, LANES), F32), jax.ShapeDtypeStruct((N, LANES, C), F32)],
        compiler_params=_cp(1))(pg, arow, dtrow)


def gates_bwd(pg, arow, dtrow, dbg, H, *, name):
    T = pg.shape[0]
    C = LA_CHUNK
    N = T // C

    def body(x_ref, a_ref, dt_ref, d_ref, dx_ref, da_ref, ddt_ref):
        n = pl.program_id(0)
        x = x_ref[...]
        d = d_ref[...]
        lane = lax.broadcasted_iota(jnp.int32, (C, LANES), 1)
        in_g = (lane >= H) & (lane < 2 * H)
        e = jnp.exp(a_ref[...])
        xs = x + dt_ref[...]
        g = -e * _softplus(xs)
        dg = _dot_hi(_tril_ones(C), jnp.where(in_g, d, 0.0), TN)
        dxs = jnp.where(in_g, dg * (-e) * _sigmoid(xs), 0.0)
        beta = _sigmoid(x)
        dx_ref[...] = jnp.where(lane < H, d * beta * (1.0 - beta), dxs).astype(dx_ref.dtype)
        pa = jnp.sum(jnp.where(in_g, dg * g, 0.0), axis=0, keepdims=True)
        pd = jnp.sum(dxs, axis=0, keepdims=True)

        @pl.when(n == 0)
        def _():
            da_ref[...] = pa
            ddt_ref[...] = pd

        @pl.when(n > 0)
        def _():
            da_ref[...] += pa
            ddt_ref[...] += pd

    return _pcall(
        body, name=name, grid=(N,),
        in_specs=[pl.BlockSpec((C, LANES), lambda n: (n, 0)), _fixed((1, LANES)), _fixed((1, LANES)),
                  pl.BlockSpec((C, LANES), lambda n: (n, 0))],
        out_specs=[pl.BlockSpec((C, LANES), lambda n: (n, 0)), _fixed((1, LANES)), _fixed((1, LANES))],
        out_shape=[jax.ShapeDtypeStruct((T, LANES), BF), jax.ShapeDtypeStruct((1, LANES), F32),
                   jax.ShapeDtypeStruct((1, LANES), F32)],
        compiler_params=_cp(1))(pg, arow, dtrow, dbg)


QK_SCALE = HEAD_DIM ** -0.5


HEADS_PER_STEP = 4


def _heads_per_step(H):
    hb = HEADS_PER_STEP
    while H % hb:
        hb //= 2
    return hb


def _head_rstd(o):
    return lax.rsqrt(jnp.mean(o * o, axis=-1, keepdims=True) + EPS)


def _gdn_gates(bg_ref, gr_ref, h, H):
    C = LA_CHUNK
    bgv = bg_ref[...]
    lane = lax.broadcasted_iota(jnp.int32, (C, LANES), 1)
    beta = jnp.sum(jnp.where(lane == h, bgv, 0.0), axis=1, keepdims=True)
    gc = jnp.sum(jnp.where(lane == H + h, bgv, 0.0), axis=1, keepdims=True)
    grow = gr_ref[pl.ds(H + h, 1), :]
    ri = lax.broadcasted_iota(jnp.int32, (C, 1), 0)
    gl = jnp.sum(jnp.where(ri == C - 1, gc, 0.0), axis=0, keepdims=True)
    return beta, gc, grow, gl


def _chunk_masks():
    C = LA_CHUNK
    ti = lax.broadcasted_iota(jnp.int32, (C, C), 0)
    si = lax.broadcasted_iota(jnp.int32, (C, C), 1)
    return ti >= si, ti > si, ti == si


def _decay(gc, grow, causal):
    return jnp.where(causal, jnp.exp(jnp.where(causal, gc - grow, 0.0)), 0.0)


def _interleave(gens):
    gens = list(gens)
    results = [None] * len(gens)
    live = list(range(len(gens)))
    while live:
        still = []
        for i in live:
            try:
                next(gens[i])
                still.append(i)
            except StopIteration as stop:
                results[i] = stop.value
        live = still
    return results


def _unit_lower_inverse(a, eye):
    x = -a
    p = jnp.where(eye, 1.0, 0.0) + x
    for _ in range(5):
        x = _dot_hi(x, x)
        yield
        p = p + _dot_hi(p, x)
        yield
    return p


def gdn_fwd(q, k, v, pm, zoff, bg, gcrow, wn, H, *, name):
    T = q.shape[0]
    C = LA_CHUNK
    N = T // C
    hd = HEAD_DIM

    HB = _heads_per_step(H)

    def body(q_ref, k_ref, v_ref, z_ref, bg_ref, gr_ref, wn_ref, og_ref, or_ref, sall_ref, tall_ref, S):
        n = pl.program_id(0)
        hg = pl.program_id(1)
        causal, strict, eye = _chunk_masks()

        @pl.when((n == 0) & (hg == 0))
        def _():
            S[...] = jnp.zeros_like(S)

        states = [S[hg * HB + i] for i in range(HB)]

        def head(i):
            h = hg * HB + i
            sl = slice(i * hd, (i + 1) * hd)
            beta, gc, grow, gl = _gdn_gates(bg_ref, gr_ref, h, H)
            dm = _decay(gc, grow, causal)
            qs = q_ref[:, sl] * QK_SCALE
            kk = k_ref[:, sl]
            vv = v_ref[:, sl]
            eg = jnp.exp(gc)
            kb = kk * beta
            a = jnp.where(strict, _dot(kb, kk, NT) * dm, 0.0)
            yield
            tm = yield from _unit_lower_inverse(a, eye)
            u = _dot(tm, vv * beta)
            w = _dot(tm, kb * eg)
            qk = jnp.where(causal, _dot(qs, kk, NT) * dm, 0.0)
            yield
            s0 = states[i]
            vnew = u - _dot(w, s0)
            o = _dot(qs * eg, s0)
            yield
            o = o + _dot(qk, vnew)
            s1 = s0 * jnp.exp(gl) + _dot(kk * jnp.exp(gl - gc), vnew, TN)
            yield
            sall_ref[i] = s0
            tall_ref[i] = tm
            or_ref[:, sl] = o
            og_ref[:, sl] = (o * _head_rstd(o) * wn_ref[...] * _silu(z_ref[:, sl])).astype(og_ref.dtype)
            return s1

        for i, s1 in enumerate(_interleave([head(i) for i in range(HB)])):
            S[hg * HB + i] = s1

    blk = lambda off: pl.BlockSpec((C, HB * hd), lambda n, h: (n, off // HB + h))
    return _pcall(
        body, name=name, grid=(N, H // HB),
        in_specs=[blk(0), blk(0), blk(0), blk(zoff),
                  pl.BlockSpec((C, LANES), lambda n, h: (n, 0)),
                  pl.BlockSpec((None, LANES, C), lambda n, h: (n, 0, 0)),
                  _fixed((1, hd))],
        out_specs=[blk(0), blk(0),
                   pl.BlockSpec((None, HB, hd, hd), lambda n, h: (n, h, 0, 0)),
                   pl.BlockSpec((None, HB, C, C), lambda n, h: (n, h, 0, 0))],
        out_shape=[jax.ShapeDtypeStruct((T, H * hd), BF), jax.ShapeDtypeStruct((T, H * hd), F32),
                   jax.ShapeDtypeStruct((N, H, hd, hd), F32), jax.ShapeDtypeStruct((N, H, C, C), F32)],
        scratch_shapes=[pltpu.VMEM((H, hd, hd), F32)],
        compiler_params=_cp(2))(q, k, v, pm, bg, gcrow, wn)


def gdn_bwd(q, k, v, pm, zoff, bg, gcrow, wn, oraw, sall, tall, dog, H, *, name):
    T = q.shape[0]
    C = LA_CHUNK
    N = T // C
    hd = HEAD_DIM

    HB = _heads_per_step(H)

    def body(*refs):
        dbg_ref, dwn_ref, dS = refs[15], refs[16], refs[17]
        n = pl.program_id(0)
        hg = pl.program_id(1)

        @pl.when((n == 0) & (hg == 0))
        def _():
            dwn_ref[...] = jnp.zeros_like(dwn_ref)
            dS[...] = jnp.zeros_like(dS)

        @pl.when(hg == 0)
        def _():
            dbg_ref[...] = jnp.zeros_like(dbg_ref)

        ds_in = [dS[hg * HB + i] for i in range(HB)]
        outs = _interleave([head(i, hg * HB + i, ds_in[i], *refs) for i in range(HB)])
        for i in range(HB):
            dS[hg * HB + i] = outs[i][0]
        dwn_ref[...] += sum(o[1] for o in outs)
        dbg_ref[...] += sum(o[2] for o in outs)

    def head(i, h, ds1, q_ref, k_ref, v_ref, z_ref, bg_ref, gr_ref, wn_ref, or_ref, sall_ref, tall_ref, dog_ref,
             dq_ref, dk_ref, dv_ref, dz_ref, dbg_ref, dwn_ref, dS):
        sl = slice(i * hd, (i + 1) * hd)
        beta, gc, grow, gl = _gdn_gates(bg_ref, gr_ref, h, H)
        causal, strict, eye = _chunk_masks()
        dm = _decay(gc, grow, causal)
        qs = q_ref[:, sl] * QK_SCALE
        kk = k_ref[:, sl]
        vv = v_ref[:, sl]
        zz = z_ref[:, sl]
        o = or_ref[:, sl]
        dog = dog_ref[:, sl]
        wn_v = wn_ref[...]
        s0 = sall_ref[i]
        tm = tall_ref[i]

        rstd = _head_rstd(o)
        on = o * rstd
        sz = _silu(zz)
        don = dog * wn_v * sz
        dwn_part = jnp.sum(dog * on * sz, axis=0, keepdims=True)
        dz_ref[:, sl] = (dog * on * wn_v * _dsilu(zz)).astype(dz_ref.dtype)
        do = rstd * (don - on * jnp.mean(don * on, axis=-1, keepdims=True))

        eg = jnp.exp(gc)
        kb = kk * beta
        vb = vv * beta
        kbg = kb * eg
        a = jnp.where(strict, _dot(kb, kk, NT) * dm, 0.0)
        u = _dot(tm, vb)
        w = _dot(tm, kbg)
        qk = jnp.where(causal, _dot(qs, kk, NT) * dm, 0.0)
        dqdec = _dot(do, s0, NT)
        yield
        vnew = u - _dot(w, s0)
        qdec = qs * eg
        etail = jnp.exp(gl - gc)
        ktail = kk * etail
        egl = jnp.exp(gl)
        dvnew = _dot(qk, do, TN) + _dot(ktail, ds1)
        yield
        dqk = jnp.where(causal, _dot(do, vnew, NT), 0.0)
        dktail = _dot(vnew, ds1, NT)
        dcd = jnp.sum(jnp.sum(s0 * ds1, axis=1, keepdims=True), axis=0, keepdims=True)
        ds0 = egl * ds1 + _dot(qdec, do, TN) - _dot(w, dvnew, TN)
        dw = -_dot(dvnew, s0, NT)
        dvb = _dot(tm, dvnew, TN)
        yield
        dkbg = _dot(tm, dw, TN)
        dtm = _dot(dvnew, vb, NT) + _dot(dw, kbg, NT)
        dqkr = dqk * dm
        dqs = _dot(dqkr, kk) + dqdec * eg
        yield
        x = _dot_hi(tm, dtm, TN)
        yield
        da = jnp.where(strict, -_dot_hi(x, tm, NT), 0.0)
        yield
        dkk = da * dm
        dkb = _dot(dkk, kk) + dkbg * eg
        dk = _dot(dkk, kb, TN)
        dk = dk + _dot(dqkr, qs, TN) + dktail * etail + dkb * beta
        g = da * a + dqk * qk
        colsum = jnp.max(_dot_hi(g, jnp.ones((C, LANES), F32), TN), axis=1, keepdims=True)
        yield
        rk = jnp.sum(dktail * ktail, axis=1, keepdims=True)
        dgc = (jnp.sum(g, axis=1, keepdims=True) - colsum
               + jnp.sum(dqdec * qdec, axis=1, keepdims=True) - rk
               + jnp.sum(dkbg * kbg, axis=1, keepdims=True))
        dgl = jnp.sum(rk, axis=0, keepdims=True) + dcd * egl
        ri = lax.broadcasted_iota(jnp.int32, (C, 1), 0)
        dgc = dgc + jnp.where(ri == C - 1, dgl, 0.0)
        dbeta = jnp.sum(dkb * kk, axis=1, keepdims=True) + jnp.sum(dvb * vv, axis=1, keepdims=True)

        dq_ref[:, sl] = dqs * QK_SCALE
        dk_ref[:, sl] = dk
        dv_ref[:, sl] = dvb * beta
        lane = lax.broadcasted_iota(jnp.int32, (C, LANES), 1)
        return ds0, dwn_part, jnp.where(lane == h, dbeta, 0.0) + jnp.where(lane == H + h, dgc, 0.0)

    blk = lambda off: pl.BlockSpec((C, HB * hd), lambda n, h: (N - 1 - n, off // HB + h))
    st = lambda r: pl.BlockSpec((None, HB, r, r), lambda n, h: (N - 1 - n, h, 0, 0))
    return _pcall(
        body, name=name, grid=(N, H // HB),
        in_specs=[blk(0), blk(0), blk(0), blk(zoff),
                  pl.BlockSpec((C, LANES), lambda n, h: (N - 1 - n, 0)),
                  pl.BlockSpec((None, LANES, C), lambda n, h: (N - 1 - n, 0, 0)),
                  _fixed((1, hd)), blk(0), st(hd), st(C), blk(0)],
        out_specs=[blk(0), blk(0), blk(0), blk(0),
                   pl.BlockSpec((C, LANES), lambda n, h: (N - 1 - n, 0)), _fixed((1, hd))],
        out_shape=[jax.ShapeDtypeStruct((T, H * hd), F32)] * 3
        + [jax.ShapeDtypeStruct((T, H * hd), BF), jax.ShapeDtypeStruct((T, LANES), F32),
           jax.ShapeDtypeStruct((1, hd), F32)],
        scratch_shapes=[pltpu.VMEM((H, hd, hd), F32)],
        compiler_params=_cp(2))(q, k, v, pm, bg, gcrow, wn, oraw, sall, tall, dog)


def _rot(x, cs, sn):
    return x * cs + pltpu.roll(x, HEAD_DIM // 2, 1) * sn


def _rot_t(dy, cs, sn):
    return dy * cs + pltpu.roll(dy * sn, HEAD_DIM // 2, 1)


def ret_fwd(pm, qoff, koff, voff, goff, cs, sn, dmat, avec, bvec, gam, H, *, name):
    T = pm.shape[0]
    C = LA_CHUNK
    N = T // C
    hd = HEAD_DIM

    HB = _heads_per_step(H)

    def body(q_ref, k_ref, v_ref, g_ref, cs_ref, sn_ref, dm_ref, a_ref, b_ref, gam_ref,
             og_ref, or_ref, sall_ref, S):
        n = pl.program_id(0)
        hg = pl.program_id(1)
        c, s = cs_ref[...], sn_ref[...]

        @pl.when((n == 0) & (hg == 0))
        def _():
            S[...] = jnp.zeros_like(S)

        states = [S[hg * HB + i] for i in range(HB)]

        def head(i):
            sl = slice(i * hd, (i + 1) * hd)
            qq = _rot(q_ref[:, sl], c, s)
            kk = _rot(k_ref[:, sl], c, s) * QK_SCALE
            vv = v_ref[:, sl]
            s0 = states[i]
            p = _dot(qq, kk, NT) * dm_ref[i]
            cross = _dot(qq * a_ref[i], s0)
            s1 = s0 * gam_ref[i] + _dot(kk * b_ref[i], vv, TN)
            yield
            o = _dot(p, vv) + cross
            yield
            sall_ref[i] = s0
            or_ref[:, sl] = o
            og_ref[:, sl] = (_silu(g_ref[:, sl]) * o * _head_rstd(o)).astype(og_ref.dtype)
            return s1

        for i, s1 in enumerate(_interleave([head(i) for i in range(HB)])):
            S[hg * HB + i] = s1

    blk = lambda off: pl.BlockSpec((C, HB * hd), lambda n, h: (n, off // HB + h))
    tab = pl.BlockSpec((C, hd), lambda n, h: (n, 0))
    per_h = lambda r, cdim: pl.BlockSpec((HB, r, cdim), lambda n, h: (h, 0, 0))
    return _pcall(
        body, name=name, grid=(N, H // HB),
        in_specs=[blk(qoff), blk(koff), blk(voff), blk(goff), tab, tab,
                  per_h(C, C), per_h(C, hd), per_h(C, hd), per_h(1, hd)],
        out_specs=[blk(0), blk(0), pl.BlockSpec((None, HB, hd, hd), lambda n, h: (n, h, 0, 0))],
        out_shape=[jax.ShapeDtypeStruct((T, H * hd), BF), jax.ShapeDtypeStruct((T, H * hd), F32),
                   jax.ShapeDtypeStruct((N, H, hd, hd), F32)],
        scratch_shapes=[pltpu.VMEM((H, hd, hd), F32)],
        compiler_params=_cp(2))(pm, pm, pm, pm, cs, sn, dmat, avec, bvec, gam)


def ret_bwd(pm, qoff, koff, voff, goff, cs, sn, dmat, avec, bvec, gam, oraw, sall, dog, dogoff, H, *, name):
    T = pm.shape[0]
    C = LA_CHUNK
    N = T // C
    hd = HEAD_DIM

    HB = _heads_per_step(H)

    def body(q_ref, k_ref, v_ref, g_ref, cs_ref, sn_ref, dm_ref, a_ref, b_ref, gam_ref, or_ref, sall_ref,
             dog_ref, dq_ref, dk_ref, dv_ref, dg_ref, dS):
        n = pl.program_id(0)
        hg = pl.program_id(1)
        c, s = cs_ref[...], sn_ref[...]

        @pl.when((n == 0) & (hg == 0))
        def _():
            dS[...] = jnp.zeros_like(dS)

        dstates = [dS[hg * HB + i] for i in range(HB)]

        def head(i):
            sl = slice(i * hd, (i + 1) * hd)
            qq = _rot(q_ref[:, sl], c, s)
            kk = _rot(k_ref[:, sl], c, s) * QK_SCALE
            vv = v_ref[:, sl]
            gg = g_ref[:, sl]
            o = or_ref[:, sl]
            dog = dog_ref[:, sl]
            dm = dm_ref[i]
            av, bv = a_ref[i], b_ref[i]
            s0 = sall_ref[i]
            ds1 = dstates[i]

            rstd = _head_rstd(o)
            on = o * rstd
            don = dog * _silu(gg)
            dg_ref[:, sl] = (dog * on * _dsilu(gg)).astype(dg_ref.dtype)
            do = rstd * (don - on * jnp.mean(don * on, axis=-1, keepdims=True))

            p = _dot(qq, kk, NT) * dm
            dp = _dot(do, vv, NT) * dm
            cross_q = _dot(do, s0, NT) * av
            cross_k = _dot(vv, ds1, NT) * bv
            cross_v = _dot(kk * bv, ds1)
            ds0 = ds1 * gam_ref[i] + _dot(qq * av, do, TN)
            yield
            dv_ref[:, sl] = (_dot(p, do, TN) + cross_v).astype(dv_ref.dtype)
            dqq = _dot(dp, kk) + cross_q
            dkk = (_dot(dp, qq, TN) + cross_k) * QK_SCALE
            yield
            dq_ref[:, sl] = _rot_t(dqq, c, s).astype(dq_ref.dtype)
            dk_ref[:, sl] = _rot_t(dkk, c, s).astype(dk_ref.dtype)
            return ds0

        for i, ds0 in enumerate(_interleave([head(i) for i in range(HB)])):
            dS[hg * HB + i] = ds0

    blk = lambda off: pl.BlockSpec((C, HB * hd), lambda n, h: (N - 1 - n, off // HB + h))
    tab = pl.BlockSpec((C, hd), lambda n, h: (N - 1 - n, 0))
    per_h = lambda r, cdim: pl.BlockSpec((HB, r, cdim), lambda n, h: (h, 0, 0))
    return _pcall(
        body, name=name, grid=(N, H // HB),
        in_specs=[blk(qoff), blk(koff), blk(voff), blk(goff), tab, tab,
                  per_h(C, C), per_h(C, hd), per_h(C, hd), per_h(1, hd), blk(0),
                  pl.BlockSpec((None, HB, hd, hd), lambda n, h: (N - 1 - n, h, 0, 0)), blk(dogoff)],
        out_specs=[blk(0)] * 4,
        out_shape=[jax.ShapeDtypeStruct((T, H * hd), BF)] * 4,
        scratch_shapes=[pltpu.VMEM((H, hd, hd), F32)],
        compiler_params=_cp(2))(pm, pm, pm, pm, cs, sn, dmat, avec, bvec, gam, oraw, sall, dog)


LN_ROWS = 128


def ln_fwd(pre, lw, lb, *, name):
    T, W2 = pre.shape
    W = W2 // 2
    tr = _tile(T, LN_ROWS, SUBLANES)

    def body(p_ref, w_ref, b_ref, o_ref):
        v = _gelu(p_ref[...])
        xc = v - jnp.mean(v, axis=-1, keepdims=True)
        r = lax.rsqrt(jnp.mean(xc * xc, axis=-1, keepdims=True) + EPS)
        o_ref[...] = xc * r * w_ref[...] + b_ref[...]

    return _pcall(body, name=name, grid=(T // tr,),
                  in_specs=[pl.BlockSpec((tr, W), lambda i: (i, 1)), _fixed((1, W)), _fixed((1, W))],
                  out_specs=_rows(tr, W), out_shape=jax.ShapeDtypeStruct((T, W), F32),
                  compiler_params=_cp(1))(pre, lw, lb)


def ln_bwd(pre, lw, dvn, *, name):
    T, W2 = pre.shape
    W = W2 // 2
    tr = _tile(T, LN_ROWS, SUBLANES)

    def body(p_ref, w_ref, d_ref, dp_ref, dw_ref, db_ref):
        i = pl.program_id(0)
        pv = p_ref[...]
        v = _gelu(pv)
        xc = v - jnp.mean(v, axis=-1, keepdims=True)
        r = lax.rsqrt(jnp.mean(xc * xc, axis=-1, keepdims=True) + EPS)
        xh = xc * r
        d = d_ref[...]
        dxh = d * w_ref[...]
        dv = r * (dxh - jnp.mean(dxh, axis=-1, keepdims=True) - xh * jnp.mean(dxh * xh, axis=-1, keepdims=True))
        dp_ref[...] = (dv * _dgelu(pv)).astype(dp_ref.dtype)
        pw = jnp.sum(d * xh, axis=0, keepdims=True)
        pb = jnp.sum(d, axis=0, keepdims=True)

        @pl.when(i == 0)
        def _():
            dw_ref[...] = pw
            db_ref[...] = pb

        @pl.when(i > 0)
        def _():
            dw_ref[...] += pw
            db_ref[...] += pb

    return _pcall(body, name=name, grid=(T // tr,),
                  in_specs=[pl.BlockSpec((tr, W), lambda i: (i, 1)), _fixed((1, W)), _rows(tr, W)],
                  out_specs=[_rows(tr, W), _fixed((1, W)), _fixed((1, W))],
                  out_shape=[jax.ShapeDtypeStruct((T, W), BF), jax.ShapeDtypeStruct((1, W), F32),
                             jax.ShapeDtypeStruct((1, W), F32)],
                  compiler_params=_cp(1))(pre, lw, dvn)


def _tril_mask(c):
    t = lax.broadcasted_iota(jnp.int32, (c, c), 0)
    s = lax.broadcasted_iota(jnp.int32, (c, c), 1)
    return t >= s


def sg_fwd(pre, vn, ws, bs3, *, name):
    T, W = vn.shape
    G = ws.shape[0]
    gd = W // G
    C = SG_CHUNK

    def body(p_ref, v_ref, w_ref, b_ref, o_ref):
        mask = _tril_mask(C)
        for g in range(G):
            sl = slice(g * gd, (g + 1) * gd)
            wm = jnp.where(mask, w_ref[g], 0.0)
            s = _dot(wm, v_ref[:, sl]) + b_ref[g]
            o_ref[:, sl] = (_gelu(p_ref[:, sl]) * s).astype(o_ref.dtype)

    blk = pl.BlockSpec((C, W), lambda n: (n, 0))
    return _pcall(body, name=name, grid=(T // C,),
                  in_specs=[blk, blk, _fixed((G, C, C)), _fixed((G, C, 1))],
                  out_specs=blk, out_shape=jax.ShapeDtypeStruct((T, W), BF),
                  compiler_params=_cp(1))(pre, vn, ws, bs3)


def sg_bwd(pre, vn, ws, bs3, dus, *, name):
    T, W = vn.shape
    G = ws.shape[0]
    gd = W // G
    C = SG_CHUNK

    def body(p_ref, v_ref, w_ref, b_ref, d_ref, dp_ref, dv_ref, dw_ref, db_ref):
        n = pl.program_id(0)
        mask = _tril_mask(C)

        @pl.when(n == 0)
        def _():
            dw_ref[...] = jnp.zeros_like(dw_ref)
            db_ref[...] = jnp.zeros_like(db_ref)

        for g in range(G):
            sl = slice(g * gd, (g + 1) * gd)
            wm = jnp.where(mask, w_ref[g], 0.0)
            pv = p_ref[:, sl]
            vv = v_ref[:, sl]
            d = d_ref[:, sl]
            s = _dot(wm, vv) + b_ref[g]
            ds = d * _gelu(pv)
            dp_ref[:, sl] = (d * s * _dgelu(pv)).astype(dp_ref.dtype)
            dv_ref[:, sl] = _dot(wm, ds, TN)
            dw_ref[g] += jnp.where(mask, _dot(ds, vv, NT), 0.0)
            db_ref[g] += jnp.sum(ds, axis=1, keepdims=True)

    blk = pl.BlockSpec((C, W), lambda n: (n, 0))
    return _pcall(body, name=name, grid=(T // C,),
                  in_specs=[blk, blk, _fixed((G, C, C)), _fixed((G, C, 1)), blk],
                  out_specs=[blk, blk, _fixed((G, C, C)), _fixed((G, C, 1))],
                  out_shape=[jax.ShapeDtypeStruct((T, W), BF), jax.ShapeDtypeStruct((T, W), F32),
                             jax.ShapeDtypeStruct((G, C, C), F32), jax.ShapeDtypeStruct((G, C, 1), F32)],
                  compiler_params=_cp(1))(pre, vn, ws, bs3, dus)


CHIP_RELATIONS = ((1, 0), (0, 1), (1, 1))


def _place():
    return lax.axis_index("x"), lax.axis_index("y"), lax.axis_index("c")


def _peer_chip(x, y, r):
    fx, fy = CHIP_RELATIONS[r]
    return (1 - x if fx else x), (1 - y if fy else y)


def all_gather_halves(arrs, *, name):
    n = len(arrs)
    per = 2 * len(CHIP_RELATIONS)

    def body(*refs):
        ins, outs = refs[:n], refs[n:2 * n]
        send, recv = refs[2 * n:2 * n + 2]
        x, y, c = _place()
        j = 2 * x + y
        sib = (x, y, 1 - c)
        sends = []
        for a in range(n):
            for r in range(3):
                px, py = _peer_chip(x, y, r)
                cp = pltpu.make_async_remote_copy(
                    src_ref=ins[a].at[c], dst_ref=outs[a].at[j, c], send_sem=send.at[a * per + r],
                    recv_sem=recv.at[a * per + r], device_id=(px, py, c), device_id_type=MESH)
                cp.start()
                sends.append(cp)
        for a in range(n):
            for r in range(3):
                px, py = _peer_chip(x, y, r)
                kk = 2 * px + py
                landed = outs[a].at[kk, c]
                pltpu.make_async_remote_copy(
                    src_ref=landed, dst_ref=landed, send_sem=send.at[a * per + r],
                    recv_sem=recv.at[a * per + r], device_id=(px, py, c), device_id_type=MESH).wait_recv()
                fw = pltpu.make_async_remote_copy(
                    src_ref=landed, dst_ref=landed, send_sem=send.at[a * per + 3 + r],
                    recv_sem=recv.at[a * per + 3 + r], device_id=sib, device_id_type=MESH)
                fw.start()
                sends.append(fw)
        for a in range(n):
            for r in range(3):
                px, py = _peer_chip(x, y, r)
                kk = 2 * px + py
                other = outs[a].at[kk, 1 - c]
                pltpu.make_async_remote_copy(
                    src_ref=other, dst_ref=other, send_sem=send.at[a * per + 3 + r],
                    recv_sem=recv.at[a * per + 3 + r], device_id=sib, device_id_type=MESH).wait_recv()
        for cp in sends:
            cp.wait_send()

    out_shape = [jax.ShapeDtypeStruct((N_CHIPS,) + a.shape, a.dtype) for a in arrs]
    res = _pcall(body, name=name, in_specs=[ANY] * n, out_specs=[ANY] * n, out_shape=out_shape,
                 scratch_shapes=[pltpu.SemaphoreType.DMA((n * per,)), pltpu.SemaphoreType.DMA((n * per,))])(*arrs)
    return list(res)


def pair_exchange(gs, *, name):
    n = len(gs)

    def body(*refs):
        ins, outs = refs[:n], refs[n:2 * n]
        send, recv = refs[2 * n:2 * n + 2]
        x, y, c = _place()
        cps = []
        for a in range(n):
            cp = pltpu.make_async_remote_copy(
                src_ref=ins[a].at[:, pl.ds(1 - c, 1)], dst_ref=outs[a], send_sem=send.at[a], recv_sem=recv.at[a],
                device_id=(x, y, 1 - c), device_id_type=MESH)
            cp.start()
            cps.append(cp)
        for cp in cps:
            cp.wait()

    out_shape = [jax.ShapeDtypeStruct((g.shape[0], 1) + g.shape[2:], g.dtype) for g in gs]
    res = _pcall(body, name=name, in_specs=[ANY] * n, out_specs=[ANY] * n, out_shape=out_shape,
                 scratch_shapes=[pltpu.SemaphoreType.DMA((n,)), pltpu.SemaphoreType.DMA((n,))])(*gs)
    return list(res)


def chip_exchange(ps, *, name):
    n = len(ps)

    def body(*refs):
        ins, outs = refs[:n], refs[n:2 * n]
        send, recv = refs[2 * n:2 * n + 2]
        x, y, c = _place()
        cps = []
        for a in range(n):
            for r in range(3):
                px, py = _peer_chip(x, y, r)
                cp = pltpu.make_async_remote_copy(
                    src_ref=ins[a].at[2 * px + py], dst_ref=outs[a].at[r], send_sem=send.at[3 * a + r],
                    recv_sem=recv.at[3 * a + r], device_id=(px, py, c), device_id_type=MESH)
                cp.start()
                cps.append(cp)
        for cp in cps:
            cp.wait()

    out_shape = [jax.ShapeDtypeStruct((3,) + p.shape[1:], p.dtype) for p in ps]
    res = _pcall(body, name=name, in_specs=[ANY] * n, out_specs=[ANY] * n, out_shape=out_shape,
                 scratch_shapes=[pltpu.SemaphoreType.DMA((3 * n,)), pltpu.SemaphoreType.DMA((3 * n,))])(*ps)
    return list(res)


def pair_share(fs, *, name):
    n = len(fs)

    def body(*refs):
        ins, outs = refs[:n], refs[n:2 * n]
        send, recv = refs[2 * n:2 * n + 2]
        x, y, c = _place()
        cps = []
        for a in range(n):
            cp = pltpu.make_async_remote_copy(
                src_ref=ins[a], dst_ref=outs[a], send_sem=send.at[a], recv_sem=recv.at[a],
                device_id=(x, y, 1 - c), device_id_type=MESH)
            cp.start()
            cps.append(cp)
        for cp in cps:
            cp.wait()

    out_shape = [jax.ShapeDtypeStruct(f.shape, f.dtype) for f in fs]
    res = _pcall(body, name=name, in_specs=[ANY] * n, out_specs=[ANY] * n, out_shape=out_shape,
                 scratch_shapes=[pltpu.SemaphoreType.DMA((n,)), pltpu.SemaphoreType.DMA((n,))])(*fs)
    return list(res)


def all_reduce_small(v, *, name):
    R = v.shape[0]

    def body(v_ref, sum_ref, gat_ref, send, recv):
        x, y, c = _place()
        me = 4 * x + 2 * y + c
        gat_ref[me] = v_ref[...]
        cps = []
        peers = []
        for r in range(1, N_DEV):
            fx, fy, fc = (r >> 2) & 1, (r >> 1) & 1, r & 1
            px, py, pc = (1 - x if fx else x), (1 - y if fy else y), (1 - c if fc else c)
            peers.append((px, py, pc))
            cp = pltpu.make_async_remote_copy(
                src_ref=v_ref, dst_ref=gat_ref.at[me], send_sem=send.at[r - 1], recv_sem=recv.at[r - 1],
                device_id=(px, py, pc), device_id_type=MESH)
            cp.start()
            cps.append(cp)
        for r in range(1, N_DEV):
            px, py, pc = peers[r - 1]
            slot = gat_ref.at[4 * px + 2 * py + pc]
            pltpu.make_async_remote_copy(
                src_ref=v_ref, dst_ref=slot, send_sem=send.at[r - 1], recv_sem=recv.at[r - 1],
                device_id=(px, py, pc), device_id_type=MESH).wait_recv()
        for cp in cps:
            cp.wait_send()
        acc = gat_ref[0]
        for s in range(1, N_DEV):
            acc = acc + gat_ref[s]
        sum_ref[...] = acc

    vm = pl.BlockSpec(memory_space=pltpu.VMEM)
    res = _pcall(body, name=name, in_specs=[vm], out_specs=[vm, vm],
                 out_shape=[jax.ShapeDtypeStruct((R, LANES), F32), jax.ShapeDtypeStruct((N_DEV, R, LANES), F32)],
                 scratch_shapes=[pltpu.SemaphoreType.DMA((N_DEV - 1,)), pltpu.SemaphoreType.DMA((N_DEV - 1,))],
                 compiler_params=pltpu.CompilerParams(vmem_limit_bytes=VMEM_LIMIT))(v)
    return res[0]


def pair_sum(g, r1, c_idx, *, name):
    nb, _, hr, C = g.shape
    tr = _tile(hr, max(BF16_ROWS, (1 << 18) // C), BF16_ROWS)

    def body(c_ref, g_ref, r_ref, o_ref, ob_ref):
        s = g_ref[...] + r_ref[...]
        o_ref[...] = s
        ob_ref[...] = s.astype(ob_ref.dtype)

    out = pl.BlockSpec((None, tr, C), lambda b, i, cr: (b, i, 0))
    gs = pltpu.PrefetchScalarGridSpec(
        num_scalar_prefetch=1, grid=(nb, hr // tr),
        in_specs=[pl.BlockSpec((None, None, tr, C), lambda b, i, cr: (b, cr[0], i, 0)),
                  pl.BlockSpec((None, None, tr, C), lambda b, i, cr: (b, 0, i, 0))],
        out_specs=[out, out])
    return _pcall(body, name=name, grid_spec=gs,
                  out_shape=[jax.ShapeDtypeStruct((nb, hr, C), F32), jax.ShapeDtypeStruct((nb, hr, C), BF)],
                  compiler_params=_cp(2))(c_idx, g, r1)


def chip_sum(p, r2, j_idx, *, name):
    _, hr, C = p.shape
    tr = _tile(hr, max(BF16_ROWS, (1 << 18) // C), BF16_ROWS)

    def body(j_ref, p_ref, a_ref, b_ref, c_ref, o_ref):
        o_ref[...] = ((p_ref[...] + a_ref[...].astype(F32)) + b_ref[...].astype(F32)) + c_ref[...].astype(F32)

    rel = lambda r: pl.BlockSpec((None, tr, C), lambda i, jr: (r, i, 0))
    gs = pltpu.PrefetchScalarGridSpec(
        num_scalar_prefetch=1, grid=(hr // tr,),
        in_specs=[pl.BlockSpec((None, tr, C), lambda i, jr: (jr[0], i, 0)), rel(0), rel(1), rel(2)],
        out_specs=pl.BlockSpec((tr, C), lambda i, jr: (i, 0)))
    return _pcall(body, name=name, grid_spec=gs, out_shape=jax.ShapeDtypeStruct((hr, C), F32),
                  compiler_params=_cp(1))(j_idx, p, r2, r2, r2)


def adamw_halves(w, g_mine, g_other, m, v, c_idx, *, name):
    R, C = w.shape
    hr = R // 2
    tr = _tile(hr, max(SUBLANES, (1 << 18) // C), SUBLANES)
    nbh = hr // tr
    c1 = 1.0 - ADAM_B1 ** ADAM_STEP
    c2 = 1.0 - ADAM_B2 ** ADAM_STEP

    def body(c_ref, w_ref, gm_ref, go_ref, m_ref, v_ref, g_ref, d_ref, mo_ref, vo_ref):
        i = pl.program_id(0)
        gv = jnp.where(i // nbh == c_ref[0], gm_ref[...], go_ref[...])
        m2 = ADAM_B1 * m_ref[...] + (1.0 - ADAM_B1) * gv
        v2 = ADAM_B2 * v_ref[...] + (1.0 - ADAM_B2) * (gv * gv)
        d_ref[...] = -ADAM_LR * ((m2 / c1) / (jnp.sqrt(v2 / c2) + ADAM_EPS) + ADAM_WD * w_ref[...])
        g_ref[...] = gv
        mo_ref[...] = m2
        vo_ref[...] = v2

    full = pl.BlockSpec((tr, C), lambda i, cr: (i, 0))
    half = pl.BlockSpec((tr, C), lambda i, cr: (i % nbh, 0))
    gs = pltpu.PrefetchScalarGridSpec(
        num_scalar_prefetch=1, grid=(R // tr,),
        in_specs=[full, half, half, full, full], out_specs=[full] * 4)
    sds = jax.ShapeDtypeStruct((R, C), F32)
    return _pcall(body, name=name, grid_spec=gs, out_shape=[sds] * 4,
                  compiler_params=_cp(1))(c_idx, w, g_mine, g_other, m, v)


def _pack_rows(arrs):
    parts = []
    for a in arrs:
        flat = a.reshape(-1).astype(F32)
        tile = SUBLANES * LANES
        pad = (-flat.shape[0]) % tile
        parts.append(jnp.pad(flat, (0, pad)).reshape(-1, LANES))
    return jnp.concatenate(parts, axis=0)


def _unpack_rows(buf, shapes):
    out, row = [], 0
    for shp in shapes:
        size = int(np.prod(shp))
        rows = -(-size // (SUBLANES * LANES)) * SUBLANES
        out.append(buf[row:row + rows].reshape(-1)[:size].reshape(shp))
        row += rows
    return out


def _halves(a2d):
    r, c = a2d.shape
    return a2d.reshape(2, r // 2, c)


def _rotary_tables(T):
    half = HEAD_DIM // 2
    pos = jnp.arange(T, dtype=F32)
    inv_freq = 1.0 / (ROPE_BASE ** jnp.linspace(0.0, 1.0, half, dtype=F32))
    ang = pos[:, None] * inv_freq[None, :]
    cos, sin = jnp.cos(ang), jnp.sin(ang)
    return jnp.concatenate([cos, cos], axis=1), jnp.concatenate([-sin, sin], axis=1)


def _retention_tables(H):
    C = LA_CHUNK
    lg = jnp.log1p(-jnp.power(2.0, -5.0 - jnp.arange(H, dtype=F32)))
    pos = jnp.arange(C, dtype=F32)
    causal = jnp.tril(jnp.ones((C, C), dtype=bool))
    dmat = jnp.exp(jnp.where(causal, (pos[:, None] - pos[None, :]) * lg[:, None, None], -jnp.inf))
    bc = lambda t: jnp.broadcast_to(t[..., None], t.shape + (HEAD_DIM,))
    avec = bc(jnp.exp((pos + 1.0)[None, :] * lg[:, None]))
    bvec = bc(jnp.exp((C - 1.0 - pos)[None, :] * lg[:, None]))
    gam = bc(jnp.exp(C * lg)[:, None])
    return dmat, avec, bvec, gam


def _relu2(acc):
    return acc, jnp.square(jnp.maximum(acc, 0.0))


def _drelu2(acc, up):
    return (acc * (2.0 * jnp.maximum(up, 0.0)),)


def _add(acc, e):
    return (acc + e,)


def _ffn_fwd(y, w_up3, w_dn, tag):
    up, act = mm_nn(y, w_up3, name=f"ffn_up_{tag}", out_dtypes=(F32, BF), epilogue=_relu2)
    dn = mm_nn(act, w_dn, name=f"ffn_down_{tag}")
    return up, act, dn


def _ffn_bwd(y, up, act, ddn, w_up3, w_dn, tag):
    g_dn = mm_tn(act, ddn, name=f"ffn_dwdown_{tag}")
    dup = mm_nt(ddn, w_dn, name=f"ffn_dup_{tag}", out_dtypes=(BF,), epilogue=_drelu2, extras=(up,))
    g_up = mm_tn(y, dup, name=f"ffn_dwup_{tag}", shards=N_CHIPS)
    dy = mm_nt(dup, w_up3, name=f"ffn_dy_{tag}")
    return dy, g_up, g_dn


def _train_local(x2, tgt, W):
    T, D = x2.shape
    H = W["a_log"].shape[0]
    nw = W["norm_w"]
    row = lambda v: v.reshape(1, -1).astype(F32)

    y0 = rms_fwd(x2, row(nw[0, 0]), name="norm00")
    pm = mm_nn(y0, W["la_in_main"], name="la_in_main")
    pg = mm_nn(y0, W["la_in_gate"], name="la_in_gate")
    wc8 = jnp.pad(jnp.transpose(W["conv_w"]), ((0, SUBLANES - CONV_WIDTH), (0, 0)))
    lanes_pad = (H, LANES - 2 * H)
    arow = jnp.pad(W["a_log"], lanes_pad).reshape(1, LANES)
    dtrow = jnp.pad(W["dt_bias"], lanes_pad).reshape(1, LANES)
    bg, gcrow = gates_fwd(pg, arow, dtrow, H, name="gates_fwd")
    q = prep_fwd(pm, 0, wc8, 0, H, True, name="prep_q")
    k = prep_fwd(pm, H, wc8, H, H, True, name="prep_k")
    v = prep_fwd(pm, 2 * H, wc8, 2 * H, H, False, name="prep_v")
    wn = row(W["out_norm_w"])
    og_a, or_a, sall_a, tall = gdn_fwd(q, k, v, pm, 3 * H, bg, gcrow, wn, H, name="gdn_fwd")
    cs, sn = _rotary_tables(T)
    dmat, avec, bvec, gam = _retention_tables(H)
    og_b, or_b, sall_b = ret_fwd(pm, 4 * H, 5 * H, 6 * H, 7 * H, cs, sn, dmat, avec, bvec, gam, H, name="ret_fwd")
    ocat = jnp.concatenate([og_a, og_b], axis=1)
    mix = mm_nn(ocat, W["la_out"], name="la_out")
    h1, y2 = res_norm(x2, mix, row(nw[0, 1]), row(nw[0, 2]), name="resnorm_0a")
    up, act, dn = _ffn_fwd(y2, W["ffn_up"][0], W["ffn_down"][0], "0")
    h2, y0b = res_norm(h1, dn, row(nw[0, 3]), row(nw[1, 0]), name="resnorm_0b")

    pre = mm_nn(y0b, W["sg_in"], name="sg_in")
    lw, lb = row(W["ln_w"]), row(W["ln_b"])
    vn = ln_fwd(pre, lw, lb, name="sg_ln")
    ws = W["w_s"]
    bs3 = W["b_s"][:, :, None]
    us = sg_fwd(pre, vn, ws, bs3, name="sg_gate")
    mix1 = mm_nn(us, W["sg_out"], name="sg_out")
    h3, y2b = res_norm(h2, mix1, row(nw[1, 1]), row(nw[1, 2]), name="resnorm_1a")
    up1, act1, dn1 = _ffn_fwd(y2b, W["ffn_up"][1], W["ffn_down"][1], "1")
    h4, _ = res_norm(h3, dn1, row(nw[1, 3]), None, name="resnorm_1b")
    dh4, lrow = loss_head(h4, tgt, name="loss_head")
    loss = lrow[0, 0]

    G = {}
    dnw = [[None] * 4 for _ in range(2)]
    ddn1, dnw[1][3] = rms_bwd(dn1, row(nw[1, 3]), dh4, None, name="dnorm13", out_dtype=BF)
    dy2b, G["ffn_up1"], G["ffn_down1"] = _ffn_bwd(y2b, up1, act1, ddn1, W["ffn_up"][1], W["ffn_down"][1], "1")
    dh3, dnw[1][2] = rms_bwd(h3, row(nw[1, 2]), dy2b, dh4, name="dnorm12", out_dtype=F32)
    dmix1, dnw[1][1] = rms_bwd(mix1, row(nw[1, 1]), dh3, None, name="dnorm11", out_dtype=BF)
    G["sg_out"] = mm_tn(us, dmix1, name="sg_dwout")
    dus = mm_nt(dmix1, W["sg_out"], name="sg_dus")
    dpre_u, dvn, G["w_s"], dbs3 = sg_bwd(pre, vn, ws, bs3, dus, name="sg_gate_bwd")
    G["b_s"] = dbs3[:, :, 0]
    dpre_v, dlw, dlb = ln_bwd(pre, lw, dvn, name="sg_ln_bwd")
    G["ln_w"], G["ln_b"] = dlw[0], dlb[0]
    dpre = jnp.concatenate([dpre_u, dpre_v], axis=1)
    G["sg_in"] = mm_tn(y0b, dpre, name="sg_dwin", shards=N_CHIPS)
    dy0b = mm_nt(dpre, W["sg_in"], name="sg_dy")
    dh2, dnw[1][0] = rms_bwd(h2, row(nw[1, 0]), dy0b, dh3, name="dnorm10", out_dtype=F32)

    ddn, dnw[0][3] = rms_bwd(dn, row(nw[0, 3]), dh2, None, name="dnorm03", out_dtype=BF)
    dy2, G["ffn_up0"], G["ffn_down0"] = _ffn_bwd(y2, up, act, ddn, W["ffn_up"][0], W["ffn_down"][0], "0")
    dh1, dnw[0][2] = rms_bwd(h1, row(nw[0, 2]), dy2, dh2, name="dnorm02", out_dtype=F32)
    dmix, dnw[0][1] = rms_bwd(mix, row(nw[0, 1]), dh1, None, name="dnorm01", out_dtype=BF)
    G["la_out"] = mm_tn(ocat, dmix, name="la_dwout")
    docat = mm_nt(dmix, W["la_out"], name="la_docat")
    dq, dk, dv, dz, dbg, dwn = gdn_bwd(q, k, v, pm, 3 * H, bg, gcrow, wn, or_a, sall_a, tall, docat, H,
                                       name="gdn_bwd")
    drq, drk, drv, drg = ret_bwd(pm, 4 * H, 5 * H, 6 * H, 7 * H, cs, sn, dmat, avec, bvec, gam, or_b, sall_b,
                                 docat, H, H, name="ret_bwd")
    dpg, da, ddt = gates_bwd(pg, arow, dtrow, dbg, H, name="gates_bwd")
    dcq, dwq = prep_bwd_act(pm, 0, wc8, 0, H, True, dq, name="prep_dq")
    dck, dwk = prep_bwd_act(pm, H, wc8, H, H, True, dk, name="prep_dk")
    dcv, dwv = prep_bwd_act(pm, 2 * H, wc8, 2 * H, H, False, dv, name="prep_dv")
    dxq = prep_bwd_conv(dcq, wc8, 0, H, name="conv_dq")
    dxk = prep_bwd_conv(dck, wc8, H, H, name="conv_dk")
    dxv = prep_bwd_conv(dcv, wc8, 2 * H, H, name="conv_dv")
    dpm = jnp.concatenate([dxq, dxk, dxv, dz, drq, drk, drv, drg], axis=1)
    G["la_in_main"] = mm_tn(y0, dpm, name="la_dwin_main")
    G["la_in_gate"] = mm_tn(y0, dpg, name="la_dwin_gate")
    dy0 = mm_nt(dpm, W["la_in_main"], name="la_dy_main")
    dy0 = mm_nt(dpg, W["la_in_gate"], name="la_dy_gate", epilogue=_add, extras=(dy0,))
    dx, dnw[0][0] = rms_bwd(x2, row(nw[0, 0]), dy0, dh1, name="dnorm00", out_dtype=F32)

    G["norm_w"] = jnp.stack([jnp.concatenate(r, axis=0) for r in dnw], axis=0)
    G["conv_w"] = jnp.transpose(jnp.concatenate([dwq, dwk, dwv], axis=1)[:CONV_WIDTH])
    G["a_log"] = da[0, H:2 * H]
    G["dt_bias"] = ddt[0, H:2 * H]
    G["out_norm_w"] = dwn[0]
    return loss, dx, G


def _as2d(a):
    n = int(np.prod(a.shape))
    if a.shape[-1] < LANES and n % LANES == 0:
        return a.reshape(-1, LANES)
    return a.reshape(-1, a.shape[-1])


def _adamw_any(w, g, m, v, name):
    shp = w.shape
    d, m2, v2 = adamw(_as2d(w), _as2d(g.reshape(shp)), _as2d(m), _as2d(v), name=name)
    return g.reshape(shp), d.reshape(shp), m2.reshape(shp), v2.reshape(shp)


def kernel(x, norm_w, la_w_in, la_conv_w, la_a_log, la_dt_bias, la_out_norm_w, la_w_out, sg_w_in, sg_ln_w, sg_ln_b, sg_w_s, sg_b_s, sg_w_out, ffn_w_up, ffn_w_down, loss_target, m_norm_w, m_la_w_in, m_la_conv_w, m_la_a_log, m_la_dt_bias, m_la_out_norm_w, m_la_w_out, m_sg_w_in, m_sg_ln_w, m_sg_ln_b, m_sg_w_s, m_sg_b_s, m_sg_w_out, m_ffn_w_up, m_ffn_w_down, v_norm_w, v_la_w_in, v_la_conv_w, v_la_a_log, v_la_dt_bias, v_la_out_norm_w, v_la_w_out, v_sg_w_in, v_sg_ln_w, v_sg_ln_b, v_sg_w_s, v_sg_b_s, v_sg_w_out, v_ffn_w_up, v_ffn_w_down):
    weights = dict(norm_w=norm_w, la_w_in=la_w_in, la_conv_w=la_conv_w, la_a_log=la_a_log, la_dt_bias=la_dt_bias,
                   la_out_norm_w=la_out_norm_w, la_w_out=la_w_out, sg_w_in=sg_w_in, sg_ln_w=sg_ln_w,
                   sg_ln_b=sg_ln_b, sg_w_s=sg_w_s, sg_b_s=sg_b_s, sg_w_out=sg_w_out, ffn_w_up=ffn_w_up,
                   ffn_w_down=ffn_w_down)
    mom_m = dict(norm_w=m_norm_w, la_w_in=m_la_w_in, la_conv_w=m_la_conv_w, la_a_log=m_la_a_log,
                 la_dt_bias=m_la_dt_bias, la_out_norm_w=m_la_out_norm_w, la_w_out=m_la_w_out, sg_w_in=m_sg_w_in,
                 sg_ln_w=m_sg_ln_w, sg_ln_b=m_sg_ln_b, sg_w_s=m_sg_w_s, sg_b_s=m_sg_b_s, sg_w_out=m_sg_w_out,
                 ffn_w_up=m_ffn_w_up, ffn_w_down=m_ffn_w_down)
    mom_v = dict(norm_w=v_norm_w, la_w_in=v_la_w_in, la_conv_w=v_la_conv_w, la_a_log=v_la_a_log,
                 la_dt_bias=v_la_dt_bias, la_out_norm_w=v_la_out_norm_w, la_w_out=v_la_w_out, sg_w_in=v_sg_w_in,
                 sg_ln_w=v_sg_ln_w, sg_ln_b=v_sg_ln_b, sg_w_s=v_sg_w_s, sg_b_s=v_sg_b_s, sg_w_out=v_sg_w_out,
                 ffn_w_up=v_ffn_w_up, ffn_w_down=v_ffn_w_down)
    order = list(weights)

    T, D = x.shape[1], x.shape[2]
    H = la_a_log.shape[1]
    HD = H * HEAD_DIM
    xi, yi, ci = _place()
    chip = 2 * xi + yi
    c_idx = jnp.reshape(ci, (1,)).astype(jnp.int32)
    j_idx = jnp.reshape(chip, (1,)).astype(jnp.int32)

    big = [la_w_in[0], la_w_out[0], sg_w_in[0], sg_w_out[0], ffn_w_up[0], ffn_w_up[1], ffn_w_down[0], ffn_w_down[1]]
    small_shapes = [norm_w.shape, la_conv_w[0].shape, sg_ln_w[0].shape, sg_ln_b[0].shape]
    small = _pack_rows([norm_w, la_conv_w[0], sg_ln_w[0], sg_ln_b[0]])
    small = jnp.pad(small, ((0, (-small.shape[0]) % (2 * SUBLANES)), (0, 0)))
    mine = [_halves(w.astype(BF)) for w in big] + [_halves(small)]
    gathered = all_gather_halves(mine, name="gather_weights")
    gathered = [lax.dynamic_update_slice(g, m[None], (chip, 0, 0, 0)) for g, m in zip(gathered, mine)]
    whole =[g.reshape(N_CHIPS, g.shape[1] * g.shape[2], g.shape[3]) for g in gathered]
    la_in_g, la_out_g, sg_in_g, sg_out_g, up0_g, up1_g, dn0_g, dn1_g, small_g = whole
    pieces = [_unpack_rows(small_g[kk], small_shapes) for kk in range(N_CHIPS)]
    la_nat = jnp.transpose(la_in_g, (1, 0, 2)).reshape(D, -1)
    rows_of = lambda g: g.reshape(-1, g.shape[-1])
    W = dict(
        norm_w=jnp.concatenate([p[0] for p in pieces], axis=-1),
        conv_w=jnp.concatenate([p[1] for p in pieces], axis=0),
        ln_w=jnp.concatenate([p[2] for p in pieces], axis=0),
        ln_b=jnp.concatenate([p[3] for p in pieces], axis=0),
        a_log=la_a_log[0], dt_bias=la_dt_bias[0], out_norm_w=la_out_norm_w[0], w_s=sg_w_s[0], b_s=sg_b_s[0],
        la_in_main=jnp.concatenate([la_nat[:, :4 * HD], la_nat[:, 4 * HD + 2 * H:]], axis=1),
        la_in_gate=jnp.pad(la_nat[:, 4 * HD:4 * HD + 2 * H], ((0, 0), (0, LANES - 2 * H))),
        la_out=rows_of(la_out_g), sg_in=sg_in_g, sg_out=rows_of(sg_out_g),
        ffn_up=[up0_g, up1_g], ffn_down=[rows_of(dn0_g), rows_of(dn1_g)],
    )

    loss_local, dx, G = _train_local(x[0], loss_target[0], W)
    loss = lax.psum(loss_local, ("x", "y", "c"))

    g_la_nat = jnp.concatenate([G["la_in_main"][:, :4 * HD], G["la_in_gate"][:, :2 * H], G["la_in_main"][:, 4 * HD:]],
                               axis=1)
    g_la_in = jnp.transpose(g_la_nat.reshape(D, N_CHIPS, -1), (1, 0, 2))
    by_chip = lambda g: g.reshape(N_CHIPS, -1, g.shape[-1])
    g_big = [g_la_in, by_chip(G["la_out"]), G["sg_in"], by_chip(G["sg_out"]), G["ffn_up0"], G["ffn_up1"],
             by_chip(G["ffn_down0"]), by_chip(G["ffn_down1"])]
    g_big = [g.reshape(N_CHIPS, 2, g.shape[1] // 2, g.shape[2]) for g in g_big]
    sib = pair_exchange(g_big, name="grads_pair_exchange")
    part = [pair_sum(g, r, c_idx, name=f"grads_pair_sum_{a}") for a, (g, r) in enumerate(zip(g_big, sib))]
    others = chip_exchange([pb for _, pb in part], name="grads_chip_exchange")
    done = [chip_sum(p, r, j_idx, name=f"grads_chip_sum_{a}") for a, ((p, _), r) in enumerate(zip(part, others))]
    from_sib = pair_share(done, name="grads_pair_share")

    def big_update(a, w, m, v, tag):
        shp = w.shape
        r4 = adamw_halves(w.reshape(-1, shp[-1]), done[a], from_sib[a], m.reshape(-1, shp[-1]),
                          v.reshape(-1, shp[-1]), c_idx, name=f"adamw_{tag}")
        return [t.reshape(shp) for t in r4]

    big_res = dict(
        la_w_in=big_update(0, la_w_in, m_la_w_in, v_la_w_in, "la_w_in"),
        la_w_out=big_update(1, la_w_out, m_la_w_out, v_la_w_out, "la_w_out"),
        sg_w_in=big_update(2, sg_w_in, m_sg_w_in, v_sg_w_in, "sg_w_in"),
        sg_w_out=big_update(3, sg_w_out, m_sg_w_out, v_sg_w_out, "sg_w_out"),
    )
    for nm, a0, w, m, v in (("ffn_w_up", 4, ffn_w_up, m_ffn_w_up, v_ffn_w_up),
                            ("ffn_w_down", 6, ffn_w_down, m_ffn_w_down, v_ffn_w_down)):
        per_layer = [big_update(a0 + l, w[l], m[l], v[l], f"{nm}_{l}") for l in range(2)]
        big_res[nm] = [jnp.stack([per_layer[0][t], per_layer[1][t]]) for t in range(4)]

    small_names = ["norm_w", "conv_w", "ln_w", "ln_b", "a_log", "dt_bias", "out_norm_w", "w_s", "b_s"]
    small_full = [G[nm] for nm in small_names]
    summed = _unpack_rows(all_reduce_small(_pack_rows(small_full), name="grads_all_reduce_small"),
                          [g.shape for g in small_full])
    sm = dict(zip(small_names, summed))
    own = lambda full, axis: lax.dynamic_slice_in_dim(full, chip * (full.shape[axis] // N_CHIPS),
                                                      full.shape[axis] // N_CHIPS, axis)
    grads = dict(
        norm_w=own(sm["norm_w"], 2), la_conv_w=own(sm["conv_w"], 0), la_a_log=sm["a_log"],
        la_dt_bias=sm["dt_bias"], la_out_norm_w=sm["out_norm_w"],
        sg_ln_w=own(sm["ln_w"], 0), sg_ln_b=own(sm["ln_b"], 0), sg_w_s=sm["w_s"], sg_b_s=sm["b_s"],
    )

    res = {nm: big_res[nm] if nm in big_res else
           _adamw_any(weights[nm], grads[nm], mom_m[nm], mom_v[nm], f"adamw_{nm}") for nm in order}
    return (loss, dx.reshape(x.shape), *[res[nm][0] for nm in order], *[res[nm][1] for nm in order],
            *[res[nm][2] for nm in order], *[res[nm][3] for nm in order])
```

```python
import math
from typing import Callable, NamedTuple

import numpy as np
import jax
import jax.numpy as jnp
from jax import lax
from jax.experimental import pallas as pl
from jax.experimental.pallas import tpu as pltpu

F32 = jnp.float32
BF = jnp.bfloat16
HI = lax.Precision.HIGHEST

V7X_VMEM_BYTES = 64 * 1024 * 1024
VMEM_LIMIT = (V7X_VMEM_BYTES * 3) // 4
LANES = 128
SUBLANES = 8
BF16_ROWS = 16
HEAD_DIM = 128
LA_CHUNK = 64
SG_CHUNK = 128
CONV_WIDTH = 4
ROPE_BASE = 10000.0
EPS = 1e-6
L2_EPS = 1e-6
N_CHIPS = 4
N_DEV = 8

ADAM_LR = 0.001
ADAM_B1 = 0.9
ADAM_B2 = 0.999
ADAM_EPS = 1e-08
ADAM_WD = 0.01
ADAM_STEP = 10

MESH = pl.DeviceIdType.MESH
ANY = pl.BlockSpec(memory_space=pl.ANY)

NN = (((1,), (0,)), ((), ()))
NT = (((1,), (1,)), ((), ()))
TN = (((0,), (0,)), ((), ()))


def _pcall(body, **kw):
    return pl.pallas_call(body, **kw)


def _cp(n_axes):
    return pltpu.CompilerParams(dimension_semantics=("arbitrary",) * n_axes, vmem_limit_bytes=VMEM_LIMIT)


def _tile(n, pref, unit=LANES):
    if n <= pref:
        return n
    t = (pref // unit) * unit
    while t >= unit:
        if n % t == 0:
            return t
        t -= unit
    return n


def _dot(a, b, dims=NN):
    return lax.dot_general(a.astype(BF), b.astype(BF), dims, preferred_element_type=F32)


def _dot_hi(a, b, dims=NN):
    return lax.dot_general(a.astype(F32), b.astype(F32), dims, precision=HI, preferred_element_type=F32)


def _sigmoid(x):
    return 1.0 / (1.0 + jnp.exp(-x))


def _silu(x):
    return x * _sigmoid(x)


def _dsilu(x):
    s = _sigmoid(x)
    return s * (1.0 + x * (1.0 - s))


GELU_C = math.sqrt(2.0 / math.pi)
GELU_A = 0.044715


def _gelu(x):
    return 0.5 * x * (1.0 + jnp.tanh(GELU_C * (x + GELU_A * x * x * x)))


def _dgelu(x):
    t = jnp.tanh(GELU_C * (x + GELU_A * x * x * x))
    return 0.5 * (1.0 + t) + 0.5 * x * (1.0 - t * t) * GELU_C * (1.0 + 3.0 * GELU_A * x * x)


class Comm(NamedTuple):
    ins: list
    outs: list
    nsem: int
    start: Callable
    finish: Callable


def _matmul(a, b, *, dims, grid, a_spec, b_spec, out_shape, out_spec, acc_shape, name,
            epilogue=None, extras=(), extra_specs=(), comm=None):
    nk = grid[2]
    outs = tuple(out_shape) if isinstance(out_shape, (tuple, list)) else (out_shape,)
    out_specs = tuple(out_spec) if isinstance(out_spec, (tuple, list)) else (out_spec,)
    n_ex, n_out = len(extras), len(outs)
    n_ci = len(comm.ins) if comm else 0
    n_co = len(comm.outs) if comm else 0

    def body(*refs):
        a_ref, b_ref = refs[0], refs[1]
        ex = refs[2:2 + n_ex]
        ci = refs[2 + n_ex:2 + n_ex + n_ci]
        o = refs[2 + n_ex + n_ci:2 + n_ex + n_ci + n_out]
        co = refs[2 + n_ex + n_ci + n_out:2 + n_ex + n_ci + n_out + n_co]
        scratch = refs[2 + n_ex + n_ci + n_out + n_co:]
        i, j, k = pl.program_id(0), pl.program_id(1), pl.program_id(2)

        if comm:
            send, recv = scratch[-2], scratch[-1]

            @pl.when((i == 0) & (j == 0) & (k == 0))
            def _():
                comm.start(ci, co, send, recv)

        part = lax.dot_general(a_ref[...].astype(BF), b_ref[...].astype(BF), dims, preferred_element_type=F32)

        def finish(val):
            res = epilogue(val, *[e[...] for e in ex]) if epilogue is not None else (val,)
            for r, oref in zip(res, o):
                oref[...] = r.astype(oref.dtype)

        if nk == 1:
            finish(part)
        else:
            acc = scratch[0]

            @pl.when(k == 0)
            def _():
                acc[...] = part

            @pl.when(k > 0)
            def _():
                acc[...] += part

            @pl.when(k == nk - 1)
            def _():
                finish(acc[...])

        if comm:
            @pl.when((i == grid[0] - 1) & (j == grid[1] - 1) & (k == nk - 1))
            def _():
                comm.finish(ci, co, send, recv)

    scratch_shapes = [pltpu.VMEM(acc_shape, F32)] if nk > 1 else []
    if comm:
        scratch_shapes += [pltpu.SemaphoreType.DMA((comm.nsem,)), pltpu.SemaphoreType.DMA((comm.nsem,))]
    res = _pcall(
        body, name=name, grid=grid,
        in_specs=[a_spec, b_spec, *extra_specs, *[ANY] * n_ci],
        out_specs=[*out_specs, *[ANY] * n_co],
        out_shape=[*outs, *(comm.outs if comm else [])],
        scratch_shapes=scratch_shapes,
        compiler_params=_cp(3),
    )(a, b, *extras, *(comm.ins if comm else []))
    main = res[0] if n_out == 1 else list(res[:n_out])
    return (main, list(res[n_out:])) if comm else main


def mm_nn(a, w, *, name, out_dtypes=(F32,), epilogue=None, extras=(), comm=None, tm=1024, tn=1024, tk=2048):
    M, K = a.shape
    if w.ndim == 3:
        S, _, Ns = w.shape
        N = S * Ns
    else:
        S, Ns = 1, w.shape[1]
        N = Ns
    tm, tn, tk = _tile(M, tm), _tile(Ns, tn), _tile(K, tk)
    npb = Ns // tn
    grid = (M // tm, N // tn, K // tk)
    a_spec = pl.BlockSpec((tm, tk), lambda i, j, k: (i, k))
    if w.ndim == 3:
        b_spec = pl.BlockSpec((None, tk, tn), lambda i, j, k: (j // npb, k, j % npb))
    else:
        b_spec = pl.BlockSpec((tk, tn), lambda i, j, k: (k, j))
    o_spec = pl.BlockSpec((tm, tn), lambda i, j, k: (i, j))
    outs = tuple(jax.ShapeDtypeStruct((M, N), d) for d in out_dtypes)
    res = _matmul(a, w, dims=NN, grid=grid, a_spec=a_spec, b_spec=b_spec,
                  out_shape=outs, out_spec=(o_spec,) * len(outs), acc_shape=(tm, tn), name=name,
                  epilogue=epilogue, extras=extras, extra_specs=(o_spec,) * len(extras), comm=comm)
    return res


def mm_nt(a, w, *, name, out_dtypes=(F32,), epilogue=None, extras=(), comm=None, tm=1024, tn=1024, tk=2048):
    M, Kc = a.shape
    if w.ndim == 3:
        S, Nout, Ks = w.shape
    else:
        S, (Nout, Ks) = 1, w.shape
    assert S * Ks == Kc
    tm, tn, tk = _tile(M, tm), _tile(Nout, tn), _tile(Ks, tk)
    kpb = Ks // tk
    grid = (M // tm, Nout // tn, Kc // tk)
    a_spec = pl.BlockSpec((tm, tk), lambda i, j, k: (i, k))
    if w.ndim == 3:
        b_spec = pl.BlockSpec((None, tn, tk), lambda i, j, k: (k // kpb, j, k % kpb))
    else:
        b_spec = pl.BlockSpec((tn, tk), lambda i, j, k: (j, k))
    o_spec = pl.BlockSpec((tm, tn), lambda i, j, k: (i, j))
    outs = tuple(jax.ShapeDtypeStruct((M, Nout), d) for d in out_dtypes)
    return _matmul(a, w, dims=NT, grid=grid, a_spec=a_spec, b_spec=b_spec,
                   out_shape=outs, out_spec=(o_spec,) * len(outs), acc_shape=(tm, tn), name=name,
                   epilogue=epilogue, extras=extras, extra_specs=(o_spec,) * len(extras), comm=comm)


def mm_tn(x, dy, *, name, shards=1, tm=1024, tn=1024, tk=2048):
    T, Kin = x.shape
    N = dy.shape[1]
    Ns = N // shards
    tm, tn, tk = _tile(Kin, tm), _tile(Ns, tn), _tile(T, tk)
    npb = Ns // tn
    grid = (Kin // tm, N // tn, T // tk)
    a_spec = pl.BlockSpec((tk, tm), lambda i, j, k: (k, i))
    b_spec = pl.BlockSpec((tk, tn), lambda i, j, k: (k, j))
    if shards > 1:
        o_spec = pl.BlockSpec((None, tm, tn), lambda i, j, k: (j // npb, i, j % npb))
        out = jax.ShapeDtypeStruct((shards, Kin, Ns), F32)
    else:
        o_spec = pl.BlockSpec((tm, tn), lambda i, j, k: (i, j))
        out = jax.ShapeDtypeStruct((Kin, N), F32)
    return _matmul(x, dy, dims=TN, grid=grid, a_spec=a_spec, b_spec=b_spec,
                   out_shape=out, out_spec=o_spec, acc_shape=(tm, tn), name=name)


ROW_TILE = 256


def _rows(tr, d):
    return pl.BlockSpec((tr, d), lambda i: (i, 0))


def _fixed(shape):
    nd = len(shape)
    return pl.BlockSpec(shape, lambda *_: (0,) * nd)


def _rms(xv, w):
    r = lax.rsqrt(jnp.mean(xv * xv, axis=-1, keepdims=True) + EPS)
    return xv * r * w


def rms_fwd(x, w, *, name):
    T, D = x.shape
    tr = _tile(T, ROW_TILE, SUBLANES)

    def body(x_ref, w_ref, y_ref):
        y_ref[...] = _rms(x_ref[...], w_ref[...]).astype(y_ref.dtype)

    return _pcall(body, name=name, grid=(T // tr,), in_specs=[_rows(tr, D), _fixed((1, D))],
                  out_specs=_rows(tr, D), out_shape=jax.ShapeDtypeStruct((T, D), BF), compiler_params=_cp(1))(x, w)


def res_norm(h, m, wa, wb, *, name):
    T, D = h.shape
    tr = _tile(T, ROW_TILE, SUBLANES)
    second = wb is not None

    def body(*refs):
        if second:
            h_ref, m_ref, wa_ref, wb_ref, ho_ref, y_ref = refs
        else:
            h_ref, m_ref, wa_ref, ho_ref = refs
        ho = h_ref[...] + _rms(m_ref[...], wa_ref[...])
        ho_ref[...] = ho
        if second:
            y_ref[...] = _rms(ho, wb_ref[...]).astype(y_ref.dtype)

    ins = [h, m, wa] + ([wb] if second else [])
    in_specs = [_rows(tr, D), _rows(tr, D), _fixed((1, D))] + ([_fixed((1, D))] if second else [])
    out_shape = [jax.ShapeDtypeStruct((T, D), F32)] + ([jax.ShapeDtypeStruct((T, D), BF)] if second else [])
    out_specs = [_rows(tr, D)] * len(out_shape)
    res = _pcall(body, name=name, grid=(T // tr,), in_specs=in_specs, out_specs=out_specs,
                 out_shape=out_shape, compiler_params=_cp(1))(*ins)
    return tuple(res) if second else (res[0], None)


def rms_bwd(x, w, dy, dres, *, name, out_dtype):
    T, D = x.shape
    tr = _tile(T, ROW_TILE, SUBLANES)
    has_res = dres is not None

    def body(*refs):
        if has_res:
            x_ref, w_ref, dy_ref, dr_ref, dx_ref, dw_ref = refs
        else:
            x_ref, w_ref, dy_ref, dx_ref, dw_ref = refs
        i = pl.program_id(0)
        xv = x_ref[...]
        r = lax.rsqrt(jnp.mean(xv * xv, axis=-1, keepdims=True) + EPS)
        xh = xv * r
        dyv = dy_ref[...].astype(F32)
        dyw = dyv * w_ref[...]
        dx = r * (dyw - xh * jnp.mean(dyw * xh, axis=-1, keepdims=True))
        if has_res:
            dx = dx + dr_ref[...].astype(F32)
        dx_ref[...] = dx.astype(dx_ref.dtype)
        part = jnp.sum(dyv * xh, axis=0, keepdims=True)

        @pl.when(i == 0)
        def _():
            dw_ref[...] = part

        @pl.when(i > 0)
        def _():
            dw_ref[...] += part

    ins = [x, w, dy] + ([dres] if has_res else [])
    in_specs = [_rows(tr, D), _fixed((1, D)), _rows(tr, D)] + ([_rows(tr, D)] if has_res else [])
    return _pcall(body, name=name, grid=(T // tr,), in_specs=in_specs,
                  out_specs=[_rows(tr, D), _fixed((1, D))],
                  out_shape=[jax.ShapeDtypeStruct((T, D), out_dtype), jax.ShapeDtypeStruct((1, D), F32)],
                  compiler_params=_cp(1))(*ins)


def loss_head(h, tgt, *, name):
    T, D = h.shape
    tr = _tile(T, ROW_TILE, SUBLANES)

    def body(h_ref, t_ref, dh_ref, l_ref):
        i = pl.program_id(0)
        e = h_ref[...] - t_ref[...]
        dh_ref[...] = e * (1.0 / D)
        part = 0.5 * jnp.sum(jnp.mean(e * e, axis=-1, keepdims=True), axis=0, keepdims=True)
        part = jnp.broadcast_to(part, (1, LANES))

        @pl.when(i == 0)
        def _():
            l_ref[...] = part

        @pl.when(i > 0)
        def _():
            l_ref[...] += part

    return _pcall(body, name=name, grid=(T // tr,), in_specs=[_rows(tr, D), _rows(tr, D)],
                  out_specs=[_rows(tr, D), _fixed((1, LANES))],
                  out_shape=[jax.ShapeDtypeStruct((T, D), F32), jax.ShapeDtypeStruct((1, LANES), F32)],
                  compiler_params=_cp(1))(h, tgt)


def adamw(w, g, m, v, *, name):
    R, C = w.shape
    tr = _tile(R, max(SUBLANES, (1 << 18) // C), SUBLANES)
    c1 = 1.0 - ADAM_B1 ** ADAM_STEP
    c2 = 1.0 - ADAM_B2 ** ADAM_STEP

    def body(w_ref, g_ref, m_ref, v_ref, d_ref, mo_ref, vo_ref):
        gv = g_ref[...]
        m2 = ADAM_B1 * m_ref[...] + (1.0 - ADAM_B1) * gv
        v2 = ADAM_B2 * v_ref[...] + (1.0 - ADAM_B2) * (gv * gv)
        d_ref[...] = -ADAM_LR * ((m2 / c1) / (jnp.sqrt(v2 / c2) + ADAM_EPS) + ADAM_WD * w_ref[...])
        mo_ref[...] = m2
        vo_ref[...] = v2

    spec = _rows(tr, C)
    sds = jax.ShapeDtypeStruct((R, C), F32)
    return _pcall(body, name=name, grid=(R // tr,), in_specs=[spec] * 4, out_specs=[spec] * 3,
                  out_shape=[sds] * 3, compiler_params=_cp(1))(w, g, m, v)


HALO = SUBLANES


def _conv_down(xx, w_ref):
    acc = xx * w_ref[pl.ds(CONV_WIDTH - 1, 1), :]
    for d in range(1, CONV_WIDTH):
        acc = acc + pltpu.roll(xx, d, 0) * w_ref[pl.ds(CONV_WIDTH - 1 - d, 1), :]
    return acc


def _conv_tile(x_ref, halo_ref, w_ref, first):
    xs = x_ref[...]
    hal = jnp.where(first, 0.0, halo_ref[...])
    cat = jnp.concatenate([hal, xs[0:HALO]], axis=0)
    return jnp.concatenate([_conv_down(cat, w_ref)[HALO:2 * HALO], _conv_down(xs, w_ref)[HALO:]], axis=0)


def _shift_down_tile(x_ref, halo_ref, first, d):
    xs = x_ref[...]
    if d == 0:
        return xs
    hal = jnp.where(first, 0.0, halo_ref[...])
    cat = jnp.concatenate([hal, xs[0:HALO]], axis=0)
    return jnp.concatenate([pltpu.roll(cat, d, 0)[HALO:2 * HALO], pltpu.roll(xs, d, 0)[HALO:]], axis=0)


def _l2n(s):
    return s * lax.rsqrt(jnp.sum(s * s, axis=-1, keepdims=True) + L2_EPS)


PREP_ROWS = 512


def _l2n_groups(s, nb):
    return jnp.concatenate([_l2n(s[:, g * LANES:(g + 1) * LANES]) for g in range(nb)], axis=1)


def prep_fwd(pm, off, wc8, woff, nblk, l2, *, name):
    T = pm.shape[0]
    tr = _tile(T, PREP_ROWS, SUBLANES)
    hb = tr // HALO
    wb = _heads_per_step(nblk)
    wl = wb * LANES

    def body(x_ref, halo_ref, w_ref, o_ref):
        i = pl.program_id(0)
        s = _silu(_conv_tile(x_ref, halo_ref, w_ref, i == 0))
        o_ref[...] = _l2n_groups(s, wb) if l2 else s

    return _pcall(
        body, name=name, grid=(T // tr, nblk // wb),
        in_specs=[pl.BlockSpec((tr, wl), lambda i, c: (i, off // wb + c)),
                  pl.BlockSpec((HALO, wl), lambda i, c: (jnp.maximum(i * hb - 1, 0), off // wb + c)),
                  pl.BlockSpec((SUBLANES, wl), lambda i, c: (0, woff // wb + c))],
        out_specs=pl.BlockSpec((tr, wl), lambda i, c: (i, c)),
        out_shape=jax.ShapeDtypeStruct((T, nblk * LANES), F32), compiler_params=_cp(2))(pm, pm, wc8)


def prep_bwd_act(pm, off, wc8, woff, nblk, l2, dout, *, name):
    T = pm.shape[0]
    tr = _tile(T, PREP_ROWS, SUBLANES)
    hb = tr // HALO
    wb = _heads_per_step(nblk)
    wl = wb * LANES

    def l2_bwd(s, do):
        r = lax.rsqrt(jnp.sum(s * s, axis=-1, keepdims=True) + L2_EPS)
        nrm = s * r
        return r * (do - nrm * jnp.sum(do * nrm, axis=-1, keepdims=True))

    def body(x_ref, halo_ref, w_ref, do_ref, dc_ref, dw_ref):
        i = pl.program_id(1)
        first = i == 0
        y = _conv_tile(x_ref, halo_ref, w_ref, first)
        s = _silu(y)
        do = do_ref[...]
        if l2:
            ds = jnp.concatenate([l2_bwd(s[:, g * LANES:(g + 1) * LANES], do[:, g * LANES:(g + 1) * LANES])
                                  for g in range(wb)], axis=1)
        else:
            ds = do
        dc = ds * _dsilu(y)
        dc_ref[...] = dc

        @pl.when(first)
        def _():
            dw_ref[...] = jnp.zeros_like(dw_ref)

        for j in range(CONV_WIDTH):
            xsh = _shift_down_tile(x_ref, halo_ref, first, CONV_WIDTH - 1 - j)
            dw_ref[pl.ds(j, 1), :] += jnp.sum(dc * xsh, axis=0, keepdims=True)

    return _pcall(
        body, name=name, grid=(nblk // wb, T // tr),
        in_specs=[pl.BlockSpec((tr, wl), lambda c, i: (i, off // wb + c)),
                  pl.BlockSpec((HALO, wl), lambda c, i: (jnp.maximum(i * hb - 1, 0), off // wb + c)),
                  pl.BlockSpec((SUBLANES, wl), lambda c, i: (0, woff // wb + c)),
                  pl.BlockSpec((tr, wl), lambda c, i: (i, c))],
        out_specs=[pl.BlockSpec((tr, wl), lambda c, i: (i, c)),
                   pl.BlockSpec((SUBLANES, wl), lambda c, i: (0, c))],
        out_shape=[jax.ShapeDtypeStruct((T, nblk * LANES), F32),
                   jax.ShapeDtypeStruct((SUBLANES, nblk * LANES), F32)],
        compiler_params=_cp(2))(pm, pm, wc8, dout)


def prep_bwd_conv(dc, wc8, woff, nblk, *, name):
    T = dc.shape[0]
    tr = _tile(T, PREP_ROWS, SUBLANES)
    hb = tr // HALO
    nt = T // tr
    last_halo = T // HALO - 1
    wb = _heads_per_step(nblk)
    wl = wb * LANES

    def up(xx, w_ref):
        rows = xx.shape[0]
        acc = xx * w_ref[pl.ds(CONV_WIDTH - 1, 1), :]
        for d in range(1, CONV_WIDTH):
            acc = acc + pltpu.roll(xx, rows - d, 0) * w_ref[pl.ds(CONV_WIDTH - 1 - d, 1), :]
        return acc

    def body(x_ref, halo_ref, w_ref, o_ref):
        i = pl.program_id(0)
        xs = x_ref[...]
        hal = jnp.where(i == nt - 1, 0.0, halo_ref[...])
        cat = jnp.concatenate([xs[tr - HALO:tr], hal], axis=0)
        out = jnp.concatenate([up(xs, w_ref)[:tr - HALO], up(cat, w_ref)[0:HALO]], axis=0)
        o_ref[...] = out.astype(o_ref.dtype)

    return _pcall(
        body, name=name, grid=(nt, nblk // wb),
        in_specs=[pl.BlockSpec((tr, wl), lambda i, c: (i, c)),
                  pl.BlockSpec((HALO, wl), lambda i, c: (jnp.minimum((i + 1) * hb, last_halo), c)),
                  pl.BlockSpec((SUBLANES, wl), lambda i, c: (0, woff // wb + c))],
        out_specs=pl.BlockSpec((tr, wl), lambda i, c: (i, c)),
        out_shape=jax.ShapeDtypeStruct((T, nblk * LANES), BF), compiler_params=_cp(2))(dc, dc, wc8)


def _softplus(x):
    return jnp.maximum(x, 0.0) + jnp.log(1.0 + jnp.exp(-jnp.abs(x)))


def _tril_ones(c):
    t = lax.broadcasted_iota(jnp.int32, (c, c), 0)
    s = lax.broadcasted_iota(jnp.int32, (c, c), 1)
    return (t >= s).astype(F32)


def _triu_ones(c):
    t = lax.broadcasted_iota(jnp.int32, (c, c), 0)
    s = lax.broadcasted_iota(jnp.int32, (c, c), 1)
    return (t <= s).astype(F32)


def gates_fwd(pg, arow, dtrow, H, *, name):
    T = pg.shape[0]
    C = LA_CHUNK
    N = T // C

    def body(x_ref, a_ref, dt_ref, bg_ref, gr_ref):
        x = x_ref[...]
        lane = lax.broadcasted_iota(jnp.int32, (C, LANES), 1)
        g = -jnp.exp(a_ref[...]) * _softplus(x + dt_ref[...])
        g = jnp.where((lane >= H) & (lane < 2 * H), g, 0.0)
        lm = _tril_ones(C)
        gc = _dot_hi(lm, g)
        bg_ref[...] = jnp.where(lane < H, _sigmoid(x), gc)
        gr_ref[...] = _dot_hi(g, _triu_ones(C), TN)

    return _pcall(
        body, name=name, grid=(N,),
        in_specs=[pl.BlockSpec((C, LANES), lambda n: (n, 0)), _fixed((1, LANES)), _fixed((1, LANES))],
        out_specs=[pl.BlockSpec((C, LANES), lambda n: (n, 0)), pl.BlockSpec((None, LANES, C), lambda n: (n, 0, 0))],
        out_shape=[jax.ShapeDtypeStruct((T, LANES), F32), jax.ShapeDtypeStruct((N, LANES, C), F32)],
        compiler_params=_cp(1))(pg, arow, dtrow)


def gates_bwd(pg, arow, dtrow, dbg, H, *, name):
    T = pg.shape[0]
    C = LA_CHUNK
    N = T // C

    def body(x_ref, a_ref, dt_ref, d_ref, dx_ref, da_ref, ddt_ref):
        n = pl.program_id(0)
        x = x_ref[...]
        d = d_ref[...]
        lane = lax.broadcasted_iota(jnp.int32, (C, LANES), 1)
        in_g = (lane >= H) & (lane < 2 * H)
        e = jnp.exp(a_ref[...])
        xs = x + dt_ref[...]
        g = -e * _softplus(xs)
        dg = _dot_hi(_tril_ones(C), jnp.where(in_g, d, 0.0), TN)
        dxs = jnp.where(in_g, dg * (-e) * _sigmoid(xs), 0.0)
        beta = _sigmoid(x)
        dx_ref[...] = jnp.where(lane < H, d * beta * (1.0 - beta), dxs).astype(dx_ref.dtype)
        pa = jnp.sum(jnp.where(in_g, dg * g, 0.0), axis=0, keepdims=True)
        pd = jnp.sum(dxs, axis=0, keepdims=True)

        @pl.when(n == 0)
        def _():
            da_ref[...] = pa
            ddt_ref[...] = pd

        @pl.when(n > 0)
        def _():
            da_ref[...] += pa
            ddt_ref[...] += pd

    return _pcall(
        body, name=name, grid=(N,),
        in_specs=[pl.BlockSpec((C, LANES), lambda n: (n, 0)), _fixed((1, LANES)), _fixed((1, LANES)),
                  pl.BlockSpec((C, LANES), lambda n: (n, 0))],
        out_specs=[pl.BlockSpec((C, LANES), lambda n: (n, 0)), _fixed((1, LANES)), _fixed((1, LANES))],
        out_shape=[jax.ShapeDtypeStruct((T, LANES), BF), jax.ShapeDtypeStruct((1, LANES), F32),
                   jax.ShapeDtypeStruct((1, LANES), F32)],
        compiler_params=_cp(1))(pg, arow, dtrow, dbg)


QK_SCALE = HEAD_DIM ** -0.5


HEADS_PER_STEP = 4


def _heads_per_step(H):
    hb = HEADS_PER_STEP
    while H % hb:
        hb //= 2
    return hb


def _head_rstd(o):
    return lax.rsqrt(jnp.mean(o * o, axis=-1, keepdims=True) + EPS)


def _gdn_gates(bg_ref, gr_ref, h, H):
    C = LA_CHUNK
    bgv = bg_ref[...]
    lane = lax.broadcasted_iota(jnp.int32, (C, LANES), 1)
    beta = jnp.sum(jnp.where(lane == h, bgv, 0.0), axis=1, keepdims=True)
    gc = jnp.sum(jnp.where(lane == H + h, bgv, 0.0), axis=1, keepdims=True)
    grow = gr_ref[pl.ds(H + h, 1), :]
    ri = lax.broadcasted_iota(jnp.int32, (C, 1), 0)
    gl = jnp.sum(jnp.where(ri == C - 1, gc, 0.0), axis=0, keepdims=True)
    return beta, gc, grow, gl


def _chunk_masks():
    C = LA_CHUNK
    ti = lax.broadcasted_iota(jnp.int32, (C, C), 0)
    si = lax.broadcasted_iota(jnp.int32, (C, C), 1)
    return ti >= si, ti > si, ti == si


def _decay(gc, grow, causal):
    return jnp.where(causal, jnp.exp(jnp.where(causal, gc - grow, 0.0)), 0.0)


def _interleave(gens):
    gens = list(gens)
    results = [None] * len(gens)
    live = list(range(len(gens)))
    while live:
        still = []
        for i in live:
            try:
                next(gens[i])
                still.append(i)
            except StopIteration as stop:
                results[i] = stop.value
        live = still
    return results


def _unit_lower_inverse(a, eye):
    x = -a
    p = jnp.where(eye, 1.0, 0.0) + x
    for _ in range(5):
        x = _dot_hi(x, x)
        yield
        p = p + _dot_hi(p, x)
        yield
    return p


def gdn_fwd(q, k, v, pm, zoff, bg, gcrow, wn, H, *, name):
    T = q.shape[0]
    C = LA_CHUNK
    N = T // C
    hd = HEAD_DIM

    HB = _heads_per_step(H)

    def body(q_ref, k_ref, v_ref, z_ref, bg_ref, gr_ref, wn_ref, og_ref, or_ref, sall_ref, tall_ref, S):
        n = pl.program_id(0)
        hg = pl.program_id(1)
        causal, strict, eye = _chunk_masks()

        @pl.when((n == 0) & (hg == 0))
        def _():
            S[...] = jnp.zeros_like(S)

        states = [S[hg * HB + i] for i in range(HB)]

        def head(i):
            h = hg * HB + i
            sl = slice(i * hd, (i + 1) * hd)
            beta, gc, grow, gl = _gdn_gates(bg_ref, gr_ref, h, H)
            dm = _decay(gc, grow, causal)
            qs = q_ref[:, sl] * QK_SCALE
            kk = k_ref[:, sl]
            vv = v_ref[:, sl]
            eg = jnp.exp(gc)
            kb = kk * beta
            a = jnp.where(strict, _dot(kb, kk, NT) * dm, 0.0)
            yield
            tm = yield from _unit_lower_inverse(a, eye)
            u = _dot(tm, vv * beta)
            w = _dot(tm, kb * eg)
            qk = jnp.where(causal, _dot(qs, kk, NT) * dm, 0.0)
            yield
            s0 = states[i]
            vnew = u - _dot(w, s0)
            o = _dot(qs * eg, s0)
            yield
            o = o + _dot(qk, vnew)
            s1 = s0 * jnp.exp(gl) + _dot(kk * jnp.exp(gl - gc), vnew, TN)
            yield
            sall_ref[i] = s0
            tall_ref[i] = tm
            or_ref[:, sl] = o
            og_ref[:, sl] = (o * _head_rstd(o) * wn_ref[...] * _silu(z_ref[:, sl])).astype(og_ref.dtype)
            return s1

        for i, s1 in enumerate(_interleave([head(i) for i in range(HB)])):
            S[hg * HB + i] = s1

    blk = lambda off: pl.BlockSpec((C, HB * hd), lambda n, h: (n, off // HB + h))
    return _pcall(
        body, name=name, grid=(N, H // HB),
        in_specs=[blk(0), blk(0), blk(0), blk(zoff),
                  pl.BlockSpec((C, LANES), lambda n, h: (n, 0)),
                  pl.BlockSpec((None, LANES, C), lambda n, h: (n, 0, 0)),
                  _fixed((1, hd))],
        out_specs=[blk(0), blk(0),
                   pl.BlockSpec((None, HB, hd, hd), lambda n, h: (n, h, 0, 0)),
                   pl.BlockSpec((None, HB, C, C), lambda n, h: (n, h, 0, 0))],
        out_shape=[jax.ShapeDtypeStruct((T, H * hd), BF), jax.ShapeDtypeStruct((T, H * hd), F32),
                   jax.ShapeDtypeStruct((N, H, hd, hd), F32), jax.ShapeDtypeStruct((N, H, C, C), F32)],
        scratch_shapes=[pltpu.VMEM((H, hd, hd), F32)],
        compiler_params=_cp(2))(q, k, v, pm, bg, gcrow, wn)


def gdn_bwd(q, k, v, pm, zoff, bg, gcrow, wn, oraw, sall, tall, dog, H, *, name):
    T = q.shape[0]
    C = LA_CHUNK
    N = T // C
    hd = HEAD_DIM

    HB = _heads_per_step(H)

    def body(*refs):
        dbg_ref, dwn_ref, dS = refs[15], refs[16], refs[17]
        n = pl.program_id(0)
        hg = pl.program_id(1)

        @pl.when((n == 0) & (hg == 0))
        def _():
            dwn_ref[...] = jnp.zeros_like(dwn_ref)
            dS[...] = jnp.zeros_like(dS)

        @pl.when(hg == 0)
        def _():
            dbg_ref[...] = jnp.zeros_like(dbg_ref)

        ds_in = [dS[hg * HB + i] for i in range(HB)]
        outs = _interleave([head(i, hg * HB + i, ds_in[i], *refs) for i in range(HB)])
        for i in range(HB):
            dS[hg * HB + i] = outs[i][0]
        dwn_ref[...] += sum(o[1] for o in outs)
        dbg_ref[...] += sum(o[2] for o in outs)

    def head(i, h, ds1, q_ref, k_ref, v_ref, z_ref, bg_ref, gr_ref, wn_ref, or_ref, sall_ref, tall_ref, dog_ref,
             dq_ref, dk_ref, dv_ref, dz_ref, dbg_ref, dwn_ref, dS):
        sl = slice(i * hd, (i + 1) * hd)
        beta, gc, grow, gl = _gdn_gates(bg_ref, gr_ref, h, H)
        causal, strict, eye = _chunk_masks()
        dm = _decay(gc, grow, causal)
        qs = q_ref[:, sl] * QK_SCALE
        kk = k_ref[:, sl]
        vv = v_ref[:, sl]
        zz = z_ref[:, sl]
        o = or_ref[:, sl]
        dog = dog_ref[:, sl]
        wn_v = wn_ref[...]
        s0 = sall_ref[i]
        tm = tall_ref[i]

        rstd = _head_rstd(o)
        on = o * rstd
        sz = _silu(zz)
        don = dog * wn_v * sz
        dwn_part = jnp.sum(dog * on * sz, axis=0, keepdims=True)
        dz_ref[:, sl] = (dog * on * wn_v * _dsilu(zz)).astype(dz_ref.dtype)
        do = rstd * (don - on * jnp.mean(don * on, axis=-1, keepdims=True))

        eg = jnp.exp(gc)
        kb = kk * beta
        vb = vv * beta
        kbg = kb * eg
        a = jnp.where(strict, _dot(kb, kk, NT) * dm, 0.0)
        u = _dot(tm, vb)
        w = _dot(tm, kbg)
        qk = jnp.where(causal, _dot(qs, kk, NT) * dm, 0.0)
        dqdec = _dot(do, s0, NT)
        yield
        vnew = u - _dot(w, s0)
        qdec = qs * eg
        etail = jnp.exp(gl - gc)
        ktail = kk * etail
        egl = jnp.exp(gl)
        dvnew = _dot(qk, do, TN) + _dot(ktail, ds1)
        yield
        dqk = jnp.where(causal, _dot(do, vnew, NT), 0.0)
        dktail = _dot(vnew, ds1, NT)
        dcd = jnp.sum(jnp.sum(s0 * ds1, axis=1, keepdims=True), axis=0, keepdims=True)
        ds0 = egl * ds1 + _dot(qdec, do, TN) - _dot(w, dvnew, TN)
        dw = -_dot(dvnew, s0, NT)
        dvb = _dot(tm, dvnew, TN)
        yield
        dkbg = _dot(tm, dw, TN)
        dtm = _dot(dvnew, vb, NT) + _dot(dw, kbg, NT)
        dqkr = dqk * dm
        dqs = _dot(dqkr, kk) + dqdec * eg
        yield
        x = _dot_hi(tm, dtm, TN)
        yield
        da = jnp.where(strict, -_dot_hi(x, tm, NT), 0.0)
        yield
        dkk = da * dm
        dkb = _dot(dkk, kk) + dkbg * eg
        dk = _dot(dkk, kb, TN)
        dk = dk + _dot(dqkr, qs, TN) + dktail * etail + dkb * beta
        g = da * a + dqk * qk
        colsum = jnp.max(_dot_hi(g, jnp.ones((C, LANES), F32), TN), axis=1, keepdims=True)
        yield
        rk = jnp.sum(dktail * ktail, axis=1, keepdims=True)
        dgc = (jnp.sum(g, axis=1, keepdims=True) - colsum
               + jnp.sum(dqdec * qdec, axis=1, keepdims=True) - rk
               + jnp.sum(dkbg * kbg, axis=1, keepdims=True))
        dgl = jnp.sum(rk, axis=0, keepdims=True) + dcd * egl
        ri = lax.broadcasted_iota(jnp.int32, (C, 1), 0)
        dgc = dgc + jnp.where(ri == C - 1, dgl, 0.0)
        dbeta = jnp.sum(dkb * kk, axis=1, keepdims=True) + jnp.sum(dvb * vv, axis=1, keepdims=True)

        dq_ref[:, sl] = dqs * QK_SCALE
        dk_ref[:, sl] = dk
        dv_ref[:, sl] = dvb * beta
        lane = lax.broadcasted_iota(jnp.int32, (C, LANES), 1)
        return ds0, dwn_part, jnp.where(lane == h, dbeta, 0.0) + jnp.where(lane == H + h, dgc, 0.0)

    blk = lambda off: pl.BlockSpec((C, HB * hd), lambda n, h: (N - 1 - n, off // HB + h))
    st = lambda r: pl.BlockSpec((None, HB, r, r), lambda n, h: (N - 1 - n, h, 0, 0))
    return _pcall(
        body, name=name, grid=(N, H // HB),
        in_specs=[blk(0), blk(0), blk(0), blk(zoff),
                  pl.BlockSpec((C, LANES), lambda n, h: (N - 1 - n, 0)),
                  pl.BlockSpec((None, LANES, C), lambda n, h: (N - 1 - n, 0, 0)),
                  _fixed((1, hd)), blk(0), st(hd), st(C), blk(0)],
        out_specs=[blk(0), blk(0), blk(0), blk(0),
                   pl.BlockSpec((C, LANES), lambda n, h: (N - 1 - n, 0)), _fixed((1, hd))],
        out_shape=[jax.ShapeDtypeStruct((T, H * hd), F32)] * 3
        + [jax.ShapeDtypeStruct((T, H * hd), BF), jax.ShapeDtypeStruct((T, LANES), F32),
           jax.ShapeDtypeStruct((1, hd), F32)],
        scratch_shapes=[pltpu.VMEM((H, hd, hd), F32)],
        compiler_params=_cp(2))(q, k, v, pm, bg, gcrow, wn, oraw, sall, tall, dog)


def _rot(x, cs, sn):
    return x * cs + pltpu.roll(x, HEAD_DIM // 2, 1) * sn


def _rot_t(dy, cs, sn):
    return dy * cs + pltpu.roll(dy * sn, HEAD_DIM // 2, 1)


def ret_fwd(pm, qoff, koff, voff, goff, cs, sn, dmat, avec, bvec, gam, H, *, name):
    T = pm.shape[0]
    C = LA_CHUNK
    N = T // C
    hd = HEAD_DIM

    HB = _heads_per_step(H)

    def body(q_ref, k_ref, v_ref, g_ref, cs_ref, sn_ref, dm_ref, a_ref, b_ref, gam_ref,
             og_ref, or_ref, sall_ref, S):
        n = pl.program_id(0)
        hg = pl.program_id(1)
        c, s = cs_ref[...], sn_ref[...]

        @pl.when((n == 0) & (hg == 0))
        def _():
            S[...] = jnp.zeros_like(S)

        states = [S[hg * HB + i] for i in range(HB)]

        def head(i):
            sl = slice(i * hd, (i + 1) * hd)
            qq = _rot(q_ref[:, sl], c, s)
            kk = _rot(k_ref[:, sl], c, s) * QK_SCALE
            vv = v_ref[:, sl]
            s0 = states[i]
            p = _dot(qq, kk, NT) * dm_ref[i]
            cross = _dot(qq * a_ref[i], s0)
            s1 = s0 * gam_ref[i] + _dot(kk * b_ref[i], vv, TN)
            yield
            o = _dot(p, vv) + cross
            yield
            sall_ref[i] = s0
            or_ref[:, sl] = o
            og_ref[:, sl] = (_silu(g_ref[:, sl]) * o * _head_rstd(o)).astype(og_ref.dtype)
            return s1

        for i, s1 in enumerate(_interleave([head(i) for i in range(HB)])):
            S[hg * HB + i] = s1

    blk = lambda off: pl.BlockSpec((C, HB * hd), lambda n, h: (n, off // HB + h))
    tab = pl.BlockSpec((C, hd), lambda n, h: (n, 0))
    per_h = lambda r, cdim: pl.BlockSpec((HB, r, cdim), lambda n, h: (h, 0, 0))
    return _pcall(
        body, name=name, grid=(N, H // HB),
        in_specs=[blk(qoff), blk(koff), blk(voff), blk(goff), tab, tab,
                  per_h(C, C), per_h(C, hd), per_h(C, hd), per_h(1, hd)],
        out_specs=[blk(0), blk(0), pl.BlockSpec((None, HB, hd, hd), lambda n, h: (n, h, 0, 0))],
        out_shape=[jax.ShapeDtypeStruct((T, H * hd), BF), jax.ShapeDtypeStruct((T, H * hd), F32),
                   jax.ShapeDtypeStruct((N, H, hd, hd), F32)],
        scratch_shapes=[pltpu.VMEM((H, hd, hd), F32)],
        compiler_params=_cp(2))(pm, pm, pm, pm, cs, sn, dmat, avec, bvec, gam)


def ret_bwd(pm, qoff, koff, voff, goff, cs, sn, dmat, avec, bvec, gam, oraw, sall, dog, dogoff, H, *, name):
    T = pm.shape[0]
    C = LA_CHUNK
    N = T // C
    hd = HEAD_DIM

    HB = _heads_per_step(H)

    def body(q_ref, k_ref, v_ref, g_ref, cs_ref, sn_ref, dm_ref, a_ref, b_ref, gam_ref, or_ref, sall_ref,
             dog_ref, dq_ref, dk_ref, dv_ref, dg_ref, dS):
        n = pl.program_id(0)
        hg = pl.program_id(1)
        c, s = cs_ref[...], sn_ref[...]

        @pl.when((n == 0) & (hg == 0))
        def _():
            dS[...] = jnp.zeros_like(dS)

        dstates = [dS[hg * HB + i] for i in range(HB)]

        def head(i):
            sl = slice(i * hd, (i + 1) * hd)
            qq = _rot(q_ref[:, sl], c, s)
            kk = _rot(k_ref[:, sl], c, s) * QK_SCALE
            vv = v_ref[:, sl]
            gg = g_ref[:, sl]
            o = or_ref[:, sl]
            dog = dog_ref[:, sl]
            dm = dm_ref[i]
            av, bv = a_ref[i], b_ref[i]
            s0 = sall_ref[i]
            ds1 = dstates[i]

            rstd = _head_rstd(o)
            on = o * rstd
            don = dog * _silu(gg)
            dg_ref[:, sl] = (dog * on * _dsilu(gg)).astype(dg_ref.dtype)
            do = rstd * (don - on * jnp.mean(don * on, axis=-1, keepdims=True))

            p = _dot(qq, kk, NT) * dm
            dp = _dot(do, vv, NT) * dm
            cross_q = _dot(do, s0, NT) * av
            cross_k = _dot(vv, ds1, NT) * bv
            cross_v = _dot(kk * bv, ds1)
            ds0 = ds1 * gam_ref[i] + _dot(qq * av, do, TN)
            yield
            dv_ref[:, sl] = (_dot(p, do, TN) + cross_v).astype(dv_ref.dtype)
            dqq = _dot(dp, kk) + cross_q
            dkk = (_dot(dp, qq, TN) + cross_k) * QK_SCALE
            yield
            dq_ref[:, sl] = _rot_t(dqq, c, s).astype(dq_ref.dtype)
            dk_ref[:, sl] = _rot_t(dkk, c, s).astype(dk_ref.dtype)
            return ds0

        for i, ds0 in enumerate(_interleave([head(i) for i in range(HB)])):
            dS[hg * HB + i] = ds0

    blk = lambda off: pl.BlockSpec((C, HB * hd), lambda n, h: (N - 1 - n, off // HB + h))
    tab = pl.BlockSpec((C, hd), lambda n, h: (N - 1 - n, 0))
    per_h = lambda r, cdim: pl.BlockSpec((HB, r, cdim), lambda n, h: (h, 0, 0))
    return _pcall(
        body, name=name, grid=(N, H // HB),
        in_specs=[blk(qoff), blk(koff), blk(voff), blk(goff), tab, tab,
                  per_h(C, C), per_h(C, hd), per_h(C, hd), per_h(1, hd), blk(0),
                  pl.BlockSpec((None, HB, hd, hd), lambda n, h: (N - 1 - n, h, 0, 0)), blk(dogoff)],
        out_specs=[blk(0)] * 4,
        out_shape=[jax.ShapeDtypeStruct((T, H * hd), BF)] * 4,
        scratch_shapes=[pltpu.VMEM((H, hd, hd), F32)],
        compiler_params=_cp(2))(pm, pm, pm, pm, cs, sn, dmat, avec, bvec, gam, oraw, sall, dog)


LN_ROWS = 128


def ln_fwd(pre, lw, lb, *, name):
    T, W2 = pre.shape
    W = W2 // 2
    tr = _tile(T, LN_ROWS, SUBLANES)

    def body(p_ref, w_ref, b_ref, o_ref):
        v = _gelu(p_ref[...])
        xc = v - jnp.mean(v, axis=-1, keepdims=True)
        r = lax.rsqrt(jnp.mean(xc * xc, axis=-1, keepdims=True) + EPS)
        o_ref[...] = xc * r * w_ref[...] + b_ref[...]

    return _pcall(body, name=name, grid=(T // tr,),
                  in_specs=[pl.BlockSpec((tr, W), lambda i: (i, 1)), _fixed((1, W)), _fixed((1, W))],
                  out_specs=_rows(tr, W), out_shape=jax.ShapeDtypeStruct((T, W), F32),
                  compiler_params=_cp(1))(pre, lw, lb)


def ln_bwd(pre, lw, dvn, *, name):
    T, W2 = pre.shape
    W = W2 // 2
    tr = _tile(T, LN_ROWS, SUBLANES)

    def body(p_ref, w_ref, d_ref, dp_ref, dw_ref, db_ref):
        i = pl.program_id(0)
        pv = p_ref[...]
        v = _gelu(pv)
        xc = v - jnp.mean(v, axis=-1, keepdims=True)
        r = lax.rsqrt(jnp.mean(xc * xc, axis=-1, keepdims=True) + EPS)
        xh = xc * r
        d = d_ref[...]
        dxh = d * w_ref[...]
        dv = r * (dxh - jnp.mean(dxh, axis=-1, keepdims=True) - xh * jnp.mean(dxh * xh, axis=-1, keepdims=True))
        dp_ref[...] = (dv * _dgelu(pv)).astype(dp_ref.dtype)
        pw = jnp.sum(d * xh, axis=0, keepdims=True)
        pb = jnp.sum(d, axis=0, keepdims=True)

        @pl.when(i == 0)
        def _():
            dw_ref[...] = pw
            db_ref[...] = pb

        @pl.when(i > 0)
        def _():
            dw_ref[...] += pw
            db_ref[...] += pb

    return _pcall(body, name=name, grid=(T // tr,),
                  in_specs=[pl.BlockSpec((tr, W), lambda i: (i, 1)), _fixed((1, W)), _rows(tr, W)],
                  out_specs=[_rows(tr, W), _fixed((1, W)), _fixed((1, W))],
                  out_shape=[jax.ShapeDtypeStruct((T, W), BF), jax.ShapeDtypeStruct((1, W), F32),
                             jax.ShapeDtypeStruct((1, W), F32)],
                  compiler_params=_cp(1))(pre, lw, dvn)


def _tril_mask(c):
    t = lax.broadcasted_iota(jnp.int32, (c, c), 0)
    s = lax.broadcasted_iota(jnp.int32, (c, c), 1)
    return t >= s


def sg_fwd(pre, vn, ws, bs3, *, name):
    T, W = vn.shape
    G = ws.shape[0]
    gd = W // G
    C = SG_CHUNK

    def body(p_ref, v_ref, w_ref, b_ref, o_ref):
        mask = _tril_mask(C)
        for g in range(G):
            sl = slice(g * gd, (g + 1) * gd)
            wm = jnp.where(mask, w_ref[g], 0.0)
            s = _dot(wm, v_ref[:, sl]) + b_ref[g]
            o_ref[:, sl] = (_gelu(p_ref[:, sl]) * s).astype(o_ref.dtype)

    blk = pl.BlockSpec((C, W), lambda n: (n, 0))
    return _pcall(body, name=name, grid=(T // C,),
                  in_specs=[blk, blk, _fixed((G, C, C)), _fixed((G, C, 1))],
                  out_specs=blk, out_shape=jax.ShapeDtypeStruct((T, W), BF),
                  compiler_params=_cp(1))(pre, vn, ws, bs3)


def sg_bwd(pre, vn, ws, bs3, dus, *, name):
    T, W = vn.shape
    G = ws.shape[0]
    gd = W // G
    C = SG_CHUNK

    def body(p_ref, v_ref, w_ref, b_ref, d_ref, dp_ref, dv_ref, dw_ref, db_ref):
        n = pl.program_id(0)
        mask = _tril_mask(C)

        @pl.when(n == 0)
        def _():
            dw_ref[...] = jnp.zeros_like(dw_ref)
            db_ref[...] = jnp.zeros_like(db_ref)

        for g in range(G):
            sl = slice(g * gd, (g + 1) * gd)
            wm = jnp.where(mask, w_ref[g], 0.0)
            pv = p_ref[:, sl]
            vv = v_ref[:, sl]
            d = d_ref[:, sl]
            s = _dot(wm, vv) + b_ref[g]
            ds = d * _gelu(pv)
            dp_ref[:, sl] = (d * s * _dgelu(pv)).astype(dp_ref.dtype)
            dv_ref[:, sl] = _dot(wm, ds, TN)
            dw_ref[g] += jnp.where(mask, _dot(ds, vv, NT), 0.0)
            db_ref[g] += jnp.sum(ds, axis=1, keepdims=True)

    blk = pl.BlockSpec((C, W), lambda n: (n, 0))
    return _pcall(body, name=name, grid=(T // C,),
                  in_specs=[blk, blk, _fixed((G, C, C)), _fixed((G, C, 1)), blk],
                  out_specs=[blk, blk, _fixed((G, C, C)), _fixed((G, C, 1))],
                  out_shape=[jax.ShapeDtypeStruct((T, W), BF), jax.ShapeDtypeStruct((T, W), F32),
                             jax.ShapeDtypeStruct((G, C, C), F32), jax.ShapeDtypeStruct((G, C, 1), F32)],
                  compiler_params=_cp(1))(pre, vn, ws, bs3, dus)


CHIP_RELATIONS = ((1, 0), (0, 1), (1, 1))


def _place():
    return lax.axis_index("x"), lax.axis_index("y"), lax.axis_index("c")


def _peer_chip(x, y, r):
    fx, fy = CHIP_RELATIONS[r]
    return (1 - x if fx else x), (1 - y if fy else y)


def gather_comm(arrs):
    n = len(arrs)
    per = 2 * len(CHIP_RELATIONS)

    def ici(a, r, ins, outs, send, recv):
        x, y, c = _place()
        px, py = _peer_chip(x, y, r)
        return pltpu.make_async_remote_copy(
            src_ref=ins[a].at[c], dst_ref=outs[a].at[2 * x + y, c], send_sem=send.at[a * per + r],
            recv_sem=recv.at[a * per + r], device_id=(px, py, c), device_id_type=MESH)

    def start(ins, outs, send, recv):
        for a in range(n):
            for r in range(3):
                ici(a, r, ins, outs, send, recv).start()

    def finish(ins, outs, send, recv):
        x, y, c = _place()
        sib = (x, y, 1 - c)
        forwards = []
        for a in range(n):
            for r in range(3):
                px, py = _peer_chip(x, y, r)
                landed = outs[a].at[2 * px + py, c]
                pltpu.make_async_remote_copy(
                    src_ref=landed, dst_ref=landed, send_sem=send.at[a * per + r],
                    recv_sem=recv.at[a * per + r], device_id=(px, py, c), device_id_type=MESH).wait_recv()
                fw = pltpu.make_async_remote_copy(
                    src_ref=landed, dst_ref=landed, send_sem=send.at[a * per + 3 + r],
                    recv_sem=recv.at[a * per + 3 + r], device_id=sib, device_id_type=MESH)
                fw.start()
                forwards.append(fw)
        for a in range(n):
            for r in range(3):
                px, py = _peer_chip(x, y, r)
                other = outs[a].at[2 * px + py, 1 - c]
                pltpu.make_async_remote_copy(
                    src_ref=other, dst_ref=other, send_sem=send.at[a * per + 3 + r],
                    recv_sem=recv.at[a * per + 3 + r], device_id=sib, device_id_type=MESH).wait_recv()
        for a in range(n):
            for r in range(3):
                ici(a, r, ins, outs, send, recv).wait_send()
        for fw in forwards:
            fw.wait_send()

    outs = [jax.ShapeDtypeStruct((N_CHIPS,) + a.shape, a.dtype) for a in arrs]
    return Comm(list(arrs), outs, n * per, start, finish)


def chip_exchange_comm(ps):
    n = len(ps)

    def copies(ins, outs, send, recv):
        x, y, c = _place()
        cps = []
        for a in range(n):
            for r in range(3):
                px, py = _peer_chip(x, y, r)
                cps.append(pltpu.make_async_remote_copy(
                    src_ref=ins[a].at[2 * px + py], dst_ref=outs[a].at[r], send_sem=send.at[3 * a + r],
                    recv_sem=recv.at[3 * a + r], device_id=(px, py, c), device_id_type=MESH))
        return cps

    def start(ins, outs, send, recv):
        for cp in copies(ins, outs, send, recv):
            cp.start()

    def finish(ins, outs, send, recv):
        for cp in copies(ins, outs, send, recv):
            cp.wait()

    outs = [jax.ShapeDtypeStruct((3,) + p.shape[1:], p.dtype) for p in ps]
    return Comm(list(ps), outs, 3 * n, start, finish)


def run_comm(comm, *, name):
    n_i, n_o = len(comm.ins), len(comm.outs)

    def body(*refs):
        ins, outs = refs[:n_i], refs[n_i:n_i + n_o]
        send, recv = refs[n_i + n_o:]
        comm.start(ins, outs, send, recv)
        comm.finish(ins, outs, send, recv)

    res = _pcall(body, name=name, in_specs=[ANY] * n_i, out_specs=[ANY] * n_o, out_shape=comm.outs,
                 scratch_shapes=[pltpu.SemaphoreType.DMA((comm.nsem,)), pltpu.SemaphoreType.DMA((comm.nsem,))])(*comm.ins)
    return list(res)


def pair_exchange(gs, *, name):
    n = len(gs)

    def body(*refs):
        ins, outs = refs[:n], refs[n:2 * n]
        send, recv = refs[2 * n:2 * n + 2]
        x, y, c = _place()
        cps = []
        for a in range(n):
            cp = pltpu.make_async_remote_copy(
                src_ref=ins[a].at[:, pl.ds(1 - c, 1)], dst_ref=outs[a], send_sem=send.at[a], recv_sem=recv.at[a],
                device_id=(x, y, 1 - c), device_id_type=MESH)
            cp.start()
            cps.append(cp)
        for cp in cps:
            cp.wait()

    out_shape = [jax.ShapeDtypeStruct((g.shape[0], 1) + g.shape[2:], g.dtype) for g in gs]
    res = _pcall(body, name=name, in_specs=[ANY] * n, out_specs=[ANY] * n, out_shape=out_shape,
                 scratch_shapes=[pltpu.SemaphoreType.DMA((n,)), pltpu.SemaphoreType.DMA((n,))])(*gs)
    return list(res)


def pair_share(fs, *, name):
    n = len(fs)

    def body(*refs):
        ins, outs = refs[:n], refs[n:2 * n]
        send, recv = refs[2 * n:2 * n + 2]
        x, y, c = _place()
        cps = []
        for a in range(n):
            cp = pltpu.make_async_remote_copy(
                src_ref=ins[a], dst_ref=outs[a], send_sem=send.at[a], recv_sem=recv.at[a],
                device_id=(x, y, 1 - c), device_id_type=MESH)
            cp.start()
            cps.append(cp)
        for cp in cps:
            cp.wait()

    out_shape = [jax.ShapeDtypeStruct(f.shape, f.dtype) for f in fs]
    res = _pcall(body, name=name, in_specs=[ANY] * n, out_specs=[ANY] * n, out_shape=out_shape,
                 scratch_shapes=[pltpu.SemaphoreType.DMA((n,)), pltpu.SemaphoreType.DMA((n,))])(*fs)
    return list(res)


def all_reduce_small(v, *, name):
    R = v.shape[0]

    def body(v_ref, sum_ref, gat_ref, send, recv):
        x, y, c = _place()
        me = 4 * x + 2 * y + c
        gat_ref[me] = v_ref[...]
        cps = []
        peers = []
        for r in range(1, N_DEV):
            fx, fy, fc = (r >> 2) & 1, (r >> 1) & 1, r & 1
            px, py, pc = (1 - x if fx else x), (1 - y if fy else y), (1 - c if fc else c)
            peers.append((px, py, pc))
            cp = pltpu.make_async_remote_copy(
                src_ref=v_ref, dst_ref=gat_ref.at[me], send_sem=send.at[r - 1], recv_sem=recv.at[r - 1],
                device_id=(px, py, pc), device_id_type=MESH)
            cp.start()
            cps.append(cp)
        for r in range(1, N_DEV):
            px, py, pc = peers[r - 1]
            slot = gat_ref.at[4 * px + 2 * py + pc]
            pltpu.make_async_remote_copy(
                src_ref=v_ref, dst_ref=slot, send_sem=send.at[r - 1], recv_sem=recv.at[r - 1],
                device_id=(px, py, pc), device_id_type=MESH).wait_recv()
        for cp in cps:
            cp.wait_send()
        acc = gat_ref[0]
        for s in range(1, N_DEV):
            acc = acc + gat_ref[s]
        sum_ref[...] = acc

    vm = pl.BlockSpec(memory_space=pltpu.VMEM)
    res = _pcall(body, name=name, in_specs=[vm], out_specs=[vm, vm],
                 out_shape=[jax.ShapeDtypeStruct((R, LANES), F32), jax.ShapeDtypeStruct((N_DEV, R, LANES), F32)],
                 scratch_shapes=[pltpu.SemaphoreType.DMA((N_DEV - 1,)), pltpu.SemaphoreType.DMA((N_DEV - 1,))],
                 compiler_params=pltpu.CompilerParams(vmem_limit_bytes=VMEM_LIMIT))(v)
    return res[0]


def pair_sum(g, r1, c_idx, *, name):
    nb, _, hr, C = g.shape
    tr = _tile(hr, max(BF16_ROWS, (1 << 18) // C), BF16_ROWS)

    def body(c_ref, g_ref, r_ref, o_ref, ob_ref):
        s = g_ref[...] + r_ref[...]
        o_ref[...] = s
        ob_ref[...] = s.astype(ob_ref.dtype)

    out = pl.BlockSpec((None, tr, C), lambda b, i, cr: (b, i, 0))
    gs = pltpu.PrefetchScalarGridSpec(
        num_scalar_prefetch=1, grid=(nb, hr // tr),
        in_specs=[pl.BlockSpec((None, None, tr, C), lambda b, i, cr: (b, cr[0], i, 0)),
                  pl.BlockSpec((None, None, tr, C), lambda b, i, cr: (b, 0, i, 0))],
        out_specs=[out, out])
    return _pcall(body, name=name, grid_spec=gs,
                  out_shape=[jax.ShapeDtypeStruct((nb, hr, C), F32), jax.ShapeDtypeStruct((nb, hr, C), BF)],
                  compiler_params=_cp(2))(c_idx, g, r1)


def chip_sum(p, r2, j_idx, *, name):
    _, hr, C = p.shape
    tr = _tile(hr, max(BF16_ROWS, (1 << 18) // C), BF16_ROWS)

    def body(j_ref, p_ref, a_ref, b_ref, c_ref, o_ref):
        o_ref[...] = ((p_ref[...] + a_ref[...].astype(F32)) + b_ref[...].astype(F32)) + c_ref[...].astype(F32)

    rel = lambda r: pl.BlockSpec((None, tr, C), lambda i, jr: (r, i, 0))
    gs = pltpu.PrefetchScalarGridSpec(
        num_scalar_prefetch=1, grid=(hr // tr,),
        in_specs=[pl.BlockSpec((None, tr, C), lambda i, jr: (jr[0], i, 0)), rel(0), rel(1), rel(2)],
        out_specs=pl.BlockSpec((tr, C), lambda i, jr: (i, 0)))
    return _pcall(body, name=name, grid_spec=gs, out_shape=jax.ShapeDtypeStruct((hr, C), F32),
                  compiler_params=_cp(1))(j_idx, p, r2, r2, r2)


def adamw_halves(w, g_mine, g_other, m, v, c_idx, *, name):
    R, C = w.shape
    hr = R // 2
    tr = _tile(hr, max(SUBLANES, (1 << 18) // C), SUBLANES)
    nbh = hr // tr
    c1 = 1.0 - ADAM_B1 ** ADAM_STEP
    c2 = 1.0 - ADAM_B2 ** ADAM_STEP

    def body(c_ref, w_ref, gm_ref, go_ref, m_ref, v_ref, g_ref, d_ref, mo_ref, vo_ref):
        i = pl.program_id(0)
        gv = jnp.where(i // nbh == c_ref[0], gm_ref[...], go_ref[...])
        m2 = ADAM_B1 * m_ref[...] + (1.0 - ADAM_B1) * gv
        v2 = ADAM_B2 * v_ref[...] + (1.0 - ADAM_B2) * (gv * gv)
        d_ref[...] = -ADAM_LR * ((m2 / c1) / (jnp.sqrt(v2 / c2) + ADAM_EPS) + ADAM_WD * w_ref[...])
        g_ref[...] = gv
        mo_ref[...] = m2
        vo_ref[...] = v2

    full = pl.BlockSpec((tr, C), lambda i, cr: (i, 0))
    half = pl.BlockSpec((tr, C), lambda i, cr: (i % nbh, 0))
    gs = pltpu.PrefetchScalarGridSpec(
        num_scalar_prefetch=1, grid=(R // tr,),
        in_specs=[full, half, half, full, full], out_specs=[full] * 4)
    sds = jax.ShapeDtypeStruct((R, C), F32)
    return _pcall(body, name=name, grid_spec=gs, out_shape=[sds] * 4,
                  compiler_params=_cp(1))(c_idx, w, g_mine, g_other, m, v)


def _pack_rows(arrs):
    parts = []
    for a in arrs:
        flat = a.reshape(-1).astype(F32)
        tile = SUBLANES * LANES
        pad = (-flat.shape[0]) % tile
        parts.append(jnp.pad(flat, (0, pad)).reshape(-1, LANES))
    return jnp.concatenate(parts, axis=0)


def _unpack_rows(buf, shapes):
    out, row = [], 0
    for shp in shapes:
        size = int(np.prod(shp))
        rows = -(-size // (SUBLANES * LANES)) * SUBLANES
        out.append(buf[row:row + rows].reshape(-1)[:size].reshape(shp))
        row += rows
    return out


def _halves(a2d):
    r, c = a2d.shape
    return a2d.reshape(2, r // 2, c)


def _rotary_tables(T):
    half = HEAD_DIM // 2
    pos = jnp.arange(T, dtype=F32)
    inv_freq = 1.0 / (ROPE_BASE ** jnp.linspace(0.0, 1.0, half, dtype=F32))
    ang = pos[:, None] * inv_freq[None, :]
    cos, sin = jnp.cos(ang), jnp.sin(ang)
    return jnp.concatenate([cos, cos], axis=1), jnp.concatenate([-sin, sin], axis=1)


def _retention_tables(H):
    C = LA_CHUNK
    lg = jnp.log1p(-jnp.power(2.0, -5.0 - jnp.arange(H, dtype=F32)))
    pos = jnp.arange(C, dtype=F32)
    causal = jnp.tril(jnp.ones((C, C), dtype=bool))
    dmat = jnp.exp(jnp.where(causal, (pos[:, None] - pos[None, :]) * lg[:, None, None], -jnp.inf))
    bc = lambda t: jnp.broadcast_to(t[..., None], t.shape + (HEAD_DIM,))
    avec = bc(jnp.exp((pos + 1.0)[None, :] * lg[:, None]))
    bvec = bc(jnp.exp((C - 1.0 - pos)[None, :] * lg[:, None]))
    gam = bc(jnp.exp(C * lg)[:, None])
    return dmat, avec, bvec, gam


def _relu2(acc):
    return acc, jnp.square(jnp.maximum(acc, 0.0))


def _drelu2(acc, up):
    return (acc * (2.0 * jnp.maximum(up, 0.0)),)


def _add(acc, e):
    return (acc + e,)


BIG_WEIGHTS = ("la_in", "la_out", "sg_in", "sg_out", "ffn_up0", "ffn_up1", "ffn_down0", "ffn_down1")
ROW_SHARDED = ("la_out", "sg_out", "ffn_down0", "ffn_down1")


class ExchangePlan:
    GATHERS = {"la_in_main": ("ffn_up0", "ffn_down0"), "ffn_up_0": ("sg_in",),
               "ffn_down_0": ("sg_out", "ffn_up1"), "sg_in": ("ffn_down1",)}
    REDUCES = {"ffn_dup_1": "ffn_down1", "ffn_dy_1": "ffn_up1", "sg_dus": "sg_out", "sg_dy": "sg_in",
               "ffn_dup_0": "ffn_down0", "ffn_dy_0": "ffn_up0", "la_docat": "la_out", "la_dy_main": "la_in"}

    def __init__(self, shard_halves, chip, c_idx, j_idx):
        self.shard_halves, self.chip, self.c_idx, self.j_idx = shard_halves, chip, c_idx, j_idx
        self.partial, self.finished = {}, {}

    def comm(self, carrier):
        if carrier in self.GATHERS:
            return gather_comm([self.shard_halves[w] for w in self.GATHERS[carrier]])
        if carrier in self.REDUCES:
            return chip_exchange_comm([self.partial[self.REDUCES[carrier]][1]])
        return None

    def done(self, carrier, outs, W):
        if carrier in self.GATHERS:
            for w, g in zip(self.GATHERS[carrier], outs):
                install_gathered(W, w, g, self.shard_halves[w], self.chip)
        else:
            w = self.REDUCES[carrier]
            self.finished[w] = chip_sum(self.partial[w][0], outs[0], self.j_idx, name=f"grads_chip_sum_{w}")

    def grad_ready(self, w, g):
        g = g.reshape(N_CHIPS, 2, g.shape[1] // 2, g.shape[2])
        sib = pair_exchange([g], name=f"grads_pair_exchange_{w}")[0]
        self.partial[w] = pair_sum(g, sib, self.c_idx, name=f"grads_pair_sum_{w}")


def install_gathered(W, w, g, own_halves, chip):
    g = lax.dynamic_update_slice(g, own_halves[None], (chip, 0, 0, 0))
    whole = g.reshape(N_CHIPS, g.shape[1] * g.shape[2], g.shape[3])
    if w in ROW_SHARDED:
        whole = whole.reshape(-1, whole.shape[-1])
    if w[:-1] in ("ffn_up", "ffn_down"):
        W[w[:-1]][int(w[-1])] = whole
    else:
        W[w] = whole


def _by_chip(w, g):
    return g.reshape(N_CHIPS, -1, g.shape[-1]) if w in ROW_SHARDED else g


def _la_in_grad_by_chip(g_main, g_gate, H):
    HD = H * HEAD_DIM
    nat = jnp.concatenate([g_main[:, :4 * HD], g_gate[:, :2 * H], g_main[:, 4 * HD:]], axis=1)
    return jnp.transpose(nat.reshape(nat.shape[0], N_CHIPS, -1), (1, 0, 2))


def _train_local(x2, tgt, W, plan=None):
    T, D = x2.shape
    H = W["a_log"].shape[0]
    nw = W["norm_w"]
    row = lambda v: v.reshape(1, -1).astype(F32)
    G = {}

    def mm(fn, *args, name, **kw):
        comm = plan.comm(name) if plan is not None else None
        if comm is None:
            return fn(*args, name=name, **kw)
        res, outs = fn(*args, name=name, comm=comm, **kw)
        plan.done(name, outs, W)
        return res

    def grad(w, g):
        G[w] = g
        if plan is not None:
            plan.grad_ready(w, _by_chip(w, g))

    def ffn_fwd(y, l):
        up, act = mm(mm_nn, y, W["ffn_up"][l], name=f"ffn_up_{l}", out_dtypes=(F32, BF), epilogue=_relu2)
        dn = mm(mm_nn, act, W["ffn_down"][l], name=f"ffn_down_{l}")
        return up, act, dn

    def ffn_bwd(y, up, act, ddn, l):
        grad(f"ffn_down{l}", mm_tn(act, ddn, name=f"ffn_dwdown_{l}"))
        dup = mm(mm_nt, ddn, W["ffn_down"][l], name=f"ffn_dup_{l}", out_dtypes=(BF,), epilogue=_drelu2, extras=(up,))
        grad(f"ffn_up{l}", mm_tn(y, dup, name=f"ffn_dwup_{l}", shards=N_CHIPS))
        return mm(mm_nt, dup, W["ffn_up"][l], name=f"ffn_dy_{l}")

    y0 = rms_fwd(x2, row(nw[0, 0]), name="norm00")
    pm = mm(mm_nn, y0, W["la_in_main"], name="la_in_main")
    pg = mm_nn(y0, W["la_in_gate"], name="la_in_gate")
    wc8 = jnp.pad(jnp.transpose(W["conv_w"]), ((0, SUBLANES - CONV_WIDTH), (0, 0)))
    lanes_pad = (H, LANES - 2 * H)
    arow = jnp.pad(W["a_log"], lanes_pad).reshape(1, LANES)
    dtrow = jnp.pad(W["dt_bias"], lanes_pad).reshape(1, LANES)
    bg, gcrow = gates_fwd(pg, arow, dtrow, H, name="gates_fwd")
    q = prep_fwd(pm, 0, wc8, 0, H, True, name="prep_q")
    k = prep_fwd(pm, H, wc8, H, H, True, name="prep_k")
    v = prep_fwd(pm, 2 * H, wc8, 2 * H, H, False, name="prep_v")
    wn = row(W["out_norm_w"])
    og_a, or_a, sall_a, tall = gdn_fwd(q, k, v, pm, 3 * H, bg, gcrow, wn, H, name="gdn_fwd")
    cs, sn = _rotary_tables(T)
    dmat, avec, bvec, gam = _retention_tables(H)
    og_b, or_b, sall_b = ret_fwd(pm, 4 * H, 5 * H, 6 * H, 7 * H, cs, sn, dmat, avec, bvec, gam, H, name="ret_fwd")
    ocat = jnp.concatenate([og_a, og_b], axis=1)
    mix = mm_nn(ocat, W["la_out"], name="la_out")
    h1, y2 = res_norm(x2, mix, row(nw[0, 1]), row(nw[0, 2]), name="resnorm_0a")
    up, act, dn = ffn_fwd(y2, 0)
    h2, y0b = res_norm(h1, dn, row(nw[0, 3]), row(nw[1, 0]), name="resnorm_0b")

    pre = mm(mm_nn, y0b, W["sg_in"], name="sg_in")
    lw, lb = row(W["ln_w"]), row(W["ln_b"])
    vn = ln_fwd(pre, lw, lb, name="sg_ln")
    ws = W["w_s"]
    bs3 = W["b_s"][:, :, None]
    us = sg_fwd(pre, vn, ws, bs3, name="sg_gate")
    mix1 = mm_nn(us, W["sg_out"], name="sg_out")
    h3, y2b = res_norm(h2, mix1, row(nw[1, 1]), row(nw[1, 2]), name="resnorm_1a")
    up1, act1, dn1 = ffn_fwd(y2b, 1)
    h4, _ = res_norm(h3, dn1, row(nw[1, 3]), None, name="resnorm_1b")
    dh4, lrow = loss_head(h4, tgt, name="loss_head")
    loss = lrow[0, 0]

    dnw = [[None] * 4 for _ in range(2)]
    ddn1, dnw[1][3] = rms_bwd(dn1, row(nw[1, 3]), dh4, None, name="dnorm13", out_dtype=BF)
    dy2b = ffn_bwd(y2b, up1, act1, ddn1, 1)
    dh3, dnw[1][2] = rms_bwd(h3, row(nw[1, 2]), dy2b, dh4, name="dnorm12", out_dtype=F32)
    dmix1, dnw[1][1] = rms_bwd(mix1, row(nw[1, 1]), dh3, None, name="dnorm11", out_dtype=BF)
    grad("sg_out", mm_tn(us, dmix1, name="sg_dwout"))
    dus = mm(mm_nt, dmix1, W["sg_out"], name="sg_dus")
    dpre_u, dvn, G["w_s"], dbs3 = sg_bwd(pre, vn, ws, bs3, dus, name="sg_gate_bwd")
    G["b_s"] = dbs3[:, :, 0]
    dpre_v, dlw, dlb = ln_bwd(pre, lw, dvn, name="sg_ln_bwd")
    G["ln_w"], G["ln_b"] = dlw[0], dlb[0]
    dpre = jnp.concatenate([dpre_u, dpre_v], axis=1)
    grad("sg_in", mm_tn(y0b, dpre, name="sg_dwin", shards=N_CHIPS))
    dy0b = mm(mm_nt, dpre, W["sg_in"], name="sg_dy")
    dh2, dnw[1][0] = rms_bwd(h2, row(nw[1, 0]), dy0b, dh3, name="dnorm10", out_dtype=F32)

    ddn, dnw[0][3] = rms_bwd(dn, row(nw[0, 3]), dh2, None, name="dnorm03", out_dtype=BF)
    dy2 = ffn_bwd(y2, up, act, ddn, 0)
    dh1, dnw[0][2] = rms_bwd(h1, row(nw[0, 2]), dy2, dh2, name="dnorm02", out_dtype=F32)
    dmix, dnw[0][1] = rms_bwd(mix, row(nw[0, 1]), dh1, None, name="dnorm01", out_dtype=BF)
    grad("la_out", mm_tn(ocat, dmix, name="la_dwout"))
    docat = mm(mm_nt, dmix, W["la_out"], name="la_docat")
    dq, dk, dv, dz, dbg, dwn = gdn_bwd(q, k, v, pm, 3 * H, bg, gcrow, wn, or_a, sall_a, tall, docat, H,
                                       name="gdn_bwd")
    drq, drk, drv, drg = ret_bwd(pm, 4 * H, 5 * H, 6 * H, 7 * H, cs, sn, dmat, avec, bvec, gam, or_b, sall_b,
                                 docat, H, H, name="ret_bwd")
    dpg, da, ddt = gates_bwd(pg, arow, dtrow, dbg, H, name="gates_bwd")
    dcq, dwq = prep_bwd_act(pm, 0, wc8, 0, H, True, dq, name="prep_dq")
    dck, dwk = prep_bwd_act(pm, H, wc8, H, H, True, dk, name="prep_dk")
    dcv, dwv = prep_bwd_act(pm, 2 * H, wc8, 2 * H, H, False, dv, name="prep_dv")
    dxq = prep_bwd_conv(dcq, wc8, 0, H, name="conv_dq")
    dxk = prep_bwd_conv(dck, wc8, H, H, name="conv_dk")
    dxv = prep_bwd_conv(dcv, wc8, 2 * H, H, name="conv_dv")
    dpm = jnp.concatenate([dxq, dxk, dxv, dz, drq, drk, drv, drg], axis=1)
    g_main = mm_tn(y0, dpm, name="la_dwin_main")
    g_gate = mm_tn(y0, dpg, name="la_dwin_gate")
    grad("la_in", _la_in_grad_by_chip(g_main, g_gate, H))
    dy0 = mm(mm_nt, dpm, W["la_in_main"], name="la_dy_main")
    dy0 = mm_nt(dpg, W["la_in_gate"], name="la_dy_gate", epilogue=_add, extras=(dy0,))
    dx, dnw[0][0] = rms_bwd(x2, row(nw[0, 0]), dy0, dh1, name="dnorm00", out_dtype=F32)

    G["norm_w"] = jnp.stack([jnp.concatenate(r, axis=0) for r in dnw], axis=0)
    G["conv_w"] = jnp.transpose(jnp.concatenate([dwq, dwk, dwv], axis=1)[:CONV_WIDTH])
    G["a_log"] = da[0, H:2 * H]
    G["dt_bias"] = ddt[0, H:2 * H]
    G["out_norm_w"] = dwn[0]
    return loss, dx, G


def _as2d(a):
    n = int(np.prod(a.shape))
    if a.shape[-1] < LANES and n % LANES == 0:
        return a.reshape(-1, LANES)
    return a.reshape(-1, a.shape[-1])


def _adamw_any(w, g, m, v, name):
    shp = w.shape
    d, m2, v2 = adamw(_as2d(w), _as2d(g.reshape(shp)), _as2d(m), _as2d(v), name=name)
    return g.reshape(shp), d.reshape(shp), m2.reshape(shp), v2.reshape(shp)


def kernel(x, norm_w, la_w_in, la_conv_w, la_a_log, la_dt_bias, la_out_norm_w, la_w_out, sg_w_in, sg_ln_w, sg_ln_b, sg_w_s, sg_b_s, sg_w_out, ffn_w_up, ffn_w_down, loss_target, m_norm_w, m_la_w_in, m_la_conv_w, m_la_a_log, m_la_dt_bias, m_la_out_norm_w, m_la_w_out, m_sg_w_in, m_sg_ln_w, m_sg_ln_b, m_sg_w_s, m_sg_b_s, m_sg_w_out, m_ffn_w_up, m_ffn_w_down, v_norm_w, v_la_w_in, v_la_conv_w, v_la_a_log, v_la_dt_bias, v_la_out_norm_w, v_la_w_out, v_sg_w_in, v_sg_ln_w, v_sg_ln_b, v_sg_w_s, v_sg_b_s, v_sg_w_out, v_ffn_w_up, v_ffn_w_down):
    weights = dict(norm_w=norm_w, la_w_in=la_w_in, la_conv_w=la_conv_w, la_a_log=la_a_log, la_dt_bias=la_dt_bias,
                   la_out_norm_w=la_out_norm_w, la_w_out=la_w_out, sg_w_in=sg_w_in, sg_ln_w=sg_ln_w,
                   sg_ln_b=sg_ln_b, sg_w_s=sg_w_s, sg_b_s=sg_b_s, sg_w_out=sg_w_out, ffn_w_up=ffn_w_up,
                   ffn_w_down=ffn_w_down)
    mom_m = dict(norm_w=m_norm_w, la_w_in=m_la_w_in, la_conv_w=m_la_conv_w, la_a_log=m_la_a_log,
                 la_dt_bias=m_la_dt_bias, la_out_norm_w=m_la_out_norm_w, la_w_out=m_la_w_out, sg_w_in=m_sg_w_in,
                 sg_ln_w=m_sg_ln_w, sg_ln_b=m_sg_ln_b, sg_w_s=m_sg_w_s, sg_b_s=m_sg_b_s, sg_w_out=m_sg_w_out,
                 ffn_w_up=m_ffn_w_up, ffn_w_down=m_ffn_w_down)
    mom_v = dict(norm_w=v_norm_w, la_w_in=v_la_w_in, la_conv_w=v_la_conv_w, la_a_log=v_la_a_log,
                 la_dt_bias=v_la_dt_bias, la_out_norm_w=v_la_out_norm_w, la_w_out=v_la_w_out, sg_w_in=v_sg_w_in,
                 sg_ln_w=v_sg_ln_w, sg_ln_b=v_sg_ln_b, sg_w_s=v_sg_w_s, sg_b_s=v_sg_b_s, sg_w_out=v_sg_w_out,
                 ffn_w_up=v_ffn_w_up, ffn_w_down=v_ffn_w_down)
    order = list(weights)

    T, D = x.shape[1], x.shape[2]
    H = la_a_log.shape[1]
    HD = H * HEAD_DIM
    xi, yi, ci = _place()
    chip = 2 * xi + yi
    c_idx = jnp.reshape(ci, (1,)).astype(jnp.int32)
    j_idx = jnp.reshape(chip, (1,)).astype(jnp.int32)

    shards = dict(la_in=la_w_in[0], la_out=la_w_out[0], sg_in=sg_w_in[0], sg_out=sg_w_out[0],
                  ffn_up0=ffn_w_up[0], ffn_up1=ffn_w_up[1], ffn_down0=ffn_w_down[0], ffn_down1=ffn_w_down[1])
    shard_halves = {w: _halves(a.astype(BF)) for w, a in shards.items()}
    small_shapes = [norm_w.shape, la_conv_w[0].shape, sg_ln_w[0].shape, sg_ln_b[0].shape]
    small = _pack_rows([norm_w, la_conv_w[0], sg_ln_w[0], sg_ln_b[0]])
    small = _halves(jnp.pad(small, ((0, (-small.shape[0]) % (2 * SUBLANES)), (0, 0))))
    first = [shard_halves["la_in"], shard_halves["la_out"], small]
    la_in_g, la_out_g, small_g = [
        lax.dynamic_update_slice(g, m[None], (chip, 0, 0, 0)).reshape(N_CHIPS, -1, g.shape[-1])
        for g, m in zip(run_comm(gather_comm(first), name="gather_first"), first)]
    pieces = [_unpack_rows(small_g[kk], small_shapes) for kk in range(N_CHIPS)]
    la_nat = jnp.transpose(la_in_g, (1, 0, 2)).reshape(D, -1)
    W = dict(
        norm_w=jnp.concatenate([p[0] for p in pieces], axis=-1),
        conv_w=jnp.concatenate([p[1] for p in pieces], axis=0),
        ln_w=jnp.concatenate([p[2] for p in pieces], axis=0),
        ln_b=jnp.concatenate([p[3] for p in pieces], axis=0),
        a_log=la_a_log[0], dt_bias=la_dt_bias[0], out_norm_w=la_out_norm_w[0], w_s=sg_w_s[0], b_s=sg_b_s[0],
        la_in_main=jnp.concatenate([la_nat[:, :4 * HD], la_nat[:, 4 * HD + 2 * H:]], axis=1),
        la_in_gate=jnp.pad(la_nat[:, 4 * HD:4 * HD + 2 * H], ((0, 0), (0, LANES - 2 * H))),
        la_out=la_out_g.reshape(-1, la_out_g.shape[-1]), ffn_up=[None, None], ffn_down=[None, None],
    )
    plan = ExchangePlan(shard_halves, chip, c_idx, j_idx)

    loss_local, dx, G = _train_local(x[0], loss_target[0], W, plan)
    loss = lax.psum(loss_local, ("x", "y", "c"))

    from_sib = dict(zip(BIG_WEIGHTS, pair_share([plan.finished[w] for w in BIG_WEIGHTS], name="grads_pair_share")))

    def big_update(key, w, m, v, tag):
        shp = w.shape
        r4 = adamw_halves(w.reshape(-1, shp[-1]), plan.finished[key], from_sib[key], m.reshape(-1, shp[-1]),
                          v.reshape(-1, shp[-1]), c_idx, name=f"adamw_{tag}")
        return [t.reshape(shp) for t in r4]

    big_res = dict(
        la_w_in=big_update("la_in", la_w_in, m_la_w_in, v_la_w_in, "la_w_in"),
        la_w_out=big_update("la_out", la_w_out, m_la_w_out, v_la_w_out, "la_w_out"),
        sg_w_in=big_update("sg_in", sg_w_in, m_sg_w_in, v_sg_w_in, "sg_w_in"),
        sg_w_out=big_update("sg_out", sg_w_out, m_sg_w_out, v_sg_w_out, "sg_w_out"),
    )
    for nm, key, w, m, v in (("ffn_w_up", "ffn_up", ffn_w_up, m_ffn_w_up, v_ffn_w_up),
                             ("ffn_w_down", "ffn_down", ffn_w_down, m_ffn_w_down, v_ffn_w_down)):
        per_layer = [big_update(f"{key}{l}", w[l], m[l], v[l], f"{nm}_{l}") for l in range(2)]
        big_res[nm] = [jnp.stack([per_layer[0][t], per_layer[1][t]]) for t in range(4)]

    small_names = ["norm_w", "conv_w", "ln_w", "ln_b", "a_log", "dt_bias", "out_norm_w", "w_s", "b_s"]
    small_full = [G[nm] for nm in small_names]
    summed = _unpack_rows(all_reduce_small(_pack_rows(small_full), name="grads_all_reduce_small"),
                          [g.shape for g in small_full])
    sm = dict(zip(small_names, summed))
    own = lambda full, axis: lax.dynamic_slice_in_dim(full, chip * (full.shape[axis] // N_CHIPS),
                                                      full.shape[axis] // N_CHIPS, axis)
    grads = dict(
        norm_w=own(sm["norm_w"], 2), la_conv_w=own(sm["conv_w"], 0), la_a_log=sm["a_log"],
        la_dt_bias=sm["dt_bias"], la_out_norm_w=sm["out_norm_w"],
        sg_ln_w=own(sm["ln_w"], 0), sg_ln_b=own(sm["ln_b"], 0), sg_w_s=sm["w_s"], sg_b_s=sm["b_s"],
    )

    res = {nm: big_res[nm] if nm in big_res else
           _adamw_any(weights[nm], grads[nm], mom_m[nm], mom_v[nm], f"adamw_{nm}") for nm in order}
    return (loss, dx.reshape(x.shape), *[res[nm][0] for nm in order], *[res[nm][1] for nm in order],
            *[res[nm][2] for nm in order], *[res[nm][3] for nm in order])
```

```python
import math
from typing import Callable, NamedTuple

import numpy as np
import jax
import jax.numpy as jnp
from jax import lax
from jax.experimental import pallas as pl
from jax.experimental.pallas import tpu as pltpu

F32 = jnp.float32
BF = jnp.bfloat16
HI = lax.Precision.HIGHEST

V7X_VMEM_BYTES = 64 * 1024 * 1024
VMEM_LIMIT = (V7X_VMEM_BYTES * 3) // 4
LANES = 128
SUBLANES = 8
BF16_ROWS = 16
HEAD_DIM = 128
LA_CHUNK = 64
SG_CHUNK = 128
CONV_WIDTH = 4
ROPE_BASE = 10000.0
EPS = 1e-6
L2_EPS = 1e-6
N_CHIPS = 4
N_DEV = 8

ADAM_LR = 0.001
ADAM_B1 = 0.9
ADAM_B2 = 0.999
ADAM_EPS = 1e-08
ADAM_WD = 0.01
ADAM_STEP = 10

MESH = pl.DeviceIdType.MESH
ANY = pl.BlockSpec(memory_space=pl.ANY)

NN = (((1,), (0,)), ((), ()))
NT = (((1,), (1,)), ((), ()))
TN = (((0,), (0,)), ((), ()))


def _pcall(body, **kw):
    return pl.pallas_call(body, **kw)


def _cp(n_axes):
    return pltpu.CompilerParams(dimension_semantics=("arbitrary",) * n_axes, vmem_limit_bytes=VMEM_LIMIT)


def _tile(n, pref, unit=LANES):
    if n <= pref:
        return n
    t = (pref // unit) * unit
    while t >= unit:
        if n % t == 0:
            return t
        t -= unit
    return n


def _dot(a, b, dims=NN):
    return lax.dot_general(a.astype(BF), b.astype(BF), dims, preferred_element_type=F32)


def _dot_hi(a, b, dims=NN):
    return lax.dot_general(a.astype(F32), b.astype(F32), dims, precision=HI, preferred_element_type=F32)


def _sigmoid(x):
    return 1.0 / (1.0 + jnp.exp(-x))


def _silu(x):
    return x * _sigmoid(x)


def _dsilu(x):
    s = _sigmoid(x)
    return s * (1.0 + x * (1.0 - s))


GELU_C = math.sqrt(2.0 / math.pi)
GELU_A = 0.044715


def _gelu(x):
    return 0.5 * x * (1.0 + jnp.tanh(GELU_C * (x + GELU_A * x * x * x)))


def _gelu_and_grad(x):
    t = jnp.tanh(GELU_C * (x + GELU_A * x * x * x))
    return 0.5 * x * (1.0 + t), 0.5 * (1.0 + t) + 0.5 * x * (1.0 - t * t) * GELU_C * (1.0 + 3.0 * GELU_A * x * x)


class Comm(NamedTuple):
    ins: list
    outs: list
    nsem: int
    start: Callable
    finish: Callable


def _matmul(a, b, *, dims, grid, a_spec, b_spec, out_shape, out_spec, acc_shape, name,
            epilogue=None, extras=(), extra_specs=(), comm=None):
    nk = grid[2]
    outs = tuple(out_shape) if isinstance(out_shape, (tuple, list)) else (out_shape,)
    out_specs = tuple(out_spec) if isinstance(out_spec, (tuple, list)) else (out_spec,)
    n_ex, n_out = len(extras), len(outs)
    n_ci = len(comm.ins) if comm else 0
    n_co = len(comm.outs) if comm else 0

    def body(*refs):
        a_ref, b_ref = refs[0], refs[1]
        ex = refs[2:2 + n_ex]
        ci = refs[2 + n_ex:2 + n_ex + n_ci]
        o = refs[2 + n_ex + n_ci:2 + n_ex + n_ci + n_out]
        co = refs[2 + n_ex + n_ci + n_out:2 + n_ex + n_ci + n_out + n_co]
        scratch = refs[2 + n_ex + n_ci + n_out + n_co:]
        i, j, k = pl.program_id(0), pl.program_id(1), pl.program_id(2)

        if comm:
            send, recv = scratch[-2], scratch[-1]

            @pl.when((i == 0) & (j == 0) & (k == 0))
            def _():
                comm.start(ci, co, send, recv)

        part = lax.dot_general(a_ref[...].astype(BF), b_ref[...].astype(BF), dims, preferred_element_type=F32)

        def finish(val):
            res = epilogue(val, *[e[...] for e in ex]) if epilogue is not None else (val,)
            for r, oref in zip(res, o):
                oref[...] = r.astype(oref.dtype)

        if nk == 1:
            finish(part)
        else:
            acc = scratch[0]

            @pl.when(k == 0)
            def _():
                acc[...] = part

            @pl.when(k > 0)
            def _():
                acc[...] += part

            @pl.when(k == nk - 1)
            def _():
                finish(acc[...])

        if comm:
            @pl.when((i == grid[0] - 1) & (j == grid[1] - 1) & (k == nk - 1))
            def _():
                comm.finish(ci, co, send, recv)

    scratch_shapes = [pltpu.VMEM(acc_shape, F32)] if nk > 1 else []
    if comm:
        scratch_shapes += [pltpu.SemaphoreType.DMA((comm.nsem,)), pltpu.SemaphoreType.DMA((comm.nsem,))]
    res = _pcall(
        body, name=name, grid=grid,
        in_specs=[a_spec, b_spec, *extra_specs, *[ANY] * n_ci],
        out_specs=[*out_specs, *[ANY] * n_co],
        out_shape=[*outs, *(comm.outs if comm else [])],
        scratch_shapes=scratch_shapes,
        compiler_params=_cp(3),
    )(a, b, *extras, *(comm.ins if comm else []))
    main = res[0] if n_out == 1 else list(res[:n_out])
    return (main, list(res[n_out:])) if comm else main


def mm_nn(a, w, *, name, out_dtypes=(F32,), epilogue=None, extras=(), comm=None, tm=1024, tn=1024, tk=2048):
    M, K = a.shape
    if w.ndim == 3:
        S, _, Ns = w.shape
        N = S * Ns
    else:
        S, Ns = 1, w.shape[1]
        N = Ns
    tm, tn, tk = _tile(M, tm), _tile(Ns, tn), _tile(K, tk)
    npb = Ns // tn
    grid = (M // tm, N // tn, K // tk)
    a_spec = pl.BlockSpec((tm, tk), lambda i, j, k: (i, k))
    if w.ndim == 3:
        b_spec = pl.BlockSpec((None, tk, tn), lambda i, j, k: (j // npb, k, j % npb))
    else:
        b_spec = pl.BlockSpec((tk, tn), lambda i, j, k: (k, j))
    o_spec = pl.BlockSpec((tm, tn), lambda i, j, k: (i, j))
    outs = tuple(jax.ShapeDtypeStruct((M, N), d) for d in out_dtypes)
    res = _matmul(a, w, dims=NN, grid=grid, a_spec=a_spec, b_spec=b_spec,
                  out_shape=outs, out_spec=(o_spec,) * len(outs), acc_shape=(tm, tn), name=name,
                  epilogue=epilogue, extras=extras, extra_specs=(o_spec,) * len(extras), comm=comm)
    return res


def mm_nt(a, w, *, name, out_dtypes=(F32,), epilogue=None, extras=(), comm=None, tm=1024, tn=1024, tk=2048):
    M, Kc = a.shape
    if w.ndim == 3:
        S, Nout, Ks = w.shape
    else:
        S, (Nout, Ks) = 1, w.shape
    assert S * Ks == Kc
    tm, tn, tk = _tile(M, tm), _tile(Nout, tn), _tile(Ks, tk)
    kpb = Ks // tk
    grid = (M // tm, Nout // tn, Kc // tk)
    a_spec = pl.BlockSpec((tm, tk), lambda i, j, k: (i, k))
    if w.ndim == 3:
        b_spec = pl.BlockSpec((None, tn, tk), lambda i, j, k: (k // kpb, j, k % kpb))
    else:
        b_spec = pl.BlockSpec((tn, tk), lambda i, j, k: (j, k))
    o_spec = pl.BlockSpec((tm, tn), lambda i, j, k: (i, j))
    outs = tuple(jax.ShapeDtypeStruct((M, Nout), d) for d in out_dtypes)
    return _matmul(a, w, dims=NT, grid=grid, a_spec=a_spec, b_spec=b_spec,
                   out_shape=outs, out_spec=(o_spec,) * len(outs), acc_shape=(tm, tn), name=name,
                   epilogue=epilogue, extras=extras, extra_specs=(o_spec,) * len(extras), comm=comm)


def mm_tn(x, dy, *, name, shards=1, bf16_copy=False, tm=1024, tn=1024, tk=2048):
    T, Kin = x.shape
    N = dy.shape[1]
    Ns = N // shards
    tm, tn, tk = _tile(Kin, tm), _tile(Ns, tn), _tile(T, tk)
    npb = Ns // tn
    grid = (Kin // tm, N // tn, T // tk)
    a_spec = pl.BlockSpec((tk, tm), lambda i, j, k: (k, i))
    b_spec = pl.BlockSpec((tk, tn), lambda i, j, k: (k, j))
    if shards > 1:
        o_spec = pl.BlockSpec((None, tm, tn), lambda i, j, k: (j // npb, i, j % npb))
        out = jax.ShapeDtypeStruct((shards, Kin, Ns), F32)
    else:
        o_spec = pl.BlockSpec((tm, tn), lambda i, j, k: (i, j))
        out = jax.ShapeDtypeStruct((Kin, N), F32)
    if bf16_copy:
        return _matmul(x, dy, dims=TN, grid=grid, a_spec=a_spec, b_spec=b_spec,
                       out_shape=(out, jax.ShapeDtypeStruct(out.shape, BF)), out_spec=(o_spec, o_spec),
                       acc_shape=(tm, tn), name=name, epilogue=lambda acc: (acc, acc))
    return _matmul(x, dy, dims=TN, grid=grid, a_spec=a_spec, b_spec=b_spec,
                   out_shape=out, out_spec=o_spec, acc_shape=(tm, tn), name=name)


ROW_TILE = 256


def _rows(tr, d):
    return pl.BlockSpec((tr, d), lambda i: (i, 0))


def _fixed(shape):
    nd = len(shape)
    return pl.BlockSpec(shape, lambda *_: (0,) * nd)


def _rms(xv, w):
    r = lax.rsqrt(jnp.mean(xv * xv, axis=-1, keepdims=True) + EPS)
    return xv * r * w


def rms_fwd(x, w, *, name):
    T, D = x.shape
    tr = _tile(T, ROW_TILE, SUBLANES)

    def body(x_ref, w_ref, y_ref):
        y_ref[...] = _rms(x_ref[...], w_ref[...]).astype(y_ref.dtype)

    return _pcall(body, name=name, grid=(T // tr,), in_specs=[_rows(tr, D), _fixed((1, D))],
                  out_specs=_rows(tr, D), out_shape=jax.ShapeDtypeStruct((T, D), BF), compiler_params=_cp(1))(x, w)


def res_norm(h, m, wa, wb, *, name):
    T, D = h.shape
    tr = _tile(T, ROW_TILE, SUBLANES)
    second = wb is not None

    def body(*refs):
        if second:
            h_ref, m_ref, wa_ref, wb_ref, ho_ref, y_ref = refs
        else:
            h_ref, m_ref, wa_ref, ho_ref = refs
        ho = h_ref[...] + _rms(m_ref[...], wa_ref[...])
        ho_ref[...] = ho
        if second:
            y_ref[...] = _rms(ho, wb_ref[...]).astype(y_ref.dtype)

    ins = [h, m, wa] + ([wb] if second else [])
    in_specs = [_rows(tr, D), _rows(tr, D), _fixed((1, D))] + ([_fixed((1, D))] if second else [])
    out_shape = [jax.ShapeDtypeStruct((T, D), F32)] + ([jax.ShapeDtypeStruct((T, D), BF)] if second else [])
    out_specs = [_rows(tr, D)] * len(out_shape)
    res = _pcall(body, name=name, grid=(T // tr,), in_specs=in_specs, out_specs=out_specs,
                 out_shape=out_shape, compiler_params=_cp(1))(*ins)
    return tuple(res) if second else (res[0], None)


def rms_bwd(x, w, dy, dres, *, name, out_dtype):
    T, D = x.shape
    tr = _tile(T, ROW_TILE, SUBLANES)
    has_res = dres is not None

    def body(*refs):
        if has_res:
            x_ref, w_ref, dy_ref, dr_ref, dx_ref, dw_ref = refs
        else:
            x_ref, w_ref, dy_ref, dx_ref, dw_ref = refs
        i = pl.program_id(0)
        xv = x_ref[...]
        r = lax.rsqrt(jnp.mean(xv * xv, axis=-1, keepdims=True) + EPS)
        xh = xv * r
        dyv = dy_ref[...].astype(F32)
        dyw = dyv * w_ref[...]
        dx = r * (dyw - xh * jnp.mean(dyw * xh, axis=-1, keepdims=True))
        if has_res:
            dx = dx + dr_ref[...].astype(F32)
        dx_ref[...] = dx.astype(dx_ref.dtype)
        part = jnp.sum(dyv * xh, axis=0, keepdims=True)

        @pl.when(i == 0)
        def _():
            dw_ref[...] = part

        @pl.when(i > 0)
        def _():
            dw_ref[...] += part

    ins = [x, w, dy] + ([dres] if has_res else [])
    in_specs = [_rows(tr, D), _fixed((1, D)), _rows(tr, D)] + ([_rows(tr, D)] if has_res else [])
    return _pcall(body, name=name, grid=(T // tr,), in_specs=in_specs,
                  out_specs=[_rows(tr, D), _fixed((1, D))],
                  out_shape=[jax.ShapeDtypeStruct((T, D), out_dtype), jax.ShapeDtypeStruct((1, D), F32)],
                  compiler_params=_cp(1))(*ins)


def loss_head(h, tgt, *, name):
    T, D = h.shape
    tr = _tile(T, ROW_TILE, SUBLANES)

    def body(h_ref, t_ref, dh_ref, l_ref):
        i = pl.program_id(0)
        e = h_ref[...] - t_ref[...]
        dh_ref[...] = e * (1.0 / D)
        part = 0.5 * jnp.sum(jnp.mean(e * e, axis=-1, keepdims=True), axis=0, keepdims=True)
        part = jnp.broadcast_to(part, (1, LANES))

        @pl.when(i == 0)
        def _():
            l_ref[...] = part

        @pl.when(i > 0)
        def _():
            l_ref[...] += part

    return _pcall(body, name=name, grid=(T // tr,), in_specs=[_rows(tr, D), _rows(tr, D)],
                  out_specs=[_rows(tr, D), _fixed((1, LANES))],
                  out_shape=[jax.ShapeDtypeStruct((T, D), F32), jax.ShapeDtypeStruct((1, LANES), F32)],
                  compiler_params=_cp(1))(h, tgt)


def adamw(w, g, m, v, *, name):
    R, C = w.shape
    tr = _tile(R, max(SUBLANES, (1 << 18) // C), SUBLANES)
    c1 = 1.0 - ADAM_B1 ** ADAM_STEP
    c2 = 1.0 - ADAM_B2 ** ADAM_STEP

    def body(w_ref, g_ref, m_ref, v_ref, d_ref, mo_ref, vo_ref):
        gv = g_ref[...]
        m2 = ADAM_B1 * m_ref[...] + (1.0 - ADAM_B1) * gv
        v2 = ADAM_B2 * v_ref[...] + (1.0 - ADAM_B2) * (gv * gv)
        d_ref[...] = -ADAM_LR * ((m2 / c1) / (jnp.sqrt(v2 / c2) + ADAM_EPS) + ADAM_WD * w_ref[...])
        mo_ref[...] = m2
        vo_ref[...] = v2

    spec = _rows(tr, C)
    sds = jax.ShapeDtypeStruct((R, C), F32)
    return _pcall(body, name=name, grid=(R // tr,), in_specs=[spec] * 4, out_specs=[spec] * 3,
                  out_shape=[sds] * 3, compiler_params=_cp(1))(w, g, m, v)


HALO = SUBLANES


def _conv_down(xx, w_ref):
    acc = xx * w_ref[pl.ds(CONV_WIDTH - 1, 1), :]
    for d in range(1, CONV_WIDTH):
        acc = acc + pltpu.roll(xx, d, 0) * w_ref[pl.ds(CONV_WIDTH - 1 - d, 1), :]
    return acc


def _conv_tile(x_ref, halo_ref, w_ref, first):
    xs = x_ref[...]
    hal = jnp.where(first, 0.0, halo_ref[...])
    cat = jnp.concatenate([hal, xs[0:HALO]], axis=0)
    return jnp.concatenate([_conv_down(cat, w_ref)[HALO:2 * HALO], _conv_down(xs, w_ref)[HALO:]], axis=0)


def _shift_down_tile(x_ref, halo_ref, first, d):
    xs = x_ref[...]
    if d == 0:
        return xs
    hal = jnp.where(first, 0.0, halo_ref[...])
    cat = jnp.concatenate([hal, xs[0:HALO]], axis=0)
    return jnp.concatenate([pltpu.roll(cat, d, 0)[HALO:2 * HALO], pltpu.roll(xs, d, 0)[HALO:]], axis=0)


def _l2n(s):
    return s * lax.rsqrt(jnp.sum(s * s, axis=-1, keepdims=True) + L2_EPS)


PREP_ROWS = 512


def _l2n_groups(s, nb):
    return jnp.concatenate([_l2n(s[:, g * LANES:(g + 1) * LANES]) for g in range(nb)], axis=1)


def prep_fwd(pm, off, wc8, woff, nblk, l2, *, name):
    T = pm.shape[0]
    tr = _tile(T, PREP_ROWS, SUBLANES)
    hb = tr // HALO
    wb = _heads_per_step(nblk)
    wl = wb * LANES

    def body(x_ref, halo_ref, w_ref, o_ref):
        i = pl.program_id(0)
        s = _silu(_conv_tile(x_ref, halo_ref, w_ref, i == 0))
        o_ref[...] = _l2n_groups(s, wb) if l2 else s

    return _pcall(
        body, name=name, grid=(T // tr, nblk // wb),
        in_specs=[pl.BlockSpec((tr, wl), lambda i, c: (i, off // wb + c)),
                  pl.BlockSpec((HALO, wl), lambda i, c: (jnp.maximum(i * hb - 1, 0), off // wb + c)),
                  pl.BlockSpec((SUBLANES, wl), lambda i, c: (0, woff // wb + c))],
        out_specs=pl.BlockSpec((tr, wl), lambda i, c: (i, c)),
        out_shape=jax.ShapeDtypeStruct((T, nblk * LANES), F32), compiler_params=_cp(2))(pm, pm, wc8)


def prep_bwd_act(pm, off, wc8, woff, nblk, l2, dout, *, name):
    T = pm.shape[0]
    tr = _tile(T, PREP_ROWS, SUBLANES)
    hb = tr // HALO
    wb = _heads_per_step(nblk)
    wl = wb * LANES

    def l2_bwd(s, do):
        r = lax.rsqrt(jnp.sum(s * s, axis=-1, keepdims=True) + L2_EPS)
        nrm = s * r
        return r * (do - nrm * jnp.sum(do * nrm, axis=-1, keepdims=True))

    def body(x_ref, halo_ref, w_ref, do_ref, dc_ref, dw_ref):
        i = pl.program_id(1)
        first = i == 0
        y = _conv_tile(x_ref, halo_ref, w_ref, first)
        s = _silu(y)
        do = do_ref[...]
        if l2:
            ds = jnp.concatenate([l2_bwd(s[:, g * LANES:(g + 1) * LANES], do[:, g * LANES:(g + 1) * LANES])
                                  for g in range(wb)], axis=1)
        else:
            ds = do
        dc = ds * _dsilu(y)
        dc_ref[...] = dc

        @pl.when(first)
        def _():
            dw_ref[...] = jnp.zeros_like(dw_ref)

        for j in range(CONV_WIDTH):
            xsh = _shift_down_tile(x_ref, halo_ref, first, CONV_WIDTH - 1 - j)
            dw_ref[pl.ds(j, 1), :] += jnp.sum(dc * xsh, axis=0, keepdims=True)

    return _pcall(
        body, name=name, grid=(nblk // wb, T // tr),
        in_specs=[pl.BlockSpec((tr, wl), lambda c, i: (i, off // wb + c)),
                  pl.BlockSpec((HALO, wl), lambda c, i: (jnp.maximum(i * hb - 1, 0), off // wb + c)),
                  pl.BlockSpec((SUBLANES, wl), lambda c, i: (0, woff // wb + c)),
                  pl.BlockSpec((tr, wl), lambda c, i: (i, c))],
        out_specs=[pl.BlockSpec((tr, wl), lambda c, i: (i, c)),
                   pl.BlockSpec((SUBLANES, wl), lambda c, i: (0, c))],
        out_shape=[jax.ShapeDtypeStruct((T, nblk * LANES), F32),
                   jax.ShapeDtypeStruct((SUBLANES, nblk * LANES), F32)],
        compiler_params=_cp(2))(pm, pm, wc8, dout)


def prep_bwd_conv(dc, wc8, woff, nblk, *, name):
    T = dc.shape[0]
    tr = _tile(T, PREP_ROWS, SUBLANES)
    hb = tr // HALO
    nt = T // tr
    last_halo = T // HALO - 1
    wb = _heads_per_step(nblk)
    wl = wb * LANES

    def up(xx, w_ref):
        rows = xx.shape[0]
        acc = xx * w_ref[pl.ds(CONV_WIDTH - 1, 1), :]
        for d in range(1, CONV_WIDTH):
            acc = acc + pltpu.roll(xx, rows - d, 0) * w_ref[pl.ds(CONV_WIDTH - 1 - d, 1), :]
        return acc

    def body(x_ref, halo_ref, w_ref, o_ref):
        i = pl.program_id(0)
        xs = x_ref[...]
        hal = jnp.where(i == nt - 1, 0.0, halo_ref[...])
        cat = jnp.concatenate([xs[tr - HALO:tr], hal], axis=0)
        out = jnp.concatenate([up(xs, w_ref)[:tr - HALO], up(cat, w_ref)[0:HALO]], axis=0)
        o_ref[...] = out.astype(o_ref.dtype)

    return _pcall(
        body, name=name, grid=(nt, nblk // wb),
        in_specs=[pl.BlockSpec((tr, wl), lambda i, c: (i, c)),
                  pl.BlockSpec((HALO, wl), lambda i, c: (jnp.minimum((i + 1) * hb, last_halo), c)),
                  pl.BlockSpec((SUBLANES, wl), lambda i, c: (0, woff // wb + c))],
        out_specs=pl.BlockSpec((tr, wl), lambda i, c: (i, c)),
        out_shape=jax.ShapeDtypeStruct((T, nblk * LANES), BF), compiler_params=_cp(2))(dc, dc, wc8)


def _softplus(x):
    return jnp.maximum(x, 0.0) + jnp.log(1.0 + jnp.exp(-jnp.abs(x)))


def _tril_ones(c):
    t = lax.broadcasted_iota(jnp.int32, (c, c), 0)
    s = lax.broadcasted_iota(jnp.int32, (c, c), 1)
    return (t >= s).astype(F32)


def _triu_ones(c):
    t = lax.broadcasted_iota(jnp.int32, (c, c), 0)
    s = lax.broadcasted_iota(jnp.int32, (c, c), 1)
    return (t <= s).astype(F32)


def gates_fwd(pg, arow, dtrow, H, *, name):
    T = pg.shape[0]
    C = LA_CHUNK
    N = T // C

    def body(x_ref, a_ref, dt_ref, bg_ref, gr_ref):
        x = x_ref[...]
        lane = lax.broadcasted_iota(jnp.int32, (C, LANES), 1)
        g = -jnp.exp(a_ref[...]) * _softplus(x + dt_ref[...])
        g = jnp.where((lane >= H) & (lane < 2 * H), g, 0.0)
        lm = _tril_ones(C)
        gc = _dot_hi(lm, g)
        bg_ref[...] = jnp.where(lane < H, _sigmoid(x), gc)
        gr_ref[...] = _dot_hi(g, _triu_ones(C), TN)

    return _pcall(
        body, name=name, grid=(N,),
        in_specs=[pl.BlockSpec((C, LANES), lambda n: (n, 0)), _fixed((1, LANES)), _fixed((1, LANES))],
        out_specs=[pl.BlockSpec((C, LANES), lambda n: (n, 0)), pl.BlockSpec((None, LANES, C), lambda n: (n, 0, 0))],
        out_shape=[jax.ShapeDtypeStruct((T, LANES), F32), jax.ShapeDtypeStruct((N, LANES, C), F32)],
        compiler_params=_cp(1))(pg, arow, dtrow)


def gates_bwd(pg, arow, dtrow, dbg, H, *, name):
    T = pg.shape[0]
    C = LA_CHUNK
    N = T // C

    def body(x_ref, a_ref, dt_ref, d_ref, dx_ref, da_ref, ddt_ref):
        n = pl.program_id(0)
        x = x_ref[...]
        d = d_ref[...]
        lane = lax.broadcasted_iota(jnp.int32, (C, LANES), 1)
        in_g = (lane >= H) & (lane < 2 * H)
        e = jnp.exp(a_ref[...])
        xs = x + dt_ref[...]
        g = -e * _softplus(xs)
        dg = _dot_hi(_tril_ones(C), jnp.where(in_g, d, 0.0), TN)
        dxs = jnp.where(in_g, dg * (-e) * _sigmoid(xs), 0.0)
        beta = _sigmoid(x)
        dx_ref[...] = jnp.where(lane < H, d * beta * (1.0 - beta), dxs).astype(dx_ref.dtype)
        pa = jnp.sum(jnp.where(in_g, dg * g, 0.0), axis=0, keepdims=True)
        pd = jnp.sum(dxs, axis=0, keepdims=True)

        @pl.when(n == 0)
        def _():
            da_ref[...] = pa
            ddt_ref[...] = pd

        @pl.when(n > 0)
        def _():
            da_ref[...] += pa
            ddt_ref[...] += pd

    return _pcall(
        body, name=name, grid=(N,),
        in_specs=[pl.BlockSpec((C, LANES), lambda n: (n, 0)), _fixed((1, LANES)), _fixed((1, LANES)),
                  pl.BlockSpec((C, LANES), lambda n: (n, 0))],
        out_specs=[pl.BlockSpec((C, LANES), lambda n: (n, 0)), _fixed((1, LANES)), _fixed((1, LANES))],
        out_shape=[jax.ShapeDtypeStruct((T, LANES), BF), jax.ShapeDtypeStruct((1, LANES), F32),
                   jax.ShapeDtypeStruct((1, LANES), F32)],
        compiler_params=_cp(1))(pg, arow, dtrow, dbg)


QK_SCALE = HEAD_DIM ** -0.5


HEADS_PER_STEP = 8


def _heads_per_step(H):
    hb = HEADS_PER_STEP
    while H % hb:
        hb //= 2
    return hb


def _head_rstd(o):
    return lax.rsqrt(jnp.mean(o * o, axis=-1, keepdims=True) + EPS)


def _gdn_gates(bg_ref, gr_ref, h, H):
    C = LA_CHUNK
    bgv = bg_ref[...]
    lane = lax.broadcasted_iota(jnp.int32, (C, LANES), 1)
    beta = jnp.sum(jnp.where(lane == h, bgv, 0.0), axis=1, keepdims=True)
    gc = jnp.sum(jnp.where(lane == H + h, bgv, 0.0), axis=1, keepdims=True)
    grow = gr_ref[pl.ds(H + h, 1), :]
    ri = lax.broadcasted_iota(jnp.int32, (C, 1), 0)
    gl = jnp.sum(jnp.where(ri == C - 1, gc, 0.0), axis=0, keepdims=True)
    return beta, gc, grow, gl


def _chunk_masks():
    C = LA_CHUNK
    ti = lax.broadcasted_iota(jnp.int32, (C, C), 0)
    si = lax.broadcasted_iota(jnp.int32, (C, C), 1)
    return ti >= si, ti > si, ti == si


def _decay(gc, grow, causal):
    return jnp.where(causal, jnp.exp(jnp.where(causal, gc - grow, 0.0)), 0.0)


def _interleave(gens):
    gens = list(gens)
    results = [None] * len(gens)
    live = list(range(len(gens)))
    while live:
        still = []
        for i in live:
            try:
                next(gens[i])
                still.append(i)
            except StopIteration as stop:
                results[i] = stop.value
        live = still
    return results


def _unit_lower_inverse(a, eye):
    x = -a
    p = jnp.where(eye, 1.0, 0.0) + x
    for _ in range(5):
        x = _dot_hi(x, x)
        yield
        p = p + _dot_hi(p, x)
        yield
    return p


def gdn_fwd(q, k, v, pm, zoff, bg, gcrow, wn, H, *, name):
    T = q.shape[0]
    C = LA_CHUNK
    N = T // C
    hd = HEAD_DIM

    HB = _heads_per_step(H)

    def body(q_ref, k_ref, v_ref, z_ref, bg_ref, gr_ref, wn_ref, og_ref, or_ref, sall_ref, tall_ref, S):
        n = pl.program_id(0)
        hg = pl.program_id(1)
        causal, strict, eye = _chunk_masks()

        @pl.when((n == 0) & (hg == 0))
        def _():
            S[...] = jnp.zeros_like(S)

        states = [S[hg * HB + i] for i in range(HB)]

        def head(i):
            h = hg * HB + i
            sl = slice(i * hd, (i + 1) * hd)
            beta, gc, grow, gl = _gdn_gates(bg_ref, gr_ref, h, H)
            dm = _decay(gc, grow, causal)
            qs = q_ref[:, sl] * QK_SCALE
            kk = k_ref[:, sl]
            vv = v_ref[:, sl]
            eg = jnp.exp(gc)
            kb = kk * beta
            a = jnp.where(strict, _dot(kb, kk, NT) * dm, 0.0)
            yield
            tm = yield from _unit_lower_inverse(a, eye)
            u = _dot(tm, vv * beta)
            w = _dot(tm, kb * eg)
            qk = jnp.where(causal, _dot(qs, kk, NT) * dm, 0.0)
            yield
            s0 = states[i]
            vnew = u - _dot(w, s0)
            o = _dot(qs * eg, s0)
            yield
            o = o + _dot(qk, vnew)
            s1 = s0 * jnp.exp(gl) + _dot(kk * jnp.exp(gl - gc), vnew, TN)
            yield
            sall_ref[i] = s0
            tall_ref[i] = tm
            or_ref[:, sl] = o
            og_ref[:, sl] = (o * _head_rstd(o) * wn_ref[...] * _silu(z_ref[:, sl])).astype(og_ref.dtype)
            return s1

        for i, s1 in enumerate(_interleave([head(i) for i in range(HB)])):
            S[hg * HB + i] = s1

    blk = lambda off: pl.BlockSpec((C, HB * hd), lambda n, h: (n, off // HB + h))
    return _pcall(
        body, name=name, grid=(N, H // HB),
        in_specs=[blk(0), blk(0), blk(0), blk(zoff),
                  pl.BlockSpec((C, LANES), lambda n, h: (n, 0)),
                  pl.BlockSpec((None, LANES, C), lambda n, h: (n, 0, 0)),
                  _fixed((1, hd))],
        out_specs=[blk(0), blk(0),
                   pl.BlockSpec((None, HB, hd, hd), lambda n, h: (n, h, 0, 0)),
                   pl.BlockSpec((None, HB, C, C), lambda n, h: (n, h, 0, 0))],
        out_shape=[jax.ShapeDtypeStruct((T, H * hd), BF), jax.ShapeDtypeStruct((T, H * hd), F32),
                   jax.ShapeDtypeStruct((N, H, hd, hd), F32), jax.ShapeDtypeStruct((N, H, C, C), F32)],
        scratch_shapes=[pltpu.VMEM((H, hd, hd), F32)],
        compiler_params=_cp(2))(q, k, v, pm, bg, gcrow, wn)


def gdn_bwd(q, k, v, pm, zoff, bg, gcrow, wn, oraw, sall, tall, dog, H, *, name):
    T = q.shape[0]
    C = LA_CHUNK
    N = T // C
    hd = HEAD_DIM

    HB = _heads_per_step(H)

    def body(*refs):
        dbg_ref, dwn_ref, dS = refs[15], refs[16], refs[17]
        n = pl.program_id(0)
        hg = pl.program_id(1)

        @pl.when((n == 0) & (hg == 0))
        def _():
            dwn_ref[...] = jnp.zeros_like(dwn_ref)
            dS[...] = jnp.zeros_like(dS)

        @pl.when(hg == 0)
        def _():
            dbg_ref[...] = jnp.zeros_like(dbg_ref)

        ds_in = [dS[hg * HB + i] for i in range(HB)]
        outs = _interleave([head(i, hg * HB + i, ds_in[i], *refs) for i in range(HB)])
        for i in range(HB):
            dS[hg * HB + i] = outs[i][0]
        dwn_ref[...] += sum(o[1] for o in outs)
        dbg_ref[...] += sum(o[2] for o in outs)

    def head(i, h, ds1, q_ref, k_ref, v_ref, z_ref, bg_ref, gr_ref, wn_ref, or_ref, sall_ref, tall_ref, dog_ref,
             dq_ref, dk_ref, dv_ref, dz_ref, dbg_ref, dwn_ref, dS):
        sl = slice(i * hd, (i + 1) * hd)
        beta, gc, grow, gl = _gdn_gates(bg_ref, gr_ref, h, H)
        causal, strict, eye = _chunk_masks()
        dm = _decay(gc, grow, causal)
        qs = q_ref[:, sl] * QK_SCALE
        kk = k_ref[:, sl]
        vv = v_ref[:, sl]
        zz = z_ref[:, sl]
        o = or_ref[:, sl]
        dog = dog_ref[:, sl]
        wn_v = wn_ref[...]
        s0 = sall_ref[i]
        tm = tall_ref[i]

        rstd = _head_rstd(o)
        on = o * rstd
        sz = _silu(zz)
        don = dog * wn_v * sz
        dwn_part = jnp.sum(dog * on * sz, axis=0, keepdims=True)
        dz_ref[:, sl] = (dog * on * wn_v * _dsilu(zz)).astype(dz_ref.dtype)
        do = rstd * (don - on * jnp.mean(don * on, axis=-1, keepdims=True))

        eg = jnp.exp(gc)
        kb = kk * beta
        vb = vv * beta
        kbg = kb * eg
        a = jnp.where(strict, _dot(kb, kk, NT) * dm, 0.0)
        u = _dot(tm, vb)
        w = _dot(tm, kbg)
        qk = jnp.where(causal, _dot(qs, kk, NT) * dm, 0.0)
        dqdec = _dot(do, s0, NT)
        yield
        vnew = u - _dot(w, s0)
        qdec = qs * eg
        etail = jnp.exp(gl - gc)
        ktail = kk * etail
        egl = jnp.exp(gl)
        dvnew = _dot(qk, do, TN) + _dot(ktail, ds1)
        yield
        dqk = jnp.where(causal, _dot(do, vnew, NT), 0.0)
        dktail = _dot(vnew, ds1, NT)
        dcd = jnp.sum(jnp.sum(s0 * ds1, axis=1, keepdims=True), axis=0, keepdims=True)
        ds0 = egl * ds1 + _dot(qdec, do, TN) - _dot(w, dvnew, TN)
        dw = -_dot(dvnew, s0, NT)
        dvb = _dot(tm, dvnew, TN)
        yield
        dkbg = _dot(tm, dw, TN)
        dtm = _dot(dvnew, vb, NT) + _dot(dw, kbg, NT)
        dqkr = dqk * dm
        dqs = _dot(dqkr, kk) + dqdec * eg
        yield
        x = _dot_hi(tm, dtm, TN)
        yield
        da = jnp.where(strict, -_dot_hi(x, tm, NT), 0.0)
        yield
        dkk = da * dm
        dkb = _dot(dkk, kk) + dkbg * eg
        dk = _dot(dkk, kb, TN)
        dk = dk + _dot(dqkr, qs, TN) + dktail * etail + dkb * beta
        g = da * a + dqk * qk
        colsum = jnp.max(_dot_hi(g, jnp.ones((C, LANES), F32), TN), axis=1, keepdims=True)
        yield
        rk = jnp.sum(dktail * ktail, axis=1, keepdims=True)
        dgc = (jnp.sum(g, axis=1, keepdims=True) - colsum
               + jnp.sum(dqdec * qdec, axis=1, keepdims=True) - rk
               + jnp.sum(dkbg * kbg, axis=1, keepdims=True))
        dgl = jnp.sum(rk, axis=0, keepdims=True) + dcd * egl
        ri = lax.broadcasted_iota(jnp.int32, (C, 1), 0)
        dgc = dgc + jnp.where(ri == C - 1, dgl, 0.0)
        dbeta = jnp.sum(dkb * kk, axis=1, keepdims=True) + jnp.sum(dvb * vv, axis=1, keepdims=True)

        dq_ref[:, sl] = dqs * QK_SCALE
        dk_ref[:, sl] = dk
        dv_ref[:, sl] = dvb * beta
        lane = lax.broadcasted_iota(jnp.int32, (C, LANES), 1)
        return ds0, dwn_part, jnp.where(lane == h, dbeta, 0.0) + jnp.where(lane == H + h, dgc, 0.0)

    blk = lambda off: pl.BlockSpec((C, HB * hd), lambda n, h: (N - 1 - n, off // HB + h))
    st = lambda r: pl.BlockSpec((None, HB, r, r), lambda n, h: (N - 1 - n, h, 0, 0))
    return _pcall(
        body, name=name, grid=(N, H // HB),
        in_specs=[blk(0), blk(0), blk(0), blk(zoff),
                  pl.BlockSpec((C, LANES), lambda n, h: (N - 1 - n, 0)),
                  pl.BlockSpec((None, LANES, C), lambda n, h: (N - 1 - n, 0, 0)),
                  _fixed((1, hd)), blk(0), st(hd), st(C), blk(0)],
        out_specs=[blk(0), blk(0), blk(0), blk(0),
                   pl.BlockSpec((C, LANES), lambda n, h: (N - 1 - n, 0)), _fixed((1, hd))],
        out_shape=[jax.ShapeDtypeStruct((T, H * hd), F32)] * 3
        + [jax.ShapeDtypeStruct((T, H * hd), BF), jax.ShapeDtypeStruct((T, LANES), F32),
           jax.ShapeDtypeStruct((1, hd), F32)],
        scratch_shapes=[pltpu.VMEM((H, hd, hd), F32)],
        compiler_params=_cp(2))(q, k, v, pm, bg, gcrow, wn, oraw, sall, tall, dog)


def _rot(x, cs, sn):
    return x * cs + pltpu.roll(x, HEAD_DIM // 2, 1) * sn


def _rot_t(dy, cs, sn):
    return dy * cs + pltpu.roll(dy * sn, HEAD_DIM // 2, 1)


def ret_fwd(pm, qoff, koff, voff, goff, cs, sn, dmat, avec, bvec, gam, H, *, name):
    T = pm.shape[0]
    C = LA_CHUNK
    N = T // C
    hd = HEAD_DIM

    HB = _heads_per_step(H)

    def body(q_ref, k_ref, v_ref, g_ref, cs_ref, sn_ref, dm_ref, a_ref, b_ref, gam_ref,
             og_ref, or_ref, sall_ref, S):
        n = pl.program_id(0)
        hg = pl.program_id(1)
        c, s = cs_ref[...], sn_ref[...]

        @pl.when((n == 0) & (hg == 0))
        def _():
            S[...] = jnp.zeros_like(S)

        states = [S[hg * HB + i] for i in range(HB)]

        def head(i):
            sl = slice(i * hd, (i + 1) * hd)
            qq = _rot(q_ref[:, sl], c, s)
            kk = _rot(k_ref[:, sl], c, s) * QK_SCALE
            vv = v_ref[:, sl]
            s0 = states[i]
            p = _dot(qq, kk, NT) * dm_ref[i]
            cross = _dot(qq * a_ref[i], s0)
            s1 = s0 * gam_ref[i] + _dot(kk * b_ref[i], vv, TN)
            yield
            o = _dot(p, vv) + cross
            yield
            sall_ref[i] = s0
            or_ref[:, sl] = o
            og_ref[:, sl] = (_silu(g_ref[:, sl]) * o * _head_rstd(o)).astype(og_ref.dtype)
            return s1

        for i, s1 in enumerate(_interleave([head(i) for i in range(HB)])):
            S[hg * HB + i] = s1

    blk = lambda off: pl.BlockSpec((C, HB * hd), lambda n, h: (n, off // HB + h))
    tab = pl.BlockSpec((C, hd), lambda n, h: (n, 0))
    per_h = lambda r, cdim: pl.BlockSpec((HB, r, cdim), lambda n, h: (h, 0, 0))
    return _pcall(
        body, name=name, grid=(N, H // HB),
        in_specs=[blk(qoff), blk(koff), blk(voff), blk(goff), tab, tab,
                  per_h(C, C), per_h(C, hd), per_h(C, hd), per_h(1, hd)],
        out_specs=[blk(0), blk(0), pl.BlockSpec((None, HB, hd, hd), lambda n, h: (n, h, 0, 0))],
        out_shape=[jax.ShapeDtypeStruct((T, H * hd), BF), jax.ShapeDtypeStruct((T, H * hd), F32),
                   jax.ShapeDtypeStruct((N, H, hd, hd), F32)],
        scratch_shapes=[pltpu.VMEM((H, hd, hd), F32)],
        compiler_params=_cp(2))(pm, pm, pm, pm, cs, sn, dmat, avec, bvec, gam)


def ret_bwd(pm, qoff, koff, voff, goff, cs, sn, dmat, avec, bvec, gam, oraw, sall, dog, dogoff, H, *, name):
    T = pm.shape[0]
    C = LA_CHUNK
    N = T // C
    hd = HEAD_DIM

    HB = _heads_per_step(H)

    def body(q_ref, k_ref, v_ref, g_ref, cs_ref, sn_ref, dm_ref, a_ref, b_ref, gam_ref, or_ref, sall_ref,
             dog_ref, dq_ref, dk_ref, dv_ref, dg_ref, dS):
        n = pl.program_id(0)
        hg = pl.program_id(1)
        c, s = cs_ref[...], sn_ref[...]

        @pl.when((n == 0) & (hg == 0))
        def _():
            dS[...] = jnp.zeros_like(dS)

        dstates = [dS[hg * HB + i] for i in range(HB)]

        def head(i):
            sl = slice(i * hd, (i + 1) * hd)
            qq = _rot(q_ref[:, sl], c, s)
            kk = _rot(k_ref[:, sl], c, s) * QK_SCALE
            vv = v_ref[:, sl]
            gg = g_ref[:, sl]
            o = or_ref[:, sl]
            dog = dog_ref[:, sl]
            dm = dm_ref[i]
            av, bv = a_ref[i], b_ref[i]
            s0 = sall_ref[i]
            ds1 = dstates[i]

            rstd = _head_rstd(o)
            on = o * rstd
            don = dog * _silu(gg)
            dg_ref[:, sl] = (dog * on * _dsilu(gg)).astype(dg_ref.dtype)
            do = rstd * (don - on * jnp.mean(don * on, axis=-1, keepdims=True))

            p = _dot(qq, kk, NT) * dm
            dp = _dot(do, vv, NT) * dm
            cross_q = _dot(do, s0, NT) * av
            cross_k = _dot(vv, ds1, NT) * bv
            cross_v = _dot(kk * bv, ds1)
            ds0 = ds1 * gam_ref[i] + _dot(qq * av, do, TN)
            yield
            dv_ref[:, sl] = (_dot(p, do, TN) + cross_v).astype(dv_ref.dtype)
            dqq = _dot(dp, kk) + cross_q
            dkk = (_dot(dp, qq, TN) + cross_k) * QK_SCALE
            yield
            dq_ref[:, sl] = _rot_t(dqq, c, s).astype(dq_ref.dtype)
            dk_ref[:, sl] = _rot_t(dkk, c, s).astype(dk_ref.dtype)
            return ds0

        for i, ds0 in enumerate(_interleave([head(i) for i in range(HB)])):
            dS[hg * HB + i] = ds0

    blk = lambda off: pl.BlockSpec((C, HB * hd), lambda n, h: (N - 1 - n, off // HB + h))
    tab = pl.BlockSpec((C, hd), lambda n, h: (N - 1 - n, 0))
    per_h = lambda r, cdim: pl.BlockSpec((HB, r, cdim), lambda n, h: (h, 0, 0))
    return _pcall(
        body, name=name, grid=(N, H // HB),
        in_specs=[blk(qoff), blk(koff), blk(voff), blk(goff), tab, tab,
                  per_h(C, C), per_h(C, hd), per_h(C, hd), per_h(1, hd), blk(0),
                  pl.BlockSpec((None, HB, hd, hd), lambda n, h: (N - 1 - n, h, 0, 0)), blk(dogoff)],
        out_specs=[blk(0)] * 4,
        out_shape=[jax.ShapeDtypeStruct((T, H * hd), BF)] * 4,
        scratch_shapes=[pltpu.VMEM((H, hd, hd), F32)],
        compiler_params=_cp(2))(pm, pm, pm, pm, cs, sn, dmat, avec, bvec, gam, oraw, sall, dog)


LN_ROWS = 128


def ln_fwd(pre, lw, lb, *, name):
    T, W2 = pre.shape
    W = W2 // 2
    tr = _tile(T, LN_ROWS, SUBLANES)

    def body(p_ref, w_ref, b_ref, o_ref):
        v = _gelu(p_ref[...])
        xc = v - jnp.mean(v, axis=-1, keepdims=True)
        r = lax.rsqrt(jnp.mean(xc * xc, axis=-1, keepdims=True) + EPS)
        o_ref[...] = xc * r * w_ref[...] + b_ref[...]

    return _pcall(body, name=name, grid=(T // tr,),
                  in_specs=[pl.BlockSpec((tr, W), lambda i: (i, 1)), _fixed((1, W)), _fixed((1, W))],
                  out_specs=_rows(tr, W), out_shape=jax.ShapeDtypeStruct((T, W), F32),
                  compiler_params=_cp(1))(pre, lw, lb)


def ln_bwd(pre, lw, dvn, *, name):
    T, W2 = pre.shape
    W = W2 // 2
    tr = _tile(T, LN_ROWS, SUBLANES)

    def body(p_ref, w_ref, d_ref, dp_ref, dw_ref, db_ref):
        i = pl.program_id(0)
        v, dgelu = _gelu_and_grad(p_ref[...])
        xc = v - jnp.mean(v, axis=-1, keepdims=True)
        r = lax.rsqrt(jnp.mean(xc * xc, axis=-1, keepdims=True) + EPS)
        xh = xc * r
        d = d_ref[...]
        dxh = d * w_ref[...]
        dv = r * (dxh - jnp.mean(dxh, axis=-1, keepdims=True) - xh * jnp.mean(dxh * xh, axis=-1, keepdims=True))
        dp_ref[...] = (dv * dgelu).astype(dp_ref.dtype)
        pw = jnp.sum(d * xh, axis=0, keepdims=True)
        pb = jnp.sum(d, axis=0, keepdims=True)

        @pl.when(i == 0)
        def _():
            dw_ref[...] = pw
            db_ref[...] = pb

        @pl.when(i > 0)
        def _():
            dw_ref[...] += pw
            db_ref[...] += pb

    return _pcall(body, name=name, grid=(T // tr,),
                  in_specs=[pl.BlockSpec((tr, W), lambda i: (i, 1)), _fixed((1, W)), _rows(tr, W)],
                  out_specs=[_rows(tr, W), _fixed((1, W)), _fixed((1, W))],
                  out_shape=[jax.ShapeDtypeStruct((T, W), BF), jax.ShapeDtypeStruct((1, W), F32),
                             jax.ShapeDtypeStruct((1, W), F32)],
                  compiler_params=_cp(1))(pre, lw, dvn)


def _tril_mask(c):
    t = lax.broadcasted_iota(jnp.int32, (c, c), 0)
    s = lax.broadcasted_iota(jnp.int32, (c, c), 1)
    return t >= s


def sg_fwd(pre, vn, ws, bs3, *, name):
    T, W = vn.shape
    G = ws.shape[0]
    gd = W // G
    C = SG_CHUNK

    def body(p_ref, v_ref, w_ref, b_ref, o_ref):
        mask = _tril_mask(C)
        for g in range(G):
            sl = slice(g * gd, (g + 1) * gd)
            wm = jnp.where(mask, w_ref[g], 0.0)
            s = _dot(wm, v_ref[:, sl]) + b_ref[g]
            o_ref[:, sl] = (_gelu(p_ref[:, sl]) * s).astype(o_ref.dtype)

    blk = pl.BlockSpec((C, W), lambda n: (n, 0))
    return _pcall(body, name=name, grid=(T // C,),
                  in_specs=[blk, blk, _fixed((G, C, C)), _fixed((G, C, 1))],
                  out_specs=blk, out_shape=jax.ShapeDtypeStruct((T, W), BF),
                  compiler_params=_cp(1))(pre, vn, ws, bs3)


def sg_bwd(pre, vn, ws, bs3, dus, *, name):
    T, W = vn.shape
    G = ws.shape[0]
    gd = W // G
    C = SG_CHUNK

    def body(p_ref, v_ref, w_ref, b_ref, d_ref, dp_ref, dv_ref, dw_ref, db_ref):
        n = pl.program_id(0)
        mask = _tril_mask(C)

        @pl.when(n == 0)
        def _():
            dw_ref[...] = jnp.zeros_like(dw_ref)
            db_ref[...] = jnp.zeros_like(db_ref)

        for g in range(G):
            sl = slice(g * gd, (g + 1) * gd)
            wm = jnp.where(mask, w_ref[g], 0.0)
            u, du = _gelu_and_grad(p_ref[:, sl])
            vv = v_ref[:, sl]
            d = d_ref[:, sl]
            s = _dot(wm, vv) + b_ref[g]
            ds = d * u
            dp_ref[:, sl] = (d * s * du).astype(dp_ref.dtype)
            dv_ref[:, sl] = _dot(wm, ds, TN)
            dw_ref[g] += jnp.where(mask, _dot(ds, vv, NT), 0.0)
            db_ref[g] += jnp.sum(ds, axis=1, keepdims=True)

    blk = pl.BlockSpec((C, W), lambda n: (n, 0))
    return _pcall(body, name=name, grid=(T // C,),
                  in_specs=[blk, blk, _fixed((G, C, C)), _fixed((G, C, 1)), blk],
                  out_specs=[blk, blk, _fixed((G, C, C)), _fixed((G, C, 1))],
                  out_shape=[jax.ShapeDtypeStruct((T, W), BF), jax.ShapeDtypeStruct((T, W), F32),
                             jax.ShapeDtypeStruct((G, C, C), F32), jax.ShapeDtypeStruct((G, C, 1), F32)],
                  compiler_params=_cp(1))(pre, vn, ws, bs3, dus)


CHIP_RELATIONS = ((1, 0), (0, 1), (1, 1))


def _place():
    return lax.axis_index("x"), lax.axis_index("y"), lax.axis_index("c")


def _peer_chip(x, y, r):
    fx, fy = CHIP_RELATIONS[r]
    return (1 - x if fx else x), (1 - y if fy else y)


def gather_comm(arrs):
    n = len(arrs)
    per = 2 * len(CHIP_RELATIONS) + 1
    own = per - 1

    def ici(a, r, ins, outs, send, recv):
        x, y, c = _place()
        px, py = _peer_chip(x, y, r)
        return pltpu.make_async_remote_copy(
            src_ref=ins[a].at[c], dst_ref=outs[a].at[2 * x + y, c], send_sem=send.at[a * per + r],
            recv_sem=recv.at[a * per + r], device_id=(px, py, c), device_id_type=MESH)

    def own_block(a, ins, outs, send, recv):
        x, y, c = _place()
        return pltpu.make_async_remote_copy(
            src_ref=ins[a], dst_ref=outs[a].at[2 * x + y], send_sem=send.at[a * per + own],
            recv_sem=recv.at[a * per + own], device_id=(x, y, 1 - c), device_id_type=MESH)

    def start(ins, outs, send, recv):
        for a in range(n):
            for r in range(3):
                ici(a, r, ins, outs, send, recv).start()
            own_block(a, ins, outs, send, recv).start()

    def finish(ins, outs, send, recv):
        x, y, c = _place()
        sib = (x, y, 1 - c)
        forwards = []
        for a in range(n):
            for r in range(3):
                px, py = _peer_chip(x, y, r)
                landed = outs[a].at[2 * px + py, c]
                pltpu.make_async_remote_copy(
                    src_ref=landed, dst_ref=landed, send_sem=send.at[a * per + r],
                    recv_sem=recv.at[a * per + r], device_id=(px, py, c), device_id_type=MESH).wait_recv()
                fw = pltpu.make_async_remote_copy(
                    src_ref=landed, dst_ref=landed, send_sem=send.at[a * per + 3 + r],
                    recv_sem=recv.at[a * per + 3 + r], device_id=sib, device_id_type=MESH)
                fw.start()
                forwards.append(fw)
        for a in range(n):
            for r in range(3):
                px, py = _peer_chip(x, y, r)
                other = outs[a].at[2 * px + py, 1 - c]
                pltpu.make_async_remote_copy(
                    src_ref=other, dst_ref=other, send_sem=send.at[a * per + 3 + r],
                    recv_sem=recv.at[a * per + 3 + r], device_id=sib, device_id_type=MESH).wait_recv()
        for a in range(n):
            for r in range(3):
                ici(a, r, ins, outs, send, recv).wait_send()
            own_block(a, ins, outs, send, recv).wait()
        for fw in forwards:
            fw.wait_send()

    outs = [jax.ShapeDtypeStruct((N_CHIPS,) + a.shape, a.dtype) for a in arrs]
    return Comm(list(arrs), outs, n * per, start, finish)


def chip_exchange_comm(ps):
    n = len(ps)

    def copies(ins, outs, send, recv):
        x, y, c = _place()
        cps = []
        for a in range(n):
            for r in range(3):
                px, py = _peer_chip(x, y, r)
                cps.append(pltpu.make_async_remote_copy(
                    src_ref=ins[a].at[2 * px + py], dst_ref=outs[a].at[r], send_sem=send.at[3 * a + r],
                    recv_sem=recv.at[3 * a + r], device_id=(px, py, c), device_id_type=MESH))
        return cps

    def start(ins, outs, send, recv):
        for cp in copies(ins, outs, send, recv):
            cp.start()

    def finish(ins, outs, send, recv):
        for cp in copies(ins, outs, send, recv):
            cp.wait()

    outs = [jax.ShapeDtypeStruct((3,) + p.shape[1:], p.dtype) for p in ps]
    return Comm(list(ps), outs, 3 * n, start, finish)


def run_comm(comm, *, name):
    n_i, n_o = len(comm.ins), len(comm.outs)

    def body(*refs):
        ins, outs = refs[:n_i], refs[n_i:n_i + n_o]
        send, recv = refs[n_i + n_o:]
        comm.start(ins, outs, send, recv)
        comm.finish(ins, outs, send, recv)

    res = _pcall(body, name=name, in_specs=[ANY] * n_i, out_specs=[ANY] * n_o, out_shape=comm.outs,
                 scratch_shapes=[pltpu.SemaphoreType.DMA((comm.nsem,)), pltpu.SemaphoreType.DMA((comm.nsem,))])(*comm.ins)
    return list(res)


def pair_exchange(gs, *, name):
    n = len(gs)

    def body(*refs):
        ins, outs = refs[:n], refs[n:2 * n]
        send, recv = refs[2 * n:2 * n + 2]
        x, y, c = _place()
        cps = []
        for a in range(n):
            cp = pltpu.make_async_remote_copy(
                src_ref=ins[a].at[:, pl.ds(1 - c, 1)], dst_ref=outs[a], send_sem=send.at[a], recv_sem=recv.at[a],
                device_id=(x, y, 1 - c), device_id_type=MESH)
            cp.start()
            cps.append(cp)
        for cp in cps:
            cp.wait()

    out_shape = [jax.ShapeDtypeStruct((g.shape[0], 1) + g.shape[2:], g.dtype) for g in gs]
    res = _pcall(body, name=name, in_specs=[ANY] * n, out_specs=[ANY] * n, out_shape=out_shape,
                 scratch_shapes=[pltpu.SemaphoreType.DMA((n,)), pltpu.SemaphoreType.DMA((n,))])(*gs)
    return list(res)


def pair_share(fs, *, name):
    n = len(fs)

    def body(*refs):
        ins, outs = refs[:n], refs[n:2 * n]
        send, recv = refs[2 * n:2 * n + 2]
        x, y, c = _place()
        cps = []
        for a in range(n):
            cp = pltpu.make_async_remote_copy(
                src_ref=ins[a], dst_ref=outs[a], send_sem=send.at[a], recv_sem=recv.at[a],
                device_id=(x, y, 1 - c), device_id_type=MESH)
            cp.start()
            cps.append(cp)
        for cp in cps:
            cp.wait()

    out_shape = [jax.ShapeDtypeStruct(f.shape, f.dtype) for f in fs]
    res = _pcall(body, name=name, in_specs=[ANY] * n, out_specs=[ANY] * n, out_shape=out_shape,
                 scratch_shapes=[pltpu.SemaphoreType.DMA((n,)), pltpu.SemaphoreType.DMA((n,))])(*fs)
    return list(res)


def all_reduce_small(v, *, name):
    R = v.shape[0]

    def body(v_ref, sum_ref, gat_ref, send, recv):
        x, y, c = _place()
        me = 4 * x + 2 * y + c
        gat_ref[me] = v_ref[...]
        cps = []
        peers = []
        for r in range(1, N_DEV):
            fx, fy, fc = (r >> 2) & 1, (r >> 1) & 1, r & 1
            px, py, pc = (1 - x if fx else x), (1 - y if fy else y), (1 - c if fc else c)
            peers.append((px, py, pc))
            cp = pltpu.make_async_remote_copy(
                src_ref=v_ref, dst_ref=gat_ref.at[me], send_sem=send.at[r - 1], recv_sem=recv.at[r - 1],
                device_id=(px, py, pc), device_id_type=MESH)
            cp.start()
            cps.append(cp)
        for r in range(1, N_DEV):
            px, py, pc = peers[r - 1]
            slot = gat_ref.at[4 * px + 2 * py + pc]
            pltpu.make_async_remote_copy(
                src_ref=v_ref, dst_ref=slot, send_sem=send.at[r - 1], recv_sem=recv.at[r - 1],
                device_id=(px, py, pc), device_id_type=MESH).wait_recv()
        for cp in cps:
            cp.wait_send()
        acc = gat_ref[0]
        for s in range(1, N_DEV):
            acc = acc + gat_ref[s]
        sum_ref[...] = acc

    vm = pl.BlockSpec(memory_space=pltpu.VMEM)
    res = _pcall(body, name=name, in_specs=[vm], out_specs=[vm, vm],
                 out_shape=[jax.ShapeDtypeStruct((R, LANES), F32), jax.ShapeDtypeStruct((N_DEV, R, LANES), F32)],
                 scratch_shapes=[pltpu.SemaphoreType.DMA((N_DEV - 1,)), pltpu.SemaphoreType.DMA((N_DEV - 1,))],
                 compiler_params=pltpu.CompilerParams(vmem_limit_bytes=VMEM_LIMIT))(v)
    return res[0]


def pair_sum(g, r1, c_idx, *, name):
    nb, _, hr, C = g.shape
    tr = _tile(hr, max(BF16_ROWS, (1 << 18) // C), BF16_ROWS)

    def body(c_ref, g_ref, r_ref, o_ref, ob_ref):
        s = g_ref[...] + r_ref[...].astype(F32)
        o_ref[...] = s
        ob_ref[...] = s.astype(ob_ref.dtype)

    out = pl.BlockSpec((None, tr, C), lambda b, i, cr: (b, i, 0))
    gs = pltpu.PrefetchScalarGridSpec(
        num_scalar_prefetch=1, grid=(nb, hr // tr),
        in_specs=[pl.BlockSpec((None, None, tr, C), lambda b, i, cr: (b, cr[0], i, 0)),
                  pl.BlockSpec((None, None, tr, C), lambda b, i, cr: (b, 0, i, 0))],
        out_specs=[out, out])
    return _pcall(body, name=name, grid_spec=gs,
                  out_shape=[jax.ShapeDtypeStruct((nb, hr, C), F32), jax.ShapeDtypeStruct((nb, hr, C), BF)],
                  compiler_params=_cp(2))(c_idx, g, r1)


def chip_sum(p, r2, j_idx, *, name, layer=0, n_layers=1, into=None):
    _, hr, C = p.shape
    tr = _tile(hr, max(BF16_ROWS, (1 << 18) // C), BF16_ROWS)

    def body(j_ref, p_ref, a_ref, b_ref, c_ref, *rest):
        o_ref = rest[-1]
        o_ref[...] = ((p_ref[...] + a_ref[...].astype(F32)) + b_ref[...].astype(F32)) + c_ref[...].astype(F32)

    rel = lambda r: pl.BlockSpec((None, tr, C), lambda i, jr: (r, i, 0))
    in_specs = [pl.BlockSpec((None, tr, C), lambda i, jr: (jr[0], i, 0)), rel(0), rel(1), rel(2)]
    operands = [j_idx, p, r2, r2, r2]
    aliases = {}
    if into is not None:
        in_specs.append(ANY)
        operands.append(into)
        aliases = {len(operands) - 1: 0}
    gs = pltpu.PrefetchScalarGridSpec(
        num_scalar_prefetch=1, grid=(hr // tr,), in_specs=in_specs,
        out_specs=pl.BlockSpec((None, tr, C), lambda i, jr: (layer, i, 0)))
    return _pcall(body, name=name, grid_spec=gs, out_shape=jax.ShapeDtypeStruct((n_layers, hr, C), F32),
                  input_output_aliases=aliases, compiler_params=_cp(1))(*operands)


def adamw_halves(w, g_mine, g_other, m, v, c_idx, *, name):
    L, R, C = w.shape
    hr = R // 2
    tr = _tile(hr, max(SUBLANES, (1 << 18) // C), SUBLANES)
    nbh = hr // tr
    c1 = 1.0 - ADAM_B1 ** ADAM_STEP
    c2 = 1.0 - ADAM_B2 ** ADAM_STEP

    def body(c_ref, w_ref, gm_ref, go_ref, m_ref, v_ref, g_ref, d_ref, mo_ref, vo_ref):
        i = pl.program_id(1)
        gv = jnp.where(i // nbh == c_ref[0], gm_ref[...], go_ref[...])
        m2 = ADAM_B1 * m_ref[...] + (1.0 - ADAM_B1) * gv
        v2 = ADAM_B2 * v_ref[...] + (1.0 - ADAM_B2) * (gv * gv)
        d_ref[...] = -ADAM_LR * ((m2 / c1) / (jnp.sqrt(v2 / c2) + ADAM_EPS) + ADAM_WD * w_ref[...])
        g_ref[...] = gv
        mo_ref[...] = m2
        vo_ref[...] = v2

    full = pl.BlockSpec((None, tr, C), lambda l, i, cr: (l, i, 0))
    half = pl.BlockSpec((None, tr, C), lambda l, i, cr: (l, i % nbh, 0))
    gs = pltpu.PrefetchScalarGridSpec(
        num_scalar_prefetch=1, grid=(L, R // tr),
        in_specs=[full, half, half, full, full], out_specs=[full] * 4)
    sds = jax.ShapeDtypeStruct((L, R, C), F32)
    return _pcall(body, name=name, grid_spec=gs, out_shape=[sds] * 4,
                  compiler_params=_cp(2))(c_idx, w, g_mine, g_other, m, v)


def _pack_rows(arrs):
    parts = []
    for a in arrs:
        flat = a.reshape(-1).astype(F32)
        tile = SUBLANES * LANES
        pad = (-flat.shape[0]) % tile
        parts.append(jnp.pad(flat, (0, pad)).reshape(-1, LANES))
    return jnp.concatenate(parts, axis=0)


def _unpack_rows(buf, shapes):
    out, row = [], 0
    for shp in shapes:
        size = int(np.prod(shp))
        rows = -(-size // (SUBLANES * LANES)) * SUBLANES
        out.append(buf[row:row + rows].reshape(-1)[:size].reshape(shp))
        row += rows
    return out


def _halves(a2d):
    r, c = a2d.shape
    return a2d.reshape(2, r // 2, c)


def _rotary_tables(T):
    half = HEAD_DIM // 2
    pos = jnp.arange(T, dtype=F32)
    inv_freq = 1.0 / (ROPE_BASE ** jnp.linspace(0.0, 1.0, half, dtype=F32))
    ang = pos[:, None] * inv_freq[None, :]
    cos, sin = jnp.cos(ang), jnp.sin(ang)
    return jnp.concatenate([cos, cos], axis=1), jnp.concatenate([-sin, sin], axis=1)


def _retention_tables(H):
    C = LA_CHUNK
    lg = jnp.log1p(-jnp.power(2.0, -5.0 - jnp.arange(H, dtype=F32)))
    pos = jnp.arange(C, dtype=F32)
    causal = jnp.tril(jnp.ones((C, C), dtype=bool))
    dmat = jnp.exp(jnp.where(causal, (pos[:, None] - pos[None, :]) * lg[:, None, None], -jnp.inf))
    bc = lambda t: jnp.broadcast_to(t[..., None], t.shape + (HEAD_DIM,))
    avec = bc(jnp.exp((pos + 1.0)[None, :] * lg[:, None]))
    bvec = bc(jnp.exp((C - 1.0 - pos)[None, :] * lg[:, None]))
    gam = bc(jnp.exp(C * lg)[:, None])
    return dmat, avec, bvec, gam


def _relu2(acc):
    return acc, jnp.square(jnp.maximum(acc, 0.0))


def _drelu2(acc, up):
    return (acc * (2.0 * jnp.maximum(up, 0.0)),)


def _add(acc, e):
    return (acc + e,)


ROW_SHARDED = ("la_out", "sg_out", "ffn_down0", "ffn_down1")


class ExchangePlan:
    GATHERS = {"la_in_main": ("ffn_up0", "ffn_down0"), "ffn_up_0": ("sg_in",),
               "ffn_down_0": ("sg_out", "ffn_up1"), "sg_in": ("ffn_down1",)}
    REDUCES = {"ffn_dup_1": "ffn_down1", "ffn_dy_1": "ffn_up1", "sg_dus": "sg_out", "sg_dy": "sg_in",
               "ffn_dup_0": "ffn_down0", "ffn_dy_0": "ffn_up0", "la_docat": "la_out", "la_dy_main": "la_in"}

    def __init__(self, shard_halves, c_idx, j_idx):
        self.shard_halves, self.c_idx, self.j_idx = shard_halves, c_idx, j_idx
        self.partial = {}
        self.finished = {}

    def comm(self, carrier):
        if carrier in self.GATHERS:
            return gather_comm([self.shard_halves[w] for w in self.GATHERS[carrier]])
        if carrier in self.REDUCES:
            return chip_exchange_comm([self.partial[self.REDUCES[carrier]][1]])
        return None

    def done(self, carrier, outs, W):
        if carrier in self.GATHERS:
            for w, g in zip(self.GATHERS[carrier], outs):
                install_gathered(W, w, g)
        else:
            w = self.REDUCES[carrier]
            per_layer = w[:-1] in ("ffn_up", "ffn_down")
            key, layer, n_layers = (w[:-1], int(w[-1]), 2) if per_layer else (w, 0, 1)
            self.finished[key] = chip_sum(self.partial[w][0], outs[0], self.j_idx, name=f"grads_chip_sum_{w}",
                                          layer=layer, n_layers=n_layers, into=self.finished.get(key))

    def grad_ready(self, w, g, payload=None):
        halves = lambda t: t.reshape(N_CHIPS, 2, t.shape[1] // 2, t.shape[2])
        sib = pair_exchange([halves(g if payload is None else payload)], name=f"grads_pair_exchange_{w}")[0]
        self.partial[w] = pair_sum(halves(g), sib, self.c_idx, name=f"grads_pair_sum_{w}")


def install_gathered(W, w, g):
    whole = g.reshape(N_CHIPS, g.shape[1] * g.shape[2], g.shape[3])
    if w in ROW_SHARDED:
        whole = whole.reshape(-1, whole.shape[-1])
    if w[:-1] in ("ffn_up", "ffn_down"):
        W[w[:-1]][int(w[-1])] = whole
    else:
        W[w] = whole


def _by_chip(w, g):
    return g.reshape(N_CHIPS, -1, g.shape[-1]) if w in ROW_SHARDED else g


def _la_in_grad_by_chip(g_main, g_gate, H):
    HD = H * HEAD_DIM
    nat = jnp.concatenate([g_main[:, :4 * HD], g_gate[:, :2 * H], g_main[:, 4 * HD:]], axis=1)
    return jnp.transpose(nat.reshape(nat.shape[0], N_CHIPS, -1), (1, 0, 2))


def _train_local(x2, tgt, W, plan=None):
    T, D = x2.shape
    H = W["a_log"].shape[0]
    nw = W["norm_w"]
    row = lambda v: v.reshape(1, -1).astype(F32)
    G = {}

    def mm(fn, *args, name, **kw):
        comm = plan.comm(name) if plan is not None else None
        if comm is None:
            return fn(*args, name=name, **kw)
        res, outs = fn(*args, name=name, comm=comm, **kw)
        plan.done(name, outs, W)
        return res

    def grad(w, g):
        payload = None
        if isinstance(g, (list, tuple)):
            g, payload = g
        G[w] = g
        if plan is not None:
            plan.grad_ready(w, _by_chip(w, g), None if payload is None else _by_chip(w, payload))

    def ffn_fwd(y, l):
        up, act = mm(mm_nn, y, W["ffn_up"][l], name=f"ffn_up_{l}", out_dtypes=(F32, BF), epilogue=_relu2)
        dn = mm(mm_nn, act, W["ffn_down"][l], name=f"ffn_down_{l}")
        return up, act, dn

    def ffn_bwd(y, up, act, ddn, l):
        grad(f"ffn_down{l}", mm_tn(act, ddn, name=f"ffn_dwdown_{l}", bf16_copy=True))
        dup = mm(mm_nt, ddn, W["ffn_down"][l], name=f"ffn_dup_{l}", out_dtypes=(BF,), epilogue=_drelu2, extras=(up,))
        grad(f"ffn_up{l}", mm_tn(y, dup, name=f"ffn_dwup_{l}", shards=N_CHIPS, bf16_copy=True))
        return mm(mm_nt, dup, W["ffn_up"][l], name=f"ffn_dy_{l}")

    y0 = rms_fwd(x2, row(nw[0, 0]), name="norm00")
    pm = mm(mm_nn, y0, W["la_in_main"], name="la_in_main")
    pg = mm_nn(y0, W["la_in_gate"], name="la_in_gate")
    wc8 = jnp.pad(jnp.transpose(W["conv_w"]), ((0, SUBLANES - CONV_WIDTH), (0, 0)))
    lanes_pad = (H, LANES - 2 * H)
    arow = jnp.pad(W["a_log"], lanes_pad).reshape(1, LANES)
    dtrow = jnp.pad(W["dt_bias"], lanes_pad).reshape(1, LANES)
    bg, gcrow = gates_fwd(pg, arow, dtrow, H, name="gates_fwd")
    q = prep_fwd(pm, 0, wc8, 0, H, True, name="prep_q")
    k = prep_fwd(pm, H, wc8, H, H, True, name="prep_k")
    v = prep_fwd(pm, 2 * H, wc8, 2 * H, H, False, name="prep_v")
    wn = row(W["out_norm_w"])
    og_a, or_a, sall_a, tall = gdn_fwd(q, k, v, pm, 3 * H, bg, gcrow, wn, H, name="gdn_fwd")
    cs, sn = _rotary_tables(T)
    dmat, avec, bvec, gam = _retention_tables(H)
    og_b, or_b, sall_b = ret_fwd(pm, 4 * H, 5 * H, 6 * H, 7 * H, cs, sn, dmat, avec, bvec, gam, H, name="ret_fwd")
    ocat = jnp.concatenate([og_a, og_b], axis=1)
    mix = mm_nn(ocat, W["la_out"], name="la_out")
    h1, y2 = res_norm(x2, mix, row(nw[0, 1]), row(nw[0, 2]), name="resnorm_0a")
    up, act, dn = ffn_fwd(y2, 0)
    h2, y0b = res_norm(h1, dn, row(nw[0, 3]), row(nw[1, 0]), name="resnorm_0b")

    pre = mm(mm_nn, y0b, W["sg_in"], name="sg_in")
    lw, lb = row(W["ln_w"]), row(W["ln_b"])
    vn = ln_fwd(pre, lw, lb, name="sg_ln")
    ws = W["w_s"]
    bs3 = W["b_s"][:, :, None]
    us = sg_fwd(pre, vn, ws, bs3, name="sg_gate")
    mix1 = mm_nn(us, W["sg_out"], name="sg_out")
    h3, y2b = res_norm(h2, mix1, row(nw[1, 1]), row(nw[1, 2]), name="resnorm_1a")
    up1, act1, dn1 = ffn_fwd(y2b, 1)
    h4, _ = res_norm(h3, dn1, row(nw[1, 3]), None, name="resnorm_1b")
    dh4, lrow = loss_head(h4, tgt, name="loss_head")
    loss = lrow[0, 0]

    dnw = [[None] * 4 for _ in range(2)]
    ddn1, dnw[1][3] = rms_bwd(dn1, row(nw[1, 3]), dh4, None, name="dnorm13", out_dtype=BF)
    dy2b = ffn_bwd(y2b, up1, act1, ddn1, 1)
    dh3, dnw[1][2] = rms_bwd(h3, row(nw[1, 2]), dy2b, dh4, name="dnorm12", out_dtype=F32)
    dmix1, dnw[1][1] = rms_bwd(mix1, row(nw[1, 1]), dh3, None, name="dnorm11", out_dtype=BF)
    grad("sg_out", mm_tn(us, dmix1, name="sg_dwout", bf16_copy=True))
    dus = mm(mm_nt, dmix1, W["sg_out"], name="sg_dus")
    dpre_u, dvn, G["w_s"], dbs3 = sg_bwd(pre, vn, ws, bs3, dus, name="sg_gate_bwd")
    G["b_s"] = dbs3[:, :, 0]
    dpre_v, dlw, dlb = ln_bwd(pre, lw, dvn, name="sg_ln_bwd")
    G["ln_w"], G["ln_b"] = dlw[0], dlb[0]
    dpre = jnp.concatenate([dpre_u, dpre_v], axis=1)
    grad("sg_in", mm_tn(y0b, dpre, name="sg_dwin", shards=N_CHIPS, bf16_copy=True))
    dy0b = mm(mm_nt, dpre, W["sg_in"], name="sg_dy")
    dh2, dnw[1][0] = rms_bwd(h2, row(nw[1, 0]), dy0b, dh3, name="dnorm10", out_dtype=F32)

    ddn, dnw[0][3] = rms_bwd(dn, row(nw[0, 3]), dh2, None, name="dnorm03", out_dtype=BF)
    dy2 = ffn_bwd(y2, up, act, ddn, 0)
    dh1, dnw[0][2] = rms_bwd(h1, row(nw[0, 2]), dy2, dh2, name="dnorm02", out_dtype=F32)
    dmix, dnw[0][1] = rms_bwd(mix, row(nw[0, 1]), dh1, None, name="dnorm01", out_dtype=BF)
    grad("la_out", mm_tn(ocat, dmix, name="la_dwout", bf16_copy=True))
    docat = mm(mm_nt, dmix, W["la_out"], name="la_docat")
    dq, dk, dv, dz, dbg, dwn = gdn_bwd(q, k, v, pm, 3 * H, bg, gcrow, wn, or_a, sall_a, tall, docat, H,
                                       name="gdn_bwd")
    drq, drk, drv, drg = ret_bwd(pm, 4 * H, 5 * H, 6 * H, 7 * H, cs, sn, dmat, avec, bvec, gam, or_b, sall_b,
                                 docat, H, H, name="ret_bwd")
    dpg, da, ddt = gates_bwd(pg, arow, dtrow, dbg, H, name="gates_bwd")
    dcq, dwq = prep_bwd_act(pm, 0, wc8, 0, H, True, dq, name="prep_dq")
    dck, dwk = prep_bwd_act(pm, H, wc8, H, H, True, dk, name="prep_dk")
    dcv, dwv = prep_bwd_act(pm, 2 * H, wc8, 2 * H, H, False, dv, name="prep_dv")
    dxq = prep_bwd_conv(dcq, wc8, 0, H, name="conv_dq")
    dxk = prep_bwd_conv(dck, wc8, H, H, name="conv_dk")
    dxv = prep_bwd_conv(dcv, wc8, 2 * H, H, name="conv_dv")
    dpm = jnp.concatenate([dxq, dxk, dxv, dz, drq, drk, drv, drg], axis=1)
    g_main = mm_tn(y0, dpm, name="la_dwin_main")
    g_gate = mm_tn(y0, dpg, name="la_dwin_gate")
    grad("la_in", _la_in_grad_by_chip(g_main, g_gate, H))
    dy0 = mm(mm_nt, dpm, W["la_in_main"], name="la_dy_main")
    dy0 = mm_nt(dpg, W["la_in_gate"], name="la_dy_gate", epilogue=_add, extras=(dy0,))
    dx, dnw[0][0] = rms_bwd(x2, row(nw[0, 0]), dy0, dh1, name="dnorm00", out_dtype=F32)

    G["norm_w"] = jnp.stack([jnp.concatenate(r, axis=0) for r in dnw], axis=0)
    G["conv_w"] = jnp.transpose(jnp.concatenate([dwq, dwk, dwv], axis=1)[:CONV_WIDTH])
    G["a_log"] = da[0, H:2 * H]
    G["dt_bias"] = ddt[0, H:2 * H]
    G["out_norm_w"] = dwn[0]
    return loss, dx, G


def _as2d(a):
    n = int(np.prod(a.shape))
    if a.shape[-1] < LANES and n % LANES == 0:
        return a.reshape(-1, LANES)
    return a.reshape(-1, a.shape[-1])


def _adamw_any(w, g, m, v, name):
    shp = w.shape
    d, m2, v2 = adamw(_as2d(w), _as2d(g.reshape(shp)), _as2d(m), _as2d(v), name=name)
    return g.reshape(shp), d.reshape(shp), m2.reshape(shp), v2.reshape(shp)


def kernel(x, norm_w, la_w_in, la_conv_w, la_a_log, la_dt_bias, la_out_norm_w, la_w_out, sg_w_in, sg_ln_w, sg_ln_b, sg_w_s, sg_b_s, sg_w_out, ffn_w_up, ffn_w_down, loss_target, m_norm_w, m_la_w_in, m_la_conv_w, m_la_a_log, m_la_dt_bias, m_la_out_norm_w, m_la_w_out, m_sg_w_in, m_sg_ln_w, m_sg_ln_b, m_sg_w_s, m_sg_b_s, m_sg_w_out, m_ffn_w_up, m_ffn_w_down, v_norm_w, v_la_w_in, v_la_conv_w, v_la_a_log, v_la_dt_bias, v_la_out_norm_w, v_la_w_out, v_sg_w_in, v_sg_ln_w, v_sg_ln_b, v_sg_w_s, v_sg_b_s, v_sg_w_out, v_ffn_w_up, v_ffn_w_down):
    weights = dict(norm_w=norm_w, la_w_in=la_w_in, la_conv_w=la_conv_w, la_a_log=la_a_log, la_dt_bias=la_dt_bias,
                   la_out_norm_w=la_out_norm_w, la_w_out=la_w_out, sg_w_in=sg_w_in, sg_ln_w=sg_ln_w,
                   sg_ln_b=sg_ln_b, sg_w_s=sg_w_s, sg_b_s=sg_b_s, sg_w_out=sg_w_out, ffn_w_up=ffn_w_up,
                   ffn_w_down=ffn_w_down)
    mom_m = dict(norm_w=m_norm_w, la_w_in=m_la_w_in, la_conv_w=m_la_conv_w, la_a_log=m_la_a_log,
                 la_dt_bias=m_la_dt_bias, la_out_norm_w=m_la_out_norm_w, la_w_out=m_la_w_out, sg_w_in=m_sg_w_in,
                 sg_ln_w=m_sg_ln_w, sg_ln_b=m_sg_ln_b, sg_w_s=m_sg_w_s, sg_b_s=m_sg_b_s, sg_w_out=m_sg_w_out,
                 ffn_w_up=m_ffn_w_up, ffn_w_down=m_ffn_w_down)
    mom_v = dict(norm_w=v_norm_w, la_w_in=v_la_w_in, la_conv_w=v_la_conv_w, la_a_log=v_la_a_log,
                 la_dt_bias=v_la_dt_bias, la_out_norm_w=v_la_out_norm_w, la_w_out=v_la_w_out, sg_w_in=v_sg_w_in,
                 sg_ln_w=v_sg_ln_w, sg_ln_b=v_sg_ln_b, sg_w_s=v_sg_w_s, sg_b_s=v_sg_b_s, sg_w_out=v_sg_w_out,
                 ffn_w_up=v_ffn_w_up, ffn_w_down=v_ffn_w_down)
    order = list(weights)

    T, D = x.shape[1], x.shape[2]
    H = la_a_log.shape[1]
    HD = H * HEAD_DIM
    xi, yi, ci = _place()
    chip = 2 * xi + yi
    c_idx = jnp.reshape(ci, (1,)).astype(jnp.int32)
    j_idx = jnp.reshape(chip, (1,)).astype(jnp.int32)

    shards = dict(la_in=la_w_in[0], la_out=la_w_out[0], sg_in=sg_w_in[0], sg_out=sg_w_out[0],
                  ffn_up0=ffn_w_up[0], ffn_up1=ffn_w_up[1], ffn_down0=ffn_w_down[0], ffn_down1=ffn_w_down[1])
    shard_halves = {w: _halves(a.astype(BF)) for w, a in shards.items()}
    small_shapes = [norm_w.shape, la_conv_w[0].shape, sg_ln_w[0].shape, sg_ln_b[0].shape]
    small = _pack_rows([norm_w, la_conv_w[0], sg_ln_w[0], sg_ln_b[0]])
    small = _halves(jnp.pad(small, ((0, (-small.shape[0]) % (2 * SUBLANES)), (0, 0))))
    first = [shard_halves["la_in"], shard_halves["la_out"], small]
    la_in_g, la_out_g, small_g = [g.reshape(N_CHIPS, -1, g.shape[-1])
                                  for g in run_comm(gather_comm(first), name="gather_first")]
    pieces = [_unpack_rows(small_g[kk], small_shapes) for kk in range(N_CHIPS)]
    la_nat = jnp.transpose(la_in_g, (1, 0, 2)).reshape(D, -1)
    W = dict(
        norm_w=jnp.concatenate([p[0] for p in pieces], axis=-1),
        conv_w=jnp.concatenate([p[1] for p in pieces], axis=0),
        ln_w=jnp.concatenate([p[2] for p in pieces], axis=0),
        ln_b=jnp.concatenate([p[3] for p in pieces], axis=0),
        a_log=la_a_log[0], dt_bias=la_dt_bias[0], out_norm_w=la_out_norm_w[0], w_s=sg_w_s[0], b_s=sg_b_s[0],
        la_in_main=jnp.concatenate([la_nat[:, :4 * HD], la_nat[:, 4 * HD + 2 * H:]], axis=1),
        la_in_gate=jnp.pad(la_nat[:, 4 * HD:4 * HD + 2 * H], ((0, 0), (0, LANES - 2 * H))),
        la_out=la_out_g.reshape(-1, la_out_g.shape[-1]), ffn_up=[None, None], ffn_down=[None, None],
    )
    plan = ExchangePlan(shard_halves, c_idx, j_idx)

    loss_local, dx, G = _train_local(x[0], loss_target[0], W, plan)
    loss = lax.psum(loss_local, ("x", "y", "c"))

    big_params = dict(la_w_in="la_in", la_w_out="la_out", sg_w_in="sg_in", sg_w_out="sg_out",
                      ffn_w_up="ffn_up", ffn_w_down="ffn_down")
    from_sib = dict(zip(big_params, pair_share([plan.finished[k] for k in big_params.values()],
                                               name="grads_pair_share")))
    big_res = {nm: adamw_halves(weights[nm], plan.finished[key], from_sib[nm], mom_m[nm], mom_v[nm], c_idx,
                                name=f"adamw_{nm}") for nm, key in big_params.items()}

    small_names = ["norm_w", "conv_w", "ln_w", "ln_b", "a_log", "dt_bias", "out_norm_w", "w_s", "b_s"]
    small_full = [G[nm] for nm in small_names]
    summed = _unpack_rows(all_reduce_small(_pack_rows(small_full), name="grads_all_reduce_small"),
                          [g.shape for g in small_full])
    sm = dict(zip(small_names, summed))
    own = lambda full, axis: lax.dynamic_slice_in_dim(full, chip * (full.shape[axis] // N_CHIPS),
                                                      full.shape[axis] // N_CHIPS, axis)
    grads = dict(
        norm_w=own(sm["norm_w"], 2), la_conv_w=own(sm["conv_w"], 0), la_a_log=sm["a_log"],
        la_dt_bias=sm["dt_bias"], la_out_norm_w=sm["out_norm_w"],
        sg_ln_w=own(sm["ln_w"], 0), sg_ln_b=own(sm["ln_b"], 0), sg_w_s=sm["w_s"], sg_b_s=sm["b_s"],
    )

    res = {nm: big_res[nm] if nm in big_res else
           _adamw_any(weights[nm], grads[nm], mom_m[nm], mom_v[nm], f"adamw_{nm}") for nm in order}
    return (loss, dx.reshape(x.shape), *[res[nm][0] for nm in order], *[res[nm][1] for nm in order],
            *[res[nm][2] for nm in order], *[res[nm][3] for nm in order])
```

```python
import math
from typing import Callable, NamedTuple

import numpy as np
import jax
import jax.numpy as jnp
from jax import lax
from jax.experimental import pallas as pl
from jax.experimental.pallas import tpu as pltpu

F32 = jnp.float32
BF = jnp.bfloat16

V7X_VMEM_BYTES = 64 * 1024 * 1024
VMEM_LIMIT = (V7X_VMEM_BYTES * 3) // 4
LANES = 128
SUBLANES = 8
BF16_ROWS = 16
HEAD_DIM = 128
LA_CHUNK = 64
SG_CHUNK = 128
CONV_WIDTH = 4
ROPE_BASE = 10000.0
EPS = 1e-6
L2_EPS = 1e-6
N_CHIPS = 4
N_DEV = 8

ADAM_LR = 0.001
ADAM_B1 = 0.9
ADAM_B2 = 0.999
ADAM_EPS = 1e-08
ADAM_WD = 0.01
ADAM_STEP = 10

MESH = pl.DeviceIdType.MESH
ANY = pl.BlockSpec(memory_space=pl.ANY)

NN = (((1,), (0,)), ((), ()))
NT = (((1,), (1,)), ((), ()))
TN = (((0,), (0,)), ((), ()))


def _pcall(body, **kw):
    return pl.pallas_call(body, **kw)


def _cp(n_axes):
    return pltpu.CompilerParams(dimension_semantics=("arbitrary",) * n_axes, vmem_limit_bytes=VMEM_LIMIT)


def _tile(n, pref, unit=LANES):
    if n <= pref:
        return n
    t = (pref // unit) * unit
    while t >= unit:
        if n % t == 0:
            return t
        t -= unit
    return n


def _dot(a, b, dims=NN):
    return lax.dot_general(a.astype(BF), b.astype(BF), dims, preferred_element_type=F32)


def _split_bf16(x):
    hi = x.astype(BF)
    return hi, (x - hi.astype(F32)).astype(BF)


def _dot_hi(a, b, dims=NN):
    ah, al = _split_bf16(a)
    bh, bl = _split_bf16(b)
    dot = lambda u, v: lax.dot_general(u, v, dims, preferred_element_type=F32)
    return dot(ah, bh) + (dot(ah, bl) + dot(al, bh))


def _sigmoid(x):
    return 1.0 / (1.0 + jnp.exp(-x))


def _silu(x):
    return x * _sigmoid(x)


def _dsilu(x):
    s = _sigmoid(x)
    return s * (1.0 + x * (1.0 - s))


GELU_C = math.sqrt(2.0 / math.pi)
GELU_A = 0.044715


def _gelu(x):
    return 0.5 * x * (1.0 + jnp.tanh(GELU_C * (x + GELU_A * x * x * x)))


def _gelu_and_grad(x):
    t = jnp.tanh(GELU_C * (x + GELU_A * x * x * x))
    return 0.5 * x * (1.0 + t), 0.5 * (1.0 + t) + 0.5 * x * (1.0 - t * t) * GELU_C * (1.0 + 3.0 * GELU_A * x * x)


class Comm(NamedTuple):
    ins: list
    outs: list
    nsem: int
    start: Callable
    finish: Callable


def _matmul(a, b, *, dims, grid, a_spec, b_spec, out_shape, out_spec, acc_shape, name,
            epilogue=None, extras=(), extra_specs=(), comm=None):
    nk = grid[2]
    outs = tuple(out_shape) if isinstance(out_shape, (tuple, list)) else (out_shape,)
    out_specs = tuple(out_spec) if isinstance(out_spec, (tuple, list)) else (out_spec,)
    n_ex, n_out = len(extras), len(outs)
    n_ci = len(comm.ins) if comm else 0
    n_co = len(comm.outs) if comm else 0

    def body(*refs):
        a_ref, b_ref = refs[0], refs[1]
        ex = refs[2:2 + n_ex]
        ci = refs[2 + n_ex:2 + n_ex + n_ci]
        o = refs[2 + n_ex + n_ci:2 + n_ex + n_ci + n_out]
        co = refs[2 + n_ex + n_ci + n_out:2 + n_ex + n_ci + n_out + n_co]
        scratch = refs[2 + n_ex + n_ci + n_out + n_co:]
        i, j, k = pl.program_id(0), pl.program_id(1), pl.program_id(2)

        if comm:
            send, recv = scratch[-2], scratch[-1]

            @pl.when((i == 0) & (j == 0) & (k == 0))
            def _():
                comm.start(ci, co, send, recv)

        part = lax.dot_general(a_ref[...].astype(BF), b_ref[...].astype(BF), dims, preferred_element_type=F32)

        def finish(val):
            res = epilogue(val, *[e[...] for e in ex]) if epilogue is not None else (val,)
            for r, oref in zip(res, o):
                oref[...] = r.astype(oref.dtype)

        if nk == 1:
            finish(part)
        else:
            acc = scratch[0]

            @pl.when(k == 0)
            def _():
                acc[...] = part

            @pl.when(k > 0)
            def _():
                acc[...] += part

            @pl.when(k == nk - 1)
            def _():
                finish(acc[...])

        if comm:
            @pl.when((i == grid[0] - 1) & (j == grid[1] - 1) & (k == nk - 1))
            def _():
                comm.finish(ci, co, send, recv)

    scratch_shapes = [pltpu.VMEM(acc_shape, F32)] if nk > 1 else []
    if comm:
        scratch_shapes += [pltpu.SemaphoreType.DMA((comm.nsem,)), pltpu.SemaphoreType.DMA((comm.nsem,))]
    res = _pcall(
        body, name=name, grid=grid,
        in_specs=[a_spec, b_spec, *extra_specs, *[ANY] * n_ci],
        out_specs=[*out_specs, *[ANY] * n_co],
        out_shape=[*outs, *(comm.outs if comm else [])],
        scratch_shapes=scratch_shapes,
        compiler_params=_cp(3),
    )(a, b, *extras, *(comm.ins if comm else []))
    main = res[0] if n_out == 1 else list(res[:n_out])
    return (main, list(res[n_out:])) if comm else main


def mm_nn(a, w, *, name, out_dtypes=(F32,), epilogue=None, extras=(), comm=None, tm=1024, tn=1024, tk=2048):
    M, K = a.shape
    if w.ndim == 3:
        S, _, Ns = w.shape
        N = S * Ns
    else:
        S, Ns = 1, w.shape[1]
        N = Ns
    tm, tn, tk = _tile(M, tm), _tile(Ns, tn), _tile(K, tk)
    npb = Ns // tn
    grid = (M // tm, N // tn, K // tk)
    a_spec = pl.BlockSpec((tm, tk), lambda i, j, k: (i, k))
    if w.ndim == 3:
        b_spec = pl.BlockSpec((None, tk, tn), lambda i, j, k: (j // npb, k, j % npb))
    else:
        b_spec = pl.BlockSpec((tk, tn), lambda i, j, k: (k, j))
    o_spec = pl.BlockSpec((tm, tn), lambda i, j, k: (i, j))
    outs = tuple(jax.ShapeDtypeStruct((M, N), d) for d in out_dtypes)
    res = _matmul(a, w, dims=NN, grid=grid, a_spec=a_spec, b_spec=b_spec,
                  out_shape=outs, out_spec=(o_spec,) * len(outs), acc_shape=(tm, tn), name=name,
                  epilogue=epilogue, extras=extras, extra_specs=(o_spec,) * len(extras), comm=comm)
    return res


def mm_nt(a, w, *, name, out_dtypes=(F32,), epilogue=None, extras=(), comm=None, tm=1024, tn=1024, tk=2048):
    M, Kc = a.shape
    if w.ndim == 3:
        S, Nout, Ks = w.shape
    else:
        S, (Nout, Ks) = 1, w.shape
    assert S * Ks == Kc
    tm, tn, tk = _tile(M, tm), _tile(Nout, tn), _tile(Ks, tk)
    kpb = Ks // tk
    grid = (M // tm, Nout // tn, Kc // tk)
    a_spec = pl.BlockSpec((tm, tk), lambda i, j, k: (i, k))
    if w.ndim == 3:
        b_spec = pl.BlockSpec((None, tn, tk), lambda i, j, k: (k // kpb, j, k % kpb))
    else:
        b_spec = pl.BlockSpec((tn, tk), lambda i, j, k: (j, k))
    o_spec = pl.BlockSpec((tm, tn), lambda i, j, k: (i, j))
    outs = tuple(jax.ShapeDtypeStruct((M, Nout), d) for d in out_dtypes)
    return _matmul(a, w, dims=NT, grid=grid, a_spec=a_spec, b_spec=b_spec,
                   out_shape=outs, out_spec=(o_spec,) * len(outs), acc_shape=(tm, tn), name=name,
                   epilogue=epilogue, extras=extras, extra_specs=(o_spec,) * len(extras), comm=comm)


def mm_tn(x, dy, *, name, shards=1, bf16_copy=False, tm=1024, tn=1024, tk=2048):
    T, Kin = x.shape
    N = dy.shape[1]
    Ns = N // shards
    tm, tn, tk = _tile(Kin, tm), _tile(Ns, tn), _tile(T, tk)
    npb = Ns // tn
    grid = (Kin // tm, N // tn, T // tk)
    a_spec = pl.BlockSpec((tk, tm), lambda i, j, k: (k, i))
    b_spec = pl.BlockSpec((tk, tn), lambda i, j, k: (k, j))
    if shards > 1:
        o_spec = pl.BlockSpec((None, tm, tn), lambda i, j, k: (j // npb, i, j % npb))
        out = jax.ShapeDtypeStruct((shards, Kin, Ns), F32)
    else:
        o_spec = pl.BlockSpec((tm, tn), lambda i, j, k: (i, j))
        out = jax.ShapeDtypeStruct((Kin, N), F32)
    if bf16_copy:
        return _matmul(x, dy, dims=TN, grid=grid, a_spec=a_spec, b_spec=b_spec,
                       out_shape=(out, jax.ShapeDtypeStruct(out.shape, BF)), out_spec=(o_spec, o_spec),
                       acc_shape=(tm, tn), name=name, epilogue=lambda acc: (acc, acc))
    return _matmul(x, dy, dims=TN, grid=grid, a_spec=a_spec, b_spec=b_spec,
                   out_shape=out, out_spec=o_spec, acc_shape=(tm, tn), name=name)


ROW_TILE = 256


def _rows(tr, d):
    return pl.BlockSpec((tr, d), lambda i: (i, 0))


def _fixed(shape):
    nd = len(shape)
    return pl.BlockSpec(shape, lambda *_: (0,) * nd)


def _rms(xv, w):
    r = lax.rsqrt(jnp.mean(xv * xv, axis=-1, keepdims=True) + EPS)
    return xv * r * w


def rms_fwd(x, w, *, name):
    T, D = x.shape
    tr = _tile(T, ROW_TILE, SUBLANES)

    def body(x_ref, w_ref, y_ref):
        y_ref[...] = _rms(x_ref[...], w_ref[...]).astype(y_ref.dtype)

    return _pcall(body, name=name, grid=(T // tr,), in_specs=[_rows(tr, D), _fixed((1, D))],
                  out_specs=_rows(tr, D), out_shape=jax.ShapeDtypeStruct((T, D), BF), compiler_params=_cp(1))(x, w)


def res_norm(h, m, wa, wb, *, name):
    T, D = h.shape
    tr = _tile(T, ROW_TILE, SUBLANES)

    def body(h_ref, m_ref, wa_ref, wb_ref, ho_ref, y_ref):
        ho = h_ref[...] + _rms(m_ref[...], wa_ref[...])
        ho_ref[...] = ho
        y_ref[...] = _rms(ho, wb_ref[...]).astype(y_ref.dtype)

    return _pcall(body, name=name, grid=(T // tr,),
                  in_specs=[_rows(tr, D), _rows(tr, D), _fixed((1, D)), _fixed((1, D))],
                  out_specs=[_rows(tr, D), _rows(tr, D)],
                  out_shape=[jax.ShapeDtypeStruct((T, D), F32), jax.ShapeDtypeStruct((T, D), BF)],
                  compiler_params=_cp(1))(h, m, wa, wb)


def rms_bwd(x, w, dy, dres, *, name, out_dtype):
    T, D = x.shape
    tr = _tile(T, ROW_TILE, SUBLANES)
    has_res = dres is not None

    def body(*refs):
        if has_res:
            x_ref, w_ref, dy_ref, dr_ref, dx_ref, dw_ref = refs
        else:
            x_ref, w_ref, dy_ref, dx_ref, dw_ref = refs
        i = pl.program_id(0)
        xv = x_ref[...]
        r = lax.rsqrt(jnp.mean(xv * xv, axis=-1, keepdims=True) + EPS)
        xh = xv * r
        dyv = dy_ref[...].astype(F32)
        dyw = dyv * w_ref[...]
        dx = r * (dyw - xh * jnp.mean(dyw * xh, axis=-1, keepdims=True))
        if has_res:
            dx = dx + dr_ref[...].astype(F32)
        dx_ref[...] = dx.astype(dx_ref.dtype)
        part = jnp.sum(dyv * xh, axis=0, keepdims=True)

        @pl.when(i == 0)
        def _():
            dw_ref[...] = part

        @pl.when(i > 0)
        def _():
            dw_ref[...] += part

    ins = [x, w, dy] + ([dres] if has_res else [])
    in_specs = [_rows(tr, D), _fixed((1, D)), _rows(tr, D)] + ([_rows(tr, D)] if has_res else [])
    return _pcall(body, name=name, grid=(T // tr,), in_specs=in_specs,
                  out_specs=[_rows(tr, D), _fixed((1, D))],
                  out_shape=[jax.ShapeDtypeStruct((T, D), out_dtype), jax.ShapeDtypeStruct((1, D), F32)],
                  compiler_params=_cp(1))(*ins)


def last_norm_and_loss(h, m, w, tgt, *, name):
    T, D = h.shape
    tr = _tile(T, ROW_TILE, SUBLANES)

    def body(h_ref, m_ref, w_ref, t_ref, dy_ref, dm_ref, dw_ref, l_ref):
        i = pl.program_id(0)
        mv = m_ref[...]
        r = lax.rsqrt(jnp.mean(mv * mv, axis=-1, keepdims=True) + EPS)
        xh = mv * r
        e = h_ref[...] + xh * w_ref[...] - t_ref[...]
        dy = e * (1.0 / D)
        dy_ref[...] = dy
        dyw = dy * w_ref[...]
        dm_ref[...] = (r * (dyw - xh * jnp.mean(dyw * xh, axis=-1, keepdims=True))).astype(dm_ref.dtype)
        pw = jnp.sum(dy * xh, axis=0, keepdims=True)
        pl_ = 0.5 * jnp.sum(jnp.mean(e * e, axis=-1, keepdims=True), axis=0, keepdims=True)
        pl_ = jnp.broadcast_to(pl_, (1, LANES))

        @pl.when(i == 0)
        def _():
            dw_ref[...] = pw
            l_ref[...] = pl_

        @pl.when(i > 0)
        def _():
            dw_ref[...] += pw
            l_ref[...] += pl_

    return _pcall(body, name=name, grid=(T // tr,),
                  in_specs=[_rows(tr, D), _rows(tr, D), _fixed((1, D)), _rows(tr, D)],
                  out_specs=[_rows(tr, D), _rows(tr, D), _fixed((1, D)), _fixed((1, LANES))],
                  out_shape=[jax.ShapeDtypeStruct((T, D), F32), jax.ShapeDtypeStruct((T, D), BF),
                             jax.ShapeDtypeStruct((1, D), F32), jax.ShapeDtypeStruct((1, LANES), F32)],
                  compiler_params=_cp(1))(h, m, w, tgt)


def adamw(w, g, m, v, *, name):
    R, C = w.shape
    tr = _tile(R, max(SUBLANES, (1 << 18) // C), SUBLANES)
    c1 = 1.0 - ADAM_B1 ** ADAM_STEP
    c2 = 1.0 - ADAM_B2 ** ADAM_STEP

    def body(w_ref, g_ref, m_ref, v_ref, d_ref, mo_ref, vo_ref):
        gv = g_ref[...]
        m2 = ADAM_B1 * m_ref[...] + (1.0 - ADAM_B1) * gv
        v2 = ADAM_B2 * v_ref[...] + (1.0 - ADAM_B2) * (gv * gv)
        d_ref[...] = -ADAM_LR * ((m2 / c1) / (jnp.sqrt(v2 / c2) + ADAM_EPS) + ADAM_WD * w_ref[...])
        mo_ref[...] = m2
        vo_ref[...] = v2

    spec = _rows(tr, C)
    sds = jax.ShapeDtypeStruct((R, C), F32)
    return _pcall(body, name=name, grid=(R // tr,), in_specs=[spec] * 4, out_specs=[spec] * 3,
                  out_shape=[sds] * 3, compiler_params=_cp(1))(w, g, m, v)


HALO = SUBLANES


def _conv_down(xx, w_ref):
    acc = xx * w_ref[pl.ds(CONV_WIDTH - 1, 1), :]
    for d in range(1, CONV_WIDTH):
        acc = acc + pltpu.roll(xx, d, 0) * w_ref[pl.ds(CONV_WIDTH - 1 - d, 1), :]
    return acc


def _conv_tile(x_ref, halo_ref, w_ref, first):
    xs = x_ref[...]
    hal = jnp.where(first, 0.0, halo_ref[...])
    cat = jnp.concatenate([hal, xs[0:HALO]], axis=0)
    return jnp.concatenate([_conv_down(cat, w_ref)[HALO:2 * HALO], _conv_down(xs, w_ref)[HALO:]], axis=0)


def _shift_down_tile(x_ref, halo_ref, first, d):
    xs = x_ref[...]
    if d == 0:
        return xs
    hal = jnp.where(first, 0.0, halo_ref[...])
    cat = jnp.concatenate([hal, xs[0:HALO]], axis=0)
    return jnp.concatenate([pltpu.roll(cat, d, 0)[HALO:2 * HALO], pltpu.roll(xs, d, 0)[HALO:]], axis=0)


def _l2n(s):
    return s * lax.rsqrt(jnp.sum(s * s, axis=-1, keepdims=True) + L2_EPS)


PREP_ROWS = 512


def _l2n_groups(s, nb):
    return jnp.concatenate([_l2n(s[:, g * LANES:(g + 1) * LANES]) for g in range(nb)], axis=1)


def prep_fwd(pm, off, wc8, woff, nblk, l2, *, name):
    T = pm.shape[0]
    tr = _tile(T, PREP_ROWS, SUBLANES)
    hb = tr // HALO
    wb = _heads_per_step(nblk)
    wl = wb * LANES

    def body(x_ref, halo_ref, w_ref, o_ref):
        i = pl.program_id(0)
        s = _silu(_conv_tile(x_ref, halo_ref, w_ref, i == 0))
        o_ref[...] = _l2n_groups(s, wb) if l2 else s

    return _pcall(
        body, name=name, grid=(T // tr, nblk // wb),
        in_specs=[pl.BlockSpec((tr, wl), lambda i, c: (i, off // wb + c)),
                  pl.BlockSpec((HALO, wl), lambda i, c: (jnp.maximum(i * hb - 1, 0), off // wb + c)),
                  pl.BlockSpec((SUBLANES, wl), lambda i, c: (0, woff // wb + c))],
        out_specs=pl.BlockSpec((tr, wl), lambda i, c: (i, c)),
        out_shape=jax.ShapeDtypeStruct((T, nblk * LANES), F32), compiler_params=_cp(2))(pm, pm, wc8)


def prep_bwd_act(pm, off, wc8, woff, nblk, l2, dout, *, name):
    T = pm.shape[0]
    tr = _tile(T, PREP_ROWS, SUBLANES)
    hb = tr // HALO
    wb = _heads_per_step(nblk)
    wl = wb * LANES

    def l2_bwd(s, do):
        r = lax.rsqrt(jnp.sum(s * s, axis=-1, keepdims=True) + L2_EPS)
        nrm = s * r
        return r * (do - nrm * jnp.sum(do * nrm, axis=-1, keepdims=True))

    def body(x_ref, halo_ref, w_ref, do_ref, dc_ref, dw_ref):
        i = pl.program_id(1)
        first = i == 0
        y = _conv_tile(x_ref, halo_ref, w_ref, first)
        s = _silu(y)
        do = do_ref[...]
        if l2:
            ds = jnp.concatenate([l2_bwd(s[:, g * LANES:(g + 1) * LANES], do[:, g * LANES:(g + 1) * LANES])
                                  for g in range(wb)], axis=1)
        else:
            ds = do
        dc = ds * _dsilu(y)
        dc_ref[...] = dc

        @pl.when(first)
        def _():
            dw_ref[...] = jnp.zeros_like(dw_ref)

        for j in range(CONV_WIDTH):
            xsh = _shift_down_tile(x_ref, halo_ref, first, CONV_WIDTH - 1 - j)
            dw_ref[pl.ds(j, 1), :] += jnp.sum(dc * xsh, axis=0, keepdims=True)

    return _pcall(
        body, name=name, grid=(nblk // wb, T // tr),
        in_specs=[pl.BlockSpec((tr, wl), lambda c, i: (i, off // wb + c)),
                  pl.BlockSpec((HALO, wl), lambda c, i: (jnp.maximum(i * hb - 1, 0), off // wb + c)),
                  pl.BlockSpec((SUBLANES, wl), lambda c, i: (0, woff // wb + c)),
                  pl.BlockSpec((tr, wl), lambda c, i: (i, c))],
        out_specs=[pl.BlockSpec((tr, wl), lambda c, i: (i, c)),
                   pl.BlockSpec((SUBLANES, wl), lambda c, i: (0, c))],
        out_shape=[jax.ShapeDtypeStruct((T, nblk * LANES), F32),
                   jax.ShapeDtypeStruct((SUBLANES, nblk * LANES), F32)],
        compiler_params=_cp(2))(pm, pm, wc8, dout)


def prep_bwd_conv(dc, wc8, woff, nblk, *, name):
    T = dc.shape[0]
    tr = _tile(T, PREP_ROWS, SUBLANES)
    hb = tr // HALO
    nt = T // tr
    last_halo = T // HALO - 1
    wb = _heads_per_step(nblk)
    wl = wb * LANES

    def up(xx, w_ref):
        rows = xx.shape[0]
        acc = xx * w_ref[pl.ds(CONV_WIDTH - 1, 1), :]
        for d in range(1, CONV_WIDTH):
            acc = acc + pltpu.roll(xx, rows - d, 0) * w_ref[pl.ds(CONV_WIDTH - 1 - d, 1), :]
        return acc

    def body(x_ref, halo_ref, w_ref, o_ref):
        i = pl.program_id(0)
        xs = x_ref[...]
        hal = jnp.where(i == nt - 1, 0.0, halo_ref[...])
        cat = jnp.concatenate([xs[tr - HALO:tr], hal], axis=0)
        out = jnp.concatenate([up(xs, w_ref)[:tr - HALO], up(cat, w_ref)[0:HALO]], axis=0)
        o_ref[...] = out.astype(o_ref.dtype)

    return _pcall(
        body, name=name, grid=(nt, nblk // wb),
        in_specs=[pl.BlockSpec((tr, wl), lambda i, c: (i, c)),
                  pl.BlockSpec((HALO, wl), lambda i, c: (jnp.minimum((i + 1) * hb, last_halo), c)),
                  pl.BlockSpec((SUBLANES, wl), lambda i, c: (0, woff // wb + c))],
        out_specs=pl.BlockSpec((tr, wl), lambda i, c: (i, c)),
        out_shape=jax.ShapeDtypeStruct((T, nblk * LANES), BF), compiler_params=_cp(2))(dc, dc, wc8)


def _softplus(x):
    return jnp.maximum(x, 0.0) + jnp.log(1.0 + jnp.exp(-jnp.abs(x)))


def _tril_ones(c):
    t = lax.broadcasted_iota(jnp.int32, (c, c), 0)
    s = lax.broadcasted_iota(jnp.int32, (c, c), 1)
    return (t >= s).astype(F32)


def _triu_ones(c):
    t = lax.broadcasted_iota(jnp.int32, (c, c), 0)
    s = lax.broadcasted_iota(jnp.int32, (c, c), 1)
    return (t <= s).astype(F32)


def gates_fwd(pg, arow, dtrow, H, *, name):
    T = pg.shape[0]
    C = LA_CHUNK
    N = T // C

    def body(x_ref, a_ref, dt_ref, bg_ref, gr_ref):
        x = x_ref[...]
        lane = lax.broadcasted_iota(jnp.int32, (C, LANES), 1)
        g = -jnp.exp(a_ref[...]) * _softplus(x + dt_ref[...])
        g = jnp.where((lane >= H) & (lane < 2 * H), g, 0.0)
        lm = _tril_ones(C)
        gc = _dot_hi(lm, g)
        bg_ref[...] = jnp.where(lane < H, _sigmoid(x), gc)
        gr_ref[...] = _dot_hi(g, _triu_ones(C), TN)

    return _pcall(
        body, name=name, grid=(N,),
        in_specs=[pl.BlockSpec((C, LANES), lambda n: (n, 0)), _fixed((1, LANES)), _fixed((1, LANES))],
        out_specs=[pl.BlockSpec((C, LANES), lambda n: (n, 0)), pl.BlockSpec((None, LANES, C), lambda n: (n, 0, 0))],
        out_shape=[jax.ShapeDtypeStruct((T, LANES), F32), jax.ShapeDtypeStruct((N, LANES, C), F32)],
        compiler_params=_cp(1))(pg, arow, dtrow)


def gates_bwd(pg, arow, dtrow, dbg, H, *, name):
    T = pg.shape[0]
    C = LA_CHUNK
    N = T // C

    def body(x_ref, a_ref, dt_ref, d_ref, dx_ref, da_ref, ddt_ref):
        n = pl.program_id(0)
        x = x_ref[...]
        d = d_ref[...]
        lane = lax.broadcasted_iota(jnp.int32, (C, LANES), 1)
        in_g = (lane >= H) & (lane < 2 * H)
        e = jnp.exp(a_ref[...])
        xs = x + dt_ref[...]
        g = -e * _softplus(xs)
        dg = _dot_hi(_tril_ones(C), jnp.where(in_g, d, 0.0), TN)
        dxs = jnp.where(in_g, dg * (-e) * _sigmoid(xs), 0.0)
        beta = _sigmoid(x)
        dx_ref[...] = jnp.where(lane < H, d * beta * (1.0 - beta), dxs).astype(dx_ref.dtype)
        pa = jnp.sum(jnp.where(in_g, dg * g, 0.0), axis=0, keepdims=True)
        pd = jnp.sum(dxs, axis=0, keepdims=True)

        @pl.when(n == 0)
        def _():
            da_ref[...] = pa
            ddt_ref[...] = pd

        @pl.when(n > 0)
        def _():
            da_ref[...] += pa
            ddt_ref[...] += pd

    return _pcall(
        body, name=name, grid=(N,),
        in_specs=[pl.BlockSpec((C, LANES), lambda n: (n, 0)), _fixed((1, LANES)), _fixed((1, LANES)),
                  pl.BlockSpec((C, LANES), lambda n: (n, 0))],
        out_specs=[pl.BlockSpec((C, LANES), lambda n: (n, 0)), _fixed((1, LANES)), _fixed((1, LANES))],
        out_shape=[jax.ShapeDtypeStruct((T, LANES), BF), jax.ShapeDtypeStruct((1, LANES), F32),
                   jax.ShapeDtypeStruct((1, LANES), F32)],
        compiler_params=_cp(1))(pg, arow, dtrow, dbg)


QK_SCALE = HEAD_DIM ** -0.5


HEADS_PER_STEP = 8


def _heads_per_step(H):
    hb = HEADS_PER_STEP
    while H % hb:
        hb //= 2
    return hb


def _head_rstd(o):
    return lax.rsqrt(jnp.mean(o * o, axis=-1, keepdims=True) + EPS)


def _gdn_gates(bg_ref, gr_ref, h, H):
    C = LA_CHUNK
    bgv = bg_ref[...]
    lane = lax.broadcasted_iota(jnp.int32, (C, LANES), 1)
    beta = jnp.sum(jnp.where(lane == h, bgv, 0.0), axis=1, keepdims=True)
    gc = jnp.sum(jnp.where(lane == H + h, bgv, 0.0), axis=1, keepdims=True)
    grow = gr_ref[pl.ds(H + h, 1), :]
    ri = lax.broadcasted_iota(jnp.int32, (C, 1), 0)
    gl = jnp.sum(jnp.where(ri == C - 1, gc, 0.0), axis=0, keepdims=True)
    return beta, gc, grow, gl


def _chunk_masks():
    C = LA_CHUNK
    ti = lax.broadcasted_iota(jnp.int32, (C, C), 0)
    si = lax.broadcasted_iota(jnp.int32, (C, C), 1)
    return ti >= si, ti > si, ti == si


def _decay(gc, grow, causal):
    return jnp.where(causal, jnp.exp(jnp.where(causal, gc - grow, 0.0)), 0.0)


def _interleave(gens):
    gens = list(gens)
    results = [None] * len(gens)
    live = list(range(len(gens)))
    while live:
        still = []
        for i in live:
            try:
                next(gens[i])
                still.append(i)
            except StopIteration as stop:
                results[i] = stop.value
        live = still
    return results


def _unit_lower_inverse(a, eye):
    x = -a
    p = jnp.where(eye, 1.0, 0.0) + x
    for _ in range(5):
        x = _dot_hi(x, x)
        yield
        p = p + _dot_hi(p, x)
        yield
    return p


def gdn_fwd(q, k, v, pm, zoff, bg, gcrow, wn, H, *, name):
    T = q.shape[0]
    C = LA_CHUNK
    N = T // C
    hd = HEAD_DIM

    HB = _heads_per_step(H)

    def body(q_ref, k_ref, v_ref, z_ref, bg_ref, gr_ref, wn_ref, og_ref, or_ref, sall_ref, tall_ref, S):
        n = pl.program_id(0)
        hg = pl.program_id(1)
        causal, strict, eye = _chunk_masks()

        @pl.when((n == 0) & (hg == 0))
        def _():
            S[...] = jnp.zeros_like(S)

        states = [S[hg * HB + i] for i in range(HB)]

        def head(i):
            h = hg * HB + i
            sl = slice(i * hd, (i + 1) * hd)
            beta, gc, grow, gl = _gdn_gates(bg_ref, gr_ref, h, H)
            dm = _decay(gc, grow, causal)
            qs = q_ref[:, sl] * QK_SCALE
            kk = k_ref[:, sl]
            vv = v_ref[:, sl]
            eg = jnp.exp(gc)
            kb = kk * beta
            a = jnp.where(strict, _dot(kb, kk, NT) * dm, 0.0)
            yield
            tm = yield from _unit_lower_inverse(a, eye)
            u = _dot(tm, vv * beta)
            w = _dot(tm, kb * eg)
            qk = jnp.where(causal, _dot(qs, kk, NT) * dm, 0.0)
            yield
            s0 = states[i]
            vnew = u - _dot(w, s0)
            o = _dot(qs * eg, s0)
            yield
            o = o + _dot(qk, vnew)
            s1 = s0 * jnp.exp(gl) + _dot(kk * jnp.exp(gl - gc), vnew, TN)
            yield
            sall_ref[i] = s0
            tall_ref[i] = tm
            or_ref[:, sl] = o
            og_ref[:, sl] = (o * _head_rstd(o) * wn_ref[...] * _silu(z_ref[:, sl])).astype(og_ref.dtype)
            return s1

        for i, s1 in enumerate(_interleave([head(i) for i in range(HB)])):
            S[hg * HB + i] = s1

    blk = lambda off: pl.BlockSpec((C, HB * hd), lambda n, h: (n, off // HB + h))
    return _pcall(
        body, name=name, grid=(N, H // HB),
        in_specs=[blk(0), blk(0), blk(0), blk(zoff),
                  pl.BlockSpec((C, LANES), lambda n, h: (n, 0)),
                  pl.BlockSpec((None, LANES, C), lambda n, h: (n, 0, 0)),
                  _fixed((1, hd))],
        out_specs=[blk(0), blk(0),
                   pl.BlockSpec((None, HB, hd, hd), lambda n, h: (n, h, 0, 0)),
                   pl.BlockSpec((None, HB, C, C), lambda n, h: (n, h, 0, 0))],
        out_shape=[jax.ShapeDtypeStruct((T, H * hd), BF), jax.ShapeDtypeStruct((T, H * hd), F32),
                   jax.ShapeDtypeStruct((N, H, hd, hd), F32), jax.ShapeDtypeStruct((N, H, C, C), F32)],
        scratch_shapes=[pltpu.VMEM((H, hd, hd), F32)],
        compiler_params=_cp(2))(q, k, v, pm, bg, gcrow, wn)


def gdn_bwd(q, k, v, pm, zoff, bg, gcrow, wn, oraw, sall, tall, dog, H, *, name):
    T = q.shape[0]
    C = LA_CHUNK
    N = T // C
    hd = HEAD_DIM

    HB = _heads_per_step(H)

    def body(*refs):
        dbg_ref, dwn_ref, dS = refs[15], refs[16], refs[17]
        n = pl.program_id(0)
        hg = pl.program_id(1)

        @pl.when((n == 0) & (hg == 0))
        def _():
            dwn_ref[...] = jnp.zeros_like(dwn_ref)
            dS[...] = jnp.zeros_like(dS)

        @pl.when(hg == 0)
        def _():
            dbg_ref[...] = jnp.zeros_like(dbg_ref)

        ds_in = [dS[hg * HB + i] for i in range(HB)]
        outs = _interleave([head(i, hg * HB + i, ds_in[i], *refs) for i in range(HB)])
        for i in range(HB):
            dS[hg * HB + i] = outs[i][0]
        dwn_ref[...] += sum(o[1] for o in outs)
        dbg_ref[...] += sum(o[2] for o in outs)

    def head(i, h, ds1, q_ref, k_ref, v_ref, z_ref, bg_ref, gr_ref, wn_ref, or_ref, sall_ref, tall_ref, dog_ref,
             dq_ref, dk_ref, dv_ref, dz_ref, dbg_ref, dwn_ref, dS):
        sl = slice(i * hd, (i + 1) * hd)
        beta, gc, grow, gl = _gdn_gates(bg_ref, gr_ref, h, H)
        causal, strict, eye = _chunk_masks()
        dm = _decay(gc, grow, causal)
        qs = q_ref[:, sl] * QK_SCALE
        kk = k_ref[:, sl]
        vv = v_ref[:, sl]
        zz = z_ref[:, sl]
        o = or_ref[:, sl]
        dog = dog_ref[:, sl]
        wn_v = wn_ref[...]
        s0 = sall_ref[i]
        tm = tall_ref[i]

        rstd = _head_rstd(o)
        on = o * rstd
        sz = _silu(zz)
        don = dog * wn_v * sz
        dwn_part = jnp.sum(dog * on * sz, axis=0, keepdims=True)
        dz_ref[:, sl] = (dog * on * wn_v * _dsilu(zz)).astype(dz_ref.dtype)
        do = rstd * (don - on * jnp.mean(don * on, axis=-1, keepdims=True))

        eg = jnp.exp(gc)
        kb = kk * beta
        vb = vv * beta
        kbg = kb * eg
        a = jnp.where(strict, _dot(kb, kk, NT) * dm, 0.0)
        u = _dot(tm, vb)
        w = _dot(tm, kbg)
        qk = jnp.where(causal, _dot(qs, kk, NT) * dm, 0.0)
        dqdec = _dot(do, s0, NT)
        yield
        vnew = u - _dot(w, s0)
        qdec = qs * eg
        etail = jnp.exp(gl - gc)
        ktail = kk * etail
        egl = jnp.exp(gl)
        dvnew = _dot(qk, do, TN) + _dot(ktail, ds1)
        yield
        dqk = jnp.where(causal, _dot(do, vnew, NT), 0.0)
        dktail = _dot(vnew, ds1, NT)
        dcd = jnp.sum(jnp.sum(s0 * ds1, axis=1, keepdims=True), axis=0, keepdims=True)
        ds0 = egl * ds1 + _dot(qdec, do, TN) - _dot(w, dvnew, TN)
        dw = -_dot(dvnew, s0, NT)
        dvb = _dot(tm, dvnew, TN)
        yield
        dkbg = _dot(tm, dw, TN)
        dtm = _dot(dvnew, vb, NT) + _dot(dw, kbg, NT)
        dqkr = dqk * dm
        dqs = _dot(dqkr, kk) + dqdec * eg
        yield
        x = _dot_hi(tm, dtm, TN)
        yield
        da = jnp.where(strict, -_dot_hi(x, tm, NT), 0.0)
        yield
        dkk = da * dm
        dkb = _dot(dkk, kk) + dkbg * eg
        dk = _dot(dkk, kb, TN)
        dk = dk + _dot(dqkr, qs, TN) + dktail * etail + dkb * beta
        g = da * a + dqk * qk
        colsum = jnp.max(_dot_hi(g, jnp.ones((C, LANES), F32), TN), axis=1, keepdims=True)
        yield
        rk = jnp.sum(dktail * ktail, axis=1, keepdims=True)
        dgc = (jnp.sum(g, axis=1, keepdims=True) - colsum
               + jnp.sum(dqdec * qdec, axis=1, keepdims=True) - rk
               + jnp.sum(dkbg * kbg, axis=1, keepdims=True))
        dgl = jnp.sum(rk, axis=0, keepdims=True) + dcd * egl
        ri = lax.broadcasted_iota(jnp.int32, (C, 1), 0)
        dgc = dgc + jnp.where(ri == C - 1, dgl, 0.0)
        dbeta = jnp.sum(dkb * kk, axis=1, keepdims=True) + jnp.sum(dvb * vv, axis=1, keepdims=True)

        dq_ref[:, sl] = dqs * QK_SCALE
        dk_ref[:, sl] = dk
        dv_ref[:, sl] = dvb * beta
        lane = lax.broadcasted_iota(jnp.int32, (C, LANES), 1)
        return ds0, dwn_part, jnp.where(lane == h, dbeta, 0.0) + jnp.where(lane == H + h, dgc, 0.0)

    blk = lambda off: pl.BlockSpec((C, HB * hd), lambda n, h: (N - 1 - n, off // HB + h))
    st = lambda r: pl.BlockSpec((None, HB, r, r), lambda n, h: (N - 1 - n, h, 0, 0))
    return _pcall(
        body, name=name, grid=(N, H // HB),
        in_specs=[blk(0), blk(0), blk(0), blk(zoff),
                  pl.BlockSpec((C, LANES), lambda n, h: (N - 1 - n, 0)),
                  pl.BlockSpec((None, LANES, C), lambda n, h: (N - 1 - n, 0, 0)),
                  _fixed((1, hd)), blk(0), st(hd), st(C), blk(0)],
        out_specs=[blk(0), blk(0), blk(0), blk(0),
                   pl.BlockSpec((C, LANES), lambda n, h: (N - 1 - n, 0)), _fixed((1, hd))],
        out_shape=[jax.ShapeDtypeStruct((T, H * hd), F32)] * 3
        + [jax.ShapeDtypeStruct((T, H * hd), BF), jax.ShapeDtypeStruct((T, LANES), F32),
           jax.ShapeDtypeStruct((1, hd), F32)],
        scratch_shapes=[pltpu.VMEM((H, hd, hd), F32)],
        compiler_params=_cp(2))(q, k, v, pm, bg, gcrow, wn, oraw, sall, tall, dog)


def _rot(x, cs, sn):
    return x * cs + pltpu.roll(x, HEAD_DIM // 2, 1) * sn


def _rot_t(dy, cs, sn):
    return dy * cs + pltpu.roll(dy * sn, HEAD_DIM // 2, 1)


def ret_fwd(pm, qoff, koff, voff, goff, cs, sn, dmat, avec, bvec, gam, H, *, name):
    T = pm.shape[0]
    C = LA_CHUNK
    N = T // C
    hd = HEAD_DIM

    HB = _heads_per_step(H)

    def body(q_ref, k_ref, v_ref, g_ref, cs_ref, sn_ref, dm_ref, a_ref, b_ref, gam_ref,
             og_ref, or_ref, sall_ref, S):
        n = pl.program_id(0)
        hg = pl.program_id(1)
        c, s = cs_ref[...], sn_ref[...]

        @pl.when((n == 0) & (hg == 0))
        def _():
            S[...] = jnp.zeros_like(S)

        states = [S[hg * HB + i] for i in range(HB)]

        def head(i):
            sl = slice(i * hd, (i + 1) * hd)
            qq = _rot(q_ref[:, sl], c, s)
            kk = _rot(k_ref[:, sl], c, s) * QK_SCALE
            vv = v_ref[:, sl]
            s0 = states[i]
            p = _dot(qq, kk, NT) * dm_ref[i]
            cross = _dot(qq * a_ref[i], s0)
            s1 = s0 * gam_ref[i] + _dot(kk * b_ref[i], vv, TN)
            yield
            o = _dot(p, vv) + cross
            yield
            sall_ref[i] = s0
            or_ref[:, sl] = o
            og_ref[:, sl] = (_silu(g_ref[:, sl]) * o * _head_rstd(o)).astype(og_ref.dtype)
            return s1

        for i, s1 in enumerate(_interleave([head(i) for i in range(HB)])):
            S[hg * HB + i] = s1

    blk = lambda off: pl.BlockSpec((C, HB * hd), lambda n, h: (n, off // HB + h))
    tab = pl.BlockSpec((C, hd), lambda n, h: (n, 0))
    per_h = lambda r, cdim: pl.BlockSpec((HB, r, cdim), lambda n, h: (h, 0, 0))
    return _pcall(
        body, name=name, grid=(N, H // HB),
        in_specs=[blk(qoff), blk(koff), blk(voff), blk(goff), tab, tab,
                  per_h(C, C), per_h(C, hd), per_h(C, hd), per_h(1, hd)],
        out_specs=[blk(0), blk(0), pl.BlockSpec((None, HB, hd, hd), lambda n, h: (n, h, 0, 0))],
        out_shape=[jax.ShapeDtypeStruct((T, H * hd), BF), jax.ShapeDtypeStruct((T, H * hd), F32),
                   jax.ShapeDtypeStruct((N, H, hd, hd), F32)],
        scratch_shapes=[pltpu.VMEM((H, hd, hd), F32)],
        compiler_params=_cp(2))(pm, pm, pm, pm, cs, sn, dmat, avec, bvec, gam)


def ret_bwd(pm, qoff, koff, voff, goff, cs, sn, dmat, avec, bvec, gam, oraw, sall, dog, dogoff, H, *, name):
    T = pm.shape[0]
    C = LA_CHUNK
    N = T // C
    hd = HEAD_DIM

    HB = _heads_per_step(H)

    def body(q_ref, k_ref, v_ref, g_ref, cs_ref, sn_ref, dm_ref, a_ref, b_ref, gam_ref, or_ref, sall_ref,
             dog_ref, dq_ref, dk_ref, dv_ref, dg_ref, dS):
        n = pl.program_id(0)
        hg = pl.program_id(1)
        c, s = cs_ref[...], sn_ref[...]

        @pl.when((n == 0) & (hg == 0))
        def _():
            dS[...] = jnp.zeros_like(dS)

        dstates = [dS[hg * HB + i] for i in range(HB)]

        def head(i):
            sl = slice(i * hd, (i + 1) * hd)
            qq = _rot(q_ref[:, sl], c, s)
            kk = _rot(k_ref[:, sl], c, s) * QK_SCALE
            vv = v_ref[:, sl]
            gg = g_ref[:, sl]
            o = or_ref[:, sl]
            dog = dog_ref[:, sl]
            dm = dm_ref[i]
            av, bv = a_ref[i], b_ref[i]
            s0 = sall_ref[i]
            ds1 = dstates[i]

            rstd = _head_rstd(o)
            on = o * rstd
            don = dog * _silu(gg)
            dg_ref[:, sl] = (dog * on * _dsilu(gg)).astype(dg_ref.dtype)
            do = rstd * (don - on * jnp.mean(don * on, axis=-1, keepdims=True))

            p = _dot(qq, kk, NT) * dm
            dp = _dot(do, vv, NT) * dm
            cross_q = _dot(do, s0, NT) * av
            cross_k = _dot(vv, ds1, NT) * bv
            cross_v = _dot(kk * bv, ds1)
            ds0 = ds1 * gam_ref[i] + _dot(qq * av, do, TN)
            yield
            dv_ref[:, sl] = (_dot(p, do, TN) + cross_v).astype(dv_ref.dtype)
            dqq = _dot(dp, kk) + cross_q
            dkk = (_dot(dp, qq, TN) + cross_k) * QK_SCALE
            yield
            dq_ref[:, sl] = _rot_t(dqq, c, s).astype(dq_ref.dtype)
            dk_ref[:, sl] = _rot_t(dkk, c, s).astype(dk_ref.dtype)
            return ds0

        for i, ds0 in enumerate(_interleave([head(i) for i in range(HB)])):
            dS[hg * HB + i] = ds0

    blk = lambda off: pl.BlockSpec((C, HB * hd), lambda n, h: (N - 1 - n, off // HB + h))
    tab = pl.BlockSpec((C, hd), lambda n, h: (N - 1 - n, 0))
    per_h = lambda r, cdim: pl.BlockSpec((HB, r, cdim), lambda n, h: (h, 0, 0))
    return _pcall(
        body, name=name, grid=(N, H // HB),
        in_specs=[blk(qoff), blk(koff), blk(voff), blk(goff), tab, tab,
                  per_h(C, C), per_h(C, hd), per_h(C, hd), per_h(1, hd), blk(0),
                  pl.BlockSpec((None, HB, hd, hd), lambda n, h: (N - 1 - n, h, 0, 0)), blk(dogoff)],
        out_specs=[blk(0)] * 4,
        out_shape=[jax.ShapeDtypeStruct((T, H * hd), BF)] * 4,
        scratch_shapes=[pltpu.VMEM((H, hd, hd), F32)],
        compiler_params=_cp(2))(pm, pm, pm, pm, cs, sn, dmat, avec, bvec, gam, oraw, sall, dog)


LN_ROWS = 128


def ln_fwd(pre, lw, lb, *, name):
    T, W2 = pre.shape
    W = W2 // 2
    tr = _tile(T, LN_ROWS, SUBLANES)

    def body(p_ref, w_ref, b_ref, o_ref):
        v = _gelu(p_ref[...])
        xc = v - jnp.mean(v, axis=-1, keepdims=True)
        r = lax.rsqrt(jnp.mean(xc * xc, axis=-1, keepdims=True) + EPS)
        o_ref[...] = xc * r * w_ref[...] + b_ref[...]

    return _pcall(body, name=name, grid=(T // tr,),
                  in_specs=[pl.BlockSpec((tr, W), lambda i: (i, 1)), _fixed((1, W)), _fixed((1, W))],
                  out_specs=_rows(tr, W), out_shape=jax.ShapeDtypeStruct((T, W), F32),
                  compiler_params=_cp(1))(pre, lw, lb)


def ln_bwd(pre, lw, dvn, *, name):
    T, W2 = pre.shape
    W = W2 // 2
    tr = _tile(T, LN_ROWS, SUBLANES)

    def body(p_ref, w_ref, d_ref, dp_ref, dw_ref, db_ref):
        i = pl.program_id(0)
        v, dgelu = _gelu_and_grad(p_ref[...])
        xc = v - jnp.mean(v, axis=-1, keepdims=True)
        r = lax.rsqrt(jnp.mean(xc * xc, axis=-1, keepdims=True) + EPS)
        xh = xc * r
        d = d_ref[...]
        dxh = d * w_ref[...]
        dv = r * (dxh - jnp.mean(dxh, axis=-1, keepdims=True) - xh * jnp.mean(dxh * xh, axis=-1, keepdims=True))
        dp_ref[...] = (dv * dgelu).astype(dp_ref.dtype)
        pw = jnp.sum(d * xh, axis=0, keepdims=True)
        pb = jnp.sum(d, axis=0, keepdims=True)

        @pl.when(i == 0)
        def _():
            dw_ref[...] = pw
            db_ref[...] = pb

        @pl.when(i > 0)
        def _():
            dw_ref[...] += pw
            db_ref[...] += pb

    return _pcall(body, name=name, grid=(T // tr,),
                  in_specs=[pl.BlockSpec((tr, W), lambda i: (i, 1)), _fixed((1, W)), _rows(tr, W)],
                  out_specs=[_rows(tr, W), _fixed((1, W)), _fixed((1, W))],
                  out_shape=[jax.ShapeDtypeStruct((T, W), BF), jax.ShapeDtypeStruct((1, W), F32),
                             jax.ShapeDtypeStruct((1, W), F32)],
                  compiler_params=_cp(1))(pre, lw, dvn)


def _tril_mask(c):
    t = lax.broadcasted_iota(jnp.int32, (c, c), 0)
    s = lax.broadcasted_iota(jnp.int32, (c, c), 1)
    return t >= s


def sg_fwd(pre, vn, ws, bs3, *, name):
    T, W = vn.shape
    G = ws.shape[0]
    gd = W // G
    C = SG_CHUNK

    def body(p_ref, v_ref, w_ref, b_ref, o_ref):
        mask = _tril_mask(C)
        for g in range(G):
            sl = slice(g * gd, (g + 1) * gd)
            wm = jnp.where(mask, w_ref[g], 0.0)
            s = _dot(wm, v_ref[:, sl]) + b_ref[g]
            o_ref[:, sl] = (_gelu(p_ref[:, sl]) * s).astype(o_ref.dtype)

    blk = pl.BlockSpec((C, W), lambda n: (n, 0))
    return _pcall(body, name=name, grid=(T // C,),
                  in_specs=[blk, blk, _fixed((G, C, C)), _fixed((G, C, 1))],
                  out_specs=blk, out_shape=jax.ShapeDtypeStruct((T, W), BF),
                  compiler_params=_cp(1))(pre, vn, ws, bs3)


def sg_bwd(pre, vn, ws, bs3, dus, *, name):
    T, W = vn.shape
    G = ws.shape[0]
    gd = W // G
    C = SG_CHUNK

    def body(p_ref, v_ref, w_ref, b_ref, d_ref, dp_ref, dv_ref, dw_ref, db_ref):
        n = pl.program_id(0)
        mask = _tril_mask(C)

        @pl.when(n == 0)
        def _():
            dw_ref[...] = jnp.zeros_like(dw_ref)
            db_ref[...] = jnp.zeros_like(db_ref)

        for g in range(G):
            sl = slice(g * gd, (g + 1) * gd)
            wm = jnp.where(mask, w_ref[g], 0.0)
            u, du = _gelu_and_grad(p_ref[:, sl])
            vv = v_ref[:, sl]
            d = d_ref[:, sl]
            s = _dot(wm, vv) + b_ref[g]
            ds = d * u
            dp_ref[:, sl] = (d * s * du).astype(dp_ref.dtype)
            dv_ref[:, sl] = _dot(wm, ds, TN)
            dw_ref[g] += jnp.where(mask, _dot(ds, vv, NT), 0.0)
            db_ref[g] += jnp.sum(ds, axis=1, keepdims=True)

    blk = pl.BlockSpec((C, W), lambda n: (n, 0))
    return _pcall(body, name=name, grid=(T // C,),
                  in_specs=[blk, blk, _fixed((G, C, C)), _fixed((G, C, 1)), blk],
                  out_specs=[blk, blk, _fixed((G, C, C)), _fixed((G, C, 1))],
                  out_shape=[jax.ShapeDtypeStruct((T, W), BF), jax.ShapeDtypeStruct((T, W), F32),
                             jax.ShapeDtypeStruct((G, C, C), F32), jax.ShapeDtypeStruct((G, C, 1), F32)],
                  compiler_params=_cp(1))(pre, vn, ws, bs3, dus)


CHIP_RELATIONS = ((1, 0), (0, 1), (1, 1))


def _place():
    return lax.axis_index("x"), lax.axis_index("y"), lax.axis_index("c")


def _peer_chip(x, y, r):
    fx, fy = CHIP_RELATIONS[r]
    return (1 - x if fx else x), (1 - y if fy else y)


def gather_comm(arrs):
    n = len(arrs)
    per = 2 * len(CHIP_RELATIONS) + 1
    own = per - 1

    def ici(a, r, ins, outs, send, recv):
        x, y, c = _place()
        px, py = _peer_chip(x, y, r)
        return pltpu.make_async_remote_copy(
            src_ref=ins[a].at[c], dst_ref=outs[a].at[2 * x + y, c], send_sem=send.at[a * per + r],
            recv_sem=recv.at[a * per + r], device_id=(px, py, c), device_id_type=MESH)

    def own_block(a, ins, outs, send, recv):
        x, y, c = _place()
        return pltpu.make_async_remote_copy(
            src_ref=ins[a], dst_ref=outs[a].at[2 * x + y], send_sem=send.at[a * per + own],
            recv_sem=recv.at[a * per + own], device_id=(x, y, 1 - c), device_id_type=MESH)

    def start(ins, outs, send, recv):
        for a in range(n):
            for r in range(3):
                ici(a, r, ins, outs, send, recv).start()
            own_block(a, ins, outs, send, recv).start()

    def finish(ins, outs, send, recv):
        x, y, c = _place()
        sib = (x, y, 1 - c)
        forwards = []
        for a in range(n):
            for r in range(3):
                px, py = _peer_chip(x, y, r)
                landed = outs[a].at[2 * px + py, c]
                pltpu.make_async_remote_copy(
                    src_ref=landed, dst_ref=landed, send_sem=send.at[a * per + r],
                    recv_sem=recv.at[a * per + r], device_id=(px, py, c), device_id_type=MESH).wait_recv()
                fw = pltpu.make_async_remote_copy(
                    src_ref=landed, dst_ref=landed, send_sem=send.at[a * per + 3 + r],
                    recv_sem=recv.at[a * per + 3 + r], device_id=sib, device_id_type=MESH)
                fw.start()
                forwards.append(fw)
        for a in range(n):
            for r in range(3):
                px, py = _peer_chip(x, y, r)
                other = outs[a].at[2 * px + py, 1 - c]
                pltpu.make_async_remote_copy(
                    src_ref=other, dst_ref=other, send_sem=send.at[a * per + 3 + r],
                    recv_sem=recv.at[a * per + 3 + r], device_id=sib, device_id_type=MESH).wait_recv()
        for a in range(n):
            for r in range(3):
                ici(a, r, ins, outs, send, recv).wait_send()
            own_block(a, ins, outs, send, recv).wait()
        for fw in forwards:
            fw.wait_send()

    outs = [jax.ShapeDtypeStruct((N_CHIPS,) + a.shape, a.dtype) for a in arrs]
    return Comm(list(arrs), outs, n * per, start, finish)


def chip_exchange_comm(ps):
    n = len(ps)

    def copies(ins, outs, send, recv):
        x, y, c = _place()
        cps = []
        for a in range(n):
            for r in range(3):
                px, py = _peer_chip(x, y, r)
                cps.append(pltpu.make_async_remote_copy(
                    src_ref=ins[a].at[2 * px + py], dst_ref=outs[a].at[r], send_sem=send.at[3 * a + r],
                    recv_sem=recv.at[3 * a + r], device_id=(px, py, c), device_id_type=MESH))
        return cps

    def start(ins, outs, send, recv):
        for cp in copies(ins, outs, send, recv):
            cp.start()

    def finish(ins, outs, send, recv):
        for cp in copies(ins, outs, send, recv):
            cp.wait()

    outs = [jax.ShapeDtypeStruct((3,) + p.shape[1:], p.dtype) for p in ps]
    return Comm(list(ps), outs, 3 * n, start, finish)


def run_comm(comm, *, name):
    n_i, n_o = len(comm.ins), len(comm.outs)

    def body(*refs):
        ins, outs = refs[:n_i], refs[n_i:n_i + n_o]
        send, recv = refs[n_i + n_o:]
        comm.start(ins, outs, send, recv)
        comm.finish(ins, outs, send, recv)

    res = _pcall(body, name=name, in_specs=[ANY] * n_i, out_specs=[ANY] * n_o, out_shape=comm.outs,
                 scratch_shapes=[pltpu.SemaphoreType.DMA((comm.nsem,)), pltpu.SemaphoreType.DMA((comm.nsem,))])(*comm.ins)
    return list(res)


def pair_exchange(gs, *, name):
    n = len(gs)

    def body(*refs):
        ins, outs = refs[:n], refs[n:2 * n]
        send, recv = refs[2 * n:2 * n + 2]
        x, y, c = _place()
        cps = []
        for a in range(n):
            cp = pltpu.make_async_remote_copy(
                src_ref=ins[a].at[:, pl.ds(1 - c, 1)], dst_ref=outs[a], send_sem=send.at[a], recv_sem=recv.at[a],
                device_id=(x, y, 1 - c), device_id_type=MESH)
            cp.start()
            cps.append(cp)
        for cp in cps:
            cp.wait()

    out_shape = [jax.ShapeDtypeStruct((g.shape[0], 1) + g.shape[2:], g.dtype) for g in gs]
    res = _pcall(body, name=name, in_specs=[ANY] * n, out_specs=[ANY] * n, out_shape=out_shape,
                 scratch_shapes=[pltpu.SemaphoreType.DMA((n,)), pltpu.SemaphoreType.DMA((n,))])(*gs)
    return list(res)


def pair_share(fs, *, name):
    n = len(fs)

    def body(*refs):
        ins, outs = refs[:n], refs[n:2 * n]
        send, recv = refs[2 * n:2 * n + 2]
        x, y, c = _place()
        cps = []
        for a in range(n):
            cp = pltpu.make_async_remote_copy(
                src_ref=ins[a], dst_ref=outs[a], send_sem=send.at[a], recv_sem=recv.at[a],
                device_id=(x, y, 1 - c), device_id_type=MESH)
            cp.start()
            cps.append(cp)
        for cp in cps:
            cp.wait()

    out_shape = [jax.ShapeDtypeStruct(f.shape, f.dtype) for f in fs]
    res = _pcall(body, name=name, in_specs=[ANY] * n, out_specs=[ANY] * n, out_shape=out_shape,
                 scratch_shapes=[pltpu.SemaphoreType.DMA((n,)), pltpu.SemaphoreType.DMA((n,))])(*fs)
    return list(res)


def all_reduce_small(v, *, name):
    R = v.shape[0]

    def body(v_ref, sum_ref, gat_ref, send, recv):
        x, y, c = _place()
        me = 4 * x + 2 * y + c
        gat_ref[me] = v_ref[...]
        cps = []
        peers = []
        for r in range(1, N_DEV):
            fx, fy, fc = (r >> 2) & 1, (r >> 1) & 1, r & 1
            px, py, pc = (1 - x if fx else x), (1 - y if fy else y), (1 - c if fc else c)
            peers.append((px, py, pc))
            cp = pltpu.make_async_remote_copy(
                src_ref=v_ref, dst_ref=gat_ref.at[me], send_sem=send.at[r - 1], recv_sem=recv.at[r - 1],
                device_id=(px, py, pc), device_id_type=MESH)
            cp.start()
            cps.append(cp)
        for r in range(1, N_DEV):
            px, py, pc = peers[r - 1]
            slot = gat_ref.at[4 * px + 2 * py + pc]
            pltpu.make_async_remote_copy(
                src_ref=v_ref, dst_ref=slot, send_sem=send.at[r - 1], recv_sem=recv.at[r - 1],
                device_id=(px, py, pc), device_id_type=MESH).wait_recv()
        for cp in cps:
            cp.wait_send()
        acc = gat_ref[0]
        for s in range(1, N_DEV):
            acc = acc + gat_ref[s]
        sum_ref[...] = acc

    vm = pl.BlockSpec(memory_space=pltpu.VMEM)
    res = _pcall(body, name=name, in_specs=[vm], out_specs=[vm, vm],
                 out_shape=[jax.ShapeDtypeStruct((R, LANES), F32), jax.ShapeDtypeStruct((N_DEV, R, LANES), F32)],
                 scratch_shapes=[pltpu.SemaphoreType.DMA((N_DEV - 1,)), pltpu.SemaphoreType.DMA((N_DEV - 1,))],
                 compiler_params=pltpu.CompilerParams(vmem_limit_bytes=VMEM_LIMIT))(v)
    return res[0]


def pair_sum(g, r1, c_idx, *, name):
    nb, _, hr, C = g.shape
    tr = _tile(hr, max(BF16_ROWS, (1 << 18) // C), BF16_ROWS)

    def body(c_ref, g_ref, r_ref, o_ref, ob_ref):
        s = g_ref[...] + r_ref[...].astype(F32)
        o_ref[...] = s
        ob_ref[...] = s.astype(ob_ref.dtype)

    out = pl.BlockSpec((None, tr, C), lambda b, i, cr: (b, i, 0))
    gs = pltpu.PrefetchScalarGridSpec(
        num_scalar_prefetch=1, grid=(nb, hr // tr),
        in_specs=[pl.BlockSpec((None, None, tr, C), lambda b, i, cr: (b, cr[0], i, 0)),
                  pl.BlockSpec((None, None, tr, C), lambda b, i, cr: (b, 0, i, 0))],
        out_specs=[out, out])
    return _pcall(body, name=name, grid_spec=gs,
                  out_shape=[jax.ShapeDtypeStruct((nb, hr, C), F32), jax.ShapeDtypeStruct((nb, hr, C), BF)],
                  compiler_params=_cp(2))(c_idx, g, r1)


def chip_sum(p, r2, j_idx, *, name, layer=0, n_layers=1, into=None):
    _, hr, C = p.shape
    tr = _tile(hr, max(BF16_ROWS, (1 << 18) // C), BF16_ROWS)

    def body(j_ref, p_ref, a_ref, b_ref, c_ref, *rest):
        o_ref = rest[-1]
        o_ref[...] = ((p_ref[...] + a_ref[...].astype(F32)) + b_ref[...].astype(F32)) + c_ref[...].astype(F32)

    rel = lambda r: pl.BlockSpec((None, tr, C), lambda i, jr: (r, i, 0))
    in_specs = [pl.BlockSpec((None, tr, C), lambda i, jr: (jr[0], i, 0)), rel(0), rel(1), rel(2)]
    operands = [j_idx, p, r2, r2, r2]
    aliases = {}
    if into is not None:
        in_specs.append(ANY)
        operands.append(into)
        aliases = {len(operands) - 1: 0}
    gs = pltpu.PrefetchScalarGridSpec(
        num_scalar_prefetch=1, grid=(hr // tr,), in_specs=in_specs,
        out_specs=pl.BlockSpec((None, tr, C), lambda i, jr: (layer, i, 0)))
    return _pcall(body, name=name, grid_spec=gs, out_shape=jax.ShapeDtypeStruct((n_layers, hr, C), F32),
                  input_output_aliases=aliases, compiler_params=_cp(1))(*operands)


def adamw_halves(w, g_mine, g_other, m, v, c_idx, *, name):
    L, R, C = w.shape
    hr = R // 2
    tr = _tile(hr, max(SUBLANES, (1 << 18) // C), SUBLANES)
    nbh = hr // tr
    c1 = 1.0 - ADAM_B1 ** ADAM_STEP
    c2 = 1.0 - ADAM_B2 ** ADAM_STEP

    def body(c_ref, w_ref, gm_ref, go_ref, m_ref, v_ref, g_ref, d_ref, mo_ref, vo_ref):
        i = pl.program_id(1)
        gv = jnp.where(i // nbh == c_ref[0], gm_ref[...], go_ref[...])
        m2 = ADAM_B1 * m_ref[...] + (1.0 - ADAM_B1) * gv
        v2 = ADAM_B2 * v_ref[...] + (1.0 - ADAM_B2) * (gv * gv)
        d_ref[...] = -ADAM_LR * ((m2 / c1) / (jnp.sqrt(v2 / c2) + ADAM_EPS) + ADAM_WD * w_ref[...])
        g_ref[...] = gv
        mo_ref[...] = m2
        vo_ref[...] = v2

    full = pl.BlockSpec((None, tr, C), lambda l, i, cr: (l, i, 0))
    half = pl.BlockSpec((None, tr, C), lambda l, i, cr: (l, i % nbh, 0))
    gs = pltpu.PrefetchScalarGridSpec(
        num_scalar_prefetch=1, grid=(L, R // tr),
        in_specs=[full, half, half, full, full], out_specs=[full] * 4)
    sds = jax.ShapeDtypeStruct((L, R, C), F32)
    return _pcall(body, name=name, grid_spec=gs, out_shape=[sds] * 4,
                  compiler_params=_cp(2))(c_idx, w, g_mine, g_other, m, v)


def _pack_rows(arrs):
    parts = []
    for a in arrs:
        flat = a.reshape(-1).astype(F32)
        tile = SUBLANES * LANES
        pad = (-flat.shape[0]) % tile
        parts.append(jnp.pad(flat, (0, pad)).reshape(-1, LANES))
    return jnp.concatenate(parts, axis=0)


def _unpack_rows(buf, shapes):
    out, row = [], 0
    for shp in shapes:
        size = int(np.prod(shp))
        rows = -(-size // (SUBLANES * LANES)) * SUBLANES
        out.append(buf[row:row + rows].reshape(-1)[:size].reshape(shp))
        row += rows
    return out


def _halves(a2d):
    r, c = a2d.shape
    return a2d.reshape(2, r // 2, c)


def _rotary_tables(T):
    half = HEAD_DIM // 2
    pos = jnp.arange(T, dtype=F32)
    inv_freq = 1.0 / (ROPE_BASE ** jnp.linspace(0.0, 1.0, half, dtype=F32))
    ang = pos[:, None] * inv_freq[None, :]
    cos, sin = jnp.cos(ang), jnp.sin(ang)
    return jnp.concatenate([cos, cos], axis=1), jnp.concatenate([-sin, sin], axis=1)


def _retention_tables(H):
    C = LA_CHUNK
    lg = jnp.log1p(-jnp.power(2.0, -5.0 - jnp.arange(H, dtype=F32)))
    pos = jnp.arange(C, dtype=F32)
    causal = jnp.tril(jnp.ones((C, C), dtype=bool))
    dmat = jnp.exp(jnp.where(causal, (pos[:, None] - pos[None, :]) * lg[:, None, None], -jnp.inf))
    bc = lambda t: jnp.broadcast_to(t[..., None], t.shape + (HEAD_DIM,))
    avec = bc(jnp.exp((pos + 1.0)[None, :] * lg[:, None]))
    bvec = bc(jnp.exp((C - 1.0 - pos)[None, :] * lg[:, None]))
    gam = bc(jnp.exp(C * lg)[:, None])
    return dmat, avec, bvec, gam


def _relu2(acc):
    return acc, jnp.square(jnp.maximum(acc, 0.0))


def _drelu2(acc, up):
    return (acc * (2.0 * jnp.maximum(up, 0.0)),)


ROW_SHARDED = ("la_out", "sg_out", "ffn_down0", "ffn_down1")


class ExchangePlan:
    GATHERS = {"la_in_main": ("ffn_up0", "ffn_down0"), "ffn_up_0": ("sg_in",),
               "ffn_down_0": ("sg_out", "ffn_up1"), "sg_in": ("ffn_down1",)}
    REDUCES = {"ffn_dup_1": "ffn_down1", "ffn_dy_1": "ffn_up1", "sg_dus": "sg_out", "sg_dy": "sg_in",
               "ffn_dup_0": "ffn_down0", "ffn_dy_0": "ffn_up0", "la_docat": "la_out", "la_dy": "la_in"}

    def __init__(self, shard_halves, c_idx, j_idx):
        self.shard_halves, self.c_idx, self.j_idx = shard_halves, c_idx, j_idx
        self.partial = {}
        self.finished = {}

    def comm(self, carrier):
        if carrier in self.GATHERS:
            return gather_comm([self.shard_halves[w] for w in self.GATHERS[carrier]])
        if carrier in self.REDUCES:
            return chip_exchange_comm([self.partial[self.REDUCES[carrier]][1]])
        return None

    def done(self, carrier, outs, W):
        if carrier in self.GATHERS:
            for w, g in zip(self.GATHERS[carrier], outs):
                install_gathered(W, w, g)
        else:
            w = self.REDUCES[carrier]
            per_layer = w[:-1] in ("ffn_up", "ffn_down")
            key, layer, n_layers = (w[:-1], int(w[-1]), 2) if per_layer else (w, 0, 1)
            self.finished[key] = chip_sum(self.partial[w][0], outs[0], self.j_idx, name=f"grads_chip_sum_{w}",
                                          layer=layer, n_layers=n_layers, into=self.finished.get(key))

    def grad_ready(self, w, g, payload=None):
        halves = lambda t: t.reshape(N_CHIPS, 2, t.shape[1] // 2, t.shape[2])
        sib = pair_exchange([halves(g if payload is None else payload)], name=f"grads_pair_exchange_{w}")[0]
        self.partial[w] = pair_sum(halves(g), sib, self.c_idx, name=f"grads_pair_sum_{w}")


def install_gathered(W, w, g):
    whole = g.reshape(N_CHIPS, g.shape[1] * g.shape[2], g.shape[3])
    if w in ROW_SHARDED:
        whole = whole.reshape(-1, whole.shape[-1])
    if w[:-1] in ("ffn_up", "ffn_down"):
        W[w[:-1]][int(w[-1])] = whole
    else:
        W[w] = whole


def _by_chip(w, g):
    return g.reshape(N_CHIPS, -1, g.shape[-1]) if w in ROW_SHARDED else g


def _la_shard_rows(H):
    cs = (8 * H * HEAD_DIM + 2 * H) // N_CHIPS
    return cs, -(-cs // (2 * BF16_ROWS)) * (2 * BF16_ROWS)


def _la_pieces(H):
    HD = H * HEAD_DIM
    mix = 8 * HD + 2 * H
    cs = mix // N_CHIPS
    segments = [(0, 0, 4 * HD, 0), (1, 4 * HD, 4 * HD + 2 * H, 0), (0, 4 * HD + 2 * H, mix, 4 * HD)]
    pieces = []
    for j in range(N_CHIPS):
        mine = []
        for src, a, b, base in segments:
            lo, hi = max(cs * j, a), min(cs * (j + 1), b)
            if lo < hi:
                mine.append((src, base + lo - a, base + hi - a))
        pieces.append(mine)
    return pieces


def _la_weights_from_gathered(g, H):
    parts = {0: [], 1: []}
    for j, mine in enumerate(_la_pieces(H)):
        row = 0
        for src, a, b in mine:
            parts[src].append(g[j, row:row + b - a])
            row += b - a
    gate = jnp.concatenate(parts[1], axis=0)
    return jnp.concatenate(parts[0], axis=0), jnp.pad(gate, ((0, LANES - gate.shape[0]), (0, 0)))


def _la_dproj_by_chip(main_parts, dpg, H):
    HD = H * HEAD_DIM
    cs, csp = _la_shard_rows(H)
    pad = jnp.zeros((dpg.shape[0], csp - cs), dpg.dtype)
    cols = []
    for mine in _la_pieces(H):
        for src, a, b in mine:
            while src == 0 and a < b:
                i, off = divmod(a, HD)
                end = min(b, (i + 1) * HD)
                cols.append(main_parts[i][:, off:off + end - a])
                a = end
            if src == 1:
                cols.append(dpg[:, a:b])
        cols.append(pad)
    return jnp.concatenate(cols, axis=1)


def _train_local(x2, tgt, W, plan=None):
    T, D = x2.shape
    H = W["a_log"].shape[0]
    nw = W["norm_w"]
    row = lambda v: v.reshape(1, -1).astype(F32)
    G = {}

    def mm(fn, *args, name, **kw):
        comm = plan.comm(name) if plan is not None else None
        if comm is None:
            return fn(*args, name=name, **kw)
        res, outs = fn(*args, name=name, comm=comm, **kw)
        plan.done(name, outs, W)
        return res

    def grad(w, g):
        payload = None
        if isinstance(g, (list, tuple)):
            g, payload = g
        G[w] = g
        if plan is not None:
            plan.grad_ready(w, _by_chip(w, g), None if payload is None else _by_chip(w, payload))

    def ffn_fwd(y, l):
        up, act = mm(mm_nn, y, W["ffn_up"][l], name=f"ffn_up_{l}", out_dtypes=(F32, BF), epilogue=_relu2)
        dn = mm(mm_nn, act, W["ffn_down"][l], name=f"ffn_down_{l}")
        return up, act, dn

    def ffn_bwd(y, up, act, ddn, l):
        grad(f"ffn_down{l}", mm_tn(act, ddn, name=f"ffn_dwdown_{l}", bf16_copy=True))
        dup = mm(mm_nt, ddn, W["ffn_down"][l], name=f"ffn_dup_{l}", out_dtypes=(BF,), epilogue=_drelu2, extras=(up,))
        grad(f"ffn_up{l}", mm_tn(y, dup, name=f"ffn_dwup_{l}", shards=N_CHIPS, bf16_copy=True))
        return mm(mm_nt, dup, W["ffn_up"][l], name=f"ffn_dy_{l}")

    y0 = rms_fwd(x2, row(nw[0, 0]), name="norm00")
    pm = mm(mm_nt, y0, W["la_in_main"], name="la_in_main")
    pg = mm_nt(y0, W["la_in_gate"], name="la_in_gate")
    wc8 = jnp.pad(jnp.transpose(W["conv_w"]), ((0, SUBLANES - CONV_WIDTH), (0, 0)))
    lanes_pad = (H, LANES - 2 * H)
    arow = jnp.pad(W["a_log"], lanes_pad).reshape(1, LANES)
    dtrow = jnp.pad(W["dt_bias"], lanes_pad).reshape(1, LANES)
    bg, gcrow = gates_fwd(pg, arow, dtrow, H, name="gates_fwd")
    q = prep_fwd(pm, 0, wc8, 0, H, True, name="prep_q")
    k = prep_fwd(pm, H, wc8, H, H, True, name="prep_k")
    v = prep_fwd(pm, 2 * H, wc8, 2 * H, H, False, name="prep_v")
    wn = row(W["out_norm_w"])
    og_a, or_a, sall_a, tall = gdn_fwd(q, k, v, pm, 3 * H, bg, gcrow, wn, H, name="gdn_fwd")
    cs, sn = _rotary_tables(T)
    dmat, avec, bvec, gam = _retention_tables(H)
    og_b, or_b, sall_b = ret_fwd(pm, 4 * H, 5 * H, 6 * H, 7 * H, cs, sn, dmat, avec, bvec, gam, H, name="ret_fwd")
    ocat = jnp.concatenate([og_a, og_b], axis=1)
    mix = mm_nn(ocat, W["la_out"], name="la_out")
    h1, y2 = res_norm(x2, mix, row(nw[0, 1]), row(nw[0, 2]), name="resnorm_0a")
    up, act, dn = ffn_fwd(y2, 0)
    h2, y0b = res_norm(h1, dn, row(nw[0, 3]), row(nw[1, 0]), name="resnorm_0b")

    pre = mm(mm_nn, y0b, W["sg_in"], name="sg_in")
    lw, lb = row(W["ln_w"]), row(W["ln_b"])
    vn = ln_fwd(pre, lw, lb, name="sg_ln")
    ws = W["w_s"]
    bs3 = W["b_s"][:, :, None]
    us = sg_fwd(pre, vn, ws, bs3, name="sg_gate")
    mix1 = mm_nn(us, W["sg_out"], name="sg_out")
    h3, y2b = res_norm(h2, mix1, row(nw[1, 1]), row(nw[1, 2]), name="resnorm_1a")
    up1, act1, dn1 = ffn_fwd(y2b, 1)
    dnw = [[None] * 4 for _ in range(2)]
    dh4, ddn1, dnw[1][3], lrow = last_norm_and_loss(h3, dn1, row(nw[1, 3]), tgt, name="last_norm_and_loss")
    loss = lrow[0, 0]

    dy2b = ffn_bwd(y2b, up1, act1, ddn1, 1)
    dh3, dnw[1][2] = rms_bwd(h3, row(nw[1, 2]), dy2b, dh4, name="dnorm12", out_dtype=F32)
    dmix1, dnw[1][1] = rms_bwd(mix1, row(nw[1, 1]), dh3, None, name="dnorm11", out_dtype=BF)
    grad("sg_out", mm_tn(us, dmix1, name="sg_dwout", bf16_copy=True))
    dus = mm(mm_nt, dmix1, W["sg_out"], name="sg_dus")
    dpre_u, dvn, G["w_s"], dbs3 = sg_bwd(pre, vn, ws, bs3, dus, name="sg_gate_bwd")
    G["b_s"] = dbs3[:, :, 0]
    dpre_v, dlw, dlb = ln_bwd(pre, lw, dvn, name="sg_ln_bwd")
    G["ln_w"], G["ln_b"] = dlw[0], dlb[0]
    dpre = jnp.concatenate([dpre_u, dpre_v], axis=1)
    grad("sg_in", mm_tn(y0b, dpre, name="sg_dwin", shards=N_CHIPS, bf16_copy=True))
    dy0b = mm(mm_nt, dpre, W["sg_in"], name="sg_dy")
    dh2, dnw[1][0] = rms_bwd(h2, row(nw[1, 0]), dy0b, dh3, name="dnorm10", out_dtype=F32)

    ddn, dnw[0][3] = rms_bwd(dn, row(nw[0, 3]), dh2, None, name="dnorm03", out_dtype=BF)
    dy2 = ffn_bwd(y2, up, act, ddn, 0)
    dh1, dnw[0][2] = rms_bwd(h1, row(nw[0, 2]), dy2, dh2, name="dnorm02", out_dtype=F32)
    dmix, dnw[0][1] = rms_bwd(mix, row(nw[0, 1]), dh1, None, name="dnorm01", out_dtype=BF)
    grad("la_out", mm_tn(ocat, dmix, name="la_dwout", bf16_copy=True))
    docat = mm(mm_nt, dmix, W["la_out"], name="la_docat")
    dq, dk, dv, dz, dbg, dwn = gdn_bwd(q, k, v, pm, 3 * H, bg, gcrow, wn, or_a, sall_a, tall, docat, H,
                                       name="gdn_bwd")
    drq, drk, drv, drg = ret_bwd(pm, 4 * H, 5 * H, 6 * H, 7 * H, cs, sn, dmat, avec, bvec, gam, or_b, sall_b,
                                 docat, H, H, name="ret_bwd")
    dpg, da, ddt = gates_bwd(pg, arow, dtrow, dbg, H, name="gates_bwd")
    dcq, dwq = prep_bwd_act(pm, 0, wc8, 0, H, True, dq, name="prep_dq")
    dck, dwk = prep_bwd_act(pm, H, wc8, H, H, True, dk, name="prep_dk")
    dcv, dwv = prep_bwd_act(pm, 2 * H, wc8, 2 * H, H, False, dv, name="prep_dv")
    dxq = prep_bwd_conv(dcq, wc8, 0, H, name="conv_dq")
    dxk = prep_bwd_conv(dck, wc8, H, H, name="conv_dk")
    dxv = prep_bwd_conv(dcv, wc8, 2 * H, H, name="conv_dv")
    dproj = _la_dproj_by_chip([dxq, dxk, dxv, dz, drq, drk, drv, drg], dpg, H)
    grad("la_in", mm_tn(dproj, y0, name="la_dwin").reshape(N_CHIPS, -1, D))
    dy0 = mm(mm_nn, dproj, W["la_in_rows"], name="la_dy")
    dx, dnw[0][0] = rms_bwd(x2, row(nw[0, 0]), dy0, dh1, name="dnorm00", out_dtype=F32)

    G["norm_w"] = jnp.stack([jnp.concatenate(r, axis=0) for r in dnw], axis=0)
    G["conv_w"] = jnp.transpose(jnp.concatenate([dwq, dwk, dwv], axis=1)[:CONV_WIDTH])
    G["a_log"] = da[0, H:2 * H]
    G["dt_bias"] = ddt[0, H:2 * H]
    G["out_norm_w"] = dwn[0]
    return loss, dx, G


def _as2d(a):
    n = int(np.prod(a.shape))
    if a.shape[-1] < LANES and n % LANES == 0:
        return a.reshape(-1, LANES)
    return a.reshape(-1, a.shape[-1])


def _adamw_any(w, g, m, v, name):
    shp = w.shape
    d, m2, v2 = adamw(_as2d(w), _as2d(g.reshape(shp)), _as2d(m), _as2d(v), name=name)
    return g.reshape(shp), d.reshape(shp), m2.reshape(shp), v2.reshape(shp)


def kernel(x, norm_w, la_w_in, la_conv_w, la_a_log, la_dt_bias, la_out_norm_w, la_w_out, sg_w_in, sg_ln_w, sg_ln_b, sg_w_s, sg_b_s, sg_w_out, ffn_w_up, ffn_w_down, loss_target, m_norm_w, m_la_w_in, m_la_conv_w, m_la_a_log, m_la_dt_bias, m_la_out_norm_w, m_la_w_out, m_sg_w_in, m_sg_ln_w, m_sg_ln_b, m_sg_w_s, m_sg_b_s, m_sg_w_out, m_ffn_w_up, m_ffn_w_down, v_norm_w, v_la_w_in, v_la_conv_w, v_la_a_log, v_la_dt_bias, v_la_out_norm_w, v_la_w_out, v_sg_w_in, v_sg_ln_w, v_sg_ln_b, v_sg_w_s, v_sg_b_s, v_sg_w_out, v_ffn_w_up, v_ffn_w_down):
    weights = dict(norm_w=norm_w, la_w_in=la_w_in, la_conv_w=la_conv_w, la_a_log=la_a_log, la_dt_bias=la_dt_bias,
                   la_out_norm_w=la_out_norm_w, la_w_out=la_w_out, sg_w_in=sg_w_in, sg_ln_w=sg_ln_w,
                   sg_ln_b=sg_ln_b, sg_w_s=sg_w_s, sg_b_s=sg_b_s, sg_w_out=sg_w_out, ffn_w_up=ffn_w_up,
                   ffn_w_down=ffn_w_down)
    mom_m = dict(norm_w=m_norm_w, la_w_in=m_la_w_in, la_conv_w=m_la_conv_w, la_a_log=m_la_a_log,
                 la_dt_bias=m_la_dt_bias, la_out_norm_w=m_la_out_norm_w, la_w_out=m_la_w_out, sg_w_in=m_sg_w_in,
                 sg_ln_w=m_sg_ln_w, sg_ln_b=m_sg_ln_b, sg_w_s=m_sg_w_s, sg_b_s=m_sg_b_s, sg_w_out=m_sg_w_out,
                 ffn_w_up=m_ffn_w_up, ffn_w_down=m_ffn_w_down)
    mom_v = dict(norm_w=v_norm_w, la_w_in=v_la_w_in, la_conv_w=v_la_conv_w, la_a_log=v_la_a_log,
                 la_dt_bias=v_la_dt_bias, la_out_norm_w=v_la_out_norm_w, la_w_out=v_la_w_out, sg_w_in=v_sg_w_in,
                 sg_ln_w=v_sg_ln_w, sg_ln_b=v_sg_ln_b, sg_w_s=v_sg_w_s, sg_b_s=v_sg_b_s, sg_w_out=v_sg_w_out,
                 ffn_w_up=v_ffn_w_up, ffn_w_down=v_ffn_w_down)
    order = list(weights)

    T, D = x.shape[1], x.shape[2]
    H = la_a_log.shape[1]
    HD = H * HEAD_DIM
    xi, yi, ci = _place()
    chip = 2 * xi + yi
    c_idx = jnp.reshape(ci, (1,)).astype(jnp.int32)
    j_idx = jnp.reshape(chip, (1,)).astype(jnp.int32)

    cs, csp = _la_shard_rows(H)
    la_rows = lambda a: jnp.pad(jnp.swapaxes(a, 1, 2), ((0, 0), (0, csp - cs), (0, 0)))
    shards = dict(la_in=la_rows(la_w_in)[0], la_out=la_w_out[0], sg_in=sg_w_in[0], sg_out=sg_w_out[0],
                  ffn_up0=ffn_w_up[0], ffn_up1=ffn_w_up[1], ffn_down0=ffn_w_down[0], ffn_down1=ffn_w_down[1])
    shard_halves = {w: _halves(a.astype(BF)) for w, a in shards.items()}
    small_shapes = [norm_w.shape, la_conv_w[0].shape, sg_ln_w[0].shape, sg_ln_b[0].shape]
    small = _pack_rows([norm_w, la_conv_w[0], sg_ln_w[0], sg_ln_b[0]])
    small = _halves(jnp.pad(small, ((0, (-small.shape[0]) % (2 * SUBLANES)), (0, 0))))
    first = [shard_halves["la_in"], shard_halves["la_out"], small]
    la_in_g, la_out_g, small_g = [g.reshape(N_CHIPS, -1, g.shape[-1])
                                  for g in run_comm(gather_comm(first), name="gather_first")]
    pieces = [_unpack_rows(small_g[kk], small_shapes) for kk in range(N_CHIPS)]
    la_main, la_gate = _la_weights_from_gathered(la_in_g, H)
    W = dict(
        norm_w=jnp.concatenate([p[0] for p in pieces], axis=-1),
        conv_w=jnp.concatenate([p[1] for p in pieces], axis=0),
        ln_w=jnp.concatenate([p[2] for p in pieces], axis=0),
        ln_b=jnp.concatenate([p[3] for p in pieces], axis=0),
        a_log=la_a_log[0], dt_bias=la_dt_bias[0], out_norm_w=la_out_norm_w[0], w_s=sg_w_s[0], b_s=sg_b_s[0],
        la_in_main=la_main, la_in_gate=la_gate, la_in_rows=la_in_g.reshape(-1, D),
        la_out=la_out_g.reshape(-1, la_out_g.shape[-1]), ffn_up=[None, None], ffn_down=[None, None],
    )
    plan = ExchangePlan(shard_halves, c_idx, j_idx)

    loss_local, dx, G = _train_local(x[0], loss_target[0], W, plan)
    loss = lax.psum(loss_local, ("x", "y", "c"))

    big_params = dict(la_w_in="la_in", la_w_out="la_out", sg_w_in="sg_in", sg_w_out="sg_out",
                      ffn_w_up="ffn_up", ffn_w_down="ffn_down")
    from_sib = dict(zip(big_params, pair_share([plan.finished[k] for k in big_params.values()],
                                               name="grads_pair_share")))

    def big_update(nm, key):
        rows = la_rows if nm == "la_w_in" else (lambda a: a)
        r4 = adamw_halves(rows(weights[nm]), plan.finished[key], from_sib[nm], rows(mom_m[nm]), rows(mom_v[nm]),
                          c_idx, name=f"adamw_{nm}")
        return [jnp.swapaxes(t[:, :cs], 1, 2) for t in r4] if nm == "la_w_in" else r4

    big_res = {nm: big_update(nm, key) for nm, key in big_params.items()}

    small_names = ["norm_w", "conv_w", "ln_w", "ln_b", "a_log", "dt_bias", "out_norm_w", "w_s", "b_s"]
    small_full = [G[nm] for nm in small_names]
    summed = _unpack_rows(all_reduce_small(_pack_rows(small_full), name="grads_all_reduce_small"),
                          [g.shape for g in small_full])
    sm = dict(zip(small_names, summed))
    own = lambda full, axis: lax.dynamic_slice_in_dim(full, chip * (full.shape[axis] // N_CHIPS),
                                                      full.shape[axis] // N_CHIPS, axis)
    grads = dict(
        norm_w=own(sm["norm_w"], 2), la_conv_w=own(sm["conv_w"], 0), la_a_log=sm["a_log"],
        la_dt_bias=sm["dt_bias"], la_out_norm_w=sm["out_norm_w"],
        sg_ln_w=own(sm["ln_w"], 0), sg_ln_b=own(sm["ln_b"], 0), sg_w_s=sm["w_s"], sg_b_s=sm["b_s"],
    )

    res = {nm: big_res[nm] if nm in big_res else
           _adamw_any(weights[nm], grads[nm], mom_m[nm], mom_v[nm], f"adamw_{nm}") for nm in order}
    return (loss, dx.reshape(x.shape), *[res[nm][0] for nm in order], *[res[nm][1] for nm in order],
            *[res[nm][2] for nm in order], *[res[nm][3] for nm in order])
```

```python
import math
from typing import Callable, NamedTuple

import numpy as np
import jax
import jax.numpy as jnp
from jax import lax
from jax.experimental import pallas as pl
from jax.experimental.pallas import tpu as pltpu

F32 = jnp.float32
BF = jnp.bfloat16

V7X_VMEM_BYTES = 64 * 1024 * 1024
VMEM_LIMIT = (V7X_VMEM_BYTES * 3) // 4
LANES = 128
SUBLANES = 8
BF16_ROWS = 16
HEAD_DIM = 128
LA_CHUNK = 64
SG_CHUNK = 128
CONV_WIDTH = 4
ROPE_BASE = 10000.0
EPS = 1e-6
L2_EPS = 1e-6
N_CHIPS = 4
N_DEV = 8

ADAM_LR = 0.001
ADAM_B1 = 0.9
ADAM_B2 = 0.999
ADAM_EPS = 1e-08
ADAM_WD = 0.01
ADAM_STEP = 10

MESH = pl.DeviceIdType.MESH
ANY = pl.BlockSpec(memory_space=pl.ANY)

NN = (((1,), (0,)), ((), ()))
NT = (((1,), (1,)), ((), ()))
TN = (((0,), (0,)), ((), ()))


def _pcall(body, **kw):
    return pl.pallas_call(body, **kw)


def _cp(n_axes):
    return pltpu.CompilerParams(dimension_semantics=("arbitrary",) * n_axes, vmem_limit_bytes=VMEM_LIMIT)


def _tile(n, pref, unit=LANES):
    if n <= pref:
        return n
    t = (pref // unit) * unit
    while t >= unit:
        if n % t == 0:
            return t
        t -= unit
    return n


def _dot(a, b, dims=NN):
    return lax.dot_general(a.astype(BF), b.astype(BF), dims, preferred_element_type=F32)


def _split_bf16(x):
    hi = x.astype(BF)
    return hi, (x - hi.astype(F32)).astype(BF)


def _dot_hi(a, b, dims=NN):
    ah, al = _split_bf16(a)
    bh, bl = _split_bf16(b)
    dot = lambda u, v: lax.dot_general(u, v, dims, preferred_element_type=F32)
    return dot(ah, bh) + (dot(ah, bl) + dot(al, bh))


def _sigmoid(x):
    return 1.0 / (1.0 + jnp.exp(-x))


def _silu(x):
    return x * _sigmoid(x)


def _dsilu(x):
    s = _sigmoid(x)
    return s * (1.0 + x * (1.0 - s))


GELU_C = math.sqrt(2.0 / math.pi)
GELU_A = 0.044715


def _gelu(x):
    return 0.5 * x * (1.0 + jnp.tanh(GELU_C * (x + GELU_A * x * x * x)))


def _gelu_and_grad(x):
    t = jnp.tanh(GELU_C * (x + GELU_A * x * x * x))
    return 0.5 * x * (1.0 + t), 0.5 * (1.0 + t) + 0.5 * x * (1.0 - t * t) * GELU_C * (1.0 + 3.0 * GELU_A * x * x)


class Comm(NamedTuple):
    ins: list
    outs: list
    nsem: int
    start: Callable
    finish: Callable


def _matmul(a, b, *, dims, grid, a_spec, b_spec, out_shape, out_spec, acc_shape, name,
            epilogue=None, extras=(), extra_specs=(), comm=None):
    nk = grid[2]
    outs = tuple(out_shape) if isinstance(out_shape, (tuple, list)) else (out_shape,)
    out_specs = tuple(out_spec) if isinstance(out_spec, (tuple, list)) else (out_spec,)
    n_ex, n_out = len(extras), len(outs)
    n_ci = len(comm.ins) if comm else 0
    n_co = len(comm.outs) if comm else 0

    def body(*refs):
        a_ref, b_ref = refs[0], refs[1]
        ex = refs[2:2 + n_ex]
        ci = refs[2 + n_ex:2 + n_ex + n_ci]
        o = refs[2 + n_ex + n_ci:2 + n_ex + n_ci + n_out]
        co = refs[2 + n_ex + n_ci + n_out:2 + n_ex + n_ci + n_out + n_co]
        scratch = refs[2 + n_ex + n_ci + n_out + n_co:]
        i, j, k = pl.program_id(0), pl.program_id(1), pl.program_id(2)

        if comm:
            send, recv = scratch[-2], scratch[-1]

            @pl.when((i == 0) & (j == 0) & (k == 0))
            def _():
                comm.start(ci, co, send, recv)

        part = lax.dot_general(a_ref[...].astype(BF), b_ref[...].astype(BF), dims, preferred_element_type=F32)

        def finish(val):
            res = epilogue(val, *[e[...] for e in ex]) if epilogue is not None else (val,)
            for r, oref in zip(res, o):
                oref[...] = r.astype(oref.dtype)

        if nk == 1:
            finish(part)
        else:
            acc = scratch[0]

            @pl.when(k == 0)
            def _():
                acc[...] = part

            @pl.when(k > 0)
            def _():
                acc[...] += part

            @pl.when(k == nk - 1)
            def _():
                finish(acc[...])

        if comm:
            @pl.when((i == grid[0] - 1) & (j == grid[1] - 1) & (k == nk - 1))
            def _():
                comm.finish(ci, co, send, recv)

    scratch_shapes = [pltpu.VMEM(acc_shape, F32)] if nk > 1 else []
    if comm:
        scratch_shapes += [pltpu.SemaphoreType.DMA((comm.nsem,)), pltpu.SemaphoreType.DMA((comm.nsem,))]
    res = _pcall(
        body, name=name, grid=grid,
        in_specs=[a_spec, b_spec, *extra_specs, *[ANY] * n_ci],
        out_specs=[*out_specs, *[ANY] * n_co],
        out_shape=[*outs, *(comm.outs if comm else [])],
        scratch_shapes=scratch_shapes,
        compiler_params=_cp(3),
    )(a, b, *extras, *(comm.ins if comm else []))
    main = res[0] if n_out == 1 else list(res[:n_out])
    return (main, list(res[n_out:])) if comm else main


def mm_nn(a, w, *, name, out_dtypes=(F32,), epilogue=None, extras=(), comm=None, tm=1024, tn=1024, tk=2048):
    M, K = a.shape
    if w.ndim == 3:
        S, _, Ns = w.shape
        N = S * Ns
    else:
        S, Ns = 1, w.shape[1]
        N = Ns
    tm, tn, tk = _tile(M, tm), _tile(Ns, tn), _tile(K, tk)
    npb = Ns // tn
    grid = (M // tm, N // tn, K // tk)
    a_spec = pl.BlockSpec((tm, tk), lambda i, j, k: (i, k))
    if w.ndim == 3:
        b_spec = pl.BlockSpec((None, tk, tn), lambda i, j, k: (j // npb, k, j % npb))
    else:
        b_spec = pl.BlockSpec((tk, tn), lambda i, j, k: (k, j))
    o_spec = pl.BlockSpec((tm, tn), lambda i, j, k: (i, j))
    outs = tuple(jax.ShapeDtypeStruct((M, N), d) for d in out_dtypes)
    res = _matmul(a, w, dims=NN, grid=grid, a_spec=a_spec, b_spec=b_spec,
                  out_shape=outs, out_spec=(o_spec,) * len(outs), acc_shape=(tm, tn), name=name,
                  epilogue=epilogue, extras=extras, extra_specs=(o_spec,) * len(extras), comm=comm)
    return res


def mm_nt(a, w, *, name, out_dtypes=(F32,), epilogue=None, extras=(), comm=None, tm=1024, tn=1024, tk=2048):
    M, Kc = a.shape
    if w.ndim == 3:
        S, Nout, Ks = w.shape
    else:
        S, (Nout, Ks) = 1, w.shape
    assert S * Ks == Kc
    tm, tn, tk = _tile(M, tm), _tile(Nout, tn), _tile(Ks, tk)
    kpb = Ks // tk
    grid = (M // tm, Nout // tn, Kc // tk)
    a_spec = pl.BlockSpec((tm, tk), lambda i, j, k: (i, k))
    if w.ndim == 3:
        b_spec = pl.BlockSpec((None, tn, tk), lambda i, j, k: (k // kpb, j, k % kpb))
    else:
        b_spec = pl.BlockSpec((tn, tk), lambda i, j, k: (j, k))
    o_spec = pl.BlockSpec((tm, tn), lambda i, j, k: (i, j))
    outs = tuple(jax.ShapeDtypeStruct((M, Nout), d) for d in out_dtypes)
    return _matmul(a, w, dims=NT, grid=grid, a_spec=a_spec, b_spec=b_spec,
                   out_shape=outs, out_spec=(o_spec,) * len(outs), acc_shape=(tm, tn), name=name,
                   epilogue=epilogue, extras=extras, extra_specs=(o_spec,) * len(extras), comm=comm)


def mm_tn(x, dy, *, name, shards=1, bf16_copy=False, tm=1024, tn=1024, tk=2048):
    T, Kin = x.shape
    N = dy.shape[1]
    Ns = N // shards
    tm, tn, tk = _tile(Kin, tm), _tile(Ns, tn), _tile(T, tk)
    npb = Ns // tn
    grid = (Kin // tm, N // tn, T // tk)
    a_spec = pl.BlockSpec((tk, tm), lambda i, j, k: (k, i))
    b_spec = pl.BlockSpec((tk, tn), lambda i, j, k: (k, j))
    if shards > 1:
        o_spec = pl.BlockSpec((None, tm, tn), lambda i, j, k: (j // npb, i, j % npb))
        out = jax.ShapeDtypeStruct((shards, Kin, Ns), F32)
    else:
        o_spec = pl.BlockSpec((tm, tn), lambda i, j, k: (i, j))
        out = jax.ShapeDtypeStruct((Kin, N), F32)
    if bf16_copy:
        return _matmul(x, dy, dims=TN, grid=grid, a_spec=a_spec, b_spec=b_spec,
                       out_shape=(out, jax.ShapeDtypeStruct(out.shape, BF)), out_spec=(o_spec, o_spec),
                       acc_shape=(tm, tn), name=name, epilogue=lambda acc: (acc, acc))
    return _matmul(x, dy, dims=TN, grid=grid, a_spec=a_spec, b_spec=b_spec,
                   out_shape=out, out_spec=o_spec, acc_shape=(tm, tn), name=name)


ROW_TILE = 256


def _rows(tr, d):
    return pl.BlockSpec((tr, d), lambda i: (i, 0))


def _fixed(shape):
    nd = len(shape)
    return pl.BlockSpec(shape, lambda *_: (0,) * nd)


def _rms(xv, w):
    r = lax.rsqrt(jnp.mean(xv * xv, axis=-1, keepdims=True) + EPS)
    return xv * r * w


def rms_fwd(x, w, *, name):
    T, D = x.shape
    tr = _tile(T, ROW_TILE, SUBLANES)

    def body(x_ref, w_ref, y_ref):
        y_ref[...] = _rms(x_ref[...], w_ref[...]).astype(y_ref.dtype)

    return _pcall(body, name=name, grid=(T // tr,), in_specs=[_rows(tr, D), _fixed((1, D))],
                  out_specs=_rows(tr, D), out_shape=jax.ShapeDtypeStruct((T, D), BF), compiler_params=_cp(1))(x, w)


def res_norm(h, m, wa, wb, *, name):
    T, D = h.shape
    tr = _tile(T, ROW_TILE, SUBLANES)

    def body(h_ref, m_ref, wa_ref, wb_ref, ho_ref, y_ref):
        ho = h_ref[...] + _rms(m_ref[...], wa_ref[...])
        ho_ref[...] = ho
        y_ref[...] = _rms(ho, wb_ref[...]).astype(y_ref.dtype)

    return _pcall(body, name=name, grid=(T // tr,),
                  in_specs=[_rows(tr, D), _rows(tr, D), _fixed((1, D)), _fixed((1, D))],
                  out_specs=[_rows(tr, D), _rows(tr, D)],
                  out_shape=[jax.ShapeDtypeStruct((T, D), F32), jax.ShapeDtypeStruct((T, D), BF)],
                  compiler_params=_cp(1))(h, m, wa, wb)


def _rms_bwd_rows(xv, w, dyv):
    r = lax.rsqrt(jnp.mean(xv * xv, axis=-1, keepdims=True) + EPS)
    xh = xv * r
    dyw = dyv * w
    return r * (dyw - xh * jnp.mean(dyw * xh, axis=-1, keepdims=True)), jnp.sum(dyv * xh, axis=0, keepdims=True)


def rms_bwd(x, w, dy, dres, *, name, inner=None):
    T, D = x.shape
    tr = _tile(T, ROW_TILE, SUBLANES)
    chained = inner is not None

    def body(*refs):
        if chained:
            x_ref, w_ref, dy_ref, dr_ref, m_ref, wa_ref, dx_ref, dw_ref, dm_ref, dwa_ref = refs
        else:
            x_ref, w_ref, dy_ref, dr_ref, dx_ref, dw_ref = refs
        i = pl.program_id(0)
        dx, part = _rms_bwd_rows(x_ref[...], w_ref[...], dy_ref[...].astype(F32))
        dx = dx + dr_ref[...]
        dx_ref[...] = dx
        if chained:
            dm, part_a = _rms_bwd_rows(m_ref[...], wa_ref[...], dx)
            dm_ref[...] = dm.astype(dm_ref.dtype)

        @pl.when(i == 0)
        def _():
            dw_ref[...] = part
            if chained:
                dwa_ref[...] = part_a

        @pl.when(i > 0)
        def _():
            dw_ref[...] += part
            if chained:
                dwa_ref[...] += part_a

    ins = [x, w, dy, dres] + (list(inner) if chained else [])
    in_specs = [_rows(tr, D), _fixed((1, D)), _rows(tr, D), _rows(tr, D)]
    out_specs = [_rows(tr, D), _fixed((1, D))]
    out_shape = [jax.ShapeDtypeStruct((T, D), F32), jax.ShapeDtypeStruct((1, D), F32)]
    if chained:
        in_specs += [_rows(tr, D), _fixed((1, D))]
        out_specs += [_rows(tr, D), _fixed((1, D))]
        out_shape += [jax.ShapeDtypeStruct((T, D), BF), jax.ShapeDtypeStruct((1, D), F32)]
    return _pcall(body, name=name, grid=(T // tr,), in_specs=in_specs, out_specs=out_specs, out_shape=out_shape,
                  compiler_params=_cp(1))(*ins)


def last_norm_and_loss(h, m, w, tgt, *, name):
    T, D = h.shape
    tr = _tile(T, ROW_TILE, SUBLANES)

    def body(h_ref, m_ref, w_ref, t_ref, dy_ref, dm_ref, dw_ref, l_ref):
        i = pl.program_id(0)
        mv = m_ref[...]
        r = lax.rsqrt(jnp.mean(mv * mv, axis=-1, keepdims=True) + EPS)
        xh = mv * r
        e = h_ref[...] + xh * w_ref[...] - t_ref[...]
        dy = e * (1.0 / D)
        dy_ref[...] = dy
        dyw = dy * w_ref[...]
        dm_ref[...] = (r * (dyw - xh * jnp.mean(dyw * xh, axis=-1, keepdims=True))).astype(dm_ref.dtype)
        pw = jnp.sum(dy * xh, axis=0, keepdims=True)
        pl_ = 0.5 * jnp.sum(jnp.mean(e * e, axis=-1, keepdims=True), axis=0, keepdims=True)
        pl_ = jnp.broadcast_to(pl_, (1, LANES))

        @pl.when(i == 0)
        def _():
            dw_ref[...] = pw
            l_ref[...] = pl_

        @pl.when(i > 0)
        def _():
            dw_ref[...] += pw
            l_ref[...] += pl_

    return _pcall(body, name=name, grid=(T // tr,),
                  in_specs=[_rows(tr, D), _rows(tr, D), _fixed((1, D)), _rows(tr, D)],
                  out_specs=[_rows(tr, D), _rows(tr, D), _fixed((1, D)), _fixed((1, LANES))],
                  out_shape=[jax.ShapeDtypeStruct((T, D), F32), jax.ShapeDtypeStruct((T, D), BF),
                             jax.ShapeDtypeStruct((1, D), F32), jax.ShapeDtypeStruct((1, LANES), F32)],
                  compiler_params=_cp(1))(h, m, w, tgt)


def adamw(w, g, m, v, *, name):
    R, C = w.shape
    tr = _tile(R, max(SUBLANES, (1 << 18) // C), SUBLANES)
    c1 = 1.0 - ADAM_B1 ** ADAM_STEP
    c2 = 1.0 - ADAM_B2 ** ADAM_STEP

    def body(w_ref, g_ref, m_ref, v_ref, d_ref, mo_ref, vo_ref):
        gv = g_ref[...]
        m2 = ADAM_B1 * m_ref[...] + (1.0 - ADAM_B1) * gv
        v2 = ADAM_B2 * v_ref[...] + (1.0 - ADAM_B2) * (gv * gv)
        d_ref[...] = -ADAM_LR * ((m2 / c1) / (jnp.sqrt(v2 / c2) + ADAM_EPS) + ADAM_WD * w_ref[...])
        mo_ref[...] = m2
        vo_ref[...] = v2

    spec = _rows(tr, C)
    sds = jax.ShapeDtypeStruct((R, C), F32)
    return _pcall(body, name=name, grid=(R // tr,), in_specs=[spec] * 4, out_specs=[spec] * 3,
                  out_shape=[sds] * 3, compiler_params=_cp(1))(w, g, m, v)


HALO = SUBLANES


def _conv_down(xx, w_ref):
    acc = xx * w_ref[pl.ds(CONV_WIDTH - 1, 1), :]
    for d in range(1, CONV_WIDTH):
        acc = acc + pltpu.roll(xx, d, 0) * w_ref[pl.ds(CONV_WIDTH - 1 - d, 1), :]
    return acc


def _conv_tile(x_ref, halo_ref, w_ref, first):
    xs = x_ref[...]
    hal = jnp.where(first, 0.0, halo_ref[...])
    cat = jnp.concatenate([hal, xs[0:HALO]], axis=0)
    return jnp.concatenate([_conv_down(cat, w_ref)[HALO:2 * HALO], _conv_down(xs, w_ref)[HALO:]], axis=0)


def _shift_down_tile(x_ref, halo_ref, first, d):
    xs = x_ref[...]
    if d == 0:
        return xs
    hal = jnp.where(first, 0.0, halo_ref[...])
    cat = jnp.concatenate([hal, xs[0:HALO]], axis=0)
    return jnp.concatenate([pltpu.roll(cat, d, 0)[HALO:2 * HALO], pltpu.roll(xs, d, 0)[HALO:]], axis=0)


def _l2n(s):
    return s * lax.rsqrt(jnp.sum(s * s, axis=-1, keepdims=True) + L2_EPS)


PREP_ROWS = 512


def _l2n_groups(s, nb):
    return jnp.concatenate([_l2n(s[:, g * LANES:(g + 1) * LANES]) for g in range(nb)], axis=1)


def prep_fwd(pm, off, wc8, woff, nblk, l2, *, name):
    T = pm.shape[0]
    tr = _tile(T, PREP_ROWS, SUBLANES)
    hb = tr // HALO
    wb = _heads_per_step(nblk)
    wl = wb * LANES

    def body(x_ref, halo_ref, w_ref, o_ref):
        i = pl.program_id(0)
        s = _silu(_conv_tile(x_ref, halo_ref, w_ref, i == 0))
        o_ref[...] = _l2n_groups(s, wb) if l2 else s

    return _pcall(
        body, name=name, grid=(T // tr, nblk // wb),
        in_specs=[pl.BlockSpec((tr, wl), lambda i, c: (i, off // wb + c)),
                  pl.BlockSpec((HALO, wl), lambda i, c: (jnp.maximum(i * hb - 1, 0), off // wb + c)),
                  pl.BlockSpec((SUBLANES, wl), lambda i, c: (0, woff // wb + c))],
        out_specs=pl.BlockSpec((tr, wl), lambda i, c: (i, c)),
        out_shape=jax.ShapeDtypeStruct((T, nblk * LANES), F32), compiler_params=_cp(2))(pm, pm, wc8)


def prep_bwd_act(pm, off, wc8, woff, nblk, l2, dout, *, name):
    T = pm.shape[0]
    tr = _tile(T, PREP_ROWS, SUBLANES)
    hb = tr // HALO
    wb = _heads_per_step(nblk)
    wl = wb * LANES

    def l2_bwd(s, do):
        r = lax.rsqrt(jnp.sum(s * s, axis=-1, keepdims=True) + L2_EPS)
        nrm = s * r
        return r * (do - nrm * jnp.sum(do * nrm, axis=-1, keepdims=True))

    def body(x_ref, halo_ref, w_ref, do_ref, dc_ref, dw_ref):
        i = pl.program_id(1)
        first = i == 0
        y = _conv_tile(x_ref, halo_ref, w_ref, first)
        s = _silu(y)
        do = do_ref[...]
        if l2:
            ds = jnp.concatenate([l2_bwd(s[:, g * LANES:(g + 1) * LANES], do[:, g * LANES:(g + 1) * LANES])
                                  for g in range(wb)], axis=1)
        else:
            ds = do
        dc = ds * _dsilu(y)
        dc_ref[...] = dc

        @pl.when(first)
        def _():
            dw_ref[...] = jnp.zeros_like(dw_ref)

        for j in range(CONV_WIDTH):
            xsh = _shift_down_tile(x_ref, halo_ref, first, CONV_WIDTH - 1 - j)
            dw_ref[pl.ds(j, 1), :] += jnp.sum(dc * xsh, axis=0, keepdims=True)

    return _pcall(
        body, name=name, grid=(nblk // wb, T // tr),
        in_specs=[pl.BlockSpec((tr, wl), lambda c, i: (i, off // wb + c)),
                  pl.BlockSpec((HALO, wl), lambda c, i: (jnp.maximum(i * hb - 1, 0), off // wb + c)),
                  pl.BlockSpec((SUBLANES, wl), lambda c, i: (0, woff // wb + c)),
                  pl.BlockSpec((tr, wl), lambda c, i: (i, c))],
        out_specs=[pl.BlockSpec((tr, wl), lambda c, i: (i, c)),
                   pl.BlockSpec((SUBLANES, wl), lambda c, i: (0, c))],
        out_shape=[jax.ShapeDtypeStruct((T, nblk * LANES), F32),
                   jax.ShapeDtypeStruct((SUBLANES, nblk * LANES), F32)],
        compiler_params=_cp(2))(pm, pm, wc8, dout)


def prep_bwd_conv(dc, wc8, woff, nblk, *, name):
    T = dc.shape[0]
    tr = _tile(T, PREP_ROWS, SUBLANES)
    hb = tr // HALO
    nt = T // tr
    last_halo = T // HALO - 1
    wb = _heads_per_step(nblk)
    wl = wb * LANES

    def up(xx, w_ref):
        rows = xx.shape[0]
        acc = xx * w_ref[pl.ds(CONV_WIDTH - 1, 1), :]
        for d in range(1, CONV_WIDTH):
            acc = acc + pltpu.roll(xx, rows - d, 0) * w_ref[pl.ds(CONV_WIDTH - 1 - d, 1), :]
        return acc

    def body(x_ref, halo_ref, w_ref, o_ref):
        i = pl.program_id(0)
        xs = x_ref[...]
        hal = jnp.where(i == nt - 1, 0.0, halo_ref[...])
        cat = jnp.concatenate([xs[tr - HALO:tr], hal], axis=0)
        out = jnp.concatenate([up(xs, w_ref)[:tr - HALO], up(cat, w_ref)[0:HALO]], axis=0)
        o_ref[...] = out.astype(o_ref.dtype)

    return _pcall(
        body, name=name, grid=(nt, nblk // wb),
        in_specs=[pl.BlockSpec((tr, wl), lambda i, c: (i, c)),
                  pl.BlockSpec((HALO, wl), lambda i, c: (jnp.minimum((i + 1) * hb, last_halo), c)),
                  pl.BlockSpec((SUBLANES, wl), lambda i, c: (0, woff // wb + c))],
        out_specs=pl.BlockSpec((tr, wl), lambda i, c: (i, c)),
        out_shape=jax.ShapeDtypeStruct((T, nblk * LANES), BF), compiler_params=_cp(2))(dc, dc, wc8)


def _softplus(x):
    return jnp.maximum(x, 0.0) + jnp.log(1.0 + jnp.exp(-jnp.abs(x)))


def _tril_ones(c):
    t = lax.broadcasted_iota(jnp.int32, (c, c), 0)
    s = lax.broadcasted_iota(jnp.int32, (c, c), 1)
    return (t >= s).astype(F32)


def _triu_ones(c):
    t = lax.broadcasted_iota(jnp.int32, (c, c), 0)
    s = lax.broadcasted_iota(jnp.int32, (c, c), 1)
    return (t <= s).astype(F32)


def gates_fwd(pg, arow, dtrow, H, *, name):
    T = pg.shape[0]
    C = LA_CHUNK
    N = T // C

    def body(x_ref, a_ref, dt_ref, bg_ref, gr_ref):
        x = x_ref[...]
        lane = lax.broadcasted_iota(jnp.int32, (C, LANES), 1)
        g = -jnp.exp(a_ref[...]) * _softplus(x + dt_ref[...])
        g = jnp.where((lane >= H) & (lane < 2 * H), g, 0.0)
        lm = _tril_ones(C)
        gc = _dot_hi(lm, g)
        bg_ref[...] = jnp.where(lane < H, _sigmoid(x), gc)
        gr_ref[...] = _dot_hi(g, _triu_ones(C), TN)

    return _pcall(
        body, name=name, grid=(N,),
        in_specs=[pl.BlockSpec((C, LANES), lambda n: (n, 0)), _fixed((1, LANES)), _fixed((1, LANES))],
        out_specs=[pl.BlockSpec((C, LANES), lambda n: (n, 0)), pl.BlockSpec((None, LANES, C), lambda n: (n, 0, 0))],
        out_shape=[jax.ShapeDtypeStruct((T, LANES), F32), jax.ShapeDtypeStruct((N, LANES, C), F32)],
        compiler_params=_cp(1))(pg, arow, dtrow)


def gates_bwd(pg, arow, dtrow, dbg, H, *, name):
    T = pg.shape[0]
    C = LA_CHUNK
    N = T // C

    def body(x_ref, a_ref, dt_ref, d_ref, dx_ref, da_ref, ddt_ref):
        n = pl.program_id(0)
        x = x_ref[...]
        d = d_ref[...]
        lane = lax.broadcasted_iota(jnp.int32, (C, LANES), 1)
        in_g = (lane >= H) & (lane < 2 * H)
        e = jnp.exp(a_ref[...])
        xs = x + dt_ref[...]
        g = -e * _softplus(xs)
        dg = _dot_hi(_tril_ones(C), jnp.where(in_g, d, 0.0), TN)
        dxs = jnp.where(in_g, dg * (-e) * _sigmoid(xs), 0.0)
        beta = _sigmoid(x)
        dx_ref[...] = jnp.where(lane < H, d * beta * (1.0 - beta), dxs).astype(dx_ref.dtype)
        pa = jnp.sum(jnp.where(in_g, dg * g, 0.0), axis=0, keepdims=True)
        pd = jnp.sum(dxs, axis=0, keepdims=True)

        @pl.when(n == 0)
        def _():
            da_ref[...] = pa
            ddt_ref[...] = pd

        @pl.when(n > 0)
        def _():
            da_ref[...] += pa
            ddt_ref[...] += pd

    return _pcall(
        body, name=name, grid=(N,),
        in_specs=[pl.BlockSpec((C, LANES), lambda n: (n, 0)), _fixed((1, LANES)), _fixed((1, LANES)),
                  pl.BlockSpec((C, LANES), lambda n: (n, 0))],
        out_specs=[pl.BlockSpec((C, LANES), lambda n: (n, 0)), _fixed((1, LANES)), _fixed((1, LANES))],
        out_shape=[jax.ShapeDtypeStruct((T, LANES), BF), jax.ShapeDtypeStruct((1, LANES), F32),
                   jax.ShapeDtypeStruct((1, LANES), F32)],
        compiler_params=_cp(1))(pg, arow, dtrow, dbg)


QK_SCALE = HEAD_DIM ** -0.5


HEADS_PER_STEP = 8


def _heads_per_step(H):
    hb = HEADS_PER_STEP
    while H % hb:
        hb //= 2
    return hb


def _head_rstd(o):
    return lax.rsqrt(jnp.mean(o * o, axis=-1, keepdims=True) + EPS)


def _gdn_gates(bg_ref, gr_ref, h, H):
    C = LA_CHUNK
    bgv = bg_ref[...]
    lane = lax.broadcasted_iota(jnp.int32, (C, LANES), 1)
    beta = jnp.sum(jnp.where(lane == h, bgv, 0.0), axis=1, keepdims=True)
    gc = jnp.sum(jnp.where(lane == H + h, bgv, 0.0), axis=1, keepdims=True)
    grow = gr_ref[pl.ds(H + h, 1), :]
    ri = lax.broadcasted_iota(jnp.int32, (C, 1), 0)
    gl = jnp.sum(jnp.where(ri == C - 1, gc, 0.0), axis=0, keepdims=True)
    return beta, gc, grow, gl


def _chunk_masks():
    C = LA_CHUNK
    ti = lax.broadcasted_iota(jnp.int32, (C, C), 0)
    si = lax.broadcasted_iota(jnp.int32, (C, C), 1)
    return ti >= si, ti > si, ti == si


def _decay(gc, grow, causal):
    return jnp.where(causal, jnp.exp(jnp.where(causal, gc - grow, 0.0)), 0.0)


def _interleave(gens):
    gens = list(gens)
    results = [None] * len(gens)
    live = list(range(len(gens)))
    while live:
        still = []
        for i in live:
            try:
                next(gens[i])
                still.append(i)
            except StopIteration as stop:
                results[i] = stop.value
        live = still
    return results


def _unit_lower_inverse(a, eye):
    x = -a
    p = jnp.where(eye, 1.0, 0.0) + x
    for _ in range(5):
        x = _dot_hi(x, x)
        yield
        p = p + _dot_hi(p, x)
        yield
    return p


def gdn_fwd(q, k, v, pm, zoff, bg, gcrow, wn, H, *, name):
    T = q.shape[0]
    C = LA_CHUNK
    N = T // C
    hd = HEAD_DIM

    HB = _heads_per_step(H)

    def body(q_ref, k_ref, v_ref, z_ref, bg_ref, gr_ref, wn_ref, og_ref, or_ref, sall_ref, tall_ref, S):
        n = pl.program_id(0)
        hg = pl.program_id(1)
        causal, strict, eye = _chunk_masks()

        @pl.when((n == 0) & (hg == 0))
        def _():
            S[...] = jnp.zeros_like(S)

        states = [S[hg * HB + i] for i in range(HB)]

        def head(i):
            h = hg * HB + i
            sl = slice(i * hd, (i + 1) * hd)
            beta, gc, grow, gl = _gdn_gates(bg_ref, gr_ref, h, H)
            dm = _decay(gc, grow, causal)
            qs = q_ref[:, sl] * QK_SCALE
            kk = k_ref[:, sl]
            vv = v_ref[:, sl]
            eg = jnp.exp(gc)
            kb = kk * beta
            a = jnp.where(strict, _dot(kb, kk, NT) * dm, 0.0)
            yield
            tm = yield from _unit_lower_inverse(a, eye)
            u = _dot(tm, vv * beta)
            w = _dot(tm, kb * eg)
            qk = jnp.where(causal, _dot(qs, kk, NT) * dm, 0.0)
            yield
            s0 = states[i]
            vnew = u - _dot(w, s0)
            o = _dot(qs * eg, s0)
            yield
            o = o + _dot(qk, vnew)
            s1 = s0 * jnp.exp(gl) + _dot(kk * jnp.exp(gl - gc), vnew, TN)
            yield
            sall_ref[i] = s0
            tall_ref[i] = tm
            or_ref[:, sl] = o
            og_ref[:, sl] = (o * _head_rstd(o) * wn_ref[...] * _silu(z_ref[:, sl])).astype(og_ref.dtype)
            return s1

        for i, s1 in enumerate(_interleave([head(i) for i in range(HB)])):
            S[hg * HB + i] = s1

    blk = lambda off: pl.BlockSpec((C, HB * hd), lambda n, h: (n, off // HB + h))
    return _pcall(
        body, name=name, grid=(N, H // HB),
        in_specs=[blk(0), blk(0), blk(0), blk(zoff),
                  pl.BlockSpec((C, LANES), lambda n, h: (n, 0)),
                  pl.BlockSpec((None, LANES, C), lambda n, h: (n, 0, 0)),
                  _fixed((1, hd))],
        out_specs=[blk(0), blk(0),
                   pl.BlockSpec((None, HB, hd, hd), lambda n, h: (n, h, 0, 0)),
                   pl.BlockSpec((None, HB, C, C), lambda n, h: (n, h, 0, 0))],
        out_shape=[jax.ShapeDtypeStruct((T, 2 * H * hd), BF),
                   jax.ShapeDtypeStruct((T, H * hd), F32),
                   jax.ShapeDtypeStruct((N, H, hd, hd), F32), jax.ShapeDtypeStruct((N, H, C, C), F32)],
        scratch_shapes=[pltpu.VMEM((H, hd, hd), F32)],
        compiler_params=_cp(2))(q, k, v, pm, bg, gcrow, wn)


def gdn_bwd(q, k, v, pm, zoff, bg, gcrow, wn, oraw, sall, tall, dog, H, *, name):
    T = q.shape[0]
    C = LA_CHUNK
    N = T // C
    hd = HEAD_DIM

    HB = _heads_per_step(H)

    def body(*refs):
        dbg_ref, dwn_ref, dS = refs[15], refs[16], refs[17]
        n = pl.program_id(0)
        hg = pl.program_id(1)

        @pl.when((n == 0) & (hg == 0))
        def _():
            dwn_ref[...] = jnp.zeros_like(dwn_ref)
            dS[...] = jnp.zeros_like(dS)

        @pl.when(hg == 0)
        def _():
            dbg_ref[...] = jnp.zeros_like(dbg_ref)

        ds_in = [dS[hg * HB + i] for i in range(HB)]
        outs = _interleave([head(i, hg * HB + i, ds_in[i], *refs) for i in range(HB)])
        for i in range(HB):
            dS[hg * HB + i] = outs[i][0]
        dwn_ref[...] += sum(o[1] for o in outs)
        dbg_ref[...] += sum(o[2] for o in outs)

    def head(i, h, ds1, q_ref, k_ref, v_ref, z_ref, bg_ref, gr_ref, wn_ref, or_ref, sall_ref, tall_ref, dog_ref,
             dq_ref, dk_ref, dv_ref, dz_ref, dbg_ref, dwn_ref, dS):
        sl = slice(i * hd, (i + 1) * hd)
        beta, gc, grow, gl = _gdn_gates(bg_ref, gr_ref, h, H)
        causal, strict, eye = _chunk_masks()
        dm = _decay(gc, grow, causal)
        qs = q_ref[:, sl] * QK_SCALE
        kk = k_ref[:, sl]
        vv = v_ref[:, sl]
        zz = z_ref[:, sl]
        o = or_ref[:, sl]
        dog = dog_ref[:, sl]
        wn_v = wn_ref[...]
        s0 = sall_ref[i]
        tm = tall_ref[i]

        rstd = _head_rstd(o)
        on = o * rstd
        sz = _silu(zz)
        don = dog * wn_v * sz
        dwn_part = jnp.sum(dog * on * sz, axis=0, keepdims=True)
        dz_ref[:, sl] = (dog * on * wn_v * _dsilu(zz)).astype(dz_ref.dtype)
        do = rstd * (don - on * jnp.mean(don * on, axis=-1, keepdims=True))

        eg = jnp.exp(gc)
        kb = kk * beta
        vb = vv * beta
        kbg = kb * eg
        a = jnp.where(strict, _dot(kb, kk, NT) * dm, 0.0)
        u = _dot(tm, vb)
        w = _dot(tm, kbg)
        qk = jnp.where(causal, _dot(qs, kk, NT) * dm, 0.0)
        dqdec = _dot(do, s0, NT)
        yield
        vnew = u - _dot(w, s0)
        qdec = qs * eg
        etail = jnp.exp(gl - gc)
        ktail = kk * etail
        egl = jnp.exp(gl)
        dvnew = _dot(qk, do, TN) + _dot(ktail, ds1)
        yield
        dqk = jnp.where(causal, _dot(do, vnew, NT), 0.0)
        dktail = _dot(vnew, ds1, NT)
        dcd = jnp.sum(jnp.sum(s0 * ds1, axis=1, keepdims=True), axis=0, keepdims=True)
        ds0 = egl * ds1 + _dot(qdec, do, TN) - _dot(w, dvnew, TN)
        dw = -_dot(dvnew, s0, NT)
        dvb = _dot(tm, dvnew, TN)
        yield
        dkbg = _dot(tm, dw, TN)
        dtm = _dot(dvnew, vb, NT) + _dot(dw, kbg, NT)
        dqkr = dqk * dm
        dqs = _dot(dqkr, kk) + dqdec * eg
        yield
        x = _dot_hi(tm, dtm, TN)
        yield
        da = jnp.where(strict, -_dot_hi(x, tm, NT), 0.0)
        yield
        dkk = da * dm
        dkb = _dot(dkk, kk) + dkbg * eg
        dk = _dot(dkk, kb, TN)
        dk = dk + _dot(dqkr, qs, TN) + dktail * etail + dkb * beta
        g = da * a + dqk * qk
        colsum = jnp.max(_dot_hi(g, jnp.ones((C, LANES), F32), TN), axis=1, keepdims=True)
        yield
        rk = jnp.sum(dktail * ktail, axis=1, keepdims=True)
        dgc = (jnp.sum(g, axis=1, keepdims=True) - colsum
               + jnp.sum(dqdec * qdec, axis=1, keepdims=True) - rk
               + jnp.sum(dkbg * kbg, axis=1, keepdims=True))
        dgl = jnp.sum(rk, axis=0, keepdims=True) + dcd * egl
        ri = lax.broadcasted_iota(jnp.int32, (C, 1), 0)
        dgc = dgc + jnp.where(ri == C - 1, dgl, 0.0)
        dbeta = jnp.sum(dkb * kk, axis=1, keepdims=True) + jnp.sum(dvb * vv, axis=1, keepdims=True)

        dq_ref[:, sl] = dqs * QK_SCALE
        dk_ref[:, sl] = dk
        dv_ref[:, sl] = dvb * beta
        lane = lax.broadcasted_iota(jnp.int32, (C, LANES), 1)
        return ds0, dwn_part, jnp.where(lane == h, dbeta, 0.0) + jnp.where(lane == H + h, dgc, 0.0)

    blk = lambda off: pl.BlockSpec((C, HB * hd), lambda n, h: (N - 1 - n, off // HB + h))
    st = lambda r: pl.BlockSpec((None, HB, r, r), lambda n, h: (N - 1 - n, h, 0, 0))
    return _pcall(
        body, name=name, grid=(N, H // HB),
        in_specs=[blk(0), blk(0), blk(0), blk(zoff),
                  pl.BlockSpec((C, LANES), lambda n, h: (N - 1 - n, 0)),
                  pl.BlockSpec((None, LANES, C), lambda n, h: (N - 1 - n, 0, 0)),
                  _fixed((1, hd)), blk(0), st(hd), st(C), blk(0)],
        out_specs=[blk(0), blk(0), blk(0), blk(0),
                   pl.BlockSpec((C, LANES), lambda n, h: (N - 1 - n, 0)), _fixed((1, hd))],
        out_shape=[jax.ShapeDtypeStruct((T, H * hd), F32)] * 3
        + [jax.ShapeDtypeStruct((T, H * hd), BF), jax.ShapeDtypeStruct((T, LANES), F32),
           jax.ShapeDtypeStruct((1, hd), F32)],
        scratch_shapes=[pltpu.VMEM((H, hd, hd), F32)],
        compiler_params=_cp(2))(q, k, v, pm, bg, gcrow, wn, oraw, sall, tall, dog)


def _rot(x, cs, sn):
    return x * cs + pltpu.roll(x, HEAD_DIM // 2, 1) * sn


def _rot_t(dy, cs, sn):
    return dy * cs + pltpu.roll(dy * sn, HEAD_DIM // 2, 1)


def ret_fwd(pm, qoff, koff, voff, goff, cs, sn, dmat, avec, bvec, gam, og_buf, H, *, name):
    T = pm.shape[0]
    C = LA_CHUNK
    N = T // C
    hd = HEAD_DIM

    HB = _heads_per_step(H)

    def body(q_ref, k_ref, v_ref, g_ref, cs_ref, sn_ref, dm_ref, a_ref, b_ref, gam_ref, _og_in,
             og_ref, or_ref, sall_ref, S):
        n = pl.program_id(0)
        hg = pl.program_id(1)
        c, s = cs_ref[...], sn_ref[...]

        @pl.when((n == 0) & (hg == 0))
        def _():
            S[...] = jnp.zeros_like(S)

        states = [S[hg * HB + i] for i in range(HB)]

        def head(i):
            sl = slice(i * hd, (i + 1) * hd)
            qq = _rot(q_ref[:, sl], c, s)
            kk = _rot(k_ref[:, sl], c, s) * QK_SCALE
            vv = v_ref[:, sl]
            s0 = states[i]
            p = _dot(qq, kk, NT) * dm_ref[i]
            cross = _dot(qq * a_ref[i], s0)
            s1 = s0 * gam_ref[i] + _dot(kk * b_ref[i], vv, TN)
            yield
            o = _dot(p, vv) + cross
            yield
            sall_ref[i] = s0
            or_ref[:, sl] = o
            og_ref[:, sl] = (_silu(g_ref[:, sl]) * o * _head_rstd(o)).astype(og_ref.dtype)
            return s1

        for i, s1 in enumerate(_interleave([head(i) for i in range(HB)])):
            S[hg * HB + i] = s1

    blk = lambda off: pl.BlockSpec((C, HB * hd), lambda n, h: (n, off // HB + h))
    tab = pl.BlockSpec((C, hd), lambda n, h: (n, 0))
    per_h = lambda r, cdim: pl.BlockSpec((HB, r, cdim), lambda n, h: (h, 0, 0))
    return _pcall(
        body, name=name, grid=(N, H // HB),
        in_specs=[blk(qoff), blk(koff), blk(voff), blk(goff), tab, tab,
                  per_h(C, C), per_h(C, hd), per_h(C, hd), per_h(1, hd), ANY],
        out_specs=[blk(H), blk(0), pl.BlockSpec((None, HB, hd, hd), lambda n, h: (n, h, 0, 0))],
        out_shape=[jax.ShapeDtypeStruct(og_buf.shape, og_buf.dtype), jax.ShapeDtypeStruct((T, H * hd), F32),
                   jax.ShapeDtypeStruct((N, H, hd, hd), F32)],
        input_output_aliases={10: 0},
        scratch_shapes=[pltpu.VMEM((H, hd, hd), F32)],
        compiler_params=_cp(2))(pm, pm, pm, pm, cs, sn, dmat, avec, bvec, gam, og_buf)


def ret_bwd(pm, qoff, koff, voff, goff, cs, sn, dmat, avec, bvec, gam, oraw, sall, dog, dogoff, H, *, name):
    T = pm.shape[0]
    C = LA_CHUNK
    N = T // C
    hd = HEAD_DIM

    HB = _heads_per_step(H)

    def body(q_ref, k_ref, v_ref, g_ref, cs_ref, sn_ref, dm_ref, a_ref, b_ref, gam_ref, or_ref, sall_ref,
             dog_ref, dq_ref, dk_ref, dv_ref, dg_ref, dS):
        n = pl.program_id(0)
        hg = pl.program_id(1)
        c, s = cs_ref[...], sn_ref[...]

        @pl.when((n == 0) & (hg == 0))
        def _():
            dS[...] = jnp.zeros_like(dS)

        dstates = [dS[hg * HB + i] for i in range(HB)]

        def head(i):
            sl = slice(i * hd, (i + 1) * hd)
            qq = _rot(q_ref[:, sl], c, s)
            kk = _rot(k_ref[:, sl], c, s) * QK_SCALE
            vv = v_ref[:, sl]
            gg = g_ref[:, sl]
            o = or_ref[:, sl]
            dog = dog_ref[:, sl]
            dm = dm_ref[i]
            av, bv = a_ref[i], b_ref[i]
            s0 = sall_ref[i]
            ds1 = dstates[i]

            rstd = _head_rstd(o)
            on = o * rstd
            don = dog * _silu(gg)
            dg_ref[:, sl] = (dog * on * _dsilu(gg)).astype(dg_ref.dtype)
            do = rstd * (don - on * jnp.mean(don * on, axis=-1, keepdims=True))

            p = _dot(qq, kk, NT) * dm
            dp = _dot(do, vv, NT) * dm
            cross_q = _dot(do, s0, NT) * av
            cross_k = _dot(vv, ds1, NT) * bv
            cross_v = _dot(kk * bv, ds1)
            ds0 = ds1 * gam_ref[i] + _dot(qq * av, do, TN)
            yield
            dv_ref[:, sl] = (_dot(p, do, TN) + cross_v).astype(dv_ref.dtype)
            dqq = _dot(dp, kk) + cross_q
            dkk = (_dot(dp, qq, TN) + cross_k) * QK_SCALE
            yield
            dq_ref[:, sl] = _rot_t(dqq, c, s).astype(dq_ref.dtype)
            dk_ref[:, sl] = _rot_t(dkk, c, s).astype(dk_ref.dtype)
            return ds0

        for i, ds0 in enumerate(_interleave([head(i) for i in range(HB)])):
            dS[hg * HB + i] = ds0

    blk = lambda off: pl.BlockSpec((C, HB * hd), lambda n, h: (N - 1 - n, off // HB + h))
    tab = pl.BlockSpec((C, hd), lambda n, h: (N - 1 - n, 0))
    per_h = lambda r, cdim: pl.BlockSpec((HB, r, cdim), lambda n, h: (h, 0, 0))
    return _pcall(
        body, name=name, grid=(N, H // HB),
        in_specs=[blk(qoff), blk(koff), blk(voff), blk(goff), tab, tab,
                  per_h(C, C), per_h(C, hd), per_h(C, hd), per_h(1, hd), blk(0),
                  pl.BlockSpec((None, HB, hd, hd), lambda n, h: (N - 1 - n, h, 0, 0)), blk(dogoff)],
        out_specs=[blk(0)] * 4,
        out_shape=[jax.ShapeDtypeStruct((T, H * hd), BF)] * 4,
        scratch_shapes=[pltpu.VMEM((H, hd, hd), F32)],
        compiler_params=_cp(2))(pm, pm, pm, pm, cs, sn, dmat, avec, bvec, gam, oraw, sall, dog)


LN_ROWS = 128


def ln_fwd(pre, lw, lb, *, name):
    T, W2 = pre.shape
    W = W2 // 2
    tr = _tile(T, LN_ROWS, SUBLANES)

    def body(p_ref, w_ref, b_ref, o_ref):
        v = _gelu(p_ref[...])
        xc = v - jnp.mean(v, axis=-1, keepdims=True)
        r = lax.rsqrt(jnp.mean(xc * xc, axis=-1, keepdims=True) + EPS)
        o_ref[...] = xc * r * w_ref[...] + b_ref[...]

    return _pcall(body, name=name, grid=(T // tr,),
                  in_specs=[pl.BlockSpec((tr, W), lambda i: (i, 1)), _fixed((1, W)), _fixed((1, W))],
                  out_specs=_rows(tr, W), out_shape=jax.ShapeDtypeStruct((T, W), F32),
                  compiler_params=_cp(1))(pre, lw, lb)


def ln_bwd(pre, lw, dvn, dpre_buf, *, name):
    T, W2 = pre.shape
    W = W2 // 2
    tr = _tile(T, LN_ROWS, SUBLANES)

    def body(p_ref, w_ref, d_ref, _dp_in, dp_ref, dw_ref, db_ref):
        i = pl.program_id(0)
        v, dgelu = _gelu_and_grad(p_ref[...])
        xc = v - jnp.mean(v, axis=-1, keepdims=True)
        r = lax.rsqrt(jnp.mean(xc * xc, axis=-1, keepdims=True) + EPS)
        xh = xc * r
        d = d_ref[...]
        dxh = d * w_ref[...]
        dv = r * (dxh - jnp.mean(dxh, axis=-1, keepdims=True) - xh * jnp.mean(dxh * xh, axis=-1, keepdims=True))
        dp_ref[...] = (dv * dgelu).astype(dp_ref.dtype)
        pw = jnp.sum(d * xh, axis=0, keepdims=True)
        pb = jnp.sum(d, axis=0, keepdims=True)

        @pl.when(i == 0)
        def _():
            dw_ref[...] = pw
            db_ref[...] = pb

        @pl.when(i > 0)
        def _():
            dw_ref[...] += pw
            db_ref[...] += pb

    return _pcall(body, name=name, grid=(T // tr,),
                  in_specs=[pl.BlockSpec((tr, W), lambda i: (i, 1)), _fixed((1, W)), _rows(tr, W), ANY],
                  out_specs=[pl.BlockSpec((tr, W), lambda i: (i, 1)), _fixed((1, W)), _fixed((1, W))],
                  out_shape=[jax.ShapeDtypeStruct(dpre_buf.shape, dpre_buf.dtype), jax.ShapeDtypeStruct((1, W), F32),
                             jax.ShapeDtypeStruct((1, W), F32)],
                  input_output_aliases={3: 0},
                  compiler_params=_cp(1))(pre, lw, dvn, dpre_buf)


def _tril_mask(c):
    t = lax.broadcasted_iota(jnp.int32, (c, c), 0)
    s = lax.broadcasted_iota(jnp.int32, (c, c), 1)
    return t >= s


def sg_fwd(pre, vn, ws, bs3, *, name):
    T, W = vn.shape
    G = ws.shape[0]
    gd = W // G
    C = SG_CHUNK

    def body(p_ref, v_ref, w_ref, b_ref, o_ref):
        mask = _tril_mask(C)
        for g in range(G):
            sl = slice(g * gd, (g + 1) * gd)
            wm = jnp.where(mask, w_ref[g], 0.0)
            s = _dot(wm, v_ref[:, sl]) + b_ref[g]
            o_ref[:, sl] = (_gelu(p_ref[:, sl]) * s).astype(o_ref.dtype)

    blk = pl.BlockSpec((C, W), lambda n: (n, 0))
    return _pcall(body, name=name, grid=(T // C,),
                  in_specs=[blk, blk, _fixed((G, C, C)), _fixed((G, C, 1))],
                  out_specs=blk, out_shape=jax.ShapeDtypeStruct((T, W), BF),
                  compiler_params=_cp(1))(pre, vn, ws, bs3)


def sg_bwd(pre, vn, ws, bs3, dus, *, name):
    T, W = vn.shape
    G = ws.shape[0]
    gd = W // G
    C = SG_CHUNK

    def body(p_ref, v_ref, w_ref, b_ref, d_ref, dp_ref, dv_ref, dw_ref, db_ref):
        n = pl.program_id(0)
        mask = _tril_mask(C)

        @pl.when(n == 0)
        def _():
            dw_ref[...] = jnp.zeros_like(dw_ref)
            db_ref[...] = jnp.zeros_like(db_ref)

        for g in range(G):
            sl = slice(g * gd, (g + 1) * gd)
            wm = jnp.where(mask, w_ref[g], 0.0)
            u, du = _gelu_and_grad(p_ref[:, sl])
            vv = v_ref[:, sl]
            d = d_ref[:, sl]
            s = _dot(wm, vv) + b_ref[g]
            ds = d * u
            dp_ref[:, sl] = (d * s * du).astype(dp_ref.dtype)
            dv_ref[:, sl] = _dot(wm, ds, TN)
            dw_ref[g] += jnp.where(mask, _dot(ds, vv, NT), 0.0)
            db_ref[g] += jnp.sum(ds, axis=1, keepdims=True)

    blk = pl.BlockSpec((C, W), lambda n: (n, 0))
    return _pcall(body, name=name, grid=(T // C,),
                  in_specs=[blk, blk, _fixed((G, C, C)), _fixed((G, C, 1)), blk],
                  out_specs=[blk, blk, _fixed((G, C, C)), _fixed((G, C, 1))],
                  out_shape=[jax.ShapeDtypeStruct((T, 2 * W), BF),
                             jax.ShapeDtypeStruct((T, W), F32),
                             jax.ShapeDtypeStruct((G, C, C), F32), jax.ShapeDtypeStruct((G, C, 1), F32)],
                  compiler_params=_cp(1))(pre, vn, ws, bs3, dus)


CHIP_RELATIONS = ((1, 0), (0, 1), (1, 1))


def _place():
    return lax.axis_index("x"), lax.axis_index("y"), lax.axis_index("c")


def _peer_chip(x, y, r):
    fx, fy = CHIP_RELATIONS[r]
    return (1 - x if fx else x), (1 - y if fy else y)


def gather_comm(arrs):
    n = len(arrs)
    per = 2 * len(CHIP_RELATIONS) + 1
    own = per - 1

    def ici(a, r, ins, outs, send, recv):
        x, y, c = _place()
        px, py = _peer_chip(x, y, r)
        return pltpu.make_async_remote_copy(
            src_ref=ins[a].at[c], dst_ref=outs[a].at[2 * x + y, c], send_sem=send.at[a * per + r],
            recv_sem=recv.at[a * per + r], device_id=(px, py, c), device_id_type=MESH)

    def own_block(a, ins, outs, send, recv):
        x, y, c = _place()
        return pltpu.make_async_remote_copy(
            src_ref=ins[a], dst_ref=outs[a].at[2 * x + y], send_sem=send.at[a * per + own],
            recv_sem=recv.at[a * per + own], device_id=(x, y, 1 - c), device_id_type=MESH)

    def start(ins, outs, send, recv):
        for a in range(n):
            for r in range(3):
                ici(a, r, ins, outs, send, recv).start()
            own_block(a, ins, outs, send, recv).start()

    def finish(ins, outs, send, recv):
        x, y, c = _place()
        sib = (x, y, 1 - c)
        forwards = []
        for a in range(n):
            for r in range(3):
                px, py = _peer_chip(x, y, r)
                landed = outs[a].at[2 * px + py, c]
                pltpu.make_async_remote_copy(
                    src_ref=landed, dst_ref=landed, send_sem=send.at[a * per + r],
                    recv_sem=recv.at[a * per + r], device_id=(px, py, c), device_id_type=MESH).wait_recv()
                fw = pltpu.make_async_remote_copy(
                    src_ref=landed, dst_ref=landed, send_sem=send.at[a * per + 3 + r],
                    recv_sem=recv.at[a * per + 3 + r], device_id=sib, device_id_type=MESH)
                fw.start()
                forwards.append(fw)
        for a in range(n):
            for r in range(3):
                px, py = _peer_chip(x, y, r)
                other = outs[a].at[2 * px + py, 1 - c]
                pltpu.make_async_remote_copy(
                    src_ref=other, dst_ref=other, send_sem=send.at[a * per + 3 + r],
                    recv_sem=recv.at[a * per + 3 + r], device_id=sib, device_id_type=MESH).wait_recv()
        for a in range(n):
            for r in range(3):
                ici(a, r, ins, outs, send, recv).wait_send()
            own_block(a, ins, outs, send, recv).wait()
        for fw in forwards:
            fw.wait_send()

    outs = [jax.ShapeDtypeStruct((N_CHIPS,) + a.shape, a.dtype) for a in arrs]
    return Comm(list(arrs), outs, n * per, start, finish)


def chip_exchange_comm(ps):
    n = len(ps)

    def copies(ins, outs, send, recv):
        x, y, c = _place()
        cps = []
        for a in range(n):
            for r in range(3):
                px, py = _peer_chip(x, y, r)
                cps.append(pltpu.make_async_remote_copy(
                    src_ref=ins[a].at[2 * px + py], dst_ref=outs[a].at[r], send_sem=send.at[3 * a + r],
                    recv_sem=recv.at[3 * a + r], device_id=(px, py, c), device_id_type=MESH))
        return cps

    def start(ins, outs, send, recv):
        for cp in copies(ins, outs, send, recv):
            cp.start()

    def finish(ins, outs, send, recv):
        for cp in copies(ins, outs, send, recv):
            cp.wait()

    outs = [jax.ShapeDtypeStruct((3,) + p.shape[1:], p.dtype) for p in ps]
    return Comm(list(ps), outs, 3 * n, start, finish)


def run_comm(comm, *, name):
    n_i, n_o = len(comm.ins), len(comm.outs)

    def body(*refs):
        ins, outs = refs[:n_i], refs[n_i:n_i + n_o]
        send, recv = refs[n_i + n_o:]
        comm.start(ins, outs, send, recv)
        comm.finish(ins, outs, send, recv)

    res = _pcall(body, name=name, in_specs=[ANY] * n_i, out_specs=[ANY] * n_o, out_shape=comm.outs,
                 scratch_shapes=[pltpu.SemaphoreType.DMA((comm.nsem,)), pltpu.SemaphoreType.DMA((comm.nsem,))])(*comm.ins)
    return list(res)


def pair_exchange(gs, *, name):
    n = len(gs)

    def body(*refs):
        ins, outs = refs[:n], refs[n:2 * n]
        send, recv = refs[2 * n:2 * n + 2]
        x, y, c = _place()
        cps = []
        for a in range(n):
            cp = pltpu.make_async_remote_copy(
                src_ref=ins[a].at[:, pl.ds(1 - c, 1)], dst_ref=outs[a], send_sem=send.at[a], recv_sem=recv.at[a],
                device_id=(x, y, 1 - c), device_id_type=MESH)
            cp.start()
            cps.append(cp)
        for cp in cps:
            cp.wait()

    out_shape = [jax.ShapeDtypeStruct((g.shape[0], 1) + g.shape[2:], g.dtype) for g in gs]
    res = _pcall(body, name=name, in_specs=[ANY] * n, out_specs=[ANY] * n, out_shape=out_shape,
                 scratch_shapes=[pltpu.SemaphoreType.DMA((n,)), pltpu.SemaphoreType.DMA((n,))])(*gs)
    return list(res)


def pair_share(fs, *, name):
    n = len(fs)

    def body(*refs):
        ins, outs = refs[:n], refs[n:2 * n]
        send, recv = refs[2 * n:2 * n + 2]
        x, y, c = _place()
        cps = []
        for a in range(n):
            cp = pltpu.make_async_remote_copy(
                src_ref=ins[a], dst_ref=outs[a], send_sem=send.at[a], recv_sem=recv.at[a],
                device_id=(x, y, 1 - c), device_id_type=MESH)
            cp.start()
            cps.append(cp)
        for cp in cps:
            cp.wait()

    out_shape = [jax.ShapeDtypeStruct(f.shape, f.dtype) for f in fs]
    res = _pcall(body, name=name, in_specs=[ANY] * n, out_specs=[ANY] * n, out_shape=out_shape,
                 scratch_shapes=[pltpu.SemaphoreType.DMA((n,)), pltpu.SemaphoreType.DMA((n,))])(*fs)
    return list(res)


def all_reduce_small(v, *, name):
    R = v.shape[0]

    def body(v_ref, sum_ref, gat_ref, send, recv):
        x, y, c = _place()
        me = 4 * x + 2 * y + c
        gat_ref[me] = v_ref[...]
        cps = []
        peers = []
        for r in range(1, N_DEV):
            fx, fy, fc = (r >> 2) & 1, (r >> 1) & 1, r & 1
            px, py, pc = (1 - x if fx else x), (1 - y if fy else y), (1 - c if fc else c)
            peers.append((px, py, pc))
            cp = pltpu.make_async_remote_copy(
                src_ref=v_ref, dst_ref=gat_ref.at[me], send_sem=send.at[r - 1], recv_sem=recv.at[r - 1],
                device_id=(px, py, pc), device_id_type=MESH)
            cp.start()
            cps.append(cp)
        for r in range(1, N_DEV):
            px, py, pc = peers[r - 1]
            slot = gat_ref.at[4 * px + 2 * py + pc]
            pltpu.make_async_remote_copy(
                src_ref=v_ref, dst_ref=slot, send_sem=send.at[r - 1], recv_sem=recv.at[r - 1],
                device_id=(px, py, pc), device_id_type=MESH).wait_recv()
        for cp in cps:
            cp.wait_send()
        acc = gat_ref[0]
        for s in range(1, N_DEV):
            acc = acc + gat_ref[s]
        sum_ref[...] = acc

    vm = pl.BlockSpec(memory_space=pltpu.VMEM)
    res = _pcall(body, name=name, in_specs=[vm], out_specs=[vm, vm],
                 out_shape=[jax.ShapeDtypeStruct((R, LANES), F32), jax.ShapeDtypeStruct((N_DEV, R, LANES), F32)],
                 scratch_shapes=[pltpu.SemaphoreType.DMA((N_DEV - 1,)), pltpu.SemaphoreType.DMA((N_DEV - 1,))],
                 compiler_params=pltpu.CompilerParams(vmem_limit_bytes=VMEM_LIMIT))(v)
    return res[0]


def pair_sum(g, r1, c_idx, *, name):
    nb, _, hr, C = g.shape
    tr = _tile(hr, max(BF16_ROWS, (1 << 18) // C), BF16_ROWS)

    def body(c_ref, g_ref, r_ref, o_ref, ob_ref):
        s = g_ref[...] + r_ref[...].astype(F32)
        o_ref[...] = s
        ob_ref[...] = s.astype(ob_ref.dtype)

    out = pl.BlockSpec((None, tr, C), lambda b, i, cr: (b, i, 0))
    gs = pltpu.PrefetchScalarGridSpec(
        num_scalar_prefetch=1, grid=(nb, hr // tr),
        in_specs=[pl.BlockSpec((None, None, tr, C), lambda b, i, cr: (b, cr[0], i, 0)),
                  pl.BlockSpec((None, None, tr, C), lambda b, i, cr: (b, 0, i, 0))],
        out_specs=[out, out])
    return _pcall(body, name=name, grid_spec=gs,
                  out_shape=[jax.ShapeDtypeStruct((nb, hr, C), F32), jax.ShapeDtypeStruct((nb, hr, C), BF)],
                  compiler_params=_cp(2))(c_idx, g, r1)


def chip_sum(p, r2, j_idx, *, name, layer=0, n_layers=1, into=None):
    _, hr, C = p.shape
    tr = _tile(hr, max(BF16_ROWS, (1 << 18) // C), BF16_ROWS)

    def body(j_ref, p_ref, a_ref, b_ref, c_ref, *rest):
        o_ref = rest[-1]
        o_ref[...] = ((p_ref[...] + a_ref[...].astype(F32)) + b_ref[...].astype(F32)) + c_ref[...].astype(F32)

    rel = lambda r: pl.BlockSpec((None, tr, C), lambda i, jr: (r, i, 0))
    in_specs = [pl.BlockSpec((None, tr, C), lambda i, jr: (jr[0], i, 0)), rel(0), rel(1), rel(2)]
    operands = [j_idx, p, r2, r2, r2]
    aliases = {}
    if into is not None:
        in_specs.append(ANY)
        operands.append(into)
        aliases = {len(operands) - 1: 0}
    gs = pltpu.PrefetchScalarGridSpec(
        num_scalar_prefetch=1, grid=(hr // tr,), in_specs=in_specs,
        out_specs=pl.BlockSpec((None, tr, C), lambda i, jr: (layer, i, 0)))
    return _pcall(body, name=name, grid_spec=gs, out_shape=jax.ShapeDtypeStruct((n_layers, hr, C), F32),
                  input_output_aliases=aliases, compiler_params=_cp(1))(*operands)


def adamw_halves(w, g_mine, g_other, m, v, c_idx, *, name):
    L, R, C = w.shape
    hr = R // 2
    tr = _tile(hr, max(SUBLANES, (1 << 18) // C), SUBLANES)
    nbh = hr // tr
    c1 = 1.0 - ADAM_B1 ** ADAM_STEP
    c2 = 1.0 - ADAM_B2 ** ADAM_STEP

    def body(c_ref, w_ref, gm_ref, go_ref, m_ref, v_ref, g_ref, d_ref, mo_ref, vo_ref):
        i = pl.program_id(1)
        gv = jnp.where(i // nbh == c_ref[0], gm_ref[...], go_ref[...])
        m2 = ADAM_B1 * m_ref[...] + (1.0 - ADAM_B1) * gv
        v2 = ADAM_B2 * v_ref[...] + (1.0 - ADAM_B2) * (gv * gv)
        d_ref[...] = -ADAM_LR * ((m2 / c1) / (jnp.sqrt(v2 / c2) + ADAM_EPS) + ADAM_WD * w_ref[...])
        g_ref[...] = gv
        mo_ref[...] = m2
        vo_ref[...] = v2

    full = pl.BlockSpec((None, tr, C), lambda l, i, cr: (l, i, 0))
    half = pl.BlockSpec((None, tr, C), lambda l, i, cr: (l, i % nbh, 0))
    gs = pltpu.PrefetchScalarGridSpec(
        num_scalar_prefetch=1, grid=(L, R // tr),
        in_specs=[full, half, half, full, full], out_specs=[full] * 4)
    sds = jax.ShapeDtypeStruct((L, R, C), F32)
    return _pcall(body, name=name, grid_spec=gs, out_shape=[sds] * 4,
                  compiler_params=_cp(2))(c_idx, w, g_mine, g_other, m, v)


def _pack_rows(arrs):
    parts = []
    for a in arrs:
        flat = a.reshape(-1).astype(F32)
        tile = SUBLANES * LANES
        pad = (-flat.shape[0]) % tile
        parts.append(jnp.pad(flat, (0, pad)).reshape(-1, LANES))
    return jnp.concatenate(parts, axis=0)


def _unpack_rows(buf, shapes):
    out, row = [], 0
    for shp in shapes:
        size = int(np.prod(shp))
        rows = -(-size // (SUBLANES * LANES)) * SUBLANES
        out.append(buf[row:row + rows].reshape(-1)[:size].reshape(shp))
        row += rows
    return out


def _halves(a2d):
    r, c = a2d.shape
    return a2d.reshape(2, r // 2, c)


def _rotary_tables(T):
    half = HEAD_DIM // 2
    pos = jnp.arange(T, dtype=F32)
    inv_freq = 1.0 / (ROPE_BASE ** jnp.linspace(0.0, 1.0, half, dtype=F32))
    ang = pos[:, None] * inv_freq[None, :]
    cos, sin = jnp.cos(ang), jnp.sin(ang)
    return jnp.concatenate([cos, cos], axis=1), jnp.concatenate([-sin, sin], axis=1)


def _retention_tables(H):
    C = LA_CHUNK
    lg = jnp.log1p(-jnp.power(2.0, -5.0 - jnp.arange(H, dtype=F32)))
    pos = jnp.arange(C, dtype=F32)
    causal = jnp.tril(jnp.ones((C, C), dtype=bool))
    dmat = jnp.exp(jnp.where(causal, (pos[:, None] - pos[None, :]) * lg[:, None, None], -jnp.inf))
    bc = lambda t: jnp.broadcast_to(t[..., None], t.shape + (HEAD_DIM,))
    avec = bc(jnp.exp((pos + 1.0)[None, :] * lg[:, None]))
    bvec = bc(jnp.exp((C - 1.0 - pos)[None, :] * lg[:, None]))
    gam = bc(jnp.exp(C * lg)[:, None])
    return dmat, avec, bvec, gam


def _relu2(acc):
    return acc, jnp.square(jnp.maximum(acc, 0.0))


def _drelu2(acc, up):
    return (acc * (2.0 * jnp.maximum(up, 0.0)),)


ROW_SHARDED = ("la_out", "sg_out", "ffn_down0", "ffn_down1")


class ExchangePlan:
    GATHERS = {"la_in_main": ("ffn_up0", "ffn_down0"), "ffn_up_0": ("sg_in",),
               "ffn_down_0": ("sg_out", "ffn_up1"), "sg_in": ("ffn_down1",)}
    REDUCES = {"ffn_dup_1": "ffn_down1", "ffn_dy_1": "ffn_up1", "sg_dus": "sg_out", "sg_dy": "sg_in",
               "ffn_dup_0": "ffn_down0", "ffn_dy_0": "ffn_up0", "la_docat": "la_out", "la_dy": "la_in"}

    def __init__(self, shard_halves, c_idx, j_idx):
        self.shard_halves, self.c_idx, self.j_idx = shard_halves, c_idx, j_idx
        self.partial = {}
        self.finished = {}

    def comm(self, carrier):
        if carrier in self.GATHERS:
            return gather_comm([self.shard_halves[w] for w in self.GATHERS[carrier]])
        if carrier in self.REDUCES:
            return chip_exchange_comm([self.partial[self.REDUCES[carrier]][1]])
        return None

    def done(self, carrier, outs, W):
        if carrier in self.GATHERS:
            for w, g in zip(self.GATHERS[carrier], outs):
                install_gathered(W, w, g)
        else:
            w = self.REDUCES[carrier]
            per_layer = w[:-1] in ("ffn_up", "ffn_down")
            key, layer, n_layers = (w[:-1], int(w[-1]), 2) if per_layer else (w, 0, 1)
            self.finished[key] = chip_sum(self.partial[w][0], outs[0], self.j_idx, name=f"grads_chip_sum_{w}",
                                          layer=layer, n_layers=n_layers, into=self.finished.get(key))

    def grad_ready(self, w, g, payload=None):
        halves = lambda t: t.reshape(N_CHIPS, 2, t.shape[1] // 2, t.shape[2])
        sib = pair_exchange([halves(g if payload is None else payload)], name=f"grads_pair_exchange_{w}")[0]
        self.partial[w] = pair_sum(halves(g), sib, self.c_idx, name=f"grads_pair_sum_{w}")


def install_gathered(W, w, g):
    whole = g.reshape(N_CHIPS, g.shape[1] * g.shape[2], g.shape[3])
    if w in ROW_SHARDED:
        whole = whole.reshape(-1, whole.shape[-1])
    if w[:-1] in ("ffn_up", "ffn_down"):
        W[w[:-1]][int(w[-1])] = whole
    else:
        W[w] = whole


def _by_chip(w, g):
    return g.reshape(N_CHIPS, -1, g.shape[-1]) if w in ROW_SHARDED else g


def _la_shard_rows(H):
    cs = (8 * H * HEAD_DIM + 2 * H) // N_CHIPS
    return cs, -(-cs // (2 * BF16_ROWS)) * (2 * BF16_ROWS)


def _la_pieces(H):
    HD = H * HEAD_DIM
    mix = 8 * HD + 2 * H
    cs = mix // N_CHIPS
    segments = [(0, 0, 4 * HD, 0), (1, 4 * HD, 4 * HD + 2 * H, 0), (0, 4 * HD + 2 * H, mix, 4 * HD)]
    pieces = []
    for j in range(N_CHIPS):
        mine = []
        for src, a, b, base in segments:
            lo, hi = max(cs * j, a), min(cs * (j + 1), b)
            if lo < hi:
                mine.append((src, base + lo - a, base + hi - a))
        pieces.append(mine)
    return pieces


def _la_weights_from_gathered(g, H):
    _, csp, D = g.shape
    tc = _tile(D, 2 * LANES)
    n_main = 8 * H * HEAD_DIM

    def body(g_ref, main_ref, gate_ref):
        parts = {0: [], 1: []}
        for j, mine in enumerate(_la_pieces(H)):
            row = 0
            for src, a, b in mine:
                parts[src].append(g_ref[j, row:row + b - a, :])
                row += b - a
        main_ref[...] = jnp.concatenate(parts[0], axis=0)
        gate = jnp.concatenate(parts[1], axis=0)
        gate_ref[...] = jnp.concatenate([gate, jnp.zeros((LANES - gate.shape[0], tc), gate.dtype)], axis=0)

    return _pcall(body, name="la_weights", grid=(D // tc,),
                  in_specs=[pl.BlockSpec((N_CHIPS, csp, tc), lambda i: (0, 0, i))],
                  out_specs=[pl.BlockSpec((n_main, tc), lambda i: (0, i)), pl.BlockSpec((LANES, tc), lambda i: (0, i))],
                  out_shape=[jax.ShapeDtypeStruct((n_main, D), g.dtype), jax.ShapeDtypeStruct((LANES, D), g.dtype)],
                  compiler_params=_cp(1))(g)


def _la_dproj_by_chip(main_parts, dpg, H):
    HD = H * HEAD_DIM
    cs, csp = _la_shard_rows(H)
    T = dpg.shape[0]
    tr = _tile(T, ROW_TILE, BF16_ROWS)
    n = len(main_parts)

    def body(*refs):
        parts, g_ref, o_ref = refs[:n], refs[n], refs[n + 1]
        pad = jnp.zeros((tr, csp - cs), o_ref.dtype)
        cols = []
        for mine in _la_pieces(H):
            for src, a, b in mine:
                while src == 0 and a < b:
                    i, off = divmod(a, HD)
                    end = min(b, (i + 1) * HD)
                    cols.append(parts[i][:, off:off + end - a])
                    a = end
                if src == 1:
                    cols.append(g_ref[:, a:b])
            cols.append(pad)
        o_ref[...] = jnp.concatenate(cols, axis=1)

    return _pcall(body, name="la_dproj", grid=(T // tr,),
                  in_specs=[_rows(tr, HD)] * n + [_rows(tr, LANES)], out_specs=_rows(tr, N_CHIPS * csp),
                  out_shape=jax.ShapeDtypeStruct((T, N_CHIPS * csp), BF), compiler_params=_cp(1))(*main_parts, dpg)


def _train_local(x2, tgt, W, plan=None):
    T, D = x2.shape
    H = W["a_log"].shape[0]
    nw = W["norm_w"]
    row = lambda v: v.reshape(1, -1).astype(F32)
    G = {}

    def mm(fn, *args, name, **kw):
        comm = plan.comm(name) if plan is not None else None
        if comm is None:
            return fn(*args, name=name, **kw)
        res, outs = fn(*args, name=name, comm=comm, **kw)
        plan.done(name, outs, W)
        return res

    def grad(w, g):
        payload = None
        if isinstance(g, (list, tuple)):
            g, payload = g
        G[w] = g
        if plan is not None:
            plan.grad_ready(w, _by_chip(w, g), None if payload is None else _by_chip(w, payload))

    def ffn_fwd(y, l):
        up, act = mm(mm_nn, y, W["ffn_up"][l], name=f"ffn_up_{l}", out_dtypes=(F32, BF), epilogue=_relu2)
        dn = mm(mm_nn, act, W["ffn_down"][l], name=f"ffn_down_{l}")
        return up, act, dn

    def ffn_bwd(y, up, act, ddn, l):
        grad(f"ffn_down{l}", mm_tn(act, ddn, name=f"ffn_dwdown_{l}", bf16_copy=True))
        dup = mm(mm_nt, ddn, W["ffn_down"][l], name=f"ffn_dup_{l}", out_dtypes=(BF,), epilogue=_drelu2, extras=(up,))
        grad(f"ffn_up{l}", mm_tn(y, dup, name=f"ffn_dwup_{l}", shards=N_CHIPS, bf16_copy=True))
        return mm(mm_nt, dup, W["ffn_up"][l], name=f"ffn_dy_{l}")

    y0 = rms_fwd(x2, row(nw[0, 0]), name="norm00")
    pm = mm(mm_nt, y0, W["la_in_main"], name="la_in_main")
    pg = mm_nt(y0, W["la_in_gate"], name="la_in_gate")
    wc8 = jnp.pad(jnp.transpose(W["conv_w"]), ((0, SUBLANES - CONV_WIDTH), (0, 0)))
    lanes_pad = (H, LANES - 2 * H)
    arow = jnp.pad(W["a_log"], lanes_pad).reshape(1, LANES)
    dtrow = jnp.pad(W["dt_bias"], lanes_pad).reshape(1, LANES)
    bg, gcrow = gates_fwd(pg, arow, dtrow, H, name="gates_fwd")
    q = prep_fwd(pm, 0, wc8, 0, H, True, name="prep_q")
    k = prep_fwd(pm, H, wc8, H, H, True, name="prep_k")
    v = prep_fwd(pm, 2 * H, wc8, 2 * H, H, False, name="prep_v")
    wn = row(W["out_norm_w"])
    og_a, or_a, sall_a, tall = gdn_fwd(q, k, v, pm, 3 * H, bg, gcrow, wn, H, name="gdn_fwd")
    cs, sn = _rotary_tables(T)
    dmat, avec, bvec, gam = _retention_tables(H)
    ocat, or_b, sall_b = ret_fwd(pm, 4 * H, 5 * H, 6 * H, 7 * H, cs, sn, dmat, avec, bvec, gam, og_a, H,
                                 name="ret_fwd")
    mix = mm_nn(ocat, W["la_out"], name="la_out")
    h1, y2 = res_norm(x2, mix, row(nw[0, 1]), row(nw[0, 2]), name="resnorm_0a")
    up, act, dn = ffn_fwd(y2, 0)
    h2, y0b = res_norm(h1, dn, row(nw[0, 3]), row(nw[1, 0]), name="resnorm_0b")

    pre = mm(mm_nn, y0b, W["sg_in"], name="sg_in")
    lw, lb = row(W["ln_w"]), row(W["ln_b"])
    vn = ln_fwd(pre, lw, lb, name="sg_ln")
    ws = W["w_s"]
    bs3 = W["b_s"][:, :, None]
    us = sg_fwd(pre, vn, ws, bs3, name="sg_gate")
    mix1 = mm_nn(us, W["sg_out"], name="sg_out")
    h3, y2b = res_norm(h2, mix1, row(nw[1, 1]), row(nw[1, 2]), name="resnorm_1a")
    up1, act1, dn1 = ffn_fwd(y2b, 1)
    dnw = [[None] * 4 for _ in range(2)]
    dh4, ddn1, dnw[1][3], lrow = last_norm_and_loss(h3, dn1, row(nw[1, 3]), tgt, name="last_norm_and_loss")
    loss = lrow[0, 0]

    dy2b = ffn_bwd(y2b, up1, act1, ddn1, 1)
    dh3, dnw[1][2], dmix1, dnw[1][1] = rms_bwd(h3, row(nw[1, 2]), dy2b, dh4, name="dnorm_1a",
                                               inner=(mix1, row(nw[1, 1])))
    grad("sg_out", mm_tn(us, dmix1, name="sg_dwout", bf16_copy=True))
    dus = mm(mm_nt, dmix1, W["sg_out"], name="sg_dus")
    dpre_u, dvn, G["w_s"], dbs3 = sg_bwd(pre, vn, ws, bs3, dus, name="sg_gate_bwd")
    G["b_s"] = dbs3[:, :, 0]
    dpre, dlw, dlb = ln_bwd(pre, lw, dvn, dpre_u, name="sg_ln_bwd")
    G["ln_w"], G["ln_b"] = dlw[0], dlb[0]
    grad("sg_in", mm_tn(y0b, dpre, name="sg_dwin", shards=N_CHIPS, bf16_copy=True))
    dy0b = mm(mm_nt, dpre, W["sg_in"], name="sg_dy")

    dh2, dnw[1][0], ddn, dnw[0][3] = rms_bwd(h2, row(nw[1, 0]), dy0b, dh3, name="dnorm_0b",
                                             inner=(dn, row(nw[0, 3])))
    dy2 = ffn_bwd(y2, up, act, ddn, 0)
    dh1, dnw[0][2], dmix, dnw[0][1] = rms_bwd(h1, row(nw[0, 2]), dy2, dh2, name="dnorm_0a",
                                              inner=(mix, row(nw[0, 1])))
    grad("la_out", mm_tn(ocat, dmix, name="la_dwout", bf16_copy=True))
    docat = mm(mm_nt, dmix, W["la_out"], name="la_docat")
    dq, dk, dv, dz, dbg, dwn = gdn_bwd(q, k, v, pm, 3 * H, bg, gcrow, wn, or_a, sall_a, tall, docat, H,
                                       name="gdn_bwd")
    drq, drk, drv, drg = ret_bwd(pm, 4 * H, 5 * H, 6 * H, 7 * H, cs, sn, dmat, avec, bvec, gam, or_b, sall_b,
                                 docat, H, H, name="ret_bwd")
    dpg, da, ddt = gates_bwd(pg, arow, dtrow, dbg, H, name="gates_bwd")
    dcq, dwq = prep_bwd_act(pm, 0, wc8, 0, H, True, dq, name="prep_dq")
    dck, dwk = prep_bwd_act(pm, H, wc8, H, H, True, dk, name="prep_dk")
    dcv, dwv = prep_bwd_act(pm, 2 * H, wc8, 2 * H, H, False, dv, name="prep_dv")
    dxq = prep_bwd_conv(dcq, wc8, 0, H, name="conv_dq")
    dxk = prep_bwd_conv(dck, wc8, H, H, name="conv_dk")
    dxv = prep_bwd_conv(dcv, wc8, 2 * H, H, name="conv_dv")
    dproj = _la_dproj_by_chip([dxq, dxk, dxv, dz, drq, drk, drv, drg], dpg, H)
    grad("la_in", mm_tn(dproj, y0, name="la_dwin").reshape(N_CHIPS, -1, D))
    dy0 = mm(mm_nn, dproj, W["la_in_rows"], name="la_dy")
    dx, dnw[0][0] = rms_bwd(x2, row(nw[0, 0]), dy0, dh1, name="dnorm00")

    G["norm_w"] = jnp.stack([jnp.concatenate(r, axis=0) for r in dnw], axis=0)
    G["conv_w"] = jnp.transpose(jnp.concatenate([dwq, dwk, dwv], axis=1)[:CONV_WIDTH])
    G["a_log"] = da[0, H:2 * H]
    G["dt_bias"] = ddt[0, H:2 * H]
    G["out_norm_w"] = dwn[0]
    return loss, dx, G


def _as2d(a):
    n = int(np.prod(a.shape))
    if a.shape[-1] < LANES and n % LANES == 0:
        return a.reshape(-1, LANES)
    return a.reshape(-1, a.shape[-1])


def _adamw_any(w, g, m, v, name):
    shp = w.shape
    d, m2, v2 = adamw(_as2d(w), _as2d(g.reshape(shp)), _as2d(m), _as2d(v), name=name)
    return g.reshape(shp), d.reshape(shp), m2.reshape(shp), v2.reshape(shp)


def kernel(x, norm_w, la_w_in, la_conv_w, la_a_log, la_dt_bias, la_out_norm_w, la_w_out, sg_w_in, sg_ln_w, sg_ln_b, sg_w_s, sg_b_s, sg_w_out, ffn_w_up, ffn_w_down, loss_target, m_norm_w, m_la_w_in, m_la_conv_w, m_la_a_log, m_la_dt_bias, m_la_out_norm_w, m_la_w_out, m_sg_w_in, m_sg_ln_w, m_sg_ln_b, m_sg_w_s, m_sg_b_s, m_sg_w_out, m_ffn_w_up, m_ffn_w_down, v_norm_w, v_la_w_in, v_la_conv_w, v_la_a_log, v_la_dt_bias, v_la_out_norm_w, v_la_w_out, v_sg_w_in, v_sg_ln_w, v_sg_ln_b, v_sg_w_s, v_sg_b_s, v_sg_w_out, v_ffn_w_up, v_ffn_w_down):
    weights = dict(norm_w=norm_w, la_w_in=la_w_in, la_conv_w=la_conv_w, la_a_log=la_a_log, la_dt_bias=la_dt_bias,
                   la_out_norm_w=la_out_norm_w, la_w_out=la_w_out, sg_w_in=sg_w_in, sg_ln_w=sg_ln_w,
                   sg_ln_b=sg_ln_b, sg_w_s=sg_w_s, sg_b_s=sg_b_s, sg_w_out=sg_w_out, ffn_w_up=ffn_w_up,
                   ffn_w_down=ffn_w_down)
    mom_m = dict(norm_w=m_norm_w, la_w_in=m_la_w_in, la_conv_w=m_la_conv_w, la_a_log=m_la_a_log,
                 la_dt_bias=m_la_dt_bias, la_out_norm_w=m_la_out_norm_w, la_w_out=m_la_w_out, sg_w_in=m_sg_w_in,
                 sg_ln_w=m_sg_ln_w, sg_ln_b=m_sg_ln_b, sg_w_s=m_sg_w_s, sg_b_s=m_sg_b_s, sg_w_out=m_sg_w_out,
                 ffn_w_up=m_ffn_w_up, ffn_w_down=m_ffn_w_down)
    mom_v = dict(norm_w=v_norm_w, la_w_in=v_la_w_in, la_conv_w=v_la_conv_w, la_a_log=v_la_a_log,
                 la_dt_bias=v_la_dt_bias, la_out_norm_w=v_la_out_norm_w, la_w_out=v_la_w_out, sg_w_in=v_sg_w_in,
                 sg_ln_w=v_sg_ln_w, sg_ln_b=v_sg_ln_b, sg_w_s=v_sg_w_s, sg_b_s=v_sg_b_s, sg_w_out=v_sg_w_out,
                 ffn_w_up=v_ffn_w_up, ffn_w_down=v_ffn_w_down)
    order = list(weights)

    T, D = x.shape[1], x.shape[2]
    H = la_a_log.shape[1]
    HD = H * HEAD_DIM
    xi, yi, ci = _place()
    chip = 2 * xi + yi
    c_idx = jnp.reshape(ci, (1,)).astype(jnp.int32)
    j_idx = jnp.reshape(chip, (1,)).astype(jnp.int32)

    cs, csp = _la_shard_rows(H)
    la_rows = lambda a: jnp.pad(jnp.swapaxes(a, 1, 2), ((0, 0), (0, csp - cs), (0, 0)))
    shards = dict(la_in=la_rows(la_w_in)[0], la_out=la_w_out[0], sg_in=sg_w_in[0], sg_out=sg_w_out[0],
                  ffn_up0=ffn_w_up[0], ffn_up1=ffn_w_up[1], ffn_down0=ffn_w_down[0], ffn_down1=ffn_w_down[1])
    shard_halves = {w: _halves(a.astype(BF)) for w, a in shards.items()}
    small_shapes = [norm_w.shape, la_conv_w[0].shape, sg_ln_w[0].shape, sg_ln_b[0].shape]
    small = _pack_rows([norm_w, la_conv_w[0], sg_ln_w[0], sg_ln_b[0]])
    small = _halves(jnp.pad(small, ((0, (-small.shape[0]) % (2 * SUBLANES)), (0, 0))))
    first = [shard_halves["la_in"], shard_halves["la_out"], small]
    la_in_g, la_out_g, small_g = [g.reshape(N_CHIPS, -1, g.shape[-1])
                                  for g in run_comm(gather_comm(first), name="gather_first")]
    pieces = [_unpack_rows(small_g[kk], small_shapes) for kk in range(N_CHIPS)]
    la_main, la_gate = _la_weights_from_gathered(la_in_g, H)
    W = dict(
        norm_w=jnp.concatenate([p[0] for p in pieces], axis=-1),
        conv_w=jnp.concatenate([p[1] for p in pieces], axis=0),
        ln_w=jnp.concatenate([p[2] for p in pieces], axis=0),
        ln_b=jnp.concatenate([p[3] for p in pieces], axis=0),
        a_log=la_a_log[0], dt_bias=la_dt_bias[0], out_norm_w=la_out_norm_w[0], w_s=sg_w_s[0], b_s=sg_b_s[0],
        la_in_main=la_main, la_in_gate=la_gate, la_in_rows=la_in_g.reshape(-1, D),
        la_out=la_out_g.reshape(-1, la_out_g.shape[-1]), ffn_up=[None, None], ffn_down=[None, None],
    )
    plan = ExchangePlan(shard_halves, c_idx, j_idx)

    loss_local, dx, G = _train_local(x[0], loss_target[0], W, plan)
    loss = lax.psum(loss_local, ("x", "y", "c"))

    big_params = dict(la_w_in="la_in", la_w_out="la_out", sg_w_in="sg_in", sg_w_out="sg_out",
                      ffn_w_up="ffn_up", ffn_w_down="ffn_down")
    from_sib = dict(zip(big_params, pair_share([plan.finished[k] for k in big_params.values()],
                                               name="grads_pair_share")))

    def big_update(nm, key):
        rows = la_rows if nm == "la_w_in" else (lambda a: a)
        r4 = adamw_halves(rows(weights[nm]), plan.finished[key], from_sib[nm], rows(mom_m[nm]), rows(mom_v[nm]),
                          c_idx, name=f"adamw_{nm}")
        return [jnp.swapaxes(t[:, :cs], 1, 2) for t in r4] if nm == "la_w_in" else r4

    big_res = {nm: big_update(nm, key) for nm, key in big_params.items()}

    small_names = ["norm_w", "conv_w", "ln_w", "ln_b", "a_log", "dt_bias", "out_norm_w", "w_s", "b_s"]
    small_full = [G[nm] for nm in small_names]
    summed = _unpack_rows(all_reduce_small(_pack_rows(small_full), name="grads_all_reduce_small"),
                          [g.shape for g in small_full])
    sm = dict(zip(small_names, summed))
    own = lambda full, axis: lax.dynamic_slice_in_dim(full, chip * (full.shape[axis] // N_CHIPS),
                                                      full.shape[axis] // N_CHIPS, axis)
    grads = dict(
        norm_w=own(sm["norm_w"], 2), la_conv_w=own(sm["conv_w"], 0), la_a_log=sm["a_log"],
        la_dt_bias=sm["dt_bias"], la_out_norm_w=sm["out_norm_w"],
        sg_ln_w=own(sm["ln_w"], 0), sg_ln_b=own(sm["ln_b"], 0), sg_w_s=sm["w_s"], sg_b_s=sm["b_s"],
    )

    res = {nm: big_res[nm] if nm in big_res else
           _adamw_any(weights[nm], grads[nm], mom_m[nm], mom_v[nm], f"adamw_{nm}") for nm in order}
    return (loss, dx.reshape(x.shape), *[res[nm][0] for nm in order], *[res[nm][1] for nm in order],
            *[res[nm][2] for nm in order], *[res[nm][3] for nm in order])
```

```python
import math
from typing import Callable, NamedTuple

import numpy as np
import jax
import jax.numpy as jnp
from jax import lax
from jax.experimental import pallas as pl
from jax.experimental.pallas import tpu as pltpu

F32 = jnp.float32
BF = jnp.bfloat16

V7X_VMEM_BYTES = 64 * 1024 * 1024
VMEM_LIMIT = (V7X_VMEM_BYTES * 3) // 4
LANES = 128
SUBLANES = 8
BF16_ROWS = 16
HEAD_DIM = 128
LA_CHUNK = 64
SG_CHUNK = 128
CONV_WIDTH = 4
ROPE_BASE = 10000.0
EPS = 1e-6
L2_EPS = 1e-6
N_CHIPS = 4
N_DEV = 8

ADAM_LR = 0.001
ADAM_B1 = 0.9
ADAM_B2 = 0.999
ADAM_EPS = 1e-08
ADAM_WD = 0.01
ADAM_STEP = 10

MESH = pl.DeviceIdType.MESH
ANY = pl.BlockSpec(memory_space=pl.ANY)

NN = (((1,), (0,)), ((), ()))
NT = (((1,), (1,)), ((), ()))
TN = (((0,), (0,)), ((), ()))


def _pcall(body, **kw):
    return pl.pallas_call(body, **kw)


def _cp(n_axes):
    return pltpu.CompilerParams(dimension_semantics=("arbitrary",) * n_axes, vmem_limit_bytes=VMEM_LIMIT)


def _tile(n, pref, unit=LANES):
    if n <= pref:
        return n
    t = (pref // unit) * unit
    while t >= unit:
        if n % t == 0:
            return t
        t -= unit
    return n


def _dot(a, b, dims=NN):
    return lax.dot_general(a.astype(BF), b.astype(BF), dims, preferred_element_type=F32)


def _split_bf16(x):
    hi = x.astype(BF)
    return hi, (x - hi.astype(F32)).astype(BF)


def _dot_hi(a, b, dims=NN):
    ah, al = _split_bf16(a)
    bh, bl = _split_bf16(b)
    dot = lambda u, v: lax.dot_general(u, v, dims, preferred_element_type=F32)
    return dot(ah, bh) + (dot(ah, bl) + dot(al, bh))


def _sigmoid(x):
    return 1.0 / (1.0 + jnp.exp(-x))


def _silu(x):
    return x * _sigmoid(x)


def _dsilu(x):
    s = _sigmoid(x)
    return s * (1.0 + x * (1.0 - s))


GELU_C = math.sqrt(2.0 / math.pi)
GELU_A = 0.044715


def _gelu(x):
    return 0.5 * x * (1.0 + jnp.tanh(GELU_C * (x + GELU_A * x * x * x)))


def _gelu_and_grad(x):
    t = jnp.tanh(GELU_C * (x + GELU_A * x * x * x))
    return 0.5 * x * (1.0 + t), 0.5 * (1.0 + t) + 0.5 * x * (1.0 - t * t) * GELU_C * (1.0 + 3.0 * GELU_A * x * x)


class Comm(NamedTuple):
    ins: list
    outs: list
    nsem: int
    start: Callable
    finish: Callable


def _matmul(a, b, *, dims, grid, a_spec, b_spec, out_shape, out_spec, acc_shape, name,
            epilogue=None, extras=(), extra_specs=(), comm=None):
    nk = grid[2]
    outs = tuple(out_shape) if isinstance(out_shape, (tuple, list)) else (out_shape,)
    out_specs = tuple(out_spec) if isinstance(out_spec, (tuple, list)) else (out_spec,)
    n_ex, n_out = len(extras), len(outs)
    n_ci = len(comm.ins) if comm else 0
    n_co = len(comm.outs) if comm else 0

    def body(*refs):
        a_ref, b_ref = refs[0], refs[1]
        ex = refs[2:2 + n_ex]
        ci = refs[2 + n_ex:2 + n_ex + n_ci]
        o = refs[2 + n_ex + n_ci:2 + n_ex + n_ci + n_out]
        co = refs[2 + n_ex + n_ci + n_out:2 + n_ex + n_ci + n_out + n_co]
        scratch = refs[2 + n_ex + n_ci + n_out + n_co:]
        i, j, k = pl.program_id(0), pl.program_id(1), pl.program_id(2)

        if comm:
            send, recv = scratch[-2], scratch[-1]

            @pl.when((i == 0) & (j == 0) & (k == 0))
            def _():
                comm.start(ci, co, send, recv)

        part = lax.dot_general(a_ref[...].astype(BF), b_ref[...].astype(BF), dims, preferred_element_type=F32)

        def finish(val):
            res = epilogue(val, *[e[...] for e in ex]) if epilogue is not None else (val,)
            for r, oref in zip(res, o):
                oref[...] = r.astype(oref.dtype)

        if nk == 1:
            finish(part)
        else:
            acc = scratch[0]

            @pl.when(k == 0)
            def _():
                acc[...] = part

            @pl.when(k > 0)
            def _():
                acc[...] += part

            @pl.when(k == nk - 1)
            def _():
                finish(acc[...])

        if comm:
            @pl.when((i == grid[0] - 1) & (j == grid[1] - 1) & (k == nk - 1))
            def _():
                comm.finish(ci, co, send, recv)

    scratch_shapes = [pltpu.VMEM(acc_shape, F32)] if nk > 1 else []
    if comm:
        scratch_shapes += [pltpu.SemaphoreType.DMA((comm.nsem,)), pltpu.SemaphoreType.DMA((comm.nsem,))]
    res = _pcall(
        body, name=name, grid=grid,
        in_specs=[a_spec, b_spec, *extra_specs, *[ANY] * n_ci],
        out_specs=[*out_specs, *[ANY] * n_co],
        out_shape=[*outs, *(comm.outs if comm else [])],
        scratch_shapes=scratch_shapes,
        compiler_params=_cp(3),
    )(a, b, *extras, *(comm.ins if comm else []))
    main = res[0] if n_out == 1 else list(res[:n_out])
    return (main, list(res[n_out:])) if comm else main


def mm_nn(a, w, *, name, out_dtypes=(F32,), epilogue=None, extras=(), comm=None, tm=1024, tn=1024, tk=2048):
    M, K = a.shape
    if w.ndim == 3:
        S, _, Ns = w.shape
        N = S * Ns
    else:
        S, Ns = 1, w.shape[1]
        N = Ns
    tm, tn, tk = _tile(M, tm), _tile(Ns, tn), _tile(K, tk)
    npb = Ns // tn
    grid = (M // tm, N // tn, K // tk)
    a_spec = pl.BlockSpec((tm, tk), lambda i, j, k: (i, k))
    if w.ndim == 3:
        b_spec = pl.BlockSpec((None, tk, tn), lambda i, j, k: (j // npb, k, j % npb))
    else:
        b_spec = pl.BlockSpec((tk, tn), lambda i, j, k: (k, j))
    o_spec = pl.BlockSpec((tm, tn), lambda i, j, k: (i, j))
    outs = tuple(jax.ShapeDtypeStruct((M, N), d) for d in out_dtypes)
    res = _matmul(a, w, dims=NN, grid=grid, a_spec=a_spec, b_spec=b_spec,
                  out_shape=outs, out_spec=(o_spec,) * len(outs), acc_shape=(tm, tn), name=name,
                  epilogue=epilogue, extras=extras, extra_specs=(o_spec,) * len(extras), comm=comm)
    return res


def mm_nt(a, w, *, name, out_dtypes=(F32,), epilogue=None, extras=(), comm=None, tm=1024, tn=1024, tk=2048):
    M, Kc = a.shape
    if w.ndim == 3:
        S, Nout, Ks = w.shape
    else:
        S, (Nout, Ks) = 1, w.shape
    assert S * Ks == Kc
    tm, tn, tk = _tile(M, tm), _tile(Nout, tn), _tile(Ks, tk)
    kpb = Ks // tk
    grid = (M // tm, Nout // tn, Kc // tk)
    a_spec = pl.BlockSpec((tm, tk), lambda i, j, k: (i, k))
    if w.ndim == 3:
        b_spec = pl.BlockSpec((None, tn, tk), lambda i, j, k: (k // kpb, j, k % kpb))
    else:
        b_spec = pl.BlockSpec((tn, tk), lambda i, j, k: (j, k))
    o_spec = pl.BlockSpec((tm, tn), lambda i, j, k: (i, j))
    outs = tuple(jax.ShapeDtypeStruct((M, Nout), d) for d in out_dtypes)
    return _matmul(a, w, dims=NT, grid=grid, a_spec=a_spec, b_spec=b_spec,
                   out_shape=outs, out_spec=(o_spec,) * len(outs), acc_shape=(tm, tn), name=name,
                   epilogue=epilogue, extras=extras, extra_specs=(o_spec,) * len(extras), comm=comm)


def mm_tn(x, dy, *, name, shards=1, bf16_copy=False, tm=1024, tn=1024, tk=2048):
    T, Kin = x.shape
    N = dy.shape[1]
    Ns = N // shards
    tm, tn, tk = _tile(Kin, tm), _tile(Ns, tn), _tile(T, tk)
    npb = Ns // tn
    grid = (Kin // tm, N // tn, T // tk)
    a_spec = pl.BlockSpec((tk, tm), lambda i, j, k: (k, i))
    b_spec = pl.BlockSpec((tk, tn), lambda i, j, k: (k, j))
    if shards > 1:
        o_spec = pl.BlockSpec((None, tm, tn), lambda i, j, k: (j // npb, i, j % npb))
        out = jax.ShapeDtypeStruct((shards, Kin, Ns), F32)
    else:
        o_spec = pl.BlockSpec((tm, tn), lambda i, j, k: (i, j))
        out = jax.ShapeDtypeStruct((Kin, N), F32)
    if bf16_copy:
        return _matmul(x, dy, dims=TN, grid=grid, a_spec=a_spec, b_spec=b_spec,
                       out_shape=(out, jax.ShapeDtypeStruct(out.shape, BF)), out_spec=(o_spec, o_spec),
                       acc_shape=(tm, tn), name=name, epilogue=lambda acc: (acc, acc))
    return _matmul(x, dy, dims=TN, grid=grid, a_spec=a_spec, b_spec=b_spec,
                   out_shape=out, out_spec=o_spec, acc_shape=(tm, tn), name=name)


ROW_TILE = 256


def _rows(tr, d):
    return pl.BlockSpec((tr, d), lambda i: (i, 0))


def _fixed(shape):
    nd = len(shape)
    return pl.BlockSpec(shape, lambda *_: (0,) * nd)


def _rms(xv, w):
    r = lax.rsqrt(jnp.mean(xv * xv, axis=-1, keepdims=True) + EPS)
    return xv * r * w


def rms_fwd(x, w, *, name):
    T, D = x.shape
    tr = _tile(T, ROW_TILE, SUBLANES)

    def body(x_ref, w_ref, y_ref):
        y_ref[...] = _rms(x_ref[...], w_ref[...]).astype(y_ref.dtype)

    return _pcall(body, name=name, grid=(T // tr,), in_specs=[_rows(tr, D), _fixed((1, D))],
                  out_specs=_rows(tr, D), out_shape=jax.ShapeDtypeStruct((T, D), BF), compiler_params=_cp(1))(x, w)


def res_norm(h, m, wa, wb, *, name):
    T, D = h.shape
    tr = _tile(T, ROW_TILE, SUBLANES)

    def body(h_ref, m_ref, wa_ref, wb_ref, ho_ref, y_ref):
        ho = h_ref[...] + _rms(m_ref[...], wa_ref[...])
        ho_ref[...] = ho
        y_ref[...] = _rms(ho, wb_ref[...]).astype(y_ref.dtype)

    return _pcall(body, name=name, grid=(T // tr,),
                  in_specs=[_rows(tr, D), _rows(tr, D), _fixed((1, D)), _fixed((1, D))],
                  out_specs=[_rows(tr, D), _rows(tr, D)],
                  out_shape=[jax.ShapeDtypeStruct((T, D), F32), jax.ShapeDtypeStruct((T, D), BF)],
                  compiler_params=_cp(1))(h, m, wa, wb)


def _rms_bwd_rows(xv, w, dyv):
    r = lax.rsqrt(jnp.mean(xv * xv, axis=-1, keepdims=True) + EPS)
    xh = xv * r
    dyw = dyv * w
    return r * (dyw - xh * jnp.mean(dyw * xh, axis=-1, keepdims=True)), jnp.sum(dyv * xh, axis=0, keepdims=True)


def rms_bwd(x, w, dy, dres, *, name, inner=None):
    T, D = x.shape
    tr = _tile(T, ROW_TILE, SUBLANES)
    chained = inner is not None

    def body(*refs):
        if chained:
            x_ref, w_ref, dy_ref, dr_ref, m_ref, wa_ref, dx_ref, dw_ref, dm_ref, dwa_ref = refs
        else:
            x_ref, w_ref, dy_ref, dr_ref, dx_ref, dw_ref = refs
        i = pl.program_id(0)
        dx, part = _rms_bwd_rows(x_ref[...], w_ref[...], dy_ref[...].astype(F32))
        dx = dx + dr_ref[...]
        dx_ref[...] = dx
        if chained:
            dm, part_a = _rms_bwd_rows(m_ref[...], wa_ref[...], dx)
            dm_ref[...] = dm.astype(dm_ref.dtype)

        @pl.when(i == 0)
        def _():
            dw_ref[...] = part
            if chained:
                dwa_ref[...] = part_a

        @pl.when(i > 0)
        def _():
            dw_ref[...] += part
            if chained:
                dwa_ref[...] += part_a

    ins = [x, w, dy, dres] + (list(inner) if chained else [])
    in_specs = [_rows(tr, D), _fixed((1, D)), _rows(tr, D), _rows(tr, D)]
    out_specs = [_rows(tr, D), _fixed((1, D))]
    out_shape = [jax.ShapeDtypeStruct((T, D), F32), jax.ShapeDtypeStruct((1, D), F32)]
    if chained:
        in_specs += [_rows(tr, D), _fixed((1, D))]
        out_specs += [_rows(tr, D), _fixed((1, D))]
        out_shape += [jax.ShapeDtypeStruct((T, D), BF), jax.ShapeDtypeStruct((1, D), F32)]
    return _pcall(body, name=name, grid=(T // tr,), in_specs=in_specs, out_specs=out_specs, out_shape=out_shape,
                  compiler_params=_cp(1))(*ins)


def last_norm_and_loss(h, m, w, tgt, *, name):
    T, D = h.shape
    tr = _tile(T, ROW_TILE, SUBLANES)

    def body(h_ref, m_ref, w_ref, t_ref, dy_ref, dm_ref, dw_ref, l_ref):
        i = pl.program_id(0)
        mv = m_ref[...]
        r = lax.rsqrt(jnp.mean(mv * mv, axis=-1, keepdims=True) + EPS)
        xh = mv * r
        e = h_ref[...] + xh * w_ref[...] - t_ref[...]
        dy = e * (1.0 / D)
        dy_ref[...] = dy
        dyw = dy * w_ref[...]
        dm_ref[...] = (r * (dyw - xh * jnp.mean(dyw * xh, axis=-1, keepdims=True))).astype(dm_ref.dtype)
        pw = jnp.sum(dy * xh, axis=0, keepdims=True)
        pl_ = 0.5 * jnp.sum(jnp.mean(e * e, axis=-1, keepdims=True), axis=0, keepdims=True)
        pl_ = jnp.broadcast_to(pl_, (1, LANES))

        @pl.when(i == 0)
        def _():
            dw_ref[...] = pw
            l_ref[...] = pl_

        @pl.when(i > 0)
        def _():
            dw_ref[...] += pw
            l_ref[...] += pl_

    return _pcall(body, name=name, grid=(T // tr,),
                  in_specs=[_rows(tr, D), _rows(tr, D), _fixed((1, D)), _rows(tr, D)],
                  out_specs=[_rows(tr, D), _rows(tr, D), _fixed((1, D)), _fixed((1, LANES))],
                  out_shape=[jax.ShapeDtypeStruct((T, D), F32), jax.ShapeDtypeStruct((T, D), BF),
                             jax.ShapeDtypeStruct((1, D), F32), jax.ShapeDtypeStruct((1, LANES), F32)],
                  compiler_params=_cp(1))(h, m, w, tgt)


def adamw(w, g, m, v, *, name):
    R, C = w.shape
    tr = _tile(R, max(SUBLANES, (1 << 18) // C), SUBLANES)
    c1 = 1.0 - ADAM_B1 ** ADAM_STEP
    c2 = 1.0 - ADAM_B2 ** ADAM_STEP

    def body(w_ref, g_ref, m_ref, v_ref, d_ref, mo_ref, vo_ref):
        gv = g_ref[...]
        m2 = ADAM_B1 * m_ref[...] + (1.0 - ADAM_B1) * gv
        v2 = ADAM_B2 * v_ref[...] + (1.0 - ADAM_B2) * (gv * gv)
        d_ref[...] = -ADAM_LR * ((m2 / c1) / (jnp.sqrt(v2 / c2) + ADAM_EPS) + ADAM_WD * w_ref[...])
        mo_ref[...] = m2
        vo_ref[...] = v2

    spec = _rows(tr, C)
    sds = jax.ShapeDtypeStruct((R, C), F32)
    return _pcall(body, name=name, grid=(R // tr,), in_specs=[spec] * 4, out_specs=[spec] * 3,
                  out_shape=[sds] * 3, compiler_params=_cp(1))(w, g, m, v)


HALO = SUBLANES


def _conv_down(xx, w_ref):
    acc = xx * w_ref[pl.ds(CONV_WIDTH - 1, 1), :]
    for d in range(1, CONV_WIDTH):
        acc = acc + pltpu.roll(xx, d, 0) * w_ref[pl.ds(CONV_WIDTH - 1 - d, 1), :]
    return acc


def _conv_tile(x_ref, halo_ref, w_ref, first):
    xs = x_ref[...]
    hal = jnp.where(first, 0.0, halo_ref[...])
    cat = jnp.concatenate([hal, xs[0:HALO]], axis=0)
    return jnp.concatenate([_conv_down(cat, w_ref)[HALO:2 * HALO], _conv_down(xs, w_ref)[HALO:]], axis=0)


def _shift_down_tile(x_ref, halo_ref, first, d):
    xs = x_ref[...]
    if d == 0:
        return xs
    hal = jnp.where(first, 0.0, halo_ref[...])
    cat = jnp.concatenate([hal, xs[0:HALO]], axis=0)
    return jnp.concatenate([pltpu.roll(cat, d, 0)[HALO:2 * HALO], pltpu.roll(xs, d, 0)[HALO:]], axis=0)


def _l2n(s):
    return s * lax.rsqrt(jnp.sum(s * s, axis=-1, keepdims=True) + L2_EPS)


PREP_ROWS = 512


def _l2n_groups(s, nb):
    return jnp.concatenate([_l2n(s[:, g * LANES:(g + 1) * LANES]) for g in range(nb)], axis=1)


def prep_fwd(pm, off, wc8, woff, nblk, l2, *, name):
    T = pm.shape[0]
    tr = _tile(T, PREP_ROWS, SUBLANES)
    hb = tr // HALO
    wb = _heads_per_step(nblk)
    wl = wb * LANES

    def body(x_ref, halo_ref, w_ref, o_ref):
        i = pl.program_id(0)
        s = _silu(_conv_tile(x_ref, halo_ref, w_ref, i == 0))
        o_ref[...] = _l2n_groups(s, wb) if l2 else s

    return _pcall(
        body, name=name, grid=(T // tr, nblk // wb),
        in_specs=[pl.BlockSpec((tr, wl), lambda i, c: (i, off // wb + c)),
                  pl.BlockSpec((HALO, wl), lambda i, c: (jnp.maximum(i * hb - 1, 0), off // wb + c)),
                  pl.BlockSpec((SUBLANES, wl), lambda i, c: (0, woff // wb + c))],
        out_specs=pl.BlockSpec((tr, wl), lambda i, c: (i, c)),
        out_shape=jax.ShapeDtypeStruct((T, nblk * LANES), F32), compiler_params=_cp(2))(pm, pm, wc8)


def prep_bwd_act(pm, off, wc8, woff, nblk, l2, dout, *, name):
    T = pm.shape[0]
    tr = _tile(T, PREP_ROWS, SUBLANES)
    hb = tr // HALO
    wb = _heads_per_step(nblk)
    wl = wb * LANES

    def l2_bwd(s, do):
        r = lax.rsqrt(jnp.sum(s * s, axis=-1, keepdims=True) + L2_EPS)
        nrm = s * r
        return r * (do - nrm * jnp.sum(do * nrm, axis=-1, keepdims=True))

    def body(x_ref, halo_ref, w_ref, do_ref, dc_ref, dw_ref):
        i = pl.program_id(1)
        first = i == 0
        y = _conv_tile(x_ref, halo_ref, w_ref, first)
        s = _silu(y)
        do = do_ref[...]
        if l2:
            ds = jnp.concatenate([l2_bwd(s[:, g * LANES:(g + 1) * LANES], do[:, g * LANES:(g + 1) * LANES])
                                  for g in range(wb)], axis=1)
        else:
            ds = do
        dc = ds * _dsilu(y)
        dc_ref[...] = dc

        @pl.when(first)
        def _():
            dw_ref[...] = jnp.zeros_like(dw_ref)

        for j in range(CONV_WIDTH):
            xsh = _shift_down_tile(x_ref, halo_ref, first, CONV_WIDTH - 1 - j)
            dw_ref[pl.ds(j, 1), :] += jnp.sum(dc * xsh, axis=0, keepdims=True)

    return _pcall(
        body, name=name, grid=(nblk // wb, T // tr),
        in_specs=[pl.BlockSpec((tr, wl), lambda c, i: (i, off // wb + c)),
                  pl.BlockSpec((HALO, wl), lambda c, i: (jnp.maximum(i * hb - 1, 0), off // wb + c)),
                  pl.BlockSpec((SUBLANES, wl), lambda c, i: (0, woff // wb + c)),
                  pl.BlockSpec((tr, wl), lambda c, i: (i, c))],
        out_specs=[pl.BlockSpec((tr, wl), lambda c, i: (i, c)),
                   pl.BlockSpec((SUBLANES, wl), lambda c, i: (0, c))],
        out_shape=[jax.ShapeDtypeStruct((T, nblk * LANES), F32),
                   jax.ShapeDtypeStruct((SUBLANES, nblk * LANES), F32)],
        compiler_params=_cp(2))(pm, pm, wc8, dout)


def prep_bwd_conv(dc, wc8, woff, nblk, *, name):
    T = dc.shape[0]
    tr = _tile(T, PREP_ROWS, SUBLANES)
    hb = tr // HALO
    nt = T // tr
    last_halo = T // HALO - 1
    wb = _heads_per_step(nblk)
    wl = wb * LANES

    def up(xx, w_ref):
        rows = xx.shape[0]
        acc = xx * w_ref[pl.ds(CONV_WIDTH - 1, 1), :]
        for d in range(1, CONV_WIDTH):
            acc = acc + pltpu.roll(xx, rows - d, 0) * w_ref[pl.ds(CONV_WIDTH - 1 - d, 1), :]
        return acc

    def body(x_ref, halo_ref, w_ref, o_ref):
        i = pl.program_id(0)
        xs = x_ref[...]
        hal = jnp.where(i == nt - 1, 0.0, halo_ref[...])
        cat = jnp.concatenate([xs[tr - HALO:tr], hal], axis=0)
        out = jnp.concatenate([up(xs, w_ref)[:tr - HALO], up(cat, w_ref)[0:HALO]], axis=0)
        o_ref[...] = out.astype(o_ref.dtype)

    return _pcall(
        body, name=name, grid=(nt, nblk // wb),
        in_specs=[pl.BlockSpec((tr, wl), lambda i, c: (i, c)),
                  pl.BlockSpec((HALO, wl), lambda i, c: (jnp.minimum((i + 1) * hb, last_halo), c)),
                  pl.BlockSpec((SUBLANES, wl), lambda i, c: (0, woff // wb + c))],
        out_specs=pl.BlockSpec((tr, wl), lambda i, c: (i, c)),
        out_shape=jax.ShapeDtypeStruct((T, nblk * LANES), BF), compiler_params=_cp(2))(dc, dc, wc8)


def _softplus(x):
    return jnp.maximum(x, 0.0) + jnp.log(1.0 + jnp.exp(-jnp.abs(x)))


def _tril_ones(c):
    t = lax.broadcasted_iota(jnp.int32, (c, c), 0)
    s = lax.broadcasted_iota(jnp.int32, (c, c), 1)
    return (t >= s).astype(F32)


GATE_CHUNKS_PER_STEP = 8


def _chunks_per_step(n_chunks):
    per = GATE_CHUNKS_PER_STEP
    while n_chunks % per:
        per //= 2
    return per


def _triu_ones(c):
    t = lax.broadcasted_iota(jnp.int32, (c, c), 0)
    s = lax.broadcasted_iota(jnp.int32, (c, c), 1)
    return (t <= s).astype(F32)


def gates_fwd(pg, arow, dtrow, H, *, name):
    T = pg.shape[0]
    C = LA_CHUNK
    N = T // C
    per = _chunks_per_step(N)

    def body(x_ref, a_ref, dt_ref, bg_ref, gr_ref):
        lane = lax.broadcasted_iota(jnp.int32, (C, LANES), 1)
        lm, um = _tril_ones(C), _triu_ones(C)
        for j in range(per):
            rows = slice(j * C, (j + 1) * C)
            x = x_ref[rows, :]
            g = -jnp.exp(a_ref[...]) * _softplus(x + dt_ref[...])
            g = jnp.where((lane >= H) & (lane < 2 * H), g, 0.0)
            bg_ref[rows, :] = jnp.where(lane < H, _sigmoid(x), _dot_hi(lm, g))
            gr_ref[j] = _dot_hi(g, um, TN)

    return _pcall(
        body, name=name, grid=(N // per,),
        in_specs=[pl.BlockSpec((per * C, LANES), lambda n: (n, 0)), _fixed((1, LANES)), _fixed((1, LANES))],
        out_specs=[pl.BlockSpec((per * C, LANES), lambda n: (n, 0)),
                   pl.BlockSpec((per, LANES, C), lambda n: (n, 0, 0))],
        out_shape=[jax.ShapeDtypeStruct((T, LANES), F32), jax.ShapeDtypeStruct((N, LANES, C), F32)],
        compiler_params=_cp(1))(pg, arow, dtrow)


def gates_bwd(pg, arow, dtrow, dbg, H, *, name):
    T = pg.shape[0]
    C = LA_CHUNK
    N = T // C
    per = _chunks_per_step(N)

    def body(x_ref, a_ref, dt_ref, d_ref, dx_ref, da_ref, ddt_ref):
        n = pl.program_id(0)
        lane = lax.broadcasted_iota(jnp.int32, (C, LANES), 1)
        in_g = (lane >= H) & (lane < 2 * H)
        e = jnp.exp(a_ref[...])
        lm = _tril_ones(C)
        pa = jnp.zeros((1, LANES), F32)
        pd = jnp.zeros((1, LANES), F32)
        for j in range(per):
            rows = slice(j * C, (j + 1) * C)
            x = x_ref[rows, :]
            d = d_ref[rows, :]
            xs = x + dt_ref[...]
            g = -e * _softplus(xs)
            dg = _dot_hi(lm, jnp.where(in_g, d, 0.0), TN)
            dxs = jnp.where(in_g, dg * (-e) * _sigmoid(xs), 0.0)
            beta = _sigmoid(x)
            dx_ref[rows, :] = jnp.where(lane < H, d * beta * (1.0 - beta), dxs).astype(dx_ref.dtype)
            pa = pa + jnp.sum(jnp.where(in_g, dg * g, 0.0), axis=0, keepdims=True)
            pd = pd + jnp.sum(dxs, axis=0, keepdims=True)

        @pl.when(n == 0)
        def _():
            da_ref[...] = pa
            ddt_ref[...] = pd

        @pl.when(n > 0)
        def _():
            da_ref[...] += pa
            ddt_ref[...] += pd

    rows = pl.BlockSpec((per * C, LANES), lambda n: (n, 0))
    return _pcall(
        body, name=name, grid=(N // per,),
        in_specs=[rows, _fixed((1, LANES)), _fixed((1, LANES)), rows],
        out_specs=[rows, _fixed((1, LANES)), _fixed((1, LANES))],
        out_shape=[jax.ShapeDtypeStruct((T, LANES), BF), jax.ShapeDtypeStruct((1, LANES), F32),
                   jax.ShapeDtypeStruct((1, LANES), F32)],
        compiler_params=_cp(1))(pg, arow, dtrow, dbg)


QK_SCALE = HEAD_DIM ** -0.5


HEADS_PER_STEP = 8


def _heads_per_step(H):
    hb = HEADS_PER_STEP
    while H % hb:
        hb //= 2
    return hb


def _head_rstd(o):
    return lax.rsqrt(jnp.mean(o * o, axis=-1, keepdims=True) + EPS)


def _gdn_gates(bg_ref, gr_ref, h, H):
    C = LA_CHUNK
    bgv = bg_ref[...]
    lane = lax.broadcasted_iota(jnp.int32, (C, LANES), 1)
    beta = jnp.sum(jnp.where(lane == h, bgv, 0.0), axis=1, keepdims=True)
    gc = jnp.sum(jnp.where(lane == H + h, bgv, 0.0), axis=1, keepdims=True)
    grow = gr_ref[pl.ds(H + h, 1), :]
    ri = lax.broadcasted_iota(jnp.int32, (C, 1), 0)
    gl = jnp.sum(jnp.where(ri == C - 1, gc, 0.0), axis=0, keepdims=True)
    return beta, gc, grow, gl


def _chunk_masks():
    C = LA_CHUNK
    ti = lax.broadcasted_iota(jnp.int32, (C, C), 0)
    si = lax.broadcasted_iota(jnp.int32, (C, C), 1)
    return ti >= si, ti > si, ti == si


def _decay(gc, grow, causal):
    return jnp.where(causal, jnp.exp(jnp.where(causal, gc - grow, 0.0)), 0.0)


def _interleave(gens):
    gens = list(gens)
    results = [None] * len(gens)
    live = list(range(len(gens)))
    while live:
        still = []
        for i in live:
            try:
                next(gens[i])
                still.append(i)
            except StopIteration as stop:
                results[i] = stop.value
        live = still
    return results


def _unit_lower_inverse(a, eye):
    x = -a
    p = jnp.where(eye, 1.0, 0.0) + x
    for _ in range(5):
        x = _dot_hi(x, x)
        yield
        p = p + _dot_hi(p, x)
        yield
    return p


def gdn_fwd(q, k, v, pm, zoff, bg, gcrow, wn, H, *, name, comm=None):
    T = q.shape[0]
    C = LA_CHUNK
    N = T // C
    hd = HEAD_DIM

    HB = _heads_per_step(H)
    n_ci = len(comm.ins) if comm else 0
    n_co = len(comm.outs) if comm else 0

    def body(*refs):
        q_ref, k_ref, v_ref, z_ref, bg_ref, gr_ref, wn_ref = refs[:7]
        ci = refs[7:7 + n_ci]
        og_ref, or_ref, sall_ref, tall_ref = refs[7 + n_ci:11 + n_ci]
        co = refs[11 + n_ci:11 + n_ci + n_co]
        S = refs[11 + n_ci + n_co]
        n = pl.program_id(0)
        hg = pl.program_id(1)
        causal, strict, eye = _chunk_masks()

        @pl.when((n == 0) & (hg == 0))
        def _():
            S[...] = jnp.zeros_like(S)
            if comm:
                comm.start(ci, co, refs[-2], refs[-1])

        states = [S[hg * HB + i] for i in range(HB)]

        def head(i):
            h = hg * HB + i
            sl = slice(i * hd, (i + 1) * hd)
            beta, gc, grow, gl = _gdn_gates(bg_ref, gr_ref, h, H)
            dm = _decay(gc, grow, causal)
            qs = q_ref[:, sl] * QK_SCALE
            kk = k_ref[:, sl]
            vv = v_ref[:, sl]
            eg = jnp.exp(gc)
            kb = kk * beta
            a = jnp.where(strict, _dot(kb, kk, NT) * dm, 0.0)
            yield
            tm = yield from _unit_lower_inverse(a, eye)
            u = _dot(tm, vv * beta)
            w = _dot(tm, kb * eg)
            qk = jnp.where(causal, _dot(qs, kk, NT) * dm, 0.0)
            yield
            s0 = states[i]
            vnew = u - _dot(w, s0)
            o = _dot(qs * eg, s0)
            yield
            o = o + _dot(qk, vnew)
            s1 = s0 * jnp.exp(gl) + _dot(kk * jnp.exp(gl - gc), vnew, TN)
            yield
            sall_ref[i] = s0
            tall_ref[i] = tm
            or_ref[:, sl] = o
            og_ref[:, sl] = (o * _head_rstd(o) * wn_ref[...] * _silu(z_ref[:, sl])).astype(og_ref.dtype)
            return s1

        for i, s1 in enumerate(_interleave([head(i) for i in range(HB)])):
            S[hg * HB + i] = s1

        if comm:
            @pl.when((n == N - 1) & (hg == H // HB - 1))
            def _():
                comm.finish(ci, co, refs[-2], refs[-1])

    blk = lambda off: pl.BlockSpec((C, HB * hd), lambda n, h: (n, off // HB + h))
    scratch_shapes = [pltpu.VMEM((H, hd, hd), F32)]
    if comm:
        scratch_shapes += [pltpu.SemaphoreType.DMA((comm.nsem,)), pltpu.SemaphoreType.DMA((comm.nsem,))]
    res = _pcall(
        body, name=name, grid=(N, H // HB),
        in_specs=[blk(0), blk(0), blk(0), blk(zoff),
                  pl.BlockSpec((C, LANES), lambda n, h: (n, 0)),
                  pl.BlockSpec((None, LANES, C), lambda n, h: (n, 0, 0)),
                  _fixed((1, hd)), *[ANY] * n_ci],
        out_specs=[blk(0), blk(0),
                   pl.BlockSpec((None, HB, hd, hd), lambda n, h: (n, h, 0, 0)),
                   pl.BlockSpec((None, HB, C, C), lambda n, h: (n, h, 0, 0)), *[ANY] * n_co],
        out_shape=[jax.ShapeDtypeStruct((T, 2 * H * hd), BF),
                   jax.ShapeDtypeStruct((T, H * hd), F32),
                   jax.ShapeDtypeStruct((N, H, hd, hd), F32), jax.ShapeDtypeStruct((N, H, C, C), F32),
                   *(comm.outs if comm else [])],
        scratch_shapes=scratch_shapes,
        compiler_params=_cp(2))(q, k, v, pm, bg, gcrow, wn, *(comm.ins if comm else []))
    return (*res[:4], list(res[4:])) if comm else tuple(res)


def gdn_bwd(q, k, v, pm, zoff, bg, gcrow, wn, oraw, sall, tall, dog, H, *, name):
    T = q.shape[0]
    C = LA_CHUNK
    N = T // C
    hd = HEAD_DIM

    HB = _heads_per_step(H)

    def body(*refs):
        dbg_ref, dwn_ref, dS = refs[15], refs[16], refs[17]
        n = pl.program_id(0)
        hg = pl.program_id(1)

        @pl.when((n == 0) & (hg == 0))
        def _():
            dwn_ref[...] = jnp.zeros_like(dwn_ref)
            dS[...] = jnp.zeros_like(dS)

        @pl.when(hg == 0)
        def _():
            dbg_ref[...] = jnp.zeros_like(dbg_ref)

        ds_in = [dS[hg * HB + i] for i in range(HB)]
        outs = _interleave([head(i, hg * HB + i, ds_in[i], *refs) for i in range(HB)])
        for i in range(HB):
            dS[hg * HB + i] = outs[i][0]
        dwn_ref[...] += sum(o[1] for o in outs)
        dbg_ref[...] += sum(o[2] for o in outs)

    def head(i, h, ds1, q_ref, k_ref, v_ref, z_ref, bg_ref, gr_ref, wn_ref, or_ref, sall_ref, tall_ref, dog_ref,
             dq_ref, dk_ref, dv_ref, dz_ref, dbg_ref, dwn_ref, dS):
        sl = slice(i * hd, (i + 1) * hd)
        beta, gc, grow, gl = _gdn_gates(bg_ref, gr_ref, h, H)
        causal, strict, eye = _chunk_masks()
        dm = _decay(gc, grow, causal)
        qs = q_ref[:, sl] * QK_SCALE
        kk = k_ref[:, sl]
        vv = v_ref[:, sl]
        zz = z_ref[:, sl]
        o = or_ref[:, sl]
        dog = dog_ref[:, sl]
        wn_v = wn_ref[...]
        s0 = sall_ref[i]
        tm = tall_ref[i]

        rstd = _head_rstd(o)
        on = o * rstd
        sz = _silu(zz)
        don = dog * wn_v * sz
        dwn_part = jnp.sum(dog * on * sz, axis=0, keepdims=True)
        dz_ref[:, sl] = (dog * on * wn_v * _dsilu(zz)).astype(dz_ref.dtype)
        do = rstd * (don - on * jnp.mean(don * on, axis=-1, keepdims=True))

        eg = jnp.exp(gc)
        kb = kk * beta
        vb = vv * beta
        kbg = kb * eg
        a = jnp.where(strict, _dot(kb, kk, NT) * dm, 0.0)
        u = _dot(tm, vb)
        w = _dot(tm, kbg)
        qk = jnp.where(causal, _dot(qs, kk, NT) * dm, 0.0)
        dqdec = _dot(do, s0, NT)
        yield
        vnew = u - _dot(w, s0)
        qdec = qs * eg
        etail = jnp.exp(gl - gc)
        ktail = kk * etail
        egl = jnp.exp(gl)
        dvnew = _dot(qk, do, TN) + _dot(ktail, ds1)
        yield
        dqk = jnp.where(causal, _dot(do, vnew, NT), 0.0)
        dktail = _dot(vnew, ds1, NT)
        dcd = jnp.sum(jnp.sum(s0 * ds1, axis=1, keepdims=True), axis=0, keepdims=True)
        ds0 = egl * ds1 + _dot(qdec, do, TN) - _dot(w, dvnew, TN)
        dw = -_dot(dvnew, s0, NT)
        dvb = _dot(tm, dvnew, TN)
        yield
        dkbg = _dot(tm, dw, TN)
        dtm = _dot(dvnew, vb, NT) + _dot(dw, kbg, NT)
        dqkr = dqk * dm
        dqs = _dot(dqkr, kk) + dqdec * eg
        yield
        x = _dot_hi(tm, dtm, TN)
        yield
        da = jnp.where(strict, -_dot_hi(x, tm, NT), 0.0)
        yield
        dkk = da * dm
        dkb = _dot(dkk, kk) + dkbg * eg
        dk = _dot(dkk, kb, TN)
        dk = dk + _dot(dqkr, qs, TN) + dktail * etail + dkb * beta
        g = da * a + dqk * qk
        colsum = jnp.max(_dot_hi(g, jnp.ones((C, LANES), F32), TN), axis=1, keepdims=True)
        yield
        rk = jnp.sum(dktail * ktail, axis=1, keepdims=True)
        dgc = (jnp.sum(g, axis=1, keepdims=True) - colsum
               + jnp.sum(dqdec * qdec, axis=1, keepdims=True) - rk
               + jnp.sum(dkbg * kbg, axis=1, keepdims=True))
        dgl = jnp.sum(rk, axis=0, keepdims=True) + dcd * egl
        ri = lax.broadcasted_iota(jnp.int32, (C, 1), 0)
        dgc = dgc + jnp.where(ri == C - 1, dgl, 0.0)
        dbeta = jnp.sum(dkb * kk, axis=1, keepdims=True) + jnp.sum(dvb * vv, axis=1, keepdims=True)

        dq_ref[:, sl] = dqs * QK_SCALE
        dk_ref[:, sl] = dk
        dv_ref[:, sl] = dvb * beta
        lane = lax.broadcasted_iota(jnp.int32, (C, LANES), 1)
        return ds0, dwn_part, jnp.where(lane == h, dbeta, 0.0) + jnp.where(lane == H + h, dgc, 0.0)

    blk = lambda off: pl.BlockSpec((C, HB * hd), lambda n, h: (N - 1 - n, off // HB + h))
    st = lambda r: pl.BlockSpec((None, HB, r, r), lambda n, h: (N - 1 - n, h, 0, 0))
    return _pcall(
        body, name=name, grid=(N, H // HB),
        in_specs=[blk(0), blk(0), blk(0), blk(zoff),
                  pl.BlockSpec((C, LANES), lambda n, h: (N - 1 - n, 0)),
                  pl.BlockSpec((None, LANES, C), lambda n, h: (N - 1 - n, 0, 0)),
                  _fixed((1, hd)), blk(0), st(hd), st(C), blk(0)],
        out_specs=[blk(0), blk(0), blk(0), blk(0),
                   pl.BlockSpec((C, LANES), lambda n, h: (N - 1 - n, 0)), _fixed((1, hd))],
        out_shape=[jax.ShapeDtypeStruct((T, H * hd), F32)] * 3
        + [jax.ShapeDtypeStruct((T, H * hd), BF), jax.ShapeDtypeStruct((T, LANES), F32),
           jax.ShapeDtypeStruct((1, hd), F32)],
        scratch_shapes=[pltpu.VMEM((H, hd, hd), F32)],
        compiler_params=_cp(2))(q, k, v, pm, bg, gcrow, wn, oraw, sall, tall, dog)


def _rot(x, cs, sn):
    return x * cs + pltpu.roll(x, HEAD_DIM // 2, 1) * sn


def _rot_t(dy, cs, sn):
    return dy * cs + pltpu.roll(dy * sn, HEAD_DIM // 2, 1)


def ret_fwd(pm, qoff, koff, voff, goff, cs, sn, dmat, avec, bvec, gam, og_buf, H, *, name):
    T = pm.shape[0]
    C = LA_CHUNK
    N = T // C
    hd = HEAD_DIM

    HB = _heads_per_step(H)

    def body(q_ref, k_ref, v_ref, g_ref, cs_ref, sn_ref, dm_ref, a_ref, b_ref, gam_ref, _og_in,
             og_ref, or_ref, sall_ref, S):
        n = pl.program_id(0)
        hg = pl.program_id(1)
        c, s = cs_ref[...], sn_ref[...]

        @pl.when((n == 0) & (hg == 0))
        def _():
            S[...] = jnp.zeros_like(S)

        states = [S[hg * HB + i] for i in range(HB)]

        def head(i):
            sl = slice(i * hd, (i + 1) * hd)
            qq = _rot(q_ref[:, sl], c, s)
            kk = _rot(k_ref[:, sl], c, s) * QK_SCALE
            vv = v_ref[:, sl]
            s0 = states[i]
            p = _dot(qq, kk, NT) * dm_ref[i]
            cross = _dot(qq * a_ref[i], s0)
            s1 = s0 * gam_ref[i] + _dot(kk * b_ref[i], vv, TN)
            yield
            o = _dot(p, vv) + cross
            yield
            sall_ref[i] = s0
            or_ref[:, sl] = o
            og_ref[:, sl] = (_silu(g_ref[:, sl]) * o * _head_rstd(o)).astype(og_ref.dtype)
            return s1

        for i, s1 in enumerate(_interleave([head(i) for i in range(HB)])):
            S[hg * HB + i] = s1

    blk = lambda off: pl.BlockSpec((C, HB * hd), lambda n, h: (n, off // HB + h))
    tab = pl.BlockSpec((C, hd), lambda n, h: (n, 0))
    per_h = lambda r, cdim: pl.BlockSpec((HB, r, cdim), lambda n, h: (h, 0, 0))
    return _pcall(
        body, name=name, grid=(N, H // HB),
        in_specs=[blk(qoff), blk(koff), blk(voff), blk(goff), tab, tab,
                  per_h(C, C), per_h(C, hd), per_h(C, hd), per_h(1, hd), ANY],
        out_specs=[blk(H), blk(0), pl.BlockSpec((None, HB, hd, hd), lambda n, h: (n, h, 0, 0))],
        out_shape=[jax.ShapeDtypeStruct(og_buf.shape, og_buf.dtype), jax.ShapeDtypeStruct((T, H * hd), F32),
                   jax.ShapeDtypeStruct((N, H, hd, hd), F32)],
        input_output_aliases={10: 0},
        scratch_shapes=[pltpu.VMEM((H, hd, hd), F32)],
        compiler_params=_cp(2))(pm, pm, pm, pm, cs, sn, dmat, avec, bvec, gam, og_buf)


def ret_bwd(pm, qoff, koff, voff, goff, cs, sn, dmat, avec, bvec, gam, oraw, sall, dog, dogoff, H, *, name):
    T = pm.shape[0]
    C = LA_CHUNK
    N = T // C
    hd = HEAD_DIM

    HB = _heads_per_step(H)

    def body(q_ref, k_ref, v_ref, g_ref, cs_ref, sn_ref, dm_ref, a_ref, b_ref, gam_ref, or_ref, sall_ref,
             dog_ref, dq_ref, dk_ref, dv_ref, dg_ref, dS):
        n = pl.program_id(0)
        hg = pl.program_id(1)
        c, s = cs_ref[...], sn_ref[...]

        @pl.when((n == 0) & (hg == 0))
        def _():
            dS[...] = jnp.zeros_like(dS)

        dstates = [dS[hg * HB + i] for i in range(HB)]

        def head(i):
            sl = slice(i * hd, (i + 1) * hd)
            qq = _rot(q_ref[:, sl], c, s)
            kk = _rot(k_ref[:, sl], c, s) * QK_SCALE
            vv = v_ref[:, sl]
            gg = g_ref[:, sl]
            o = or_ref[:, sl]
            dog = dog_ref[:, sl]
            dm = dm_ref[i]
            av, bv = a_ref[i], b_ref[i]
            s0 = sall_ref[i]
            ds1 = dstates[i]

            rstd = _head_rstd(o)
            on = o * rstd
            don = dog * _silu(gg)
            dg_ref[:, sl] = (dog * on * _dsilu(gg)).astype(dg_ref.dtype)
            do = rstd * (don - on * jnp.mean(don * on, axis=-1, keepdims=True))

            p = _dot(qq, kk, NT) * dm
            dp = _dot(do, vv, NT) * dm
            cross_q = _dot(do, s0, NT) * av
            cross_k = _dot(vv, ds1, NT) * bv
            cross_v = _dot(kk * bv, ds1)
            ds0 = ds1 * gam_ref[i] + _dot(qq * av, do, TN)
            yield
            dv_ref[:, sl] = (_dot(p, do, TN) + cross_v).astype(dv_ref.dtype)
            dqq = _dot(dp, kk) + cross_q
            dkk = (_dot(dp, qq, TN) + cross_k) * QK_SCALE
            yield
            dq_ref[:, sl] = _rot_t(dqq, c, s).astype(dq_ref.dtype)
            dk_ref[:, sl] = _rot_t(dkk, c, s).astype(dk_ref.dtype)
            return ds0

        for i, ds0 in enumerate(_interleave([head(i) for i in range(HB)])):
            dS[hg * HB + i] = ds0

    blk = lambda off: pl.BlockSpec((C, HB * hd), lambda n, h: (N - 1 - n, off // HB + h))
    tab = pl.BlockSpec((C, hd), lambda n, h: (N - 1 - n, 0))
    per_h = lambda r, cdim: pl.BlockSpec((HB, r, cdim), lambda n, h: (h, 0, 0))
    return _pcall(
        body, name=name, grid=(N, H // HB),
        in_specs=[blk(qoff), blk(koff), blk(voff), blk(goff), tab, tab,
                  per_h(C, C), per_h(C, hd), per_h(C, hd), per_h(1, hd), blk(0),
                  pl.BlockSpec((None, HB, hd, hd), lambda n, h: (N - 1 - n, h, 0, 0)), blk(dogoff)],
        out_specs=[blk(0)] * 4,
        out_shape=[jax.ShapeDtypeStruct((T, H * hd), BF)] * 4,
        scratch_shapes=[pltpu.VMEM((H, hd, hd), F32)],
        compiler_params=_cp(2))(pm, pm, pm, pm, cs, sn, dmat, avec, bvec, gam, oraw, sall, dog)


LN_ROWS = 128


def ln_fwd(pre, lw, lb, *, name):
    T, W2 = pre.shape
    W = W2 // 2
    tr = _tile(T, LN_ROWS, SUBLANES)

    def body(p_ref, w_ref, b_ref, o_ref):
        v = _gelu(p_ref[...])
        xc = v - jnp.mean(v, axis=-1, keepdims=True)
        r = lax.rsqrt(jnp.mean(xc * xc, axis=-1, keepdims=True) + EPS)
        o_ref[...] = xc * r * w_ref[...] + b_ref[...]

    return _pcall(body, name=name, grid=(T // tr,),
                  in_specs=[pl.BlockSpec((tr, W), lambda i: (i, 1)), _fixed((1, W)), _fixed((1, W))],
                  out_specs=_rows(tr, W), out_shape=jax.ShapeDtypeStruct((T, W), F32),
                  compiler_params=_cp(1))(pre, lw, lb)


def ln_bwd(pre, lw, dvn, dpre_buf, *, name):
    T, W2 = pre.shape
    W = W2 // 2
    tr = _tile(T, LN_ROWS, SUBLANES)

    def body(p_ref, w_ref, d_ref, _dp_in, dp_ref, dw_ref, db_ref):
        i = pl.program_id(0)
        v, dgelu = _gelu_and_grad(p_ref[...])
        xc = v - jnp.mean(v, axis=-1, keepdims=True)
        r = lax.rsqrt(jnp.mean(xc * xc, axis=-1, keepdims=True) + EPS)
        xh = xc * r
        d = d_ref[...]
        dxh = d * w_ref[...]
        dv = r * (dxh - jnp.mean(dxh, axis=-1, keepdims=True) - xh * jnp.mean(dxh * xh, axis=-1, keepdims=True))
        dp_ref[...] = (dv * dgelu).astype(dp_ref.dtype)
        pw = jnp.sum(d * xh, axis=0, keepdims=True)
        pb = jnp.sum(d, axis=0, keepdims=True)

        @pl.when(i == 0)
        def _():
            dw_ref[...] = pw
            db_ref[...] = pb

        @pl.when(i > 0)
        def _():
            dw_ref[...] += pw
            db_ref[...] += pb

    return _pcall(body, name=name, grid=(T // tr,),
                  in_specs=[pl.BlockSpec((tr, W), lambda i: (i, 1)), _fixed((1, W)), _rows(tr, W), ANY],
                  out_specs=[pl.BlockSpec((tr, W), lambda i: (i, 1)), _fixed((1, W)), _fixed((1, W))],
                  out_shape=[jax.ShapeDtypeStruct(dpre_buf.shape, dpre_buf.dtype), jax.ShapeDtypeStruct((1, W), F32),
                             jax.ShapeDtypeStruct((1, W), F32)],
                  input_output_aliases={3: 0},
                  compiler_params=_cp(1))(pre, lw, dvn, dpre_buf)


def _tril_mask(c):
    t = lax.broadcasted_iota(jnp.int32, (c, c), 0)
    s = lax.broadcasted_iota(jnp.int32, (c, c), 1)
    return t >= s


def sg_fwd(pre, vn, ws, bs3, *, name):
    T, W = vn.shape
    G = ws.shape[0]
    gd = W // G
    C = SG_CHUNK

    def body(p_ref, v_ref, w_ref, b_ref, o_ref):
        mask = _tril_mask(C)
        for g in range(G):
            sl = slice(g * gd, (g + 1) * gd)
            wm = jnp.where(mask, w_ref[g], 0.0)
            s = _dot(wm, v_ref[:, sl]) + b_ref[g]
            o_ref[:, sl] = (_gelu(p_ref[:, sl]) * s).astype(o_ref.dtype)

    blk = pl.BlockSpec((C, W), lambda n: (n, 0))
    return _pcall(body, name=name, grid=(T // C,),
                  in_specs=[blk, blk, _fixed((G, C, C)), _fixed((G, C, 1))],
                  out_specs=blk, out_shape=jax.ShapeDtypeStruct((T, W), BF),
                  compiler_params=_cp(1))(pre, vn, ws, bs3)


def sg_bwd(pre, vn, ws, bs3, dus, *, name):
    T, W = vn.shape
    G = ws.shape[0]
    gd = W // G
    C = SG_CHUNK

    def body(p_ref, v_ref, w_ref, b_ref, d_ref, dp_ref, dv_ref, dw_ref, db_ref):
        n = pl.program_id(0)
        mask = _tril_mask(C)

        @pl.when(n == 0)
        def _():
            dw_ref[...] = jnp.zeros_like(dw_ref)
            db_ref[...] = jnp.zeros_like(db_ref)

        for g in range(G):
            sl = slice(g * gd, (g + 1) * gd)
            wm = jnp.where(mask, w_ref[g], 0.0)
            u, du = _gelu_and_grad(p_ref[:, sl])
            vv = v_ref[:, sl]
            d = d_ref[:, sl]
            s = _dot(wm, vv) + b_ref[g]
            ds = d * u
            dp_ref[:, sl] = (d * s * du).astype(dp_ref.dtype)
            dv_ref[:, sl] = _dot(wm, ds, TN)
            dw_ref[g] += jnp.where(mask, _dot(ds, vv, NT), 0.0)
            db_ref[g] += jnp.sum(ds, axis=1, keepdims=True)

    blk = pl.BlockSpec((C, W), lambda n: (n, 0))
    return _pcall(body, name=name, grid=(T // C,),
                  in_specs=[blk, blk, _fixed((G, C, C)), _fixed((G, C, 1)), blk],
                  out_specs=[blk, blk, _fixed((G, C, C)), _fixed((G, C, 1))],
                  out_shape=[jax.ShapeDtypeStruct((T, 2 * W), BF),
                             jax.ShapeDtypeStruct((T, W), F32),
                             jax.ShapeDtypeStruct((G, C, C), F32), jax.ShapeDtypeStruct((G, C, 1), F32)],
                  compiler_params=_cp(1))(pre, vn, ws, bs3, dus)


CHIP_RELATIONS = ((1, 0), (0, 1), (1, 1))


def _place():
    return lax.axis_index("x"), lax.axis_index("y"), lax.axis_index("c")


def _peer_chip(x, y, r):
    fx, fy = CHIP_RELATIONS[r]
    return (1 - x if fx else x), (1 - y if fy else y)


def gather_comm(arrs):
    n = len(arrs)
    per = 2 * len(CHIP_RELATIONS) + 1
    own = per - 1

    def ici(a, r, ins, outs, send, recv):
        x, y, c = _place()
        px, py = _peer_chip(x, y, r)
        return pltpu.make_async_remote_copy(
            src_ref=ins[a].at[c], dst_ref=outs[a].at[2 * x + y, c], send_sem=send.at[a * per + r],
            recv_sem=recv.at[a * per + r], device_id=(px, py, c), device_id_type=MESH)

    def own_block(a, ins, outs, send, recv):
        x, y, c = _place()
        return pltpu.make_async_remote_copy(
            src_ref=ins[a], dst_ref=outs[a].at[2 * x + y], send_sem=send.at[a * per + own],
            recv_sem=recv.at[a * per + own], device_id=(x, y, 1 - c), device_id_type=MESH)

    def start(ins, outs, send, recv):
        for a in range(n):
            for r in range(3):
                ici(a, r, ins, outs, send, recv).start()
            own_block(a, ins, outs, send, recv).start()

    def finish(ins, outs, send, recv):
        x, y, c = _place()
        sib = (x, y, 1 - c)
        forwards = []
        for a in range(n):
            for r in range(3):
                px, py = _peer_chip(x, y, r)
                landed = outs[a].at[2 * px + py, c]
                pltpu.make_async_remote_copy(
                    src_ref=landed, dst_ref=landed, send_sem=send.at[a * per + r],
                    recv_sem=recv.at[a * per + r], device_id=(px, py, c), device_id_type=MESH).wait_recv()
                fw = pltpu.make_async_remote_copy(
                    src_ref=landed, dst_ref=landed, send_sem=send.at[a * per + 3 + r],
                    recv_sem=recv.at[a * per + 3 + r], device_id=sib, device_id_type=MESH)
                fw.start()
                forwards.append(fw)
        for a in range(n):
            for r in range(3):
                px, py = _peer_chip(x, y, r)
                other = outs[a].at[2 * px + py, 1 - c]
                pltpu.make_async_remote_copy(
                    src_ref=other, dst_ref=other, send_sem=send.at[a * per + 3 + r],
                    recv_sem=recv.at[a * per + 3 + r], device_id=sib, device_id_type=MESH).wait_recv()
        for a in range(n):
            for r in range(3):
                ici(a, r, ins, outs, send, recv).wait_send()
            own_block(a, ins, outs, send, recv).wait()
        for fw in forwards:
            fw.wait_send()

    outs = [jax.ShapeDtypeStruct((N_CHIPS,) + a.shape, a.dtype) for a in arrs]
    return Comm(list(arrs), outs, n * per, start, finish)


def chip_exchange_comm(ps):
    n = len(ps)

    def copies(ins, outs, send, recv):
        x, y, c = _place()
        cps = []
        for a in range(n):
            for r in range(3):
                px, py = _peer_chip(x, y, r)
                cps.append(pltpu.make_async_remote_copy(
                    src_ref=ins[a].at[2 * px + py], dst_ref=outs[a].at[r], send_sem=send.at[3 * a + r],
                    recv_sem=recv.at[3 * a + r], device_id=(px, py, c), device_id_type=MESH))
        return cps

    def start(ins, outs, send, recv):
        for cp in copies(ins, outs, send, recv):
            cp.start()

    def finish(ins, outs, send, recv):
        for cp in copies(ins, outs, send, recv):
            cp.wait()

    outs = [jax.ShapeDtypeStruct((3,) + p.shape[1:], p.dtype) for p in ps]
    return Comm(list(ps), outs, 3 * n, start, finish)


def run_comm(comm, *, name):
    n_i, n_o = len(comm.ins), len(comm.outs)

    def body(*refs):
        ins, outs = refs[:n_i], refs[n_i:n_i + n_o]
        send, recv = refs[n_i + n_o:]
        comm.start(ins, outs, send, recv)
        comm.finish(ins, outs, send, recv)

    res = _pcall(body, name=name, in_specs=[ANY] * n_i, out_specs=[ANY] * n_o, out_shape=comm.outs,
                 scratch_shapes=[pltpu.SemaphoreType.DMA((comm.nsem,)), pltpu.SemaphoreType.DMA((comm.nsem,))])(*comm.ins)
    return list(res)


def pair_exchange(gs, *, name):
    n = len(gs)

    def body(*refs):
        ins, outs = refs[:n], refs[n:2 * n]
        send, recv = refs[2 * n:2 * n + 2]
        x, y, c = _place()
        cps = []
        for a in range(n):
            cp = pltpu.make_async_remote_copy(
                src_ref=ins[a].at[:, pl.ds(1 - c, 1)], dst_ref=outs[a], send_sem=send.at[a], recv_sem=recv.at[a],
                device_id=(x, y, 1 - c), device_id_type=MESH)
            cp.start()
            cps.append(cp)
        for cp in cps:
            cp.wait()

    out_shape = [jax.ShapeDtypeStruct((g.shape[0], 1) + g.shape[2:], g.dtype) for g in gs]
    res = _pcall(body, name=name, in_specs=[ANY] * n, out_specs=[ANY] * n, out_shape=out_shape,
                 scratch_shapes=[pltpu.SemaphoreType.DMA((n,)), pltpu.SemaphoreType.DMA((n,))])(*gs)
    return list(res)


def pair_share(fs, *, name):
    n = len(fs)

    def body(*refs):
        ins, outs = refs[:n], refs[n:2 * n]
        send, recv = refs[2 * n:2 * n + 2]
        x, y, c = _place()
        cps = []
        for a in range(n):
            cp = pltpu.make_async_remote_copy(
                src_ref=ins[a], dst_ref=outs[a], send_sem=send.at[a], recv_sem=recv.at[a],
                device_id=(x, y, 1 - c), device_id_type=MESH)
            cp.start()
            cps.append(cp)
        for cp in cps:
            cp.wait()

    out_shape = [jax.ShapeDtypeStruct(f.shape, f.dtype) for f in fs]
    res = _pcall(body, name=name, in_specs=[ANY] * n, out_specs=[ANY] * n, out_shape=out_shape,
                 scratch_shapes=[pltpu.SemaphoreType.DMA((n,)), pltpu.SemaphoreType.DMA((n,))])(*fs)
    return list(res)


def all_reduce_small(v, *, name):
    R = v.shape[0]

    def body(v_ref, sum_ref, gat_ref, send, recv):
        x, y, c = _place()
        me = 4 * x + 2 * y + c
        gat_ref[me] = v_ref[...]
        cps = []
        peers = []
        for r in range(1, N_DEV):
            fx, fy, fc = (r >> 2) & 1, (r >> 1) & 1, r & 1
            px, py, pc = (1 - x if fx else x), (1 - y if fy else y), (1 - c if fc else c)
            peers.append((px, py, pc))
            cp = pltpu.make_async_remote_copy(
                src_ref=v_ref, dst_ref=gat_ref.at[me], send_sem=send.at[r - 1], recv_sem=recv.at[r - 1],
                device_id=(px, py, pc), device_id_type=MESH)
            cp.start()
            cps.append(cp)
        for r in range(1, N_DEV):
            px, py, pc = peers[r - 1]
            slot = gat_ref.at[4 * px + 2 * py + pc]
            pltpu.make_async_remote_copy(
                src_ref=v_ref, dst_ref=slot, send_sem=send.at[r - 1], recv_sem=recv.at[r - 1],
                device_id=(px, py, pc), device_id_type=MESH).wait_recv()
        for cp in cps:
            cp.wait_send()
        acc = gat_ref[0]
        for s in range(1, N_DEV):
            acc = acc + gat_ref[s]
        sum_ref[...] = acc

    vm = pl.BlockSpec(memory_space=pltpu.VMEM)
    res = _pcall(body, name=name, in_specs=[vm], out_specs=[vm, vm],
                 out_shape=[jax.ShapeDtypeStruct((R, LANES), F32), jax.ShapeDtypeStruct((N_DEV, R, LANES), F32)],
                 scratch_shapes=[pltpu.SemaphoreType.DMA((N_DEV - 1,)), pltpu.SemaphoreType.DMA((N_DEV - 1,))],
                 compiler_params=pltpu.CompilerParams(vmem_limit_bytes=VMEM_LIMIT))(v)
    return res[0]


def pair_sum(g, r1, c_idx, *, name):
    nb, _, hr, C = g.shape
    tr = _tile(hr, max(BF16_ROWS, (1 << 18) // C), BF16_ROWS)

    def body(c_ref, g_ref, r_ref, pb_ref):
        pb_ref[...] = (g_ref[...] + r_ref[...].astype(F32)).astype(pb_ref.dtype)

    gs = pltpu.PrefetchScalarGridSpec(
        num_scalar_prefetch=1, grid=(nb, hr // tr),
        in_specs=[pl.BlockSpec((None, None, tr, C), lambda b, i, cr: (b, cr[0], i, 0)),
                  pl.BlockSpec((None, None, tr, C), lambda b, i, cr: (b, 0, i, 0))],
        out_specs=pl.BlockSpec((None, tr, C), lambda b, i, cr: (b, i, 0)))
    return _pcall(body, name=name, grid_spec=gs, out_shape=jax.ShapeDtypeStruct((nb, hr, C), BF),
                  compiler_params=_cp(2))(c_idx, g, r1)


def chip_sum(g, r1, r2, c_idx, j_idx, *, name, layer=0, n_layers=1, into=None):
    _, _, hr, C = g.shape
    tr = _tile(hr, max(BF16_ROWS, (1 << 18) // C), BF16_ROWS)

    def body(c_ref, j_ref, g_ref, s_ref, a_ref, b_ref, d_ref, *rest):
        o_ref = rest[-1]
        own = g_ref[...] + s_ref[...].astype(F32)
        o_ref[...] = ((own + a_ref[...].astype(F32)) + b_ref[...].astype(F32)) + d_ref[...].astype(F32)

    rel = lambda r: pl.BlockSpec((None, tr, C), lambda i, cr, jr: (r, i, 0))
    in_specs = [pl.BlockSpec((None, None, tr, C), lambda i, cr, jr: (jr[0], cr[0], i, 0)),
                pl.BlockSpec((None, None, tr, C), lambda i, cr, jr: (jr[0], 0, i, 0)), rel(0), rel(1), rel(2)]
    operands = [c_idx, j_idx, g, r1, r2, r2, r2]
    aliases = {}
    if into is not None:
        in_specs.append(ANY)
        operands.append(into)
        aliases = {len(operands) - 1: 0}
    gs = pltpu.PrefetchScalarGridSpec(
        num_scalar_prefetch=2, grid=(hr // tr,), in_specs=in_specs,
        out_specs=pl.BlockSpec((None, tr, C), lambda i, cr, jr: (layer, i, 0)))
    return _pcall(body, name=name, grid_spec=gs, out_shape=jax.ShapeDtypeStruct((n_layers, hr, C), F32),
                  input_output_aliases=aliases, compiler_params=_cp(1))(*operands)


def adamw_halves(w, g_mine, g_other, m, v, c_idx, *, name):
    L, R, C = w.shape
    hr = R // 2
    tr = _tile(hr, max(SUBLANES, (1 << 18) // C), SUBLANES)
    nbh = hr // tr
    c1 = 1.0 - ADAM_B1 ** ADAM_STEP
    c2 = 1.0 - ADAM_B2 ** ADAM_STEP

    def body(c_ref, w_ref, gm_ref, go_ref, m_ref, v_ref, g_ref, d_ref, mo_ref, vo_ref):
        i = pl.program_id(1)
        gv = jnp.where(i // nbh == c_ref[0], gm_ref[...], go_ref[...])
        m2 = ADAM_B1 * m_ref[...] + (1.0 - ADAM_B1) * gv
        v2 = ADAM_B2 * v_ref[...] + (1.0 - ADAM_B2) * (gv * gv)
        d_ref[...] = -ADAM_LR * ((m2 / c1) / (jnp.sqrt(v2 / c2) + ADAM_EPS) + ADAM_WD * w_ref[...])
        g_ref[...] = gv
        mo_ref[...] = m2
        vo_ref[...] = v2

    full = pl.BlockSpec((None, tr, C), lambda l, i, cr: (l, i, 0))
    half = pl.BlockSpec((None, tr, C), lambda l, i, cr: (l, i % nbh, 0))
    gs = pltpu.PrefetchScalarGridSpec(
        num_scalar_prefetch=1, grid=(L, R // tr),
        in_specs=[full, half, half, full, full], out_specs=[full] * 4)
    sds = jax.ShapeDtypeStruct((L, R, C), F32)
    return _pcall(body, name=name, grid_spec=gs, out_shape=[sds] * 4,
                  compiler_params=_cp(2))(c_idx, w, g_mine, g_other, m, v)


def _pack_rows(arrs):
    parts = []
    for a in arrs:
        flat = a.reshape(-1).astype(F32)
        tile = SUBLANES * LANES
        pad = (-flat.shape[0]) % tile
        parts.append(jnp.pad(flat, (0, pad)).reshape(-1, LANES))
    return jnp.concatenate(parts, axis=0)


def _unpack_rows(buf, shapes):
    out, row = [], 0
    for shp in shapes:
        size = int(np.prod(shp))
        rows = -(-size // (SUBLANES * LANES)) * SUBLANES
        out.append(buf[row:row + rows].reshape(-1)[:size].reshape(shp))
        row += rows
    return out


def _halves(a2d):
    r, c = a2d.shape
    return a2d.reshape(2, r // 2, c)


def _rotary_tables(T):
    half = HEAD_DIM // 2
    pos = jnp.arange(T, dtype=F32)
    inv_freq = 1.0 / (ROPE_BASE ** jnp.linspace(0.0, 1.0, half, dtype=F32))
    ang = pos[:, None] * inv_freq[None, :]
    cos, sin = jnp.cos(ang), jnp.sin(ang)
    return jnp.concatenate([cos, cos], axis=1), jnp.concatenate([-sin, sin], axis=1)


def _retention_tables(H):
    C = LA_CHUNK
    lg = jnp.log1p(-jnp.power(2.0, -5.0 - jnp.arange(H, dtype=F32)))
    pos = jnp.arange(C, dtype=F32)
    causal = jnp.tril(jnp.ones((C, C), dtype=bool))
    dmat = jnp.exp(jnp.where(causal, (pos[:, None] - pos[None, :]) * lg[:, None, None], -jnp.inf))
    bc = lambda t: jnp.broadcast_to(t[..., None], t.shape + (HEAD_DIM,))
    avec = bc(jnp.exp((pos + 1.0)[None, :] * lg[:, None]))
    bvec = bc(jnp.exp((C - 1.0 - pos)[None, :] * lg[:, None]))
    gam = bc(jnp.exp(C * lg)[:, None])
    return dmat, avec, bvec, gam


def _relu2(acc):
    return acc, jnp.square(jnp.maximum(acc, 0.0))


def _drelu2(acc, up):
    return (acc * (2.0 * jnp.maximum(up, 0.0)),)


ROW_SHARDED = ("la_out", "sg_out", "ffn_down0", "ffn_down1")


class ExchangePlan:
    GATHERS = {"la_in_main": ("la_out", "ffn_up0"), "gdn_fwd": ("ffn_down0",), "ffn_up_0": ("sg_in",),
               "ffn_down_0": ("sg_out", "ffn_up1"), "sg_in": ("ffn_down1",)}
    REDUCES = {"ffn_dup_1": "ffn_down1", "ffn_dy_1": "ffn_up1", "sg_dus": "sg_out", "sg_dy": "sg_in",
               "ffn_dup_0": "ffn_down0", "ffn_dy_0": "ffn_up0", "la_docat": "la_out", "la_dy": "la_in"}

    def __init__(self, shard_halves, c_idx, j_idx):
        self.shard_halves, self.c_idx, self.j_idx = shard_halves, c_idx, j_idx
        self.partial = {}
        self.finished = {}

    def comm(self, carrier):
        if carrier in self.GATHERS:
            return gather_comm([self.shard_halves[w] for w in self.GATHERS[carrier]])
        if carrier in self.REDUCES:
            return chip_exchange_comm([self.partial[self.REDUCES[carrier]][2]])
        return None

    def done(self, carrier, outs, W):
        if carrier in self.GATHERS:
            for w, g in zip(self.GATHERS[carrier], outs):
                install_gathered(W, w, g)
        else:
            w = self.REDUCES[carrier]
            per_layer = w[:-1] in ("ffn_up", "ffn_down")
            key, layer, n_layers = (w[:-1], int(w[-1]), 2) if per_layer else (w, 0, 1)
            g, sib, _ = self.partial[w]
            self.finished[key] = chip_sum(g, sib, outs[0], self.c_idx, self.j_idx, name=f"grads_chip_sum_{w}",
                                          layer=layer, n_layers=n_layers, into=self.finished.get(key))

    def grad_ready(self, w, g, payload=None):
        halves = lambda t: t.reshape(N_CHIPS, 2, t.shape[1] // 2, t.shape[2])
        sib = pair_exchange([halves(g if payload is None else payload)], name=f"grads_pair_exchange_{w}")[0]
        self.partial[w] = (halves(g), sib, pair_sum(halves(g), sib, self.c_idx, name=f"grads_pair_sum_{w}"))


def install_gathered(W, w, g):
    whole = g.reshape(N_CHIPS, g.shape[1] * g.shape[2], g.shape[3])
    if w in ROW_SHARDED:
        whole = whole.reshape(-1, whole.shape[-1])
    if w[:-1] in ("ffn_up", "ffn_down"):
        W[w[:-1]][int(w[-1])] = whole
    else:
        W[w] = whole


def _by_chip(w, g):
    return g.reshape(N_CHIPS, -1, g.shape[-1]) if w in ROW_SHARDED else g


def _la_shard_rows(H):
    cs = (8 * H * HEAD_DIM + 2 * H) // N_CHIPS
    return cs, -(-cs // (2 * BF16_ROWS)) * (2 * BF16_ROWS)


def _la_pieces(H):
    HD = H * HEAD_DIM
    mix = 8 * HD + 2 * H
    cs = mix // N_CHIPS
    segments = [(0, 0, 4 * HD, 0), (1, 4 * HD, 4 * HD + 2 * H, 0), (0, 4 * HD + 2 * H, mix, 4 * HD)]
    pieces = []
    for j in range(N_CHIPS):
        mine = []
        for src, a, b, base in segments:
            lo, hi = max(cs * j, a), min(cs * (j + 1), b)
            if lo < hi:
                mine.append((src, base + lo - a, base + hi - a))
        pieces.append(mine)
    return pieces


def _la_weights_from_gathered(g, H):
    _, csp, D = g.shape
    tc = _tile(D, 2 * LANES)
    n_main = 8 * H * HEAD_DIM

    def body(g_ref, main_ref, gate_ref):
        parts = {0: [], 1: []}
        for j, mine in enumerate(_la_pieces(H)):
            row = 0
            for src, a, b in mine:
                parts[src].append(g_ref[j, row:row + b - a, :])
                row += b - a
        main_ref[...] = jnp.concatenate(parts[0], axis=0)
        gate = jnp.concatenate(parts[1], axis=0)
        gate_ref[...] = jnp.concatenate([gate, jnp.zeros((LANES - gate.shape[0], tc), gate.dtype)], axis=0)

    return _pcall(body, name="la_weights", grid=(D // tc,),
                  in_specs=[pl.BlockSpec((N_CHIPS, csp, tc), lambda i: (0, 0, i))],
                  out_specs=[pl.BlockSpec((n_main, tc), lambda i: (0, i)), pl.BlockSpec((LANES, tc), lambda i: (0, i))],
                  out_shape=[jax.ShapeDtypeStruct((n_main, D), g.dtype), jax.ShapeDtypeStruct((LANES, D), g.dtype)],
                  compiler_params=_cp(1))(g)


def _la_dproj_by_chip(main_parts, dpg, H):
    HD = H * HEAD_DIM
    cs, csp = _la_shard_rows(H)
    T = dpg.shape[0]
    tr = _tile(T, ROW_TILE, BF16_ROWS)
    n = len(main_parts)

    def body(*refs):
        parts, g_ref, o_ref = refs[:n], refs[n], refs[n + 1]
        pad = jnp.zeros((tr, csp - cs), o_ref.dtype)
        cols = []
        for mine in _la_pieces(H):
            for src, a, b in mine:
                while src == 0 and a < b:
                    i, off = divmod(a, HD)
                    end = min(b, (i + 1) * HD)
                    cols.append(parts[i][:, off:off + end - a])
                    a = end
                if src == 1:
                    cols.append(g_ref[:, a:b])
            cols.append(pad)
        o_ref[...] = jnp.concatenate(cols, axis=1)

    return _pcall(body, name="la_dproj", grid=(T // tr,),
                  in_specs=[_rows(tr, HD)] * n + [_rows(tr, LANES)], out_specs=_rows(tr, N_CHIPS * csp),
                  out_shape=jax.ShapeDtypeStruct((T, N_CHIPS * csp), BF), compiler_params=_cp(1))(*main_parts, dpg)


def _train_local(x2, tgt, W, plan=None):
    T, D = x2.shape
    H = W["a_log"].shape[0]
    nw = W["norm_w"]
    row = lambda v: v.reshape(1, -1).astype(F32)
    G = {}

    def mm(fn, *args, name, **kw):
        comm = plan.comm(name) if plan is not None else None
        if comm is None:
            return fn(*args, name=name, **kw)
        res, outs = fn(*args, name=name, comm=comm, **kw)
        plan.done(name, outs, W)
        return res

    def grad(w, g):
        payload = None
        if isinstance(g, (list, tuple)):
            g, payload = g
        G[w] = g
        if plan is not None:
            plan.grad_ready(w, _by_chip(w, g), None if payload is None else _by_chip(w, payload))

    def ffn_fwd(y, l):
        up, act = mm(mm_nn, y, W["ffn_up"][l], name=f"ffn_up_{l}", out_dtypes=(F32, BF), epilogue=_relu2)
        dn = mm(mm_nn, act, W["ffn_down"][l], name=f"ffn_down_{l}")
        return up, act, dn

    def ffn_bwd(y, up, act, ddn, l):
        grad(f"ffn_down{l}", mm_tn(act, ddn, name=f"ffn_dwdown_{l}", bf16_copy=True))
        dup = mm(mm_nt, ddn, W["ffn_down"][l], name=f"ffn_dup_{l}", out_dtypes=(BF,), epilogue=_drelu2, extras=(up,))
        grad(f"ffn_up{l}", mm_tn(y, dup, name=f"ffn_dwup_{l}", shards=N_CHIPS, bf16_copy=True))
        return mm(mm_nt, dup, W["ffn_up"][l], name=f"ffn_dy_{l}")

    y0 = rms_fwd(x2, row(nw[0, 0]), name="norm00")
    pm = mm(mm_nt, y0, W["la_in_main"], name="la_in_main")
    pg = mm_nt(y0, W["la_in_gate"], name="la_in_gate")
    wc8 = jnp.pad(jnp.transpose(W["conv_w"]), ((0, SUBLANES - CONV_WIDTH), (0, 0)))
    lanes_pad = (H, LANES - 2 * H)
    arow = jnp.pad(W["a_log"], lanes_pad).reshape(1, LANES)
    dtrow = jnp.pad(W["dt_bias"], lanes_pad).reshape(1, LANES)
    bg, gcrow = gates_fwd(pg, arow, dtrow, H, name="gates_fwd")
    q = prep_fwd(pm, 0, wc8, 0, H, True, name="prep_q")
    k = prep_fwd(pm, H, wc8, H, H, True, name="prep_k")
    v = prep_fwd(pm, 2 * H, wc8, 2 * H, H, False, name="prep_v")
    wn = row(W["out_norm_w"])
    gdn_comm = plan.comm("gdn_fwd") if plan is not None else None
    og_a, or_a, sall_a, tall, *carried = gdn_fwd(q, k, v, pm, 3 * H, bg, gcrow, wn, H, name="gdn_fwd", comm=gdn_comm)
    if gdn_comm is not None:
        plan.done("gdn_fwd", carried[0], W)
    cs, sn = _rotary_tables(T)
    dmat, avec, bvec, gam = _retention_tables(H)
    ocat, or_b, sall_b = ret_fwd(pm, 4 * H, 5 * H, 6 * H, 7 * H, cs, sn, dmat, avec, bvec, gam, og_a, H,
                                 name="ret_fwd")
    mix = mm_nn(ocat, W["la_out"], name="la_out")
    h1, y2 = res_norm(x2, mix, row(nw[0, 1]), row(nw[0, 2]), name="resnorm_0a")
    up, act, dn = ffn_fwd(y2, 0)
    h2, y0b = res_norm(h1, dn, row(nw[0, 3]), row(nw[1, 0]), name="resnorm_0b")

    pre = mm(mm_nn, y0b, W["sg_in"], name="sg_in")
    lw, lb = row(W["ln_w"]), row(W["ln_b"])
    vn = ln_fwd(pre, lw, lb, name="sg_ln")
    ws = W["w_s"]
    bs3 = W["b_s"][:, :, None]
    us = sg_fwd(pre, vn, ws, bs3, name="sg_gate")
    mix1 = mm_nn(us, W["sg_out"], name="sg_out")
    h3, y2b = res_norm(h2, mix1, row(nw[1, 1]), row(nw[1, 2]), name="resnorm_1a")
    up1, act1, dn1 = ffn_fwd(y2b, 1)
    dnw = [[None] * 4 for _ in range(2)]
    dh4, ddn1, dnw[1][3], lrow = last_norm_and_loss(h3, dn1, row(nw[1, 3]), tgt, name="last_norm_and_loss")
    loss = lrow[0, 0]

    dy2b = ffn_bwd(y2b, up1, act1, ddn1, 1)
    dh3, dnw[1][2], dmix1, dnw[1][1] = rms_bwd(h3, row(nw[1, 2]), dy2b, dh4, name="dnorm_1a",
                                               inner=(mix1, row(nw[1, 1])))
    grad("sg_out", mm_tn(us, dmix1, name="sg_dwout", bf16_copy=True))
    dus = mm(mm_nt, dmix1, W["sg_out"], name="sg_dus")
    dpre_u, dvn, G["w_s"], dbs3 = sg_bwd(pre, vn, ws, bs3, dus, name="sg_gate_bwd")
    G["b_s"] = dbs3[:, :, 0]
    dpre, dlw, dlb = ln_bwd(pre, lw, dvn, dpre_u, name="sg_ln_bwd")
    G["ln_w"], G["ln_b"] = dlw[0], dlb[0]
    grad("sg_in", mm_tn(y0b, dpre, name="sg_dwin", shards=N_CHIPS, bf16_copy=True))
    dy0b = mm(mm_nt, dpre, W["sg_in"], name="sg_dy")

    dh2, dnw[1][0], ddn, dnw[0][3] = rms_bwd(h2, row(nw[1, 0]), dy0b, dh3, name="dnorm_0b",
                                             inner=(dn, row(nw[0, 3])))
    dy2 = ffn_bwd(y2, up, act, ddn, 0)
    dh1, dnw[0][2], dmix, dnw[0][1] = rms_bwd(h1, row(nw[0, 2]), dy2, dh2, name="dnorm_0a",
                                              inner=(mix, row(nw[0, 1])))
    grad("la_out", mm_tn(ocat, dmix, name="la_dwout", bf16_copy=True))
    docat = mm(mm_nt, dmix, W["la_out"], name="la_docat")
    dq, dk, dv, dz, dbg, dwn = gdn_bwd(q, k, v, pm, 3 * H, bg, gcrow, wn, or_a, sall_a, tall, docat, H,
                                       name="gdn_bwd")
    drq, drk, drv, drg = ret_bwd(pm, 4 * H, 5 * H, 6 * H, 7 * H, cs, sn, dmat, avec, bvec, gam, or_b, sall_b,
                                 docat, H, H, name="ret_bwd")
    dpg, da, ddt = gates_bwd(pg, arow, dtrow, dbg, H, name="gates_bwd")
    dcq, dwq = prep_bwd_act(pm, 0, wc8, 0, H, True, dq, name="prep_dq")
    dck, dwk = prep_bwd_act(pm, H, wc8, H, H, True, dk, name="prep_dk")
    dcv, dwv = prep_bwd_act(pm, 2 * H, wc8, 2 * H, H, False, dv, name="prep_dv")
    dxq = prep_bwd_conv(dcq, wc8, 0, H, name="conv_dq")
    dxk = prep_bwd_conv(dck, wc8, H, H, name="conv_dk")
    dxv = prep_bwd_conv(dcv, wc8, 2 * H, H, name="conv_dv")
    dproj = _la_dproj_by_chip([dxq, dxk, dxv, dz, drq, drk, drv, drg], dpg, H)
    grad("la_in", mm_tn(dproj, y0, name="la_dwin").reshape(N_CHIPS, -1, D))
    dy0 = mm(mm_nn, dproj, W["la_in_rows"], name="la_dy")
    dx, dnw[0][0] = rms_bwd(x2, row(nw[0, 0]), dy0, dh1, name="dnorm00")

    G["norm_w"] = jnp.stack([jnp.concatenate(r, axis=0) for r in dnw], axis=0)
    G["conv_w"] = jnp.transpose(jnp.concatenate([dwq, dwk, dwv], axis=1)[:CONV_WIDTH])
    G["a_log"] = da[0, H:2 * H]
    G["dt_bias"] = ddt[0, H:2 * H]
    G["out_norm_w"] = dwn[0]
    return loss, dx, G


def _as2d(a):
    n = int(np.prod(a.shape))
    if a.shape[-1] < LANES and n % LANES == 0:
        return a.reshape(-1, LANES)
    return a.reshape(-1, a.shape[-1])


def _adamw_any(w, g, m, v, name):
    shp = w.shape
    d, m2, v2 = adamw(_as2d(w), _as2d(g.reshape(shp)), _as2d(m), _as2d(v), name=name)
    return g.reshape(shp), d.reshape(shp), m2.reshape(shp), v2.reshape(shp)


def kernel(x, norm_w, la_w_in, la_conv_w, la_a_log, la_dt_bias, la_out_norm_w, la_w_out, sg_w_in, sg_ln_w, sg_ln_b, sg_w_s, sg_b_s, sg_w_out, ffn_w_up, ffn_w_down, loss_target, m_norm_w, m_la_w_in, m_la_conv_w, m_la_a_log, m_la_dt_bias, m_la_out_norm_w, m_la_w_out, m_sg_w_in, m_sg_ln_w, m_sg_ln_b, m_sg_w_s, m_sg_b_s, m_sg_w_out, m_ffn_w_up, m_ffn_w_down, v_norm_w, v_la_w_in, v_la_conv_w, v_la_a_log, v_la_dt_bias, v_la_out_norm_w, v_la_w_out, v_sg_w_in, v_sg_ln_w, v_sg_ln_b, v_sg_w_s, v_sg_b_s, v_sg_w_out, v_ffn_w_up, v_ffn_w_down):
    weights = dict(norm_w=norm_w, la_w_in=la_w_in, la_conv_w=la_conv_w, la_a_log=la_a_log, la_dt_bias=la_dt_bias,
                   la_out_norm_w=la_out_norm_w, la_w_out=la_w_out, sg_w_in=sg_w_in, sg_ln_w=sg_ln_w,
                   sg_ln_b=sg_ln_b, sg_w_s=sg_w_s, sg_b_s=sg_b_s, sg_w_out=sg_w_out, ffn_w_up=ffn_w_up,
                   ffn_w_down=ffn_w_down)
    mom_m = dict(norm_w=m_norm_w, la_w_in=m_la_w_in, la_conv_w=m_la_conv_w, la_a_log=m_la_a_log,
                 la_dt_bias=m_la_dt_bias, la_out_norm_w=m_la_out_norm_w, la_w_out=m_la_w_out, sg_w_in=m_sg_w_in,
                 sg_ln_w=m_sg_ln_w, sg_ln_b=m_sg_ln_b, sg_w_s=m_sg_w_s, sg_b_s=m_sg_b_s, sg_w_out=m_sg_w_out,
                 ffn_w_up=m_ffn_w_up, ffn_w_down=m_ffn_w_down)
    mom_v = dict(norm_w=v_norm_w, la_w_in=v_la_w_in, la_conv_w=v_la_conv_w, la_a_log=v_la_a_log,
                 la_dt_bias=v_la_dt_bias, la_out_norm_w=v_la_out_norm_w, la_w_out=v_la_w_out, sg_w_in=v_sg_w_in,
                 sg_ln_w=v_sg_ln_w, sg_ln_b=v_sg_ln_b, sg_w_s=v_sg_w_s, sg_b_s=v_sg_b_s, sg_w_out=v_sg_w_out,
                 ffn_w_up=v_ffn_w_up, ffn_w_down=v_ffn_w_down)
    order = list(weights)

    T, D = x.shape[1], x.shape[2]
    H = la_a_log.shape[1]
    HD = H * HEAD_DIM
    xi, yi, ci = _place()
    chip = 2 * xi + yi
    c_idx = jnp.reshape(ci, (1,)).astype(jnp.int32)
    j_idx = jnp.reshape(chip, (1,)).astype(jnp.int32)

    cs, csp = _la_shard_rows(H)
    la_rows = lambda a: jnp.pad(jnp.swapaxes(a, 1, 2), ((0, 0), (0, csp - cs), (0, 0)))
    shards = dict(la_in=la_rows(la_w_in)[0], la_out=la_w_out[0], sg_in=sg_w_in[0], sg_out=sg_w_out[0],
                  ffn_up0=ffn_w_up[0], ffn_up1=ffn_w_up[1], ffn_down0=ffn_w_down[0], ffn_down1=ffn_w_down[1])
    shard_halves = {w: _halves(a.astype(BF)) for w, a in shards.items()}
    small_shapes = [norm_w.shape, la_conv_w[0].shape, sg_ln_w[0].shape, sg_ln_b[0].shape]
    small = _pack_rows([norm_w, la_conv_w[0], sg_ln_w[0], sg_ln_b[0]])
    small = _halves(jnp.pad(small, ((0, (-small.shape[0]) % (2 * SUBLANES)), (0, 0))))
    la_in_g, small_g = [g.reshape(N_CHIPS, -1, g.shape[-1])
                        for g in run_comm(gather_comm([shard_halves["la_in"], small]), name="gather_first")]
    pieces = [_unpack_rows(small_g[kk], small_shapes) for kk in range(N_CHIPS)]
    la_main, la_gate = _la_weights_from_gathered(la_in_g, H)
    W = dict(
        norm_w=jnp.concatenate([p[0] for p in pieces], axis=-1),
        conv_w=jnp.concatenate([p[1] for p in pieces], axis=0),
        ln_w=jnp.concatenate([p[2] for p in pieces], axis=0),
        ln_b=jnp.concatenate([p[3] for p in pieces], axis=0),
        a_log=la_a_log[0], dt_bias=la_dt_bias[0], out_norm_w=la_out_norm_w[0], w_s=sg_w_s[0], b_s=sg_b_s[0],
        la_in_main=la_main, la_in_gate=la_gate, la_in_rows=la_in_g.reshape(-1, D),
        ffn_up=[None, None], ffn_down=[None, None],
    )
    plan = ExchangePlan(shard_halves, c_idx, j_idx)

    loss_local, dx, G = _train_local(x[0], loss_target[0], W, plan)
    loss = lax.psum(loss_local, ("x", "y", "c"))

    big_params = dict(la_w_in="la_in", la_w_out="la_out", sg_w_in="sg_in", sg_w_out="sg_out",
                      ffn_w_up="ffn_up", ffn_w_down="ffn_down")
    from_sib = dict(zip(big_params, pair_share([plan.finished[k] for k in big_params.values()],
                                               name="grads_pair_share")))

    def big_update(nm, key):
        rows = la_rows if nm == "la_w_in" else (lambda a: a)
        r4 = adamw_halves(rows(weights[nm]), plan.finished[key], from_sib[nm], rows(mom_m[nm]), rows(mom_v[nm]),
                          c_idx, name=f"adamw_{nm}")
        return [jnp.swapaxes(t[:, :cs], 1, 2) for t in r4] if nm == "la_w_in" else r4

    big_res = {nm: big_update(nm, key) for nm, key in big_params.items()}

    small_names = ["norm_w", "conv_w", "ln_w", "ln_b", "a_log", "dt_bias", "out_norm_w", "w_s", "b_s"]
    small_full = [G[nm] for nm in small_names]
    summed = _unpack_rows(all_reduce_small(_pack_rows(small_full), name="grads_all_reduce_small"),
                          [g.shape for g in small_full])
    sm = dict(zip(small_names, summed))
    own = lambda full, axis: lax.dynamic_slice_in_dim(full, chip * (full.shape[axis] // N_CHIPS),
                                                      full.shape[axis] // N_CHIPS, axis)
    grads = dict(
        norm_w=own(sm["norm_w"], 2), la_conv_w=own(sm["conv_w"], 0), la_a_log=sm["a_log"],
        la_dt_bias=sm["dt_bias"], la_out_norm_w=sm["out_norm_w"],
        sg_ln_w=own(sm["ln_w"], 0), sg_ln_b=own(sm["ln_b"], 0), sg_w_s=sm["w_s"], sg_b_s=sm["b_s"],
    )

    res = {nm: big_res[nm] if nm in big_res else
           _adamw_any(weights[nm], grads[nm], mom_m[nm], mom_v[nm], f"adamw_{nm}") for nm in order}
    return (loss, dx.reshape(x.shape), *[res[nm][0] for nm in order], *[res[nm][1] for nm in order],
            *[res[nm][2] for nm in order], *[res[nm][3] for nm in order])
```

```python
import math
from typing import Callable, NamedTuple, Optional

import numpy as np
import jax
import jax.numpy as jnp
from jax import lax
from jax.experimental import pallas as pl
from jax.experimental.pallas import tpu as pltpu

F32 = jnp.float32
BF = jnp.bfloat16

V7X_VMEM_BYTES = 64 * 1024 * 1024
VMEM_LIMIT = (V7X_VMEM_BYTES * 3) // 4
LANES = 128
SUBLANES = 8
BF16_ROWS = 16
HEAD_DIM = 128
LA_CHUNK = 64
SG_CHUNK = 128
CONV_WIDTH = 4
ROPE_BASE = 10000.0
EPS = 1e-6
L2_EPS = 1e-6
N_CHIPS = 4
N_DEV = 8

ADAM_LR = 0.001
ADAM_B1 = 0.9
ADAM_B2 = 0.999
ADAM_EPS = 1e-08
ADAM_WD = 0.01
ADAM_STEP = 10

MESH = pl.DeviceIdType.MESH
ANY = pl.BlockSpec(memory_space=pl.ANY)

NN = (((1,), (0,)), ((), ()))
NT = (((1,), (1,)), ((), ()))
TN = (((0,), (0,)), ((), ()))


def _pcall(body, **kw):
    return pl.pallas_call(body, **kw)


def _cp(n_axes):
    return pltpu.CompilerParams(dimension_semantics=("arbitrary",) * n_axes, vmem_limit_bytes=VMEM_LIMIT)


def _tile(n, pref, unit=LANES):
    if n <= pref:
        return n
    t = (pref // unit) * unit
    while t >= unit:
        if n % t == 0:
            return t
        t -= unit
    return n


def _dot(a, b, dims=NN):
    return lax.dot_general(a.astype(BF), b.astype(BF), dims, preferred_element_type=F32)


def _split_bf16(x):
    hi = x.astype(BF)
    return hi, (x - hi.astype(F32)).astype(BF)


def _dot_hi(a, b, dims=NN):
    ah, al = _split_bf16(a)
    bh, bl = _split_bf16(b)
    dot = lambda u, v: lax.dot_general(u, v, dims, preferred_element_type=F32)
    return dot(ah, bh) + (dot(ah, bl) + dot(al, bh))


def _sigmoid(x):
    return 1.0 / (1.0 + jnp.exp(-x))


def _silu(x):
    return x * _sigmoid(x)


def _dsilu(x):
    s = _sigmoid(x)
    return s * (1.0 + x * (1.0 - s))


GELU_C = math.sqrt(2.0 / math.pi)
GELU_A = 0.044715


def _gelu(x):
    return 0.5 * x * (1.0 + jnp.tanh(GELU_C * (x + GELU_A * x * x * x)))


def _gelu_and_grad(x):
    t = jnp.tanh(GELU_C * (x + GELU_A * x * x * x))
    return 0.5 * x * (1.0 + t), 0.5 * (1.0 + t) + 0.5 * x * (1.0 - t * t) * GELU_C * (1.0 + 3.0 * GELU_A * x * x)


class Comm(NamedTuple):
    ins: list
    outs: list
    nsem: int
    start: Callable
    finish: Callable
    middle: Optional[Callable] = None


def _carry_steps(comm, step, n_steps, refs):
    ci, co, send, recv = refs

    @pl.when(step == 0)
    def _():
        comm.start(ci, co, send, recv)

    if comm.middle is not None:
        @pl.when(step == (3 * n_steps) // 4)
        def _():
            comm.middle(ci, co, send, recv)


def _carry_end(comm, step, n_steps, refs):
    ci, co, send, recv = refs

    @pl.when(step == n_steps - 1)
    def _():
        comm.finish(ci, co, send, recv)


def _matmul(a, b, *, dims, grid, a_spec, b_spec, out_shape, out_spec, acc_shape, name,
            epilogue=None, extras=(), extra_specs=(), comm=None):
    nk = grid[2]
    outs = tuple(out_shape) if isinstance(out_shape, (tuple, list)) else (out_shape,)
    out_specs = tuple(out_spec) if isinstance(out_spec, (tuple, list)) else (out_spec,)
    n_ex, n_out = len(extras), len(outs)
    n_ci = len(comm.ins) if comm else 0
    n_co = len(comm.outs) if comm else 0

    def body(*refs):
        a_ref, b_ref = refs[0], refs[1]
        ex = refs[2:2 + n_ex]
        ci = refs[2 + n_ex:2 + n_ex + n_ci]
        o = refs[2 + n_ex + n_ci:2 + n_ex + n_ci + n_out]
        co = refs[2 + n_ex + n_ci + n_out:2 + n_ex + n_ci + n_out + n_co]
        scratch = refs[2 + n_ex + n_ci + n_out + n_co:]
        i, j, k = pl.program_id(0), pl.program_id(1), pl.program_id(2)

        if comm:
            step, n_steps = (i * grid[1] + j) * nk + k, grid[0] * grid[1] * nk
            carried = (ci, co, scratch[-2], scratch[-1])
            _carry_steps(comm, step, n_steps, carried)

        part = lax.dot_general(a_ref[...].astype(BF), b_ref[...].astype(BF), dims, preferred_element_type=F32)

        def finish(val):
            res = epilogue(val, *[e[...] for e in ex]) if epilogue is not None else (val,)
            for r, oref in zip(res, o):
                oref[...] = r.astype(oref.dtype)

        if nk == 1:
            finish(part)
        else:
            acc = scratch[0]

            @pl.when(k == 0)
            def _():
                acc[...] = part

            @pl.when(k > 0)
            def _():
                acc[...] += part

            @pl.when(k == nk - 1)
            def _():
                finish(acc[...])

        if comm:
            _carry_end(comm, step, n_steps, carried)

    scratch_shapes = [pltpu.VMEM(acc_shape, F32)] if nk > 1 else []
    if comm:
        scratch_shapes += [pltpu.SemaphoreType.DMA((comm.nsem,)), pltpu.SemaphoreType.DMA((comm.nsem,))]
    res = _pcall(
        body, name=name, grid=grid,
        in_specs=[a_spec, b_spec, *extra_specs, *[ANY] * n_ci],
        out_specs=[*out_specs, *[ANY] * n_co],
        out_shape=[*outs, *(comm.outs if comm else [])],
        scratch_shapes=scratch_shapes,
        compiler_params=_cp(3),
    )(a, b, *extras, *(comm.ins if comm else []))
    main = res[0] if n_out == 1 else list(res[:n_out])
    return (main, list(res[n_out:])) if comm else main


def mm_nn(a, w, *, name, out_dtypes=(F32,), epilogue=None, extras=(), comm=None, tm=1024, tn=1024, tk=2048):
    M, K = a.shape
    if w.ndim == 3:
        S, _, Ns = w.shape
        N = S * Ns
    else:
        S, Ns = 1, w.shape[1]
        N = Ns
    tm, tn, tk = _tile(M, tm), _tile(Ns, tn), _tile(K, tk)
    npb = Ns // tn
    grid = (M // tm, N // tn, K // tk)
    a_spec = pl.BlockSpec((tm, tk), lambda i, j, k: (i, k))
    if w.ndim == 3:
        b_spec = pl.BlockSpec((None, tk, tn), lambda i, j, k: (j // npb, k, j % npb))
    else:
        b_spec = pl.BlockSpec((tk, tn), lambda i, j, k: (k, j))
    o_spec = pl.BlockSpec((tm, tn), lambda i, j, k: (i, j))
    outs = tuple(jax.ShapeDtypeStruct((M, N), d) for d in out_dtypes)
    res = _matmul(a, w, dims=NN, grid=grid, a_spec=a_spec, b_spec=b_spec,
                  out_shape=outs, out_spec=(o_spec,) * len(outs), acc_shape=(tm, tn), name=name,
                  epilogue=epilogue, extras=extras, extra_specs=(o_spec,) * len(extras), comm=comm)
    return res


def mm_nt(a, w, *, name, out_dtypes=(F32,), epilogue=None, extras=(), comm=None, tm=1024, tn=1024, tk=2048):
    M, Kc = a.shape
    if w.ndim == 3:
        S, Nout, Ks = w.shape
    else:
        S, (Nout, Ks) = 1, w.shape
    assert S * Ks == Kc
    tm, tn, tk = _tile(M, tm), _tile(Nout, tn), _tile(Ks, tk)
    kpb = Ks // tk
    grid = (M // tm, Nout // tn, Kc // tk)
    a_spec = pl.BlockSpec((tm, tk), lambda i, j, k: (i, k))
    if w.ndim == 3:
        b_spec = pl.BlockSpec((None, tn, tk), lambda i, j, k: (k // kpb, j, k % kpb))
    else:
        b_spec = pl.BlockSpec((tn, tk), lambda i, j, k: (j, k))
    o_spec = pl.BlockSpec((tm, tn), lambda i, j, k: (i, j))
    outs = tuple(jax.ShapeDtypeStruct((M, Nout), d) for d in out_dtypes)
    return _matmul(a, w, dims=NT, grid=grid, a_spec=a_spec, b_spec=b_spec,
                   out_shape=outs, out_spec=(o_spec,) * len(outs), acc_shape=(tm, tn), name=name,
                   epilogue=epilogue, extras=extras, extra_specs=(o_spec,) * len(extras), comm=comm)


def mm_tn(x, dy, *, name, shards=1, bf16_copy=False, tm=1024, tn=1024, tk=2048):
    T, Kin = x.shape
    N = dy.shape[1]
    Ns = N // shards
    tm, tn, tk = _tile(Kin, tm), _tile(Ns, tn), _tile(T, tk)
    npb = Ns // tn
    grid = (Kin // tm, N // tn, T // tk)
    a_spec = pl.BlockSpec((tk, tm), lambda i, j, k: (k, i))
    b_spec = pl.BlockSpec((tk, tn), lambda i, j, k: (k, j))
    if shards > 1:
        o_spec = pl.BlockSpec((None, tm, tn), lambda i, j, k: (j // npb, i, j % npb))
        out = jax.ShapeDtypeStruct((shards, Kin, Ns), F32)
    else:
        o_spec = pl.BlockSpec((tm, tn), lambda i, j, k: (i, j))
        out = jax.ShapeDtypeStruct((Kin, N), F32)
    if bf16_copy:
        return _matmul(x, dy, dims=TN, grid=grid, a_spec=a_spec, b_spec=b_spec,
                       out_shape=(out, jax.ShapeDtypeStruct(out.shape, BF)), out_spec=(o_spec, o_spec),
                       acc_shape=(tm, tn), name=name, epilogue=lambda acc: (acc, acc))
    return _matmul(x, dy, dims=TN, grid=grid, a_spec=a_spec, b_spec=b_spec,
                   out_shape=out, out_spec=o_spec, acc_shape=(tm, tn), name=name)


ROW_TILE = 256


def _rows(tr, d):
    return pl.BlockSpec((tr, d), lambda i: (i, 0))


def _fixed(shape):
    nd = len(shape)
    return pl.BlockSpec(shape, lambda *_: (0,) * nd)


def _rms(xv, w):
    r = lax.rsqrt(jnp.mean(xv * xv, axis=-1, keepdims=True) + EPS)
    return xv * r * w


def rms_fwd(x, w, *, name):
    T, D = x.shape
    tr = _tile(T, ROW_TILE, SUBLANES)

    def body(x_ref, w_ref, y_ref):
        y_ref[...] = _rms(x_ref[...], w_ref[...]).astype(y_ref.dtype)

    return _pcall(body, name=name, grid=(T // tr,), in_specs=[_rows(tr, D), _fixed((1, D))],
                  out_specs=_rows(tr, D), out_shape=jax.ShapeDtypeStruct((T, D), BF), compiler_params=_cp(1))(x, w)


def res_norm(h, m, wa, wb, *, name):
    T, D = h.shape
    tr = _tile(T, ROW_TILE, SUBLANES)

    def body(h_ref, m_ref, wa_ref, wb_ref, ho_ref, y_ref):
        ho = h_ref[...] + _rms(m_ref[...], wa_ref[...])
        ho_ref[...] = ho
        y_ref[...] = _rms(ho, wb_ref[...]).astype(y_ref.dtype)

    return _pcall(body, name=name, grid=(T // tr,),
                  in_specs=[_rows(tr, D), _rows(tr, D), _fixed((1, D)), _fixed((1, D))],
                  out_specs=[_rows(tr, D), _rows(tr, D)],
                  out_shape=[jax.ShapeDtypeStruct((T, D), F32), jax.ShapeDtypeStruct((T, D), BF)],
                  compiler_params=_cp(1))(h, m, wa, wb)


def _rms_bwd_rows(xv, w, dyv):
    r = lax.rsqrt(jnp.mean(xv * xv, axis=-1, keepdims=True) + EPS)
    xh = xv * r
    dyw = dyv * w
    return r * (dyw - xh * jnp.mean(dyw * xh, axis=-1, keepdims=True)), jnp.sum(dyv * xh, axis=0, keepdims=True)


def rms_bwd(x, w, dy, dres, *, name, inner=None):
    T, D = x.shape
    tr = _tile(T, ROW_TILE, SUBLANES)
    chained = inner is not None

    def body(*refs):
        if chained:
            x_ref, w_ref, dy_ref, dr_ref, m_ref, wa_ref, dx_ref, dw_ref, dm_ref, dwa_ref = refs
        else:
            x_ref, w_ref, dy_ref, dr_ref, dx_ref, dw_ref = refs
        i = pl.program_id(0)
        dx, part = _rms_bwd_rows(x_ref[...], w_ref[...], dy_ref[...].astype(F32))
        dx = dx + dr_ref[...]
        dx_ref[...] = dx
        if chained:
            dm, part_a = _rms_bwd_rows(m_ref[...], wa_ref[...], dx)
            dm_ref[...] = dm.astype(dm_ref.dtype)

        @pl.when(i == 0)
        def _():
            dw_ref[...] = part
            if chained:
                dwa_ref[...] = part_a

        @pl.when(i > 0)
        def _():
            dw_ref[...] += part
            if chained:
                dwa_ref[...] += part_a

    ins = [x, w, dy, dres] + (list(inner) if chained else [])
    in_specs = [_rows(tr, D), _fixed((1, D)), _rows(tr, D), _rows(tr, D)]
    out_specs = [_rows(tr, D), _fixed((1, D))]
    out_shape = [jax.ShapeDtypeStruct((T, D), F32), jax.ShapeDtypeStruct((1, D), F32)]
    if chained:
        in_specs += [_rows(tr, D), _fixed((1, D))]
        out_specs += [_rows(tr, D), _fixed((1, D))]
        out_shape += [jax.ShapeDtypeStruct((T, D), BF), jax.ShapeDtypeStruct((1, D), F32)]
    return _pcall(body, name=name, grid=(T // tr,), in_specs=in_specs, out_specs=out_specs, out_shape=out_shape,
                  compiler_params=_cp(1))(*ins)


def last_norm_and_loss(h, m, w, tgt, *, name):
    T, D = h.shape
    tr = _tile(T, ROW_TILE, SUBLANES)

    def body(h_ref, m_ref, w_ref, t_ref, dy_ref, dm_ref, dw_ref, l_ref):
        i = pl.program_id(0)
        mv = m_ref[...]
        r = lax.rsqrt(jnp.mean(mv * mv, axis=-1, keepdims=True) + EPS)
        xh = mv * r
        e = h_ref[...] + xh * w_ref[...] - t_ref[...]
        dy = e * (1.0 / D)
        dy_ref[...] = dy
        dyw = dy * w_ref[...]
        dm_ref[...] = (r * (dyw - xh * jnp.mean(dyw * xh, axis=-1, keepdims=True))).astype(dm_ref.dtype)
        pw = jnp.sum(dy * xh, axis=0, keepdims=True)
        pl_ = 0.5 * jnp.sum(jnp.mean(e * e, axis=-1, keepdims=True), axis=0, keepdims=True)
        pl_ = jnp.broadcast_to(pl_, (1, LANES))

        @pl.when(i == 0)
        def _():
            dw_ref[...] = pw
            l_ref[...] = pl_

        @pl.when(i > 0)
        def _():
            dw_ref[...] += pw
            l_ref[...] += pl_

    return _pcall(body, name=name, grid=(T // tr,),
                  in_specs=[_rows(tr, D), _rows(tr, D), _fixed((1, D)), _rows(tr, D)],
                  out_specs=[_rows(tr, D), _rows(tr, D), _fixed((1, D)), _fixed((1, LANES))],
                  out_shape=[jax.ShapeDtypeStruct((T, D), F32), jax.ShapeDtypeStruct((T, D), BF),
                             jax.ShapeDtypeStruct((1, D), F32), jax.ShapeDtypeStruct((1, LANES), F32)],
                  compiler_params=_cp(1))(h, m, w, tgt)


def adamw(w, g, m, v, *, name):
    R, C = w.shape
    tr = _tile(R, max(SUBLANES, (1 << 18) // C), SUBLANES)
    c1 = 1.0 - ADAM_B1 ** ADAM_STEP
    c2 = 1.0 - ADAM_B2 ** ADAM_STEP

    def body(w_ref, g_ref, m_ref, v_ref, d_ref, mo_ref, vo_ref):
        gv = g_ref[...]
        m2 = ADAM_B1 * m_ref[...] + (1.0 - ADAM_B1) * gv
        v2 = ADAM_B2 * v_ref[...] + (1.0 - ADAM_B2) * (gv * gv)
        d_ref[...] = -ADAM_LR * ((m2 / c1) / (jnp.sqrt(v2 / c2) + ADAM_EPS) + ADAM_WD * w_ref[...])
        mo_ref[...] = m2
        vo_ref[...] = v2

    spec = _rows(tr, C)
    sds = jax.ShapeDtypeStruct((R, C), F32)
    return _pcall(body, name=name, grid=(R // tr,), in_specs=[spec] * 4, out_specs=[spec] * 3,
                  out_shape=[sds] * 3, compiler_params=_cp(1))(w, g, m, v)


HALO = SUBLANES


def _conv_down(xx, w_ref):
    acc = xx * w_ref[pl.ds(CONV_WIDTH - 1, 1), :]
    for d in range(1, CONV_WIDTH):
        acc = acc + pltpu.roll(xx, d, 0) * w_ref[pl.ds(CONV_WIDTH - 1 - d, 1), :]
    return acc


def _conv_tile(x_ref, halo_ref, w_ref, first):
    xs = x_ref[...]
    hal = jnp.where(first, 0.0, halo_ref[...])
    cat = jnp.concatenate([hal, xs[0:HALO]], axis=0)
    return jnp.concatenate([_conv_down(cat, w_ref)[HALO:2 * HALO], _conv_down(xs, w_ref)[HALO:]], axis=0)


def _shift_down_tile(x_ref, halo_ref, first, d):
    xs = x_ref[...]
    if d == 0:
        return xs
    hal = jnp.where(first, 0.0, halo_ref[...])
    cat = jnp.concatenate([hal, xs[0:HALO]], axis=0)
    return jnp.concatenate([pltpu.roll(cat, d, 0)[HALO:2 * HALO], pltpu.roll(xs, d, 0)[HALO:]], axis=0)


def _l2n(s):
    return s * lax.rsqrt(jnp.sum(s * s, axis=-1, keepdims=True) + L2_EPS)


PREP_ROWS = 512


def _l2n_groups(s, nb):
    return jnp.concatenate([_l2n(s[:, g * LANES:(g + 1) * LANES]) for g in range(nb)], axis=1)


def prep_fwd(pm, off, wc8, woff, nblk, l2, *, name):
    T = pm.shape[0]
    tr = _tile(T, PREP_ROWS, SUBLANES)
    hb = tr // HALO
    wb = _heads_per_step(nblk)
    wl = wb * LANES

    def body(x_ref, halo_ref, w_ref, o_ref):
        i = pl.program_id(0)
        s = _silu(_conv_tile(x_ref, halo_ref, w_ref, i == 0))
        o_ref[...] = _l2n_groups(s, wb) if l2 else s

    return _pcall(
        body, name=name, grid=(T // tr, nblk // wb),
        in_specs=[pl.BlockSpec((tr, wl), lambda i, c: (i, off // wb + c)),
                  pl.BlockSpec((HALO, wl), lambda i, c: (jnp.maximum(i * hb - 1, 0), off // wb + c)),
                  pl.BlockSpec((SUBLANES, wl), lambda i, c: (0, woff // wb + c))],
        out_specs=pl.BlockSpec((tr, wl), lambda i, c: (i, c)),
        out_shape=jax.ShapeDtypeStruct((T, nblk * LANES), F32), compiler_params=_cp(2))(pm, pm, wc8)


def prep_bwd_act(pm, off, wc8, woff, nblk, l2, dout, *, name):
    T = pm.shape[0]
    tr = _tile(T, PREP_ROWS, SUBLANES)
    hb = tr // HALO
    wb = _heads_per_step(nblk)
    wl = wb * LANES

    def l2_bwd(s, do):
        r = lax.rsqrt(jnp.sum(s * s, axis=-1, keepdims=True) + L2_EPS)
        nrm = s * r
        return r * (do - nrm * jnp.sum(do * nrm, axis=-1, keepdims=True))

    def body(x_ref, halo_ref, w_ref, do_ref, dc_ref, dw_ref):
        i = pl.program_id(1)
        first = i == 0
        y = _conv_tile(x_ref, halo_ref, w_ref, first)
        s = _silu(y)
        do = do_ref[...]
        if l2:
            ds = jnp.concatenate([l2_bwd(s[:, g * LANES:(g + 1) * LANES], do[:, g * LANES:(g + 1) * LANES])
                                  for g in range(wb)], axis=1)
        else:
            ds = do
        dc = ds * _dsilu(y)
        dc_ref[...] = dc

        @pl.when(first)
        def _():
            dw_ref[...] = jnp.zeros_like(dw_ref)

        for j in range(CONV_WIDTH):
            xsh = _shift_down_tile(x_ref, halo_ref, first, CONV_WIDTH - 1 - j)
            dw_ref[pl.ds(j, 1), :] += jnp.sum(dc * xsh, axis=0, keepdims=True)

    return _pcall(
        body, name=name, grid=(nblk // wb, T // tr),
        in_specs=[pl.BlockSpec((tr, wl), lambda c, i: (i, off // wb + c)),
                  pl.BlockSpec((HALO, wl), lambda c, i: (jnp.maximum(i * hb - 1, 0), off // wb + c)),
                  pl.BlockSpec((SUBLANES, wl), lambda c, i: (0, woff // wb + c)),
                  pl.BlockSpec((tr, wl), lambda c, i: (i, c))],
        out_specs=[pl.BlockSpec((tr, wl), lambda c, i: (i, c)),
                   pl.BlockSpec((SUBLANES, wl), lambda c, i: (0, c))],
        out_shape=[jax.ShapeDtypeStruct((T, nblk * LANES), F32),
                   jax.ShapeDtypeStruct((SUBLANES, nblk * LANES), F32)],
        compiler_params=_cp(2))(pm, pm, wc8, dout)


def prep_bwd_conv(dc, wc8, woff, nblk, *, name):
    T = dc.shape[0]
    tr = _tile(T, PREP_ROWS, SUBLANES)
    hb = tr // HALO
    nt = T // tr
    last_halo = T // HALO - 1
    wb = _heads_per_step(nblk)
    wl = wb * LANES

    def up(xx, w_ref):
        rows = xx.shape[0]
        acc = xx * w_ref[pl.ds(CONV_WIDTH - 1, 1), :]
        for d in range(1, CONV_WIDTH):
            acc = acc + pltpu.roll(xx, rows - d, 0) * w_ref[pl.ds(CONV_WIDTH - 1 - d, 1), :]
        return acc

    def body(x_ref, halo_ref, w_ref, o_ref):
        i = pl.program_id(0)
        xs = x_ref[...]
        hal = jnp.where(i == nt - 1, 0.0, halo_ref[...])
        cat = jnp.concatenate([xs[tr - HALO:tr], hal], axis=0)
        out = jnp.concatenate([up(xs, w_ref)[:tr - HALO], up(cat, w_ref)[0:HALO]], axis=0)
        o_ref[...] = out.astype(o_ref.dtype)

    return _pcall(
        body, name=name, grid=(nt, nblk // wb),
        in_specs=[pl.BlockSpec((tr, wl), lambda i, c: (i, c)),
                  pl.BlockSpec((HALO, wl), lambda i, c: (jnp.minimum((i + 1) * hb, last_halo), c)),
                  pl.BlockSpec((SUBLANES, wl), lambda i, c: (0, woff // wb + c))],
        out_specs=pl.BlockSpec((tr, wl), lambda i, c: (i, c)),
        out_shape=jax.ShapeDtypeStruct((T, nblk * LANES), BF), compiler_params=_cp(2))(dc, dc, wc8)


def _softplus(x):
    return jnp.maximum(x, 0.0) + jnp.log(1.0 + jnp.exp(-jnp.abs(x)))


def _tril_ones(c):
    t = lax.broadcasted_iota(jnp.int32, (c, c), 0)
    s = lax.broadcasted_iota(jnp.int32, (c, c), 1)
    return (t >= s).astype(F32)


GATE_CHUNKS_PER_STEP = 8


def _chunks_per_step(n_chunks):
    per = GATE_CHUNKS_PER_STEP
    while n_chunks % per:
        per //= 2
    return per


def _triu_ones(c):
    t = lax.broadcasted_iota(jnp.int32, (c, c), 0)
    s = lax.broadcasted_iota(jnp.int32, (c, c), 1)
    return (t <= s).astype(F32)


def gates_fwd(pg, arow, dtrow, H, *, name):
    T = pg.shape[0]
    C = LA_CHUNK
    N = T // C
    per = _chunks_per_step(N)

    def body(x_ref, a_ref, dt_ref, bg_ref, gr_ref):
        lane = lax.broadcasted_iota(jnp.int32, (C, LANES), 1)
        lm, um = _tril_ones(C), _triu_ones(C)
        for j in range(per):
            rows = slice(j * C, (j + 1) * C)
            x = x_ref[rows, :]
            g = -jnp.exp(a_ref[...]) * _softplus(x + dt_ref[...])
            g = jnp.where((lane >= H) & (lane < 2 * H), g, 0.0)
            bg_ref[rows, :] = jnp.where(lane < H, _sigmoid(x), _dot_hi(lm, g))
            gr_ref[j] = _dot_hi(g, um, TN)

    return _pcall(
        body, name=name, grid=(N // per,),
        in_specs=[pl.BlockSpec((per * C, LANES), lambda n: (n, 0)), _fixed((1, LANES)), _fixed((1, LANES))],
        out_specs=[pl.BlockSpec((per * C, LANES), lambda n: (n, 0)),
                   pl.BlockSpec((per, LANES, C), lambda n: (n, 0, 0))],
        out_shape=[jax.ShapeDtypeStruct((T, LANES), F32), jax.ShapeDtypeStruct((N, LANES, C), F32)],
        compiler_params=_cp(1))(pg, arow, dtrow)


def gates_bwd(pg, arow, dtrow, dbg, H, *, name):
    T = pg.shape[0]
    C = LA_CHUNK
    N = T // C
    per = _chunks_per_step(N)

    def body(x_ref, a_ref, dt_ref, d_ref, dx_ref, da_ref, ddt_ref):
        n = pl.program_id(0)
        lane = lax.broadcasted_iota(jnp.int32, (C, LANES), 1)
        in_g = (lane >= H) & (lane < 2 * H)
        e = jnp.exp(a_ref[...])
        lm = _tril_ones(C)
        pa = jnp.zeros((1, LANES), F32)
        pd = jnp.zeros((1, LANES), F32)
        for j in range(per):
            rows = slice(j * C, (j + 1) * C)
            x = x_ref[rows, :]
            d = d_ref[rows, :]
            xs = x + dt_ref[...]
            g = -e * _softplus(xs)
            dg = _dot_hi(lm, jnp.where(in_g, d, 0.0), TN)
            dxs = jnp.where(in_g, dg * (-e) * _sigmoid(xs), 0.0)
            beta = _sigmoid(x)
            dx_ref[rows, :] = jnp.where(lane < H, d * beta * (1.0 - beta), dxs).astype(dx_ref.dtype)
            pa = pa + jnp.sum(jnp.where(in_g, dg * g, 0.0), axis=0, keepdims=True)
            pd = pd + jnp.sum(dxs, axis=0, keepdims=True)

        @pl.when(n == 0)
        def _():
            da_ref[...] = pa
            ddt_ref[...] = pd

        @pl.when(n > 0)
        def _():
            da_ref[...] += pa
            ddt_ref[...] += pd

    rows = pl.BlockSpec((per * C, LANES), lambda n: (n, 0))
    return _pcall(
        body, name=name, grid=(N // per,),
        in_specs=[rows, _fixed((1, LANES)), _fixed((1, LANES)), rows],
        out_specs=[rows, _fixed((1, LANES)), _fixed((1, LANES))],
        out_shape=[jax.ShapeDtypeStruct((T, LANES), BF), jax.ShapeDtypeStruct((1, LANES), F32),
                   jax.ShapeDtypeStruct((1, LANES), F32)],
        compiler_params=_cp(1))(pg, arow, dtrow, dbg)


QK_SCALE = HEAD_DIM ** -0.5


HEADS_PER_STEP = 8


def _heads_per_step(H):
    hb = HEADS_PER_STEP
    while H % hb:
        hb //= 2
    return hb


def _head_rstd(o):
    return lax.rsqrt(jnp.mean(o * o, axis=-1, keepdims=True) + EPS)


def _gdn_gates(bg_ref, gr_ref, h, H):
    C = LA_CHUNK
    bgv = bg_ref[...]
    lane = lax.broadcasted_iota(jnp.int32, (C, LANES), 1)
    beta = jnp.sum(jnp.where(lane == h, bgv, 0.0), axis=1, keepdims=True)
    gc = jnp.sum(jnp.where(lane == H + h, bgv, 0.0), axis=1, keepdims=True)
    grow = gr_ref[pl.ds(H + h, 1), :]
    ri = lax.broadcasted_iota(jnp.int32, (C, 1), 0)
    gl = jnp.sum(jnp.where(ri == C - 1, gc, 0.0), axis=0, keepdims=True)
    return beta, gc, grow, gl


def _chunk_masks():
    C = LA_CHUNK
    ti = lax.broadcasted_iota(jnp.int32, (C, C), 0)
    si = lax.broadcasted_iota(jnp.int32, (C, C), 1)
    return ti >= si, ti > si, ti == si


def _decay(gc, grow, causal):
    return jnp.where(causal, jnp.exp(jnp.where(causal, gc - grow, 0.0)), 0.0)


def _interleave(gens):
    gens = list(gens)
    results = [None] * len(gens)
    live = list(range(len(gens)))
    while live:
        still = []
        for i in live:
            try:
                next(gens[i])
                still.append(i)
            except StopIteration as stop:
                results[i] = stop.value
        live = still
    return results


def _unit_lower_inverse(a, eye):
    x = -a
    p = jnp.where(eye, 1.0, 0.0) + x
    for _ in range(5):
        x = _dot_hi(x, x)
        yield
        p = p + _dot_hi(p, x)
        yield
    return p


def gdn_fwd(q, k, v, pm, zoff, bg, gcrow, wn, H, *, name, comm=None):
    T = q.shape[0]
    C = LA_CHUNK
    N = T // C
    hd = HEAD_DIM

    HB = _heads_per_step(H)
    n_ci = len(comm.ins) if comm else 0
    n_co = len(comm.outs) if comm else 0

    def body(*refs):
        q_ref, k_ref, v_ref, z_ref, bg_ref, gr_ref, wn_ref = refs[:7]
        ci = refs[7:7 + n_ci]
        og_ref, or_ref, sall_ref, tall_ref = refs[7 + n_ci:11 + n_ci]
        co = refs[11 + n_ci:11 + n_ci + n_co]
        S = refs[11 + n_ci + n_co]
        n = pl.program_id(0)
        hg = pl.program_id(1)
        causal, strict, eye = _chunk_masks()

        @pl.when((n == 0) & (hg == 0))
        def _():
            S[...] = jnp.zeros_like(S)

        if comm:
            step, n_steps = n * (H // HB) + hg, N * (H // HB)
            carried = (ci, co, refs[-2], refs[-1])
            _carry_steps(comm, step, n_steps, carried)

        states = [S[hg * HB + i] for i in range(HB)]

        def head(i):
            h = hg * HB + i
            sl = slice(i * hd, (i + 1) * hd)
            beta, gc, grow, gl = _gdn_gates(bg_ref, gr_ref, h, H)
            dm = _decay(gc, grow, causal)
            qs = q_ref[:, sl] * QK_SCALE
            kk = k_ref[:, sl]
            vv = v_ref[:, sl]
            eg = jnp.exp(gc)
            kb = kk * beta
            a = jnp.where(strict, _dot(kb, kk, NT) * dm, 0.0)
            yield
            tm = yield from _unit_lower_inverse(a, eye)
            u = _dot(tm, vv * beta)
            w = _dot(tm, kb * eg)
            qk = jnp.where(causal, _dot(qs, kk, NT) * dm, 0.0)
            yield
            s0 = states[i]
            vnew = u - _dot(w, s0)
            o = _dot(qs * eg, s0)
            yield
            o = o + _dot(qk, vnew)
            s1 = s0 * jnp.exp(gl) + _dot(kk * jnp.exp(gl - gc), vnew, TN)
            yield
            sall_ref[i] = s0
            tall_ref[i] = tm
            or_ref[:, sl] = o
            og_ref[:, sl] = (o * _head_rstd(o) * wn_ref[...] * _silu(z_ref[:, sl])).astype(og_ref.dtype)
            return s1

        for i, s1 in enumerate(_interleave([head(i) for i in range(HB)])):
            S[hg * HB + i] = s1

        if comm:
            _carry_end(comm, step, n_steps, carried)

    blk = lambda off: pl.BlockSpec((C, HB * hd), lambda n, h: (n, off // HB + h))
    scratch_shapes = [pltpu.VMEM((H, hd, hd), F32)]
    if comm:
        scratch_shapes += [pltpu.SemaphoreType.DMA((comm.nsem,)), pltpu.SemaphoreType.DMA((comm.nsem,))]
    res = _pcall(
        body, name=name, grid=(N, H // HB),
        in_specs=[blk(0), blk(0), blk(0), blk(zoff),
                  pl.BlockSpec((C, LANES), lambda n, h: (n, 0)),
                  pl.BlockSpec((None, LANES, C), lambda n, h: (n, 0, 0)),
                  _fixed((1, hd)), *[ANY] * n_ci],
        out_specs=[blk(0), blk(0),
                   pl.BlockSpec((None, HB, hd, hd), lambda n, h: (n, h, 0, 0)),
                   pl.BlockSpec((None, HB, C, C), lambda n, h: (n, h, 0, 0)), *[ANY] * n_co],
        out_shape=[jax.ShapeDtypeStruct((T, 2 * H * hd), BF),
                   jax.ShapeDtypeStruct((T, H * hd), F32),
                   jax.ShapeDtypeStruct((N, H, hd, hd), F32), jax.ShapeDtypeStruct((N, H, C, C), F32),
                   *(comm.outs if comm else [])],
        scratch_shapes=scratch_shapes,
        compiler_params=_cp(2))(q, k, v, pm, bg, gcrow, wn, *(comm.ins if comm else []))
    return (*res[:4], list(res[4:])) if comm else tuple(res)


def gdn_bwd(q, k, v, pm, zoff, bg, gcrow, wn, oraw, sall, tall, dog, H, *, name):
    T = q.shape[0]
    C = LA_CHUNK
    N = T // C
    hd = HEAD_DIM

    HB = _heads_per_step(H)

    def body(*refs):
        dbg_ref, dwn_ref, dS = refs[15], refs[16], refs[17]
        n = pl.program_id(0)
        hg = pl.program_id(1)

        @pl.when((n == 0) & (hg == 0))
        def _():
            dwn_ref[...] = jnp.zeros_like(dwn_ref)
            dS[...] = jnp.zeros_like(dS)

        @pl.when(hg == 0)
        def _():
            dbg_ref[...] = jnp.zeros_like(dbg_ref)

        ds_in = [dS[hg * HB + i] for i in range(HB)]
        outs = _interleave([head(i, hg * HB + i, ds_in[i], *refs) for i in range(HB)])
        for i in range(HB):
            dS[hg * HB + i] = outs[i][0]
        dwn_ref[...] += sum(o[1] for o in outs)
        dbg_ref[...] += sum(o[2] for o in outs)

    def head(i, h, ds1, q_ref, k_ref, v_ref, z_ref, bg_ref, gr_ref, wn_ref, or_ref, sall_ref, tall_ref, dog_ref,
             dq_ref, dk_ref, dv_ref, dz_ref, dbg_ref, dwn_ref, dS):
        sl = slice(i * hd, (i + 1) * hd)
        beta, gc, grow, gl = _gdn_gates(bg_ref, gr_ref, h, H)
        causal, strict, eye = _chunk_masks()
        dm = _decay(gc, grow, causal)
        qs = q_ref[:, sl] * QK_SCALE
        kk = k_ref[:, sl]
        vv = v_ref[:, sl]
        zz = z_ref[:, sl]
        o = or_ref[:, sl]
        dog = dog_ref[:, sl]
        wn_v = wn_ref[...]
        s0 = sall_ref[i]
        tm = tall_ref[i]

        rstd = _head_rstd(o)
        on = o * rstd
        sz = _silu(zz)
        don = dog * wn_v * sz
        dwn_part = jnp.sum(dog * on * sz, axis=0, keepdims=True)
        dz_ref[:, sl] = (dog * on * wn_v * _dsilu(zz)).astype(dz_ref.dtype)
        do = rstd * (don - on * jnp.mean(don * on, axis=-1, keepdims=True))

        eg = jnp.exp(gc)
        kb = kk * beta
        vb = vv * beta
        kbg = kb * eg
        a = jnp.where(strict, _dot(kb, kk, NT) * dm, 0.0)
        u = _dot(tm, vb)
        w = _dot(tm, kbg)
        qk = jnp.where(causal, _dot(qs, kk, NT) * dm, 0.0)
        dqdec = _dot(do, s0, NT)
        yield
        vnew = u - _dot(w, s0)
        qdec = qs * eg
        etail = jnp.exp(gl - gc)
        ktail = kk * etail
        egl = jnp.exp(gl)
        dvnew = _dot(qk, do, TN) + _dot(ktail, ds1)
        yield
        dqk = jnp.where(causal, _dot(do, vnew, NT), 0.0)
        dktail = _dot(vnew, ds1, NT)
        dcd = jnp.sum(jnp.sum(s0 * ds1, axis=1, keepdims=True), axis=0, keepdims=True)
        ds0 = egl * ds1 + _dot(qdec, do, TN) - _dot(w, dvnew, TN)
        dw = -_dot(dvnew, s0, NT)
        dvb = _dot(tm, dvnew, TN)
        yield
        dkbg = _dot(tm, dw, TN)
        dtm = _dot(dvnew, vb, NT) + _dot(dw, kbg, NT)
        dqkr = dqk * dm
        dqs = _dot(dqkr, kk) + dqdec * eg
        yield
        x = _dot_hi(tm, dtm, TN)
        yield
        da = jnp.where(strict, -_dot_hi(x, tm, NT), 0.0)
        yield
        dkk = da * dm
        dkb = _dot(dkk, kk) + dkbg * eg
        dk = _dot(dkk, kb, TN)
        dk = dk + _dot(dqkr, qs, TN) + dktail * etail + dkb * beta
        g = da * a + dqk * qk
        colsum = jnp.max(_dot_hi(g, jnp.ones((C, LANES), F32), TN), axis=1, keepdims=True)
        yield
        rk = jnp.sum(dktail * ktail, axis=1, keepdims=True)
        dgc = (jnp.sum(g, axis=1, keepdims=True) - colsum
               + jnp.sum(dqdec * qdec, axis=1, keepdims=True) - rk
               + jnp.sum(dkbg * kbg, axis=1, keepdims=True))
        dgl = jnp.sum(rk, axis=0, keepdims=True) + dcd * egl
        ri = lax.broadcasted_iota(jnp.int32, (C, 1), 0)
        dgc = dgc + jnp.where(ri == C - 1, dgl, 0.0)
        dbeta = jnp.sum(dkb * kk, axis=1, keepdims=True) + jnp.sum(dvb * vv, axis=1, keepdims=True)

        dq_ref[:, sl] = dqs * QK_SCALE
        dk_ref[:, sl] = dk
        dv_ref[:, sl] = dvb * beta
        lane = lax.broadcasted_iota(jnp.int32, (C, LANES), 1)
        return ds0, dwn_part, jnp.where(lane == h, dbeta, 0.0) + jnp.where(lane == H + h, dgc, 0.0)

    blk = lambda off: pl.BlockSpec((C, HB * hd), lambda n, h: (N - 1 - n, off // HB + h))
    st = lambda r: pl.BlockSpec((None, HB, r, r), lambda n, h: (N - 1 - n, h, 0, 0))
    return _pcall(
        body, name=name, grid=(N, H // HB),
        in_specs=[blk(0), blk(0), blk(0), blk(zoff),
                  pl.BlockSpec((C, LANES), lambda n, h: (N - 1 - n, 0)),
                  pl.BlockSpec((None, LANES, C), lambda n, h: (N - 1 - n, 0, 0)),
                  _fixed((1, hd)), blk(0), st(hd), st(C), blk(0)],
        out_specs=[blk(0), blk(0), blk(0), blk(0),
                   pl.BlockSpec((C, LANES), lambda n, h: (N - 1 - n, 0)), _fixed((1, hd))],
        out_shape=[jax.ShapeDtypeStruct((T, H * hd), F32)] * 3
        + [jax.ShapeDtypeStruct((T, H * hd), BF), jax.ShapeDtypeStruct((T, LANES), F32),
           jax.ShapeDtypeStruct((1, hd), F32)],
        scratch_shapes=[pltpu.VMEM((H, hd, hd), F32)],
        compiler_params=_cp(2))(q, k, v, pm, bg, gcrow, wn, oraw, sall, tall, dog)


def _rot(x, cs, sn):
    return x * cs + pltpu.roll(x, HEAD_DIM // 2, 1) * sn


def _rot_t(dy, cs, sn):
    return dy * cs + pltpu.roll(dy * sn, HEAD_DIM // 2, 1)


def ret_fwd(pm, qoff, koff, voff, goff, cs, sn, dmat, avec, bvec, gam, og_buf, H, *, name):
    T = pm.shape[0]
    C = LA_CHUNK
    N = T // C
    hd = HEAD_DIM

    HB = _heads_per_step(H)

    def body(q_ref, k_ref, v_ref, g_ref, cs_ref, sn_ref, dm_ref, a_ref, b_ref, gam_ref, _og_in,
             og_ref, or_ref, sall_ref, S):
        n = pl.program_id(0)
        hg = pl.program_id(1)
        c, s = cs_ref[...], sn_ref[...]

        @pl.when((n == 0) & (hg == 0))
        def _():
            S[...] = jnp.zeros_like(S)

        states = [S[hg * HB + i] for i in range(HB)]

        def head(i):
            sl = slice(i * hd, (i + 1) * hd)
            qq = _rot(q_ref[:, sl], c, s)
            kk = _rot(k_ref[:, sl], c, s) * QK_SCALE
            vv = v_ref[:, sl]
            s0 = states[i]
            p = _dot(qq, kk, NT) * dm_ref[i]
            cross = _dot(qq * a_ref[i], s0)
            s1 = s0 * gam_ref[i] + _dot(kk * b_ref[i], vv, TN)
            yield
            o = _dot(p, vv) + cross
            yield
            sall_ref[i] = s0
            or_ref[:, sl] = o
            og_ref[:, sl] = (_silu(g_ref[:, sl]) * o * _head_rstd(o)).astype(og_ref.dtype)
            return s1

        for i, s1 in enumerate(_interleave([head(i) for i in range(HB)])):
            S[hg * HB + i] = s1

    blk = lambda off: pl.BlockSpec((C, HB * hd), lambda n, h: (n, off // HB + h))
    tab = pl.BlockSpec((C, hd), lambda n, h: (n, 0))
    per_h = lambda r, cdim: pl.BlockSpec((HB, r, cdim), lambda n, h: (h, 0, 0))
    return _pcall(
        body, name=name, grid=(N, H // HB),
        in_specs=[blk(qoff), blk(koff), blk(voff), blk(goff), tab, tab,
                  per_h(C, C), per_h(C, hd), per_h(C, hd), per_h(1, hd), ANY],
        out_specs=[blk(H), blk(0), pl.BlockSpec((None, HB, hd, hd), lambda n, h: (n, h, 0, 0))],
        out_shape=[jax.ShapeDtypeStruct(og_buf.shape, og_buf.dtype), jax.ShapeDtypeStruct((T, H * hd), F32),
                   jax.ShapeDtypeStruct((N, H, hd, hd), F32)],
        input_output_aliases={10: 0},
        scratch_shapes=[pltpu.VMEM((H, hd, hd), F32)],
        compiler_params=_cp(2))(pm, pm, pm, pm, cs, sn, dmat, avec, bvec, gam, og_buf)


def ret_bwd(pm, qoff, koff, voff, goff, cs, sn, dmat, avec, bvec, gam, oraw, sall, dog, dogoff, H, *, name):
    T = pm.shape[0]
    C = LA_CHUNK
    N = T // C
    hd = HEAD_DIM

    HB = _heads_per_step(H)

    def body(q_ref, k_ref, v_ref, g_ref, cs_ref, sn_ref, dm_ref, a_ref, b_ref, gam_ref, or_ref, sall_ref,
             dog_ref, dq_ref, dk_ref, dv_ref, dg_ref, dS):
        n = pl.program_id(0)
        hg = pl.program_id(1)
        c, s = cs_ref[...], sn_ref[...]

        @pl.when((n == 0) & (hg == 0))
        def _():
            dS[...] = jnp.zeros_like(dS)

        dstates = [dS[hg * HB + i] for i in range(HB)]

        def head(i):
            sl = slice(i * hd, (i + 1) * hd)
            qq = _rot(q_ref[:, sl], c, s)
            kk = _rot(k_ref[:, sl], c, s) * QK_SCALE
            vv = v_ref[:, sl]
            gg = g_ref[:, sl]
            o = or_ref[:, sl]
            dog = dog_ref[:, sl]
            dm = dm_ref[i]
            av, bv = a_ref[i], b_ref[i]
            s0 = sall_ref[i]
            ds1 = dstates[i]

            rstd = _head_rstd(o)
            on = o * rstd
            don = dog * _silu(gg)
            dg_ref[:, sl] = (dog * on * _dsilu(gg)).astype(dg_ref.dtype)
            do = rstd * (don - on * jnp.mean(don * on, axis=-1, keepdims=True))

            p = _dot(qq, kk, NT) * dm
            dp = _dot(do, vv, NT) * dm
            cross_q = _dot(do, s0, NT) * av
            cross_k = _dot(vv, ds1, NT) * bv
            cross_v = _dot(kk * bv, ds1)
            ds0 = ds1 * gam_ref[i] + _dot(qq * av, do, TN)
            yield
            dv_ref[:, sl] = (_dot(p, do, TN) + cross_v).astype(dv_ref.dtype)
            dqq = _dot(dp, kk) + cross_q
            dkk = (_dot(dp, qq, TN) + cross_k) * QK_SCALE
            yield
            dq_ref[:, sl] = _rot_t(dqq, c, s).astype(dq_ref.dtype)
            dk_ref[:, sl] = _rot_t(dkk, c, s).astype(dk_ref.dtype)
            return ds0

        for i, ds0 in enumerate(_interleave([head(i) for i in range(HB)])):
            dS[hg * HB + i] = ds0

    blk = lambda off: pl.BlockSpec((C, HB * hd), lambda n, h: (N - 1 - n, off // HB + h))
    tab = pl.BlockSpec((C, hd), lambda n, h: (N - 1 - n, 0))
    per_h = lambda r, cdim: pl.BlockSpec((HB, r, cdim), lambda n, h: (h, 0, 0))
    return _pcall(
        body, name=name, grid=(N, H // HB),
        in_specs=[blk(qoff), blk(koff), blk(voff), blk(goff), tab, tab,
                  per_h(C, C), per_h(C, hd), per_h(C, hd), per_h(1, hd), blk(0),
                  pl.BlockSpec((None, HB, hd, hd), lambda n, h: (N - 1 - n, h, 0, 0)), blk(dogoff)],
        out_specs=[blk(0)] * 4,
        out_shape=[jax.ShapeDtypeStruct((T, H * hd), BF)] * 4,
        scratch_shapes=[pltpu.VMEM((H, hd, hd), F32)],
        compiler_params=_cp(2))(pm, pm, pm, pm, cs, sn, dmat, avec, bvec, gam, oraw, sall, dog)


LN_ROWS = 128


def ln_fwd(pre, lw, lb, *, name):
    T, W2 = pre.shape
    W = W2 // 2
    tr = _tile(T, LN_ROWS, SUBLANES)

    def body(p_ref, w_ref, b_ref, o_ref):
        v = _gelu(p_ref[...])
        xc = v - jnp.mean(v, axis=-1, keepdims=True)
        r = lax.rsqrt(jnp.mean(xc * xc, axis=-1, keepdims=True) + EPS)
        o_ref[...] = xc * r * w_ref[...] + b_ref[...]

    return _pcall(body, name=name, grid=(T // tr,),
                  in_specs=[pl.BlockSpec((tr, W), lambda i: (i, 1)), _fixed((1, W)), _fixed((1, W))],
                  out_specs=_rows(tr, W), out_shape=jax.ShapeDtypeStruct((T, W), F32),
                  compiler_params=_cp(1))(pre, lw, lb)


def ln_bwd(pre, lw, dvn, dpre_buf, *, name):
    T, W2 = pre.shape
    W = W2 // 2
    tr = _tile(T, LN_ROWS, SUBLANES)

    def body(p_ref, w_ref, d_ref, _dp_in, dp_ref, dw_ref, db_ref):
        i = pl.program_id(0)
        v, dgelu = _gelu_and_grad(p_ref[...])
        xc = v - jnp.mean(v, axis=-1, keepdims=True)
        r = lax.rsqrt(jnp.mean(xc * xc, axis=-1, keepdims=True) + EPS)
        xh = xc * r
        d = d_ref[...]
        dxh = d * w_ref[...]
        dv = r * (dxh - jnp.mean(dxh, axis=-1, keepdims=True) - xh * jnp.mean(dxh * xh, axis=-1, keepdims=True))
        dp_ref[...] = (dv * dgelu).astype(dp_ref.dtype)
        pw = jnp.sum(d * xh, axis=0, keepdims=True)
        pb = jnp.sum(d, axis=0, keepdims=True)

        @pl.when(i == 0)
        def _():
            dw_ref[...] = pw
            db_ref[...] = pb

        @pl.when(i > 0)
        def _():
            dw_ref[...] += pw
            db_ref[...] += pb

    return _pcall(body, name=name, grid=(T // tr,),
                  in_specs=[pl.BlockSpec((tr, W), lambda i: (i, 1)), _fixed((1, W)), _rows(tr, W), ANY],
                  out_specs=[pl.BlockSpec((tr, W), lambda i: (i, 1)), _fixed((1, W)), _fixed((1, W))],
                  out_shape=[jax.ShapeDtypeStruct(dpre_buf.shape, dpre_buf.dtype), jax.ShapeDtypeStruct((1, W), F32),
                             jax.ShapeDtypeStruct((1, W), F32)],
                  input_output_aliases={3: 0},
                  compiler_params=_cp(1))(pre, lw, dvn, dpre_buf)


def _tril_mask(c):
    t = lax.broadcasted_iota(jnp.int32, (c, c), 0)
    s = lax.broadcasted_iota(jnp.int32, (c, c), 1)
    return t >= s


def sg_fwd(pre, vn, ws, bs3, *, name):
    T, W = vn.shape
    G = ws.shape[0]
    gd = W // G
    C = SG_CHUNK

    def body(p_ref, v_ref, w_ref, b_ref, o_ref):
        mask = _tril_mask(C)
        for g in range(G):
            sl = slice(g * gd, (g + 1) * gd)
            wm = jnp.where(mask, w_ref[g], 0.0)
            s = _dot(wm, v_ref[:, sl]) + b_ref[g]
            o_ref[:, sl] = (_gelu(p_ref[:, sl]) * s).astype(o_ref.dtype)

    blk = pl.BlockSpec((C, W), lambda n: (n, 0))
    return _pcall(body, name=name, grid=(T // C,),
                  in_specs=[blk, blk, _fixed((G, C, C)), _fixed((G, C, 1))],
                  out_specs=blk, out_shape=jax.ShapeDtypeStruct((T, W), BF),
                  compiler_params=_cp(1))(pre, vn, ws, bs3)


def sg_bwd(pre, vn, ws, bs3, dus, *, name):
    T, W = vn.shape
    G = ws.shape[0]
    gd = W // G
    C = SG_CHUNK

    def body(p_ref, v_ref, w_ref, b_ref, d_ref, dp_ref, dv_ref, dw_ref, db_ref):
        n = pl.program_id(0)
        mask = _tril_mask(C)

        @pl.when(n == 0)
        def _():
            dw_ref[...] = jnp.zeros_like(dw_ref)
            db_ref[...] = jnp.zeros_like(db_ref)

        for g in range(G):
            sl = slice(g * gd, (g + 1) * gd)
            wm = jnp.where(mask, w_ref[g], 0.0)
            u, du = _gelu_and_grad(p_ref[:, sl])
            vv = v_ref[:, sl]
            d = d_ref[:, sl]
            s = _dot(wm, vv) + b_ref[g]
            ds = d * u
            dp_ref[:, sl] = (d * s * du).astype(dp_ref.dtype)
            dv_ref[:, sl] = _dot(wm, ds, TN)
            dw_ref[g] += jnp.where(mask, _dot(ds, vv, NT), 0.0)
            db_ref[g] += jnp.sum(ds, axis=1, keepdims=True)

    blk = pl.BlockSpec((C, W), lambda n: (n, 0))
    return _pcall(body, name=name, grid=(T // C,),
                  in_specs=[blk, blk, _fixed((G, C, C)), _fixed((G, C, 1)), blk],
                  out_specs=[blk, blk, _fixed((G, C, C)), _fixed((G, C, 1))],
                  out_shape=[jax.ShapeDtypeStruct((T, 2 * W), BF),
                             jax.ShapeDtypeStruct((T, W), F32),
                             jax.ShapeDtypeStruct((G, C, C), F32), jax.ShapeDtypeStruct((G, C, 1), F32)],
                  compiler_params=_cp(1))(pre, vn, ws, bs3, dus)


CHIP_RELATIONS = ((1, 0), (0, 1), (1, 1))


def _place():
    return lax.axis_index("x"), lax.axis_index("y"), lax.axis_index("c")


def _peer_chip(x, y, r):
    fx, fy = CHIP_RELATIONS[r]
    return (1 - x if fx else x), (1 - y if fy else y)


def gather_comm(arrs):
    n = len(arrs)
    per = 2 * len(CHIP_RELATIONS) + 1
    own = per - 1

    def ici(a, r, ins, outs, send, recv):
        x, y, c = _place()
        px, py = _peer_chip(x, y, r)
        return pltpu.make_async_remote_copy(
            src_ref=ins[a].at[c], dst_ref=outs[a].at[2 * x + y, c], send_sem=send.at[a * per + r],
            recv_sem=recv.at[a * per + r], device_id=(px, py, c), device_id_type=MESH)

    def own_block(a, ins, outs, send, recv):
        x, y, c = _place()
        return pltpu.make_async_remote_copy(
            src_ref=ins[a], dst_ref=outs[a].at[2 * x + y], send_sem=send.at[a * per + own],
            recv_sem=recv.at[a * per + own], device_id=(x, y, 1 - c), device_id_type=MESH)

    def start(ins, outs, send, recv):
        for a in range(n):
            for r in range(3):
                ici(a, r, ins, outs, send, recv).start()
            own_block(a, ins, outs, send, recv).start()

    def forward(a, r, outs, send, recv):
        x, y, c = _place()
        px, py = _peer_chip(x, y, r)
        landed = outs[a].at[2 * px + py, c]
        return pltpu.make_async_remote_copy(
            src_ref=landed, dst_ref=landed, send_sem=send.at[a * per + 3 + r],
            recv_sem=recv.at[a * per + 3 + r], device_id=(x, y, 1 - c), device_id_type=MESH)

    def middle(ins, outs, send, recv):
        x, y, c = _place()
        for a in range(n):
            for r in range(3):
                px, py = _peer_chip(x, y, r)
                landed = outs[a].at[2 * px + py, c]
                pltpu.make_async_remote_copy(
                    src_ref=landed, dst_ref=landed, send_sem=send.at[a * per + r],
                    recv_sem=recv.at[a * per + r], device_id=(px, py, c), device_id_type=MESH).wait_recv()
                forward(a, r, outs, send, recv).start()

    def finish(ins, outs, send, recv):
        x, y, c = _place()
        for a in range(n):
            for r in range(3):
                px, py = _peer_chip(x, y, r)
                other = outs[a].at[2 * px + py, 1 - c]
                pltpu.make_async_remote_copy(
                    src_ref=other, dst_ref=other, send_sem=send.at[a * per + 3 + r],
                    recv_sem=recv.at[a * per + 3 + r], device_id=(x, y, 1 - c), device_id_type=MESH).wait_recv()
        for a in range(n):
            for r in range(3):
                ici(a, r, ins, outs, send, recv).wait_send()
                forward(a, r, outs, send, recv).wait_send()
            own_block(a, ins, outs, send, recv).wait()

    outs = [jax.ShapeDtypeStruct((N_CHIPS,) + a.shape, a.dtype) for a in arrs]
    return Comm(list(arrs), outs, n * per, start, finish, middle)


def chip_exchange_comm(ps):
    n = len(ps)

    def copies(ins, outs, send, recv):
        x, y, c = _place()
        cps = []
        for a in range(n):
            for r in range(3):
                px, py = _peer_chip(x, y, r)
                cps.append(pltpu.make_async_remote_copy(
                    src_ref=ins[a].at[2 * px + py], dst_ref=outs[a].at[r], send_sem=send.at[3 * a + r],
                    recv_sem=recv.at[3 * a + r], device_id=(px, py, c), device_id_type=MESH))
        return cps

    def start(ins, outs, send, recv):
        for cp in copies(ins, outs, send, recv):
            cp.start()

    def finish(ins, outs, send, recv):
        for cp in copies(ins, outs, send, recv):
            cp.wait()

    outs = [jax.ShapeDtypeStruct((3,) + p.shape[1:], p.dtype) for p in ps]
    return Comm(list(ps), outs, 3 * n, start, finish)


def run_comm(comm, *, name):
    n_i, n_o = len(comm.ins), len(comm.outs)

    def body(*refs):
        ins, outs = refs[:n_i], refs[n_i:n_i + n_o]
        send, recv = refs[n_i + n_o:]
        comm.start(ins, outs, send, recv)
        if comm.middle is not None:
            comm.middle(ins, outs, send, recv)
        comm.finish(ins, outs, send, recv)

    res = _pcall(body, name=name, in_specs=[ANY] * n_i, out_specs=[ANY] * n_o, out_shape=comm.outs,
                 scratch_shapes=[pltpu.SemaphoreType.DMA((comm.nsem,)), pltpu.SemaphoreType.DMA((comm.nsem,))])(*comm.ins)
    return list(res)


def pair_exchange(gs, *, name):
    n = len(gs)

    def body(*refs):
        ins, outs = refs[:n], refs[n:2 * n]
        send, recv = refs[2 * n:2 * n + 2]
        x, y, c = _place()
        cps = []
        for a in range(n):
            cp = pltpu.make_async_remote_copy(
                src_ref=ins[a].at[:, pl.ds(1 - c, 1)], dst_ref=outs[a], send_sem=send.at[a], recv_sem=recv.at[a],
                device_id=(x, y, 1 - c), device_id_type=MESH)
            cp.start()
            cps.append(cp)
        for cp in cps:
            cp.wait()

    out_shape = [jax.ShapeDtypeStruct((g.shape[0], 1) + g.shape[2:], g.dtype) for g in gs]
    res = _pcall(body, name=name, in_specs=[ANY] * n, out_specs=[ANY] * n, out_shape=out_shape,
                 scratch_shapes=[pltpu.SemaphoreType.DMA((n,)), pltpu.SemaphoreType.DMA((n,))])(*gs)
    return list(res)


def pair_share(fs, *, name):
    n = len(fs)

    def body(*refs):
        ins, outs = refs[:n], refs[n:2 * n]
        send, recv = refs[2 * n:2 * n + 2]
        x, y, c = _place()
        cps = []
        for a in range(n):
            cp = pltpu.make_async_remote_copy(
                src_ref=ins[a], dst_ref=outs[a], send_sem=send.at[a], recv_sem=recv.at[a],
                device_id=(x, y, 1 - c), device_id_type=MESH)
            cp.start()
            cps.append(cp)
        for cp in cps:
            cp.wait()

    out_shape = [jax.ShapeDtypeStruct(f.shape, f.dtype) for f in fs]
    res = _pcall(body, name=name, in_specs=[ANY] * n, out_specs=[ANY] * n, out_shape=out_shape,
                 scratch_shapes=[pltpu.SemaphoreType.DMA((n,)), pltpu.SemaphoreType.DMA((n,))])(*fs)
    return list(res)


def all_reduce_small(v, *, name):
    R = v.shape[0]

    def body(v_ref, sum_ref, gat_ref, send, recv):
        x, y, c = _place()
        me = 4 * x + 2 * y + c
        gat_ref[me] = v_ref[...]
        cps = []
        peers = []
        for r in range(1, N_DEV):
            fx, fy, fc = (r >> 2) & 1, (r >> 1) & 1, r & 1
            px, py, pc = (1 - x if fx else x), (1 - y if fy else y), (1 - c if fc else c)
            peers.append((px, py, pc))
            cp = pltpu.make_async_remote_copy(
                src_ref=v_ref, dst_ref=gat_ref.at[me], send_sem=send.at[r - 1], recv_sem=recv.at[r - 1],
                device_id=(px, py, pc), device_id_type=MESH)
            cp.start()
            cps.append(cp)
        for r in range(1, N_DEV):
            px, py, pc = peers[r - 1]
            slot = gat_ref.at[4 * px + 2 * py + pc]
            pltpu.make_async_remote_copy(
                src_ref=v_ref, dst_ref=slot, send_sem=send.at[r - 1], recv_sem=recv.at[r - 1],
                device_id=(px, py, pc), device_id_type=MESH).wait_recv()
        for cp in cps:
            cp.wait_send()
        acc = gat_ref[0]
        for s in range(1, N_DEV):
            acc = acc + gat_ref[s]
        sum_ref[...] = acc

    vm = pl.BlockSpec(memory_space=pltpu.VMEM)
    res = _pcall(body, name=name, in_specs=[vm], out_specs=[vm, vm],
                 out_shape=[jax.ShapeDtypeStruct((R, LANES), F32), jax.ShapeDtypeStruct((N_DEV, R, LANES), F32)],
                 scratch_shapes=[pltpu.SemaphoreType.DMA((N_DEV - 1,)), pltpu.SemaphoreType.DMA((N_DEV - 1,))],
                 compiler_params=pltpu.CompilerParams(vmem_limit_bytes=VMEM_LIMIT))(v)
    return res[0]


def pair_sum(g, r1, c_idx, *, name):
    nb, _, hr, C = g.shape
    tr = _tile(hr, max(BF16_ROWS, (1 << 18) // C), BF16_ROWS)

    def body(c_ref, g_ref, r_ref, pb_ref):
        pb_ref[...] = (g_ref[...] + r_ref[...].astype(F32)).astype(pb_ref.dtype)

    gs = pltpu.PrefetchScalarGridSpec(
        num_scalar_prefetch=1, grid=(nb, hr // tr),
        in_specs=[pl.BlockSpec((None, None, tr, C), lambda b, i, cr: (b, cr[0], i, 0)),
                  pl.BlockSpec((None, None, tr, C), lambda b, i, cr: (b, 0, i, 0))],
        out_specs=pl.BlockSpec((None, tr, C), lambda b, i, cr: (b, i, 0)))
    return _pcall(body, name=name, grid_spec=gs, out_shape=jax.ShapeDtypeStruct((nb, hr, C), BF),
                  compiler_params=_cp(2))(c_idx, g, r1)


def chip_sum(g, r1, r2, c_idx, j_idx, *, name, layer=0, n_layers=1, into=None):
    _, _, hr, C = g.shape
    tr = _tile(hr, max(BF16_ROWS, (1 << 18) // C), BF16_ROWS)

    def body(c_ref, j_ref, g_ref, s_ref, a_ref, b_ref, d_ref, *rest):
        o_ref = rest[-1]
        own = g_ref[...] + s_ref[...].astype(F32)
        o_ref[...] = ((own + a_ref[...].astype(F32)) + b_ref[...].astype(F32)) + d_ref[...].astype(F32)

    rel = lambda r: pl.BlockSpec((None, tr, C), lambda i, cr, jr: (r, i, 0))
    in_specs = [pl.BlockSpec((None, None, tr, C), lambda i, cr, jr: (jr[0], cr[0], i, 0)),
                pl.BlockSpec((None, None, tr, C), lambda i, cr, jr: (jr[0], 0, i, 0)), rel(0), rel(1), rel(2)]
    operands = [c_idx, j_idx, g, r1, r2, r2, r2]
    aliases = {}
    if into is not None:
        in_specs.append(ANY)
        operands.append(into)
        aliases = {len(operands) - 1: 0}
    gs = pltpu.PrefetchScalarGridSpec(
        num_scalar_prefetch=2, grid=(hr // tr,), in_specs=in_specs,
        out_specs=pl.BlockSpec((None, tr, C), lambda i, cr, jr: (layer, i, 0)))
    return _pcall(body, name=name, grid_spec=gs, out_shape=jax.ShapeDtypeStruct((n_layers, hr, C), F32),
                  input_output_aliases=aliases, compiler_params=_cp(1))(*operands)


def adamw_halves(w, g_mine, g_other, m, v, c_idx, *, name):
    L, R, C = w.shape
    hr = R // 2
    tr = _tile(hr, max(SUBLANES, (1 << 18) // C), SUBLANES)
    nbh = hr // tr
    c1 = 1.0 - ADAM_B1 ** ADAM_STEP
    c2 = 1.0 - ADAM_B2 ** ADAM_STEP

    def body(c_ref, w_ref, gm_ref, go_ref, m_ref, v_ref, g_ref, d_ref, mo_ref, vo_ref):
        i = pl.program_id(1)
        gv = jnp.where(i // nbh == c_ref[0], gm_ref[...], go_ref[...])
        m2 = ADAM_B1 * m_ref[...] + (1.0 - ADAM_B1) * gv
        v2 = ADAM_B2 * v_ref[...] + (1.0 - ADAM_B2) * (gv * gv)
        d_ref[...] = -ADAM_LR * ((m2 / c1) / (jnp.sqrt(v2 / c2) + ADAM_EPS) + ADAM_WD * w_ref[...])
        g_ref[...] = gv
        mo_ref[...] = m2
        vo_ref[...] = v2

    full = pl.BlockSpec((None, tr, C), lambda l, i, cr: (l, i, 0))
    half = pl.BlockSpec((None, tr, C), lambda l, i, cr: (l, i % nbh, 0))
    gs = pltpu.PrefetchScalarGridSpec(
        num_scalar_prefetch=1, grid=(L, R // tr),
        in_specs=[full, half, half, full, full], out_specs=[full] * 4)
    sds = jax.ShapeDtypeStruct((L, R, C), F32)
    return _pcall(body, name=name, grid_spec=gs, out_shape=[sds] * 4,
                  compiler_params=_cp(2))(c_idx, w, g_mine, g_other, m, v)


def _pack_rows(arrs):
    parts = []
    for a in arrs:
        flat = a.reshape(-1).astype(F32)
        tile = SUBLANES * LANES
        pad = (-flat.shape[0]) % tile
        parts.append(jnp.pad(flat, (0, pad)).reshape(-1, LANES))
    return jnp.concatenate(parts, axis=0)


def _unpack_rows(buf, shapes):
    out, row = [], 0
    for shp in shapes:
        size = int(np.prod(shp))
        rows = -(-size // (SUBLANES * LANES)) * SUBLANES
        out.append(buf[row:row + rows].reshape(-1)[:size].reshape(shp))
        row += rows
    return out


def _halves(a2d):
    r, c = a2d.shape
    return a2d.reshape(2, r // 2, c)


def _rotary_tables(T):
    half = HEAD_DIM // 2
    pos = jnp.arange(T, dtype=F32)
    inv_freq = 1.0 / (ROPE_BASE ** jnp.linspace(0.0, 1.0, half, dtype=F32))
    ang = pos[:, None] * inv_freq[None, :]
    cos, sin = jnp.cos(ang), jnp.sin(ang)
    return jnp.concatenate([cos, cos], axis=1), jnp.concatenate([-sin, sin], axis=1)


def _retention_tables(H):
    C = LA_CHUNK
    lg = jnp.log1p(-jnp.power(2.0, -5.0 - jnp.arange(H, dtype=F32)))
    pos = jnp.arange(C, dtype=F32)
    causal = jnp.tril(jnp.ones((C, C), dtype=bool))
    dmat = jnp.exp(jnp.where(causal, (pos[:, None] - pos[None, :]) * lg[:, None, None], -jnp.inf))
    bc = lambda t: jnp.broadcast_to(t[..., None], t.shape + (HEAD_DIM,))
    avec = bc(jnp.exp((pos + 1.0)[None, :] * lg[:, None]))
    bvec = bc(jnp.exp((C - 1.0 - pos)[None, :] * lg[:, None]))
    gam = bc(jnp.exp(C * lg)[:, None])
    return dmat, avec, bvec, gam


def _relu2(acc):
    return acc, jnp.square(jnp.maximum(acc, 0.0))


def _drelu2(acc, up):
    return (acc * (2.0 * jnp.maximum(up, 0.0)),)


ROW_SHARDED = ("la_out", "sg_out", "ffn_down0", "ffn_down1")


class ExchangePlan:
    GATHERS = {"la_in_main": ("la_out", "ffn_up0"), "gdn_fwd": ("ffn_down0",), "ffn_up_0": ("sg_in",),
               "ffn_down_0": ("sg_out", "ffn_up1"), "sg_in": ("ffn_down1",)}
    REDUCES = {"ffn_dup_1": "ffn_down1", "ffn_dy_1": "ffn_up1", "sg_dus": "sg_out", "sg_dy": "sg_in",
               "ffn_dup_0": "ffn_down0", "ffn_dy_0": "ffn_up0", "la_docat": "la_out", "la_dy": "la_in"}

    def __init__(self, shard_halves, c_idx, j_idx):
        self.shard_halves, self.c_idx, self.j_idx = shard_halves, c_idx, j_idx
        self.partial = {}
        self.finished = {}

    def comm(self, carrier):
        if carrier in self.GATHERS:
            return gather_comm([self.shard_halves[w] for w in self.GATHERS[carrier]])
        if carrier in self.REDUCES:
            return chip_exchange_comm([self.partial[self.REDUCES[carrier]][2]])
        return None

    def done(self, carrier, outs, W):
        if carrier in self.GATHERS:
            for w, g in zip(self.GATHERS[carrier], outs):
                install_gathered(W, w, g)
        else:
            w = self.REDUCES[carrier]
            per_layer = w[:-1] in ("ffn_up", "ffn_down")
            key, layer, n_layers = (w[:-1], int(w[-1]), 2) if per_layer else (w, 0, 1)
            g, sib, _ = self.partial[w]
            self.finished[key] = chip_sum(g, sib, outs[0], self.c_idx, self.j_idx, name=f"grads_chip_sum_{w}",
                                          layer=layer, n_layers=n_layers, into=self.finished.get(key))

    def grad_ready(self, w, g, payload=None):
        halves = lambda t: t.reshape(N_CHIPS, 2, t.shape[1] // 2, t.shape[2])
        sib = pair_exchange([halves(g if payload is None else payload)], name=f"grads_pair_exchange_{w}")[0]
        self.partial[w] = (halves(g), sib, pair_sum(halves(g), sib, self.c_idx, name=f"grads_pair_sum_{w}"))


def install_gathered(W, w, g):
    whole = g.reshape(N_CHIPS, g.shape[1] * g.shape[2], g.shape[3])
    if w in ROW_SHARDED:
        whole = whole.reshape(-1, whole.shape[-1])
    if w[:-1] in ("ffn_up", "ffn_down"):
        W[w[:-1]][int(w[-1])] = whole
    else:
        W[w] = whole


def _by_chip(w, g):
    return g.reshape(N_CHIPS, -1, g.shape[-1]) if w in ROW_SHARDED else g


def _la_shard_rows(H):
    cs = (8 * H * HEAD_DIM + 2 * H) // N_CHIPS
    return cs, -(-cs // (2 * BF16_ROWS)) * (2 * BF16_ROWS)


def _la_pieces(H):
    HD = H * HEAD_DIM
    mix = 8 * HD + 2 * H
    cs = mix // N_CHIPS
    segments = [(0, 0, 4 * HD, 0), (1, 4 * HD, 4 * HD + 2 * H, 0), (0, 4 * HD + 2 * H, mix, 4 * HD)]
    pieces = []
    for j in range(N_CHIPS):
        mine = []
        for src, a, b, base in segments:
            lo, hi = max(cs * j, a), min(cs * (j + 1), b)
            if lo < hi:
                mine.append((src, base + lo - a, base + hi - a))
        pieces.append(mine)
    return pieces


def _la_weights_from_gathered(g, H):
    _, csp, D = g.shape
    tc = _tile(D, 2 * LANES)
    n_main = 8 * H * HEAD_DIM

    def body(g_ref, main_ref, gate_ref):
        parts = {0: [], 1: []}
        for j, mine in enumerate(_la_pieces(H)):
            row = 0
            for src, a, b in mine:
                parts[src].append(g_ref[j, row:row + b - a, :])
                row += b - a
        main_ref[...] = jnp.concatenate(parts[0], axis=0)
        gate = jnp.concatenate(parts[1], axis=0)
        gate_ref[...] = jnp.concatenate([gate, jnp.zeros((LANES - gate.shape[0], tc), gate.dtype)], axis=0)

    return _pcall(body, name="la_weights", grid=(D // tc,),
                  in_specs=[pl.BlockSpec((N_CHIPS, csp, tc), lambda i: (0, 0, i))],
                  out_specs=[pl.BlockSpec((n_main, tc), lambda i: (0, i)), pl.BlockSpec((LANES, tc), lambda i: (0, i))],
                  out_shape=[jax.ShapeDtypeStruct((n_main, D), g.dtype), jax.ShapeDtypeStruct((LANES, D), g.dtype)],
                  compiler_params=_cp(1))(g)


def _la_dproj_by_chip(main_parts, dpg, H):
    HD = H * HEAD_DIM
    cs, csp = _la_shard_rows(H)
    T = dpg.shape[0]
    tr = _tile(T, ROW_TILE, BF16_ROWS)
    n = len(main_parts)

    def body(*refs):
        parts, g_ref, o_ref = refs[:n], refs[n], refs[n + 1]
        pad = jnp.zeros((tr, csp - cs), o_ref.dtype)
        cols = []
        for mine in _la_pieces(H):
            for src, a, b in mine:
                while src == 0 and a < b:
                    i, off = divmod(a, HD)
                    end = min(b, (i + 1) * HD)
                    cols.append(parts[i][:, off:off + end - a])
                    a = end
                if src == 1:
                    cols.append(g_ref[:, a:b])
            cols.append(pad)
        o_ref[...] = jnp.concatenate(cols, axis=1)

    return _pcall(body, name="la_dproj", grid=(T // tr,),
                  in_specs=[_rows(tr, HD)] * n + [_rows(tr, LANES)], out_specs=_rows(tr, N_CHIPS * csp),
                  out_shape=jax.ShapeDtypeStruct((T, N_CHIPS * csp), BF), compiler_params=_cp(1))(*main_parts, dpg)


def _train_local(x2, tgt, W, plan=None):
    T, D = x2.shape
    H = W["a_log"].shape[0]
    nw = W["norm_w"]
    row = lambda v: v.reshape(1, -1).astype(F32)
    G = {}

    def mm(fn, *args, name, **kw):
        comm = plan.comm(name) if plan is not None else None
        if comm is None:
            return fn(*args, name=name, **kw)
        res, outs = fn(*args, name=name, comm=comm, **kw)
        plan.done(name, outs, W)
        return res

    def grad(w, g):
        payload = None
        if isinstance(g, (list, tuple)):
            g, payload = g
        G[w] = g
        if plan is not None:
            plan.grad_ready(w, _by_chip(w, g), None if payload is None else _by_chip(w, payload))

    def ffn_fwd(y, l):
        up, act = mm(mm_nn, y, W["ffn_up"][l], name=f"ffn_up_{l}", out_dtypes=(F32, BF), epilogue=_relu2)
        dn = mm(mm_nn, act, W["ffn_down"][l], name=f"ffn_down_{l}")
        return up, act, dn

    def ffn_bwd(y, up, act, ddn, l):
        grad(f"ffn_down{l}", mm_tn(act, ddn, name=f"ffn_dwdown_{l}", bf16_copy=True))
        dup = mm(mm_nt, ddn, W["ffn_down"][l], name=f"ffn_dup_{l}", out_dtypes=(BF,), epilogue=_drelu2, extras=(up,))
        grad(f"ffn_up{l}", mm_tn(y, dup, name=f"ffn_dwup_{l}", shards=N_CHIPS, bf16_copy=True))
        return mm(mm_nt, dup, W["ffn_up"][l], name=f"ffn_dy_{l}")

    y0 = rms_fwd(x2, row(nw[0, 0]), name="norm00")
    pm = mm(mm_nt, y0, W["la_in_main"], name="la_in_main")
    pg = mm_nt(y0, W["la_in_gate"], name="la_in_gate")
    wc8 = jnp.pad(jnp.transpose(W["conv_w"]), ((0, SUBLANES - CONV_WIDTH), (0, 0)))
    lanes_pad = (H, LANES - 2 * H)
    arow = jnp.pad(W["a_log"], lanes_pad).reshape(1, LANES)
    dtrow = jnp.pad(W["dt_bias"], lanes_pad).reshape(1, LANES)
    bg, gcrow = gates_fwd(pg, arow, dtrow, H, name="gates_fwd")
    q = prep_fwd(pm, 0, wc8, 0, H, True, name="prep_q")
    k = prep_fwd(pm, H, wc8, H, H, True, name="prep_k")
    v = prep_fwd(pm, 2 * H, wc8, 2 * H, H, False, name="prep_v")
    wn = row(W["out_norm_w"])
    gdn_comm = plan.comm("gdn_fwd") if plan is not None else None
    og_a, or_a, sall_a, tall, *carried = gdn_fwd(q, k, v, pm, 3 * H, bg, gcrow, wn, H, name="gdn_fwd", comm=gdn_comm)
    if gdn_comm is not None:
        plan.done("gdn_fwd", carried[0], W)
    cs, sn = _rotary_tables(T)
    dmat, avec, bvec, gam = _retention_tables(H)
    ocat, or_b, sall_b = ret_fwd(pm, 4 * H, 5 * H, 6 * H, 7 * H, cs, sn, dmat, avec, bvec, gam, og_a, H,
                                 name="ret_fwd")
    mix = mm_nn(ocat, W["la_out"], name="la_out")
    h1, y2 = res_norm(x2, mix, row(nw[0, 1]), row(nw[0, 2]), name="resnorm_0a")
    up, act, dn = ffn_fwd(y2, 0)
    h2, y0b = res_norm(h1, dn, row(nw[0, 3]), row(nw[1, 0]), name="resnorm_0b")

    pre = mm(mm_nn, y0b, W["sg_in"], name="sg_in")
    lw, lb = row(W["ln_w"]), row(W["ln_b"])
    vn = ln_fwd(pre, lw, lb, name="sg_ln")
    ws = W["w_s"]
    bs3 = W["b_s"][:, :, None]
    us = sg_fwd(pre, vn, ws, bs3, name="sg_gate")
    mix1 = mm_nn(us, W["sg_out"], name="sg_out")
    h3, y2b = res_norm(h2, mix1, row(nw[1, 1]), row(nw[1, 2]), name="resnorm_1a")
    up1, act1, dn1 = ffn_fwd(y2b, 1)
    dnw = [[None] * 4 for _ in range(2)]
    dh4, ddn1, dnw[1][3], lrow = last_norm_and_loss(h3, dn1, row(nw[1, 3]), tgt, name="last_norm_and_loss")
    loss = lrow[0, 0]

    dy2b = ffn_bwd(y2b, up1, act1, ddn1, 1)
    dh3, dnw[1][2], dmix1, dnw[1][1] = rms_bwd(h3, row(nw[1, 2]), dy2b, dh4, name="dnorm_1a",
                                               inner=(mix1, row(nw[1, 1])))
    grad("sg_out", mm_tn(us, dmix1, name="sg_dwout", bf16_copy=True))
    dus = mm(mm_nt, dmix1, W["sg_out"], name="sg_dus")
    dpre_u, dvn, G["w_s"], dbs3 = sg_bwd(pre, vn, ws, bs3, dus, name="sg_gate_bwd")
    G["b_s"] = dbs3[:, :, 0]
    dpre, dlw, dlb = ln_bwd(pre, lw, dvn, dpre_u, name="sg_ln_bwd")
    G["ln_w"], G["ln_b"] = dlw[0], dlb[0]
    grad("sg_in", mm_tn(y0b, dpre, name="sg_dwin", shards=N_CHIPS, bf16_copy=True))
    dy0b = mm(mm_nt, dpre, W["sg_in"], name="sg_dy")

    dh2, dnw[1][0], ddn, dnw[0][3] = rms_bwd(h2, row(nw[1, 0]), dy0b, dh3, name="dnorm_0b",
                                             inner=(dn, row(nw[0, 3])))
    dy2 = ffn_bwd(y2, up, act, ddn, 0)
    dh1, dnw[0][2], dmix, dnw[0][1] = rms_bwd(h1, row(nw[0, 2]), dy2, dh2, name="dnorm_0a",
                                              inner=(mix, row(nw[0, 1])))
    grad("la_out", mm_tn(ocat, dmix, name="la_dwout", bf16_copy=True))
    docat = mm(mm_nt, dmix, W["la_out"], name="la_docat")
    dq, dk, dv, dz, dbg, dwn = gdn_bwd(q, k, v, pm, 3 * H, bg, gcrow, wn, or_a, sall_a, tall, docat, H,
                                       name="gdn_bwd")
    drq, drk, drv, drg = ret_bwd(pm, 4 * H, 5 * H, 6 * H, 7 * H, cs, sn, dmat, avec, bvec, gam, or_b, sall_b,
                                 docat, H, H, name="ret_bwd")
    dpg, da, ddt = gates_bwd(pg, arow, dtrow, dbg, H, name="gates_bwd")
    dcq, dwq = prep_bwd_act(pm, 0, wc8, 0, H, True, dq, name="prep_dq")
    dck, dwk = prep_bwd_act(pm, H, wc8, H, H, True, dk, name="prep_dk")
    dcv, dwv = prep_bwd_act(pm, 2 * H, wc8, 2 * H, H, False, dv, name="prep_dv")
    dxq = prep_bwd_conv(dcq, wc8, 0, H, name="conv_dq")
    dxk = prep_bwd_conv(dck, wc8, H, H, name="conv_dk")
    dxv = prep_bwd_conv(dcv, wc8, 2 * H, H, name="conv_dv")
    dproj = _la_dproj_by_chip([dxq, dxk, dxv, dz, drq, drk, drv, drg], dpg, H)
    grad("la_in", mm_tn(dproj, y0, name="la_dwin").reshape(N_CHIPS, -1, D))
    dy0 = mm(mm_nn, dproj, W["la_in_rows"], name="la_dy")
    dx, dnw[0][0] = rms_bwd(x2, row(nw[0, 0]), dy0, dh1, name="dnorm00")

    G["norm_w"] = jnp.stack([jnp.concatenate(r, axis=0) for r in dnw], axis=0)
    G["conv_w"] = jnp.transpose(jnp.concatenate([dwq, dwk, dwv], axis=1)[:CONV_WIDTH])
    G["a_log"] = da[0, H:2 * H]
    G["dt_bias"] = ddt[0, H:2 * H]
    G["out_norm_w"] = dwn[0]
    return loss, dx, G


def _as2d(a):
    n = int(np.prod(a.shape))
    if a.shape[-1] < LANES and n % LANES == 0:
        return a.reshape(-1, LANES)
    return a.reshape(-1, a.shape[-1])


def _adamw_any(w, g, m, v, name):
    shp = w.shape
    d, m2, v2 = adamw(_as2d(w), _as2d(g.reshape(shp)), _as2d(m), _as2d(v), name=name)
    return g.reshape(shp), d.reshape(shp), m2.reshape(shp), v2.reshape(shp)


def kernel(x, norm_w, la_w_in, la_conv_w, la_a_log, la_dt_bias, la_out_norm_w, la_w_out, sg_w_in, sg_ln_w, sg_ln_b, sg_w_s, sg_b_s, sg_w_out, ffn_w_up, ffn_w_down, loss_target, m_norm_w, m_la_w_in, m_la_conv_w, m_la_a_log, m_la_dt_bias, m_la_out_norm_w, m_la_w_out, m_sg_w_in, m_sg_ln_w, m_sg_ln_b, m_sg_w_s, m_sg_b_s, m_sg_w_out, m_ffn_w_up, m_ffn_w_down, v_norm_w, v_la_w_in, v_la_conv_w, v_la_a_log, v_la_dt_bias, v_la_out_norm_w, v_la_w_out, v_sg_w_in, v_sg_ln_w, v_sg_ln_b, v_sg_w_s, v_sg_b_s, v_sg_w_out, v_ffn_w_up, v_ffn_w_down):
    weights = dict(norm_w=norm_w, la_w_in=la_w_in, la_conv_w=la_conv_w, la_a_log=la_a_log, la_dt_bias=la_dt_bias,
                   la_out_norm_w=la_out_norm_w, la_w_out=la_w_out, sg_w_in=sg_w_in, sg_ln_w=sg_ln_w,
                   sg_ln_b=sg_ln_b, sg_w_s=sg_w_s, sg_b_s=sg_b_s, sg_w_out=sg_w_out, ffn_w_up=ffn_w_up,
                   ffn_w_down=ffn_w_down)
    mom_m = dict(norm_w=m_norm_w, la_w_in=m_la_w_in, la_conv_w=m_la_conv_w, la_a_log=m_la_a_log,
                 la_dt_bias=m_la_dt_bias, la_out_norm_w=m_la_out_norm_w, la_w_out=m_la_w_out, sg_w_in=m_sg_w_in,
                 sg_ln_w=m_sg_ln_w, sg_ln_b=m_sg_ln_b, sg_w_s=m_sg_w_s, sg_b_s=m_sg_b_s, sg_w_out=m_sg_w_out,
                 ffn_w_up=m_ffn_w_up, ffn_w_down=m_ffn_w_down)
    mom_v = dict(norm_w=v_norm_w, la_w_in=v_la_w_in, la_conv_w=v_la_conv_w, la_a_log=v_la_a_log,
                 la_dt_bias=v_la_dt_bias, la_out_norm_w=v_la_out_norm_w, la_w_out=v_la_w_out, sg_w_in=v_sg_w_in,
                 sg_ln_w=v_sg_ln_w, sg_ln_b=v_sg_ln_b, sg_w_s=v_sg_w_s, sg_b_s=v_sg_b_s, sg_w_out=v_sg_w_out,
                 ffn_w_up=v_ffn_w_up, ffn_w_down=v_ffn_w_down)
    order = list(weights)

    T, D = x.shape[1], x.shape[2]
    H = la_a_log.shape[1]
    HD = H * HEAD_DIM
    xi, yi, ci = _place()
    chip = 2 * xi + yi
    c_idx = jnp.reshape(ci, (1,)).astype(jnp.int32)
    j_idx = jnp.reshape(chip, (1,)).astype(jnp.int32)

    cs, csp = _la_shard_rows(H)
    la_rows = lambda a: jnp.pad(jnp.swapaxes(a, 1, 2), ((0, 0), (0, csp - cs), (0, 0)))
    shards = dict(la_in=la_rows(la_w_in)[0], la_out=la_w_out[0], sg_in=sg_w_in[0], sg_out=sg_w_out[0],
                  ffn_up0=ffn_w_up[0], ffn_up1=ffn_w_up[1], ffn_down0=ffn_w_down[0], ffn_down1=ffn_w_down[1])
    shard_halves = {w: _halves(a.astype(BF)) for w, a in shards.items()}
    small_shapes = [norm_w.shape, la_conv_w[0].shape, sg_ln_w[0].shape, sg_ln_b[0].shape]
    small = _pack_rows([norm_w, la_conv_w[0], sg_ln_w[0], sg_ln_b[0]])
    small = _halves(jnp.pad(small, ((0, (-small.shape[0]) % (2 * SUBLANES)), (0, 0))))
    la_in_g, small_g = [g.reshape(N_CHIPS, -1, g.shape[-1])
                        for g in run_comm(gather_comm([shard_halves["la_in"], small]), name="gather_first")]
    pieces = [_unpack_rows(small_g[kk], small_shapes) for kk in range(N_CHIPS)]
    la_main, la_gate = _la_weights_from_gathered(la_in_g, H)
    W = dict(
        norm_w=jnp.concatenate([p[0] for p in pieces], axis=-1),
        conv_w=jnp.concatenate([p[1] for p in pieces], axis=0),
        ln_w=jnp.concatenate([p[2] for p in pieces], axis=0),
        ln_b=jnp.concatenate([p[3] for p in pieces], axis=0),
        a_log=la_a_log[0], dt_bias=la_dt_bias[0], out_norm_w=la_out_norm_w[0], w_s=sg_w_s[0], b_s=sg_b_s[0],
        la_in_main=la_main, la_in_gate=la_gate, la_in_rows=la_in_g.reshape(-1, D),
        ffn_up=[None, None], ffn_down=[None, None],
    )
    plan = ExchangePlan(shard_halves, c_idx, j_idx)

    loss_local, dx, G = _train_local(x[0], loss_target[0], W, plan)
    loss = lax.psum(loss_local, ("x", "y", "c"))

    big_params = dict(la_w_in="la_in", la_w_out="la_out", sg_w_in="sg_in", sg_w_out="sg_out",
                      ffn_w_up="ffn_up", ffn_w_down="ffn_down")
    from_sib = dict(zip(big_params, pair_share([plan.finished[k] for k in big_params.values()],
                                               name="grads_pair_share")))

    def big_update(nm, key):
        rows = la_rows if nm == "la_w_in" else (lambda a: a)
        r4 = adamw_halves(rows(weights[nm]), plan.finished[key], from_sib[nm], rows(mom_m[nm]), rows(mom_v[nm]),
                          c_idx, name=f"adamw_{nm}")
        return [jnp.swapaxes(t[:, :cs], 1, 2) for t in r4] if nm == "la_w_in" else r4

    big_res = {nm: big_update(nm, key) for nm, key in big_params.items()}

    small_names = ["norm_w", "conv_w", "ln_w", "ln_b", "a_log", "dt_bias", "out_norm_w", "w_s", "b_s"]
    small_full = [G[nm] for nm in small_names]
    summed = _unpack_rows(all_reduce_small(_pack_rows(small_full), name="grads_all_reduce_small"),
                          [g.shape for g in small_full])
    sm = dict(zip(small_names, summed))
    own = lambda full, axis: lax.dynamic_slice_in_dim(full, chip * (full.shape[axis] // N_CHIPS),
                                                      full.shape[axis] // N_CHIPS, axis)
    grads = dict(
        norm_w=own(sm["norm_w"], 2), la_conv_w=own(sm["conv_w"], 0), la_a_log=sm["a_log"],
        la_dt_bias=sm["dt_bias"], la_out_norm_w=sm["out_norm_w"],
        sg_ln_w=own(sm["ln_w"], 0), sg_ln_b=own(sm["ln_b"], 0), sg_w_s=sm["w_s"], sg_b_s=sm["b_s"],
    )

    res = {nm: big_res[nm] if nm in big_res else
           _adamw_any(weights[nm], grads[nm], mom_m[nm], mom_v[nm], f"adamw_{nm}") for nm in order}
    return (loss, dx.reshape(x.shape), *[res[nm][0] for nm in order], *[res[nm][1] for nm in order],
            *[res[nm][2] for nm in order], *[res[nm][3] for nm in order])
```

```python
import math
from typing import Callable, NamedTuple, Optional

import numpy as np
import jax
import jax.numpy as jnp
from jax import lax
from jax.experimental import pallas as pl
from jax.experimental.pallas import tpu as pltpu

F32 = jnp.float32
BF = jnp.bfloat16

V7X_VMEM_BYTES = 64 * 1024 * 1024
VMEM_LIMIT = (V7X_VMEM_BYTES * 3) // 4
LANES = 128
SUBLANES = 8
BF16_ROWS = 16
HEAD_DIM = 128
LA_CHUNK = 64
SG_CHUNK = 128
CONV_WIDTH = 4
ROPE_BASE = 10000.0
EPS = 1e-6
L2_EPS = 1e-6
N_CHIPS = 4
N_DEV = 8

ADAM_LR = 0.001
ADAM_B1 = 0.9
ADAM_B2 = 0.999
ADAM_EPS = 1e-08
ADAM_WD = 0.01
ADAM_STEP = 10

MESH = pl.DeviceIdType.MESH
ANY = pl.BlockSpec(memory_space=pl.ANY)

NN = (((1,), (0,)), ((), ()))
NT = (((1,), (1,)), ((), ()))
TN = (((0,), (0,)), ((), ()))


def _pcall(body, **kw):
    return pl.pallas_call(body, **kw)


def _cp(n_axes):
    return pltpu.CompilerParams(dimension_semantics=("arbitrary",) * n_axes, vmem_limit_bytes=VMEM_LIMIT)


def _tile(n, pref, unit=LANES):
    if n <= pref:
        return n
    t = (pref // unit) * unit
    while t >= unit:
        if n % t == 0:
            return t
        t -= unit
    return n


def _dot(a, b, dims=NN):
    return lax.dot_general(a.astype(BF), b.astype(BF), dims, preferred_element_type=F32)


def _split_bf16(x):
    hi = x.astype(BF)
    return hi, (x - hi.astype(F32)).astype(BF)


def _dot_hi(a, b, dims=NN):
    ah, al = _split_bf16(a)
    bh, bl = _split_bf16(b)
    dot = lambda u, v: lax.dot_general(u, v, dims, preferred_element_type=F32)
    return dot(ah, bh) + (dot(ah, bl) + dot(al, bh))


def _sigmoid(x):
    return 1.0 / (1.0 + jnp.exp(-x))


def _silu(x):
    return x * _sigmoid(x)


def _dsilu(x):
    s = _sigmoid(x)
    return s * (1.0 + x * (1.0 - s))


GELU_C = math.sqrt(2.0 / math.pi)
GELU_A = 0.044715


def _gelu(x):
    return 0.5 * x * (1.0 + jnp.tanh(GELU_C * (x + GELU_A * x * x * x)))


def _gelu_and_grad(x):
    t = jnp.tanh(GELU_C * (x + GELU_A * x * x * x))
    return 0.5 * x * (1.0 + t), 0.5 * (1.0 + t) + 0.5 * x * (1.0 - t * t) * GELU_C * (1.0 + 3.0 * GELU_A * x * x)


class Comm(NamedTuple):
    ins: list
    outs: list
    nsem: int
    start: Callable
    finish: Callable
    middle: Optional[Callable] = None


def _carry_steps(comm, step, n_steps, refs):
    ci, co, send, recv = refs

    @pl.when(step == 0)
    def _():
        comm.start(ci, co, send, recv)

    if comm.middle is not None:
        @pl.when(step == (3 * n_steps) // 4)
        def _():
            comm.middle(ci, co, send, recv)


def _carry_end(comm, step, n_steps, refs):
    ci, co, send, recv = refs

    @pl.when(step == n_steps - 1)
    def _():
        comm.finish(ci, co, send, recv)


def _matmul(a, b, *, dims, grid, a_spec, b_spec, out_shape, out_spec, acc_shape, name,
            epilogue=None, extras=(), extra_specs=(), comm=None):
    nk = grid[2]
    outs = tuple(out_shape) if isinstance(out_shape, (tuple, list)) else (out_shape,)
    out_specs = tuple(out_spec) if isinstance(out_spec, (tuple, list)) else (out_spec,)
    n_ex, n_out = len(extras), len(outs)
    n_ci = len(comm.ins) if comm else 0
    n_co = len(comm.outs) if comm else 0

    def body(*refs):
        a_ref, b_ref = refs[0], refs[1]
        ex = refs[2:2 + n_ex]
        ci = refs[2 + n_ex:2 + n_ex + n_ci]
        o = refs[2 + n_ex + n_ci:2 + n_ex + n_ci + n_out]
        co = refs[2 + n_ex + n_ci + n_out:2 + n_ex + n_ci + n_out + n_co]
        scratch = refs[2 + n_ex + n_ci + n_out + n_co:]
        i, j, k = pl.program_id(0), pl.program_id(1), pl.program_id(2)

        if comm:
            step, n_steps = (i * grid[1] + j) * nk + k, grid[0] * grid[1] * nk
            carried = (ci, co, scratch[-2], scratch[-1])
            _carry_steps(comm, step, n_steps, carried)

        part = lax.dot_general(a_ref[...].astype(BF), b_ref[...].astype(BF), dims, preferred_element_type=F32)

        def finish(val):
            res = epilogue(val, *[e[...] for e in ex]) if epilogue is not None else (val,)
            for r, oref in zip(res, o):
                oref[...] = r.astype(oref.dtype)

        if nk == 1:
            finish(part)
        else:
            acc = scratch[0]

            @pl.when(k == 0)
            def _():
                acc[...] = part

            @pl.when(k > 0)
            def _():
                acc[...] += part

            @pl.when(k == nk - 1)
            def _():
                finish(acc[...])

        if comm:
            _carry_end(comm, step, n_steps, carried)

    scratch_shapes = [pltpu.VMEM(acc_shape, F32)] if nk > 1 else []
    if comm:
        scratch_shapes += [pltpu.SemaphoreType.DMA((comm.nsem,)), pltpu.SemaphoreType.DMA((comm.nsem,))]
    res = _pcall(
        body, name=name, grid=grid,
        in_specs=[a_spec, b_spec, *extra_specs, *[ANY] * n_ci],
        out_specs=[*out_specs, *[ANY] * n_co],
        out_shape=[*outs, *(comm.outs if comm else [])],
        scratch_shapes=scratch_shapes,
        compiler_params=_cp(3),
    )(a, b, *extras, *(comm.ins if comm else []))
    main = res[0] if n_out == 1 else list(res[:n_out])
    return (main, list(res[n_out:])) if comm else main


def mm_nn(a, w, *, name, out_dtypes=(F32,), epilogue=None, extras=(), comm=None, tm=1024, tn=1024, tk=2048):
    M, K = a.shape
    if w.ndim == 3:
        S, _, Ns = w.shape
        N = S * Ns
    else:
        S, Ns = 1, w.shape[1]
        N = Ns
    tm, tn, tk = _tile(M, tm), _tile(Ns, tn), _tile(K, tk)
    npb = Ns // tn
    grid = (M // tm, N // tn, K // tk)
    a_spec = pl.BlockSpec((tm, tk), lambda i, j, k: (i, k))
    if w.ndim == 3:
        b_spec = pl.BlockSpec((None, tk, tn), lambda i, j, k: (j // npb, k, j % npb))
    else:
        b_spec = pl.BlockSpec((tk, tn), lambda i, j, k: (k, j))
    o_spec = pl.BlockSpec((tm, tn), lambda i, j, k: (i, j))
    outs = tuple(jax.ShapeDtypeStruct((M, N), d) for d in out_dtypes)
    res = _matmul(a, w, dims=NN, grid=grid, a_spec=a_spec, b_spec=b_spec,
                  out_shape=outs, out_spec=(o_spec,) * len(outs), acc_shape=(tm, tn), name=name,
                  epilogue=epilogue, extras=extras, extra_specs=(o_spec,) * len(extras), comm=comm)
    return res


def mm_nt(a, w, *, name, out_dtypes=(F32,), epilogue=None, extras=(), comm=None, tm=1024, tn=1024, tk=2048):
    M, Kc = a.shape
    if w.ndim == 3:
        S, Nout, Ks = w.shape
    else:
        S, (Nout, Ks) = 1, w.shape
    assert S * Ks == Kc
    tm, tn, tk = _tile(M, tm), _tile(Nout, tn), _tile(Ks, tk)
    kpb = Ks // tk
    grid = (M // tm, Nout // tn, Kc // tk)
    a_spec = pl.BlockSpec((tm, tk), lambda i, j, k: (i, k))
    if w.ndim == 3:
        b_spec = pl.BlockSpec((None, tn, tk), lambda i, j, k: (k // kpb, j, k % kpb))
    else:
        b_spec = pl.BlockSpec((tn, tk), lambda i, j, k: (j, k))
    o_spec = pl.BlockSpec((tm, tn), lambda i, j, k: (i, j))
    outs = tuple(jax.ShapeDtypeStruct((M, Nout), d) for d in out_dtypes)
    return _matmul(a, w, dims=NT, grid=grid, a_spec=a_spec, b_spec=b_spec,
                   out_shape=outs, out_spec=(o_spec,) * len(outs), acc_shape=(tm, tn), name=name,
                   epilogue=epilogue, extras=extras, extra_specs=(o_spec,) * len(extras), comm=comm)


def mm_tn(x, dy, *, name, shards=1, bf16_copy=False, tm=1024, tn=1024, tk=2048):
    T, Kin = x.shape
    N = dy.shape[1]
    Ns = N // shards
    tm, tn, tk = _tile(Kin, tm), _tile(Ns, tn), _tile(T, tk)
    npb = Ns // tn
    grid = (Kin // tm, N // tn, T // tk)
    a_spec = pl.BlockSpec((tk, tm), lambda i, j, k: (k, i))
    b_spec = pl.BlockSpec((tk, tn), lambda i, j, k: (k, j))
    if shards > 1:
        o_spec = pl.BlockSpec((None, tm, tn), lambda i, j, k: (j // npb, i, j % npb))
        out = jax.ShapeDtypeStruct((shards, Kin, Ns), F32)
    else:
        o_spec = pl.BlockSpec((tm, tn), lambda i, j, k: (i, j))
        out = jax.ShapeDtypeStruct((Kin, N), F32)
    if bf16_copy:
        return _matmul(x, dy, dims=TN, grid=grid, a_spec=a_spec, b_spec=b_spec,
                       out_shape=(out, jax.ShapeDtypeStruct(out.shape, BF)), out_spec=(o_spec, o_spec),
                       acc_shape=(tm, tn), name=name, epilogue=lambda acc: (acc, acc))
    return _matmul(x, dy, dims=TN, grid=grid, a_spec=a_spec, b_spec=b_spec,
                   out_shape=out, out_spec=o_spec, acc_shape=(tm, tn), name=name)


ROW_TILE = 256


def _rows(tr, d):
    return pl.BlockSpec((tr, d), lambda i: (i, 0))


def _fixed(shape):
    nd = len(shape)
    return pl.BlockSpec(shape, lambda *_: (0,) * nd)


def _rms(xv, w):
    r = lax.rsqrt(jnp.mean(xv * xv, axis=-1, keepdims=True) + EPS)
    return xv * r * w


def rms_fwd(x, w, *, name):
    T, D = x.shape
    tr = _tile(T, ROW_TILE, SUBLANES)

    def body(x_ref, w_ref, y_ref):
        y_ref[...] = _rms(x_ref[...], w_ref[...]).astype(y_ref.dtype)

    return _pcall(body, name=name, grid=(T // tr,), in_specs=[_rows(tr, D), _fixed((1, D))],
                  out_specs=_rows(tr, D), out_shape=jax.ShapeDtypeStruct((T, D), BF), compiler_params=_cp(1))(x, w)


def res_norm(h, m, wa, wb, *, name):
    T, D = h.shape
    tr = _tile(T, ROW_TILE, SUBLANES)

    def body(h_ref, m_ref, wa_ref, wb_ref, ho_ref, y_ref):
        ho = h_ref[...] + _rms(m_ref[...], wa_ref[...])
        ho_ref[...] = ho
        y_ref[...] = _rms(ho, wb_ref[...]).astype(y_ref.dtype)

    return _pcall(body, name=name, grid=(T // tr,),
                  in_specs=[_rows(tr, D), _rows(tr, D), _fixed((1, D)), _fixed((1, D))],
                  out_specs=[_rows(tr, D), _rows(tr, D)],
                  out_shape=[jax.ShapeDtypeStruct((T, D), F32), jax.ShapeDtypeStruct((T, D), BF)],
                  compiler_params=_cp(1))(h, m, wa, wb)


def _rms_bwd_rows(xv, w, dyv):
    r = lax.rsqrt(jnp.mean(xv * xv, axis=-1, keepdims=True) + EPS)
    xh = xv * r
    dyw = dyv * w
    return r * (dyw - xh * jnp.mean(dyw * xh, axis=-1, keepdims=True)), jnp.sum(dyv * xh, axis=0, keepdims=True)


def rms_bwd(x, w, dy, dres, *, name, inner=None, comm=None):
    T, D = x.shape
    tr = _tile(T, ROW_TILE, SUBLANES)
    chained = inner is not None
    n_in = 6 if chained else 4
    n_out = 4 if chained else 2
    n_ci = len(comm.ins) if comm else 0
    n_co = len(comm.outs) if comm else 0

    def body(*refs):
        ci = refs[n_in:n_in + n_ci]
        outs = refs[n_in + n_ci:n_in + n_ci + n_out]
        co = refs[n_in + n_ci + n_out:n_in + n_ci + n_out + n_co]
        if chained:
            x_ref, w_ref, dy_ref, dr_ref, m_ref, wa_ref = refs[:n_in]
            dx_ref, dw_ref, dm_ref, dwa_ref = outs
        else:
            x_ref, w_ref, dy_ref, dr_ref = refs[:n_in]
            dx_ref, dw_ref = outs
        i = pl.program_id(0)
        if comm:
            carried = (ci, co, refs[-2], refs[-1])
            _carry_steps(comm, i, T // tr, carried)
        dx, part = _rms_bwd_rows(x_ref[...], w_ref[...], dy_ref[...].astype(F32))
        dx = dx + dr_ref[...]
        dx_ref[...] = dx
        if chained:
            dm, part_a = _rms_bwd_rows(m_ref[...], wa_ref[...], dx)
            dm_ref[...] = dm.astype(dm_ref.dtype)

        @pl.when(i == 0)
        def _():
            dw_ref[...] = part
            if chained:
                dwa_ref[...] = part_a

        @pl.when(i > 0)
        def _():
            dw_ref[...] += part
            if chained:
                dwa_ref[...] += part_a

        if comm:
            _carry_end(comm, i, T // tr, carried)

    ins = [x, w, dy, dres] + (list(inner) if chained else [])
    in_specs = [_rows(tr, D), _fixed((1, D)), _rows(tr, D), _rows(tr, D)]
    out_specs = [_rows(tr, D), _fixed((1, D))]
    out_shape = [jax.ShapeDtypeStruct((T, D), F32), jax.ShapeDtypeStruct((1, D), F32)]
    if chained:
        in_specs += [_rows(tr, D), _fixed((1, D))]
        out_specs += [_rows(tr, D), _fixed((1, D))]
        out_shape += [jax.ShapeDtypeStruct((T, D), BF), jax.ShapeDtypeStruct((1, D), F32)]
    scratch_shapes = []
    if comm:
        ins, in_specs = ins + list(comm.ins), in_specs + [ANY] * n_ci
        out_specs, out_shape = out_specs + [ANY] * n_co, out_shape + list(comm.outs)
        scratch_shapes = [pltpu.SemaphoreType.DMA((comm.nsem,)), pltpu.SemaphoreType.DMA((comm.nsem,))]
    res = _pcall(body, name=name, grid=(T // tr,), in_specs=in_specs, out_specs=out_specs, out_shape=out_shape,
                 scratch_shapes=scratch_shapes, compiler_params=_cp(1))(*ins)
    return (list(res[:n_out]), list(res[n_out:])) if comm else res


def last_norm_and_loss(h, m, w, tgt, *, name):
    T, D = h.shape
    tr = _tile(T, ROW_TILE, SUBLANES)

    def body(h_ref, m_ref, w_ref, t_ref, dy_ref, dm_ref, dw_ref, l_ref):
        i = pl.program_id(0)
        mv = m_ref[...]
        r = lax.rsqrt(jnp.mean(mv * mv, axis=-1, keepdims=True) + EPS)
        xh = mv * r
        e = h_ref[...] + xh * w_ref[...] - t_ref[...]
        dy = e * (1.0 / D)
        dy_ref[...] = dy
        dyw = dy * w_ref[...]
        dm_ref[...] = (r * (dyw - xh * jnp.mean(dyw * xh, axis=-1, keepdims=True))).astype(dm_ref.dtype)
        pw = jnp.sum(dy * xh, axis=0, keepdims=True)
        pl_ = 0.5 * jnp.sum(jnp.mean(e * e, axis=-1, keepdims=True), axis=0, keepdims=True)
        pl_ = jnp.broadcast_to(pl_, (1, LANES))

        @pl.when(i == 0)
        def _():
            dw_ref[...] = pw
            l_ref[...] = pl_

        @pl.when(i > 0)
        def _():
            dw_ref[...] += pw
            l_ref[...] += pl_

    return _pcall(body, name=name, grid=(T // tr,),
                  in_specs=[_rows(tr, D), _rows(tr, D), _fixed((1, D)), _rows(tr, D)],
                  out_specs=[_rows(tr, D), _rows(tr, D), _fixed((1, D)), _fixed((1, LANES))],
                  out_shape=[jax.ShapeDtypeStruct((T, D), F32), jax.ShapeDtypeStruct((T, D), BF),
                             jax.ShapeDtypeStruct((1, D), F32), jax.ShapeDtypeStruct((1, LANES), F32)],
                  compiler_params=_cp(1))(h, m, w, tgt)


def adamw(w, g, m, v, *, name):
    R, C = w.shape
    tr = _tile(R, max(SUBLANES, (1 << 18) // C), SUBLANES)
    c1 = 1.0 - ADAM_B1 ** ADAM_STEP
    c2 = 1.0 - ADAM_B2 ** ADAM_STEP

    def body(w_ref, g_ref, m_ref, v_ref, d_ref, mo_ref, vo_ref):
        gv = g_ref[...]
        m2 = ADAM_B1 * m_ref[...] + (1.0 - ADAM_B1) * gv
        v2 = ADAM_B2 * v_ref[...] + (1.0 - ADAM_B2) * (gv * gv)
        d_ref[...] = -ADAM_LR * ((m2 / c1) / (jnp.sqrt(v2 / c2) + ADAM_EPS) + ADAM_WD * w_ref[...])
        mo_ref[...] = m2
        vo_ref[...] = v2

    spec = _rows(tr, C)
    sds = jax.ShapeDtypeStruct((R, C), F32)
    return _pcall(body, name=name, grid=(R // tr,), in_specs=[spec] * 4, out_specs=[spec] * 3,
                  out_shape=[sds] * 3, compiler_params=_cp(1))(w, g, m, v)


HALO = SUBLANES


def _conv_down(xx, w_ref):
    acc = xx * w_ref[pl.ds(CONV_WIDTH - 1, 1), :]
    for d in range(1, CONV_WIDTH):
        acc = acc + pltpu.roll(xx, d, 0) * w_ref[pl.ds(CONV_WIDTH - 1 - d, 1), :]
    return acc


def _conv_tile(x_ref, halo_ref, w_ref, first):
    xs = x_ref[...]
    hal = jnp.where(first, 0.0, halo_ref[...])
    cat = jnp.concatenate([hal, xs[0:HALO]], axis=0)
    return jnp.concatenate([_conv_down(cat, w_ref)[HALO:2 * HALO], _conv_down(xs, w_ref)[HALO:]], axis=0)


def _shift_down_tile(x_ref, halo_ref, first, d):
    xs = x_ref[...]
    if d == 0:
        return xs
    hal = jnp.where(first, 0.0, halo_ref[...])
    cat = jnp.concatenate([hal, xs[0:HALO]], axis=0)
    return jnp.concatenate([pltpu.roll(cat, d, 0)[HALO:2 * HALO], pltpu.roll(xs, d, 0)[HALO:]], axis=0)


def _l2n(s):
    return s * lax.rsqrt(jnp.sum(s * s, axis=-1, keepdims=True) + L2_EPS)


PREP_ROWS = 512


def _l2n_groups(s, nb):
    return jnp.concatenate([_l2n(s[:, g * LANES:(g + 1) * LANES]) for g in range(nb)], axis=1)


def prep_fwd(pm, off, wc8, woff, nblk, l2, *, name):
    T = pm.shape[0]
    tr = _tile(T, PREP_ROWS, SUBLANES)
    hb = tr // HALO
    wb = _heads_per_step(nblk)
    wl = wb * LANES

    def body(x_ref, halo_ref, w_ref, o_ref):
        i = pl.program_id(0)
        s = _silu(_conv_tile(x_ref, halo_ref, w_ref, i == 0))
        o_ref[...] = _l2n_groups(s, wb) if l2 else s

    return _pcall(
        body, name=name, grid=(T // tr, nblk // wb),
        in_specs=[pl.BlockSpec((tr, wl), lambda i, c: (i, off // wb + c)),
                  pl.BlockSpec((HALO, wl), lambda i, c: (jnp.maximum(i * hb - 1, 0), off // wb + c)),
                  pl.BlockSpec((SUBLANES, wl), lambda i, c: (0, woff // wb + c))],
        out_specs=pl.BlockSpec((tr, wl), lambda i, c: (i, c)),
        out_shape=jax.ShapeDtypeStruct((T, nblk * LANES), F32), compiler_params=_cp(2))(pm, pm, wc8)


def prep_bwd_act(pm, off, wc8, woff, nblk, l2, dout, *, name):
    T = pm.shape[0]
    tr = _tile(T, PREP_ROWS, SUBLANES)
    hb = tr // HALO
    wb = _heads_per_step(nblk)
    wl = wb * LANES

    def l2_bwd(s, do):
        r = lax.rsqrt(jnp.sum(s * s, axis=-1, keepdims=True) + L2_EPS)
        nrm = s * r
        return r * (do - nrm * jnp.sum(do * nrm, axis=-1, keepdims=True))

    def body(x_ref, halo_ref, w_ref, do_ref, dc_ref, dw_ref):
        i = pl.program_id(1)
        first = i == 0
        y = _conv_tile(x_ref, halo_ref, w_ref, first)
        s = _silu(y)
        do = do_ref[...]
        if l2:
            ds = jnp.concatenate([l2_bwd(s[:, g * LANES:(g + 1) * LANES], do[:, g * LANES:(g + 1) * LANES])
                                  for g in range(wb)], axis=1)
        else:
            ds = do
        dc = ds * _dsilu(y)
        dc_ref[...] = dc

        @pl.when(first)
        def _():
            dw_ref[...] = jnp.zeros_like(dw_ref)

        for j in range(CONV_WIDTH):
            xsh = _shift_down_tile(x_ref, halo_ref, first, CONV_WIDTH - 1 - j)
            dw_ref[pl.ds(j, 1), :] += jnp.sum(dc * xsh, axis=0, keepdims=True)

    return _pcall(
        body, name=name, grid=(nblk // wb, T // tr),
        in_specs=[pl.BlockSpec((tr, wl), lambda c, i: (i, off // wb + c)),
                  pl.BlockSpec((HALO, wl), lambda c, i: (jnp.maximum(i * hb - 1, 0), off // wb + c)),
                  pl.BlockSpec((SUBLANES, wl), lambda c, i: (0, woff // wb + c)),
                  pl.BlockSpec((tr, wl), lambda c, i: (i, c))],
        out_specs=[pl.BlockSpec((tr, wl), lambda c, i: (i, c)),
                   pl.BlockSpec((SUBLANES, wl), lambda c, i: (0, c))],
        out_shape=[jax.ShapeDtypeStruct((T, nblk * LANES), F32),
                   jax.ShapeDtypeStruct((SUBLANES, nblk * LANES), F32)],
        compiler_params=_cp(2))(pm, pm, wc8, dout)


def prep_bwd_conv(dc, wc8, woff, nblk, *, name):
    T = dc.shape[0]
    tr = _tile(T, PREP_ROWS, SUBLANES)
    hb = tr // HALO
    nt = T // tr
    last_halo = T // HALO - 1
    wb = _heads_per_step(nblk)
    wl = wb * LANES

    def up(xx, w_ref):
        rows = xx.shape[0]
        acc = xx * w_ref[pl.ds(CONV_WIDTH - 1, 1), :]
        for d in range(1, CONV_WIDTH):
            acc = acc + pltpu.roll(xx, rows - d, 0) * w_ref[pl.ds(CONV_WIDTH - 1 - d, 1), :]
        return acc

    def body(x_ref, halo_ref, w_ref, o_ref):
        i = pl.program_id(0)
        xs = x_ref[...]
        hal = jnp.where(i == nt - 1, 0.0, halo_ref[...])
        cat = jnp.concatenate([xs[tr - HALO:tr], hal], axis=0)
        out = jnp.concatenate([up(xs, w_ref)[:tr - HALO], up(cat, w_ref)[0:HALO]], axis=0)
        o_ref[...] = out.astype(o_ref.dtype)

    return _pcall(
        body, name=name, grid=(nt, nblk // wb),
        in_specs=[pl.BlockSpec((tr, wl), lambda i, c: (i, c)),
                  pl.BlockSpec((HALO, wl), lambda i, c: (jnp.minimum((i + 1) * hb, last_halo), c)),
                  pl.BlockSpec((SUBLANES, wl), lambda i, c: (0, woff // wb + c))],
        out_specs=pl.BlockSpec((tr, wl), lambda i, c: (i, c)),
        out_shape=jax.ShapeDtypeStruct((T, nblk * LANES), BF), compiler_params=_cp(2))(dc, dc, wc8)


def _softplus(x):
    return jnp.maximum(x, 0.0) + jnp.log(1.0 + jnp.exp(-jnp.abs(x)))


def _tril_ones(c):
    t = lax.broadcasted_iota(jnp.int32, (c, c), 0)
    s = lax.broadcasted_iota(jnp.int32, (c, c), 1)
    return (t >= s).astype(F32)


GATE_CHUNKS_PER_STEP = 8


def _chunks_per_step(n_chunks):
    per = GATE_CHUNKS_PER_STEP
    while n_chunks % per:
        per //= 2
    return per


def _triu_ones(c):
    t = lax.broadcasted_iota(jnp.int32, (c, c), 0)
    s = lax.broadcasted_iota(jnp.int32, (c, c), 1)
    return (t <= s).astype(F32)


def gates_fwd(pg, arow, dtrow, H, *, name):
    T = pg.shape[0]
    C = LA_CHUNK
    N = T // C
    per = _chunks_per_step(N)

    def body(x_ref, a_ref, dt_ref, bg_ref, gr_ref):
        lane = lax.broadcasted_iota(jnp.int32, (C, LANES), 1)
        lm, um = _tril_ones(C), _triu_ones(C)
        for j in range(per):
            rows = slice(j * C, (j + 1) * C)
            x = x_ref[rows, :]
            g = -jnp.exp(a_ref[...]) * _softplus(x + dt_ref[...])
            g = jnp.where((lane >= H) & (lane < 2 * H), g, 0.0)
            bg_ref[rows, :] = jnp.where(lane < H, _sigmoid(x), _dot_hi(lm, g))
            gr_ref[j] = _dot_hi(g, um, TN)

    return _pcall(
        body, name=name, grid=(N // per,),
        in_specs=[pl.BlockSpec((per * C, LANES), lambda n: (n, 0)), _fixed((1, LANES)), _fixed((1, LANES))],
        out_specs=[pl.BlockSpec((per * C, LANES), lambda n: (n, 0)),
                   pl.BlockSpec((per, LANES, C), lambda n: (n, 0, 0))],
        out_shape=[jax.ShapeDtypeStruct((T, LANES), F32), jax.ShapeDtypeStruct((N, LANES, C), F32)],
        compiler_params=_cp(1))(pg, arow, dtrow)


def gates_bwd(pg, arow, dtrow, dbg, H, *, name):
    T = pg.shape[0]
    C = LA_CHUNK
    N = T // C
    per = _chunks_per_step(N)

    def body(x_ref, a_ref, dt_ref, d_ref, dx_ref, da_ref, ddt_ref):
        n = pl.program_id(0)
        lane = lax.broadcasted_iota(jnp.int32, (C, LANES), 1)
        in_g = (lane >= H) & (lane < 2 * H)
        e = jnp.exp(a_ref[...])
        lm = _tril_ones(C)
        pa = jnp.zeros((1, LANES), F32)
        pd = jnp.zeros((1, LANES), F32)
        for j in range(per):
            rows = slice(j * C, (j + 1) * C)
            x = x_ref[rows, :]
            d = d_ref[rows, :]
            xs = x + dt_ref[...]
            g = -e * _softplus(xs)
            dg = _dot_hi(lm, jnp.where(in_g, d, 0.0), TN)
            dxs = jnp.where(in_g, dg * (-e) * _sigmoid(xs), 0.0)
            beta = _sigmoid(x)
            dx_ref[rows, :] = jnp.where(lane < H, d * beta * (1.0 - beta), dxs).astype(dx_ref.dtype)
            pa = pa + jnp.sum(jnp.where(in_g, dg * g, 0.0), axis=0, keepdims=True)
            pd = pd + jnp.sum(dxs, axis=0, keepdims=True)

        @pl.when(n == 0)
        def _():
            da_ref[...] = pa
            ddt_ref[...] = pd

        @pl.when(n > 0)
        def _():
            da_ref[...] += pa
            ddt_ref[...] += pd

    rows = pl.BlockSpec((per * C, LANES), lambda n: (n, 0))
    return _pcall(
        body, name=name, grid=(N // per,),
        in_specs=[rows, _fixed((1, LANES)), _fixed((1, LANES)), rows],
        out_specs=[rows, _fixed((1, LANES)), _fixed((1, LANES))],
        out_shape=[jax.ShapeDtypeStruct((T, LANES), BF), jax.ShapeDtypeStruct((1, LANES), F32),
                   jax.ShapeDtypeStruct((1, LANES), F32)],
        compiler_params=_cp(1))(pg, arow, dtrow, dbg)


QK_SCALE = HEAD_DIM ** -0.5


HEADS_PER_STEP = 8


def _heads_per_step(H):
    hb = HEADS_PER_STEP
    while H % hb:
        hb //= 2
    return hb


def _head_rstd(o):
    return lax.rsqrt(jnp.mean(o * o, axis=-1, keepdims=True) + EPS)


def _gdn_gates(bg_ref, gr_ref, h, H):
    C = LA_CHUNK
    bgv = bg_ref[...]
    lane = lax.broadcasted_iota(jnp.int32, (C, LANES), 1)
    beta = jnp.sum(jnp.where(lane == h, bgv, 0.0), axis=1, keepdims=True)
    gc = jnp.sum(jnp.where(lane == H + h, bgv, 0.0), axis=1, keepdims=True)
    grow = gr_ref[pl.ds(H + h, 1), :]
    ri = lax.broadcasted_iota(jnp.int32, (C, 1), 0)
    gl = jnp.sum(jnp.where(ri == C - 1, gc, 0.0), axis=0, keepdims=True)
    return beta, gc, grow, gl


def _chunk_masks():
    C = LA_CHUNK
    ti = lax.broadcasted_iota(jnp.int32, (C, C), 0)
    si = lax.broadcasted_iota(jnp.int32, (C, C), 1)
    return ti >= si, ti > si, ti == si


def _decay(gc, grow, causal):
    return jnp.where(causal, jnp.exp(jnp.where(causal, gc - grow, 0.0)), 0.0)


def _interleave(gens):
    gens = list(gens)
    results = [None] * len(gens)
    live = list(range(len(gens)))
    while live:
        still = []
        for i in live:
            try:
                next(gens[i])
                still.append(i)
            except StopIteration as stop:
                results[i] = stop.value
        live = still
    return results


def _unit_lower_inverse(a, eye):
    x = -a
    p = jnp.where(eye, 1.0, 0.0) + x
    for _ in range(5):
        x = _dot_hi(x, x)
        yield
        p = p + _dot_hi(p, x)
        yield
    return p


def gdn_fwd(q, k, v, pm, zoff, bg, gcrow, wn, H, *, name, comm=None):
    T = q.shape[0]
    C = LA_CHUNK
    N = T // C
    hd = HEAD_DIM

    HB = _heads_per_step(H)
    n_ci = len(comm.ins) if comm else 0
    n_co = len(comm.outs) if comm else 0

    def body(*refs):
        q_ref, k_ref, v_ref, z_ref, bg_ref, gr_ref, wn_ref = refs[:7]
        ci = refs[7:7 + n_ci]
        og_ref, or_ref, sall_ref, tall_ref = refs[7 + n_ci:11 + n_ci]
        co = refs[11 + n_ci:11 + n_ci + n_co]
        S = refs[11 + n_ci + n_co]
        n = pl.program_id(0)
        hg = pl.program_id(1)
        causal, strict, eye = _chunk_masks()

        @pl.when((n == 0) & (hg == 0))
        def _():
            S[...] = jnp.zeros_like(S)

        if comm:
            step, n_steps = n * (H // HB) + hg, N * (H // HB)
            carried = (ci, co, refs[-2], refs[-1])
            _carry_steps(comm, step, n_steps, carried)

        states = [S[hg * HB + i] for i in range(HB)]

        def head(i):
            h = hg * HB + i
            sl = slice(i * hd, (i + 1) * hd)
            beta, gc, grow, gl = _gdn_gates(bg_ref, gr_ref, h, H)
            dm = _decay(gc, grow, causal)
            qs = q_ref[:, sl] * QK_SCALE
            kk = k_ref[:, sl]
            vv = v_ref[:, sl]
            eg = jnp.exp(gc)
            kb = kk * beta
            a = jnp.where(strict, _dot(kb, kk, NT) * dm, 0.0)
            yield
            tm = yield from _unit_lower_inverse(a, eye)
            u = _dot(tm, vv * beta)
            w = _dot(tm, kb * eg)
            qk = jnp.where(causal, _dot(qs, kk, NT) * dm, 0.0)
            yield
            s0 = states[i]
            vnew = u - _dot(w, s0)
            o = _dot(qs * eg, s0)
            yield
            o = o + _dot(qk, vnew)
            s1 = s0 * jnp.exp(gl) + _dot(kk * jnp.exp(gl - gc), vnew, TN)
            yield
            sall_ref[i] = s0
            tall_ref[i] = tm
            or_ref[:, sl] = o
            og_ref[:, sl] = (o * _head_rstd(o) * wn_ref[...] * _silu(z_ref[:, sl])).astype(og_ref.dtype)
            return s1

        for i, s1 in enumerate(_interleave([head(i) for i in range(HB)])):
            S[hg * HB + i] = s1

        if comm:
            _carry_end(comm, step, n_steps, carried)

    blk = lambda off: pl.BlockSpec((C, HB * hd), lambda n, h: (n, off // HB + h))
    scratch_shapes = [pltpu.VMEM((H, hd, hd), F32)]
    if comm:
        scratch_shapes += [pltpu.SemaphoreType.DMA((comm.nsem,)), pltpu.SemaphoreType.DMA((comm.nsem,))]
    res = _pcall(
        body, name=name, grid=(N, H // HB),
        in_specs=[blk(0), blk(0), blk(0), blk(zoff),
                  pl.BlockSpec((C, LANES), lambda n, h: (n, 0)),
                  pl.BlockSpec((None, LANES, C), lambda n, h: (n, 0, 0)),
                  _fixed((1, hd)), *[ANY] * n_ci],
        out_specs=[blk(0), blk(0),
                   pl.BlockSpec((None, HB, hd, hd), lambda n, h: (n, h, 0, 0)),
                   pl.BlockSpec((None, HB, C, C), lambda n, h: (n, h, 0, 0)), *[ANY] * n_co],
        out_shape=[jax.ShapeDtypeStruct((T, 2 * H * hd), BF),
                   jax.ShapeDtypeStruct((T, H * hd), F32),
                   jax.ShapeDtypeStruct((N, H, hd, hd), F32), jax.ShapeDtypeStruct((N, H, C, C), F32),
                   *(comm.outs if comm else [])],
        scratch_shapes=scratch_shapes,
        compiler_params=_cp(2))(q, k, v, pm, bg, gcrow, wn, *(comm.ins if comm else []))
    return (*res[:4], list(res[4:])) if comm else tuple(res)


def gdn_bwd(q, k, v, pm, zoff, bg, gcrow, wn, oraw, sall, tall, dog, H, *, name):
    T = q.shape[0]
    C = LA_CHUNK
    N = T // C
    hd = HEAD_DIM

    HB = _heads_per_step(H)

    def body(*refs):
        dbg_ref, dwn_ref, dS = refs[15], refs[16], refs[17]
        n = pl.program_id(0)
        hg = pl.program_id(1)

        @pl.when((n == 0) & (hg == 0))
        def _():
            dwn_ref[...] = jnp.zeros_like(dwn_ref)
            dS[...] = jnp.zeros_like(dS)

        @pl.when(hg == 0)
        def _():
            dbg_ref[...] = jnp.zeros_like(dbg_ref)

        ds_in = [dS[hg * HB + i] for i in range(HB)]
        outs = _interleave([head(i, hg * HB + i, ds_in[i], *refs) for i in range(HB)])
        for i in range(HB):
            dS[hg * HB + i] = outs[i][0]
        dwn_ref[...] += sum(o[1] for o in outs)
        dbg_ref[...] += sum(o[2] for o in outs)

    def head(i, h, ds1, q_ref, k_ref, v_ref, z_ref, bg_ref, gr_ref, wn_ref, or_ref, sall_ref, tall_ref, dog_ref,
             dq_ref, dk_ref, dv_ref, dz_ref, dbg_ref, dwn_ref, dS):
        sl = slice(i * hd, (i + 1) * hd)
        beta, gc, grow, gl = _gdn_gates(bg_ref, gr_ref, h, H)
        causal, strict, eye = _chunk_masks()
        dm = _decay(gc, grow, causal)
        qs = q_ref[:, sl] * QK_SCALE
        kk = k_ref[:, sl]
        vv = v_ref[:, sl]
        zz = z_ref[:, sl]
        o = or_ref[:, sl]
        dog = dog_ref[:, sl]
        wn_v = wn_ref[...]
        s0 = sall_ref[i]
        tm = tall_ref[i]

        rstd = _head_rstd(o)
        on = o * rstd
        sz = _silu(zz)
        don = dog * wn_v * sz
        dwn_part = jnp.sum(dog * on * sz, axis=0, keepdims=True)
        dz_ref[:, sl] = (dog * on * wn_v * _dsilu(zz)).astype(dz_ref.dtype)
        do = rstd * (don - on * jnp.mean(don * on, axis=-1, keepdims=True))

        eg = jnp.exp(gc)
        kb = kk * beta
        vb = vv * beta
        kbg = kb * eg
        a = jnp.where(strict, _dot(kb, kk, NT) * dm, 0.0)
        u = _dot(tm, vb)
        w = _dot(tm, kbg)
        qk = jnp.where(causal, _dot(qs, kk, NT) * dm, 0.0)
        dqdec = _dot(do, s0, NT)
        yield
        vnew = u - _dot(w, s0)
        qdec = qs * eg
        etail = jnp.exp(gl - gc)
        ktail = kk * etail
        egl = jnp.exp(gl)
        dvnew = _dot(qk, do, TN) + _dot(ktail, ds1)
        yield
        dqk = jnp.where(causal, _dot(do, vnew, NT), 0.0)
        dktail = _dot(vnew, ds1, NT)
        dcd = jnp.sum(jnp.sum(s0 * ds1, axis=1, keepdims=True), axis=0, keepdims=True)
        ds0 = egl * ds1 + _dot(qdec, do, TN) - _dot(w, dvnew, TN)
        dw = -_dot(dvnew, s0, NT)
        dvb = _dot(tm, dvnew, TN)
        yield
        dkbg = _dot(tm, dw, TN)
        dtm = _dot(dvnew, vb, NT) + _dot(dw, kbg, NT)
        dqkr = dqk * dm
        dqs = _dot(dqkr, kk) + dqdec * eg
        yield
        x = _dot_hi(tm, dtm, TN)
        yield
        da = jnp.where(strict, -_dot_hi(x, tm, NT), 0.0)
        yield
        dkk = da * dm
        dkb = _dot(dkk, kk) + dkbg * eg
        dk = _dot(dkk, kb, TN)
        dk = dk + _dot(dqkr, qs, TN) + dktail * etail + dkb * beta
        g = da * a + dqk * qk
        colsum = jnp.max(_dot_hi(g, jnp.ones((C, LANES), F32), TN), axis=1, keepdims=True)
        yield
        rk = jnp.sum(dktail * ktail, axis=1, keepdims=True)
        dgc = (jnp.sum(g, axis=1, keepdims=True) - colsum
               + jnp.sum(dqdec * qdec, axis=1, keepdims=True) - rk
               + jnp.sum(dkbg * kbg, axis=1, keepdims=True))
        dgl = jnp.sum(rk, axis=0, keepdims=True) + dcd * egl
        ri = lax.broadcasted_iota(jnp.int32, (C, 1), 0)
        dgc = dgc + jnp.where(ri == C - 1, dgl, 0.0)
        dbeta = jnp.sum(dkb * kk, axis=1, keepdims=True) + jnp.sum(dvb * vv, axis=1, keepdims=True)

        dq_ref[:, sl] = dqs * QK_SCALE
        dk_ref[:, sl] = dk
        dv_ref[:, sl] = dvb * beta
        lane = lax.broadcasted_iota(jnp.int32, (C, LANES), 1)
        return ds0, dwn_part, jnp.where(lane == h, dbeta, 0.0) + jnp.where(lane == H + h, dgc, 0.0)

    blk = lambda off: pl.BlockSpec((C, HB * hd), lambda n, h: (N - 1 - n, off // HB + h))
    st = lambda r: pl.BlockSpec((None, HB, r, r), lambda n, h: (N - 1 - n, h, 0, 0))
    return _pcall(
        body, name=name, grid=(N, H // HB),
        in_specs=[blk(0), blk(0), blk(0), blk(zoff),
                  pl.BlockSpec((C, LANES), lambda n, h: (N - 1 - n, 0)),
                  pl.BlockSpec((None, LANES, C), lambda n, h: (N - 1 - n, 0, 0)),
                  _fixed((1, hd)), blk(0), st(hd), st(C), blk(0)],
        out_specs=[blk(0), blk(0), blk(0), blk(0),
                   pl.BlockSpec((C, LANES), lambda n, h: (N - 1 - n, 0)), _fixed((1, hd))],
        out_shape=[jax.ShapeDtypeStruct((T, H * hd), F32)] * 3
        + [jax.ShapeDtypeStruct((T, H * hd), BF), jax.ShapeDtypeStruct((T, LANES), F32),
           jax.ShapeDtypeStruct((1, hd), F32)],
        scratch_shapes=[pltpu.VMEM((H, hd, hd), F32)],
        compiler_params=_cp(2))(q, k, v, pm, bg, gcrow, wn, oraw, sall, tall, dog)


def _rot(x, cs, sn):
    return x * cs + pltpu.roll(x, HEAD_DIM // 2, 1) * sn


def _rot_t(dy, cs, sn):
    return dy * cs + pltpu.roll(dy * sn, HEAD_DIM // 2, 1)


def ret_fwd(pm, qoff, koff, voff, goff, cs, sn, dmat, avec, bvec, gam, og_buf, H, *, name):
    T = pm.shape[0]
    C = LA_CHUNK
    N = T // C
    hd = HEAD_DIM

    HB = _heads_per_step(H)

    def body(q_ref, k_ref, v_ref, g_ref, cs_ref, sn_ref, dm_ref, a_ref, b_ref, gam_ref, _og_in,
             og_ref, or_ref, sall_ref, S):
        n = pl.program_id(0)
        hg = pl.program_id(1)
        c, s = cs_ref[...], sn_ref[...]

        @pl.when((n == 0) & (hg == 0))
        def _():
            S[...] = jnp.zeros_like(S)

        states = [S[hg * HB + i] for i in range(HB)]

        def head(i):
            sl = slice(i * hd, (i + 1) * hd)
            qq = _rot(q_ref[:, sl], c, s)
            kk = _rot(k_ref[:, sl], c, s) * QK_SCALE
            vv = v_ref[:, sl]
            s0 = states[i]
            p = _dot(qq, kk, NT) * dm_ref[i]
            cross = _dot(qq * a_ref[i], s0)
            s1 = s0 * gam_ref[i] + _dot(kk * b_ref[i], vv, TN)
            yield
            o = _dot(p, vv) + cross
            yield
            sall_ref[i] = s0
            or_ref[:, sl] = o
            og_ref[:, sl] = (_silu(g_ref[:, sl]) * o * _head_rstd(o)).astype(og_ref.dtype)
            return s1

        for i, s1 in enumerate(_interleave([head(i) for i in range(HB)])):
            S[hg * HB + i] = s1

    blk = lambda off: pl.BlockSpec((C, HB * hd), lambda n, h: (n, off // HB + h))
    tab = pl.BlockSpec((C, hd), lambda n, h: (n, 0))
    per_h = lambda r, cdim: pl.BlockSpec((HB, r, cdim), lambda n, h: (h, 0, 0))
    return _pcall(
        body, name=name, grid=(N, H // HB),
        in_specs=[blk(qoff), blk(koff), blk(voff), blk(goff), tab, tab,
                  per_h(C, C), per_h(C, hd), per_h(C, hd), per_h(1, hd), ANY],
        out_specs=[blk(H), blk(0), pl.BlockSpec((None, HB, hd, hd), lambda n, h: (n, h, 0, 0))],
        out_shape=[jax.ShapeDtypeStruct(og_buf.shape, og_buf.dtype), jax.ShapeDtypeStruct((T, H * hd), F32),
                   jax.ShapeDtypeStruct((N, H, hd, hd), F32)],
        input_output_aliases={10: 0},
        scratch_shapes=[pltpu.VMEM((H, hd, hd), F32)],
        compiler_params=_cp(2))(pm, pm, pm, pm, cs, sn, dmat, avec, bvec, gam, og_buf)


def ret_bwd(pm, qoff, koff, voff, goff, cs, sn, dmat, avec, bvec, gam, oraw, sall, dog, dogoff, H, *, name):
    T = pm.shape[0]
    C = LA_CHUNK
    N = T // C
    hd = HEAD_DIM

    HB = _heads_per_step(H)

    def body(q_ref, k_ref, v_ref, g_ref, cs_ref, sn_ref, dm_ref, a_ref, b_ref, gam_ref, or_ref, sall_ref,
             dog_ref, dq_ref, dk_ref, dv_ref, dg_ref, dS):
        n = pl.program_id(0)
        hg = pl.program_id(1)
        c, s = cs_ref[...], sn_ref[...]

        @pl.when((n == 0) & (hg == 0))
        def _():
            dS[...] = jnp.zeros_like(dS)

        dstates = [dS[hg * HB + i] for i in range(HB)]

        def head(i):
            sl = slice(i * hd, (i + 1) * hd)
            qq = _rot(q_ref[:, sl], c, s)
            kk = _rot(k_ref[:, sl], c, s) * QK_SCALE
            vv = v_ref[:, sl]
            gg = g_ref[:, sl]
            o = or_ref[:, sl]
            dog = dog_ref[:, sl]
            dm = dm_ref[i]
            av, bv = a_ref[i], b_ref[i]
            s0 = sall_ref[i]
            ds1 = dstates[i]

            rstd = _head_rstd(o)
            on = o * rstd
            don = dog * _silu(gg)
            dg_ref[:, sl] = (dog * on * _dsilu(gg)).astype(dg_ref.dtype)
            do = rstd * (don - on * jnp.mean(don * on, axis=-1, keepdims=True))

            p = _dot(qq, kk, NT) * dm
            dp = _dot(do, vv, NT) * dm
            cross_q = _dot(do, s0, NT) * av
            cross_k = _dot(vv, ds1, NT) * bv
            cross_v = _dot(kk * bv, ds1)
            ds0 = ds1 * gam_ref[i] + _dot(qq * av, do, TN)
            yield
            dv_ref[:, sl] = (_dot(p, do, TN) + cross_v).astype(dv_ref.dtype)
            dqq = _dot(dp, kk) + cross_q
            dkk = (_dot(dp, qq, TN) + cross_k) * QK_SCALE
            yield
            dq_ref[:, sl] = _rot_t(dqq, c, s).astype(dq_ref.dtype)
            dk_ref[:, sl] = _rot_t(dkk, c, s).astype(dk_ref.dtype)
            return ds0

        for i, ds0 in enumerate(_interleave([head(i) for i in range(HB)])):
            dS[hg * HB + i] = ds0

    blk = lambda off: pl.BlockSpec((C, HB * hd), lambda n, h: (N - 1 - n, off // HB + h))
    tab = pl.BlockSpec((C, hd), lambda n, h: (N - 1 - n, 0))
    per_h = lambda r, cdim: pl.BlockSpec((HB, r, cdim), lambda n, h: (h, 0, 0))
    return _pcall(
        body, name=name, grid=(N, H // HB),
        in_specs=[blk(qoff), blk(koff), blk(voff), blk(goff), tab, tab,
                  per_h(C, C), per_h(C, hd), per_h(C, hd), per_h(1, hd), blk(0),
                  pl.BlockSpec((None, HB, hd, hd), lambda n, h: (N - 1 - n, h, 0, 0)), blk(dogoff)],
        out_specs=[blk(0)] * 4,
        out_shape=[jax.ShapeDtypeStruct((T, H * hd), BF)] * 4,
        scratch_shapes=[pltpu.VMEM((H, hd, hd), F32)],
        compiler_params=_cp(2))(pm, pm, pm, pm, cs, sn, dmat, avec, bvec, gam, oraw, sall, dog)


LN_ROWS = 128


def ln_fwd(pre, lw, lb, *, name):
    T, W2 = pre.shape
    W = W2 // 2
    tr = _tile(T, LN_ROWS, SUBLANES)

    def body(p_ref, w_ref, b_ref, o_ref):
        v = _gelu(p_ref[...])
        xc = v - jnp.mean(v, axis=-1, keepdims=True)
        r = lax.rsqrt(jnp.mean(xc * xc, axis=-1, keepdims=True) + EPS)
        o_ref[...] = xc * r * w_ref[...] + b_ref[...]

    return _pcall(body, name=name, grid=(T // tr,),
                  in_specs=[pl.BlockSpec((tr, W), lambda i: (i, 1)), _fixed((1, W)), _fixed((1, W))],
                  out_specs=_rows(tr, W), out_shape=jax.ShapeDtypeStruct((T, W), F32),
                  compiler_params=_cp(1))(pre, lw, lb)


def ln_bwd(pre, lw, dvn, dpre_buf, *, name):
    T, W2 = pre.shape
    W = W2 // 2
    tr = _tile(T, LN_ROWS, SUBLANES)

    def body(p_ref, w_ref, d_ref, _dp_in, dp_ref, dw_ref, db_ref):
        i = pl.program_id(0)
        v, dgelu = _gelu_and_grad(p_ref[...])
        xc = v - jnp.mean(v, axis=-1, keepdims=True)
        r = lax.rsqrt(jnp.mean(xc * xc, axis=-1, keepdims=True) + EPS)
        xh = xc * r
        d = d_ref[...]
        dxh = d * w_ref[...]
        dv = r * (dxh - jnp.mean(dxh, axis=-1, keepdims=True) - xh * jnp.mean(dxh * xh, axis=-1, keepdims=True))
        dp_ref[...] = (dv * dgelu).astype(dp_ref.dtype)
        pw = jnp.sum(d * xh, axis=0, keepdims=True)
        pb = jnp.sum(d, axis=0, keepdims=True)

        @pl.when(i == 0)
        def _():
            dw_ref[...] = pw
            db_ref[...] = pb

        @pl.when(i > 0)
        def _():
            dw_ref[...] += pw
            db_ref[...] += pb

    return _pcall(body, name=name, grid=(T // tr,),
                  in_specs=[pl.BlockSpec((tr, W), lambda i: (i, 1)), _fixed((1, W)), _rows(tr, W), ANY],
                  out_specs=[pl.BlockSpec((tr, W), lambda i: (i, 1)), _fixed((1, W)), _fixed((1, W))],
                  out_shape=[jax.ShapeDtypeStruct(dpre_buf.shape, dpre_buf.dtype), jax.ShapeDtypeStruct((1, W), F32),
                             jax.ShapeDtypeStruct((1, W), F32)],
                  input_output_aliases={3: 0},
                  compiler_params=_cp(1))(pre, lw, dvn, dpre_buf)


def _tril_mask(c):
    t = lax.broadcasted_iota(jnp.int32, (c, c), 0)
    s = lax.broadcasted_iota(jnp.int32, (c, c), 1)
    return t >= s


def sg_fwd(pre, vn, ws, bs3, *, name):
    T, W = vn.shape
    G = ws.shape[0]
    gd = W // G
    C = SG_CHUNK

    def body(p_ref, v_ref, w_ref, b_ref, o_ref):
        mask = _tril_mask(C)
        for g in range(G):
            sl = slice(g * gd, (g + 1) * gd)
            wm = jnp.where(mask, w_ref[g], 0.0)
            s = _dot(wm, v_ref[:, sl]) + b_ref[g]
            o_ref[:, sl] = (_gelu(p_ref[:, sl]) * s).astype(o_ref.dtype)

    blk = pl.BlockSpec((C, W), lambda n: (n, 0))
    return _pcall(body, name=name, grid=(T // C,),
                  in_specs=[blk, blk, _fixed((G, C, C)), _fixed((G, C, 1))],
                  out_specs=blk, out_shape=jax.ShapeDtypeStruct((T, W), BF),
                  compiler_params=_cp(1))(pre, vn, ws, bs3)


def sg_bwd(pre, vn, ws, bs3, dus, *, name):
    T, W = vn.shape
    G = ws.shape[0]
    gd = W // G
    C = SG_CHUNK

    def body(p_ref, v_ref, w_ref, b_ref, d_ref, dp_ref, dv_ref, dw_ref, db_ref):
        n = pl.program_id(0)
        mask = _tril_mask(C)

        @pl.when(n == 0)
        def _():
            dw_ref[...] = jnp.zeros_like(dw_ref)
            db_ref[...] = jnp.zeros_like(db_ref)

        for g in range(G):
            sl = slice(g * gd, (g + 1) * gd)
            wm = jnp.where(mask, w_ref[g], 0.0)
            u, du = _gelu_and_grad(p_ref[:, sl])
            vv = v_ref[:, sl]
            d = d_ref[:, sl]
            s = _dot(wm, vv) + b_ref[g]
            ds = d * u
            dp_ref[:, sl] = (d * s * du).astype(dp_ref.dtype)
            dv_ref[:, sl] = _dot(wm, ds, TN)
            dw_ref[g] += jnp.where(mask, _dot(ds, vv, NT), 0.0)
            db_ref[g] += jnp.sum(ds, axis=1, keepdims=True)

    blk = pl.BlockSpec((C, W), lambda n: (n, 0))
    return _pcall(body, name=name, grid=(T // C,),
                  in_specs=[blk, blk, _fixed((G, C, C)), _fixed((G, C, 1)), blk],
                  out_specs=[blk, blk, _fixed((G, C, C)), _fixed((G, C, 1))],
                  out_shape=[jax.ShapeDtypeStruct((T, 2 * W), BF),
                             jax.ShapeDtypeStruct((T, W), F32),
                             jax.ShapeDtypeStruct((G, C, C), F32), jax.ShapeDtypeStruct((G, C, 1), F32)],
                  compiler_params=_cp(1))(pre, vn, ws, bs3, dus)


CHIP_RELATIONS = ((1, 0), (0, 1), (1, 1))


def _place():
    return lax.axis_index("x"), lax.axis_index("y"), lax.axis_index("c")


def _peer_chip(x, y, r):
    fx, fy = CHIP_RELATIONS[r]
    return (1 - x if fx else x), (1 - y if fy else y)


def gather_comm(arrs):
    n = len(arrs)
    per = 2 * len(CHIP_RELATIONS) + 1
    own = per - 1

    def ici(a, r, ins, outs, send, recv):
        x, y, c = _place()
        px, py = _peer_chip(x, y, r)
        return pltpu.make_async_remote_copy(
            src_ref=ins[a].at[c], dst_ref=outs[a].at[2 * x + y, c], send_sem=send.at[a * per + r],
            recv_sem=recv.at[a * per + r], device_id=(px, py, c), device_id_type=MESH)

    def own_block(a, ins, outs, send, recv):
        x, y, c = _place()
        return pltpu.make_async_remote_copy(
            src_ref=ins[a], dst_ref=outs[a].at[2 * x + y], send_sem=send.at[a * per + own],
            recv_sem=recv.at[a * per + own], device_id=(x, y, 1 - c), device_id_type=MESH)

    def start(ins, outs, send, recv):
        for a in range(n):
            for r in range(3):
                ici(a, r, ins, outs, send, recv).start()
            own_block(a, ins, outs, send, recv).start()

    def forward(a, r, outs, send, recv):
        x, y, c = _place()
        px, py = _peer_chip(x, y, r)
        landed = outs[a].at[2 * px + py, c]
        return pltpu.make_async_remote_copy(
            src_ref=landed, dst_ref=landed, send_sem=send.at[a * per + 3 + r],
            recv_sem=recv.at[a * per + 3 + r], device_id=(x, y, 1 - c), device_id_type=MESH)

    def middle(ins, outs, send, recv):
        x, y, c = _place()
        for a in range(n):
            for r in range(3):
                px, py = _peer_chip(x, y, r)
                landed = outs[a].at[2 * px + py, c]
                pltpu.make_async_remote_copy(
                    src_ref=landed, dst_ref=landed, send_sem=send.at[a * per + r],
                    recv_sem=recv.at[a * per + r], device_id=(px, py, c), device_id_type=MESH).wait_recv()
                forward(a, r, outs, send, recv).start()

    def finish(ins, outs, send, recv):
        x, y, c = _place()
        for a in range(n):
            for r in range(3):
                px, py = _peer_chip(x, y, r)
                other = outs[a].at[2 * px + py, 1 - c]
                pltpu.make_async_remote_copy(
                    src_ref=other, dst_ref=other, send_sem=send.at[a * per + 3 + r],
                    recv_sem=recv.at[a * per + 3 + r], device_id=(x, y, 1 - c), device_id_type=MESH).wait_recv()
        for a in range(n):
            for r in range(3):
                ici(a, r, ins, outs, send, recv).wait_send()
                forward(a, r, outs, send, recv).wait_send()
            own_block(a, ins, outs, send, recv).wait()

    outs = [jax.ShapeDtypeStruct((N_CHIPS,) + a.shape, a.dtype) for a in arrs]
    return Comm(list(arrs), outs, n * per, start, finish, middle)


def chip_exchange_comm(ps):
    n = len(ps)

    def copies(ins, outs, send, recv):
        x, y, c = _place()
        cps = []
        for a in range(n):
            for r in range(3):
                px, py = _peer_chip(x, y, r)
                cps.append(pltpu.make_async_remote_copy(
                    src_ref=ins[a].at[2 * px + py], dst_ref=outs[a].at[r], send_sem=send.at[3 * a + r],
                    recv_sem=recv.at[3 * a + r], device_id=(px, py, c), device_id_type=MESH))
        return cps

    def start(ins, outs, send, recv):
        for cp in copies(ins, outs, send, recv):
            cp.start()

    def finish(ins, outs, send, recv):
        for cp in copies(ins, outs, send, recv):
            cp.wait()

    outs = [jax.ShapeDtypeStruct((3,) + p.shape[1:], p.dtype) for p in ps]
    return Comm(list(ps), outs, 3 * n, start, finish)


def run_comm(comm, *, name):
    n_i, n_o = len(comm.ins), len(comm.outs)

    def body(*refs):
        ins, outs = refs[:n_i], refs[n_i:n_i + n_o]
        send, recv = refs[n_i + n_o:]
        comm.start(ins, outs, send, recv)
        if comm.middle is not None:
            comm.middle(ins, outs, send, recv)
        comm.finish(ins, outs, send, recv)

    res = _pcall(body, name=name, in_specs=[ANY] * n_i, out_specs=[ANY] * n_o, out_shape=comm.outs,
                 scratch_shapes=[pltpu.SemaphoreType.DMA((comm.nsem,)), pltpu.SemaphoreType.DMA((comm.nsem,))])(*comm.ins)
    return list(res)


def pair_exchange(gs, *, name):
    n = len(gs)

    def body(*refs):
        ins, outs = refs[:n], refs[n:2 * n]
        send, recv = refs[2 * n:2 * n + 2]
        x, y, c = _place()
        cps = []
        for a in range(n):
            cp = pltpu.make_async_remote_copy(
                src_ref=ins[a].at[:, pl.ds(1 - c, 1)], dst_ref=outs[a], send_sem=send.at[a], recv_sem=recv.at[a],
                device_id=(x, y, 1 - c), device_id_type=MESH)
            cp.start()
            cps.append(cp)
        for cp in cps:
            cp.wait()

    out_shape = [jax.ShapeDtypeStruct((g.shape[0], 1) + g.shape[2:], g.dtype) for g in gs]
    res = _pcall(body, name=name, in_specs=[ANY] * n, out_specs=[ANY] * n, out_shape=out_shape,
                 scratch_shapes=[pltpu.SemaphoreType.DMA((n,)), pltpu.SemaphoreType.DMA((n,))])(*gs)
    return list(res)


def pair_share_comm(fs):
    n = len(fs)

    def copies(ins, outs, send, recv):
        x, y, c = _place()
        return [pltpu.make_async_remote_copy(
            src_ref=ins[a], dst_ref=outs[a], send_sem=send.at[a], recv_sem=recv.at[a],
            device_id=(x, y, 1 - c), device_id_type=MESH) for a in range(n)]

    def start(ins, outs, send, recv):
        for cp in copies(ins, outs, send, recv):
            cp.start()

    def finish(ins, outs, send, recv):
        for cp in copies(ins, outs, send, recv):
            cp.wait()

    return Comm(list(fs), [jax.ShapeDtypeStruct(f.shape, f.dtype) for f in fs], n, start, finish)


def all_reduce_small(v, *, name):
    R = v.shape[0]

    def body(v_ref, sum_ref, gat_ref, send, recv):
        x, y, c = _place()
        me = 4 * x + 2 * y + c
        gat_ref[me] = v_ref[...]
        cps = []
        peers = []
        for r in range(1, N_DEV):
            fx, fy, fc = (r >> 2) & 1, (r >> 1) & 1, r & 1
            px, py, pc = (1 - x if fx else x), (1 - y if fy else y), (1 - c if fc else c)
            peers.append((px, py, pc))
            cp = pltpu.make_async_remote_copy(
                src_ref=v_ref, dst_ref=gat_ref.at[me], send_sem=send.at[r - 1], recv_sem=recv.at[r - 1],
                device_id=(px, py, pc), device_id_type=MESH)
            cp.start()
            cps.append(cp)
        for r in range(1, N_DEV):
            px, py, pc = peers[r - 1]
            slot = gat_ref.at[4 * px + 2 * py + pc]
            pltpu.make_async_remote_copy(
                src_ref=v_ref, dst_ref=slot, send_sem=send.at[r - 1], recv_sem=recv.at[r - 1],
                device_id=(px, py, pc), device_id_type=MESH).wait_recv()
        for cp in cps:
            cp.wait_send()
        acc = gat_ref[0]
        for s in range(1, N_DEV):
            acc = acc + gat_ref[s]
        sum_ref[...] = acc

    vm = pl.BlockSpec(memory_space=pltpu.VMEM)
    res = _pcall(body, name=name, in_specs=[vm], out_specs=[vm, vm],
                 out_shape=[jax.ShapeDtypeStruct((R, LANES), F32), jax.ShapeDtypeStruct((N_DEV, R, LANES), F32)],
                 scratch_shapes=[pltpu.SemaphoreType.DMA((N_DEV - 1,)), pltpu.SemaphoreType.DMA((N_DEV - 1,))],
                 compiler_params=pltpu.CompilerParams(vmem_limit_bytes=VMEM_LIMIT))(v)
    return res[0]


def pair_sum(g, r1, c_idx, *, name):
    nb, _, hr, C = g.shape
    tr = _tile(hr, max(BF16_ROWS, (1 << 18) // C), BF16_ROWS)

    def body(c_ref, g_ref, r_ref, pb_ref):
        pb_ref[...] = (g_ref[...] + r_ref[...].astype(F32)).astype(pb_ref.dtype)

    gs = pltpu.PrefetchScalarGridSpec(
        num_scalar_prefetch=1, grid=(nb, hr // tr),
        in_specs=[pl.BlockSpec((None, None, tr, C), lambda b, i, cr: (b, cr[0], i, 0)),
                  pl.BlockSpec((None, None, tr, C), lambda b, i, cr: (b, 0, i, 0))],
        out_specs=pl.BlockSpec((None, tr, C), lambda b, i, cr: (b, i, 0)))
    return _pcall(body, name=name, grid_spec=gs, out_shape=jax.ShapeDtypeStruct((nb, hr, C), BF),
                  compiler_params=_cp(2))(c_idx, g, r1)


def chip_sum(g, r1, r2, c_idx, j_idx, *, name, layer=0, n_layers=1, into=None):
    _, _, hr, C = g.shape
    tr = _tile(hr, max(BF16_ROWS, (1 << 18) // C), BF16_ROWS)

    def body(c_ref, j_ref, g_ref, s_ref, a_ref, b_ref, d_ref, *rest):
        o_ref = rest[-1]
        own = g_ref[...] + s_ref[...].astype(F32)
        o_ref[...] = ((own + a_ref[...].astype(F32)) + b_ref[...].astype(F32)) + d_ref[...].astype(F32)

    rel = lambda r: pl.BlockSpec((None, tr, C), lambda i, cr, jr: (r, i, 0))
    in_specs = [pl.BlockSpec((None, None, tr, C), lambda i, cr, jr: (jr[0], cr[0], i, 0)),
                pl.BlockSpec((None, None, tr, C), lambda i, cr, jr: (jr[0], 0, i, 0)), rel(0), rel(1), rel(2)]
    operands = [c_idx, j_idx, g, r1, r2, r2, r2]
    aliases = {}
    if into is not None:
        in_specs.append(ANY)
        operands.append(into)
        aliases = {len(operands) - 1: 0}
    gs = pltpu.PrefetchScalarGridSpec(
        num_scalar_prefetch=2, grid=(hr // tr,), in_specs=in_specs,
        out_specs=pl.BlockSpec((None, tr, C), lambda i, cr, jr: (layer, i, 0)))
    return _pcall(body, name=name, grid_spec=gs, out_shape=jax.ShapeDtypeStruct((n_layers, hr, C), F32),
                  input_output_aliases=aliases, compiler_params=_cp(1))(*operands)


def adamw_halves(w, g_mine, g_other, m, v, c_idx, *, name):
    L, R, C = w.shape
    hr = R // 2
    tr = _tile(hr, max(SUBLANES, (1 << 18) // C), SUBLANES)
    nbh = hr // tr
    c1 = 1.0 - ADAM_B1 ** ADAM_STEP
    c2 = 1.0 - ADAM_B2 ** ADAM_STEP

    def body(c_ref, w_ref, gm_ref, go_ref, m_ref, v_ref, g_ref, d_ref, mo_ref, vo_ref):
        i = pl.program_id(1)
        gv = jnp.where(i // nbh == c_ref[0], gm_ref[...], go_ref[...])
        m2 = ADAM_B1 * m_ref[...] + (1.0 - ADAM_B1) * gv
        v2 = ADAM_B2 * v_ref[...] + (1.0 - ADAM_B2) * (gv * gv)
        d_ref[...] = -ADAM_LR * ((m2 / c1) / (jnp.sqrt(v2 / c2) + ADAM_EPS) + ADAM_WD * w_ref[...])
        g_ref[...] = gv
        mo_ref[...] = m2
        vo_ref[...] = v2

    full = pl.BlockSpec((None, tr, C), lambda l, i, cr: (l, i, 0))
    half = pl.BlockSpec((None, tr, C), lambda l, i, cr: (l, i % nbh, 0))
    gs = pltpu.PrefetchScalarGridSpec(
        num_scalar_prefetch=1, grid=(L, R // tr),
        in_specs=[full, half, half, full, full], out_specs=[full] * 4)
    sds = jax.ShapeDtypeStruct((L, R, C), F32)
    return _pcall(body, name=name, grid_spec=gs, out_shape=[sds] * 4,
                  compiler_params=_cp(2))(c_idx, w, g_mine, g_other, m, v)


def _pack_rows(arrs):
    parts = []
    for a in arrs:
        flat = a.reshape(-1).astype(F32)
        tile = SUBLANES * LANES
        pad = (-flat.shape[0]) % tile
        parts.append(jnp.pad(flat, (0, pad)).reshape(-1, LANES))
    return jnp.concatenate(parts, axis=0)


def _unpack_rows(buf, shapes):
    out, row = [], 0
    for shp in shapes:
        size = int(np.prod(shp))
        rows = -(-size // (SUBLANES * LANES)) * SUBLANES
        out.append(buf[row:row + rows].reshape(-1)[:size].reshape(shp))
        row += rows
    return out


def _halves(a2d):
    r, c = a2d.shape
    return a2d.reshape(2, r // 2, c)


def _rotary_tables(T):
    half = HEAD_DIM // 2
    pos = jnp.arange(T, dtype=F32)
    inv_freq = 1.0 / (ROPE_BASE ** jnp.linspace(0.0, 1.0, half, dtype=F32))
    ang = pos[:, None] * inv_freq[None, :]
    cos, sin = jnp.cos(ang), jnp.sin(ang)
    return jnp.concatenate([cos, cos], axis=1), jnp.concatenate([-sin, sin], axis=1)


def _retention_tables(H):
    C = LA_CHUNK
    lg = jnp.log1p(-jnp.power(2.0, -5.0 - jnp.arange(H, dtype=F32)))
    pos = jnp.arange(C, dtype=F32)
    causal = jnp.tril(jnp.ones((C, C), dtype=bool))
    dmat = jnp.exp(jnp.where(causal, (pos[:, None] - pos[None, :]) * lg[:, None, None], -jnp.inf))
    bc = lambda t: jnp.broadcast_to(t[..., None], t.shape + (HEAD_DIM,))
    avec = bc(jnp.exp((pos + 1.0)[None, :] * lg[:, None]))
    bvec = bc(jnp.exp((C - 1.0 - pos)[None, :] * lg[:, None]))
    gam = bc(jnp.exp(C * lg)[:, None])
    return dmat, avec, bvec, gam


def _relu2(acc):
    return acc, jnp.square(jnp.maximum(acc, 0.0))


def _drelu2(acc, up):
    return (acc * (2.0 * jnp.maximum(up, 0.0)),)


ROW_SHARDED = ("la_out", "sg_out", "ffn_down0", "ffn_down1")


class ExchangePlan:
    GATHERS = {"la_in_main": ("la_out", "ffn_up0"), "gdn_fwd": ("ffn_down0",), "ffn_up_0": ("sg_in",),
               "ffn_down_0": ("sg_out", "ffn_up1"), "sg_in": ("ffn_down1",)}
    REDUCES = {"ffn_dup_1": "ffn_down1", "ffn_dy_1": "ffn_up1", "sg_dus": "sg_out", "sg_dy": "sg_in",
               "ffn_dup_0": "ffn_down0", "ffn_dy_0": "ffn_up0", "la_docat": "la_out", "la_dy": "la_in"}
    SHARE = "dnorm00"
    SHARED = ("la_in", "la_out", "sg_in", "sg_out", "ffn_up", "ffn_down")

    def __init__(self, shard_halves, c_idx, j_idx):
        self.shard_halves, self.c_idx, self.j_idx = shard_halves, c_idx, j_idx
        self.partial = {}
        self.finished = {}

    def comm(self, carrier):
        if carrier in self.GATHERS:
            return gather_comm([self.shard_halves[w] for w in self.GATHERS[carrier]])
        if carrier in self.REDUCES:
            return chip_exchange_comm([self.partial[self.REDUCES[carrier]][2]])
        if carrier == self.SHARE:
            return pair_share_comm([self.finished[k] for k in self.SHARED])
        return None

    def done(self, carrier, outs, W):
        if carrier in self.GATHERS:
            for w, g in zip(self.GATHERS[carrier], outs):
                install_gathered(W, w, g)
        elif carrier == self.SHARE:
            self.from_sibling = dict(zip(self.SHARED, outs))
        else:
            w = self.REDUCES[carrier]
            per_layer = w[:-1] in ("ffn_up", "ffn_down")
            key, layer, n_layers = (w[:-1], int(w[-1]), 2) if per_layer else (w, 0, 1)
            g, sib, _ = self.partial[w]
            self.finished[key] = chip_sum(g, sib, outs[0], self.c_idx, self.j_idx, name=f"grads_chip_sum_{w}",
                                          layer=layer, n_layers=n_layers, into=self.finished.get(key))

    def grad_ready(self, w, g, payload=None):
        halves = lambda t: t.reshape(N_CHIPS, 2, t.shape[1] // 2, t.shape[2])
        sib = pair_exchange([halves(g if payload is None else payload)], name=f"grads_pair_exchange_{w}")[0]
        self.partial[w] = (halves(g), sib, pair_sum(halves(g), sib, self.c_idx, name=f"grads_pair_sum_{w}"))


def install_gathered(W, w, g):
    whole = g.reshape(N_CHIPS, g.shape[1] * g.shape[2], g.shape[3])
    if w in ROW_SHARDED:
        whole = whole.reshape(-1, whole.shape[-1])
    if w[:-1] in ("ffn_up", "ffn_down"):
        W[w[:-1]][int(w[-1])] = whole
    else:
        W[w] = whole


def _by_chip(w, g):
    return g.reshape(N_CHIPS, -1, g.shape[-1]) if w in ROW_SHARDED else g


def _la_shard_rows(H):
    cs = (8 * H * HEAD_DIM + 2 * H) // N_CHIPS
    return cs, -(-cs // (2 * BF16_ROWS)) * (2 * BF16_ROWS)


def _la_pieces(H):
    HD = H * HEAD_DIM
    mix = 8 * HD + 2 * H
    cs = mix // N_CHIPS
    segments = [(0, 0, 4 * HD, 0), (1, 4 * HD, 4 * HD + 2 * H, 0), (0, 4 * HD + 2 * H, mix, 4 * HD)]
    pieces = []
    for j in range(N_CHIPS):
        mine = []
        for src, a, b, base in segments:
            lo, hi = max(cs * j, a), min(cs * (j + 1), b)
            if lo < hi:
                mine.append((src, base + lo - a, base + hi - a))
        pieces.append(mine)
    return pieces


def _la_weights_from_gathered(g, H):
    _, csp, D = g.shape
    tc = _tile(D, 2 * LANES)
    n_main = 8 * H * HEAD_DIM

    def body(g_ref, main_ref, gate_ref):
        parts = {0: [], 1: []}
        for j, mine in enumerate(_la_pieces(H)):
            row = 0
            for src, a, b in mine:
                parts[src].append(g_ref[j, row:row + b - a, :])
                row += b - a
        main_ref[...] = jnp.concatenate(parts[0], axis=0)
        gate = jnp.concatenate(parts[1], axis=0)
        gate_ref[...] = jnp.concatenate([gate, jnp.zeros((LANES - gate.shape[0], tc), gate.dtype)], axis=0)

    return _pcall(body, name="la_weights", grid=(D // tc,),
                  in_specs=[pl.BlockSpec((N_CHIPS, csp, tc), lambda i: (0, 0, i))],
                  out_specs=[pl.BlockSpec((n_main, tc), lambda i: (0, i)), pl.BlockSpec((LANES, tc), lambda i: (0, i))],
                  out_shape=[jax.ShapeDtypeStruct((n_main, D), g.dtype), jax.ShapeDtypeStruct((LANES, D), g.dtype)],
                  compiler_params=_cp(1))(g)


def _la_dproj_by_chip(main_parts, dpg, H):
    HD = H * HEAD_DIM
    cs, csp = _la_shard_rows(H)
    T = dpg.shape[0]
    tr = _tile(T, ROW_TILE, BF16_ROWS)
    n = len(main_parts)

    def body(*refs):
        parts, g_ref, o_ref = refs[:n], refs[n], refs[n + 1]
        pad = jnp.zeros((tr, csp - cs), o_ref.dtype)
        cols = []
        for mine in _la_pieces(H):
            for src, a, b in mine:
                while src == 0 and a < b:
                    i, off = divmod(a, HD)
                    end = min(b, (i + 1) * HD)
                    cols.append(parts[i][:, off:off + end - a])
                    a = end
                if src == 1:
                    cols.append(g_ref[:, a:b])
            cols.append(pad)
        o_ref[...] = jnp.concatenate(cols, axis=1)

    return _pcall(body, name="la_dproj", grid=(T // tr,),
                  in_specs=[_rows(tr, HD)] * n + [_rows(tr, LANES)], out_specs=_rows(tr, N_CHIPS * csp),
                  out_shape=jax.ShapeDtypeStruct((T, N_CHIPS * csp), BF), compiler_params=_cp(1))(*main_parts, dpg)


def _train_local(x2, tgt, W, plan=None):
    T, D = x2.shape
    H = W["a_log"].shape[0]
    nw = W["norm_w"]
    row = lambda v: v.reshape(1, -1).astype(F32)
    G = {}

    def mm(fn, *args, name, **kw):
        comm = plan.comm(name) if plan is not None else None
        if comm is None:
            return fn(*args, name=name, **kw)
        res, outs = fn(*args, name=name, comm=comm, **kw)
        plan.done(name, outs, W)
        return res

    def grad(w, g):
        payload = None
        if isinstance(g, (list, tuple)):
            g, payload = g
        G[w] = g
        if plan is not None:
            plan.grad_ready(w, _by_chip(w, g), None if payload is None else _by_chip(w, payload))

    def ffn_fwd(y, l):
        up, act = mm(mm_nn, y, W["ffn_up"][l], name=f"ffn_up_{l}", out_dtypes=(F32, BF), epilogue=_relu2)
        dn = mm(mm_nn, act, W["ffn_down"][l], name=f"ffn_down_{l}")
        return up, act, dn

    def ffn_bwd(y, up, act, ddn, l):
        grad(f"ffn_down{l}", mm_tn(act, ddn, name=f"ffn_dwdown_{l}", bf16_copy=True))
        dup = mm(mm_nt, ddn, W["ffn_down"][l], name=f"ffn_dup_{l}", out_dtypes=(BF,), epilogue=_drelu2, extras=(up,))
        grad(f"ffn_up{l}", mm_tn(y, dup, name=f"ffn_dwup_{l}", shards=N_CHIPS, bf16_copy=True))
        return mm(mm_nt, dup, W["ffn_up"][l], name=f"ffn_dy_{l}")

    y0 = rms_fwd(x2, row(nw[0, 0]), name="norm00")
    pm = mm(mm_nt, y0, W["la_in_main"], name="la_in_main")
    pg = mm_nt(y0, W["la_in_gate"], name="la_in_gate")
    wc8 = jnp.pad(jnp.transpose(W["conv_w"]), ((0, SUBLANES - CONV_WIDTH), (0, 0)))
    lanes_pad = (H, LANES - 2 * H)
    arow = jnp.pad(W["a_log"], lanes_pad).reshape(1, LANES)
    dtrow = jnp.pad(W["dt_bias"], lanes_pad).reshape(1, LANES)
    bg, gcrow = gates_fwd(pg, arow, dtrow, H, name="gates_fwd")
    q = prep_fwd(pm, 0, wc8, 0, H, True, name="prep_q")
    k = prep_fwd(pm, H, wc8, H, H, True, name="prep_k")
    v = prep_fwd(pm, 2 * H, wc8, 2 * H, H, False, name="prep_v")
    wn = row(W["out_norm_w"])
    gdn_comm = plan.comm("gdn_fwd") if plan is not None else None
    og_a, or_a, sall_a, tall, *carried = gdn_fwd(q, k, v, pm, 3 * H, bg, gcrow, wn, H, name="gdn_fwd", comm=gdn_comm)
    if gdn_comm is not None:
        plan.done("gdn_fwd", carried[0], W)
    cs, sn = _rotary_tables(T)
    dmat, avec, bvec, gam = _retention_tables(H)
    ocat, or_b, sall_b = ret_fwd(pm, 4 * H, 5 * H, 6 * H, 7 * H, cs, sn, dmat, avec, bvec, gam, og_a, H,
                                 name="ret_fwd")
    mix = mm_nn(ocat, W["la_out"], name="la_out")
    h1, y2 = res_norm(x2, mix, row(nw[0, 1]), row(nw[0, 2]), name="resnorm_0a")
    up, act, dn = ffn_fwd(y2, 0)
    h2, y0b = res_norm(h1, dn, row(nw[0, 3]), row(nw[1, 0]), name="resnorm_0b")

    pre = mm(mm_nn, y0b, W["sg_in"], name="sg_in")
    lw, lb = row(W["ln_w"]), row(W["ln_b"])
    vn = ln_fwd(pre, lw, lb, name="sg_ln")
    ws = W["w_s"]
    bs3 = W["b_s"][:, :, None]
    us = sg_fwd(pre, vn, ws, bs3, name="sg_gate")
    mix1 = mm_nn(us, W["sg_out"], name="sg_out")
    h3, y2b = res_norm(h2, mix1, row(nw[1, 1]), row(nw[1, 2]), name="resnorm_1a")
    up1, act1, dn1 = ffn_fwd(y2b, 1)
    dnw = [[None] * 4 for _ in range(2)]
    dh4, ddn1, dnw[1][3], lrow = last_norm_and_loss(h3, dn1, row(nw[1, 3]), tgt, name="last_norm_and_loss")
    loss = lrow[0, 0]

    dy2b = ffn_bwd(y2b, up1, act1, ddn1, 1)
    dh3, dnw[1][2], dmix1, dnw[1][1] = rms_bwd(h3, row(nw[1, 2]), dy2b, dh4, name="dnorm_1a",
                                               inner=(mix1, row(nw[1, 1])))
    grad("sg_out", mm_tn(us, dmix1, name="sg_dwout", bf16_copy=True))
    dus = mm(mm_nt, dmix1, W["sg_out"], name="sg_dus")
    dpre_u, dvn, G["w_s"], dbs3 = sg_bwd(pre, vn, ws, bs3, dus, name="sg_gate_bwd")
    G["b_s"] = dbs3[:, :, 0]
    dpre, dlw, dlb = ln_bwd(pre, lw, dvn, dpre_u, name="sg_ln_bwd")
    G["ln_w"], G["ln_b"] = dlw[0], dlb[0]
    grad("sg_in", mm_tn(y0b, dpre, name="sg_dwin", shards=N_CHIPS, bf16_copy=True))
    dy0b = mm(mm_nt, dpre, W["sg_in"], name="sg_dy")

    dh2, dnw[1][0], ddn, dnw[0][3] = rms_bwd(h2, row(nw[1, 0]), dy0b, dh3, name="dnorm_0b",
                                             inner=(dn, row(nw[0, 3])))
    dy2 = ffn_bwd(y2, up, act, ddn, 0)
    dh1, dnw[0][2], dmix, dnw[0][1] = rms_bwd(h1, row(nw[0, 2]), dy2, dh2, name="dnorm_0a",
                                              inner=(mix, row(nw[0, 1])))
    grad("la_out", mm_tn(ocat, dmix, name="la_dwout", bf16_copy=True))
    docat = mm(mm_nt, dmix, W["la_out"], name="la_docat")
    dq, dk, dv, dz, dbg, dwn = gdn_bwd(q, k, v, pm, 3 * H, bg, gcrow, wn, or_a, sall_a, tall, docat, H,
                                       name="gdn_bwd")
    drq, drk, drv, drg = ret_bwd(pm, 4 * H, 5 * H, 6 * H, 7 * H, cs, sn, dmat, avec, bvec, gam, or_b, sall_b,
                                 docat, H, H, name="ret_bwd")
    dpg, da, ddt = gates_bwd(pg, arow, dtrow, dbg, H, name="gates_bwd")
    dcq, dwq = prep_bwd_act(pm, 0, wc8, 0, H, True, dq, name="prep_dq")
    dck, dwk = prep_bwd_act(pm, H, wc8, H, H, True, dk, name="prep_dk")
    dcv, dwv = prep_bwd_act(pm, 2 * H, wc8, 2 * H, H, False, dv, name="prep_dv")
    dxq = prep_bwd_conv(dcq, wc8, 0, H, name="conv_dq")
    dxk = prep_bwd_conv(dck, wc8, H, H, name="conv_dk")
    dxv = prep_bwd_conv(dcv, wc8, 2 * H, H, name="conv_dv")
    dproj = _la_dproj_by_chip([dxq, dxk, dxv, dz, drq, drk, drv, drg], dpg, H)
    grad("la_in", mm_tn(dproj, y0, name="la_dwin").reshape(N_CHIPS, -1, D))
    dy0 = mm(mm_nn, dproj, W["la_in_rows"], name="la_dy")
    dx, dnw[0][0] = mm(rms_bwd, x2, row(nw[0, 0]), dy0, dh1, name="dnorm00")

    G["norm_w"] = jnp.stack([jnp.concatenate(r, axis=0) for r in dnw], axis=0)
    G["conv_w"] = jnp.transpose(jnp.concatenate([dwq, dwk, dwv], axis=1)[:CONV_WIDTH])
    G["a_log"] = da[0, H:2 * H]
    G["dt_bias"] = ddt[0, H:2 * H]
    G["out_norm_w"] = dwn[0]
    return loss, dx, G


def _as2d(a):
    n = int(np.prod(a.shape))
    if a.shape[-1] < LANES and n % LANES == 0:
        return a.reshape(-1, LANES)
    return a.reshape(-1, a.shape[-1])


def _adamw_any(w, g, m, v, name):
    shp = w.shape
    d, m2, v2 = adamw(_as2d(w), _as2d(g.reshape(shp)), _as2d(m), _as2d(v), name=name)
    return g.reshape(shp), d.reshape(shp), m2.reshape(shp), v2.reshape(shp)


def kernel(x, norm_w, la_w_in, la_conv_w, la_a_log, la_dt_bias, la_out_norm_w, la_w_out, sg_w_in, sg_ln_w, sg_ln_b, sg_w_s, sg_b_s, sg_w_out, ffn_w_up, ffn_w_down, loss_target, m_norm_w, m_la_w_in, m_la_conv_w, m_la_a_log, m_la_dt_bias, m_la_out_norm_w, m_la_w_out, m_sg_w_in, m_sg_ln_w, m_sg_ln_b, m_sg_w_s, m_sg_b_s, m_sg_w_out, m_ffn_w_up, m_ffn_w_down, v_norm_w, v_la_w_in, v_la_conv_w, v_la_a_log, v_la_dt_bias, v_la_out_norm_w, v_la_w_out, v_sg_w_in, v_sg_ln_w, v_sg_ln_b, v_sg_w_s, v_sg_b_s, v_sg_w_out, v_ffn_w_up, v_ffn_w_down):
    weights = dict(norm_w=norm_w, la_w_in=la_w_in, la_conv_w=la_conv_w, la_a_log=la_a_log, la_dt_bias=la_dt_bias,
                   la_out_norm_w=la_out_norm_w, la_w_out=la_w_out, sg_w_in=sg_w_in, sg_ln_w=sg_ln_w,
                   sg_ln_b=sg_ln_b, sg_w_s=sg_w_s, sg_b_s=sg_b_s, sg_w_out=sg_w_out, ffn_w_up=ffn_w_up,
                   ffn_w_down=ffn_w_down)
    mom_m = dict(norm_w=m_norm_w, la_w_in=m_la_w_in, la_conv_w=m_la_conv_w, la_a_log=m_la_a_log,
                 la_dt_bias=m_la_dt_bias, la_out_norm_w=m_la_out_norm_w, la_w_out=m_la_w_out, sg_w_in=m_sg_w_in,
                 sg_ln_w=m_sg_ln_w, sg_ln_b=m_sg_ln_b, sg_w_s=m_sg_w_s, sg_b_s=m_sg_b_s, sg_w_out=m_sg_w_out,
                 ffn_w_up=m_ffn_w_up, ffn_w_down=m_ffn_w_down)
    mom_v = dict(norm_w=v_norm_w, la_w_in=v_la_w_in, la_conv_w=v_la_conv_w, la_a_log=v_la_a_log,
                 la_dt_bias=v_la_dt_bias, la_out_norm_w=v_la_out_norm_w, la_w_out=v_la_w_out, sg_w_in=v_sg_w_in,
                 sg_ln_w=v_sg_ln_w, sg_ln_b=v_sg_ln_b, sg_w_s=v_sg_w_s, sg_b_s=v_sg_b_s, sg_w_out=v_sg_w_out,
                 ffn_w_up=v_ffn_w_up, ffn_w_down=v_ffn_w_down)
    order = list(weights)

    T, D = x.shape[1], x.shape[2]
    H = la_a_log.shape[1]
    HD = H * HEAD_DIM
    xi, yi, ci = _place()
    chip = 2 * xi + yi
    c_idx = jnp.reshape(ci, (1,)).astype(jnp.int32)
    j_idx = jnp.reshape(chip, (1,)).astype(jnp.int32)

    cs, csp = _la_shard_rows(H)
    la_rows = lambda a: jnp.pad(jnp.swapaxes(a, 1, 2), ((0, 0), (0, csp - cs), (0, 0)))
    shards = dict(la_in=la_rows(la_w_in)[0], la_out=la_w_out[0], sg_in=sg_w_in[0], sg_out=sg_w_out[0],
                  ffn_up0=ffn_w_up[0], ffn_up1=ffn_w_up[1], ffn_down0=ffn_w_down[0], ffn_down1=ffn_w_down[1])
    shard_halves = {w: _halves(a.astype(BF)) for w, a in shards.items()}
    small_shapes = [norm_w.shape, la_conv_w[0].shape, sg_ln_w[0].shape, sg_ln_b[0].shape]
    small = _pack_rows([norm_w, la_conv_w[0], sg_ln_w[0], sg_ln_b[0]])
    small = _halves(jnp.pad(small, ((0, (-small.shape[0]) % (2 * SUBLANES)), (0, 0))))
    la_in_g, small_g = [g.reshape(N_CHIPS, -1, g.shape[-1])
                        for g in run_comm(gather_comm([shard_halves["la_in"], small]), name="gather_first")]
    pieces = [_unpack_rows(small_g[kk], small_shapes) for kk in range(N_CHIPS)]
    la_main, la_gate = _la_weights_from_gathered(la_in_g, H)
    W = dict(
        norm_w=jnp.concatenate([p[0] for p in pieces], axis=-1),
        conv_w=jnp.concatenate([p[1] for p in pieces], axis=0),
        ln_w=jnp.concatenate([p[2] for p in pieces], axis=0),
        ln_b=jnp.concatenate([p[3] for p in pieces], axis=0),
        a_log=la_a_log[0], dt_bias=la_dt_bias[0], out_norm_w=la_out_norm_w[0], w_s=sg_w_s[0], b_s=sg_b_s[0],
        la_in_main=la_main, la_in_gate=la_gate, la_in_rows=la_in_g.reshape(-1, D),
        ffn_up=[None, None], ffn_down=[None, None],
    )
    plan = ExchangePlan(shard_halves, c_idx, j_idx)

    loss_local, dx, G = _train_local(x[0], loss_target[0], W, plan)
    loss = lax.psum(loss_local, ("x", "y", "c"))

    big_params = dict(la_w_in="la_in", la_w_out="la_out", sg_w_in="sg_in", sg_w_out="sg_out",
                      ffn_w_up="ffn_up", ffn_w_down="ffn_down")
    from_sib = {nm: plan.from_sibling[key] for nm, key in big_params.items()}

    def big_update(nm, key):
        rows = la_rows if nm == "la_w_in" else (lambda a: a)
        r4 = adamw_halves(rows(weights[nm]), plan.finished[key], from_sib[nm], rows(mom_m[nm]), rows(mom_v[nm]),
                          c_idx, name=f"adamw_{nm}")
        return [jnp.swapaxes(t[:, :cs], 1, 2) for t in r4] if nm == "la_w_in" else r4

    big_res = {nm: big_update(nm, key) for nm, key in big_params.items()}

    small_names = ["norm_w", "conv_w", "ln_w", "ln_b", "a_log", "dt_bias", "out_norm_w", "w_s", "b_s"]
    small_full = [G[nm] for nm in small_names]
    summed = _unpack_rows(all_reduce_small(_pack_rows(small_full), name="grads_all_reduce_small"),
                          [g.shape for g in small_full])
    sm = dict(zip(small_names, summed))
    own = lambda full, axis: lax.dynamic_slice_in_dim(full, chip * (full.shape[axis] // N_CHIPS),
                                                      full.shape[axis] // N_CHIPS, axis)
    grads = dict(
        norm_w=own(sm["norm_w"], 2), la_conv_w=own(sm["conv_w"], 0), la_a_log=sm["a_log"],
        la_dt_bias=sm["dt_bias"], la_out_norm_w=sm["out_norm_w"],
        sg_ln_w=own(sm["ln_w"], 0), sg_ln_b=own(sm["ln_b"], 0), sg_w_s=sm["w_s"], sg_b_s=sm["b_s"],
    )

    res = {nm: big_res[nm] if nm in big_res else
           _adamw_any(weights[nm], grads[nm], mom_m[nm], mom_v[nm], f"adamw_{nm}") for nm in order}
    return (loss, dx.reshape(x.shape), *[res[nm][0] for nm in order], *[res[nm][1] for nm in order],
            *[res[nm][2] for nm in order], *[res[nm][3] for nm in order])
```

```python
import math
from typing import Callable, NamedTuple, Optional

import numpy as np
import jax
import jax.numpy as jnp
from jax import lax
from jax.experimental import pallas as pl
from jax.experimental.pallas import tpu as pltpu

F32 = jnp.float32
BF = jnp.bfloat16

V7X_VMEM_BYTES = 64 * 1024 * 1024
VMEM_LIMIT = (V7X_VMEM_BYTES * 3) // 4
LANES = 128
SUBLANES = 8
BF16_ROWS = 16
HEAD_DIM = 128
LA_CHUNK = 64
SG_CHUNK = 128
CONV_WIDTH = 4
ROPE_BASE = 10000.0
EPS = 1e-6
L2_EPS = 1e-6
N_CHIPS = 4
N_DEV = 8

ADAM_LR = 0.001
ADAM_B1 = 0.9
ADAM_B2 = 0.999
ADAM_EPS = 1e-08
ADAM_WD = 0.01
ADAM_STEP = 10

MESH = pl.DeviceIdType.MESH
ANY = pl.BlockSpec(memory_space=pl.ANY)

NN = (((1,), (0,)), ((), ()))
NT = (((1,), (1,)), ((), ()))
TN = (((0,), (0,)), ((), ()))


def _pcall(body, **kw):
    return pl.pallas_call(body, **kw)


def _cp(n_axes):
    return pltpu.CompilerParams(dimension_semantics=("arbitrary",) * n_axes, vmem_limit_bytes=VMEM_LIMIT)


def _tile(n, pref, unit=LANES):
    if n <= pref:
        return n
    t = (pref // unit) * unit
    while t >= unit:
        if n % t == 0:
            return t
        t -= unit
    return n


def _dot(a, b, dims=NN):
    return lax.dot_general(a.astype(BF), b.astype(BF), dims, preferred_element_type=F32)


def _split_bf16(x):
    hi = x.astype(BF)
    return hi, (x - hi.astype(F32)).astype(BF)


def _dot_hi(a, b, dims=NN):
    ah, al = _split_bf16(a)
    bh, bl = _split_bf16(b)
    dot = lambda u, v: lax.dot_general(u, v, dims, preferred_element_type=F32)
    return dot(ah, bh) + (dot(ah, bl) + dot(al, bh))


def _sigmoid(x):
    return 1.0 / (1.0 + jnp.exp(-x))


def _silu(x):
    return x * _sigmoid(x)


def _dsilu(x):
    s = _sigmoid(x)
    return s * (1.0 + x * (1.0 - s))


GELU_C = math.sqrt(2.0 / math.pi)
GELU_A = 0.044715


def _gelu(x):
    hx = 0.5 * x
    return hx + hx * jnp.tanh(x * (GELU_C + (GELU_C * GELU_A) * (x * x)))


def _gelu_and_grad(x):
    x2 = x * x
    hx = 0.5 * x
    t = jnp.tanh(x * (GELU_C + (GELU_C * GELU_A) * x2))
    return hx + hx * t, (0.5 + 0.5 * t) + (hx - hx * (t * t)) * (GELU_C + (3.0 * GELU_C * GELU_A) * x2)


class Comm(NamedTuple):
    ins: list
    outs: list
    nsem: int
    start: Callable
    finish: Callable
    middle: Optional[Callable] = None


def _carry_steps(comm, step, n_steps, refs):
    ci, co, send, recv = refs

    @pl.when(step == 0)
    def _():
        comm.start(ci, co, send, recv)

    if comm.middle is not None:
        @pl.when(step == (3 * n_steps) // 4)
        def _():
            comm.middle(ci, co, send, recv)


def _carry_end(comm, step, n_steps, refs):
    ci, co, send, recv = refs

    @pl.when(step == n_steps - 1)
    def _():
        comm.finish(ci, co, send, recv)


def _matmul(a, b, *, dims, grid, a_spec, b_spec, out_shape, out_spec, acc_shape, name,
            epilogue=None, extras=(), extra_specs=(), comm=None):
    nk = grid[2]
    outs = tuple(out_shape) if isinstance(out_shape, (tuple, list)) else (out_shape,)
    out_specs = tuple(out_spec) if isinstance(out_spec, (tuple, list)) else (out_spec,)
    n_ex, n_out = len(extras), len(outs)
    n_ci = len(comm.ins) if comm else 0
    n_co = len(comm.outs) if comm else 0

    def body(*refs):
        a_ref, b_ref = refs[0], refs[1]
        ex = refs[2:2 + n_ex]
        ci = refs[2 + n_ex:2 + n_ex + n_ci]
        o = refs[2 + n_ex + n_ci:2 + n_ex + n_ci + n_out]
        co = refs[2 + n_ex + n_ci + n_out:2 + n_ex + n_ci + n_out + n_co]
        scratch = refs[2 + n_ex + n_ci + n_out + n_co:]
        i, j, k = pl.program_id(0), pl.program_id(1), pl.program_id(2)

        if comm:
            step, n_steps = (i * grid[1] + j) * nk + k, grid[0] * grid[1] * nk
            carried = (ci, co, scratch[-2], scratch[-1])
            _carry_steps(comm, step, n_steps, carried)

        part = lax.dot_general(a_ref[...].astype(BF), b_ref[...].astype(BF), dims, preferred_element_type=F32)

        def finish(val):
            res = epilogue(val, *[e[...] for e in ex]) if epilogue is not None else (val,)
            for r, oref in zip(res, o):
                oref[...] = r.astype(oref.dtype)

        if nk == 1:
            finish(part)
        else:
            acc = scratch[0]

            @pl.when(k == 0)
            def _():
                acc[...] = part

            @pl.when(k > 0)
            def _():
                acc[...] += part

            @pl.when(k == nk - 1)
            def _():
                finish(acc[...])

        if comm:
            _carry_end(comm, step, n_steps, carried)

    scratch_shapes = [pltpu.VMEM(acc_shape, F32)] if nk > 1 else []
    if comm:
        scratch_shapes += [pltpu.SemaphoreType.DMA((comm.nsem,)), pltpu.SemaphoreType.DMA((comm.nsem,))]
    res = _pcall(
        body, name=name, grid=grid,
        in_specs=[a_spec, b_spec, *extra_specs, *[ANY] * n_ci],
        out_specs=[*out_specs, *[ANY] * n_co],
        out_shape=[*outs, *(comm.outs if comm else [])],
        scratch_shapes=scratch_shapes,
        compiler_params=_cp(3),
    )(a, b, *extras, *(comm.ins if comm else []))
    main = res[0] if n_out == 1 else list(res[:n_out])
    return (main, list(res[n_out:])) if comm else main


def mm_nn(a, w, *, name, out_dtypes=(F32,), epilogue=None, extras=(), comm=None, tm=1024, tn=1024, tk=2048):
    M, K = a.shape
    if w.ndim == 3:
        S, _, Ns = w.shape
        N = S * Ns
    else:
        S, Ns = 1, w.shape[1]
        N = Ns
    tm, tn, tk = _tile(M, tm), _tile(Ns, tn), _tile(K, tk)
    npb = Ns // tn
    grid = (M // tm, N // tn, K // tk)
    a_spec = pl.BlockSpec((tm, tk), lambda i, j, k: (i, k))
    if w.ndim == 3:
        b_spec = pl.BlockSpec((None, tk, tn), lambda i, j, k: (j // npb, k, j % npb))
    else:
        b_spec = pl.BlockSpec((tk, tn), lambda i, j, k: (k, j))
    o_spec = pl.BlockSpec((tm, tn), lambda i, j, k: (i, j))
    outs = tuple(jax.ShapeDtypeStruct((M, N), d) for d in out_dtypes)
    res = _matmul(a, w, dims=NN, grid=grid, a_spec=a_spec, b_spec=b_spec,
                  out_shape=outs, out_spec=(o_spec,) * len(outs), acc_shape=(tm, tn), name=name,
                  epilogue=epilogue, extras=extras, extra_specs=(o_spec,) * len(extras), comm=comm)
    return res


def mm_nt(a, w, *, name, out_dtypes=(F32,), epilogue=None, extras=(), comm=None, tm=1024, tn=1024, tk=2048):
    M, Kc = a.shape
    if w.ndim == 3:
        S, Nout, Ks = w.shape
    else:
        S, (Nout, Ks) = 1, w.shape
    assert S * Ks == Kc
    tm, tn, tk = _tile(M, tm), _tile(Nout, tn), _tile(Ks, tk)
    kpb = Ks // tk
    grid = (M // tm, Nout // tn, Kc // tk)
    a_spec = pl.BlockSpec((tm, tk), lambda i, j, k: (i, k))
    if w.ndim == 3:
        b_spec = pl.BlockSpec((None, tn, tk), lambda i, j, k: (k // kpb, j, k % kpb))
    else:
        b_spec = pl.BlockSpec((tn, tk), lambda i, j, k: (j, k))
    o_spec = pl.BlockSpec((tm, tn), lambda i, j, k: (i, j))
    outs = tuple(jax.ShapeDtypeStruct((M, Nout), d) for d in out_dtypes)
    return _matmul(a, w, dims=NT, grid=grid, a_spec=a_spec, b_spec=b_spec,
                   out_shape=outs, out_spec=(o_spec,) * len(outs), acc_shape=(tm, tn), name=name,
                   epilogue=epilogue, extras=extras, extra_specs=(o_spec,) * len(extras), comm=comm)


def mm_tn(x, dy, *, name, shards=1, bf16_copy=False, tm=1024, tn=1024, tk=2048):
    T, Kin = x.shape
    N = dy.shape[1]
    Ns = N // shards
    tm, tn, tk = _tile(Kin, tm), _tile(Ns, tn), _tile(T, tk)
    npb = Ns // tn
    grid = (Kin // tm, N // tn, T // tk)
    a_spec = pl.BlockSpec((tk, tm), lambda i, j, k: (k, i))
    b_spec = pl.BlockSpec((tk, tn), lambda i, j, k: (k, j))
    if shards > 1:
        o_spec = pl.BlockSpec((None, tm, tn), lambda i, j, k: (j // npb, i, j % npb))
        out = jax.ShapeDtypeStruct((shards, Kin, Ns), F32)
    else:
        o_spec = pl.BlockSpec((tm, tn), lambda i, j, k: (i, j))
        out = jax.ShapeDtypeStruct((Kin, N), F32)
    if bf16_copy:
        return _matmul(x, dy, dims=TN, grid=grid, a_spec=a_spec, b_spec=b_spec,
                       out_shape=(out, jax.ShapeDtypeStruct(out.shape, BF)), out_spec=(o_spec, o_spec),
                       acc_shape=(tm, tn), name=name, epilogue=lambda acc: (acc, acc))
    return _matmul(x, dy, dims=TN, grid=grid, a_spec=a_spec, b_spec=b_spec,
                   out_shape=out, out_spec=o_spec, acc_shape=(tm, tn), name=name)


ROW_TILE = 256


def _rows(tr, d):
    return pl.BlockSpec((tr, d), lambda i: (i, 0))


def _fixed(shape):
    nd = len(shape)
    return pl.BlockSpec(shape, lambda *_: (0,) * nd)


def _rms(xv, w):
    r = lax.rsqrt(jnp.mean(xv * xv, axis=-1, keepdims=True) + EPS)
    return xv * r * w


def rms_fwd(x, w, *, name):
    T, D = x.shape
    tr = _tile(T, ROW_TILE, SUBLANES)

    def body(x_ref, w_ref, y_ref):
        y_ref[...] = _rms(x_ref[...], w_ref[...]).astype(y_ref.dtype)

    return _pcall(body, name=name, grid=(T // tr,), in_specs=[_rows(tr, D), _fixed((1, D))],
                  out_specs=_rows(tr, D), out_shape=jax.ShapeDtypeStruct((T, D), BF), compiler_params=_cp(1))(x, w)


def res_norm(h, m, wa, wb, *, name):
    T, D = h.shape
    tr = _tile(T, ROW_TILE, SUBLANES)

    def body(h_ref, m_ref, wa_ref, wb_ref, ho_ref, y_ref):
        ho = h_ref[...] + _rms(m_ref[...], wa_ref[...])
        ho_ref[...] = ho
        y_ref[...] = _rms(ho, wb_ref[...]).astype(y_ref.dtype)

    return _pcall(body, name=name, grid=(T // tr,),
                  in_specs=[_rows(tr, D), _rows(tr, D), _fixed((1, D)), _fixed((1, D))],
                  out_specs=[_rows(tr, D), _rows(tr, D)],
                  out_shape=[jax.ShapeDtypeStruct((T, D), F32), jax.ShapeDtypeStruct((T, D), BF)],
                  compiler_params=_cp(1))(h, m, wa, wb)


def _rms_bwd_rows(xv, w, dyv):
    r = lax.rsqrt(jnp.mean(xv * xv, axis=-1, keepdims=True) + EPS)
    xh = xv * r
    dyw = dyv * w
    return r * (dyw - xh * jnp.mean(dyw * xh, axis=-1, keepdims=True)), jnp.sum(dyv * xh, axis=0, keepdims=True)


def rms_bwd(x, w, dy, dres, *, name, inner=None, comm=None):
    T, D = x.shape
    tr = _tile(T, ROW_TILE, SUBLANES)
    chained = inner is not None
    n_in = 6 if chained else 4
    n_out = 4 if chained else 2
    n_ci = len(comm.ins) if comm else 0
    n_co = len(comm.outs) if comm else 0

    def body(*refs):
        ci = refs[n_in:n_in + n_ci]
        outs = refs[n_in + n_ci:n_in + n_ci + n_out]
        co = refs[n_in + n_ci + n_out:n_in + n_ci + n_out + n_co]
        if chained:
            x_ref, w_ref, dy_ref, dr_ref, m_ref, wa_ref = refs[:n_in]
            dx_ref, dw_ref, dm_ref, dwa_ref = outs
        else:
            x_ref, w_ref, dy_ref, dr_ref = refs[:n_in]
            dx_ref, dw_ref = outs
        i = pl.program_id(0)
        if comm:
            carried = (ci, co, refs[-2], refs[-1])
            _carry_steps(comm, i, T // tr, carried)
        dx, part = _rms_bwd_rows(x_ref[...], w_ref[...], dy_ref[...].astype(F32))
        dx = dx + dr_ref[...]
        dx_ref[...] = dx
        if chained:
            dm, part_a = _rms_bwd_rows(m_ref[...], wa_ref[...], dx)
            dm_ref[...] = dm.astype(dm_ref.dtype)

        @pl.when(i == 0)
        def _():
            dw_ref[...] = part
            if chained:
                dwa_ref[...] = part_a

        @pl.when(i > 0)
        def _():
            dw_ref[...] += part
            if chained:
                dwa_ref[...] += part_a

        if comm:
            _carry_end(comm, i, T // tr, carried)

    ins = [x, w, dy, dres] + (list(inner) if chained else [])
    in_specs = [_rows(tr, D), _fixed((1, D)), _rows(tr, D), _rows(tr, D)]
    out_specs = [_rows(tr, D), _fixed((1, D))]
    out_shape = [jax.ShapeDtypeStruct((T, D), F32), jax.ShapeDtypeStruct((1, D), F32)]
    if chained:
        in_specs += [_rows(tr, D), _fixed((1, D))]
        out_specs += [_rows(tr, D), _fixed((1, D))]
        out_shape += [jax.ShapeDtypeStruct((T, D), BF), jax.ShapeDtypeStruct((1, D), F32)]
    scratch_shapes = []
    if comm:
        ins, in_specs = ins + list(comm.ins), in_specs + [ANY] * n_ci
        out_specs, out_shape = out_specs + [ANY] * n_co, out_shape + list(comm.outs)
        scratch_shapes = [pltpu.SemaphoreType.DMA((comm.nsem,)), pltpu.SemaphoreType.DMA((comm.nsem,))]
    res = _pcall(body, name=name, grid=(T // tr,), in_specs=in_specs, out_specs=out_specs, out_shape=out_shape,
                 scratch_shapes=scratch_shapes, compiler_params=_cp(1))(*ins)
    return (list(res[:n_out]), list(res[n_out:])) if comm else res


def last_norm_and_loss(h, m, w, tgt, *, name):
    T, D = h.shape
    tr = _tile(T, ROW_TILE, SUBLANES)

    def body(h_ref, m_ref, w_ref, t_ref, dy_ref, dm_ref, dw_ref, l_ref):
        i = pl.program_id(0)
        mv = m_ref[...]
        r = lax.rsqrt(jnp.mean(mv * mv, axis=-1, keepdims=True) + EPS)
        xh = mv * r
        e = h_ref[...] + xh * w_ref[...] - t_ref[...]
        dy = e * (1.0 / D)
        dy_ref[...] = dy
        dyw = dy * w_ref[...]
        dm_ref[...] = (r * (dyw - xh * jnp.mean(dyw * xh, axis=-1, keepdims=True))).astype(dm_ref.dtype)
        pw = jnp.sum(dy * xh, axis=0, keepdims=True)
        pl_ = 0.5 * jnp.sum(jnp.mean(e * e, axis=-1, keepdims=True), axis=0, keepdims=True)
        pl_ = jnp.broadcast_to(pl_, (1, LANES))

        @pl.when(i == 0)
        def _():
            dw_ref[...] = pw
            l_ref[...] = pl_

        @pl.when(i > 0)
        def _():
            dw_ref[...] += pw
            l_ref[...] += pl_

    return _pcall(body, name=name, grid=(T // tr,),
                  in_specs=[_rows(tr, D), _rows(tr, D), _fixed((1, D)), _rows(tr, D)],
                  out_specs=[_rows(tr, D), _rows(tr, D), _fixed((1, D)), _fixed((1, LANES))],
                  out_shape=[jax.ShapeDtypeStruct((T, D), F32), jax.ShapeDtypeStruct((T, D), BF),
                             jax.ShapeDtypeStruct((1, D), F32), jax.ShapeDtypeStruct((1, LANES), F32)],
                  compiler_params=_cp(1))(h, m, w, tgt)


def adamw(w, g, m, v, *, name):
    R, C = w.shape
    tr = _tile(R, max(SUBLANES, (1 << 18) // C), SUBLANES)
    c1 = 1.0 - ADAM_B1 ** ADAM_STEP
    c2 = 1.0 - ADAM_B2 ** ADAM_STEP

    def body(w_ref, g_ref, m_ref, v_ref, d_ref, mo_ref, vo_ref):
        gv = g_ref[...]
        m2 = ADAM_B1 * m_ref[...] + (1.0 - ADAM_B1) * gv
        v2 = ADAM_B2 * v_ref[...] + (1.0 - ADAM_B2) * (gv * gv)
        d_ref[...] = -ADAM_LR * ((m2 / c1) / (jnp.sqrt(v2 / c2) + ADAM_EPS) + ADAM_WD * w_ref[...])
        mo_ref[...] = m2
        vo_ref[...] = v2

    spec = _rows(tr, C)
    sds = jax.ShapeDtypeStruct((R, C), F32)
    return _pcall(body, name=name, grid=(R // tr,), in_specs=[spec] * 4, out_specs=[spec] * 3,
                  out_shape=[sds] * 3, compiler_params=_cp(1))(w, g, m, v)


HALO = SUBLANES


def _conv_down(xx, w_ref):
    acc = xx * w_ref[pl.ds(CONV_WIDTH - 1, 1), :]
    for d in range(1, CONV_WIDTH):
        acc = acc + pltpu.roll(xx, d, 0) * w_ref[pl.ds(CONV_WIDTH - 1 - d, 1), :]
    return acc


def _conv_tile(x_ref, halo_ref, w_ref, first):
    xs = x_ref[...]
    hal = jnp.where(first, 0.0, halo_ref[...])
    cat = jnp.concatenate([hal, xs[0:HALO]], axis=0)
    return jnp.concatenate([_conv_down(cat, w_ref)[HALO:2 * HALO], _conv_down(xs, w_ref)[HALO:]], axis=0)


def _shift_down_tile(x_ref, halo_ref, first, d):
    xs = x_ref[...]
    if d == 0:
        return xs
    hal = jnp.where(first, 0.0, halo_ref[...])
    cat = jnp.concatenate([hal, xs[0:HALO]], axis=0)
    return jnp.concatenate([pltpu.roll(cat, d, 0)[HALO:2 * HALO], pltpu.roll(xs, d, 0)[HALO:]], axis=0)


def _l2n(s):
    return s * lax.rsqrt(jnp.sum(s * s, axis=-1, keepdims=True) + L2_EPS)


PREP_ROWS = 512


def _l2n_groups(s, nb):
    return jnp.concatenate([_l2n(s[:, g * LANES:(g + 1) * LANES]) for g in range(nb)], axis=1)


def prep_fwd(pm, off, wc8, woff, nblk, l2, *, name):
    T = pm.shape[0]
    tr = _tile(T, PREP_ROWS, SUBLANES)
    hb = tr // HALO
    wb = _heads_per_step(nblk)
    wl = wb * LANES

    def body(x_ref, halo_ref, w_ref, o_ref):
        i = pl.program_id(0)
        s = _silu(_conv_tile(x_ref, halo_ref, w_ref, i == 0))
        o_ref[...] = _l2n_groups(s, wb) if l2 else s

    return _pcall(
        body, name=name, grid=(T // tr, nblk // wb),
        in_specs=[pl.BlockSpec((tr, wl), lambda i, c: (i, off // wb + c)),
                  pl.BlockSpec((HALO, wl), lambda i, c: (jnp.maximum(i * hb - 1, 0), off // wb + c)),
                  pl.BlockSpec((SUBLANES, wl), lambda i, c: (0, woff // wb + c))],
        out_specs=pl.BlockSpec((tr, wl), lambda i, c: (i, c)),
        out_shape=jax.ShapeDtypeStruct((T, nblk * LANES), F32), compiler_params=_cp(2))(pm, pm, wc8)


def prep_bwd_act(pm, off, wc8, woff, nblk, l2, dout, *, name):
    T = pm.shape[0]
    tr = _tile(T, PREP_ROWS, SUBLANES)
    hb = tr // HALO
    wb = _heads_per_step(nblk)
    wl = wb * LANES

    def l2_bwd(s, do):
        r = lax.rsqrt(jnp.sum(s * s, axis=-1, keepdims=True) + L2_EPS)
        nrm = s * r
        return r * (do - nrm * jnp.sum(do * nrm, axis=-1, keepdims=True))

    def body(x_ref, halo_ref, w_ref, do_ref, dc_ref, dw_ref):
        i = pl.program_id(1)
        first = i == 0
        y = _conv_tile(x_ref, halo_ref, w_ref, first)
        s = _silu(y)
        do = do_ref[...]
        if l2:
            ds = jnp.concatenate([l2_bwd(s[:, g * LANES:(g + 1) * LANES], do[:, g * LANES:(g + 1) * LANES])
                                  for g in range(wb)], axis=1)
        else:
            ds = do
        dc = ds * _dsilu(y)
        dc_ref[...] = dc

        @pl.when(first)
        def _():
            dw_ref[...] = jnp.zeros_like(dw_ref)

        for j in range(CONV_WIDTH):
            xsh = _shift_down_tile(x_ref, halo_ref, first, CONV_WIDTH - 1 - j)
            dw_ref[pl.ds(j, 1), :] += jnp.sum(dc * xsh, axis=0, keepdims=True)

    return _pcall(
        body, name=name, grid=(nblk // wb, T // tr),
        in_specs=[pl.BlockSpec((tr, wl), lambda c, i: (i, off // wb + c)),
                  pl.BlockSpec((HALO, wl), lambda c, i: (jnp.maximum(i * hb - 1, 0), off // wb + c)),
                  pl.BlockSpec((SUBLANES, wl), lambda c, i: (0, woff // wb + c)),
                  pl.BlockSpec((tr, wl), lambda c, i: (i, c))],
        out_specs=[pl.BlockSpec((tr, wl), lambda c, i: (i, c)),
                   pl.BlockSpec((SUBLANES, wl), lambda c, i: (0, c))],
        out_shape=[jax.ShapeDtypeStruct((T, nblk * LANES), F32),
                   jax.ShapeDtypeStruct((SUBLANES, nblk * LANES), F32)],
        compiler_params=_cp(2))(pm, pm, wc8, dout)


def prep_bwd_conv(dc, wc8, woff, nblk, *, name):
    T = dc.shape[0]
    tr = _tile(T, PREP_ROWS, SUBLANES)
    hb = tr // HALO
    nt = T // tr
    last_halo = T // HALO - 1
    wb = _heads_per_step(nblk)
    wl = wb * LANES

    def up(xx, w_ref):
        rows = xx.shape[0]
        acc = xx * w_ref[pl.ds(CONV_WIDTH - 1, 1), :]
        for d in range(1, CONV_WIDTH):
            acc = acc + pltpu.roll(xx, rows - d, 0) * w_ref[pl.ds(CONV_WIDTH - 1 - d, 1), :]
        return acc

    def body(x_ref, halo_ref, w_ref, o_ref):
        i = pl.program_id(0)
        xs = x_ref[...]
        hal = jnp.where(i == nt - 1, 0.0, halo_ref[...])
        cat = jnp.concatenate([xs[tr - HALO:tr], hal], axis=0)
        out = jnp.concatenate([up(xs, w_ref)[:tr - HALO], up(cat, w_ref)[0:HALO]], axis=0)
        o_ref[...] = out.astype(o_ref.dtype)

    return _pcall(
        body, name=name, grid=(nt, nblk // wb),
        in_specs=[pl.BlockSpec((tr, wl), lambda i, c: (i, c)),
                  pl.BlockSpec((HALO, wl), lambda i, c: (jnp.minimum((i + 1) * hb, last_halo), c)),
                  pl.BlockSpec((SUBLANES, wl), lambda i, c: (0, woff // wb + c))],
        out_specs=pl.BlockSpec((tr, wl), lambda i, c: (i, c)),
        out_shape=jax.ShapeDtypeStruct((T, nblk * LANES), BF), compiler_params=_cp(2))(dc, dc, wc8)


def _softplus(x):
    return jnp.maximum(x, 0.0) + jnp.log(1.0 + jnp.exp(-jnp.abs(x)))


def _tril_ones(c):
    t = lax.broadcasted_iota(jnp.int32, (c, c), 0)
    s = lax.broadcasted_iota(jnp.int32, (c, c), 1)
    return (t >= s).astype(F32)


GATE_CHUNKS_PER_STEP = 8


def _chunks_per_step(n_chunks):
    per = GATE_CHUNKS_PER_STEP
    while n_chunks % per:
        per //= 2
    return per


def _triu_ones(c):
    t = lax.broadcasted_iota(jnp.int32, (c, c), 0)
    s = lax.broadcasted_iota(jnp.int32, (c, c), 1)
    return (t <= s).astype(F32)


def gates_fwd(pg, arow, dtrow, H, *, name):
    T = pg.shape[0]
    C = LA_CHUNK
    N = T // C
    per = _chunks_per_step(N)

    def body(x_ref, a_ref, dt_ref, bg_ref, gr_ref):
        lane = lax.broadcasted_iota(jnp.int32, (C, LANES), 1)
        lm, um = _tril_ones(C), _triu_ones(C)
        for j in range(per):
            rows = slice(j * C, (j + 1) * C)
            x = x_ref[rows, :]
            g = -jnp.exp(a_ref[...]) * _softplus(x + dt_ref[...])
            g = jnp.where((lane >= H) & (lane < 2 * H), g, 0.0)
            bg_ref[rows, :] = jnp.where(lane < H, _sigmoid(x), _dot_hi(lm, g))
            gr_ref[j] = _dot_hi(g, um, TN)

    return _pcall(
        body, name=name, grid=(N // per,),
        in_specs=[pl.BlockSpec((per * C, LANES), lambda n: (n, 0)), _fixed((1, LANES)), _fixed((1, LANES))],
        out_specs=[pl.BlockSpec((per * C, LANES), lambda n: (n, 0)),
                   pl.BlockSpec((per, LANES, C), lambda n: (n, 0, 0))],
        out_shape=[jax.ShapeDtypeStruct((T, LANES), F32), jax.ShapeDtypeStruct((N, LANES, C), F32)],
        compiler_params=_cp(1))(pg, arow, dtrow)


def gates_bwd(pg, arow, dtrow, dbg, H, *, name):
    T = pg.shape[0]
    C = LA_CHUNK
    N = T // C
    per = _chunks_per_step(N)

    def body(x_ref, a_ref, dt_ref, d_ref, dx_ref, da_ref, ddt_ref):
        n = pl.program_id(0)
        lane = lax.broadcasted_iota(jnp.int32, (C, LANES), 1)
        in_g = (lane >= H) & (lane < 2 * H)
        e = jnp.exp(a_ref[...])
        lm = _tril_ones(C)
        pa = jnp.zeros((1, LANES), F32)
        pd = jnp.zeros((1, LANES), F32)
        for j in range(per):
            rows = slice(j * C, (j + 1) * C)
            x = x_ref[rows, :]
            d = d_ref[rows, :]
            xs = x + dt_ref[...]
            g = -e * _softplus(xs)
            dg = _dot_hi(lm, jnp.where(in_g, d, 0.0), TN)
            dxs = jnp.where(in_g, dg * (-e) * _sigmoid(xs), 0.0)
            beta = _sigmoid(x)
            dx_ref[rows, :] = jnp.where(lane < H, d * beta * (1.0 - beta), dxs).astype(dx_ref.dtype)
            pa = pa + jnp.sum(jnp.where(in_g, dg * g, 0.0), axis=0, keepdims=True)
            pd = pd + jnp.sum(dxs, axis=0, keepdims=True)

        @pl.when(n == 0)
        def _():
            da_ref[...] = pa
            ddt_ref[...] = pd

        @pl.when(n > 0)
        def _():
            da_ref[...] += pa
            ddt_ref[...] += pd

    rows = pl.BlockSpec((per * C, LANES), lambda n: (n, 0))
    return _pcall(
        body, name=name, grid=(N // per,),
        in_specs=[rows, _fixed((1, LANES)), _fixed((1, LANES)), rows],
        out_specs=[rows, _fixed((1, LANES)), _fixed((1, LANES))],
        out_shape=[jax.ShapeDtypeStruct((T, LANES), BF), jax.ShapeDtypeStruct((1, LANES), F32),
                   jax.ShapeDtypeStruct((1, LANES), F32)],
        compiler_params=_cp(1))(pg, arow, dtrow, dbg)


QK_SCALE = HEAD_DIM ** -0.5


HEADS_PER_STEP = 8


def _heads_per_step(H):
    hb = HEADS_PER_STEP
    while H % hb:
        hb //= 2
    return hb


def _head_rstd(o):
    return lax.rsqrt(jnp.mean(o * o, axis=-1, keepdims=True) + EPS)


def _gdn_gates(bg_ref, gr_ref, h, H):
    C = LA_CHUNK
    bgv = bg_ref[...]
    lane = lax.broadcasted_iota(jnp.int32, (C, LANES), 1)
    beta = jnp.sum(jnp.where(lane == h, bgv, 0.0), axis=1, keepdims=True)
    gc = jnp.sum(jnp.where(lane == H + h, bgv, 0.0), axis=1, keepdims=True)
    grow = gr_ref[pl.ds(H + h, 1), :]
    ri = lax.broadcasted_iota(jnp.int32, (C, 1), 0)
    gl = jnp.sum(jnp.where(ri == C - 1, gc, 0.0), axis=0, keepdims=True)
    return beta, gc, grow, gl


def _chunk_masks():
    C = LA_CHUNK
    ti = lax.broadcasted_iota(jnp.int32, (C, C), 0)
    si = lax.broadcasted_iota(jnp.int32, (C, C), 1)
    return ti >= si, ti > si, ti == si


def _decay(gc, grow, causal):
    return jnp.where(causal, jnp.exp(jnp.where(causal, gc - grow, 0.0)), 0.0)


def _interleave(gens):
    gens = list(gens)
    results = [None] * len(gens)
    live = list(range(len(gens)))
    while live:
        still = []
        for i in live:
            try:
                next(gens[i])
                still.append(i)
            except StopIteration as stop:
                results[i] = stop.value
        live = still
    return results


def _unit_lower_inverse(a, eye):
    x = -a
    p = jnp.where(eye, 1.0, 0.0) + x
    for _ in range(5):
        x = _dot_hi(x, x)
        yield
        p = p + _dot_hi(p, x)
        yield
    return p


def gdn_fwd(q, k, v, pm, zoff, bg, gcrow, wn, H, *, name, comm=None):
    T = q.shape[0]
    C = LA_CHUNK
    N = T // C
    hd = HEAD_DIM

    HB = _heads_per_step(H)
    n_ci = len(comm.ins) if comm else 0
    n_co = len(comm.outs) if comm else 0

    def body(*refs):
        q_ref, k_ref, v_ref, z_ref, bg_ref, gr_ref, wn_ref = refs[:7]
        ci = refs[7:7 + n_ci]
        og_ref, or_ref, sall_ref, tall_ref = refs[7 + n_ci:11 + n_ci]
        co = refs[11 + n_ci:11 + n_ci + n_co]
        S = refs[11 + n_ci + n_co]
        n = pl.program_id(0)
        hg = pl.program_id(1)
        causal, strict, eye = _chunk_masks()

        @pl.when((n == 0) & (hg == 0))
        def _():
            S[...] = jnp.zeros_like(S)

        if comm:
            step, n_steps = n * (H // HB) + hg, N * (H // HB)
            carried = (ci, co, refs[-2], refs[-1])
            _carry_steps(comm, step, n_steps, carried)

        states = [S[hg * HB + i] for i in range(HB)]

        def head(i):
            h = hg * HB + i
            sl = slice(i * hd, (i + 1) * hd)
            beta, gc, grow, gl = _gdn_gates(bg_ref, gr_ref, h, H)
            dm = _decay(gc, grow, causal)
            qs = q_ref[:, sl] * QK_SCALE
            kk = k_ref[:, sl]
            vv = v_ref[:, sl]
            eg = jnp.exp(gc)
            kb = kk * beta
            a = jnp.where(strict, _dot(kb, kk, NT) * dm, 0.0)
            yield
            tm = yield from _unit_lower_inverse(a, eye)
            u = _dot(tm, vv * beta)
            w = _dot(tm, kb * eg)
            qk = jnp.where(causal, _dot(qs, kk, NT) * dm, 0.0)
            yield
            s0 = states[i]
            vnew = u - _dot(w, s0)
            o = _dot(qs * eg, s0)
            yield
            o = o + _dot(qk, vnew)
            s1 = s0 * jnp.exp(gl) + _dot(kk * jnp.exp(gl - gc), vnew, TN)
            yield
            sall_ref[i] = s0
            tall_ref[i] = tm
            or_ref[:, sl] = o
            og_ref[:, sl] = (o * _head_rstd(o) * wn_ref[...] * _silu(z_ref[:, sl])).astype(og_ref.dtype)
            return s1

        for i, s1 in enumerate(_interleave([head(i) for i in range(HB)])):
            S[hg * HB + i] = s1

        if comm:
            _carry_end(comm, step, n_steps, carried)

    blk = lambda off: pl.BlockSpec((C, HB * hd), lambda n, h: (n, off // HB + h))
    scratch_shapes = [pltpu.VMEM((H, hd, hd), F32)]
    if comm:
        scratch_shapes += [pltpu.SemaphoreType.DMA((comm.nsem,)), pltpu.SemaphoreType.DMA((comm.nsem,))]
    res = _pcall(
        body, name=name, grid=(N, H // HB),
        in_specs=[blk(0), blk(0), blk(0), blk(zoff),
                  pl.BlockSpec((C, LANES), lambda n, h: (n, 0)),
                  pl.BlockSpec((None, LANES, C), lambda n, h: (n, 0, 0)),
                  _fixed((1, hd)), *[ANY] * n_ci],
        out_specs=[blk(0), blk(0),
                   pl.BlockSpec((None, HB, hd, hd), lambda n, h: (n, h, 0, 0)),
                   pl.BlockSpec((None, HB, C, C), lambda n, h: (n, h, 0, 0)), *[ANY] * n_co],
        out_shape=[jax.ShapeDtypeStruct((T, 2 * H * hd), BF),
                   jax.ShapeDtypeStruct((T, H * hd), F32),
                   jax.ShapeDtypeStruct((N, H, hd, hd), F32), jax.ShapeDtypeStruct((N, H, C, C), F32),
                   *(comm.outs if comm else [])],
        scratch_shapes=scratch_shapes,
        compiler_params=_cp(2))(q, k, v, pm, bg, gcrow, wn, *(comm.ins if comm else []))
    return (*res[:4], list(res[4:])) if comm else tuple(res)


def gdn_bwd(q, k, v, pm, zoff, bg, gcrow, wn, oraw, sall, tall, dog, H, *, name):
    T = q.shape[0]
    C = LA_CHUNK
    N = T // C
    hd = HEAD_DIM

    HB = _heads_per_step(H)

    def body(*refs):
        dbg_ref, dwn_ref, dS = refs[15], refs[16], refs[17]
        n = pl.program_id(0)
        hg = pl.program_id(1)

        @pl.when((n == 0) & (hg == 0))
        def _():
            dwn_ref[...] = jnp.zeros_like(dwn_ref)
            dS[...] = jnp.zeros_like(dS)

        @pl.when(hg == 0)
        def _():
            dbg_ref[...] = jnp.zeros_like(dbg_ref)

        ds_in = [dS[hg * HB + i] for i in range(HB)]
        outs = _interleave([head(i, hg * HB + i, ds_in[i], *refs) for i in range(HB)])
        for i in range(HB):
            dS[hg * HB + i] = outs[i][0]
        dwn_ref[...] += sum(o[1] for o in outs)
        dbg_ref[...] += sum(o[2] for o in outs)

    def head(i, h, ds1, q_ref, k_ref, v_ref, z_ref, bg_ref, gr_ref, wn_ref, or_ref, sall_ref, tall_ref, dog_ref,
             dq_ref, dk_ref, dv_ref, dz_ref, dbg_ref, dwn_ref, dS):
        sl = slice(i * hd, (i + 1) * hd)
        beta, gc, grow, gl = _gdn_gates(bg_ref, gr_ref, h, H)
        causal, strict, eye = _chunk_masks()
        dm = _decay(gc, grow, causal)
        qs = q_ref[:, sl] * QK_SCALE
        kk = k_ref[:, sl]
        vv = v_ref[:, sl]
        zz = z_ref[:, sl]
        o = or_ref[:, sl]
        dog = dog_ref[:, sl]
        wn_v = wn_ref[...]
        s0 = sall_ref[i]
        tm = tall_ref[i]

        rstd = _head_rstd(o)
        on = o * rstd
        sz = _silu(zz)
        don = dog * wn_v * sz
        dwn_part = jnp.sum(dog * on * sz, axis=0, keepdims=True)
        dz_ref[:, sl] = (dog * on * wn_v * _dsilu(zz)).astype(dz_ref.dtype)
        do = rstd * (don - on * jnp.mean(don * on, axis=-1, keepdims=True))

        eg = jnp.exp(gc)
        kb = kk * beta
        vb = vv * beta
        kbg = kb * eg
        a = jnp.where(strict, _dot(kb, kk, NT) * dm, 0.0)
        u = _dot(tm, vb)
        w = _dot(tm, kbg)
        qk = jnp.where(causal, _dot(qs, kk, NT) * dm, 0.0)
        dqdec = _dot(do, s0, NT)
        yield
        vnew = u - _dot(w, s0)
        qdec = qs * eg
        etail = jnp.exp(gl - gc)
        ktail = kk * etail
        egl = jnp.exp(gl)
        dvnew = _dot(qk, do, TN) + _dot(ktail, ds1)
        yield
        dqk = jnp.where(causal, _dot(do, vnew, NT), 0.0)
        dktail = _dot(vnew, ds1, NT)
        dcd = jnp.sum(jnp.sum(s0 * ds1, axis=1, keepdims=True), axis=0, keepdims=True)
        ds0 = egl * ds1 + _dot(qdec, do, TN) - _dot(w, dvnew, TN)
        dw = -_dot(dvnew, s0, NT)
        dvb = _dot(tm, dvnew, TN)
        yield
        dkbg = _dot(tm, dw, TN)
        dtm = _dot(dvnew, vb, NT) + _dot(dw, kbg, NT)
        dqkr = dqk * dm
        dqs = _dot(dqkr, kk) + dqdec * eg
        yield
        x = _dot_hi(tm, dtm, TN)
        yield
        da = jnp.where(strict, -_dot_hi(x, tm, NT), 0.0)
        yield
        dkk = da * dm
        dkb = _dot(dkk, kk) + dkbg * eg
        dk = _dot(dkk, kb, TN)
        dk = dk + _dot(dqkr, qs, TN) + dktail * etail + dkb * beta
        g = da * a + dqk * qk
        colsum = jnp.max(_dot_hi(g, jnp.ones((C, LANES), F32), TN), axis=1, keepdims=True)
        yield
        rk = jnp.sum(dktail * ktail, axis=1, keepdims=True)
        dgc = (jnp.sum(g, axis=1, keepdims=True) - colsum
               + jnp.sum(dqdec * qdec, axis=1, keepdims=True) - rk
               + jnp.sum(dkbg * kbg, axis=1, keepdims=True))
        dgl = jnp.sum(rk, axis=0, keepdims=True) + dcd * egl
        ri = lax.broadcasted_iota(jnp.int32, (C, 1), 0)
        dgc = dgc + jnp.where(ri == C - 1, dgl, 0.0)
        dbeta = jnp.sum(dkb * kk, axis=1, keepdims=True) + jnp.sum(dvb * vv, axis=1, keepdims=True)

        dq_ref[:, sl] = dqs * QK_SCALE
        dk_ref[:, sl] = dk
        dv_ref[:, sl] = dvb * beta
        lane = lax.broadcasted_iota(jnp.int32, (C, LANES), 1)
        return ds0, dwn_part, jnp.where(lane == h, dbeta, 0.0) + jnp.where(lane == H + h, dgc, 0.0)

    blk = lambda off: pl.BlockSpec((C, HB * hd), lambda n, h: (N - 1 - n, off // HB + h))
    st = lambda r: pl.BlockSpec((None, HB, r, r), lambda n, h: (N - 1 - n, h, 0, 0))
    return _pcall(
        body, name=name, grid=(N, H // HB),
        in_specs=[blk(0), blk(0), blk(0), blk(zoff),
                  pl.BlockSpec((C, LANES), lambda n, h: (N - 1 - n, 0)),
                  pl.BlockSpec((None, LANES, C), lambda n, h: (N - 1 - n, 0, 0)),
                  _fixed((1, hd)), blk(0), st(hd), st(C), blk(0)],
        out_specs=[blk(0), blk(0), blk(0), blk(0),
                   pl.BlockSpec((C, LANES), lambda n, h: (N - 1 - n, 0)), _fixed((1, hd))],
        out_shape=[jax.ShapeDtypeStruct((T, H * hd), F32)] * 3
        + [jax.ShapeDtypeStruct((T, H * hd), BF), jax.ShapeDtypeStruct((T, LANES), F32),
           jax.ShapeDtypeStruct((1, hd), F32)],
        scratch_shapes=[pltpu.VMEM((H, hd, hd), F32)],
        compiler_params=_cp(2))(q, k, v, pm, bg, gcrow, wn, oraw, sall, tall, dog)


def _rot(x, cs, sn):
    return x * cs + pltpu.roll(x, HEAD_DIM // 2, 1) * sn


def _rot_t(dy, cs, sn):
    return dy * cs + pltpu.roll(dy * sn, HEAD_DIM // 2, 1)


def ret_fwd(pm, qoff, koff, voff, goff, cs, sn, dmat, avec, bvec, gam, og_buf, H, *, name):
    T = pm.shape[0]
    C = LA_CHUNK
    N = T // C
    hd = HEAD_DIM

    HB = _heads_per_step(H)

    def body(q_ref, k_ref, v_ref, g_ref, cs_ref, sn_ref, dm_ref, a_ref, b_ref, gam_ref, _og_in,
             og_ref, or_ref, sall_ref, S):
        n = pl.program_id(0)
        hg = pl.program_id(1)
        c, s = cs_ref[...], sn_ref[...]

        @pl.when((n == 0) & (hg == 0))
        def _():
            S[...] = jnp.zeros_like(S)

        states = [S[hg * HB + i] for i in range(HB)]

        def head(i):
            sl = slice(i * hd, (i + 1) * hd)
            qq = _rot(q_ref[:, sl], c, s)
            kk = _rot(k_ref[:, sl], c, s) * QK_SCALE
            vv = v_ref[:, sl]
            s0 = states[i]
            p = _dot(qq, kk, NT) * dm_ref[i]
            cross = _dot(qq * a_ref[i], s0)
            s1 = s0 * gam_ref[i] + _dot(kk * b_ref[i], vv, TN)
            yield
            o = _dot(p, vv) + cross
            yield
            sall_ref[i] = s0
            or_ref[:, sl] = o
            og_ref[:, sl] = (_silu(g_ref[:, sl]) * o * _head_rstd(o)).astype(og_ref.dtype)
            return s1

        for i, s1 in enumerate(_interleave([head(i) for i in range(HB)])):
            S[hg * HB + i] = s1

    blk = lambda off: pl.BlockSpec((C, HB * hd), lambda n, h: (n, off // HB + h))
    tab = pl.BlockSpec((C, hd), lambda n, h: (n, 0))
    per_h = lambda r, cdim: pl.BlockSpec((HB, r, cdim), lambda n, h: (h, 0, 0))
    return _pcall(
        body, name=name, grid=(N, H // HB),
        in_specs=[blk(qoff), blk(koff), blk(voff), blk(goff), tab, tab,
                  per_h(C, C), per_h(C, hd), per_h(C, hd), per_h(1, hd), ANY],
        out_specs=[blk(H), blk(0), pl.BlockSpec((None, HB, hd, hd), lambda n, h: (n, h, 0, 0))],
        out_shape=[jax.ShapeDtypeStruct(og_buf.shape, og_buf.dtype), jax.ShapeDtypeStruct((T, H * hd), F32),
                   jax.ShapeDtypeStruct((N, H, hd, hd), F32)],
        input_output_aliases={10: 0},
        scratch_shapes=[pltpu.VMEM((H, hd, hd), F32)],
        compiler_params=_cp(2))(pm, pm, pm, pm, cs, sn, dmat, avec, bvec, gam, og_buf)


def ret_bwd(pm, qoff, koff, voff, goff, cs, sn, dmat, avec, bvec, gam, oraw, sall, dog, dogoff, H, *, name):
    T = pm.shape[0]
    C = LA_CHUNK
    N = T // C
    hd = HEAD_DIM

    HB = _heads_per_step(H)

    def body(q_ref, k_ref, v_ref, g_ref, cs_ref, sn_ref, dm_ref, a_ref, b_ref, gam_ref, or_ref, sall_ref,
             dog_ref, dq_ref, dk_ref, dv_ref, dg_ref, dS):
        n = pl.program_id(0)
        hg = pl.program_id(1)
        c, s = cs_ref[...], sn_ref[...]

        @pl.when((n == 0) & (hg == 0))
        def _():
            dS[...] = jnp.zeros_like(dS)

        dstates = [dS[hg * HB + i] for i in range(HB)]

        def head(i):
            sl = slice(i * hd, (i + 1) * hd)
            qq = _rot(q_ref[:, sl], c, s)
            kk = _rot(k_ref[:, sl], c, s) * QK_SCALE
            vv = v_ref[:, sl]
            gg = g_ref[:, sl]
            o = or_ref[:, sl]
            dog = dog_ref[:, sl]
            dm = dm_ref[i]
            av, bv = a_ref[i], b_ref[i]
            s0 = sall_ref[i]
            ds1 = dstates[i]

            rstd = _head_rstd(o)
            on = o * rstd
            don = dog * _silu(gg)
            dg_ref[:, sl] = (dog * on * _dsilu(gg)).astype(dg_ref.dtype)
            do = rstd * (don - on * jnp.mean(don * on, axis=-1, keepdims=True))

            p = _dot(qq, kk, NT) * dm
            dp = _dot(do, vv, NT) * dm
            cross_q = _dot(do, s0, NT) * av
            cross_k = _dot(vv, ds1, NT) * bv
            cross_v = _dot(kk * bv, ds1)
            ds0 = ds1 * gam_ref[i] + _dot(qq * av, do, TN)
            yield
            dv_ref[:, sl] = (_dot(p, do, TN) + cross_v).astype(dv_ref.dtype)
            dqq = _dot(dp, kk) + cross_q
            dkk = (_dot(dp, qq, TN) + cross_k) * QK_SCALE
            yield
            dq_ref[:, sl] = _rot_t(dqq, c, s).astype(dq_ref.dtype)
            dk_ref[:, sl] = _rot_t(dkk, c, s).astype(dk_ref.dtype)
            return ds0

        for i, ds0 in enumerate(_interleave([head(i) for i in range(HB)])):
            dS[hg * HB + i] = ds0

    blk = lambda off: pl.BlockSpec((C, HB * hd), lambda n, h: (N - 1 - n, off // HB + h))
    tab = pl.BlockSpec((C, hd), lambda n, h: (N - 1 - n, 0))
    per_h = lambda r, cdim: pl.BlockSpec((HB, r, cdim), lambda n, h: (h, 0, 0))
    return _pcall(
        body, name=name, grid=(N, H // HB),
        in_specs=[blk(qoff), blk(koff), blk(voff), blk(goff), tab, tab,
                  per_h(C, C), per_h(C, hd), per_h(C, hd), per_h(1, hd), blk(0),
                  pl.BlockSpec((None, HB, hd, hd), lambda n, h: (N - 1 - n, h, 0, 0)), blk(dogoff)],
        out_specs=[blk(0)] * 4,
        out_shape=[jax.ShapeDtypeStruct((T, H * hd), BF)] * 4,
        scratch_shapes=[pltpu.VMEM((H, hd, hd), F32)],
        compiler_params=_cp(2))(pm, pm, pm, pm, cs, sn, dmat, avec, bvec, gam, oraw, sall, dog)


LN_ROWS = 128


def ln_fwd(pre, lw, lb, *, name):
    T, W2 = pre.shape
    W = W2 // 2
    tr = _tile(T, LN_ROWS, SUBLANES)

    def body(p_ref, w_ref, b_ref, o_ref):
        v = _gelu(p_ref[...])
        xc = v - jnp.mean(v, axis=-1, keepdims=True)
        r = lax.rsqrt(jnp.mean(xc * xc, axis=-1, keepdims=True) + EPS)
        o_ref[...] = xc * r * w_ref[...] + b_ref[...]

    return _pcall(body, name=name, grid=(T // tr,),
                  in_specs=[pl.BlockSpec((tr, W), lambda i: (i, 1)), _fixed((1, W)), _fixed((1, W))],
                  out_specs=_rows(tr, W), out_shape=jax.ShapeDtypeStruct((T, W), F32),
                  compiler_params=_cp(1))(pre, lw, lb)


def ln_bwd(pre, lw, dvn, dpre_buf, *, name):
    T, W2 = pre.shape
    W = W2 // 2
    tr = _tile(T, LN_ROWS, SUBLANES)

    def body(p_ref, w_ref, d_ref, _dp_in, dp_ref, dw_ref, db_ref):
        i = pl.program_id(0)
        v, dgelu = _gelu_and_grad(p_ref[...])
        xc = v - jnp.mean(v, axis=-1, keepdims=True)
        r = lax.rsqrt(jnp.mean(xc * xc, axis=-1, keepdims=True) + EPS)
        xh = xc * r
        d = d_ref[...]
        dxh = d * w_ref[...]
        dv = r * (dxh - jnp.mean(dxh, axis=-1, keepdims=True) - xh * jnp.mean(dxh * xh, axis=-1, keepdims=True))
        dp_ref[...] = (dv * dgelu).astype(dp_ref.dtype)
        pw = jnp.sum(d * xh, axis=0, keepdims=True)
        pb = jnp.sum(d, axis=0, keepdims=True)

        @pl.when(i == 0)
        def _():
            dw_ref[...] = pw
            db_ref[...] = pb

        @pl.when(i > 0)
        def _():
            dw_ref[...] += pw
            db_ref[...] += pb

    return _pcall(body, name=name, grid=(T // tr,),
                  in_specs=[pl.BlockSpec((tr, W), lambda i: (i, 1)), _fixed((1, W)), _rows(tr, W), ANY],
                  out_specs=[pl.BlockSpec((tr, W), lambda i: (i, 1)), _fixed((1, W)), _fixed((1, W))],
                  out_shape=[jax.ShapeDtypeStruct(dpre_buf.shape, dpre_buf.dtype), jax.ShapeDtypeStruct((1, W), F32),
                             jax.ShapeDtypeStruct((1, W), F32)],
                  input_output_aliases={3: 0},
                  compiler_params=_cp(1))(pre, lw, dvn, dpre_buf)


def _tril_mask(c):
    t = lax.broadcasted_iota(jnp.int32, (c, c), 0)
    s = lax.broadcasted_iota(jnp.int32, (c, c), 1)
    return t >= s


def sg_fwd(pre, vn, ws, bs3, *, name):
    T, W = vn.shape
    G = ws.shape[0]
    gd = W // G
    C = SG_CHUNK

    def body(p_ref, v_ref, w_ref, b_ref, o_ref):
        mask = _tril_mask(C)
        for g in range(G):
            sl = slice(g * gd, (g + 1) * gd)
            wm = jnp.where(mask, w_ref[g], 0.0)
            s = _dot(wm, v_ref[:, sl]) + b_ref[g]
            o_ref[:, sl] = (_gelu(p_ref[:, sl]) * s).astype(o_ref.dtype)

    blk = pl.BlockSpec((C, W), lambda n: (n, 0))
    return _pcall(body, name=name, grid=(T // C,),
                  in_specs=[blk, blk, _fixed((G, C, C)), _fixed((G, C, 1))],
                  out_specs=blk, out_shape=jax.ShapeDtypeStruct((T, W), BF),
                  compiler_params=_cp(1))(pre, vn, ws, bs3)


def sg_bwd(pre, vn, ws, bs3, dus, *, name):
    T, W = vn.shape
    G = ws.shape[0]
    gd = W // G
    C = SG_CHUNK

    def body(p_ref, v_ref, w_ref, b_ref, d_ref, dp_ref, dv_ref, dw_ref, db_ref):
        n = pl.program_id(0)
        mask = _tril_mask(C)

        @pl.when(n == 0)
        def _():
            dw_ref[...] = jnp.zeros_like(dw_ref)
            db_ref[...] = jnp.zeros_like(db_ref)

        for g in range(G):
            sl = slice(g * gd, (g + 1) * gd)
            wm = jnp.where(mask, w_ref[g], 0.0)
            u, du = _gelu_and_grad(p_ref[:, sl])
            vv = v_ref[:, sl]
            d = d_ref[:, sl]
            s = _dot(wm, vv) + b_ref[g]
            ds = d * u
            dp_ref[:, sl] = (d * s * du).astype(dp_ref.dtype)
            dv_ref[:, sl] = _dot(wm, ds, TN)
            dw_ref[g] += jnp.where(mask, _dot(ds, vv, NT), 0.0)
            db_ref[g] += jnp.sum(ds, axis=1, keepdims=True)

    blk = pl.BlockSpec((C, W), lambda n: (n, 0))
    return _pcall(body, name=name, grid=(T // C,),
                  in_specs=[blk, blk, _fixed((G, C, C)), _fixed((G, C, 1)), blk],
                  out_specs=[blk, blk, _fixed((G, C, C)), _fixed((G, C, 1))],
                  out_shape=[jax.ShapeDtypeStruct((T, 2 * W), BF),
                             jax.ShapeDtypeStruct((T, W), F32),
                             jax.ShapeDtypeStruct((G, C, C), F32), jax.ShapeDtypeStruct((G, C, 1), F32)],
                  compiler_params=_cp(1))(pre, vn, ws, bs3, dus)


CHIP_RELATIONS = ((1, 0), (0, 1), (1, 1))


def _place():
    return lax.axis_index("x"), lax.axis_index("y"), lax.axis_index("c")


def _peer_chip(x, y, r):
    fx, fy = CHIP_RELATIONS[r]
    return (1 - x if fx else x), (1 - y if fy else y)


def gather_comm(arrs):
    n = len(arrs)
    per = 2 * len(CHIP_RELATIONS) + 1
    own = per - 1

    def ici(a, r, ins, outs, send, recv):
        x, y, c = _place()
        px, py = _peer_chip(x, y, r)
        return pltpu.make_async_remote_copy(
            src_ref=ins[a].at[c], dst_ref=outs[a].at[2 * x + y, c], send_sem=send.at[a * per + r],
            recv_sem=recv.at[a * per + r], device_id=(px, py, c), device_id_type=MESH)

    def own_block(a, ins, outs, send, recv):
        x, y, c = _place()
        return pltpu.make_async_remote_copy(
            src_ref=ins[a], dst_ref=outs[a].at[2 * x + y], send_sem=send.at[a * per + own],
            recv_sem=recv.at[a * per + own], device_id=(x, y, 1 - c), device_id_type=MESH)

    def start(ins, outs, send, recv):
        for a in range(n):
            for r in range(3):
                ici(a, r, ins, outs, send, recv).start()
            own_block(a, ins, outs, send, recv).start()

    def forward(a, r, outs, send, recv):
        x, y, c = _place()
        px, py = _peer_chip(x, y, r)
        landed = outs[a].at[2 * px + py, c]
        return pltpu.make_async_remote_copy(
            src_ref=landed, dst_ref=landed, send_sem=send.at[a * per + 3 + r],
            recv_sem=recv.at[a * per + 3 + r], device_id=(x, y, 1 - c), device_id_type=MESH)

    def middle(ins, outs, send, recv):
        x, y, c = _place()
        for a in range(n):
            for r in range(3):
                px, py = _peer_chip(x, y, r)
                landed = outs[a].at[2 * px + py, c]
                pltpu.make_async_remote_copy(
                    src_ref=landed, dst_ref=landed, send_sem=send.at[a * per + r],
                    recv_sem=recv.at[a * per + r], device_id=(px, py, c), device_id_type=MESH).wait_recv()
                forward(a, r, outs, send, recv).start()

    def finish(ins, outs, send, recv):
        x, y, c = _place()
        for a in range(n):
            for r in range(3):
                px, py = _peer_chip(x, y, r)
                other = outs[a].at[2 * px + py, 1 - c]
                pltpu.make_async_remote_copy(
                    src_ref=other, dst_ref=other, send_sem=send.at[a * per + 3 + r],
                    recv_sem=recv.at[a * per + 3 + r], device_id=(x, y, 1 - c), device_id_type=MESH).wait_recv()
        for a in range(n):
            for r in range(3):
                ici(a, r, ins, outs, send, recv).wait_send()
                forward(a, r, outs, send, recv).wait_send()
            own_block(a, ins, outs, send, recv).wait()

    outs = [jax.ShapeDtypeStruct((N_CHIPS,) + a.shape, a.dtype) for a in arrs]
    return Comm(list(arrs), outs, n * per, start, finish, middle)


def chip_exchange_comm(ps):
    n = len(ps)

    def copies(ins, outs, send, recv):
        x, y, c = _place()
        cps = []
        for a in range(n):
            for r in range(3):
                px, py = _peer_chip(x, y, r)
                cps.append(pltpu.make_async_remote_copy(
                    src_ref=ins[a].at[2 * px + py], dst_ref=outs[a].at[r], send_sem=send.at[3 * a + r],
                    recv_sem=recv.at[3 * a + r], device_id=(px, py, c), device_id_type=MESH))
        return cps

    def start(ins, outs, send, recv):
        for cp in copies(ins, outs, send, recv):
            cp.start()

    def finish(ins, outs, send, recv):
        for cp in copies(ins, outs, send, recv):
            cp.wait()

    outs = [jax.ShapeDtypeStruct((3,) + p.shape[1:], p.dtype) for p in ps]
    return Comm(list(ps), outs, 3 * n, start, finish)


def run_comm(comm, *, name):
    n_i, n_o = len(comm.ins), len(comm.outs)

    def body(*refs):
        ins, outs = refs[:n_i], refs[n_i:n_i + n_o]
        send, recv = refs[n_i + n_o:]
        comm.start(ins, outs, send, recv)
        if comm.middle is not None:
            comm.middle(ins, outs, send, recv)
        comm.finish(ins, outs, send, recv)

    res = _pcall(body, name=name, in_specs=[ANY] * n_i, out_specs=[ANY] * n_o, out_shape=comm.outs,
                 scratch_shapes=[pltpu.SemaphoreType.DMA((comm.nsem,)), pltpu.SemaphoreType.DMA((comm.nsem,))])(*comm.ins)
    return list(res)


def pair_exchange(gs, *, name):
    n = len(gs)

    def body(*refs):
        ins, outs = refs[:n], refs[n:2 * n]
        send, recv = refs[2 * n:2 * n + 2]
        x, y, c = _place()
        cps = []
        for a in range(n):
            cp = pltpu.make_async_remote_copy(
                src_ref=ins[a].at[:, pl.ds(1 - c, 1)], dst_ref=outs[a], send_sem=send.at[a], recv_sem=recv.at[a],
                device_id=(x, y, 1 - c), device_id_type=MESH)
            cp.start()
            cps.append(cp)
        for cp in cps:
            cp.wait()

    out_shape = [jax.ShapeDtypeStruct((g.shape[0], 1) + g.shape[2:], g.dtype) for g in gs]
    res = _pcall(body, name=name, in_specs=[ANY] * n, out_specs=[ANY] * n, out_shape=out_shape,
                 scratch_shapes=[pltpu.SemaphoreType.DMA((n,)), pltpu.SemaphoreType.DMA((n,))])(*gs)
    return list(res)


def pair_share_comm(fs):
    n = len(fs)

    def copies(ins, outs, send, recv):
        x, y, c = _place()
        return [pltpu.make_async_remote_copy(
            src_ref=ins[a], dst_ref=outs[a], send_sem=send.at[a], recv_sem=recv.at[a],
            device_id=(x, y, 1 - c), device_id_type=MESH) for a in range(n)]

    def start(ins, outs, send, recv):
        for cp in copies(ins, outs, send, recv):
            cp.start()

    def finish(ins, outs, send, recv):
        for cp in copies(ins, outs, send, recv):
            cp.wait()

    return Comm(list(fs), [jax.ShapeDtypeStruct(f.shape, f.dtype) for f in fs], n, start, finish)


def all_reduce_small(v, *, name):
    R = v.shape[0]

    def body(v_ref, sum_ref, gat_ref, send, recv):
        x, y, c = _place()
        me = 4 * x + 2 * y + c
        gat_ref[me] = v_ref[...]
        cps = []
        peers = []
        for r in range(1, N_DEV):
            fx, fy, fc = (r >> 2) & 1, (r >> 1) & 1, r & 1
            px, py, pc = (1 - x if fx else x), (1 - y if fy else y), (1 - c if fc else c)
            peers.append((px, py, pc))
            cp = pltpu.make_async_remote_copy(
                src_ref=v_ref, dst_ref=gat_ref.at[me], send_sem=send.at[r - 1], recv_sem=recv.at[r - 1],
                device_id=(px, py, pc), device_id_type=MESH)
            cp.start()
            cps.append(cp)
        for r in range(1, N_DEV):
            px, py, pc = peers[r - 1]
            slot = gat_ref.at[4 * px + 2 * py + pc]
            pltpu.make_async_remote_copy(
                src_ref=v_ref, dst_ref=slot, send_sem=send.at[r - 1], recv_sem=recv.at[r - 1],
                device_id=(px, py, pc), device_id_type=MESH).wait_recv()
        for cp in cps:
            cp.wait_send()
        acc = gat_ref[0]
        for s in range(1, N_DEV):
            acc = acc + gat_ref[s]
        sum_ref[...] = acc

    vm = pl.BlockSpec(memory_space=pltpu.VMEM)
    res = _pcall(body, name=name, in_specs=[vm], out_specs=[vm, vm],
                 out_shape=[jax.ShapeDtypeStruct((R, LANES), F32), jax.ShapeDtypeStruct((N_DEV, R, LANES), F32)],
                 scratch_shapes=[pltpu.SemaphoreType.DMA((N_DEV - 1,)), pltpu.SemaphoreType.DMA((N_DEV - 1,))],
                 compiler_params=pltpu.CompilerParams(vmem_limit_bytes=VMEM_LIMIT))(v)
    return res[0]


def pair_sum(g, r1, c_idx, *, name):
    nb, _, hr, C = g.shape
    tr = _tile(hr, max(BF16_ROWS, (1 << 18) // C), BF16_ROWS)

    def body(c_ref, g_ref, r_ref, pb_ref):
        pb_ref[...] = (g_ref[...] + r_ref[...].astype(F32)).astype(pb_ref.dtype)

    gs = pltpu.PrefetchScalarGridSpec(
        num_scalar_prefetch=1, grid=(nb, hr // tr),
        in_specs=[pl.BlockSpec((None, None, tr, C), lambda b, i, cr: (b, cr[0], i, 0)),
                  pl.BlockSpec((None, None, tr, C), lambda b, i, cr: (b, 0, i, 0))],
        out_specs=pl.BlockSpec((None, tr, C), lambda b, i, cr: (b, i, 0)))
    return _pcall(body, name=name, grid_spec=gs, out_shape=jax.ShapeDtypeStruct((nb, hr, C), BF),
                  compiler_params=_cp(2))(c_idx, g, r1)


def chip_sum(g, r1, r2, c_idx, j_idx, *, name, layer=0, n_layers=1, into=None):
    _, _, hr, C = g.shape
    tr = _tile(hr, max(BF16_ROWS, (1 << 18) // C), BF16_ROWS)

    def body(c_ref, j_ref, g_ref, s_ref, a_ref, b_ref, d_ref, *rest):
        o_ref = rest[-1]
        own = g_ref[...] + s_ref[...].astype(F32)
        o_ref[...] = ((own + a_ref[...].astype(F32)) + b_ref[...].astype(F32)) + d_ref[...].astype(F32)

    rel = lambda r: pl.BlockSpec((None, tr, C), lambda i, cr, jr: (r, i, 0))
    in_specs = [pl.BlockSpec((None, None, tr, C), lambda i, cr, jr: (jr[0], cr[0], i, 0)),
                pl.BlockSpec((None, None, tr, C), lambda i, cr, jr: (jr[0], 0, i, 0)), rel(0), rel(1), rel(2)]
    operands = [c_idx, j_idx, g, r1, r2, r2, r2]
    aliases = {}
    if into is not None:
        in_specs.append(ANY)
        operands.append(into)
        aliases = {len(operands) - 1: 0}
    gs = pltpu.PrefetchScalarGridSpec(
        num_scalar_prefetch=2, grid=(hr // tr,), in_specs=in_specs,
        out_specs=pl.BlockSpec((None, tr, C), lambda i, cr, jr: (layer, i, 0)))
    return _pcall(body, name=name, grid_spec=gs, out_shape=jax.ShapeDtypeStruct((n_layers, hr, C), F32),
                  input_output_aliases=aliases, compiler_params=_cp(1))(*operands)


def adamw_halves(w, g_mine, g_other, m, v, c_idx, *, name):
    L, R, C = w.shape
    hr = R // 2
    tr = _tile(hr, max(SUBLANES, (1 << 18) // C), SUBLANES)
    nbh = hr // tr
    c1 = 1.0 - ADAM_B1 ** ADAM_STEP
    c2 = 1.0 - ADAM_B2 ** ADAM_STEP

    def body(c_ref, w_ref, gm_ref, go_ref, m_ref, v_ref, g_ref, d_ref, mo_ref, vo_ref):
        i = pl.program_id(1)
        gv = jnp.where(i // nbh == c_ref[0], gm_ref[...], go_ref[...])
        m2 = ADAM_B1 * m_ref[...] + (1.0 - ADAM_B1) * gv
        v2 = ADAM_B2 * v_ref[...] + (1.0 - ADAM_B2) * (gv * gv)
        d_ref[...] = -ADAM_LR * ((m2 / c1) / (jnp.sqrt(v2 / c2) + ADAM_EPS) + ADAM_WD * w_ref[...])
        g_ref[...] = gv
        mo_ref[...] = m2
        vo_ref[...] = v2

    full = pl.BlockSpec((None, tr, C), lambda l, i, cr: (l, i, 0))
    half = pl.BlockSpec((None, tr, C), lambda l, i, cr: (l, i % nbh, 0))
    gs = pltpu.PrefetchScalarGridSpec(
        num_scalar_prefetch=1, grid=(L, R // tr),
        in_specs=[full, half, half, full, full], out_specs=[full] * 4)
    sds = jax.ShapeDtypeStruct((L, R, C), F32)
    return _pcall(body, name=name, grid_spec=gs, out_shape=[sds] * 4,
                  compiler_params=_cp(2))(c_idx, w, g_mine, g_other, m, v)


def _pack_rows(arrs):
    parts = []
    for a in arrs:
        flat = a.reshape(-1).astype(F32)
        tile = SUBLANES * LANES
        pad = (-flat.shape[0]) % tile
        parts.append(jnp.pad(flat, (0, pad)).reshape(-1, LANES))
    return jnp.concatenate(parts, axis=0)


def _unpack_rows(buf, shapes):
    out, row = [], 0
    for shp in shapes:
        size = int(np.prod(shp))
        rows = -(-size // (SUBLANES * LANES)) * SUBLANES
        out.append(buf[row:row + rows].reshape(-1)[:size].reshape(shp))
        row += rows
    return out


def _halves(a2d):
    r, c = a2d.shape
    return a2d.reshape(2, r // 2, c)


def _rotary_tables(T):
    half = HEAD_DIM // 2
    pos = jnp.arange(T, dtype=F32)
    inv_freq = 1.0 / (ROPE_BASE ** jnp.linspace(0.0, 1.0, half, dtype=F32))
    ang = pos[:, None] * inv_freq[None, :]
    cos, sin = jnp.cos(ang), jnp.sin(ang)
    return jnp.concatenate([cos, cos], axis=1), jnp.concatenate([-sin, sin], axis=1)


def _retention_tables(H):
    C = LA_CHUNK
    lg = jnp.log1p(-jnp.power(2.0, -5.0 - jnp.arange(H, dtype=F32)))
    pos = jnp.arange(C, dtype=F32)
    causal = jnp.tril(jnp.ones((C, C), dtype=bool))
    dmat = jnp.exp(jnp.where(causal, (pos[:, None] - pos[None, :]) * lg[:, None, None], -jnp.inf))
    bc = lambda t: jnp.broadcast_to(t[..., None], t.shape + (HEAD_DIM,))
    avec = bc(jnp.exp((pos + 1.0)[None, :] * lg[:, None]))
    bvec = bc(jnp.exp((C - 1.0 - pos)[None, :] * lg[:, None]))
    gam = bc(jnp.exp(C * lg)[:, None])
    return dmat, avec, bvec, gam


def _relu2(acc):
    return acc, jnp.square(jnp.maximum(acc, 0.0))


def _drelu2(acc, up):
    return (acc * (2.0 * jnp.maximum(up, 0.0)),)


ROW_SHARDED = ("la_out", "sg_out", "ffn_down0", "ffn_down1")


class ExchangePlan:
    GATHERS = {"la_in_main": ("la_out", "ffn_up0"), "gdn_fwd": ("ffn_down0",), "ffn_up_0": ("sg_in",),
               "ffn_down_0": ("sg_out", "ffn_up1"), "sg_in": ("ffn_down1",)}
    REDUCES = {"ffn_dup_1": "ffn_down1", "ffn_dy_1": "ffn_up1", "sg_dus": "sg_out", "sg_dy": "sg_in",
               "ffn_dup_0": "ffn_down0", "ffn_dy_0": "ffn_up0", "la_docat": "la_out", "la_dy": "la_in"}
    SHARE = "dnorm00"
    SHARED = ("la_in", "la_out", "sg_in", "sg_out", "ffn_up", "ffn_down")

    def __init__(self, shard_halves, c_idx, j_idx):
        self.shard_halves, self.c_idx, self.j_idx = shard_halves, c_idx, j_idx
        self.partial = {}
        self.finished = {}

    def comm(self, carrier):
        if carrier in self.GATHERS:
            return gather_comm([self.shard_halves[w] for w in self.GATHERS[carrier]])
        if carrier in self.REDUCES:
            return chip_exchange_comm([self.partial[self.REDUCES[carrier]][2]])
        if carrier == self.SHARE:
            return pair_share_comm([self.finished[k] for k in self.SHARED])
        return None

    def done(self, carrier, outs, W):
        if carrier in self.GATHERS:
            for w, g in zip(self.GATHERS[carrier], outs):
                install_gathered(W, w, g)
        elif carrier == self.SHARE:
            self.from_sibling = dict(zip(self.SHARED, outs))
        else:
            w = self.REDUCES[carrier]
            per_layer = w[:-1] in ("ffn_up", "ffn_down")
            key, layer, n_layers = (w[:-1], int(w[-1]), 2) if per_layer else (w, 0, 1)
            g, sib, _ = self.partial[w]
            self.finished[key] = chip_sum(g, sib, outs[0], self.c_idx, self.j_idx, name=f"grads_chip_sum_{w}",
                                          layer=layer, n_layers=n_layers, into=self.finished.get(key))

    def grad_ready(self, w, g, payload=None):
        halves = lambda t: t.reshape(N_CHIPS, 2, t.shape[1] // 2, t.shape[2])
        sib = pair_exchange([halves(g if payload is None else payload)], name=f"grads_pair_exchange_{w}")[0]
        self.partial[w] = (halves(g), sib, pair_sum(halves(g), sib, self.c_idx, name=f"grads_pair_sum_{w}"))


def install_gathered(W, w, g):
    whole = g.reshape(N_CHIPS, g.shape[1] * g.shape[2], g.shape[3])
    if w in ROW_SHARDED:
        whole = whole.reshape(-1, whole.shape[-1])
    if w[:-1] in ("ffn_up", "ffn_down"):
        W[w[:-1]][int(w[-1])] = whole
    else:
        W[w] = whole


def _by_chip(w, g):
    return g.reshape(N_CHIPS, -1, g.shape[-1]) if w in ROW_SHARDED else g


def _la_shard_rows(H):
    cs = (8 * H * HEAD_DIM + 2 * H) // N_CHIPS
    return cs, -(-cs // (2 * BF16_ROWS)) * (2 * BF16_ROWS)


def _la_pieces(H):
    HD = H * HEAD_DIM
    mix = 8 * HD + 2 * H
    cs = mix // N_CHIPS
    segments = [(0, 0, 4 * HD, 0), (1, 4 * HD, 4 * HD + 2 * H, 0), (0, 4 * HD + 2 * H, mix, 4 * HD)]
    pieces = []
    for j in range(N_CHIPS):
        mine = []
        for src, a, b, base in segments:
            lo, hi = max(cs * j, a), min(cs * (j + 1), b)
            if lo < hi:
                mine.append((src, base + lo - a, base + hi - a))
        pieces.append(mine)
    return pieces


def _la_weights_from_gathered(g, H):
    _, csp, D = g.shape
    tc = _tile(D, 2 * LANES)
    n_main = 8 * H * HEAD_DIM

    def body(g_ref, main_ref, gate_ref):
        parts = {0: [], 1: []}
        for j, mine in enumerate(_la_pieces(H)):
            row = 0
            for src, a, b in mine:
                parts[src].append(g_ref[j, row:row + b - a, :])
                row += b - a
        main_ref[...] = jnp.concatenate(parts[0], axis=0)
        gate = jnp.concatenate(parts[1], axis=0)
        gate_ref[...] = jnp.concatenate([gate, jnp.zeros((LANES - gate.shape[0], tc), gate.dtype)], axis=0)

    return _pcall(body, name="la_weights", grid=(D // tc,),
                  in_specs=[pl.BlockSpec((N_CHIPS, csp, tc), lambda i: (0, 0, i))],
                  out_specs=[pl.BlockSpec((n_main, tc), lambda i: (0, i)), pl.BlockSpec((LANES, tc), lambda i: (0, i))],
                  out_shape=[jax.ShapeDtypeStruct((n_main, D), g.dtype), jax.ShapeDtypeStruct((LANES, D), g.dtype)],
                  compiler_params=_cp(1))(g)


def _la_dproj_by_chip(main_parts, dpg, H):
    HD = H * HEAD_DIM
    cs, csp = _la_shard_rows(H)
    T = dpg.shape[0]
    tr = _tile(T, ROW_TILE, BF16_ROWS)
    n = len(main_parts)

    def body(*refs):
        parts, g_ref, o_ref = refs[:n], refs[n], refs[n + 1]
        pad = jnp.zeros((tr, csp - cs), o_ref.dtype)
        cols = []
        for mine in _la_pieces(H):
            for src, a, b in mine:
                while src == 0 and a < b:
                    i, off = divmod(a, HD)
                    end = min(b, (i + 1) * HD)
                    cols.append(parts[i][:, off:off + end - a])
                    a = end
                if src == 1:
                    cols.append(g_ref[:, a:b])
            cols.append(pad)
        o_ref[...] = jnp.concatenate(cols, axis=1)

    return _pcall(body, name="la_dproj", grid=(T // tr,),
                  in_specs=[_rows(tr, HD)] * n + [_rows(tr, LANES)], out_specs=_rows(tr, N_CHIPS * csp),
                  out_shape=jax.ShapeDtypeStruct((T, N_CHIPS * csp), BF), compiler_params=_cp(1))(*main_parts, dpg)


def _train_local(x2, tgt, W, plan=None):
    T, D = x2.shape
    H = W["a_log"].shape[0]
    nw = W["norm_w"]
    row = lambda v: v.reshape(1, -1).astype(F32)
    G = {}

    def mm(fn, *args, name, **kw):
        comm = plan.comm(name) if plan is not None else None
        if comm is None:
            return fn(*args, name=name, **kw)
        res, outs = fn(*args, name=name, comm=comm, **kw)
        plan.done(name, outs, W)
        return res

    def grad(w, g):
        payload = None
        if isinstance(g, (list, tuple)):
            g, payload = g
        G[w] = g
        if plan is not None:
            plan.grad_ready(w, _by_chip(w, g), None if payload is None else _by_chip(w, payload))

    def ffn_fwd(y, l):
        up, act = mm(mm_nn, y, W["ffn_up"][l], name=f"ffn_up_{l}", out_dtypes=(F32, BF), epilogue=_relu2)
        dn = mm(mm_nn, act, W["ffn_down"][l], name=f"ffn_down_{l}")
        return up, act, dn

    def ffn_bwd(y, up, act, ddn, l):
        grad(f"ffn_down{l}", mm_tn(act, ddn, name=f"ffn_dwdown_{l}", bf16_copy=True))
        dup = mm(mm_nt, ddn, W["ffn_down"][l], name=f"ffn_dup_{l}", out_dtypes=(BF,), epilogue=_drelu2, extras=(up,))
        grad(f"ffn_up{l}", mm_tn(y, dup, name=f"ffn_dwup_{l}", shards=N_CHIPS, bf16_copy=True))
        return mm(mm_nt, dup, W["ffn_up"][l], name=f"ffn_dy_{l}")

    y0 = rms_fwd(x2, row(nw[0, 0]), name="norm00")
    pm = mm(mm_nt, y0, W["la_in_main"], name="la_in_main")
    pg = mm_nt(y0, W["la_in_gate"], name="la_in_gate")
    wc8 = jnp.pad(jnp.transpose(W["conv_w"]), ((0, SUBLANES - CONV_WIDTH), (0, 0)))
    lanes_pad = (H, LANES - 2 * H)
    arow = jnp.pad(W["a_log"], lanes_pad).reshape(1, LANES)
    dtrow = jnp.pad(W["dt_bias"], lanes_pad).reshape(1, LANES)
    bg, gcrow = gates_fwd(pg, arow, dtrow, H, name="gates_fwd")
    q = prep_fwd(pm, 0, wc8, 0, H, True, name="prep_q")
    k = prep_fwd(pm, H, wc8, H, H, True, name="prep_k")
    v = prep_fwd(pm, 2 * H, wc8, 2 * H, H, False, name="prep_v")
    wn = row(W["out_norm_w"])
    gdn_comm = plan.comm("gdn_fwd") if plan is not None else None
    og_a, or_a, sall_a, tall, *carried = gdn_fwd(q, k, v, pm, 3 * H, bg, gcrow, wn, H, name="gdn_fwd", comm=gdn_comm)
    if gdn_comm is not None:
        plan.done("gdn_fwd", carried[0], W)
    cs, sn = _rotary_tables(T)
    dmat, avec, bvec, gam = _retention_tables(H)
    ocat, or_b, sall_b = ret_fwd(pm, 4 * H, 5 * H, 6 * H, 7 * H, cs, sn, dmat, avec, bvec, gam, og_a, H,
                                 name="ret_fwd")
    mix = mm_nn(ocat, W["la_out"], name="la_out")
    h1, y2 = res_norm(x2, mix, row(nw[0, 1]), row(nw[0, 2]), name="resnorm_0a")
    up, act, dn = ffn_fwd(y2, 0)
    h2, y0b = res_norm(h1, dn, row(nw[0, 3]), row(nw[1, 0]), name="resnorm_0b")

    pre = mm(mm_nn, y0b, W["sg_in"], name="sg_in")
    lw, lb = row(W["ln_w"]), row(W["ln_b"])
    vn = ln_fwd(pre, lw, lb, name="sg_ln")
    ws = W["w_s"]
    bs3 = W["b_s"][:, :, None]
    us = sg_fwd(pre, vn, ws, bs3, name="sg_gate")
    mix1 = mm_nn(us, W["sg_out"], name="sg_out")
    h3, y2b = res_norm(h2, mix1, row(nw[1, 1]), row(nw[1, 2]), name="resnorm_1a")
    up1, act1, dn1 = ffn_fwd(y2b, 1)
    dnw = [[None] * 4 for _ in range(2)]
    dh4, ddn1, dnw[1][3], lrow = last_norm_and_loss(h3, dn1, row(nw[1, 3]), tgt, name="last_norm_and_loss")
    loss = lrow[0, 0]

    dy2b = ffn_bwd(y2b, up1, act1, ddn1, 1)
    dh3, dnw[1][2], dmix1, dnw[1][1] = rms_bwd(h3, row(nw[1, 2]), dy2b, dh4, name="dnorm_1a",
                                               inner=(mix1, row(nw[1, 1])))
    grad("sg_out", mm_tn(us, dmix1, name="sg_dwout", bf16_copy=True))
    dus = mm(mm_nt, dmix1, W["sg_out"], name="sg_dus")
    dpre_u, dvn, G["w_s"], dbs3 = sg_bwd(pre, vn, ws, bs3, dus, name="sg_gate_bwd")
    G["b_s"] = dbs3[:, :, 0]
    dpre, dlw, dlb = ln_bwd(pre, lw, dvn, dpre_u, name="sg_ln_bwd")
    G["ln_w"], G["ln_b"] = dlw[0], dlb[0]
    grad("sg_in", mm_tn(y0b, dpre, name="sg_dwin", shards=N_CHIPS, bf16_copy=True))
    dy0b = mm(mm_nt, dpre, W["sg_in"], name="sg_dy")

    dh2, dnw[1][0], ddn, dnw[0][3] = rms_bwd(h2, row(nw[1, 0]), dy0b, dh3, name="dnorm_0b",
                                             inner=(dn, row(nw[0, 3])))
    dy2 = ffn_bwd(y2, up, act, ddn, 0)
    dh1, dnw[0][2], dmix, dnw[0][1] = rms_bwd(h1, row(nw[0, 2]), dy2, dh2, name="dnorm_0a",
                                              inner=(mix, row(nw[0, 1])))
    grad("la_out", mm_tn(ocat, dmix, name="la_dwout", bf16_copy=True))
    docat = mm(mm_nt, dmix, W["la_out"], name="la_docat")
    dq, dk, dv, dz, dbg, dwn = gdn_bwd(q, k, v, pm, 3 * H, bg, gcrow, wn, or_a, sall_a, tall, docat, H,
                                       name="gdn_bwd")
    drq, drk, drv, drg = ret_bwd(pm, 4 * H, 5 * H, 6 * H, 7 * H, cs, sn, dmat, avec, bvec, gam, or_b, sall_b,
                                 docat, H, H, name="ret_bwd")
    dpg, da, ddt = gates_bwd(pg, arow, dtrow, dbg, H, name="gates_bwd")
    dcq, dwq = prep_bwd_act(pm, 0, wc8, 0, H, True, dq, name="prep_dq")
    dck, dwk = prep_bwd_act(pm, H, wc8, H, H, True, dk, name="prep_dk")
    dcv, dwv = prep_bwd_act(pm, 2 * H, wc8, 2 * H, H, False, dv, name="prep_dv")
    dxq = prep_bwd_conv(dcq, wc8, 0, H, name="conv_dq")
    dxk = prep_bwd_conv(dck, wc8, H, H, name="conv_dk")
    dxv = prep_bwd_conv(dcv, wc8, 2 * H, H, name="conv_dv")
    dproj = _la_dproj_by_chip([dxq, dxk, dxv, dz, drq, drk, drv, drg], dpg, H)
    grad("la_in", mm_tn(dproj, y0, name="la_dwin").reshape(N_CHIPS, -1, D))
    dy0 = mm(mm_nn, dproj, W["la_in_rows"], name="la_dy")
    dx, dnw[0][0] = mm(rms_bwd, x2, row(nw[0, 0]), dy0, dh1, name="dnorm00")

    G["norm_w"] = jnp.stack([jnp.concatenate(r, axis=0) for r in dnw], axis=0)
    G["conv_w"] = jnp.transpose(jnp.concatenate([dwq, dwk, dwv], axis=1)[:CONV_WIDTH])
    G["a_log"] = da[0, H:2 * H]
    G["dt_bias"] = ddt[0, H:2 * H]
    G["out_norm_w"] = dwn[0]
    return loss, dx, G


def _as2d(a):
    n = int(np.prod(a.shape))
    if a.shape[-1] < LANES and n % LANES == 0:
        return a.reshape(-1, LANES)
    return a.reshape(-1, a.shape[-1])


def _adamw_any(w, g, m, v, name):
    shp = w.shape
    d, m2, v2 = adamw(_as2d(w), _as2d(g.reshape(shp)), _as2d(m), _as2d(v), name=name)
    return g.reshape(shp), d.reshape(shp), m2.reshape(shp), v2.reshape(shp)


def kernel(x, norm_w, la_w_in, la_conv_w, la_a_log, la_dt_bias, la_out_norm_w, la_w_out, sg_w_in, sg_ln_w, sg_ln_b, sg_w_s, sg_b_s, sg_w_out, ffn_w_up, ffn_w_down, loss_target, m_norm_w, m_la_w_in, m_la_conv_w, m_la_a_log, m_la_dt_bias, m_la_out_norm_w, m_la_w_out, m_sg_w_in, m_sg_ln_w, m_sg_ln_b, m_sg_w_s, m_sg_b_s, m_sg_w_out, m_ffn_w_up, m_ffn_w_down, v_norm_w, v_la_w_in, v_la_conv_w, v_la_a_log, v_la_dt_bias, v_la_out_norm_w, v_la_w_out, v_sg_w_in, v_sg_ln_w, v_sg_ln_b, v_sg_w_s, v_sg_b_s, v_sg_w_out, v_ffn_w_up, v_ffn_w_down):
    weights = dict(norm_w=norm_w, la_w_in=la_w_in, la_conv_w=la_conv_w, la_a_log=la_a_log, la_dt_bias=la_dt_bias,
                   la_out_norm_w=la_out_norm_w, la_w_out=la_w_out, sg_w_in=sg_w_in, sg_ln_w=sg_ln_w,
                   sg_ln_b=sg_ln_b, sg_w_s=sg_w_s, sg_b_s=sg_b_s, sg_w_out=sg_w_out, ffn_w_up=ffn_w_up,
                   ffn_w_down=ffn_w_down)
    mom_m = dict(norm_w=m_norm_w, la_w_in=m_la_w_in, la_conv_w=m_la_conv_w, la_a_log=m_la_a_log,
                 la_dt_bias=m_la_dt_bias, la_out_norm_w=m_la_out_norm_w, la_w_out=m_la_w_out, sg_w_in=m_sg_w_in,
                 sg_ln_w=m_sg_ln_w, sg_ln_b=m_sg_ln_b, sg_w_s=m_sg_w_s, sg_b_s=m_sg_b_s, sg_w_out=m_sg_w_out,
                 ffn_w_up=m_ffn_w_up, ffn_w_down=m_ffn_w_down)
    mom_v = dict(norm_w=v_norm_w, la_w_in=v_la_w_in, la_conv_w=v_la_conv_w, la_a_log=v_la_a_log,
                 la_dt_bias=v_la_dt_bias, la_out_norm_w=v_la_out_norm_w, la_w_out=v_la_w_out, sg_w_in=v_sg_w_in,
                 sg_ln_w=v_sg_ln_w, sg_ln_b=v_sg_ln_b, sg_w_s=v_sg_w_s, sg_b_s=v_sg_b_s, sg_w_out=v_sg_w_out,
                 ffn_w_up=v_ffn_w_up, ffn_w_down=v_ffn_w_down)
    order = list(weights)

    T, D = x.shape[1], x.shape[2]
    H = la_a_log.shape[1]
    HD = H * HEAD_DIM
    xi, yi, ci = _place()
    chip = 2 * xi + yi
    c_idx = jnp.reshape(ci, (1,)).astype(jnp.int32)
    j_idx = jnp.reshape(chip, (1,)).astype(jnp.int32)

    cs, csp = _la_shard_rows(H)
    la_rows = lambda a: jnp.pad(jnp.swapaxes(a, 1, 2), ((0, 0), (0, csp - cs), (0, 0)))
    shards = dict(la_in=la_rows(la_w_in)[0], la_out=la_w_out[0], sg_in=sg_w_in[0], sg_out=sg_w_out[0],
                  ffn_up0=ffn_w_up[0], ffn_up1=ffn_w_up[1], ffn_down0=ffn_w_down[0], ffn_down1=ffn_w_down[1])
    shard_halves = {w: _halves(a.astype(BF)) for w, a in shards.items()}
    small_shapes = [norm_w.shape, la_conv_w[0].shape, sg_ln_w[0].shape, sg_ln_b[0].shape]
    small = _pack_rows([norm_w, la_conv_w[0], sg_ln_w[0], sg_ln_b[0]])
    small = _halves(jnp.pad(small, ((0, (-small.shape[0]) % (2 * SUBLANES)), (0, 0))))
    la_in_g, small_g = [g.reshape(N_CHIPS, -1, g.shape[-1])
                        for g in run_comm(gather_comm([shard_halves["la_in"], small]), name="gather_first")]
    pieces = [_unpack_rows(small_g[kk], small_shapes) for kk in range(N_CHIPS)]
    la_main, la_gate = _la_weights_from_gathered(la_in_g, H)
    W = dict(
        norm_w=jnp.concatenate([p[0] for p in pieces], axis=-1),
        conv_w=jnp.concatenate([p[1] for p in pieces], axis=0),
        ln_w=jnp.concatenate([p[2] for p in pieces], axis=0),
        ln_b=jnp.concatenate([p[3] for p in pieces], axis=0),
        a_log=la_a_log[0], dt_bias=la_dt_bias[0], out_norm_w=la_out_norm_w[0], w_s=sg_w_s[0], b_s=sg_b_s[0],
        la_in_main=la_main, la_in_gate=la_gate, la_in_rows=la_in_g.reshape(-1, D),
        ffn_up=[None, None], ffn_down=[None, None],
    )
    plan = ExchangePlan(shard_halves, c_idx, j_idx)

    loss_local, dx, G = _train_local(x[0], loss_target[0], W, plan)
    loss = lax.psum(loss_local, ("x", "y", "c"))

    big_params = dict(la_w_in="la_in", la_w_out="la_out", sg_w_in="sg_in", sg_w_out="sg_out",
                      ffn_w_up="ffn_up", ffn_w_down="ffn_down")
    from_sib = {nm: plan.from_sibling[key] for nm, key in big_params.items()}

    def big_update(nm, key):
        rows = la_rows if nm == "la_w_in" else (lambda a: a)
        r4 = adamw_halves(rows(weights[nm]), plan.finished[key], from_sib[nm], rows(mom_m[nm]), rows(mom_v[nm]),
                          c_idx, name=f"adamw_{nm}")
        return [jnp.swapaxes(t[:, :cs], 1, 2) for t in r4] if nm == "la_w_in" else r4

    big_res = {nm: big_update(nm, key) for nm, key in big_params.items()}

    small_names = ["norm_w", "conv_w", "ln_w", "ln_b", "a_log", "dt_bias", "out_norm_w", "w_s", "b_s"]
    small_full = [G[nm] for nm in small_names]
    summed = _unpack_rows(all_reduce_small(_pack_rows(small_full), name="grads_all_reduce_small"),
                          [g.shape for g in small_full])
    sm = dict(zip(small_names, summed))
    own = lambda full, axis: lax.dynamic_slice_in_dim(full, chip * (full.shape[axis] // N_CHIPS),
                                                      full.shape[axis] // N_CHIPS, axis)
    grads = dict(
        norm_w=own(sm["norm_w"], 2), la_conv_w=own(sm["conv_w"], 0), la_a_log=sm["a_log"],
        la_dt_bias=sm["dt_bias"], la_out_norm_w=sm["out_norm_w"],
        sg_ln_w=own(sm["ln_w"], 0), sg_ln_b=own(sm["ln_b"], 0), sg_w_s=sm["w_s"], sg_b_s=sm["b_s"],
    )

    res = {nm: big_res[nm] if nm in big_res else
           _adamw_any(weights[nm], grads[nm], mom_m[nm], mom_v[nm], f"adamw_{nm}") for nm in order}
    return (loss, dx.reshape(x.shape), *[res[nm][0] for nm in order], *[res[nm][1] for nm in order],
            *[res[nm][2] for nm in order], *[res[nm][3] for nm in order])
```
